```python
import math
import jax, jax.numpy as jnp
from jax import lax
import numpy as np

D_MODEL = 1024
BATCH = 8
SEQ = 4096
DEPTH = 2

CHUNK = 64
EPS = 1e-6
D_MIX = D_MODEL
CONV_K = 4

SSD_HEADS = 8
SSD_HEAD_DIM = 64
SSD_DIM = SSD_HEADS * SSD_HEAD_DIM
SSD_GROUPS = 2
SSD_STATE = 128
SSD_CONV_DIM = SSD_DIM + 2 * SSD_GROUPS * SSD_STATE

ATT_HEADS = 4
ATT_KV_HEADS = 2
ATT_HEAD_DIM = 64
ATT_DIM = ATT_HEADS * ATT_HEAD_DIM
ATT_KV_DIM = ATT_KV_HEADS * ATT_HEAD_DIM
WINDOW = 128
WIN_CHUNKS = WINDOW // CHUNK

GDN_HEADS = 4
GDN_HEAD_K = 64
GDN_HEAD_V = 64
GDN_KDIM = GDN_HEADS * GDN_HEAD_K
GDN_DIM = GDN_HEADS * GDN_HEAD_V
GDN_CONV_DIM = 2 * GDN_KDIM + GDN_DIM

FF = ((8 * D_MODEL // 3 + 255) // 256) * 256

IN_SIZES = (ATT_DIM, ATT_KV_DIM, ATT_KV_DIM,
            SSD_DIM, SSD_CONV_DIM, SSD_HEADS,
            GDN_CONV_DIM, GDN_DIM, GDN_HEADS, GDN_HEADS)
IN_COLS = sum(IN_SIZES)

kernel_name = "hybrid_ssd_swa_gdn_sandwich_block"


def _offsets(sizes):
    out, acc = [], 0
    for s in sizes[:-1]:
        acc += s
        out.append(acc)
    return out


def rmsnorm(x, w):
    xf = x.astype(jnp.float32)
    y = xf * lax.rsqrt(jnp.mean(xf * xf, axis=-1, keepdims=True) + EPS)
    return (y * w.astype(jnp.float32)).astype(x.dtype)


def l2norm(x):
    return x * lax.rsqrt(jnp.sum(x * x, axis=-1, keepdims=True) + EPS)


def causal_dwconv(x, w, b=None):
    k = w.shape[0]
    t = x.shape[1]
    xp = jnp.pad(x, ((0, 0), (k - 1, 0), (0, 0)))
    y = xp[:, 0:t] * w[0]
    for i in range(1, k):
        y = y + xp[:, i:i + t] * w[i]
    if b is not None:
        y = y + b
    return y


def alibi_slopes(n):
    return 2.0 ** (-8.0 * jnp.arange(1, n + 1, dtype=jnp.float32) / n)


def lower_exp_diff(cs):
    n = cs.shape[-1]
    tril = jnp.tril(jnp.ones((n, n), dtype=bool))
    diff = cs[..., :, None] - cs[..., None, :]
    return jnp.where(tril, jnp.exp(jnp.where(tril, diff, 0.0)), 0.0)


def swa_sink_alibi(q, k, v, sinks):
    bsz, t = q.shape[0], q.shape[1]
    nc = t // CHUNK
    grp = ATT_HEADS // ATT_KV_HEADS
    band = (WIN_CHUNKS + 1) * CHUNK
    qc = q.reshape(bsz, nc, CHUNK, ATT_KV_HEADS, grp, ATT_HEAD_DIM)
    padw = ((0, 0), (WIN_CHUNKS * CHUNK, 0), (0, 0))
    kp = jnp.pad(k, padw).reshape(bsz, nc + WIN_CHUNKS, CHUNK, ATT_KV_HEADS, ATT_HEAD_DIM)
    vp = jnp.pad(v, padw).reshape(bsz, nc + WIN_CHUNKS, CHUNK, ATT_KV_HEADS, ATT_HEAD_DIM)
    kb = jnp.concatenate([kp[:, j:j + nc] for j in range(WIN_CHUNKS + 1)], axis=2)
    vb = jnp.concatenate([vp[:, j:j + nc] for j in range(WIN_CHUNKS + 1)], axis=2)
    s = jnp.einsum('bcikgd,bcjkd->bckgij', qc, kb).astype(jnp.float32) * (ATT_HEAD_DIM ** -0.5)
    qi = jnp.arange(CHUNK)[:, None]
    kj = jnp.arange(band)[None, :]
    dist = jnp.abs(qi + WIN_CHUNKS * CHUNK - kj).astype(jnp.float32)
    slopes = alibi_slopes(ATT_HEADS).reshape(ATT_KV_HEADS, grp)
    s = s - slopes[:, :, None, None] * dist
    key_chunk = jnp.arange(nc)[:, None] - WIN_CHUNKS + (jnp.arange(band) // CHUNK)[None, :]
    valid = key_chunk >= 0
    s = jnp.where(valid[None, :, None, None, None, :], s, -jnp.inf)
    sink = jnp.broadcast_to(sinks.astype(jnp.float32).reshape(1, 1, ATT_KV_HEADS, grp, 1, 1),
                            s.shape[:-1] + (1,))
    p = jax.nn.softmax(jnp.concatenate([s, sink], axis=-1), axis=-1)[..., :band]
    o = jnp.einsum('bckgij,bcjkd->bcikgd', p.astype(v.dtype), vb)
    return o.reshape(bsz, t, ATT_DIM)


def ssd_mixer(z, xbc, dt, conv_w, conv_b, dt_bias, a_log, d_skip, norm_w):
    bsz, t = xbc.shape[0], xbc.shape[1]
    nc = t // CHUNK
    hpg = SSD_HEADS // SSD_GROUPS
    xbc = jax.nn.silu(causal_dwconv(xbc, conv_w, conv_b)).astype(jnp.float32)
    xs, bm, cm = jnp.split(xbc, [SSD_DIM, SSD_DIM + SSD_GROUPS * SSD_STATE], axis=-1)
    xs = xs.reshape(bsz, t, SSD_HEADS, SSD_HEAD_DIM)
    bm = jnp.repeat(bm.reshape(bsz, t, SSD_GROUPS, SSD_STATE), hpg, axis=2)
    cm = jnp.repeat(cm.reshape(bsz, t, SSD_GROUPS, SSD_STATE), hpg, axis=2)
    dt = jax.nn.softplus(dt.astype(jnp.float32) + dt_bias.astype(jnp.float32))
    a = -jnp.exp(a_log.astype(jnp.float32))
    xc = (xs * dt[..., None]).reshape(bsz, nc, CHUNK, SSD_HEADS, SSD_HEAD_DIM)
    bc = bm.reshape(bsz, nc, CHUNK, SSD_HEADS, SSD_STATE)
    cc = cm.reshape(bsz, nc, CHUNK, SSD_HEADS, SSD_STATE)
    da = (dt * a).reshape(bsz, nc, CHUNK, SSD_HEADS).transpose(0, 3, 1, 2)
    a_cs = jnp.cumsum(da, axis=-1)
    lmat = lower_exp_diff(a_cs)
    scores = jnp.einsum('bclhn,bcshn->bhcls', cc, bc) * lmat
    y_diag = jnp.einsum('bhcls,bcshp->bclhp', scores, xc)
    decay_states = jnp.exp(a_cs[..., -1:] - a_cs)
    chunk_states = jnp.einsum('bclhn,bhcl,bclhp->bchpn', bc, decay_states, xc)
    chunk_decay = jnp.exp(a_cs[..., -1])

    def step(state, inp):
        new, dec = inp
        return state * dec[..., None, None] + new, state

    init = jnp.zeros((bsz, SSD_HEADS, SSD_HEAD_DIM, SSD_STATE), jnp.float32)
    _, prev = lax.scan(step, init, (chunk_states.transpose(1, 0, 2, 3, 4),
                                    chunk_decay.transpose(2, 0, 1)))
    prev = prev.transpose(1, 0, 2, 3, 4)
    y_off = jnp.einsum('bclhn,bchpn,bhcl->bclhp', cc, prev, jnp.exp(a_cs))
    y = (y_diag + y_off).reshape(bsz, t, SSD_HEADS, SSD_HEAD_DIM) + xs * d_skip.astype(jnp.float32)[:, None]
    g = y.reshape(bsz, t, SSD_DIM) * jax.nn.silu(z.astype(jnp.float32))
    g = g.reshape(bsz, t, SSD_GROUPS, SSD_DIM // SSD_GROUPS)
    g = g * lax.rsqrt(jnp.mean(g * g, axis=-1, keepdims=True) + EPS)
    return (g.reshape(bsz, t, SSD_DIM) * norm_w.astype(jnp.float32)).astype(z.dtype)


def gdn_mixer(qkv, z, b, a, conv_w, dt_bias, a_log, norm_w):
    bsz, t = qkv.shape[0], qkv.shape[1]
    nc = t // CHUNK
    qkv = jax.nn.silu(causal_dwconv(qkv, conv_w)).astype(jnp.float32)
    q, k, v = jnp.split(qkv, [GDN_KDIM, 2 * GDN_KDIM], axis=-1)
    q = l2norm(q.reshape(bsz, t, GDN_HEADS, GDN_HEAD_K)) * (GDN_HEAD_K ** -0.5)
    k = l2norm(k.reshape(bsz, t, GDN_HEADS, GDN_HEAD_K))
    v = v.reshape(bsz, t, GDN_HEADS, GDN_HEAD_V)
    beta = jax.nn.sigmoid(b.astype(jnp.float32))
    g = -jnp.exp(a_log.astype(jnp.float32)) * jax.nn.softplus(a.astype(jnp.float32) + dt_bias.astype(jnp.float32))

    def chunks(u):
        return u.reshape(bsz, nc, CHUNK, GDN_HEADS, u.shape[-1]).transpose(0, 3, 1, 2, 4)

    qc, kc, vc = chunks(q), chunks(k), chunks(v)
    betac = beta.reshape(bsz, nc, CHUNK, GDN_HEADS).transpose(0, 3, 1, 2)
    gc = jnp.cumsum(g.reshape(bsz, nc, CHUNK, GDN_HEADS).transpose(0, 3, 1, 2), axis=-1)
    decay = lower_exp_diff(gc)
    kbeta = kc * betac[..., None]
    strict = jnp.tril(jnp.einsum('bhcid,bhcjd->bhcij', kbeta, kc) * decay, -1)
    rhs = jnp.concatenate([vc * betac[..., None], kbeta * jnp.exp(gc)[..., None]], axis=-1)
    sol = lax.linalg.triangular_solve(strict, rhs, left_side=True, lower=True, unit_diagonal=True)
    u, w = jnp.split(sol, [GDN_HEAD_V], axis=-1)
    qk = jnp.einsum('bhcid,bhcjd->bhcij', qc, kc) * decay

    def step(state, inp):
        q_i, k_i, u_i, w_i, qk_i, g_i = inp
        v_new = u_i - jnp.einsum('bhld,bhde->bhle', w_i, state)
        o = (jnp.einsum('bhld,bhde->bhle', q_i * jnp.exp(g_i)[..., None], state)
             + jnp.einsum('bhls,bhse->bhle', qk_i, v_new))
        g_last = g_i[..., -1]
        state = (state * jnp.exp(g_last)[..., None, None]
                 + jnp.einsum('bhld,bhle->bhde', k_i * jnp.exp(g_last[..., None] - g_i)[..., None], v_new))
        return state, o

    xs = tuple(jnp.moveaxis(u_, 2, 0) for u_ in (qc, kc, u, w, qk, gc))
    init = jnp.zeros((bsz, GDN_HEADS, GDN_HEAD_K, GDN_HEAD_V), jnp.float32)
    _, o = lax.scan(step, init, xs)
    o = o.transpose(1, 0, 3, 2, 4).reshape(bsz, t, GDN_HEADS, GDN_HEAD_V)
    o = o * lax.rsqrt(jnp.mean(o * o, axis=-1, keepdims=True) + EPS) * norm_w.astype(jnp.float32)
    o = o * jax.nn.silu(z.astype(jnp.float32).reshape(bsz, t, GDN_HEADS, GDN_HEAD_V))
    return o.reshape(bsz, t, GDN_DIM).astype(z.dtype)


def _fwd_setup_inputs(seed: int = 0) -> dict:
    key = jax.random.key(seed)
    ks = jax.random.split(key, 24)
    L = DEPTH

    def nrm(k, shape, scale):
        return jax.random.normal(k, shape, jnp.float32) * scale

    def gain(k, shape):
        return 1.0 + 0.05 * jax.random.normal(k, shape, jnp.float32)

    def dt_bias_init(k, shape):
        u = jax.random.uniform(k, shape, jnp.float32, math.log(1e-3), math.log(1e-1))
        dtv = jnp.exp(u)
        return dtv + jnp.log(-jnp.expm1(-dtv))

    def a_log_init(k, shape):
        return jnp.log(jax.random.uniform(k, shape, jnp.float32, 1.0, 16.0))

    return {
        "x": nrm(ks[0], (BATCH, SEQ, D_MODEL), 1.0),
        "pre_mix_norm": gain(ks[1], (L, D_MODEL)),
        "post_mix_norm": gain(ks[2], (L, D_MODEL)),
        "pre_ffn_norm": gain(ks[3], (L, D_MODEL)),
        "post_ffn_norm": gain(ks[4], (L, D_MODEL)),
        "w_in": nrm(ks[5], (L, D_MODEL, IN_COLS), D_MODEL ** -0.5),
        "w_out": nrm(ks[6], (L, D_MIX, D_MODEL), D_MIX ** -0.5),
        "attn_sinks": nrm(ks[7], (L, ATT_HEADS), 0.5),
        "ssd_conv_w": nrm(ks[8], (L, CONV_K, SSD_CONV_DIM), CONV_K ** -0.5),
        "ssd_conv_b": nrm(ks[9], (L, SSD_CONV_DIM), 0.01),
        "ssd_dt_bias": dt_bias_init(ks[10], (L, SSD_HEADS)),
        "ssd_A_log": a_log_init(ks[11], (L, SSD_HEADS)),
        "ssd_D": 1.0 + 0.1 * jax.random.normal(ks[12], (L, SSD_HEADS), jnp.float32),
        "ssd_norm_w": gain(ks[13], (L, SSD_DIM)),
        "gdn_conv_w": nrm(ks[14], (L, CONV_K, GDN_CONV_DIM), CONV_K ** -0.5),
        "gdn_dt_bias": dt_bias_init(ks[15], (L, GDN_HEADS)),
        "gdn_A_log": a_log_init(ks[16], (L, GDN_HEADS)),
        "gdn_norm_w": gain(ks[17], (L, GDN_HEAD_V)),
        "ffn_w_gate": nrm(ks[18], (L, D_MODEL, FF), D_MODEL ** -0.5),
        "ffn_w_up": nrm(ks[19], (L, D_MODEL, FF), D_MODEL ** -0.5),
        "ffn_w_down": nrm(ks[20], (L, FF, D_MODEL), FF ** -0.5),
    }


def _fwd_reference(x, pre_mix_norm, post_mix_norm, pre_ffn_norm, post_ffn_norm, w_in, w_out,
              attn_sinks, ssd_conv_w, ssd_conv_b, ssd_dt_bias, ssd_A_log, ssd_D, ssd_norm_w,
              gdn_conv_w, gdn_dt_bias, gdn_A_log, gdn_norm_w, ffn_w_gate, ffn_w_up, ffn_w_down):
    offs = _offsets(IN_SIZES)
    for l in range(DEPTH):
        h = rmsnorm(x, pre_mix_norm[l])
        proj = h @ w_in[l]
        (a_q, a_k, a_v, s_z, s_xbc, s_dt, g_qkv, g_z, g_b, g_a) = jnp.split(proj, offs, axis=-1)
        att = swa_sink_alibi(a_q, a_k, a_v, attn_sinks[l])
        ssd = ssd_mixer(s_z, s_xbc, s_dt, ssd_conv_w[l], ssd_conv_b[l], ssd_dt_bias[l],
                        ssd_A_log[l], ssd_D[l], ssd_norm_w[l])
        gdn = gdn_mixer(g_qkv, g_z, g_b, g_a, gdn_conv_w[l], gdn_dt_bias[l], gdn_A_log[l], gdn_norm_w[l])
        mix = jnp.concatenate([att, ssd, gdn], axis=-1) @ w_out[l]
        x = x + rmsnorm(mix, post_mix_norm[l])
        h = rmsnorm(x, pre_ffn_norm[l])
        f = (jax.nn.silu(h @ ffn_w_gate[l]) * (h @ ffn_w_up[l])) @ ffn_w_down[l]
        x = x + rmsnorm(f, post_ffn_norm[l])
    return x


import jax as _jax
import jax.numpy as _jnp

TWIN_FORMAT = 'train_step'
FWD_PARAMS = ['x', 'pre_mix_norm', 'post_mix_norm', 'pre_ffn_norm', 'post_ffn_norm', 'w_in', 'w_out', 'attn_sinks', 'ssd_conv_w', 'ssd_conv_b', 'ssd_dt_bias', 'ssd_A_log', 'ssd_D', 'ssd_norm_w', 'gdn_conv_w', 'gdn_dt_bias', 'gdn_A_log', 'gdn_norm_w', 'ffn_w_gate', 'ffn_w_up', 'ffn_w_down']
TWIN_WEIGHTS = ['pre_mix_norm', 'post_mix_norm', 'pre_ffn_norm', 'post_ffn_norm', 'w_in', 'w_out', 'attn_sinks', 'ssd_conv_w', 'ssd_conv_b', 'ssd_dt_bias', 'ssd_A_log', 'ssd_D', 'ssd_norm_w', 'gdn_conv_w', 'gdn_dt_bias', 'gdn_A_log', 'gdn_norm_w', 'ffn_w_gate', 'ffn_w_up', 'ffn_w_down']
TWIN_DIFF_INPUT = 'x'
TWIN_INPUTS = ['x', 'pre_mix_norm', 'post_mix_norm', 'pre_ffn_norm', 'post_ffn_norm', 'w_in', 'w_out', 'attn_sinks', 'ssd_conv_w', 'ssd_conv_b', 'ssd_dt_bias', 'ssd_A_log', 'ssd_D', 'ssd_norm_w', 'gdn_conv_w', 'gdn_dt_bias', 'gdn_A_log', 'gdn_norm_w', 'ffn_w_gate', 'ffn_w_up', 'ffn_w_down', 'loss_target', 'm_pre_mix_norm', 'm_post_mix_norm', 'm_pre_ffn_norm', 'm_post_ffn_norm', 'm_w_in', 'm_w_out', 'm_attn_sinks', 'm_ssd_conv_w', 'm_ssd_conv_b', 'm_ssd_dt_bias', 'm_ssd_A_log', 'm_ssd_D', 'm_ssd_norm_w', 'm_gdn_conv_w', 'm_gdn_dt_bias', 'm_gdn_A_log', 'm_gdn_norm_w', 'm_ffn_w_gate', 'm_ffn_w_up', 'm_ffn_w_down', 'v_pre_mix_norm', 'v_post_mix_norm', 'v_pre_ffn_norm', 'v_post_ffn_norm', 'v_w_in', 'v_w_out', 'v_attn_sinks', 'v_ssd_conv_w', 'v_ssd_conv_b', 'v_ssd_dt_bias', 'v_ssd_A_log', 'v_ssd_D', 'v_ssd_norm_w', 'v_gdn_conv_w', 'v_gdn_dt_bias', 'v_gdn_A_log', 'v_gdn_norm_w', 'v_ffn_w_gate', 'v_ffn_w_up', 'v_ffn_w_down']
TWIN_OUTPUTS = ['loss', 'grad_x', 'grad_pre_mix_norm', 'grad_post_mix_norm', 'grad_pre_ffn_norm', 'grad_post_ffn_norm', 'grad_w_in', 'grad_w_out', 'grad_attn_sinks', 'grad_ssd_conv_w', 'grad_ssd_conv_b', 'grad_ssd_dt_bias', 'grad_ssd_A_log', 'grad_ssd_D', 'grad_ssd_norm_w', 'grad_gdn_conv_w', 'grad_gdn_dt_bias', 'grad_gdn_A_log', 'grad_gdn_norm_w', 'grad_ffn_w_gate', 'grad_ffn_w_up', 'grad_ffn_w_down', 'delta_pre_mix_norm', 'delta_post_mix_norm', 'delta_pre_ffn_norm', 'delta_post_ffn_norm', 'delta_w_in', 'delta_w_out', 'delta_attn_sinks', 'delta_ssd_conv_w', 'delta_ssd_conv_b', 'delta_ssd_dt_bias', 'delta_ssd_A_log', 'delta_ssd_D', 'delta_ssd_norm_w', 'delta_gdn_conv_w', 'delta_gdn_dt_bias', 'delta_gdn_A_log', 'delta_gdn_norm_w', 'delta_ffn_w_gate', 'delta_ffn_w_up', 'delta_ffn_w_down', 'new_m_pre_mix_norm', 'new_m_post_mix_norm', 'new_m_pre_ffn_norm', 'new_m_post_ffn_norm', 'new_m_w_in', 'new_m_w_out', 'new_m_attn_sinks', 'new_m_ssd_conv_w', 'new_m_ssd_conv_b', 'new_m_ssd_dt_bias', 'new_m_ssd_A_log', 'new_m_ssd_D', 'new_m_ssd_norm_w', 'new_m_gdn_conv_w', 'new_m_gdn_dt_bias', 'new_m_gdn_A_log', 'new_m_gdn_norm_w', 'new_m_ffn_w_gate', 'new_m_ffn_w_up', 'new_m_ffn_w_down', 'new_v_pre_mix_norm', 'new_v_post_mix_norm', 'new_v_pre_ffn_norm', 'new_v_post_ffn_norm', 'new_v_w_in', 'new_v_w_out', 'new_v_attn_sinks', 'new_v_ssd_conv_w', 'new_v_ssd_conv_b', 'new_v_ssd_dt_bias', 'new_v_ssd_A_log', 'new_v_ssd_D', 'new_v_ssd_norm_w', 'new_v_gdn_conv_w', 'new_v_gdn_dt_bias', 'new_v_gdn_A_log', 'new_v_gdn_norm_w', 'new_v_ffn_w_gate', 'new_v_ffn_w_up', 'new_v_ffn_w_down']
TWIN_LEAF_KINDS = {'loss': 'loss', 'grad_x': 'grad_x', 'grad_pre_mix_norm': 'grad_w', 'grad_post_mix_norm': 'grad_w', 'grad_pre_ffn_norm': 'grad_w', 'grad_post_ffn_norm': 'grad_w', 'grad_w_in': 'grad_w', 'grad_w_out': 'grad_w', 'grad_attn_sinks': 'grad_w', 'grad_ssd_conv_w': 'grad_w', 'grad_ssd_conv_b': 'grad_w', 'grad_ssd_dt_bias': 'grad_w', 'grad_ssd_A_log': 'grad_w', 'grad_ssd_D': 'grad_w', 'grad_ssd_norm_w': 'grad_w', 'grad_gdn_conv_w': 'grad_w', 'grad_gdn_dt_bias': 'grad_w', 'grad_gdn_A_log': 'grad_w', 'grad_gdn_norm_w': 'grad_w', 'grad_ffn_w_gate': 'grad_w', 'grad_ffn_w_up': 'grad_w', 'grad_ffn_w_down': 'grad_w', 'delta_pre_mix_norm': 'delta_w', 'delta_post_mix_norm': 'delta_w', 'delta_pre_ffn_norm': 'delta_w', 'delta_post_ffn_norm': 'delta_w', 'delta_w_in': 'delta_w', 'delta_w_out': 'delta_w', 'delta_attn_sinks': 'delta_w', 'delta_ssd_conv_w': 'delta_w', 'delta_ssd_conv_b': 'delta_w', 'delta_ssd_dt_bias': 'delta_w', 'delta_ssd_A_log': 'delta_w', 'delta_ssd_D': 'delta_w', 'delta_ssd_norm_w': 'delta_w', 'delta_gdn_conv_w': 'delta_w', 'delta_gdn_dt_bias': 'delta_w', 'delta_gdn_A_log': 'delta_w', 'delta_gdn_norm_w': 'delta_w', 'delta_ffn_w_gate': 'delta_w', 'delta_ffn_w_up': 'delta_w', 'delta_ffn_w_down': 'delta_w', 'new_m_pre_mix_norm': 'new_m', 'new_m_post_mix_norm': 'new_m', 'new_m_pre_ffn_norm': 'new_m', 'new_m_post_ffn_norm': 'new_m', 'new_m_w_in': 'new_m', 'new_m_w_out': 'new_m', 'new_m_attn_sinks': 'new_m', 'new_m_ssd_conv_w': 'new_m', 'new_m_ssd_conv_b': 'new_m', 'new_m_ssd_dt_bias': 'new_m', 'new_m_ssd_A_log': 'new_m', 'new_m_ssd_D': 'new_m', 'new_m_ssd_norm_w': 'new_m', 'new_m_gdn_conv_w': 'new_m', 'new_m_gdn_dt_bias': 'new_m', 'new_m_gdn_A_log': 'new_m', 'new_m_gdn_norm_w': 'new_m', 'new_m_ffn_w_gate': 'new_m', 'new_m_ffn_w_up': 'new_m', 'new_m_ffn_w_down': 'new_m', 'new_v_pre_mix_norm': 'new_v', 'new_v_post_mix_norm': 'new_v', 'new_v_pre_ffn_norm': 'new_v', 'new_v_post_ffn_norm': 'new_v', 'new_v_w_in': 'new_v', 'new_v_w_out': 'new_v', 'new_v_attn_sinks': 'new_v', 'new_v_ssd_conv_w': 'new_v', 'new_v_ssd_conv_b': 'new_v', 'new_v_ssd_dt_bias': 'new_v', 'new_v_ssd_A_log': 'new_v', 'new_v_ssd_D': 'new_v', 'new_v_ssd_norm_w': 'new_v', 'new_v_gdn_conv_w': 'new_v', 'new_v_gdn_dt_bias': 'new_v', 'new_v_gdn_A_log': 'new_v', 'new_v_gdn_norm_w': 'new_v', 'new_v_ffn_w_gate': 'new_v', 'new_v_ffn_w_up': 'new_v', 'new_v_ffn_w_down': 'new_v'}


def _forward(args):
    return _fwd_reference(*[args[k] for k in FWD_PARAMS])


def _output_shape():
    def fwd():
        inp = _fwd_setup_inputs(0)
        return _fwd_reference(*[inp[k] for k in FWD_PARAMS])
    out = _jax.eval_shape(fwd)
    return out.shape, out.dtype

N_MICROBATCH = 1
ADAM_LR = 0.001
ADAM_B1 = 0.9
ADAM_B2 = 0.999
ADAM_EPS = 1e-08
ADAM_WD = 0.01
ADAM_STEP = 10
PER_EXAMPLE_BATCH_AXIS = {'x': 0, 'loss_target': 0}
SHARED_INPUTS = []
_WEIGHT_DTYPES = {'pre_mix_norm': _jnp.float32, 'post_mix_norm': _jnp.float32, 'pre_ffn_norm': _jnp.float32, 'post_ffn_norm': _jnp.float32, 'w_in': _jnp.float32, 'w_out': _jnp.float32, 'attn_sinks': _jnp.float32, 'ssd_conv_w': _jnp.float32, 'ssd_conv_b': _jnp.float32, 'ssd_dt_bias': _jnp.float32, 'ssd_A_log': _jnp.float32, 'ssd_D': _jnp.float32, 'ssd_norm_w': _jnp.float32, 'gdn_conv_w': _jnp.float32, 'gdn_dt_bias': _jnp.float32, 'gdn_A_log': _jnp.float32, 'gdn_norm_w': _jnp.float32, 'ffn_w_gate': _jnp.float32, 'ffn_w_up': _jnp.float32, 'ffn_w_down': _jnp.float32}
MOMENT_SCALE = {'pre_mix_norm': 1.435011e+00, 'post_mix_norm': 3.223777e+01, 'pre_ffn_norm': 1.179651e+00, 'post_ffn_norm': 3.207093e+01, 'w_in': 7.966597e-01, 'w_out': 1.754702e+00, 'attn_sinks': 2.172326e-01, 'ssd_conv_w': 1.201292e+00, 'ssd_conv_b': 3.301217e+00, 'ssd_dt_bias': 3.601668e+00, 'ssd_A_log': 6.470222e+00, 'ssd_D': 3.905776e+00, 'ssd_norm_w': 2.272328e+00, 'gdn_conv_w': 6.650727e-01, 'gdn_dt_bias': 3.715166e+00, 'gdn_A_log': 3.910490e+00, 'gdn_norm_w': 2.574449e+00, 'ffn_w_gate': 4.114718e-01, 'ffn_w_up': 5.855920e-01, 'ffn_w_down': 9.844955e-01}


def _to_microbatches(a, axis):
    t = _jnp.moveaxis(a, axis, 0)
    t = t.reshape((N_MICROBATCH, t.shape[0] // N_MICROBATCH) + t.shape[1:])
    return _jnp.moveaxis(t, 1, axis + 1)


def setup_inputs(seed: int = 0) -> dict:
    inp = _fwd_setup_inputs(seed)
    key = _jax.random.fold_in(_jax.random.key(seed), 7919)
    shape, _ = _output_shape()
    out = dict(inp)
    out["loss_target"] = _jax.random.normal(_jax.random.fold_in(key, 0), shape, _jnp.float32)
    for i, name in enumerate(TWIN_WEIGHTS):
        w = inp[name].astype(_jnp.float32)
        if MOMENT_SCALE is None:
            s = _jnp.sqrt(_jnp.mean(_jnp.square(w)) + 1e-30)
        else:
            s = MOMENT_SCALE[name]
        km, kv = _jax.random.split(_jax.random.fold_in(key, i + 1))
        out[name] = w
        out["m_" + name] = s * _jax.random.normal(km, w.shape, _jnp.float32)
        out["v_" + name] = (s * s) * _jax.random.uniform(kv, w.shape, _jnp.float32, 0.5, 1.5)
    if N_MICROBATCH > 1:
        for name, axis in PER_EXAMPLE_BATCH_AXIS.items():
            out[name] = _to_microbatches(out[name], axis)
    return {'x': out['x'], 'pre_mix_norm': out['pre_mix_norm'], 'post_mix_norm': out['post_mix_norm'], 'pre_ffn_norm': out['pre_ffn_norm'], 'post_ffn_norm': out['post_ffn_norm'], 'w_in': out['w_in'], 'w_out': out['w_out'], 'attn_sinks': out['attn_sinks'], 'ssd_conv_w': out['ssd_conv_w'], 'ssd_conv_b': out['ssd_conv_b'], 'ssd_dt_bias': out['ssd_dt_bias'], 'ssd_A_log': out['ssd_A_log'], 'ssd_D': out['ssd_D'], 'ssd_norm_w': out['ssd_norm_w'], 'gdn_conv_w': out['gdn_conv_w'], 'gdn_dt_bias': out['gdn_dt_bias'], 'gdn_A_log': out['gdn_A_log'], 'gdn_norm_w': out['gdn_norm_w'], 'ffn_w_gate': out['ffn_w_gate'], 'ffn_w_up': out['ffn_w_up'], 'ffn_w_down': out['ffn_w_down'], 'loss_target': out['loss_target'], 'm_pre_mix_norm': out['m_pre_mix_norm'], 'm_post_mix_norm': out['m_post_mix_norm'], 'm_pre_ffn_norm': out['m_pre_ffn_norm'], 'm_post_ffn_norm': out['m_post_ffn_norm'], 'm_w_in': out['m_w_in'], 'm_w_out': out['m_w_out'], 'm_attn_sinks': out['m_attn_sinks'], 'm_ssd_conv_w': out['m_ssd_conv_w'], 'm_ssd_conv_b': out['m_ssd_conv_b'], 'm_ssd_dt_bias': out['m_ssd_dt_bias'], 'm_ssd_A_log': out['m_ssd_A_log'], 'm_ssd_D': out['m_ssd_D'], 'm_ssd_norm_w': out['m_ssd_norm_w'], 'm_gdn_conv_w': out['m_gdn_conv_w'], 'm_gdn_dt_bias': out['m_gdn_dt_bias'], 'm_gdn_A_log': out['m_gdn_A_log'], 'm_gdn_norm_w': out['m_gdn_norm_w'], 'm_ffn_w_gate': out['m_ffn_w_gate'], 'm_ffn_w_up': out['m_ffn_w_up'], 'm_ffn_w_down': out['m_ffn_w_down'], 'v_pre_mix_norm': out['v_pre_mix_norm'], 'v_post_mix_norm': out['v_post_mix_norm'], 'v_pre_ffn_norm': out['v_pre_ffn_norm'], 'v_post_ffn_norm': out['v_post_ffn_norm'], 'v_w_in': out['v_w_in'], 'v_w_out': out['v_w_out'], 'v_attn_sinks': out['v_attn_sinks'], 'v_ssd_conv_w': out['v_ssd_conv_w'], 'v_ssd_conv_b': out['v_ssd_conv_b'], 'v_ssd_dt_bias': out['v_ssd_dt_bias'], 'v_ssd_A_log': out['v_ssd_A_log'], 'v_ssd_D': out['v_ssd_D'], 'v_ssd_norm_w': out['v_ssd_norm_w'], 'v_gdn_conv_w': out['v_gdn_conv_w'], 'v_gdn_dt_bias': out['v_gdn_dt_bias'], 'v_gdn_A_log': out['v_gdn_A_log'], 'v_gdn_norm_w': out['v_gdn_norm_w'], 'v_ffn_w_gate': out['v_ffn_w_gate'], 'v_ffn_w_up': out['v_ffn_w_up'], 'v_ffn_w_down': out['v_ffn_w_down']}


def _loss(weights, diff, rest, loss_target):
    with _jax.named_scope("forward"):
        args = {**rest, TWIN_DIFF_INPUT: diff, **{k: w.astype(_WEIGHT_DTYPES[k]) for k, w in weights.items()}}
        y = _forward(args)
    with _jax.named_scope("loss_head"):
        err = _jnp.square(y.astype(_jnp.float32) - loss_target)
        return 0.5 * _jnp.sum(_jnp.mean(err, axis=-1)) if err.ndim else 0.5 * err


def _adamw(w, g, m, v):
    m = ADAM_B1 * m + (1.0 - ADAM_B1) * g
    v = ADAM_B2 * v + (1.0 - ADAM_B2) * _jnp.square(g)
    m_hat = m / (1.0 - ADAM_B1 ** ADAM_STEP)
    v_hat = v / (1.0 - ADAM_B2 ** ADAM_STEP)
    delta = -ADAM_LR * (m_hat / (_jnp.sqrt(v_hat) + ADAM_EPS) + ADAM_WD * w)
    return delta, m, v


def reference(x, pre_mix_norm, post_mix_norm, pre_ffn_norm, post_ffn_norm, w_in, w_out, attn_sinks, ssd_conv_w, ssd_conv_b, ssd_dt_bias, ssd_A_log, ssd_D, ssd_norm_w, gdn_conv_w, gdn_dt_bias, gdn_A_log, gdn_norm_w, ffn_w_gate, ffn_w_up, ffn_w_down, loss_target, m_pre_mix_norm, m_post_mix_norm, m_pre_ffn_norm, m_post_ffn_norm, m_w_in, m_w_out, m_attn_sinks, m_ssd_conv_w, m_ssd_conv_b, m_ssd_dt_bias, m_ssd_A_log, m_ssd_D, m_ssd_norm_w, m_gdn_conv_w, m_gdn_dt_bias, m_gdn_A_log, m_gdn_norm_w, m_ffn_w_gate, m_ffn_w_up, m_ffn_w_down, v_pre_mix_norm, v_post_mix_norm, v_pre_ffn_norm, v_post_ffn_norm, v_w_in, v_w_out, v_attn_sinks, v_ssd_conv_w, v_ssd_conv_b, v_ssd_dt_bias, v_ssd_A_log, v_ssd_D, v_ssd_norm_w, v_gdn_conv_w, v_gdn_dt_bias, v_gdn_A_log, v_gdn_norm_w, v_ffn_w_gate, v_ffn_w_up, v_ffn_w_down):
    given = dict(x=x, pre_mix_norm=pre_mix_norm, post_mix_norm=post_mix_norm, pre_ffn_norm=pre_ffn_norm, post_ffn_norm=post_ffn_norm, w_in=w_in, w_out=w_out, attn_sinks=attn_sinks, ssd_conv_w=ssd_conv_w, ssd_conv_b=ssd_conv_b, ssd_dt_bias=ssd_dt_bias, ssd_A_log=ssd_A_log, ssd_D=ssd_D, ssd_norm_w=ssd_norm_w, gdn_conv_w=gdn_conv_w, gdn_dt_bias=gdn_dt_bias, gdn_A_log=gdn_A_log, gdn_norm_w=gdn_norm_w, ffn_w_gate=ffn_w_gate, ffn_w_up=ffn_w_up, ffn_w_down=ffn_w_down, loss_target=loss_target, m_pre_mix_norm=m_pre_mix_norm, m_post_mix_norm=m_post_mix_norm, m_pre_ffn_norm=m_pre_ffn_norm, m_post_ffn_norm=m_post_ffn_norm, m_w_in=m_w_in, m_w_out=m_w_out, m_attn_sinks=m_attn_sinks, m_ssd_conv_w=m_ssd_conv_w, m_ssd_conv_b=m_ssd_conv_b, m_ssd_dt_bias=m_ssd_dt_bias, m_ssd_A_log=m_ssd_A_log, m_ssd_D=m_ssd_D, m_ssd_norm_w=m_ssd_norm_w, m_gdn_conv_w=m_gdn_conv_w, m_gdn_dt_bias=m_gdn_dt_bias, m_gdn_A_log=m_gdn_A_log, m_gdn_norm_w=m_gdn_norm_w, m_ffn_w_gate=m_ffn_w_gate, m_ffn_w_up=m_ffn_w_up, m_ffn_w_down=m_ffn_w_down, v_pre_mix_norm=v_pre_mix_norm, v_post_mix_norm=v_post_mix_norm, v_pre_ffn_norm=v_pre_ffn_norm, v_post_ffn_norm=v_post_ffn_norm, v_w_in=v_w_in, v_w_out=v_w_out, v_attn_sinks=v_attn_sinks, v_ssd_conv_w=v_ssd_conv_w, v_ssd_conv_b=v_ssd_conv_b, v_ssd_dt_bias=v_ssd_dt_bias, v_ssd_A_log=v_ssd_A_log, v_ssd_D=v_ssd_D, v_ssd_norm_w=v_ssd_norm_w, v_gdn_conv_w=v_gdn_conv_w, v_gdn_dt_bias=v_gdn_dt_bias, v_gdn_A_log=v_gdn_A_log, v_gdn_norm_w=v_gdn_norm_w, v_ffn_w_gate=v_ffn_w_gate, v_ffn_w_up=v_ffn_w_up, v_ffn_w_down=v_ffn_w_down)
    weights = {n: given[n] for n in TWIN_WEIGHTS}
    shared = {n: given[n] for n in SHARED_INPUTS}
    per_example = {n: given[n] for n in ['x']}
    grad_fn = _jax.value_and_grad(_loss, argnums=(0, 1))

    def one_microbatch(ex, loss_target):
        ex = dict(ex)
        diff = ex.pop(TWIN_DIFF_INPUT)
        return grad_fn(weights, diff, {**shared, **ex}, loss_target)

    if N_MICROBATCH == 1:
        loss, (grad_w, grad_x) = one_microbatch(per_example, given["loss_target"])
    else:
        def body(carry, xs):
            loss_sum, grad_sum = carry
            l_k, (gw_k, gx_k) = one_microbatch(xs[0], xs[1])
            with _jax.named_scope("update"):
                return (loss_sum + l_k, _jax.tree.map(_jnp.add, grad_sum, gw_k)), gx_k

        init = (_jnp.zeros((), _jnp.float32), _jax.tree.map(_jnp.zeros_like, weights))
        (loss, grad_w), grad_x = _jax.lax.scan(body, init, (per_example, given["loss_target"]))
    with _jax.named_scope("update"):
        delta_w, new_m, new_v = {}, {}, {}
        for n in TWIN_WEIGHTS:
            delta_w[n], new_m[n], new_v[n] = _adamw(weights[n], grad_w[n], given["m_" + n], given["v_" + n])
    return (loss, grad_x, *[grad_w[n] for n in TWIN_WEIGHTS], *[delta_w[n] for n in TWIN_WEIGHTS],
            *[new_m[n] for n in TWIN_WEIGHTS], *[new_v[n] for n in TWIN_WEIGHTS])
```

```python
import functools
import math

import jax
import jax.numpy as jnp
from jax import lax
from jax.experimental import pallas as pl
from jax.experimental.pallas import tpu as pltpu

F32, BF16 = jnp.float32, jnp.bfloat16
HI = lax.Precision.HIGHEST
MESH = pl.DeviceIdType.MESH
S = jax.ShapeDtypeStruct

D_MODEL = 1024
DEPTH = 2
CHUNK = 64
EPS = 1e-6
FF = 2816
N_CHIPS = 4
N_DEV = 8
LANES = 128

VMEM_LIMIT_BYTES = 56 * 1024 * 1024

PC_GQKV, PC_GZ, PC_XBC, PC_ATT, PC_SZ, PC_DT, PC_BA, PC_TOT = 0, 768, 1024, 2048, 2560, 3072, 3200, 3328

ADAM_LR, ADAM_B1, ADAM_B2, ADAM_EPS, ADAM_WD, ADAM_STEP = 0.001, 0.9, 0.999, 1e-08, 0.01, 10

ALIBI_SLOPES = tuple(2.0 ** (-8.0 * (h + 1) / 4) for h in range(4))


def _params(sem=None, **kw):
    if sem is not None:
        kw["dimension_semantics"] = sem
    return pltpu.CompilerParams(vmem_limit_bytes=VMEM_LIMIT_BYTES, **kw)


def _dot(a, b, prec=None):
    return jnp.dot(a, b, precision=prec, preferred_element_type=F32)


def _dot_nt(a, b, prec=None):
    return lax.dot_general(a, b, (((1,), (1,)), ((), ())), precision=prec, preferred_element_type=F32)


def _dot_tn(a, b, prec=None):
    return lax.dot_general(a, b, (((0,), (0,)), ((), ())), precision=prec, preferred_element_type=F32)


def _iota2(n, m):
    return lax.broadcasted_iota(jnp.int32, (n, m), 0), lax.broadcasted_iota(jnp.int32, (n, m), 1)


def _pick_col(arr, idx):
    ci = lax.broadcasted_iota(jnp.int32, arr.shape, 1)
    return jnp.sum(jnp.where(ci == idx, arr, 0.0), axis=1, keepdims=True)


def _pick_row(arr, idx):
    ri = lax.broadcasted_iota(jnp.int32, arr.shape, 0)
    return jnp.sum(jnp.where(ri == idx, arr, 0.0), axis=0, keepdims=True)


def _col_to_row(col, eye):
    return jnp.sum(eye * col, axis=0, keepdims=True)


def _rms(x, w):
    return x * lax.rsqrt(jnp.mean(x * x, axis=-1, keepdims=True) + EPS) * w


def _mm_nn(a, b, tm, tn, out_dtype, name):
    m, k = a.shape
    n = b.shape[1]
    tm, tn = min(tm, m), min(tn, n)

    def body(a_ref, b_ref, o_ref):
        o_ref[...] = _dot(a_ref[...], b_ref[...]).astype(o_ref.dtype)

    return pl.pallas_call(
        body, grid=(n // tn, m // tm), name=name,
        in_specs=[pl.BlockSpec((tm, k), lambda j, i: (i, 0)), pl.BlockSpec((k, tn), lambda j, i: (0, j))],
        out_specs=pl.BlockSpec((tm, tn), lambda j, i: (i, j)),
        out_shape=S((m, n), out_dtype), compiler_params=_params(("arbitrary", "arbitrary")),
    )(a, b)


def _mm_nt(a, b, tm, tn, out_dtype, name):
    m, k = a.shape
    n = b.shape[0]
    tm, tn = min(tm, m), min(tn, n)

    def body(a_ref, b_ref, o_ref):
        o_ref[...] = _dot_nt(a_ref[...], b_ref[...]).astype(o_ref.dtype)

    return pl.pallas_call(
        body, grid=(n // tn, m // tm), name=name,
        in_specs=[pl.BlockSpec((tm, k), lambda j, i: (i, 0)), pl.BlockSpec((tn, k), lambda j, i: (j, 0))],
        out_specs=pl.BlockSpec((tm, tn), lambda j, i: (i, j)),
        out_shape=S((m, n), out_dtype), compiler_params=_params(("arbitrary", "arbitrary")),
    )(a, b)


def _mm_tn(a, b, tm, tn, tk, name):
    t, m = a.shape
    n = b.shape[1]
    tm, tn, tk = min(tm, m), min(tn, n), min(tk, t)

    def body(a_ref, b_ref, o_ref):
        part = _dot_tn(a_ref[...], b_ref[...])

        @pl.when(pl.program_id(2) == 0)
        def _():
            o_ref[...] = part

        @pl.when(pl.program_id(2) > 0)
        def _():
            o_ref[...] += part

    return pl.pallas_call(
        body, grid=(m // tm, n // tn, t // tk), name=name,
        in_specs=[pl.BlockSpec((tk, tm), lambda i, j, k: (k, i)), pl.BlockSpec((tk, tn), lambda i, j, k: (k, j))],
        out_specs=pl.BlockSpec((tm, tn), lambda i, j, k: (i, j)),
        out_shape=S((m, n), F32), compiler_params=_params(("arbitrary", "arbitrary", "arbitrary")),
    )(a, b)


def _rowcall(fn, rows, params, row_outs, acc_outs, name, tm=512):
    t = rows[0].shape[0]
    tm = min(tm, t)
    n_in = len(rows) + len(params)
    n_ro = len(row_outs)

    def body(*refs):
        ro, ao = fn(*[r[...] for r in refs[:n_in]])
        for ref, v in zip(refs[n_in:n_in + n_ro], ro):
            ref[...] = v.astype(ref.dtype)
        acc_refs = refs[n_in + n_ro:]
        if acc_refs:
            @pl.when(pl.program_id(0) == 0)
            def _():
                for ref, v in zip(acc_refs, ao):
                    ref[...] = v

            @pl.when(pl.program_id(0) > 0)
            def _():
                for ref, v in zip(acc_refs, ao):
                    ref[...] += v

    in_specs = [pl.BlockSpec((tm, r.shape[1]), lambda i: (i, 0)) for r in rows]
    in_specs += [pl.BlockSpec(p.shape, lambda i: (0, 0)) for p in params]
    out_specs = [pl.BlockSpec((tm, c), lambda i: (i, 0)) for c, _ in row_outs]
    out_specs += [pl.BlockSpec(shape, lambda i: (0, 0)) for shape in acc_outs]
    out_shape = [S((t, c), dt) for c, dt in row_outs] + [S(shape, F32) for shape in acc_outs]
    return pl.pallas_call(
        body, grid=(t // tm,), name=name, in_specs=in_specs, out_specs=out_specs, out_shape=out_shape,
        compiler_params=_params(("arbitrary",)),
    )(*rows, *params)


def _prenorm(x, w, name):
    def fn(x, w):
        return (_rms(x, w),), ()
    return _rowcall(fn, [x], [w], [(D_MODEL, BF16)], [], name)[0]


def _resid_norm(xin, m, w_post, w_next, name):
    def fn(xin, m, w_post, w_next):
        xo = xin + _rms(m, w_post)
        return (xo, _rms(xo, w_next)), ()
    return _rowcall(fn, [xin, m], [w_post, w_next], [(D_MODEL, F32), (D_MODEL, BF16)], [], name)


def _resid_loss(xin, m, w_post, target, name):
    def fn(xin, m, target, w_post):
        r, vjp = jax.vjp(_rms, m, w_post)
        err = xin + r - target
        dy = err * (1.0 / D_MODEL)
        dm, dw = vjp(dy)
        tot = jnp.sum(jnp.sum(err * err, axis=1, keepdims=True), axis=0, keepdims=True) * (0.5 / D_MODEL)
        lane = lax.broadcasted_iota(jnp.int32, (1, LANES), 1)
        return (dy, dm), (jnp.where(lane == 0, tot, 0.0), dw)
    return _rowcall(fn, [xin, m, target], [w_post], [(D_MODEL, F32), (D_MODEL, BF16)],
                    [(1, LANES), (1, D_MODEL)], name)


def _resid_norm_bwd(x_out, m, d_direct, dh, w_post, w_next, name):
    def fn(x_out, m, d_direct, dh, w_post, w_next):
        _, vjp_n = jax.vjp(_rms, x_out, w_next)
        dx, dwn = vjp_n(dh)
        d_total = d_direct + dx
        _, vjp_p = jax.vjp(_rms, m, w_post)
        dm, dwp = vjp_p(d_total)
        return (d_total, dm), (dwn, dwp)
    return _rowcall(fn, [x_out, m, d_direct, dh], [w_post, w_next], [(D_MODEL, F32), (D_MODEL, BF16)],
                    [(1, D_MODEL), (1, D_MODEL)], name)


def _prenorm_bwd(x, d_direct, dh, w, name):
    def fn(x, d_direct, dh, w):
        _, vjp = jax.vjp(_rms, x, w)
        dx, dw = vjp(dh)
        return (d_direct + dx,), (dw,)
    return _rowcall(fn, [x, d_direct, dh], [w], [(D_MODEL, F32)], [(1, D_MODEL)], name)


def _swiglu_fn(gu):
    return jax.nn.silu(gu[:, :FF]) * gu[:, FF:]


def _swiglu(gu, name):
    def fn(gu):
        return (_swiglu_fn(gu),), ()
    return _rowcall(fn, [gu], [], [(FF, BF16)], [], name, tm=256)[0]


def _swiglu_bwd(gu, da, name):
    def fn(gu, da):
        _, vjp = jax.vjp(_swiglu_fn, gu)
        return (vjp(da)[0],), ()
    return _rowcall(fn, [gu, da], [], [(2 * FF, BF16)], [], name, tm=256)[0]


def _conv_fwd(proj, col0, width, w, b, name, tm=512):
    t = proj.shape[0]
    tm = min(tm, t)
    cb = col0 // width

    def body(x_ref, w_ref, b_ref, o_ref, ext):
        @pl.when(pl.program_id(0) == 0)
        def _():
            ext[0:8, :] = jnp.zeros((8, width), F32)

        ext[8:8 + tm, :] = x_ref[...]
        y = b_ref[...] + w_ref[0:1, :] * ext[pl.ds(5, tm), :]
        for k in range(1, 4):
            y = y + w_ref[k:k + 1, :] * ext[pl.ds(5 + k, tm), :]
        o_ref[...] = jax.nn.silu(y)
        ext[0:8, :] = ext[tm:tm + 8, :]

    return pl.pallas_call(
        body, grid=(t // tm,), name=name,
        in_specs=[pl.BlockSpec((tm, width), lambda i: (i, cb)), pl.BlockSpec((4, width), lambda i: (0, 0)),
                  pl.BlockSpec((1, width), lambda i: (0, 0))],
        out_specs=pl.BlockSpec((tm, width), lambda i: (i, 0)),
        out_shape=S((t, width), F32), scratch_shapes=[pltpu.VMEM((tm + 8, width), F32)],
        compiler_params=_params(("arbitrary",)),
    )(proj, w, b)


def _conv_bwd(proj, col0, width, w, b, dact, name, tm=512):
    t = proj.shape[0]
    tm = min(tm, t)
    nb = t // tm
    cb = col0 // width
    hb = tm // 8

    def body(x_ref, halo_ref, d_ref, w_ref, b_ref, dx_ref, dw_ref, db_ref, extx, extd):
        i = pl.program_id(0)
        blk = nb - 1 - i

        @pl.when(i == 0)
        def _():
            extd[tm:tm + 8, :] = jnp.zeros((8, width), F32)
            dw_ref[...] = jnp.zeros((4, width), F32)
            db_ref[...] = jnp.zeros((1, width), F32)

        extx[0:8, :] = jnp.where(blk == 0, 0.0, halo_ref[...])
        extx[8:8 + tm, :] = x_ref[...]
        y = b_ref[...] + w_ref[0:1, :] * extx[pl.ds(5, tm), :]
        for k in range(1, 4):
            y = y + w_ref[k:k + 1, :] * extx[pl.ds(5 + k, tm), :]
        sig = jax.nn.sigmoid(y)
        dy = d_ref[...] * (sig * (1.0 + y * (1.0 - sig)))
        extd[0:tm, :] = dy
        dx = w_ref[0:1, :] * extd[pl.ds(3, tm), :]
        for k in range(1, 4):
            dx = dx + w_ref[k:k + 1, :] * extd[pl.ds(3 - k, tm), :]
        dx_ref[...] = dx.astype(dx_ref.dtype)
        for k in range(4):
            dw_ref[k:k + 1, :] += jnp.sum(dy * extx[pl.ds(5 + k, tm), :], axis=0, keepdims=True)
        db_ref[...] += jnp.sum(dy, axis=0, keepdims=True)
        extd[tm:tm + 8, :] = extd[0:8, :]

    return pl.pallas_call(
        body, grid=(nb,), name=name,
        in_specs=[pl.BlockSpec((tm, width), lambda i: (nb - 1 - i, cb)),
                  pl.BlockSpec((8, width), lambda i: (jnp.maximum((nb - 1 - i) * hb - 1, 0), cb)),
                  pl.BlockSpec((tm, width), lambda i: (nb - 1 - i, 0)),
                  pl.BlockSpec((4, width), lambda i: (0, 0)), pl.BlockSpec((1, width), lambda i: (0, 0))],
        out_specs=[pl.BlockSpec((tm, width), lambda i: (nb - 1 - i, 0)), pl.BlockSpec((4, width), lambda i: (0, 0)),
                   pl.BlockSpec((1, width), lambda i: (0, 0))],
        out_shape=[S((t, width), BF16), S((4, width), F32), S((1, width), F32)],
        scratch_shapes=[pltpu.VMEM((tm + 8, width), F32), pltpu.VMEM((tm + 8, width), F32)],
        compiler_params=_params(("arbitrary",)),
    )(proj, proj, dact, w, b)


SWA_BQ = 256
SWA_BACK = 128


def _swa_block(q, kw, vw, sinks, blk):
    nq, nk = SWA_BQ, SWA_BQ + SWA_BACK
    r, j = _iota2(nq, nk)
    rel = r // CHUNK + 2 - j // CHUNK
    valid = (rel >= 0) & (rel <= 2) & (blk * (SWA_BQ // CHUNK) + j // CHUNK - 2 >= 0)
    dist = jnp.abs(r + SWA_BACK - j).astype(F32)
    outs = []
    for h in range(4):
        kv = h // 2
        qh = q[:, 64 * h:64 * h + 64]
        kh = kw[:, 64 * kv:64 * kv + 64]
        vh = vw[:, 64 * kv:64 * kv + 64]
        s = _dot_nt(qh, kh) * 0.125 - ALIBI_SLOPES[h] * dist
        s = jnp.where(valid, s, -1e30)
        sink = _pick_col(sinks, h)
        m = jnp.maximum(jnp.max(s, axis=1, keepdims=True), sink)
        e = jnp.exp(s - m)
        den = jnp.sum(e, axis=1, keepdims=True) + jnp.exp(sink - m)
        outs.append(_dot(e / den, vh))
    return jnp.concatenate(outs, axis=1)


def _swa_fwd(proj, sinks, name):
    t = proj.shape[0]
    qb, kb = PC_ATT // 256, PC_ATT // 128 + 2
    win = SWA_BQ + SWA_BACK

    def body(q_ref, k_ref, v_ref, s_ref, o_ref, kp, vp):
        i = pl.program_id(0)

        @pl.when(i == 0)
        def _():
            kp[0:SWA_BACK, :] = jnp.zeros((SWA_BACK, 128), F32)
            vp[0:SWA_BACK, :] = jnp.zeros((SWA_BACK, 128), F32)
            kp[SWA_BACK:, :] = k_ref[...]
            vp[SWA_BACK:, :] = v_ref[...]

        start = pl.multiple_of(i * SWA_BQ, SWA_BQ)
        o = _swa_block(q_ref[...], kp[pl.ds(start, win), :], vp[pl.ds(start, win), :], s_ref[...], i)
        o_ref[...] = o.astype(o_ref.dtype)

    return pl.pallas_call(
        body, grid=(t // SWA_BQ,), name=name,
        in_specs=[pl.BlockSpec((SWA_BQ, 256), lambda i: (i, qb)), pl.BlockSpec((t, 128), lambda i: (0, kb)),
                  pl.BlockSpec((t, 128), lambda i: (0, kb + 1)), pl.BlockSpec((1, LANES), lambda i: (0, 0))],
        out_specs=pl.BlockSpec((SWA_BQ, 256), lambda i: (i, 0)),
        out_shape=S((t, 256), BF16),
        scratch_shapes=[pltpu.VMEM((t + SWA_BACK, 128), F32), pltpu.VMEM((t + SWA_BACK, 128), F32)],
        compiler_params=_params(("arbitrary",)),
    )(proj, proj, proj, sinks)


def _swa_bwd(proj, sinks, dcat, dcol0, name):
    t = proj.shape[0]
    nb = t // SWA_BQ
    qb, kb = PC_ATT // 256, PC_ATT // 128 + 2
    db = dcol0 // 256
    win = SWA_BQ + SWA_BACK

    def body(q_ref, k_ref, v_ref, s_ref, do_ref, dq_ref, dk_ref, dv_ref, ds_ref, kp, vp, dkp, dvp):
        i = pl.program_id(0)

        @pl.when(i == 0)
        def _():
            kp[0:SWA_BACK, :] = jnp.zeros((SWA_BACK, 128), F32)
            vp[0:SWA_BACK, :] = jnp.zeros((SWA_BACK, 128), F32)
            kp[SWA_BACK:, :] = k_ref[...]
            vp[SWA_BACK:, :] = v_ref[...]
            dkp[...] = jnp.zeros_like(dkp)
            dvp[...] = jnp.zeros_like(dvp)
            ds_ref[...] = jnp.zeros_like(ds_ref)

        start = pl.multiple_of(i * SWA_BQ, SWA_BQ)
        _, vjp = jax.vjp(functools.partial(_swa_block, blk=i), q_ref[...], kp[pl.ds(start, win), :],
                         vp[pl.ds(start, win), :], s_ref[...])
        dq, dkw, dvw, dsk = vjp(do_ref[...])
        dq_ref[...] = dq.astype(dq_ref.dtype)
        dkp[pl.ds(start, win), :] += dkw
        dvp[pl.ds(start, win), :] += dvw
        ds_ref[...] += dsk

        @pl.when(i == nb - 1)
        def _():
            dk_ref[...] = dkp[SWA_BACK:, :].astype(dk_ref.dtype)
            dv_ref[...] = dvp[SWA_BACK:, :].astype(dv_ref.dtype)

    return pl.pallas_call(
        body, grid=(nb,), name=name,
        in_specs=[pl.BlockSpec((SWA_BQ, 256), lambda i: (i, qb)), pl.BlockSpec((t, 128), lambda i: (0, kb)),
                  pl.BlockSpec((t, 128), lambda i: (0, kb + 1)), pl.BlockSpec((1, LANES), lambda i: (0, 0)),
                  pl.BlockSpec((SWA_BQ, 256), lambda i: (i, db))],
        out_specs=[pl.BlockSpec((SWA_BQ, 256), lambda i: (i, 0)), pl.BlockSpec((t, 128), lambda i: (0, 0)),
                   pl.BlockSpec((t, 128), lambda i: (0, 0)), pl.BlockSpec((1, LANES), lambda i: (0, 0))],
        out_shape=[S((t, 256), BF16), S((t, 128), BF16), S((t, 128), BF16), S((1, LANES), F32)],
        scratch_shapes=[pltpu.VMEM((t + SWA_BACK, 128), F32) for _ in range(4)],
        compiler_params=_params(("arbitrary",)),
    )(proj, proj, proj, sinks, dcat)


def _ssd_chunk(z, xbc, dt_raw, state, dtb, alog, dsk, nw):
    n = CHUNK
    r, c = _iota2(n, n)
    tril = r >= c
    eye = (r == c).astype(F32)
    dt = jax.nn.softplus(dt_raw + dtb)
    acs = _dot(tril.astype(F32), dt * (-jnp.exp(alog)), HI)
    xs, bm, cm = xbc[:, :512], xbc[:, 512:768], xbc[:, 768:1024]
    ys, new_states = [], []
    for h in range(8):
        g = h // 4
        dth = _pick_col(dt, h)
        acol = _pick_col(acs, h)
        arow = _col_to_row(acol, eye)
        lmat = jnp.where(tril, jnp.exp(jnp.where(tril, acol - arow, 0.0)), 0.0)
        xh = xs[:, 64 * h:64 * h + 64]
        xc = xh * dth
        bh = bm[:, 128 * g:128 * g + 128]
        ch = cm[:, 128 * g:128 * g + 128]
        st = state[64 * h:64 * h + 64, :]
        alast = _pick_row(acol, n - 1)
        y = _dot(_dot_nt(ch, bh) * lmat, xc) + _dot_nt(ch, st) * jnp.exp(acol) + xh * _pick_col(dsk, h)
        new_states.append(st * jnp.exp(alast) + _dot_tn(xc * jnp.exp(alast - acol), bh))
        ys.append(y)
    gg = jnp.concatenate(ys, axis=1) * jax.nn.silu(z)
    outs = []
    for gi in range(2):
        gv = gg[:, 256 * gi:256 * gi + 256]
        outs.append(gv * lax.rsqrt(jnp.mean(gv * gv, axis=-1, keepdims=True) + EPS))
    return jnp.concatenate(outs, axis=1) * nw, jnp.concatenate(new_states, axis=0)


def _ssd_fwd(proj, xbc, dtb, alog, dsk, nw, name):
    t = proj.shape[0]
    nc = t // CHUNK

    def body(z_ref, x_ref, dt_ref, dtb_ref, al_ref, d_ref, nw_ref, o_ref, st_ref, state):
        @pl.when(pl.program_id(0) == 0)
        def _():
            state[...] = jnp.zeros_like(state)

        st_ref[0] = state[...]
        o, ns = _ssd_chunk(z_ref[...], x_ref[...], dt_ref[...], state[...], dtb_ref[...], al_ref[...], d_ref[...],
                           nw_ref[...])
        o_ref[...] = o.astype(o_ref.dtype)
        state[...] = ns

    vec = pl.BlockSpec((1, LANES), lambda i: (0, 0))
    return pl.pallas_call(
        body, grid=(nc,), name=name,
        in_specs=[pl.BlockSpec((CHUNK, 512), lambda i: (i, PC_SZ // 512)), pl.BlockSpec((CHUNK, 1024), lambda i: (i, 0)),
                  pl.BlockSpec((CHUNK, 128), lambda i: (i, PC_DT // 128)), vec, vec, vec,
                  pl.BlockSpec((1, 512), lambda i: (0, 0))],
        out_specs=[pl.BlockSpec((CHUNK, 512), lambda i: (i, 0)), pl.BlockSpec((1, 512, 128), lambda i: (i, 0, 0))],
        out_shape=[S((t, 512), BF16), S((nc, 512, 128), F32)],
        scratch_shapes=[pltpu.VMEM((512, 128), F32)],
        compiler_params=_params(("arbitrary",)),
    )(proj, xbc, proj, dtb, alog, dsk, nw)


def _ssd_bwd(proj, xbc, states, dtb, alog, dsk, nw, dcat, dcol0, name):
    t = proj.shape[0]
    nc = t // CHUNK
    db = dcol0 // 512

    def body(z_ref, x_ref, dt_ref, st_ref, dtb_ref, al_ref, d_ref, nw_ref, do_ref,
             dz_ref, dx_ref, ddt_ref, gdtb_ref, gal_ref, gd_ref, gnw_ref, dstate):
        @pl.when(pl.program_id(0) == 0)
        def _():
            dstate[...] = jnp.zeros_like(dstate)
            gdtb_ref[...] = jnp.zeros_like(gdtb_ref)
            gal_ref[...] = jnp.zeros_like(gal_ref)
            gd_ref[...] = jnp.zeros_like(gd_ref)
            gnw_ref[...] = jnp.zeros_like(gnw_ref)

        _, vjp = jax.vjp(_ssd_chunk, z_ref[...], x_ref[...], dt_ref[...], st_ref[0], dtb_ref[...], al_ref[...],
                         d_ref[...], nw_ref[...])
        dz, dx, ddt, dst, gdtb, gal, gd, gnw = vjp((do_ref[...], dstate[...]))
        dz_ref[...] = dz.astype(dz_ref.dtype)
        dx_ref[...] = dx
        ddt_ref[...] = ddt.astype(ddt_ref.dtype)
        dstate[...] = dst
        gdtb_ref[...] += gdtb
        gal_ref[...] += gal
        gd_ref[...] += gd
        gnw_ref[...] += gnw

    rev = lambda i: nc - 1 - i
    vec = pl.BlockSpec((1, LANES), lambda i: (0, 0))
    vec512 = pl.BlockSpec((1, 512), lambda i: (0, 0))
    return pl.pallas_call(
        body, grid=(nc,), name=name,
        in_specs=[pl.BlockSpec((CHUNK, 512), lambda i: (rev(i), PC_SZ // 512)),
                  pl.BlockSpec((CHUNK, 1024), lambda i: (rev(i), 0)),
                  pl.BlockSpec((CHUNK, 128), lambda i: (rev(i), PC_DT // 128)),
                  pl.BlockSpec((1, 512, 128), lambda i: (rev(i), 0, 0)), vec, vec, vec, vec512,
                  pl.BlockSpec((CHUNK, 512), lambda i: (rev(i), db))],
        out_specs=[pl.BlockSpec((CHUNK, 512), lambda i: (rev(i), 0)), pl.BlockSpec((CHUNK, 1024), lambda i: (rev(i), 0)),
                   pl.BlockSpec((CHUNK, 128), lambda i: (rev(i), 0)), vec, vec, vec, vec512],
        out_shape=[S((t, 512), BF16), S((t, 1024), F32), S((t, 128), BF16), S((1, LANES), F32), S((1, LANES), F32),
                   S((1, LANES), F32), S((1, 512), F32)],
        scratch_shapes=[pltpu.VMEM((512, 128), F32)],
        compiler_params=_params(("arbitrary",)),
    )(proj, xbc, proj, states, dtb, alog, dsk, nw, dcat)


def _gdn_chunk(qkv, z, ba, state, dtb, alog, nw):
    n = CHUNK
    r, c = _iota2(n, n)
    tril = r >= c
    stril = r > c
    eye = (r == c).astype(F32)
    beta_all = jax.nn.sigmoid(ba)
    gcs = _dot(tril.astype(F32), -jnp.exp(alog) * jax.nn.softplus(ba + dtb), HI)
    outs, new_states = [], []
    for h in range(4):
        qh = qkv[:, 64 * h:64 * h + 64]
        kh = qkv[:, 256 + 64 * h:256 + 64 * h + 64]
        vh = qkv[:, 512 + 64 * h:512 + 64 * h + 64]
        qn = qh * lax.rsqrt(jnp.sum(qh * qh, axis=-1, keepdims=True) + EPS) * 0.125
        kn = kh * lax.rsqrt(jnp.sum(kh * kh, axis=-1, keepdims=True) + EPS)
        beta = _pick_col(beta_all, h)
        gcol = _pick_col(gcs, 4 + h)
        grow = _col_to_row(gcol, eye)
        decay = jnp.where(tril, jnp.exp(jnp.where(tril, gcol - grow, 0.0)), 0.0)
        kbeta = kn * beta
        na = -jnp.where(stril, _dot_nt(kbeta, kn) * decay, 0.0)
        tm, pw = eye + na, na
        for _ in range(5):
            pw = _dot(pw, pw, HI)
            tm = tm + _dot(tm, pw, HI)
        sol = _dot(tm, jnp.concatenate([vh * beta, kbeta * jnp.exp(gcol)], axis=1), HI)
        u, w = sol[:, :64], sol[:, 64:]
        st = state[64 * h:64 * h + 64, :]
        v_new = u - _dot(w, st)
        o = _dot(qn * jnp.exp(gcol), st) + _dot(_dot_nt(qn, kn) * decay, v_new)
        glast = _pick_row(gcol, n - 1)
        new_states.append(st * jnp.exp(glast) + _dot_tn(kn * jnp.exp(glast - gcol), v_new))
        o = o * lax.rsqrt(jnp.mean(o * o, axis=-1, keepdims=True) + EPS) * nw
        outs.append(o * jax.nn.silu(z[:, 64 * h:64 * h + 64]))
    return jnp.concatenate(outs, axis=1), jnp.concatenate(new_states, axis=0)


def _gdn_fwd(proj, qkv, dtb, alog, nw, name):
    t = proj.shape[0]
    nc = t // CHUNK

    def body(q_ref, z_ref, ba_ref, dtb_ref, al_ref, nw_ref, o_ref, st_ref, state):
        @pl.when(pl.program_id(0) == 0)
        def _():
            state[...] = jnp.zeros_like(state)

        st_ref[0] = state[...]
        o, ns = _gdn_chunk(q_ref[...], z_ref[...], ba_ref[...], state[...], dtb_ref[...], al_ref[...], nw_ref[...])
        o_ref[...] = o.astype(o_ref.dtype)
        state[...] = ns

    vec = pl.BlockSpec((1, LANES), lambda i: (0, 0))
    return pl.pallas_call(
        body, grid=(nc,), name=name,
        in_specs=[pl.BlockSpec((CHUNK, 768), lambda i: (i, 0)), pl.BlockSpec((CHUNK, 256), lambda i: (i, PC_GZ // 256)),
                  pl.BlockSpec((CHUNK, 128), lambda i: (i, PC_BA // 128)), vec, vec, pl.BlockSpec((1, 64), lambda i: (0, 0))],
        out_specs=[pl.BlockSpec((CHUNK, 256), lambda i: (i, 0)), pl.BlockSpec((1, 256, 64), lambda i: (i, 0, 0))],
        out_shape=[S((t, 256), BF16), S((nc, 256, 64), F32)],
        scratch_shapes=[pltpu.VMEM((256, 64), F32)],
        compiler_params=_params(("arbitrary",)),
    )(qkv, proj, proj, dtb, alog, nw)


def _gdn_bwd(proj, qkv, states, dtb, alog, nw, dcat, dcol0, name):
    t = proj.shape[0]
    nc = t // CHUNK
    db = dcol0 // 256

    def body(q_ref, z_ref, ba_ref, st_ref, dtb_ref, al_ref, nw_ref, do_ref,
             dq_ref, dz_ref, dba_ref, gdtb_ref, gal_ref, gnw_ref, dstate):
        @pl.when(pl.program_id(0) == 0)
        def _():
            dstate[...] = jnp.zeros_like(dstate)
            gdtb_ref[...] = jnp.zeros_like(gdtb_ref)
            gal_ref[...] = jnp.zeros_like(gal_ref)
            gnw_ref[...] = jnp.zeros_like(gnw_ref)

        _, vjp = jax.vjp(_gdn_chunk, q_ref[...], z_ref[...], ba_ref[...], st_ref[0], dtb_ref[...], al_ref[...],
                         nw_ref[...])
        dq, dz, dba, dst, gdtb, gal, gnw = vjp((do_ref[...], dstate[...]))
        dq_ref[...] = dq
        dz_ref[...] = dz.astype(dz_ref.dtype)
        dba_ref[...] = dba.astype(dba_ref.dtype)
        dstate[...] = dst
        gdtb_ref[...] += gdtb
        gal_ref[...] += gal
        gnw_ref[...] += gnw

    rev = lambda i: nc - 1 - i
    vec = pl.BlockSpec((1, LANES), lambda i: (0, 0))
    vec64 = pl.BlockSpec((1, 64), lambda i: (0, 0))
    return pl.pallas_call(
        body, grid=(nc,), name=name,
        in_specs=[pl.BlockSpec((CHUNK, 768), lambda i: (rev(i), 0)),
                  pl.BlockSpec((CHUNK, 256), lambda i: (rev(i), PC_GZ // 256)),
                  pl.BlockSpec((CHUNK, 128), lambda i: (rev(i), PC_BA // 128)),
                  pl.BlockSpec((1, 256, 64), lambda i: (rev(i), 0, 0)), vec, vec, vec64,
                  pl.BlockSpec((CHUNK, 256), lambda i: (rev(i), db))],
        out_specs=[pl.BlockSpec((CHUNK, 768), lambda i: (rev(i), 0)), pl.BlockSpec((CHUNK, 256), lambda i: (rev(i), 0)),
                   pl.BlockSpec((CHUNK, 128), lambda i: (rev(i), 0)), vec, vec, vec64],
        out_shape=[S((t, 768), F32), S((t, 256), BF16), S((t, 128), BF16), S((1, LANES), F32), S((1, LANES), F32),
                   S((1, 64), F32)],
        scratch_shapes=[pltpu.VMEM((256, 64), F32)],
        compiler_params=_params(("arbitrary",)),
    )(qkv, proj, proj, states, dtb, alog, nw, dcat)


def _pad_cols(w):
    z = jnp.zeros((w.shape[0], 120), w.dtype)
    return jnp.concatenate([w[:, 2056:2824], w[:, 2824:3080], w[:, 1024:2048], w[:, 0:512], w[:, 512:1024],
                            w[:, 2048:2056], z, w[:, 3080:3088], z], axis=1)


def _unpad_cols(g):
    return jnp.concatenate([g[:, PC_ATT:PC_ATT + 512], g[:, PC_SZ:PC_SZ + 512], g[:, PC_XBC:PC_XBC + 1024],
                            g[:, PC_DT:PC_DT + 8], g[:, PC_GQKV:PC_GQKV + 768], g[:, PC_GZ:PC_GZ + 256],
                            g[:, PC_BA:PC_BA + 8]], axis=1)


def _vec128(v, at=0):
    return jnp.zeros((1, LANES), F32).at[0, at:at + v.shape[0]].set(v)


def _local_step(x, target, w):
    lw = []
    for l in range(DEPTH):
        w_out = w["w_out"][l]
        lw.append(dict(
            pre_mix=w["pre_mix_norm"][l][None], post_mix=w["post_mix_norm"][l][None],
            pre_ffn=w["pre_ffn_norm"][l][None], post_ffn=w["post_ffn_norm"][l][None],
            w_in=_pad_cols(w["w_in"][l]).astype(BF16),
            w_out=jnp.concatenate([w_out[256:768], w_out[0:256], w_out[768:1024]], axis=0).astype(BF16),
            w_gu=jnp.concatenate([w["ffn_w_gate"][l], w["ffn_w_up"][l]], axis=1).astype(BF16),
            w_down=w["ffn_w_down"][l].astype(BF16),
            sinks=_vec128(w["attn_sinks"][l]),
            s_cw=w["ssd_conv_w"][l], s_cb=w["ssd_conv_b"][l][None],
            s_dtb=_vec128(w["ssd_dt_bias"][l]), s_alog=_vec128(w["ssd_A_log"][l]), s_d=_vec128(w["ssd_D"][l]),
            s_nw=w["ssd_norm_w"][l][None],
            g_cw=w["gdn_conv_w"][l], g_cb=jnp.zeros((1, 768), F32),
            g_dtb=_vec128(w["gdn_dt_bias"][l], 4), g_alog=_vec128(w["gdn_A_log"][l], 4), g_nw=w["gdn_norm_w"][l][None],
        ))

    saved = []
    xin = x
    h = _prenorm(x, lw[0]["pre_mix"], "prenorm0")
    for l in range(DEPTH):
        p = lw[l]
        proj = _mm_nn(h, p["w_in"], 512, PC_TOT, F32, f"inproj{l}")
        xbc = _conv_fwd(proj, PC_XBC, 1024, p["s_cw"], p["s_cb"], f"ssd_conv{l}")
        gqkv = _conv_fwd(proj, PC_GQKV, 768, p["g_cw"], p["g_cb"], f"gdn_conv{l}")
        att = _swa_fwd(proj, p["sinks"], f"swa{l}")
        ssd, s_states = _ssd_fwd(proj, xbc, p["s_dtb"], p["s_alog"], p["s_d"], p["s_nw"], f"ssd{l}")
        gdn, g_states = _gdn_fwd(proj, gqkv, p["g_dtb"], p["g_alog"], p["g_nw"], f"gdn{l}")
        cat = jnp.concatenate([ssd, att, gdn], axis=1)
        mix = _mm_nn(cat, p["w_out"], 512, 1024, F32, f"outproj{l}")
        x1, h2 = _resid_norm(xin, mix, p["post_mix"], p["pre_ffn"], f"postmix{l}")
        gu = _mm_nn(h2, p["w_gu"], 512, FF, F32, f"ffn_gu{l}")
        act = _swiglu(gu, f"swiglu{l}")
        f = _mm_nn(act, p["w_down"], 512, 1024, F32, f"ffn_down{l}")
        saved.append(dict(xin=xin, h=h, proj=proj, xbc=xbc, gqkv=gqkv, s_states=s_states, g_states=g_states, cat=cat,
                          mix=mix, x1=x1, h2=h2, gu=gu, act=act, f=f))
        if l + 1 < DEPTH:
            xin, h = _resid_norm(x1, f, p["post_ffn"], lw[l + 1]["pre_mix"], f"postffn{l}")

    g = {k: [None] * DEPTH for k in w}
    last = saved[-1]
    d_x2, d_f, loss_part, g["post_ffn_norm"][DEPTH - 1] = _resid_loss(
        last["x1"], last["f"], lw[-1]["post_ffn"], target, "loss")
    for l in reversed(range(DEPTH)):
        p, s = lw[l], saved[l]
        d_act = _mm_nt(d_f, p["w_down"], 512, FF, F32, f"d_act{l}")
        g["ffn_w_down"][l] = _mm_tn(s["act"], d_f, 1408, 1024, 512, f"dw_down{l}")
        d_gu = _swiglu_bwd(s["gu"], d_act, f"d_swiglu{l}")
        d_h2 = _mm_nt(d_gu, p["w_gu"], 256, 1024, F32, f"d_h2{l}")
        g_gu = _mm_tn(s["h2"], d_gu, 512, FF, 512, f"dw_gu{l}")
        g["ffn_w_gate"][l], g["ffn_w_up"][l] = g_gu[:, :FF], g_gu[:, FF:]
        d_x1, d_mix, g["pre_ffn_norm"][l], g["post_mix_norm"][l] = _resid_norm_bwd(
            s["x1"], s["mix"], d_x2, d_h2, p["post_mix"], p["pre_ffn"], f"d_postmix{l}")
        d_cat = _mm_nt(d_mix, p["w_out"], 512, 1024, F32, f"d_cat{l}")
        g_out = _mm_tn(s["cat"], d_mix, 512, 1024, 512, f"dw_out{l}")
        g["w_out"][l] = jnp.concatenate([g_out[512:768], g_out[0:512], g_out[768:1024]], axis=0)
        d_q, d_k, d_v, g_sinks = _swa_bwd(s["proj"], p["sinks"], d_cat, 512, f"d_swa{l}")
        d_sz, d_xbc, d_dt, g_dtb, g_alog, g_d, g["ssd_norm_w"][l] = _ssd_bwd(
            s["proj"], s["xbc"], s["s_states"], p["s_dtb"], p["s_alog"], p["s_d"], p["s_nw"], d_cat, 0, f"d_ssd{l}")
        d_gq, d_gz, d_ba, gg_dtb, gg_alog, g["gdn_norm_w"][l] = _gdn_bwd(
            s["proj"], s["gqkv"], s["g_states"], p["g_dtb"], p["g_alog"], p["g_nw"], d_cat, 768, f"d_gdn{l}")
        d_xbc_raw, g["ssd_conv_w"][l], g["ssd_conv_b"][l] = _conv_bwd(
            s["proj"], PC_XBC, 1024, p["s_cw"], p["s_cb"], d_xbc, f"d_ssd_conv{l}")
        d_gq_raw, g["gdn_conv_w"][l], _ = _conv_bwd(s["proj"], PC_GQKV, 768, p["g_cw"], p["g_cb"], d_gq, f"d_gdn_conv{l}")
        d_proj = jnp.concatenate([d_gq_raw, d_gz, d_xbc_raw, d_q, d_k, d_v, d_sz, d_dt, d_ba], axis=1)
        d_h = _mm_nt(d_proj, p["w_in"], 512, 1024, F32, f"d_h{l}")
        g["w_in"][l] = _unpad_cols(_mm_tn(s["h"], d_proj, 512, PC_TOT // 2, 512, f"dw_in{l}"))
        g["attn_sinks"][l] = g_sinks[0, :4]
        g["ssd_dt_bias"][l], g["ssd_A_log"][l], g["ssd_D"][l] = g_dtb[0, :8], g_alog[0, :8], g_d[0, :8]
        g["gdn_dt_bias"][l], g["gdn_A_log"][l] = gg_dtb[0, 4:8], gg_alog[0, 4:8]
        if l > 0:
            sp = saved[l - 1]
            d_x2, d_f, g["pre_mix_norm"][l], g["post_ffn_norm"][l - 1] = _resid_norm_bwd(
                s["xin"], sp["f"], d_x1, d_h, lw[l - 1]["post_ffn"], p["pre_mix"], f"d_postffn{l - 1}")
        else:
            grad_x, g["pre_mix_norm"][0] = _prenorm_bwd(s["xin"], d_x1, d_h, p["pre_mix"], "d_prenorm0")

    grads = {}
    for k, parts in g.items():
        parts = [a.reshape(w[k].shape[1:]) for a in parts]
        grads[k] = jnp.stack(parts, axis=0)
    return loss_part, grad_x, grads


SHARDED = (("w_in", 2), ("w_out", 1), ("ffn_w_gate", 2), ("ffn_w_up", 2), ("ffn_w_down", 1), ("ssd_conv_w", 2),
           ("gdn_conv_w", 2))
SMALL = ("pre_mix_norm", "post_mix_norm", "pre_ffn_norm", "post_ffn_norm", "attn_sinks", "ssd_conv_b", "ssd_dt_bias",
         "ssd_A_log", "ssd_D", "ssd_norm_w", "gdn_dt_bias", "gdn_A_log", "gdn_norm_w")
SLAB_COLS = 1024


SLAB_ROW_UNIT = 256


def _slab_rows(shards):
    total = sum(math.prod(a.shape) for a in shards)
    return -(-total // (SLAB_COLS * SLAB_ROW_UNIT)) * SLAB_ROW_UNIT


def _row_tile(rows, cap):
    best = rows
    for t in range(8, min(cap, rows) + 1, 8):
        if rows % t == 0:
            best = t
    return best


def _pack(shards, rows):
    flat = [a.reshape(-1) for a in shards]
    pad = rows * SLAB_COLS - sum(a.shape[0] for a in flat)
    return jnp.concatenate(flat + [jnp.zeros((pad,), F32)]).reshape(rows, SLAB_COLS)


def _unpack(slab, shapes):
    flat = slab.reshape(-1)
    out, off = [], 0
    for shp in shapes:
        n = math.prod(shp)
        out.append(flat[off:off + n].reshape(shp))
        off += n
    return out


def _pack_small(vals):
    rows = []
    for a in vals:
        f = a.reshape(-1)
        pad = -f.shape[0] % LANES
        rows.append(jnp.concatenate([f, jnp.zeros((pad,), F32)]).reshape(-1, LANES))
    return jnp.concatenate(rows, axis=0)


def _unpack_small(mat, shapes):
    out, r = [], 0
    for shp in shapes:
        n = math.prod(shp)
        nr = -(-n // LANES)
        out.append(mat[r:r + nr].reshape(-1)[:n].reshape(shp))
        r += nr
    return out


def _place():
    x, y, c = lax.axis_index("x"), lax.axis_index("y"), lax.axis_index("c")
    chips = [(1 - x, y), (x, 1 - y), (1 - x, 1 - y)]
    return x, y, c, chips


ANY = pl.BlockSpec(memory_space=pl.ANY)


def _gather_weights(slab):
    rows = slab.shape[0]
    hr = rows // 2

    def body(s_ref, g_ref, send_sems, recv_sems, local_sem):
        x, y, c, chips = _place()
        sib = (x, y, 1 - c)

        def half(chip, hc):
            return g_ref.at[chip, pl.ds(hc * hr, hr), :]

        def copy(k, src, dst, to):
            return pltpu.make_async_remote_copy(src_ref=src, dst_ref=dst, send_sem=send_sems.at[k],
                                                recv_sem=recv_sems.at[k], device_id=to, device_id_type=MESH)

        me = 2 * x + y
        mine = pltpu.make_async_copy(s_ref, g_ref.at[me], local_sem)
        mine.start()
        first = [copy(j, s_ref.at[pl.ds(c * hr, hr), :], half(me, c), (px, py, c)) for j, (px, py) in enumerate(chips)]
        for cp in first:
            cp.start()
        passed = []
        for j, (px, py) in enumerate(chips):
            src = half(2 * px + py, c)
            copy(j, src, src, (px, py, c)).wait_recv()
            fw = copy(3 + j, src, src, sib)
            fw.start()
            passed.append(fw)
        for j, (px, py) in enumerate(chips):
            dst = half(2 * px + py, 1 - c)
            copy(3 + j, dst, dst, sib).wait_recv()
        for cp in first + passed:
            cp.wait_send()
        mine.wait()

    return pl.pallas_call(
        body, name="gather_weights", in_specs=[ANY], out_specs=ANY, out_shape=S((N_CHIPS, rows, SLAB_COLS), F32),
        scratch_shapes=[pltpu.SemaphoreType.DMA((6,)), pltpu.SemaphoreType.DMA((6,)), pltpu.SemaphoreType.DMA],
    )(slab)


def _exchange_halves(d):
    rows = d.shape[1]
    hr = rows // 2

    def body(d_ref, t_ref, send_sem, recv_sem):
        x, y, c, _ = _place()
        cp = pltpu.make_async_remote_copy(
            src_ref=d_ref.at[:, pl.ds((1 - c) * hr, hr), :], dst_ref=t_ref, send_sem=send_sem, recv_sem=recv_sem,
            device_id=(x, y, 1 - c), device_id_type=MESH)
        cp.start()
        cp.wait()

    return pl.pallas_call(
        body, name="exchange_halves", in_specs=[ANY], out_specs=ANY, out_shape=S((N_CHIPS, hr, SLAB_COLS), F32),
        scratch_shapes=[pltpu.SemaphoreType.DMA, pltpu.SemaphoreType.DMA],
    )(d)


def _scatter_chips(p):
    hr = p.shape[1]

    def body(p_ref, u_ref, send_sems, recv_sems):
        x, y, c, chips = _place()
        cps = [pltpu.make_async_remote_copy(
            src_ref=p_ref.at[2 * px + py], dst_ref=u_ref.at[j], send_sem=send_sems.at[j], recv_sem=recv_sems.at[j],
            device_id=(px, py, c), device_id_type=MESH) for j, (px, py) in enumerate(chips)]
        for cp in cps:
            cp.start()
        for cp in cps:
            cp.wait()

    return pl.pallas_call(
        body, name="scatter_chips", in_specs=[ANY], out_specs=ANY, out_shape=S((3, hr, SLAB_COLS), F32),
        scratch_shapes=[pltpu.SemaphoreType.DMA((3,)), pltpu.SemaphoreType.DMA((3,))],
    )(p)


def _join_halves(q):
    hr = q.shape[0]

    def body(q_ref, o_ref, send_sem, recv_sem, local_sem):
        x, y, c, _ = _place()
        mine = pltpu.make_async_copy(q_ref, o_ref.at[pl.ds(c * hr, hr), :], local_sem)
        mine.start()
        cp = pltpu.make_async_remote_copy(
            src_ref=q_ref, dst_ref=o_ref.at[pl.ds(c * hr, hr), :], send_sem=send_sem, recv_sem=recv_sem,
            device_id=(x, y, 1 - c), device_id_type=MESH)
        cp.start()
        other = o_ref.at[pl.ds((1 - c) * hr, hr), :]
        pltpu.make_async_remote_copy(src_ref=other, dst_ref=other, send_sem=send_sem, recv_sem=recv_sem,
                                     device_id=(x, y, 1 - c), device_id_type=MESH).wait_recv()
        cp.wait_send()
        mine.wait()

    return pl.pallas_call(
        body, name="join_halves", in_specs=[ANY], out_specs=ANY, out_shape=S((2 * hr, SLAB_COLS), F32),
        scratch_shapes=[pltpu.SemaphoreType.DMA, pltpu.SemaphoreType.DMA, pltpu.SemaphoreType.DMA],
    )(q)


def _gather_small(v):
    def body(v_ref, o_ref, send_sems, recv_sems, local_sem):
        x, y, c, _ = _place()
        me = 4 * x + 2 * y + c
        mine = pltpu.make_async_copy(v_ref, o_ref.at[me], local_sem)
        mine.start()
        cps = []
        for k in range(1, N_DEV):
            fx, fy, fc = (k >> 2) & 1, (k >> 1) & 1, k & 1
            peer = (x ^ fx, y ^ fy, c ^ fc)
            cps.append(pltpu.make_async_remote_copy(
                src_ref=v_ref, dst_ref=o_ref.at[me], send_sem=send_sems.at[k - 1], recv_sem=recv_sems.at[k - 1],
                device_id=peer, device_id_type=MESH))
        for cp in cps:
            cp.start()
        for k in range(1, N_DEV):
            fx, fy, fc = (k >> 2) & 1, (k >> 1) & 1, k & 1
            dst = o_ref.at[4 * (x ^ fx) + 2 * (y ^ fy) + (c ^ fc)]
            pltpu.make_async_remote_copy(src_ref=dst, dst_ref=dst, send_sem=send_sems.at[k - 1],
                                         recv_sem=recv_sems.at[k - 1], device_id=(x, y, c),
                                         device_id_type=MESH).wait_recv()
        for cp in cps:
            cp.wait_send()
        mine.wait()

    return pl.pallas_call(
        body, name="gather_small", in_specs=[ANY], out_specs=ANY, out_shape=S((N_DEV,) + v.shape, F32),
        scratch_shapes=[pltpu.SemaphoreType.DMA((N_DEV - 1,)), pltpu.SemaphoreType.DMA((N_DEV - 1,)),
                        pltpu.SemaphoreType.DMA],
    )(v)


def _sum_leading(a, name):
    n, rows, cols = a.shape
    tm = _row_tile(rows, 640)

    def body(a_ref, o_ref):
        acc = a_ref[0]
        for k in range(1, n):
            acc = acc + a_ref[k]
        o_ref[...] = acc

    return pl.pallas_call(
        body, grid=(rows // tm,), name=name, in_specs=[pl.BlockSpec((n, tm, cols), lambda i: (0, i, 0))],
        out_specs=pl.BlockSpec((tm, cols), lambda i: (i, 0)), out_shape=S((rows, cols), F32),
        compiler_params=_params(("arbitrary",)),
    )(a)


def _add_slabs(a, b, name):
    n, hr, cols = b.shape
    tm = _row_tile(hr, 640)

    def body(a_ref, b_ref, o_ref):
        o_ref[...] = a_ref[...] + b_ref[...]

    a_half = lax.dynamic_slice_in_dim(a, lax.axis_index("c") * hr, hr, axis=1)
    spec = pl.BlockSpec((1, tm, cols), lambda k, i: (k, i, 0))
    return pl.pallas_call(
        body, grid=(n, hr // tm), name=name, in_specs=[spec, spec], out_specs=spec, out_shape=S((n, hr, cols), F32),
        compiler_params=_params(("arbitrary", "arbitrary")),
    )(a_half, b)


def _adamw(wt, g, m, v, name):
    shape = wt.shape
    cols = shape[-1]
    rows = math.prod(shape[:-1])
    tm = rows
    for cand in (512, 256, 128, 64, 32, 16, 8):
        if rows % cand == 0:
            tm = cand
            break
    c1 = 1.0 - ADAM_B1 ** ADAM_STEP
    c2 = 1.0 - ADAM_B2 ** ADAM_STEP

    def body(w_ref, g_ref, m_ref, v_ref, d_ref, nm_ref, nv_ref):
        gv = g_ref[...]
        nm = ADAM_B1 * m_ref[...] + (1.0 - ADAM_B1) * gv
        nv = ADAM_B2 * v_ref[...] + (1.0 - ADAM_B2) * (gv * gv)
        d_ref[...] = -ADAM_LR * ((nm / c1) / (jnp.sqrt(nv / c2) + ADAM_EPS) + ADAM_WD * w_ref[...])
        nm_ref[...] = nm
        nv_ref[...] = nv

    spec = pl.BlockSpec((tm, cols), lambda i: (i, 0))
    outs = pl.pallas_call(
        body, grid=(rows // tm,), name=name, in_specs=[spec] * 4, out_specs=[spec] * 3,
        out_shape=[S((rows, cols), F32)] * 3, compiler_params=_params(("arbitrary",)),
    )(*[a.reshape(rows, cols) for a in (wt, g, m, v)])
    return [o.reshape(shape) for o in outs]


WEIGHTS = ('pre_mix_norm', 'post_mix_norm', 'pre_ffn_norm', 'post_ffn_norm', 'w_in', 'w_out', 'attn_sinks', 'ssd_conv_w',
           'ssd_conv_b', 'ssd_dt_bias', 'ssd_A_log', 'ssd_D', 'ssd_norm_w', 'gdn_conv_w', 'gdn_dt_bias', 'gdn_A_log',
           'gdn_norm_w', 'ffn_w_gate', 'ffn_w_up', 'ffn_w_down')


def _step(x, target, wts, ms, vs):
    shards = [wts[k] for k, _ in SHARDED]
    shard_shapes = [a.shape for a in shards]
    rows = _slab_rows(shards)
    gathered = _gather_weights(_pack(shards, rows))
    per_chip = [_unpack(gathered[k], shard_shapes) for k in range(N_CHIPS)]
    full = dict(wts)
    for i, (k, axis) in enumerate(SHARDED):
        full[k] = jnp.concatenate([per_chip[c][i] for c in range(N_CHIPS)], axis=axis)

    loss_part, grad_x, grads = _local_step(x[0], target[0], full)

    slabs = []
    for k in range(N_CHIPS):
        parts = []
        for (name, axis), shp in zip(SHARDED, shard_shapes):
            n = shp[axis]
            parts.append(lax.slice_in_dim(grads[name], k * n, (k + 1) * n, axis=axis))
        slabs.append(_pack(parts, rows))
    d = jnp.stack(slabs, axis=0)
    chip_sum = _add_slabs(d, _exchange_halves(d), "add_sibling")
    others = _scatter_chips(chip_sum)
    me = 2 * lax.axis_index("x") + lax.axis_index("y")
    mine = lax.dynamic_index_in_dim(chip_sum, me, axis=0, keepdims=True)
    reduced = _join_halves(_sum_leading(jnp.concatenate([mine, others], axis=0), "add_chips"))
    g_shards = dict(zip([k for k, _ in SHARDED], _unpack(reduced, shard_shapes)))

    small_shapes = [wts[k].shape for k in SMALL]
    small = _pack_small([grads[k] for k in SMALL] + [loss_part])
    small_sum = _sum_leading(_gather_small(small), "add_small")
    small_vals = _unpack_small(small_sum, small_shapes + [(1, LANES)])
    g_small = dict(zip(SMALL, small_vals[:-1]))
    loss = small_vals[-1][0, 0]

    g_all = {**g_shards, **g_small}
    d_s, m_s, v_s = _adamw(_pack_small([wts[k] for k in SMALL]), _pack_small([g_small[k] for k in SMALL]),
                           _pack_small([ms[k] for k in SMALL]), _pack_small([vs[k] for k in SMALL]), "adamw_small")
    upd = {}
    for k, dv, mv, vv in zip(SMALL, _unpack_small(d_s, small_shapes), _unpack_small(m_s, small_shapes),
                             _unpack_small(v_s, small_shapes)):
        upd[k] = (dv, mv, vv)
    for k, _ in SHARDED:
        upd[k] = _adamw(wts[k], g_shards[k], ms[k], vs[k], f"adamw_{k}")
    return (loss, grad_x[None], *[g_all[k] for k in WEIGHTS], *[upd[k][0] for k in WEIGHTS],
            *[upd[k][1] for k in WEIGHTS], *[upd[k][2] for k in WEIGHTS])


def kernel(x, pre_mix_norm, post_mix_norm, pre_ffn_norm, post_ffn_norm, w_in, w_out, attn_sinks, ssd_conv_w, ssd_conv_b, ssd_dt_bias, ssd_A_log, ssd_D, ssd_norm_w, gdn_conv_w, gdn_dt_bias, gdn_A_log, gdn_norm_w, ffn_w_gate, ffn_w_up, ffn_w_down, loss_target, m_pre_mix_norm, m_post_mix_norm, m_pre_ffn_norm, m_post_ffn_norm, m_w_in, m_w_out, m_attn_sinks, m_ssd_conv_w, m_ssd_conv_b, m_ssd_dt_bias, m_ssd_A_log, m_ssd_D, m_ssd_norm_w, m_gdn_conv_w, m_gdn_dt_bias, m_gdn_A_log, m_gdn_norm_w, m_ffn_w_gate, m_ffn_w_up, m_ffn_w_down, v_pre_mix_norm, v_post_mix_norm, v_pre_ffn_norm, v_post_ffn_norm, v_w_in, v_w_out, v_attn_sinks, v_ssd_conv_w, v_ssd_conv_b, v_ssd_dt_bias, v_ssd_A_log, v_ssd_D, v_ssd_norm_w, v_gdn_conv_w, v_gdn_dt_bias, v_gdn_A_log, v_gdn_norm_w, v_ffn_w_gate, v_ffn_w_up, v_ffn_w_down):
    wts = dict(zip(WEIGHTS, (pre_mix_norm, post_mix_norm, pre_ffn_norm, post_ffn_norm, w_in, w_out, attn_sinks, ssd_conv_w, ssd_conv_b, ssd_dt_bias, ssd_A_log, ssd_D, ssd_norm_w, gdn_conv_w, gdn_dt_bias, gdn_A_log, gdn_norm_w, ffn_w_gate, ffn_w_up, ffn_w_down)))
    ms = dict(zip(WEIGHTS, (m_pre_mix_norm, m_post_mix_norm, m_pre_ffn_norm, m_post_ffn_norm, m_w_in, m_w_out, m_attn_sinks, m_ssd_conv_w, m_ssd_conv_b, m_ssd_dt_bias, m_ssd_A_log, m_ssd_D, m_ssd_norm_w, m_gdn_conv_w, m_gdn_dt_bias, m_gdn_A_log, m_gdn_norm_w, m_ffn_w_gate, m_ffn_w_up, m_ffn_w_down)))
    vs = dict(zip(WEIGHTS, (v_pre_mix_norm, v_post_mix_norm, v_pre_ffn_norm, v_post_ffn_norm, v_w_in, v_w_out, v_attn_sinks, v_ssd_conv_w, v_ssd_conv_b, v_ssd_dt_bias, v_ssd_A_log, v_ssd_D, v_ssd_norm_w, v_gdn_conv_w, v_gdn_dt_bias, v_gdn_A_log, v_gdn_norm_w, v_ffn_w_gate, v_ffn_w_up, v_ffn_w_down)))
    return _step(x, loss_target, wts, ms, vs)
```

```python
import functools
import math

import jax
import jax.numpy as jnp
from jax import lax
from jax.experimental import pallas as pl
from jax.experimental.pallas import tpu as pltpu

F32, BF16 = jnp.float32, jnp.bfloat16
HI = lax.Precision.HIGHEST
MESH = pl.DeviceIdType.MESH
S = jax.ShapeDtypeStruct

D_MODEL = 1024
DEPTH = 2
CHUNK = 64
EPS = 1e-6
FF = 2816
N_CHIPS = 4
N_DEV = 8
LANES = 128

VMEM_LIMIT_BYTES = 56 * 1024 * 1024

PC_GQKV, PC_GZ, PC_XBC, PC_ATT, PC_SZ, PC_DT, PC_BA, PC_TOT = 0, 768, 1024, 2048, 2560, 3072, 3200, 3328

ADAM_LR, ADAM_B1, ADAM_B2, ADAM_EPS, ADAM_WD, ADAM_STEP = 0.001, 0.9, 0.999, 1e-08, 0.01, 10

ALIBI_SLOPES = tuple(2.0 ** (-8.0 * (h + 1) / 4) for h in range(4))


def _params(sem=None, **kw):
    if sem is not None:
        kw["dimension_semantics"] = sem
    return pltpu.CompilerParams(vmem_limit_bytes=VMEM_LIMIT_BYTES, **kw)


def _dot(a, b, prec=None):
    return jnp.dot(a, b, precision=prec, preferred_element_type=F32)


def _dot_nt(a, b, prec=None):
    return lax.dot_general(a, b, (((1,), (1,)), ((), ())), precision=prec, preferred_element_type=F32)


def _dot_tn(a, b, prec=None):
    return lax.dot_general(a, b, (((0,), (0,)), ((), ())), precision=prec, preferred_element_type=F32)


def _iota2(n, m):
    return lax.broadcasted_iota(jnp.int32, (n, m), 0), lax.broadcasted_iota(jnp.int32, (n, m), 1)


def _pick_col(arr, idx):
    ci = lax.broadcasted_iota(jnp.int32, arr.shape, 1)
    return jnp.sum(jnp.where(ci == idx, arr, 0.0), axis=1, keepdims=True)


def _pick_row(arr, idx):
    ri = lax.broadcasted_iota(jnp.int32, arr.shape, 0)
    return jnp.sum(jnp.where(ri == idx, arr, 0.0), axis=0, keepdims=True)


def _col_to_row(col, eye):
    return jnp.sum(eye * col, axis=0, keepdims=True)


def _rms(x, w):
    return x * lax.rsqrt(jnp.mean(x * x, axis=-1, keepdims=True) + EPS) * w


def _mm_nn(a, b, tm, tn, out_dtype, name):
    m, k = a.shape
    n = b.shape[1]
    tm, tn = min(tm, m), min(tn, n)

    def body(a_ref, b_ref, o_ref):
        o_ref[...] = _dot(a_ref[...], b_ref[...]).astype(o_ref.dtype)

    return pl.pallas_call(
        body, grid=(n // tn, m // tm), name=name,
        in_specs=[pl.BlockSpec((tm, k), lambda j, i: (i, 0)), pl.BlockSpec((k, tn), lambda j, i: (0, j))],
        out_specs=pl.BlockSpec((tm, tn), lambda j, i: (i, j)),
        out_shape=S((m, n), out_dtype), compiler_params=_params(("arbitrary", "arbitrary")),
    )(a, b)


def _mm_nt(a, b, tm, tn, out_dtype, name):
    m, k = a.shape
    n = b.shape[0]
    tm, tn = min(tm, m), min(tn, n)

    def body(a_ref, b_ref, o_ref):
        o_ref[...] = _dot_nt(a_ref[...], b_ref[...]).astype(o_ref.dtype)

    return pl.pallas_call(
        body, grid=(n // tn, m // tm), name=name,
        in_specs=[pl.BlockSpec((tm, k), lambda j, i: (i, 0)), pl.BlockSpec((tn, k), lambda j, i: (j, 0))],
        out_specs=pl.BlockSpec((tm, tn), lambda j, i: (i, j)),
        out_shape=S((m, n), out_dtype), compiler_params=_params(("arbitrary", "arbitrary")),
    )(a, b)


def _mm_tn(a, b, tm, tn, tk, name):
    t, m = a.shape
    n = b.shape[1]
    tm, tn, tk = min(tm, m), min(tn, n), min(tk, t)

    def body(a_ref, b_ref, o_ref):
        part = _dot_tn(a_ref[...], b_ref[...])

        @pl.when(pl.program_id(2) == 0)
        def _():
            o_ref[...] = part

        @pl.when(pl.program_id(2) > 0)
        def _():
            o_ref[...] += part

    return pl.pallas_call(
        body, grid=(m // tm, n // tn, t // tk), name=name,
        in_specs=[pl.BlockSpec((tk, tm), lambda i, j, k: (k, i)), pl.BlockSpec((tk, tn), lambda i, j, k: (k, j))],
        out_specs=pl.BlockSpec((tm, tn), lambda i, j, k: (i, j)),
        out_shape=S((m, n), F32), compiler_params=_params(("arbitrary", "arbitrary", "arbitrary")),
    )(a, b)


def _rowcall(fn, rows, params, row_outs, acc_outs, name, tm=512):
    t = rows[0].shape[0]
    tm = min(tm, t)
    n_in = len(rows) + len(params)
    n_ro = len(row_outs)

    def body(*refs):
        ro, ao = fn(*[r[...] for r in refs[:n_in]])
        for ref, v in zip(refs[n_in:n_in + n_ro], ro):
            ref[...] = v.astype(ref.dtype)
        acc_refs = refs[n_in + n_ro:]
        if acc_refs:
            @pl.when(pl.program_id(0) == 0)
            def _():
                for ref, v in zip(acc_refs, ao):
                    ref[...] = v

            @pl.when(pl.program_id(0) > 0)
            def _():
                for ref, v in zip(acc_refs, ao):
                    ref[...] += v

    in_specs = [pl.BlockSpec((tm, r.shape[1]), lambda i: (i, 0)) for r in rows]
    in_specs += [pl.BlockSpec(p.shape, lambda i: (0, 0)) for p in params]
    out_specs = [pl.BlockSpec((tm, c), lambda i: (i, 0)) for c, _ in row_outs]
    out_specs += [pl.BlockSpec(shape, lambda i: (0, 0)) for shape in acc_outs]
    out_shape = [S((t, c), dt) for c, dt in row_outs] + [S(shape, F32) for shape in acc_outs]
    return pl.pallas_call(
        body, grid=(t // tm,), name=name, in_specs=in_specs, out_specs=out_specs, out_shape=out_shape,
        compiler_params=_params(("arbitrary",)),
    )(*rows, *params)


def _prenorm(x, w, name):
    def fn(x, w):
        return (_rms(x, w),), ()
    return _rowcall(fn, [x], [w], [(D_MODEL, BF16)], [], name)[0]


def _resid_norm(xin, m, w_post, w_next, name):
    def fn(xin, m, w_post, w_next):
        xo = xin + _rms(m, w_post)
        return (xo, _rms(xo, w_next)), ()
    return _rowcall(fn, [xin, m], [w_post, w_next], [(D_MODEL, F32), (D_MODEL, BF16)], [], name)


def _resid_loss(xin, m, w_post, target, name):
    def fn(xin, m, target, w_post):
        r, vjp = jax.vjp(_rms, m, w_post)
        err = xin + r - target
        dy = err * (1.0 / D_MODEL)
        dm, dw = vjp(dy)
        tot = jnp.sum(jnp.sum(err * err, axis=1, keepdims=True), axis=0, keepdims=True) * (0.5 / D_MODEL)
        lane = lax.broadcasted_iota(jnp.int32, (1, LANES), 1)
        return (dy, dm), (jnp.where(lane == 0, tot, 0.0), dw)
    return _rowcall(fn, [xin, m, target], [w_post], [(D_MODEL, F32), (D_MODEL, BF16)],
                    [(1, LANES), (1, D_MODEL)], name)


def _resid_norm_bwd(x_out, m, d_direct, dh, w_post, w_next, name):
    def fn(x_out, m, d_direct, dh, w_post, w_next):
        _, vjp_n = jax.vjp(_rms, x_out, w_next)
        dx, dwn = vjp_n(dh)
        d_total = d_direct + dx
        _, vjp_p = jax.vjp(_rms, m, w_post)
        dm, dwp = vjp_p(d_total)
        return (d_total, dm), (dwn, dwp)
    return _rowcall(fn, [x_out, m, d_direct, dh], [w_post, w_next], [(D_MODEL, F32), (D_MODEL, BF16)],
                    [(1, D_MODEL), (1, D_MODEL)], name)


def _prenorm_bwd(x, d_direct, dh, w, name):
    def fn(x, d_direct, dh, w):
        _, vjp = jax.vjp(_rms, x, w)
        dx, dw = vjp(dh)
        return (d_direct + dx,), (dw,)
    return _rowcall(fn, [x, d_direct, dh], [w], [(D_MODEL, F32)], [(1, D_MODEL)], name)


def _swiglu_fn(gu):
    return jax.nn.silu(gu[:, :FF]) * gu[:, FF:]


def _swiglu(gu, name):
    def fn(gu):
        return (_swiglu_fn(gu),), ()
    return _rowcall(fn, [gu], [], [(FF, BF16)], [], name, tm=256)[0]


def _swiglu_bwd(gu, da, name):
    def fn(gu, da):
        _, vjp = jax.vjp(_swiglu_fn, gu)
        return (vjp(da)[0],), ()
    return _rowcall(fn, [gu, da], [], [(2 * FF, BF16)], [], name, tm=256)[0]


def _conv_fwd(proj, col0, width, w, b, name, tm=512):
    t = proj.shape[0]
    tm = min(tm, t)
    cb = col0 // width

    def body(x_ref, w_ref, b_ref, o_ref, ext):
        @pl.when(pl.program_id(0) == 0)
        def _():
            ext[0:8, :] = jnp.zeros((8, width), F32)

        ext[8:8 + tm, :] = x_ref[...]
        y = b_ref[...] + w_ref[0:1, :] * ext[pl.ds(5, tm), :]
        for k in range(1, 4):
            y = y + w_ref[k:k + 1, :] * ext[pl.ds(5 + k, tm), :]
        o_ref[...] = jax.nn.silu(y)
        ext[0:8, :] = ext[tm:tm + 8, :]

    return pl.pallas_call(
        body, grid=(t // tm,), name=name,
        in_specs=[pl.BlockSpec((tm, width), lambda i: (i, cb)), pl.BlockSpec((4, width), lambda i: (0, 0)),
                  pl.BlockSpec((1, width), lambda i: (0, 0))],
        out_specs=pl.BlockSpec((tm, width), lambda i: (i, 0)),
        out_shape=S((t, width), F32), scratch_shapes=[pltpu.VMEM((tm + 8, width), F32)],
        compiler_params=_params(("arbitrary",)),
    )(proj, w, b)


def _conv_bwd(proj, col0, width, w, b, dact, name, tm=512):
    t = proj.shape[0]
    tm = min(tm, t)
    nb = t // tm
    cb = col0 // width
    hb = tm // 8

    def body(x_ref, halo_ref, d_ref, w_ref, b_ref, dx_ref, dw_ref, db_ref, extx, extd):
        i = pl.program_id(0)
        blk = nb - 1 - i

        @pl.when(i == 0)
        def _():
            extd[tm:tm + 8, :] = jnp.zeros((8, width), F32)
            dw_ref[...] = jnp.zeros((4, width), F32)
            db_ref[...] = jnp.zeros((1, width), F32)

        extx[0:8, :] = jnp.where(blk == 0, 0.0, halo_ref[...])
        extx[8:8 + tm, :] = x_ref[...]
        y = b_ref[...] + w_ref[0:1, :] * extx[pl.ds(5, tm), :]
        for k in range(1, 4):
            y = y + w_ref[k:k + 1, :] * extx[pl.ds(5 + k, tm), :]
        sig = jax.nn.sigmoid(y)
        dy = d_ref[...] * (sig * (1.0 + y * (1.0 - sig)))
        extd[0:tm, :] = dy
        dx = w_ref[0:1, :] * extd[pl.ds(3, tm), :]
        for k in range(1, 4):
            dx = dx + w_ref[k:k + 1, :] * extd[pl.ds(3 - k, tm), :]
        dx_ref[...] = dx.astype(dx_ref.dtype)
        for k in range(4):
            dw_ref[k:k + 1, :] += jnp.sum(dy * extx[pl.ds(5 + k, tm), :], axis=0, keepdims=True)
        db_ref[...] += jnp.sum(dy, axis=0, keepdims=True)
        extd[tm:tm + 8, :] = extd[0:8, :]

    return pl.pallas_call(
        body, grid=(nb,), name=name,
        in_specs=[pl.BlockSpec((tm, width), lambda i: (nb - 1 - i, cb)),
                  pl.BlockSpec((8, width), lambda i: (jnp.maximum((nb - 1 - i) * hb - 1, 0), cb)),
                  pl.BlockSpec((tm, width), lambda i: (nb - 1 - i, 0)),
                  pl.BlockSpec((4, width), lambda i: (0, 0)), pl.BlockSpec((1, width), lambda i: (0, 0))],
        out_specs=[pl.BlockSpec((tm, width), lambda i: (nb - 1 - i, 0)), pl.BlockSpec((4, width), lambda i: (0, 0)),
                   pl.BlockSpec((1, width), lambda i: (0, 0))],
        out_shape=[S((t, width), BF16), S((4, width), F32), S((1, width), F32)],
        scratch_shapes=[pltpu.VMEM((tm + 8, width), F32), pltpu.VMEM((tm + 8, width), F32)],
        compiler_params=_params(("arbitrary",)),
    )(proj, proj, dact, w, b)


SWA_BQ = 256
SWA_BACK = 128


def _swa_block(q, kw, vw, sinks, blk):
    nq, nk = SWA_BQ, SWA_BQ + SWA_BACK
    r, j = _iota2(nq, nk)
    rel = r // CHUNK + 2 - j // CHUNK
    valid = (rel >= 0) & (rel <= 2) & (blk * (SWA_BQ // CHUNK) + j // CHUNK - 2 >= 0)
    dist = jnp.abs(r + SWA_BACK - j).astype(F32)
    outs = []
    for h in range(4):
        kv = h // 2
        qh = q[:, 64 * h:64 * h + 64]
        kh = kw[:, 64 * kv:64 * kv + 64]
        vh = vw[:, 64 * kv:64 * kv + 64]
        s = _dot_nt(qh, kh) * 0.125 - ALIBI_SLOPES[h] * dist
        s = jnp.where(valid, s, -1e30)
        sink = _pick_col(sinks, h)
        m = jnp.maximum(jnp.max(s, axis=1, keepdims=True), sink)
        e = jnp.exp(s - m)
        den = jnp.sum(e, axis=1, keepdims=True) + jnp.exp(sink - m)
        outs.append(_dot(e / den, vh))
    return jnp.concatenate(outs, axis=1)


def _swa_fwd(proj, sinks, name):
    t = proj.shape[0]
    qb, kb = PC_ATT // 256, PC_ATT // 128 + 2
    win = SWA_BQ + SWA_BACK

    def body(q_ref, k_ref, v_ref, s_ref, o_ref, kp, vp):
        i = pl.program_id(0)

        @pl.when(i == 0)
        def _():
            kp[0:SWA_BACK, :] = jnp.zeros((SWA_BACK, 128), F32)
            vp[0:SWA_BACK, :] = jnp.zeros((SWA_BACK, 128), F32)
            kp[SWA_BACK:, :] = k_ref[...]
            vp[SWA_BACK:, :] = v_ref[...]

        start = pl.multiple_of(i * SWA_BQ, SWA_BQ)
        o = _swa_block(q_ref[...], kp[pl.ds(start, win), :], vp[pl.ds(start, win), :], s_ref[...], i)
        o_ref[...] = o.astype(o_ref.dtype)

    return pl.pallas_call(
        body, grid=(t // SWA_BQ,), name=name,
        in_specs=[pl.BlockSpec((SWA_BQ, 256), lambda i: (i, qb)), pl.BlockSpec((t, 128), lambda i: (0, kb)),
                  pl.BlockSpec((t, 128), lambda i: (0, kb + 1)), pl.BlockSpec((1, LANES), lambda i: (0, 0))],
        out_specs=pl.BlockSpec((SWA_BQ, 256), lambda i: (i, 0)),
        out_shape=S((t, 256), BF16),
        scratch_shapes=[pltpu.VMEM((t + SWA_BACK, 128), F32), pltpu.VMEM((t + SWA_BACK, 128), F32)],
        compiler_params=_params(("arbitrary",)),
    )(proj, proj, proj, sinks)


def _swa_bwd(proj, sinks, dcat, dcol0, name):
    t = proj.shape[0]
    nb = t // SWA_BQ
    qb, kb = PC_ATT // 256, PC_ATT // 128 + 2
    db = dcol0 // 256
    win = SWA_BQ + SWA_BACK

    def body(q_ref, k_ref, v_ref, s_ref, do_ref, dq_ref, dk_ref, dv_ref, ds_ref, kp, vp, dkp, dvp):
        i = pl.program_id(0)

        @pl.when(i == 0)
        def _():
            kp[0:SWA_BACK, :] = jnp.zeros((SWA_BACK, 128), F32)
            vp[0:SWA_BACK, :] = jnp.zeros((SWA_BACK, 128), F32)
            kp[SWA_BACK:, :] = k_ref[...]
            vp[SWA_BACK:, :] = v_ref[...]
            dkp[...] = jnp.zeros_like(dkp)
            dvp[...] = jnp.zeros_like(dvp)
            ds_ref[...] = jnp.zeros_like(ds_ref)

        start = pl.multiple_of(i * SWA_BQ, SWA_BQ)
        _, vjp = jax.vjp(functools.partial(_swa_block, blk=i), q_ref[...], kp[pl.ds(start, win), :],
                         vp[pl.ds(start, win), :], s_ref[...])
        dq, dkw, dvw, dsk = vjp(do_ref[...])
        dq_ref[...] = dq.astype(dq_ref.dtype)
        dkp[pl.ds(start, win), :] += dkw
        dvp[pl.ds(start, win), :] += dvw
        ds_ref[...] += dsk

        @pl.when(i == nb - 1)
        def _():
            dk_ref[...] = dkp[SWA_BACK:, :].astype(dk_ref.dtype)
            dv_ref[...] = dvp[SWA_BACK:, :].astype(dv_ref.dtype)

    return pl.pallas_call(
        body, grid=(nb,), name=name,
        in_specs=[pl.BlockSpec((SWA_BQ, 256), lambda i: (i, qb)), pl.BlockSpec((t, 128), lambda i: (0, kb)),
                  pl.BlockSpec((t, 128), lambda i: (0, kb + 1)), pl.BlockSpec((1, LANES), lambda i: (0, 0)),
                  pl.BlockSpec((SWA_BQ, 256), lambda i: (i, db))],
        out_specs=[pl.BlockSpec((SWA_BQ, 256), lambda i: (i, 0)), pl.BlockSpec((t, 128), lambda i: (0, 0)),
                   pl.BlockSpec((t, 128), lambda i: (0, 0)), pl.BlockSpec((1, LANES), lambda i: (0, 0))],
        out_shape=[S((t, 256), BF16), S((t, 128), BF16), S((t, 128), BF16), S((1, LANES), F32)],
        scratch_shapes=[pltpu.VMEM((t + SWA_BACK, 128), F32) for _ in range(4)],
        compiler_params=_params(("arbitrary",)),
    )(proj, proj, proj, sinks, dcat)


def _ssd_chunk(z, xbc, dt_raw, state, dtb, alog, dsk, nw):
    n = CHUNK
    r, c = _iota2(n, n)
    tril = r >= c
    eye = (r == c).astype(F32)
    dt = jax.nn.softplus(dt_raw + dtb)
    acs = _dot(tril.astype(F32), dt * (-jnp.exp(alog)), HI)
    xs, bm, cm = xbc[:, :512], xbc[:, 512:768], xbc[:, 768:1024]
    heads = range(8)
    bg = [bm[:, 128 * g:128 * g + 128] for g in range(2)]
    cg = [cm[:, 128 * g:128 * g + 128] for g in range(2)]
    cb = [_dot_nt(cg[g], bg[g]) for g in range(2)]
    dth = [_pick_col(dt, h) for h in heads]
    acol = [_pick_col(acs, h) for h in heads]
    arow = [_col_to_row(a, eye) for a in acol]
    lmat = [jnp.where(tril, jnp.exp(jnp.where(tril, a - b, 0.0)), 0.0) for a, b in zip(acol, arow)]
    xh = [xs[:, 64 * h:64 * h + 64] for h in heads]
    xc = [x * t for x, t in zip(xh, dth)]
    st = [state[64 * h:64 * h + 64, :] for h in heads]
    alast = [_pick_row(a, n - 1) for a in acol]
    y_in = [_dot(cb[h // 4] * lmat[h], xc[h]) for h in heads]
    y_st = [_dot_nt(cg[h // 4], st[h]) * jnp.exp(acol[h]) for h in heads]
    ys = [y_in[h] + y_st[h] + xh[h] * _pick_col(dsk, h) for h in heads]
    new_states = [st[h] * jnp.exp(alast[h]) + _dot_tn(xc[h] * jnp.exp(alast[h] - acol[h]), bg[h // 4]) for h in heads]
    gg = jnp.concatenate(ys, axis=1) * jax.nn.silu(z)
    outs = []
    for gi in range(2):
        gv = gg[:, 256 * gi:256 * gi + 256]
        outs.append(gv * lax.rsqrt(jnp.mean(gv * gv, axis=-1, keepdims=True) + EPS))
    return jnp.concatenate(outs, axis=1) * nw, jnp.concatenate(new_states, axis=0)


def _ssd_fwd(proj, xbc, dtb, alog, dsk, nw, name):
    t = proj.shape[0]
    nc = t // CHUNK

    def body(z_ref, x_ref, dt_ref, dtb_ref, al_ref, d_ref, nw_ref, o_ref, st_ref, state):
        @pl.when(pl.program_id(0) == 0)
        def _():
            state[...] = jnp.zeros_like(state)

        st_ref[0] = state[...]
        o, ns = _ssd_chunk(z_ref[...], x_ref[...], dt_ref[...], state[...], dtb_ref[...], al_ref[...], d_ref[...],
                           nw_ref[...])
        o_ref[...] = o.astype(o_ref.dtype)
        state[...] = ns

    vec = pl.BlockSpec((1, LANES), lambda i: (0, 0))
    return pl.pallas_call(
        body, grid=(nc,), name=name,
        in_specs=[pl.BlockSpec((CHUNK, 512), lambda i: (i, PC_SZ // 512)), pl.BlockSpec((CHUNK, 1024), lambda i: (i, 0)),
                  pl.BlockSpec((CHUNK, 128), lambda i: (i, PC_DT // 128)), vec, vec, vec,
                  pl.BlockSpec((1, 512), lambda i: (0, 0))],
        out_specs=[pl.BlockSpec((CHUNK, 512), lambda i: (i, 0)), pl.BlockSpec((1, 512, 128), lambda i: (i, 0, 0))],
        out_shape=[S((t, 512), BF16), S((nc, 512, 128), F32)],
        scratch_shapes=[pltpu.VMEM((512, 128), F32)],
        compiler_params=_params(("arbitrary",)),
    )(proj, xbc, proj, dtb, alog, dsk, nw)


def _ssd_bwd(proj, xbc, states, dtb, alog, dsk, nw, dcat, dcol0, name):
    t = proj.shape[0]
    nc = t // CHUNK
    db = dcol0 // 512

    def body(z_ref, x_ref, dt_ref, st_ref, dtb_ref, al_ref, d_ref, nw_ref, do_ref,
             dz_ref, dx_ref, ddt_ref, gdtb_ref, gal_ref, gd_ref, gnw_ref, dstate):
        @pl.when(pl.program_id(0) == 0)
        def _():
            dstate[...] = jnp.zeros_like(dstate)
            gdtb_ref[...] = jnp.zeros_like(gdtb_ref)
            gal_ref[...] = jnp.zeros_like(gal_ref)
            gd_ref[...] = jnp.zeros_like(gd_ref)
            gnw_ref[...] = jnp.zeros_like(gnw_ref)

        _, vjp = jax.vjp(_ssd_chunk, z_ref[...], x_ref[...], dt_ref[...], st_ref[0], dtb_ref[...], al_ref[...],
                         d_ref[...], nw_ref[...])
        dz, dx, ddt, dst, gdtb, gal, gd, gnw = vjp((do_ref[...], dstate[...]))
        dz_ref[...] = dz.astype(dz_ref.dtype)
        dx_ref[...] = dx
        ddt_ref[...] = ddt.astype(ddt_ref.dtype)
        dstate[...] = dst
        gdtb_ref[...] += gdtb
        gal_ref[...] += gal
        gd_ref[...] += gd
        gnw_ref[...] += gnw

    rev = lambda i: nc - 1 - i
    vec = pl.BlockSpec((1, LANES), lambda i: (0, 0))
    vec512 = pl.BlockSpec((1, 512), lambda i: (0, 0))
    return pl.pallas_call(
        body, grid=(nc,), name=name,
        in_specs=[pl.BlockSpec((CHUNK, 512), lambda i: (rev(i), PC_SZ // 512)),
                  pl.BlockSpec((CHUNK, 1024), lambda i: (rev(i), 0)),
                  pl.BlockSpec((CHUNK, 128), lambda i: (rev(i), PC_DT // 128)),
                  pl.BlockSpec((1, 512, 128), lambda i: (rev(i), 0, 0)), vec, vec, vec, vec512,
                  pl.BlockSpec((CHUNK, 512), lambda i: (rev(i), db))],
        out_specs=[pl.BlockSpec((CHUNK, 512), lambda i: (rev(i), 0)), pl.BlockSpec((CHUNK, 1024), lambda i: (rev(i), 0)),
                   pl.BlockSpec((CHUNK, 128), lambda i: (rev(i), 0)), vec, vec, vec, vec512],
        out_shape=[S((t, 512), BF16), S((t, 1024), F32), S((t, 128), BF16), S((1, LANES), F32), S((1, LANES), F32),
                   S((1, LANES), F32), S((1, 512), F32)],
        scratch_shapes=[pltpu.VMEM((512, 128), F32)],
        compiler_params=_params(("arbitrary",)),
    )(proj, xbc, proj, states, dtb, alog, dsk, nw, dcat)


SOLVE_PREC = lax.Precision.HIGH


@jax.custom_vjp
def _unit_lower_inverses(nas):
    n = nas[0].shape[0]
    r, c = _iota2(n, n)
    eye = (r == c).astype(F32)
    tm, pw = [eye + a for a in nas], list(nas)
    for _ in range(5):
        pw = [_dot(p, p, SOLVE_PREC) for p in pw]
        tm = [t + _dot(t, p, SOLVE_PREC) for t, p in zip(tm, pw)]
    return tuple(tm)


def _unit_lower_inverses_fwd(nas):
    ts = _unit_lower_inverses(nas)
    return ts, ts


def _unit_lower_inverses_bwd(ts, gs):
    part = [_dot_nt(g, t, SOLVE_PREC) for g, t in zip(gs, ts)]
    return (tuple(_dot_tn(t, p, SOLVE_PREC) for t, p in zip(ts, part)),)


_unit_lower_inverses.defvjp(_unit_lower_inverses_fwd, _unit_lower_inverses_bwd)


def _gdn_chunk(qkv, z, ba, state, dtb, alog, nw):
    n = CHUNK
    r, c = _iota2(n, n)
    tril = r >= c
    stril = r > c
    eye = (r == c).astype(F32)
    beta_all = jax.nn.sigmoid(ba)
    gcs = _dot(tril.astype(F32), -jnp.exp(alog) * jax.nn.softplus(ba + dtb), HI)
    heads = range(4)
    qh = [qkv[:, 64 * h:64 * h + 64] for h in heads]
    kh = [qkv[:, 256 + 64 * h:256 + 64 * h + 64] for h in heads]
    vh = [qkv[:, 512 + 64 * h:512 + 64 * h + 64] for h in heads]
    qn = [q * lax.rsqrt(jnp.sum(q * q, axis=-1, keepdims=True) + EPS) * 0.125 for q in qh]
    kn = [k * lax.rsqrt(jnp.sum(k * k, axis=-1, keepdims=True) + EPS) for k in kh]
    beta = [_pick_col(beta_all, h) for h in heads]
    gcol = [_pick_col(gcs, 4 + h) for h in heads]
    grow = [_col_to_row(g, eye) for g in gcol]
    decay = [jnp.where(tril, jnp.exp(jnp.where(tril, gc - gr, 0.0)), 0.0) for gc, gr in zip(gcol, grow)]
    kbeta = [k * b for k, b in zip(kn, beta)]
    kk = [_dot_nt(kb, k) for kb, k in zip(kbeta, kn)]
    qk = [_dot_nt(q, k) * dc for q, k, dc in zip(qn, kn, decay)]
    tms = _unit_lower_inverses(tuple(-jnp.where(stril, x * dc, 0.0) for x, dc in zip(kk, decay)))
    rhs = [jnp.concatenate([v * b, kb * jnp.exp(g)], axis=1) for v, b, kb, g in zip(vh, beta, kbeta, gcol)]
    sol = [_dot(t, x, SOLVE_PREC) for t, x in zip(tms, rhs)]
    st = [state[64 * h:64 * h + 64, :] for h in heads]
    v_new = [s_[:, :64] - _dot(s_[:, 64:], s) for s_, s in zip(sol, st)]
    o = [_dot(q * jnp.exp(g), s) + _dot(x, vn) for q, g, s, x, vn in zip(qn, gcol, st, qk, v_new)]
    glast = [_pick_row(g, n - 1) for g in gcol]
    new_states = [s * jnp.exp(gl) + _dot_tn(k * jnp.exp(gl - g), vn)
                  for s, gl, k, g, vn in zip(st, glast, kn, gcol, v_new)]
    o = [x * lax.rsqrt(jnp.mean(x * x, axis=-1, keepdims=True) + EPS) * nw for x in o]
    outs = [x * jax.nn.silu(z[:, 64 * h:64 * h + 64]) for h, x in zip(heads, o)]
    return jnp.concatenate(outs, axis=1), jnp.concatenate(new_states, axis=0)


def _gdn_fwd(proj, qkv, dtb, alog, nw, name):
    t = proj.shape[0]
    nc = t // CHUNK

    def body(q_ref, z_ref, ba_ref, dtb_ref, al_ref, nw_ref, o_ref, st_ref, state):
        @pl.when(pl.program_id(0) == 0)
        def _():
            state[...] = jnp.zeros_like(state)

        st_ref[0] = state[...]
        o, ns = _gdn_chunk(q_ref[...], z_ref[...], ba_ref[...], state[...], dtb_ref[...], al_ref[...], nw_ref[...])
        o_ref[...] = o.astype(o_ref.dtype)
        state[...] = ns

    vec = pl.BlockSpec((1, LANES), lambda i: (0, 0))
    return pl.pallas_call(
        body, grid=(nc,), name=name,
        in_specs=[pl.BlockSpec((CHUNK, 768), lambda i: (i, 0)), pl.BlockSpec((CHUNK, 256), lambda i: (i, PC_GZ // 256)),
                  pl.BlockSpec((CHUNK, 128), lambda i: (i, PC_BA // 128)), vec, vec, pl.BlockSpec((1, 64), lambda i: (0, 0))],
        out_specs=[pl.BlockSpec((CHUNK, 256), lambda i: (i, 0)), pl.BlockSpec((1, 256, 64), lambda i: (i, 0, 0))],
        out_shape=[S((t, 256), BF16), S((nc, 256, 64), F32)],
        scratch_shapes=[pltpu.VMEM((256, 64), F32)],
        compiler_params=_params(("arbitrary",)),
    )(qkv, proj, proj, dtb, alog, nw)


def _gdn_bwd(proj, qkv, states, dtb, alog, nw, dcat, dcol0, name):
    t = proj.shape[0]
    nc = t // CHUNK
    db = dcol0 // 256

    def body(q_ref, z_ref, ba_ref, st_ref, dtb_ref, al_ref, nw_ref, do_ref,
             dq_ref, dz_ref, dba_ref, gdtb_ref, gal_ref, gnw_ref, dstate):
        @pl.when(pl.program_id(0) == 0)
        def _():
            dstate[...] = jnp.zeros_like(dstate)
            gdtb_ref[...] = jnp.zeros_like(gdtb_ref)
            gal_ref[...] = jnp.zeros_like(gal_ref)
            gnw_ref[...] = jnp.zeros_like(gnw_ref)

        _, vjp = jax.vjp(_gdn_chunk, q_ref[...], z_ref[...], ba_ref[...], st_ref[0], dtb_ref[...], al_ref[...],
                         nw_ref[...])
        dq, dz, dba, dst, gdtb, gal, gnw = vjp((do_ref[...], dstate[...]))
        dq_ref[...] = dq
        dz_ref[...] = dz.astype(dz_ref.dtype)
        dba_ref[...] = dba.astype(dba_ref.dtype)
        dstate[...] = dst
        gdtb_ref[...] += gdtb
        gal_ref[...] += gal
        gnw_ref[...] += gnw

    rev = lambda i: nc - 1 - i
    vec = pl.BlockSpec((1, LANES), lambda i: (0, 0))
    vec64 = pl.BlockSpec((1, 64), lambda i: (0, 0))
    return pl.pallas_call(
        body, grid=(nc,), name=name,
        in_specs=[pl.BlockSpec((CHUNK, 768), lambda i: (rev(i), 0)),
                  pl.BlockSpec((CHUNK, 256), lambda i: (rev(i), PC_GZ // 256)),
                  pl.BlockSpec((CHUNK, 128), lambda i: (rev(i), PC_BA // 128)),
                  pl.BlockSpec((1, 256, 64), lambda i: (rev(i), 0, 0)), vec, vec, vec64,
                  pl.BlockSpec((CHUNK, 256), lambda i: (rev(i), db))],
        out_specs=[pl.BlockSpec((CHUNK, 768), lambda i: (rev(i), 0)), pl.BlockSpec((CHUNK, 256), lambda i: (rev(i), 0)),
                   pl.BlockSpec((CHUNK, 128), lambda i: (rev(i), 0)), vec, vec, vec64],
        out_shape=[S((t, 768), F32), S((t, 256), BF16), S((t, 128), BF16), S((1, LANES), F32), S((1, LANES), F32),
                   S((1, 64), F32)],
        scratch_shapes=[pltpu.VMEM((256, 64), F32)],
        compiler_params=_params(("arbitrary",)),
    )(qkv, proj, proj, states, dtb, alog, nw, dcat)


def _pad_cols(w):
    z = jnp.zeros((w.shape[0], 120), w.dtype)
    return jnp.concatenate([w[:, 2056:2824], w[:, 2824:3080], w[:, 1024:2048], w[:, 0:512], w[:, 512:1024],
                            w[:, 2048:2056], z, w[:, 3080:3088], z], axis=1)


def _unpad_cols(g):
    return jnp.concatenate([g[:, PC_ATT:PC_ATT + 512], g[:, PC_SZ:PC_SZ + 512], g[:, PC_XBC:PC_XBC + 1024],
                            g[:, PC_DT:PC_DT + 8], g[:, PC_GQKV:PC_GQKV + 768], g[:, PC_GZ:PC_GZ + 256],
                            g[:, PC_BA:PC_BA + 8]], axis=1)


def _vec128(v, at=0):
    return jnp.zeros((1, LANES), F32).at[0, at:at + v.shape[0]].set(v)


def _local_step(x, target, w):
    lw = []
    for l in range(DEPTH):
        w_out = w["w_out"][l]
        lw.append(dict(
            pre_mix=w["pre_mix_norm"][l][None], post_mix=w["post_mix_norm"][l][None],
            pre_ffn=w["pre_ffn_norm"][l][None], post_ffn=w["post_ffn_norm"][l][None],
            w_in=_pad_cols(w["w_in"][l]).astype(BF16),
            w_out=jnp.concatenate([w_out[256:768], w_out[0:256], w_out[768:1024]], axis=0).astype(BF16),
            w_gu=jnp.concatenate([w["ffn_w_gate"][l], w["ffn_w_up"][l]], axis=1).astype(BF16),
            w_down=w["ffn_w_down"][l].astype(BF16),
            sinks=_vec128(w["attn_sinks"][l]),
            s_cw=w["ssd_conv_w"][l], s_cb=w["ssd_conv_b"][l][None],
            s_dtb=_vec128(w["ssd_dt_bias"][l]), s_alog=_vec128(w["ssd_A_log"][l]), s_d=_vec128(w["ssd_D"][l]),
            s_nw=w["ssd_norm_w"][l][None],
            g_cw=w["gdn_conv_w"][l], g_cb=jnp.zeros((1, 768), F32),
            g_dtb=_vec128(w["gdn_dt_bias"][l], 4), g_alog=_vec128(w["gdn_A_log"][l], 4), g_nw=w["gdn_norm_w"][l][None],
        ))

    saved = []
    xin = x
    h = _prenorm(x, lw[0]["pre_mix"], "prenorm0")
    for l in range(DEPTH):
        p = lw[l]
        proj = _mm_nn(h, p["w_in"], 512, PC_TOT, F32, f"inproj{l}")
        xbc = _conv_fwd(proj, PC_XBC, 1024, p["s_cw"], p["s_cb"], f"ssd_conv{l}")
        gqkv = _conv_fwd(proj, PC_GQKV, 768, p["g_cw"], p["g_cb"], f"gdn_conv{l}")
        att = _swa_fwd(proj, p["sinks"], f"swa{l}")
        ssd, s_states = _ssd_fwd(proj, xbc, p["s_dtb"], p["s_alog"], p["s_d"], p["s_nw"], f"ssd{l}")
        gdn, g_states = _gdn_fwd(proj, gqkv, p["g_dtb"], p["g_alog"], p["g_nw"], f"gdn{l}")
        cat = jnp.concatenate([ssd, att, gdn], axis=1)
        mix = _mm_nn(cat, p["w_out"], 512, 1024, F32, f"outproj{l}")
        x1, h2 = _resid_norm(xin, mix, p["post_mix"], p["pre_ffn"], f"postmix{l}")
        gu = _mm_nn(h2, p["w_gu"], 512, FF, F32, f"ffn_gu{l}")
        act = _swiglu(gu, f"swiglu{l}")
        f = _mm_nn(act, p["w_down"], 512, 1024, F32, f"ffn_down{l}")
        saved.append(dict(xin=xin, h=h, proj=proj, xbc=xbc, gqkv=gqkv, s_states=s_states, g_states=g_states, cat=cat,
                          mix=mix, x1=x1, h2=h2, gu=gu, act=act, f=f))
        if l + 1 < DEPTH:
            xin, h = _resid_norm(x1, f, p["post_ffn"], lw[l + 1]["pre_mix"], f"postffn{l}")

    g = {k: [None] * DEPTH for k in w}
    last = saved[-1]
    d_x2, d_f, loss_part, g["post_ffn_norm"][DEPTH - 1] = _resid_loss(
        last["x1"], last["f"], lw[-1]["post_ffn"], target, "loss")
    for l in reversed(range(DEPTH)):
        p, s = lw[l], saved[l]
        d_act = _mm_nt(d_f, p["w_down"], 512, FF, F32, f"d_act{l}")
        g["ffn_w_down"][l] = _mm_tn(s["act"], d_f, 1408, 1024, 512, f"dw_down{l}")
        d_gu = _swiglu_bwd(s["gu"], d_act, f"d_swiglu{l}")
        d_h2 = _mm_nt(d_gu, p["w_gu"], 256, 1024, F32, f"d_h2{l}")
        g_gu = _mm_tn(s["h2"], d_gu, 512, FF, 512, f"dw_gu{l}")
        g["ffn_w_gate"][l], g["ffn_w_up"][l] = g_gu[:, :FF], g_gu[:, FF:]
        d_x1, d_mix, g["pre_ffn_norm"][l], g["post_mix_norm"][l] = _resid_norm_bwd(
            s["x1"], s["mix"], d_x2, d_h2, p["post_mix"], p["pre_ffn"], f"d_postmix{l}")
        d_cat = _mm_nt(d_mix, p["w_out"], 512, 1024, F32, f"d_cat{l}")
        g_out = _mm_tn(s["cat"], d_mix, 512, 1024, 512, f"dw_out{l}")
        g["w_out"][l] = jnp.concatenate([g_out[512:768], g_out[0:512], g_out[768:1024]], axis=0)
        d_q, d_k, d_v, g_sinks = _swa_bwd(s["proj"], p["sinks"], d_cat, 512, f"d_swa{l}")
        d_sz, d_xbc, d_dt, g_dtb, g_alog, g_d, g["ssd_norm_w"][l] = _ssd_bwd(
            s["proj"], s["xbc"], s["s_states"], p["s_dtb"], p["s_alog"], p["s_d"], p["s_nw"], d_cat, 0, f"d_ssd{l}")
        d_gq, d_gz, d_ba, gg_dtb, gg_alog, g["gdn_norm_w"][l] = _gdn_bwd(
            s["proj"], s["gqkv"], s["g_states"], p["g_dtb"], p["g_alog"], p["g_nw"], d_cat, 768, f"d_gdn{l}")
        d_xbc_raw, g["ssd_conv_w"][l], g["ssd_conv_b"][l] = _conv_bwd(
            s["proj"], PC_XBC, 1024, p["s_cw"], p["s_cb"], d_xbc, f"d_ssd_conv{l}")
        d_gq_raw, g["gdn_conv_w"][l], _ = _conv_bwd(s["proj"], PC_GQKV, 768, p["g_cw"], p["g_cb"], d_gq, f"d_gdn_conv{l}")
        d_proj = jnp.concatenate([d_gq_raw, d_gz, d_xbc_raw, d_q, d_k, d_v, d_sz, d_dt, d_ba], axis=1)
        d_h = _mm_nt(d_proj, p["w_in"], 512, 1024, F32, f"d_h{l}")
        g["w_in"][l] = _unpad_cols(_mm_tn(s["h"], d_proj, 512, PC_TOT // 2, 512, f"dw_in{l}"))
        g["attn_sinks"][l] = g_sinks[0, :4]
        g["ssd_dt_bias"][l], g["ssd_A_log"][l], g["ssd_D"][l] = g_dtb[0, :8], g_alog[0, :8], g_d[0, :8]
        g["gdn_dt_bias"][l], g["gdn_A_log"][l] = gg_dtb[0, 4:8], gg_alog[0, 4:8]
        if l > 0:
            sp = saved[l - 1]
            d_x2, d_f, g["pre_mix_norm"][l], g["post_ffn_norm"][l - 1] = _resid_norm_bwd(
                s["xin"], sp["f"], d_x1, d_h, lw[l - 1]["post_ffn"], p["pre_mix"], f"d_postffn{l - 1}")
        else:
            grad_x, g["pre_mix_norm"][0] = _prenorm_bwd(s["xin"], d_x1, d_h, p["pre_mix"], "d_prenorm0")

    grads = {}
    for k, parts in g.items():
        parts = [a.reshape(w[k].shape[1:]) for a in parts]
        grads[k] = jnp.stack(parts, axis=0)
    return loss_part, grad_x, grads


SHARDED = (("w_in", 2), ("w_out", 1), ("ffn_w_gate", 2), ("ffn_w_up", 2), ("ffn_w_down", 1), ("ssd_conv_w", 2),
           ("gdn_conv_w", 2))
SMALL = ("pre_mix_norm", "post_mix_norm", "pre_ffn_norm", "post_ffn_norm", "attn_sinks", "ssd_conv_b", "ssd_dt_bias",
         "ssd_A_log", "ssd_D", "ssd_norm_w", "gdn_dt_bias", "gdn_A_log", "gdn_norm_w")
SLAB_COLS = 1024


SLAB_ROW_UNIT = 256


def _slab_rows(shards):
    total = sum(math.prod(a.shape) for a in shards)
    return -(-total // (SLAB_COLS * SLAB_ROW_UNIT)) * SLAB_ROW_UNIT


def _row_tile(rows, cap):
    best = rows
    for t in range(8, min(cap, rows) + 1, 8):
        if rows % t == 0:
            best = t
    return best


def _pack(shards, rows):
    flat = [a.reshape(-1) for a in shards]
    pad = rows * SLAB_COLS - sum(a.shape[0] for a in flat)
    return jnp.concatenate(flat + [jnp.zeros((pad,), flat[0].dtype)]).reshape(rows, SLAB_COLS)


def _unpack(slab, shapes):
    flat = slab.reshape(-1)
    out, off = [], 0
    for shp in shapes:
        n = math.prod(shp)
        out.append(flat[off:off + n].reshape(shp))
        off += n
    return out


SMALL_UNIT = 8 * LANES


def _pack_small(vals):
    rows = []
    for a in vals:
        f = a.reshape(-1)
        pad = -f.shape[0] % SMALL_UNIT
        rows.append(jnp.concatenate([f, jnp.zeros((pad,), F32)]).reshape(-1, LANES))
    return jnp.concatenate(rows, axis=0)


def _unpack_small(mat, shapes):
    out, r = [], 0
    for shp in shapes:
        n = math.prod(shp)
        nr = -(-n // SMALL_UNIT) * 8
        out.append(mat[r:r + nr].reshape(-1)[:n].reshape(shp))
        r += nr
    return out


def _place():
    x, y, c = lax.axis_index("x"), lax.axis_index("y"), lax.axis_index("c")
    chips = [(1 - x, y), (x, 1 - y), (1 - x, 1 - y)]
    return x, y, c, chips


ANY = pl.BlockSpec(memory_space=pl.ANY)


def _gather_weights(slab):
    rows = slab.shape[0]
    hr = rows // 2

    def body(s_ref, g_ref, send_sems, recv_sems, local_sem):
        x, y, c, chips = _place()
        sib = (x, y, 1 - c)

        def half(chip, hc):
            return g_ref.at[chip, pl.ds(hc * hr, hr), :]

        def copy(k, src, dst, to):
            return pltpu.make_async_remote_copy(src_ref=src, dst_ref=dst, send_sem=send_sems.at[k],
                                                recv_sem=recv_sems.at[k], device_id=to, device_id_type=MESH)

        me = 2 * x + y
        mine = pltpu.make_async_copy(s_ref, g_ref.at[me], local_sem)
        mine.start()
        first = [copy(j, s_ref.at[pl.ds(c * hr, hr), :], half(me, c), (px, py, c)) for j, (px, py) in enumerate(chips)]
        for cp in first:
            cp.start()
        passed = []
        for j, (px, py) in enumerate(chips):
            src = half(2 * px + py, c)
            copy(j, src, src, (px, py, c)).wait_recv()
            fw = copy(3 + j, src, src, sib)
            fw.start()
            passed.append(fw)
        for j, (px, py) in enumerate(chips):
            dst = half(2 * px + py, 1 - c)
            copy(3 + j, dst, dst, sib).wait_recv()
        for cp in first + passed:
            cp.wait_send()
        mine.wait()

    return pl.pallas_call(
        body, name="gather_weights", in_specs=[ANY], out_specs=ANY, out_shape=S((N_CHIPS, rows, SLAB_COLS), slab.dtype),
        scratch_shapes=[pltpu.SemaphoreType.DMA((6,)), pltpu.SemaphoreType.DMA((6,)), pltpu.SemaphoreType.DMA],
    )(slab)


def _exchange_halves(d):
    rows = d.shape[1]
    hr = rows // 2

    def body(d_ref, t_ref, send_sem, recv_sem):
        x, y, c, _ = _place()
        cp = pltpu.make_async_remote_copy(
            src_ref=d_ref.at[:, pl.ds((1 - c) * hr, hr), :], dst_ref=t_ref, send_sem=send_sem, recv_sem=recv_sem,
            device_id=(x, y, 1 - c), device_id_type=MESH)
        cp.start()
        cp.wait()

    return pl.pallas_call(
        body, name="exchange_halves", in_specs=[ANY], out_specs=ANY, out_shape=S((N_CHIPS, hr, SLAB_COLS), F32),
        scratch_shapes=[pltpu.SemaphoreType.DMA, pltpu.SemaphoreType.DMA],
    )(d)


def _scatter_chips(p):
    hr = p.shape[1]

    def body(p_ref, u_ref, send_sems, recv_sems):
        x, y, c, chips = _place()
        cps = [pltpu.make_async_remote_copy(
            src_ref=p_ref.at[2 * px + py], dst_ref=u_ref.at[j], send_sem=send_sems.at[j], recv_sem=recv_sems.at[j],
            device_id=(px, py, c), device_id_type=MESH) for j, (px, py) in enumerate(chips)]
        for cp in cps:
            cp.start()
        for cp in cps:
            cp.wait()

    return pl.pallas_call(
        body, name="scatter_chips", in_specs=[ANY], out_specs=ANY, out_shape=S((3, hr, SLAB_COLS), p.dtype),
        scratch_shapes=[pltpu.SemaphoreType.DMA((3,)), pltpu.SemaphoreType.DMA((3,))],
    )(p)


def _join_halves(q):
    hr = q.shape[0]

    def body(q_ref, o_ref, send_sem, recv_sem, local_sem):
        x, y, c, _ = _place()
        mine = pltpu.make_async_copy(q_ref, o_ref.at[pl.ds(c * hr, hr), :], local_sem)
        mine.start()
        cp = pltpu.make_async_remote_copy(
            src_ref=q_ref, dst_ref=o_ref.at[pl.ds(c * hr, hr), :], send_sem=send_sem, recv_sem=recv_sem,
            device_id=(x, y, 1 - c), device_id_type=MESH)
        cp.start()
        other = o_ref.at[pl.ds((1 - c) * hr, hr), :]
        pltpu.make_async_remote_copy(src_ref=other, dst_ref=other, send_sem=send_sem, recv_sem=recv_sem,
                                     device_id=(x, y, 1 - c), device_id_type=MESH).wait_recv()
        cp.wait_send()
        mine.wait()

    return pl.pallas_call(
        body, name="join_halves", in_specs=[ANY], out_specs=ANY, out_shape=S((2 * hr, SLAB_COLS), F32),
        scratch_shapes=[pltpu.SemaphoreType.DMA, pltpu.SemaphoreType.DMA, pltpu.SemaphoreType.DMA],
    )(q)


def _gather_small(v, name):
    def body(v_ref, o_ref, send_sems, recv_sems, local_sem):
        x, y, c, _ = _place()
        me = 4 * x + 2 * y + c
        mine = pltpu.make_async_copy(v_ref, o_ref.at[me], local_sem)
        mine.start()
        cps = []
        for k in range(1, N_DEV):
            fx, fy, fc = (k >> 2) & 1, (k >> 1) & 1, k & 1
            peer = (x ^ fx, y ^ fy, c ^ fc)
            cps.append(pltpu.make_async_remote_copy(
                src_ref=v_ref, dst_ref=o_ref.at[me], send_sem=send_sems.at[k - 1], recv_sem=recv_sems.at[k - 1],
                device_id=peer, device_id_type=MESH))
        for cp in cps:
            cp.start()
        for k in range(1, N_DEV):
            fx, fy, fc = (k >> 2) & 1, (k >> 1) & 1, k & 1
            dst = o_ref.at[4 * (x ^ fx) + 2 * (y ^ fy) + (c ^ fc)]
            pltpu.make_async_remote_copy(src_ref=dst, dst_ref=dst, send_sem=send_sems.at[k - 1],
                                         recv_sem=recv_sems.at[k - 1], device_id=(x, y, c),
                                         device_id_type=MESH).wait_recv()
        for cp in cps:
            cp.wait_send()
        mine.wait()

    return pl.pallas_call(
        body, name=name, in_specs=[ANY], out_specs=ANY, out_shape=S((N_DEV,) + v.shape, F32),
        scratch_shapes=[pltpu.SemaphoreType.DMA((N_DEV - 1,)), pltpu.SemaphoreType.DMA((N_DEV - 1,)),
                        pltpu.SemaphoreType.DMA],
    )(v)


def _sum_leading(a, name):
    n, rows, cols = a.shape
    tm = _row_tile(rows, 640)

    def body(a_ref, o_ref):
        acc = a_ref[0]
        for k in range(1, n):
            acc = acc + a_ref[k]
        o_ref[...] = acc

    return pl.pallas_call(
        body, grid=(rows // tm,), name=name, in_specs=[pl.BlockSpec((n, tm, cols), lambda i: (0, i, 0))],
        out_specs=pl.BlockSpec((tm, cols), lambda i: (i, 0)), out_shape=S((rows, cols), F32),
        compiler_params=_params(("arbitrary",)),
    )(a)


def _add_slabs(a, b, name):
    n, hr, cols = b.shape
    tm = _row_tile(hr, 640)

    def body(a_ref, b_ref, o_ref, o16_ref):
        tot = a_ref[...] + b_ref[...]
        o_ref[...] = tot
        o16_ref[...] = tot.astype(BF16)

    a_half = lax.dynamic_slice_in_dim(a, lax.axis_index("c") * hr, hr, axis=1)
    spec = pl.BlockSpec((1, tm, cols), lambda k, i: (k, i, 0))
    return pl.pallas_call(
        body, grid=(n, hr // tm), name=name, in_specs=[spec, spec], out_specs=[spec, spec],
        out_shape=[S((n, hr, cols), F32), S((n, hr, cols), BF16)],
        compiler_params=_params(("arbitrary", "arbitrary")),
    )(a_half, b)


def _add_chips(mine, others, name):
    hr, cols = mine.shape
    tm = _row_tile(hr, 640)

    def body(m_ref, o_ref, out_ref):
        acc = m_ref[...]
        for k in range(others.shape[0]):
            acc = acc + o_ref[k].astype(F32)
        out_ref[...] = acc

    return pl.pallas_call(
        body, grid=(hr // tm,), name=name,
        in_specs=[pl.BlockSpec((tm, cols), lambda i: (i, 0)), pl.BlockSpec((others.shape[0], tm, cols), lambda i: (0, i, 0))],
        out_specs=pl.BlockSpec((tm, cols), lambda i: (i, 0)), out_shape=S((hr, cols), F32),
        compiler_params=_params(("arbitrary",)),
    )(mine, others)


def _adamw(wt, g, m, v, name):
    shape = wt.shape
    cols = shape[-1]
    rows = math.prod(shape[:-1])
    tm = rows
    for cand in (512, 256, 128, 64, 32, 16, 8):
        if rows % cand == 0:
            tm = cand
            break
    c1 = 1.0 - ADAM_B1 ** ADAM_STEP
    c2 = 1.0 - ADAM_B2 ** ADAM_STEP

    def body(w_ref, g_ref, m_ref, v_ref, d_ref, nm_ref, nv_ref):
        gv = g_ref[...]
        nm = ADAM_B1 * m_ref[...] + (1.0 - ADAM_B1) * gv
        nv = ADAM_B2 * v_ref[...] + (1.0 - ADAM_B2) * (gv * gv)
        d_ref[...] = -ADAM_LR * ((nm / c1) / (jnp.sqrt(nv / c2) + ADAM_EPS) + ADAM_WD * w_ref[...])
        nm_ref[...] = nm
        nv_ref[...] = nv

    spec = pl.BlockSpec((tm, cols), lambda i: (i, 0))
    outs = pl.pallas_call(
        body, grid=(rows // tm,), name=name, in_specs=[spec] * 4, out_specs=[spec] * 3,
        out_shape=[S((rows, cols), F32)] * 3, compiler_params=_params(("arbitrary",)),
    )(*[a.reshape(rows, cols) for a in (wt, g, m, v)])
    return [o.reshape(shape) for o in outs]


WEIGHTS = ('pre_mix_norm', 'post_mix_norm', 'pre_ffn_norm', 'post_ffn_norm', 'w_in', 'w_out', 'attn_sinks', 'ssd_conv_w',
           'ssd_conv_b', 'ssd_dt_bias', 'ssd_A_log', 'ssd_D', 'ssd_norm_w', 'gdn_conv_w', 'gdn_dt_bias', 'gdn_A_log',
           'gdn_norm_w', 'ffn_w_gate', 'ffn_w_up', 'ffn_w_down')


def _step(x, target, wts, ms, vs):
    shards = [wts[k] for k, _ in SHARDED]
    shard_shapes = [a.shape for a in shards]
    rows = _slab_rows(shards)
    n_big = len(SHARDED) - 2
    gathered = _gather_weights(_pack([a.astype(BF16) for a in shards[:n_big]], rows))
    per_chip = [_unpack(gathered[k], shard_shapes[:n_big]) for k in range(N_CHIPS)]
    conv = _gather_small(_pack_small(shards[n_big:]), "gather_conv_weights")
    for k in range(N_CHIPS):
        per_chip[k] += _unpack_small(conv[2 * k], shard_shapes[n_big:])
    full = dict(wts)
    for i, (k, axis) in enumerate(SHARDED):
        full[k] = jnp.concatenate([per_chip[c][i] for c in range(N_CHIPS)], axis=axis)

    loss_part, grad_x, grads = _local_step(x[0], target[0], full)

    slabs = []
    for k in range(N_CHIPS):
        parts = []
        for (name, axis), shp in zip(SHARDED, shard_shapes):
            n = shp[axis]
            parts.append(lax.slice_in_dim(grads[name], k * n, (k + 1) * n, axis=axis))
        slabs.append(_pack(parts, rows))
    d = jnp.stack(slabs, axis=0)
    chip_sum, chip_sum16 = _add_slabs(d, _exchange_halves(d), "add_sibling")
    others = _scatter_chips(chip_sum16)
    me = 2 * lax.axis_index("x") + lax.axis_index("y")
    mine = lax.dynamic_index_in_dim(chip_sum, me, axis=0, keepdims=False)
    reduced = _join_halves(_add_chips(mine, others, "add_chips"))
    g_shards = dict(zip([k for k, _ in SHARDED], _unpack(reduced, shard_shapes)))

    small_shapes = [wts[k].shape for k in SMALL]
    small = _pack_small([grads[k] for k in SMALL] + [loss_part])
    small_sum = _sum_leading(_gather_small(small, "gather_small_grads"), "add_small")
    small_vals = _unpack_small(small_sum, small_shapes + [(1, LANES)])
    g_small = dict(zip(SMALL, small_vals[:-1]))
    loss = small_vals[-1][0, 0]

    g_all = {**g_shards, **g_small}
    d_s, m_s, v_s = _adamw(_pack_small([wts[k] for k in SMALL]), _pack_small([g_small[k] for k in SMALL]),
                           _pack_small([ms[k] for k in SMALL]), _pack_small([vs[k] for k in SMALL]), "adamw_small")
    upd = {}
    for k, dv, mv, vv in zip(SMALL, _unpack_small(d_s, small_shapes), _unpack_small(m_s, small_shapes),
                             _unpack_small(v_s, small_shapes)):
        upd[k] = (dv, mv, vv)
    for k, _ in SHARDED:
        upd[k] = _adamw(wts[k], g_shards[k], ms[k], vs[k], f"adamw_{k}")
    return (loss, grad_x[None], *[g_all[k] for k in WEIGHTS], *[upd[k][0] for k in WEIGHTS],
            *[upd[k][1] for k in WEIGHTS], *[upd[k][2] for k in WEIGHTS])


def kernel(x, pre_mix_norm, post_mix_norm, pre_ffn_norm, post_ffn_norm, w_in, w_out, attn_sinks, ssd_conv_w, ssd_conv_b, ssd_dt_bias, ssd_A_log, ssd_D, ssd_norm_w, gdn_conv_w, gdn_dt_bias, gdn_A_log, gdn_norm_w, ffn_w_gate, ffn_w_up, ffn_w_down, loss_target, m_pre_mix_norm, m_post_mix_norm, m_pre_ffn_norm, m_post_ffn_norm, m_w_in, m_w_out, m_attn_sinks, m_ssd_conv_w, m_ssd_conv_b, m_ssd_dt_bias, m_ssd_A_log, m_ssd_D, m_ssd_norm_w, m_gdn_conv_w, m_gdn_dt_bias, m_gdn_A_log, m_gdn_norm_w, m_ffn_w_gate, m_ffn_w_up, m_ffn_w_down, v_pre_mix_norm, v_post_mix_norm, v_pre_ffn_norm, v_post_ffn_norm, v_w_in, v_w_out, v_attn_sinks, v_ssd_conv_w, v_ssd_conv_b, v_ssd_dt_bias, v_ssd_A_log, v_ssd_D, v_ssd_norm_w, v_gdn_conv_w, v_gdn_dt_bias, v_gdn_A_log, v_gdn_norm_w, v_ffn_w_gate, v_ffn_w_up, v_ffn_w_down):
    wts = dict(zip(WEIGHTS, (pre_mix_norm, post_mix_norm, pre_ffn_norm, post_ffn_norm, w_in, w_out, attn_sinks, ssd_conv_w, ssd_conv_b, ssd_dt_bias, ssd_A_log, ssd_D, ssd_norm_w, gdn_conv_w, gdn_dt_bias, gdn_A_log, gdn_norm_w, ffn_w_gate, ffn_w_up, ffn_w_down)))
    ms = dict(zip(WEIGHTS, (m_pre_mix_norm, m_post_mix_norm, m_pre_ffn_norm, m_post_ffn_norm, m_w_in, m_w_out, m_attn_sinks, m_ssd_conv_w, m_ssd_conv_b, m_ssd_dt_bias, m_ssd_A_log, m_ssd_D, m_ssd_norm_w, m_gdn_conv_w, m_gdn_dt_bias, m_gdn_A_log, m_gdn_norm_w, m_ffn_w_gate, m_ffn_w_up, m_ffn_w_down)))
    vs = dict(zip(WEIGHTS, (v_pre_mix_norm, v_post_mix_norm, v_pre_ffn_norm, v_post_ffn_norm, v_w_in, v_w_out, v_attn_sinks, v_ssd_conv_w, v_ssd_conv_b, v_ssd_dt_bias, v_ssd_A_log, v_ssd_D, v_ssd_norm_w, v_gdn_conv_w, v_gdn_dt_bias, v_gdn_A_log, v_gdn_norm_w, v_ffn_w_gate, v_ffn_w_up, v_ffn_w_down)))
    return _step(x, loss_target, wts, ms, vs)
```

```python
import functools
import math

import jax
import jax.numpy as jnp
from jax import lax
from jax.experimental import pallas as pl
from jax.experimental.pallas import tpu as pltpu

F32, BF16 = jnp.float32, jnp.bfloat16
HI = lax.Precision.HIGHEST
MESH = pl.DeviceIdType.MESH
S = jax.ShapeDtypeStruct

D_MODEL = 1024
DEPTH = 2
CHUNK = 64
EPS = 1e-6
FF = 2816
N_CHIPS = 4
N_DEV = 8
LANES = 128

VMEM_LIMIT_BYTES = 56 * 1024 * 1024

PC_GQKV, PC_GZ, PC_XBC, PC_ATT, PC_SZ, PC_DT, PC_BA, PC_TOT = 0, 768, 1024, 2048, 2560, 3072, 3200, 3328

ADAM_LR, ADAM_B1, ADAM_B2, ADAM_EPS, ADAM_WD, ADAM_STEP = 0.001, 0.9, 0.999, 1e-08, 0.01, 10

ALIBI_SLOPES = tuple(2.0 ** (-8.0 * (h + 1) / 4) for h in range(4))


def _params(sem=None, **kw):
    if sem is not None:
        kw["dimension_semantics"] = sem
    return pltpu.CompilerParams(vmem_limit_bytes=VMEM_LIMIT_BYTES, **kw)


def _dot(a, b, prec=None):
    return jnp.dot(a, b, precision=prec, preferred_element_type=F32)


def _dot_nt(a, b, prec=None):
    return lax.dot_general(a, b, (((1,), (1,)), ((), ())), precision=prec, preferred_element_type=F32)


def _dot_tn(a, b, prec=None):
    return lax.dot_general(a, b, (((0,), (0,)), ((), ())), precision=prec, preferred_element_type=F32)


def _iota2(n, m):
    return lax.broadcasted_iota(jnp.int32, (n, m), 0), lax.broadcasted_iota(jnp.int32, (n, m), 1)


def _pick_col(arr, idx):
    ci = lax.broadcasted_iota(jnp.int32, arr.shape, 1)
    return jnp.sum(jnp.where(ci == idx, arr, 0.0), axis=1, keepdims=True)


def _pick_row(arr, idx):
    ri = lax.broadcasted_iota(jnp.int32, arr.shape, 0)
    return jnp.sum(jnp.where(ri == idx, arr, 0.0), axis=0, keepdims=True)


def _col_to_row(col, eye):
    return jnp.sum(eye * col, axis=0, keepdims=True)


def _rms(x, w):
    return x * lax.rsqrt(jnp.mean(x * x, axis=-1, keepdims=True) + EPS) * w


def _mm_nn(a, b, tm, tn, out_dtype, name):
    m, k = a.shape
    n = b.shape[1]
    tm, tn = min(tm, m), min(tn, n)

    def body(a_ref, b_ref, o_ref):
        o_ref[...] = _dot(a_ref[...], b_ref[...]).astype(o_ref.dtype)

    return pl.pallas_call(
        body, grid=(n // tn, m // tm), name=name,
        in_specs=[pl.BlockSpec((tm, k), lambda j, i: (i, 0)), pl.BlockSpec((k, tn), lambda j, i: (0, j))],
        out_specs=pl.BlockSpec((tm, tn), lambda j, i: (i, j)),
        out_shape=S((m, n), out_dtype), compiler_params=_params(("arbitrary", "arbitrary")),
    )(a, b)


def _mm_nt(a, b, tm, tn, out_dtype, name):
    m, k = a.shape
    n = b.shape[0]
    tm, tn = min(tm, m), min(tn, n)

    def body(a_ref, b_ref, o_ref):
        o_ref[...] = _dot_nt(a_ref[...], b_ref[...]).astype(o_ref.dtype)

    return pl.pallas_call(
        body, grid=(n // tn, m // tm), name=name,
        in_specs=[pl.BlockSpec((tm, k), lambda j, i: (i, 0)), pl.BlockSpec((tn, k), lambda j, i: (j, 0))],
        out_specs=pl.BlockSpec((tm, tn), lambda j, i: (i, j)),
        out_shape=S((m, n), out_dtype), compiler_params=_params(("arbitrary", "arbitrary")),
    )(a, b)


def _mm_tn(a, b, tm, tn, tk, name):
    t, m = a.shape
    n = b.shape[1]
    tm, tn, tk = min(tm, m), min(tn, n), min(tk, t)

    def body(a_ref, b_ref, o_ref):
        part = _dot_tn(a_ref[...], b_ref[...])

        @pl.when(pl.program_id(2) == 0)
        def _():
            o_ref[...] = part

        @pl.when(pl.program_id(2) > 0)
        def _():
            o_ref[...] += part

    return pl.pallas_call(
        body, grid=(m // tm, n // tn, t // tk), name=name,
        in_specs=[pl.BlockSpec((tk, tm), lambda i, j, k: (k, i)), pl.BlockSpec((tk, tn), lambda i, j, k: (k, j))],
        out_specs=pl.BlockSpec((tm, tn), lambda i, j, k: (i, j)),
        out_shape=S((m, n), F32), compiler_params=_params(("arbitrary", "arbitrary", "arbitrary")),
    )(a, b)


def _rowcall(fn, rows, params, row_outs, acc_outs, name, tm=512):
    t = rows[0].shape[0]
    tm = min(tm, t)
    n_in = len(rows) + len(params)
    n_ro = len(row_outs)

    def body(*refs):
        ro, ao = fn(*[r[...] for r in refs[:n_in]])
        for ref, v in zip(refs[n_in:n_in + n_ro], ro):
            ref[...] = v.astype(ref.dtype)
        acc_refs = refs[n_in + n_ro:]
        if acc_refs:
            @pl.when(pl.program_id(0) == 0)
            def _():
                for ref, v in zip(acc_refs, ao):
                    ref[...] = v

            @pl.when(pl.program_id(0) > 0)
            def _():
                for ref, v in zip(acc_refs, ao):
                    ref[...] += v

    in_specs = [pl.BlockSpec((tm, r.shape[1]), lambda i: (i, 0)) for r in rows]
    in_specs += [pl.BlockSpec(p.shape, lambda i: (0, 0)) for p in params]
    out_specs = [pl.BlockSpec((tm, c), lambda i: (i, 0)) for c, _ in row_outs]
    out_specs += [pl.BlockSpec(shape, lambda i: (0, 0)) for shape in acc_outs]
    out_shape = [S((t, c), dt) for c, dt in row_outs] + [S(shape, F32) for shape in acc_outs]
    return pl.pallas_call(
        body, grid=(t // tm,), name=name, in_specs=in_specs, out_specs=out_specs, out_shape=out_shape,
        compiler_params=_params(("arbitrary",)),
    )(*rows, *params)


def _prenorm(x, w, name):
    def fn(x, w):
        return (_rms(x, w),), ()
    return _rowcall(fn, [x], [w], [(D_MODEL, BF16)], [], name)[0]


def _resid_norm(xin, m, w_post, w_next, name):
    def fn(xin, m, w_post, w_next):
        xo = xin + _rms(m, w_post)
        return (xo, _rms(xo, w_next)), ()
    return _rowcall(fn, [xin, m], [w_post, w_next], [(D_MODEL, F32), (D_MODEL, BF16)], [], name)


def _resid_loss(xin, m, w_post, target, name):
    def fn(xin, m, target, w_post):
        r, vjp = jax.vjp(_rms, m, w_post)
        err = xin + r - target
        dy = err * (1.0 / D_MODEL)
        dm, dw = vjp(dy)
        tot = jnp.sum(jnp.sum(err * err, axis=1, keepdims=True), axis=0, keepdims=True) * (0.5 / D_MODEL)
        lane = lax.broadcasted_iota(jnp.int32, (1, LANES), 1)
        return (dy, dm), (jnp.where(lane == 0, tot, 0.0), dw)
    return _rowcall(fn, [xin, m, target], [w_post], [(D_MODEL, F32), (D_MODEL, BF16)],
                    [(1, LANES), (1, D_MODEL)], name)


def _resid_norm_bwd(x_out, m, d_direct, dh, w_post, w_next, name):
    def fn(x_out, m, d_direct, dh, w_post, w_next):
        _, vjp_n = jax.vjp(_rms, x_out, w_next)
        dx, dwn = vjp_n(dh)
        d_total = d_direct + dx
        _, vjp_p = jax.vjp(_rms, m, w_post)
        dm, dwp = vjp_p(d_total)
        return (d_total, dm), (dwn, dwp)
    return _rowcall(fn, [x_out, m, d_direct, dh], [w_post, w_next], [(D_MODEL, F32), (D_MODEL, BF16)],
                    [(1, D_MODEL), (1, D_MODEL)], name)


def _prenorm_bwd(x, d_direct, dh, w, name):
    def fn(x, d_direct, dh, w):
        _, vjp = jax.vjp(_rms, x, w)
        dx, dw = vjp(dh)
        return (d_direct + dx,), (dw,)
    return _rowcall(fn, [x, d_direct, dh], [w], [(D_MODEL, F32)], [(1, D_MODEL)], name)


def _swiglu_fn(gu):
    return jax.nn.silu(gu[:, :FF]) * gu[:, FF:]


def _swiglu(gu, name):
    def fn(gu):
        return (_swiglu_fn(gu),), ()
    return _rowcall(fn, [gu], [], [(FF, BF16)], [], name, tm=256)[0]


def _swiglu_bwd(gu, da, name):
    def fn(gu, da):
        _, vjp = jax.vjp(_swiglu_fn, gu)
        return (vjp(da)[0],), ()
    return _rowcall(fn, [gu, da], [], [(2 * FF, BF16)], [], name, tm=256)[0]


def _conv_fwd(proj, col0, width, w, b, name, tm=512):
    t = proj.shape[0]
    tm = min(tm, t)
    cb = col0 // width

    def body(x_ref, w_ref, b_ref, o_ref, ext):
        @pl.when(pl.program_id(0) == 0)
        def _():
            ext[0:8, :] = jnp.zeros((8, width), F32)

        ext[8:8 + tm, :] = x_ref[...]
        y = b_ref[...] + w_ref[0:1, :] * ext[pl.ds(5, tm), :]
        for k in range(1, 4):
            y = y + w_ref[k:k + 1, :] * ext[pl.ds(5 + k, tm), :]
        o_ref[...] = jax.nn.silu(y)
        ext[0:8, :] = ext[tm:tm + 8, :]

    return pl.pallas_call(
        body, grid=(t // tm,), name=name,
        in_specs=[pl.BlockSpec((tm, width), lambda i: (i, cb)), pl.BlockSpec((4, width), lambda i: (0, 0)),
                  pl.BlockSpec((1, width), lambda i: (0, 0))],
        out_specs=pl.BlockSpec((tm, width), lambda i: (i, 0)),
        out_shape=S((t, width), F32), scratch_shapes=[pltpu.VMEM((tm + 8, width), F32)],
        compiler_params=_params(("arbitrary",)),
    )(proj, w, b)


def _conv_bwd(proj, col0, width, w, b, dact, name, tm=512):
    t = proj.shape[0]
    tm = min(tm, t)
    nb = t // tm
    cb = col0 // width
    hb = tm // 8

    def body(x_ref, halo_ref, d_ref, w_ref, b_ref, dx_ref, dw_ref, db_ref, extx, extd):
        i = pl.program_id(0)
        blk = nb - 1 - i

        @pl.when(i == 0)
        def _():
            extd[tm:tm + 8, :] = jnp.zeros((8, width), F32)
            dw_ref[...] = jnp.zeros((4, width), F32)
            db_ref[...] = jnp.zeros((1, width), F32)

        extx[0:8, :] = jnp.where(blk == 0, 0.0, halo_ref[...])
        extx[8:8 + tm, :] = x_ref[...]
        y = b_ref[...] + w_ref[0:1, :] * extx[pl.ds(5, tm), :]
        for k in range(1, 4):
            y = y + w_ref[k:k + 1, :] * extx[pl.ds(5 + k, tm), :]
        sig = jax.nn.sigmoid(y)
        dy = d_ref[...] * (sig * (1.0 + y * (1.0 - sig)))
        extd[0:tm, :] = dy
        dx = w_ref[0:1, :] * extd[pl.ds(3, tm), :]
        for k in range(1, 4):
            dx = dx + w_ref[k:k + 1, :] * extd[pl.ds(3 - k, tm), :]
        dx_ref[...] = dx.astype(dx_ref.dtype)
        for k in range(4):
            dw_ref[k:k + 1, :] += jnp.sum(dy * extx[pl.ds(5 + k, tm), :], axis=0, keepdims=True)
        db_ref[...] += jnp.sum(dy, axis=0, keepdims=True)
        extd[tm:tm + 8, :] = extd[0:8, :]

    return pl.pallas_call(
        body, grid=(nb,), name=name,
        in_specs=[pl.BlockSpec((tm, width), lambda i: (nb - 1 - i, cb)),
                  pl.BlockSpec((8, width), lambda i: (jnp.maximum((nb - 1 - i) * hb - 1, 0), cb)),
                  pl.BlockSpec((tm, width), lambda i: (nb - 1 - i, 0)),
                  pl.BlockSpec((4, width), lambda i: (0, 0)), pl.BlockSpec((1, width), lambda i: (0, 0))],
        out_specs=[pl.BlockSpec((tm, width), lambda i: (nb - 1 - i, 0)), pl.BlockSpec((4, width), lambda i: (0, 0)),
                   pl.BlockSpec((1, width), lambda i: (0, 0))],
        out_shape=[S((t, width), BF16), S((4, width), F32), S((1, width), F32)],
        scratch_shapes=[pltpu.VMEM((tm + 8, width), F32), pltpu.VMEM((tm + 8, width), F32)],
        compiler_params=_params(("arbitrary",)),
    )(proj, proj, dact, w, b)


SWA_BQ = 256
SWA_BACK = 128


def _swa_block(q, kw, vw, sinks, blk):
    nq, nk = SWA_BQ, SWA_BQ + SWA_BACK
    r, j = _iota2(nq, nk)
    rel = r // CHUNK + 2 - j // CHUNK
    valid = (rel >= 0) & (rel <= 2) & (blk * (SWA_BQ // CHUNK) + j // CHUNK - 2 >= 0)
    dist = jnp.abs(r + SWA_BACK - j).astype(F32)
    outs = []
    for h in range(4):
        kv = h // 2
        qh = q[:, 64 * h:64 * h + 64]
        kh = kw[:, 64 * kv:64 * kv + 64]
        vh = vw[:, 64 * kv:64 * kv + 64]
        s = _dot_nt(qh, kh) * 0.125 - ALIBI_SLOPES[h] * dist
        s = jnp.where(valid, s, -1e30)
        sink = _pick_col(sinks, h)
        m = jnp.maximum(jnp.max(s, axis=1, keepdims=True), sink)
        e = jnp.exp(s - m)
        den = jnp.sum(e, axis=1, keepdims=True) + jnp.exp(sink - m)
        outs.append(_dot(e / den, vh))
    return jnp.concatenate(outs, axis=1)


def _swa_fwd(proj, sinks, name):
    t = proj.shape[0]
    qb, kb = PC_ATT // 256, PC_ATT // 128 + 2
    win = SWA_BQ + SWA_BACK

    def body(q_ref, k_ref, v_ref, s_ref, o_ref, kp, vp):
        i = pl.program_id(0)

        @pl.when(i == 0)
        def _():
            kp[0:SWA_BACK, :] = jnp.zeros((SWA_BACK, 128), F32)
            vp[0:SWA_BACK, :] = jnp.zeros((SWA_BACK, 128), F32)
            kp[SWA_BACK:, :] = k_ref[...]
            vp[SWA_BACK:, :] = v_ref[...]

        start = pl.multiple_of(i * SWA_BQ, SWA_BQ)
        o = _swa_block(q_ref[...], kp[pl.ds(start, win), :], vp[pl.ds(start, win), :], s_ref[...], i)
        o_ref[...] = o.astype(o_ref.dtype)

    return pl.pallas_call(
        body, grid=(t // SWA_BQ,), name=name,
        in_specs=[pl.BlockSpec((SWA_BQ, 256), lambda i: (i, qb)), pl.BlockSpec((t, 128), lambda i: (0, kb)),
                  pl.BlockSpec((t, 128), lambda i: (0, kb + 1)), pl.BlockSpec((1, LANES), lambda i: (0, 0))],
        out_specs=pl.BlockSpec((SWA_BQ, 256), lambda i: (i, 0)),
        out_shape=S((t, 256), BF16),
        scratch_shapes=[pltpu.VMEM((t + SWA_BACK, 128), F32), pltpu.VMEM((t + SWA_BACK, 128), F32)],
        compiler_params=_params(("arbitrary",)),
    )(proj, proj, proj, sinks)


def _swa_bwd(proj, sinks, dcat, dcol0, name):
    t = proj.shape[0]
    nb = t // SWA_BQ
    qb, kb = PC_ATT // 256, PC_ATT // 128 + 2
    db = dcol0 // 256
    win = SWA_BQ + SWA_BACK

    def body(q_ref, k_ref, v_ref, s_ref, do_ref, dq_ref, dk_ref, dv_ref, ds_ref, kp, vp, dkp, dvp):
        i = pl.program_id(0)

        @pl.when(i == 0)
        def _():
            kp[0:SWA_BACK, :] = jnp.zeros((SWA_BACK, 128), F32)
            vp[0:SWA_BACK, :] = jnp.zeros((SWA_BACK, 128), F32)
            kp[SWA_BACK:, :] = k_ref[...]
            vp[SWA_BACK:, :] = v_ref[...]
            dkp[...] = jnp.zeros_like(dkp)
            dvp[...] = jnp.zeros_like(dvp)
            ds_ref[...] = jnp.zeros_like(ds_ref)

        start = pl.multiple_of(i * SWA_BQ, SWA_BQ)
        _, vjp = jax.vjp(functools.partial(_swa_block, blk=i), q_ref[...], kp[pl.ds(start, win), :],
                         vp[pl.ds(start, win), :], s_ref[...])
        dq, dkw, dvw, dsk = vjp(do_ref[...])
        dq_ref[...] = dq.astype(dq_ref.dtype)
        dkp[pl.ds(start, win), :] += dkw
        dvp[pl.ds(start, win), :] += dvw
        ds_ref[...] += dsk

        @pl.when(i == nb - 1)
        def _():
            dk_ref[...] = dkp[SWA_BACK:, :].astype(dk_ref.dtype)
            dv_ref[...] = dvp[SWA_BACK:, :].astype(dv_ref.dtype)

    return pl.pallas_call(
        body, grid=(nb,), name=name,
        in_specs=[pl.BlockSpec((SWA_BQ, 256), lambda i: (i, qb)), pl.BlockSpec((t, 128), lambda i: (0, kb)),
                  pl.BlockSpec((t, 128), lambda i: (0, kb + 1)), pl.BlockSpec((1, LANES), lambda i: (0, 0)),
                  pl.BlockSpec((SWA_BQ, 256), lambda i: (i, db))],
        out_specs=[pl.BlockSpec((SWA_BQ, 256), lambda i: (i, 0)), pl.BlockSpec((t, 128), lambda i: (0, 0)),
                   pl.BlockSpec((t, 128), lambda i: (0, 0)), pl.BlockSpec((1, LANES), lambda i: (0, 0))],
        out_shape=[S((t, 256), BF16), S((t, 128), BF16), S((t, 128), BF16), S((1, LANES), F32)],
        scratch_shapes=[pltpu.VMEM((t + SWA_BACK, 128), F32) for _ in range(4)],
        compiler_params=_params(("arbitrary",)),
    )(proj, proj, proj, sinks, dcat)


def _ssd_chunk(z, xbc, dt_raw, state, dtb, alog, dsk, nw):
    n = CHUNK
    r, c = _iota2(n, n)
    tril = r >= c
    eye = (r == c).astype(F32)
    dt = jax.nn.softplus(dt_raw + dtb)
    acs = _dot(tril.astype(F32), dt * (-jnp.exp(alog)), HI)
    xs, bm, cm = xbc[:, :512], xbc[:, 512:768], xbc[:, 768:1024]
    heads = range(8)
    bg = [bm[:, 128 * g:128 * g + 128] for g in range(2)]
    cg = [cm[:, 128 * g:128 * g + 128] for g in range(2)]
    cb = [_dot_nt(cg[g], bg[g]) for g in range(2)]
    dth = [_pick_col(dt, h) for h in heads]
    acol = [_pick_col(acs, h) for h in heads]
    arow = [_col_to_row(a, eye) for a in acol]
    lmat = [jnp.where(tril, jnp.exp(jnp.where(tril, a - b, 0.0)), 0.0) for a, b in zip(acol, arow)]
    xh = [xs[:, 64 * h:64 * h + 64] for h in heads]
    xc = [x * t for x, t in zip(xh, dth)]
    st = [state[64 * h:64 * h + 64, :] for h in heads]
    alast = [_pick_row(a, n - 1) for a in acol]
    y_in = [_dot(cb[h // 4] * lmat[h], xc[h]) for h in heads]
    y_st = [_dot_nt(cg[h // 4], st[h]) * jnp.exp(acol[h]) for h in heads]
    ys = [y_in[h] + y_st[h] + xh[h] * _pick_col(dsk, h) for h in heads]
    new_states = [st[h] * jnp.exp(alast[h]) + _dot_tn(xc[h] * jnp.exp(alast[h] - acol[h]), bg[h // 4]) for h in heads]
    gg = jnp.concatenate(ys, axis=1) * jax.nn.silu(z)
    outs = []
    for gi in range(2):
        gv = gg[:, 256 * gi:256 * gi + 256]
        outs.append(gv * lax.rsqrt(jnp.mean(gv * gv, axis=-1, keepdims=True) + EPS))
    return jnp.concatenate(outs, axis=1) * nw, jnp.concatenate(new_states, axis=0)


def _ssd_fwd(proj, xbc, dtb, alog, dsk, nw, name):
    t = proj.shape[0]
    nc = t // CHUNK

    def body(z_ref, x_ref, dt_ref, dtb_ref, al_ref, d_ref, nw_ref, o_ref, st_ref, state):
        @pl.when(pl.program_id(0) == 0)
        def _():
            state[...] = jnp.zeros_like(state)

        st_ref[0] = state[...]
        o, ns = _ssd_chunk(z_ref[...], x_ref[...], dt_ref[...], state[...], dtb_ref[...], al_ref[...], d_ref[...],
                           nw_ref[...])
        o_ref[...] = o.astype(o_ref.dtype)
        state[...] = ns

    vec = pl.BlockSpec((1, LANES), lambda i: (0, 0))
    return pl.pallas_call(
        body, grid=(nc,), name=name,
        in_specs=[pl.BlockSpec((CHUNK, 512), lambda i: (i, PC_SZ // 512)), pl.BlockSpec((CHUNK, 1024), lambda i: (i, 0)),
                  pl.BlockSpec((CHUNK, 128), lambda i: (i, PC_DT // 128)), vec, vec, vec,
                  pl.BlockSpec((1, 512), lambda i: (0, 0))],
        out_specs=[pl.BlockSpec((CHUNK, 512), lambda i: (i, 0)), pl.BlockSpec((1, 512, 128), lambda i: (i, 0, 0))],
        out_shape=[S((t, 512), BF16), S((nc, 512, 128), F32)],
        scratch_shapes=[pltpu.VMEM((512, 128), F32)],
        compiler_params=_params(("arbitrary",)),
    )(proj, xbc, proj, dtb, alog, dsk, nw)


def _ssd_bwd(proj, xbc, states, dtb, alog, dsk, nw, dcat, dcol0, name):
    t = proj.shape[0]
    nc = t // CHUNK
    db = dcol0 // 512

    def body(z_ref, x_ref, dt_ref, st_ref, dtb_ref, al_ref, d_ref, nw_ref, do_ref,
             dz_ref, dx_ref, ddt_ref, gdtb_ref, gal_ref, gd_ref, gnw_ref, dstate):
        @pl.when(pl.program_id(0) == 0)
        def _():
            dstate[...] = jnp.zeros_like(dstate)
            gdtb_ref[...] = jnp.zeros_like(gdtb_ref)
            gal_ref[...] = jnp.zeros_like(gal_ref)
            gd_ref[...] = jnp.zeros_like(gd_ref)
            gnw_ref[...] = jnp.zeros_like(gnw_ref)

        _, vjp = jax.vjp(_ssd_chunk, z_ref[...], x_ref[...], dt_ref[...], st_ref[0], dtb_ref[...], al_ref[...],
                         d_ref[...], nw_ref[...])
        dz, dx, ddt, dst, gdtb, gal, gd, gnw = vjp((do_ref[...], dstate[...]))
        dz_ref[...] = dz.astype(dz_ref.dtype)
        dx_ref[...] = dx
        ddt_ref[...] = ddt.astype(ddt_ref.dtype)
        dstate[...] = dst
        gdtb_ref[...] += gdtb
        gal_ref[...] += gal
        gd_ref[...] += gd
        gnw_ref[...] += gnw

    rev = lambda i: nc - 1 - i
    vec = pl.BlockSpec((1, LANES), lambda i: (0, 0))
    vec512 = pl.BlockSpec((1, 512), lambda i: (0, 0))
    return pl.pallas_call(
        body, grid=(nc,), name=name,
        in_specs=[pl.BlockSpec((CHUNK, 512), lambda i: (rev(i), PC_SZ // 512)),
                  pl.BlockSpec((CHUNK, 1024), lambda i: (rev(i), 0)),
                  pl.BlockSpec((CHUNK, 128), lambda i: (rev(i), PC_DT // 128)),
                  pl.BlockSpec((1, 512, 128), lambda i: (rev(i), 0, 0)), vec, vec, vec, vec512,
                  pl.BlockSpec((CHUNK, 512), lambda i: (rev(i), db))],
        out_specs=[pl.BlockSpec((CHUNK, 512), lambda i: (rev(i), 0)), pl.BlockSpec((CHUNK, 1024), lambda i: (rev(i), 0)),
                   pl.BlockSpec((CHUNK, 128), lambda i: (rev(i), 0)), vec, vec, vec, vec512],
        out_shape=[S((t, 512), BF16), S((t, 1024), F32), S((t, 128), BF16), S((1, LANES), F32), S((1, LANES), F32),
                   S((1, LANES), F32), S((1, 512), F32)],
        scratch_shapes=[pltpu.VMEM((512, 128), F32)],
        compiler_params=_params(("arbitrary",)),
    )(proj, xbc, proj, states, dtb, alog, dsk, nw, dcat)


SOLVE_PREC = lax.Precision.HIGH


@jax.custom_vjp
def _unit_lower_inverses(nas):
    n = nas[0].shape[0]
    r, c = _iota2(n, n)
    eye = (r == c).astype(F32)
    tm, pw = [eye + a for a in nas], list(nas)
    for _ in range(5):
        pw = [_dot(p, p, SOLVE_PREC) for p in pw]
        tm = [t + _dot(t, p, SOLVE_PREC) for t, p in zip(tm, pw)]
    return tuple(tm)


def _unit_lower_inverses_fwd(nas):
    ts = _unit_lower_inverses(nas)
    return ts, ts


def _unit_lower_inverses_bwd(ts, gs):
    part = [_dot_nt(g, t, SOLVE_PREC) for g, t in zip(gs, ts)]
    return (tuple(_dot_tn(t, p, SOLVE_PREC) for t, p in zip(ts, part)),)


_unit_lower_inverses.defvjp(_unit_lower_inverses_fwd, _unit_lower_inverses_bwd)


def _gdn_chunk(qkv, z, ba, state, dtb, alog, nw):
    n = CHUNK
    r, c = _iota2(n, n)
    tril = r >= c
    stril = r > c
    eye = (r == c).astype(F32)
    beta_all = jax.nn.sigmoid(ba)
    gcs = _dot(tril.astype(F32), -jnp.exp(alog) * jax.nn.softplus(ba + dtb), HI)
    heads = range(4)
    qh = [qkv[:, 64 * h:64 * h + 64] for h in heads]
    kh = [qkv[:, 256 + 64 * h:256 + 64 * h + 64] for h in heads]
    vh = [qkv[:, 512 + 64 * h:512 + 64 * h + 64] for h in heads]
    qn = [q * lax.rsqrt(jnp.sum(q * q, axis=-1, keepdims=True) + EPS) * 0.125 for q in qh]
    kn = [k * lax.rsqrt(jnp.sum(k * k, axis=-1, keepdims=True) + EPS) for k in kh]
    beta = [_pick_col(beta_all, h) for h in heads]
    gcol = [_pick_col(gcs, 4 + h) for h in heads]
    grow = [_col_to_row(g, eye) for g in gcol]
    decay = [jnp.where(tril, jnp.exp(jnp.where(tril, gc - gr, 0.0)), 0.0) for gc, gr in zip(gcol, grow)]
    kbeta = [k * b for k, b in zip(kn, beta)]
    kk = [_dot_nt(kb, k) for kb, k in zip(kbeta, kn)]
    qk = [_dot_nt(q, k) * dc for q, k, dc in zip(qn, kn, decay)]
    tms = _unit_lower_inverses(tuple(-jnp.where(stril, x * dc, 0.0) for x, dc in zip(kk, decay)))
    rhs = [jnp.concatenate([v * b, kb * jnp.exp(g)], axis=1) for v, b, kb, g in zip(vh, beta, kbeta, gcol)]
    sol = [_dot(t, x, SOLVE_PREC) for t, x in zip(tms, rhs)]
    st = [state[64 * h:64 * h + 64, :] for h in heads]
    v_new = [s_[:, :64] - _dot(s_[:, 64:], s) for s_, s in zip(sol, st)]
    o = [_dot(q * jnp.exp(g), s) + _dot(x, vn) for q, g, s, x, vn in zip(qn, gcol, st, qk, v_new)]
    glast = [_pick_row(g, n - 1) for g in gcol]
    new_states = [s * jnp.exp(gl) + _dot_tn(k * jnp.exp(gl - g), vn)
                  for s, gl, k, g, vn in zip(st, glast, kn, gcol, v_new)]
    o = [x * lax.rsqrt(jnp.mean(x * x, axis=-1, keepdims=True) + EPS) * nw for x in o]
    outs = [x * jax.nn.silu(z[:, 64 * h:64 * h + 64]) for h, x in zip(heads, o)]
    return jnp.concatenate(outs, axis=1), jnp.concatenate(new_states, axis=0)


def _gdn_fwd(proj, qkv, dtb, alog, nw, name):
    t = proj.shape[0]
    nc = t // CHUNK

    def body(q_ref, z_ref, ba_ref, dtb_ref, al_ref, nw_ref, o_ref, st_ref, state):
        @pl.when(pl.program_id(0) == 0)
        def _():
            state[...] = jnp.zeros_like(state)

        st_ref[0] = state[...]
        o, ns = _gdn_chunk(q_ref[...], z_ref[...], ba_ref[...], state[...], dtb_ref[...], al_ref[...], nw_ref[...])
        o_ref[...] = o.astype(o_ref.dtype)
        state[...] = ns

    vec = pl.BlockSpec((1, LANES), lambda i: (0, 0))
    return pl.pallas_call(
        body, grid=(nc,), name=name,
        in_specs=[pl.BlockSpec((CHUNK, 768), lambda i: (i, 0)), pl.BlockSpec((CHUNK, 256), lambda i: (i, PC_GZ // 256)),
                  pl.BlockSpec((CHUNK, 128), lambda i: (i, PC_BA // 128)), vec, vec, pl.BlockSpec((1, 64), lambda i: (0, 0))],
        out_specs=[pl.BlockSpec((CHUNK, 256), lambda i: (i, 0)), pl.BlockSpec((1, 256, 64), lambda i: (i, 0, 0))],
        out_shape=[S((t, 256), BF16), S((nc, 256, 64), F32)],
        scratch_shapes=[pltpu.VMEM((256, 64), F32)],
        compiler_params=_params(("arbitrary",)),
    )(qkv, proj, proj, dtb, alog, nw)


def _gdn_bwd(proj, qkv, states, dtb, alog, nw, dcat, dcol0, name):
    t = proj.shape[0]
    nc = t // CHUNK
    db = dcol0 // 256

    def body(q_ref, z_ref, ba_ref, st_ref, dtb_ref, al_ref, nw_ref, do_ref,
             dq_ref, dz_ref, dba_ref, gdtb_ref, gal_ref, gnw_ref, dstate):
        @pl.when(pl.program_id(0) == 0)
        def _():
            dstate[...] = jnp.zeros_like(dstate)
            gdtb_ref[...] = jnp.zeros_like(gdtb_ref)
            gal_ref[...] = jnp.zeros_like(gal_ref)
            gnw_ref[...] = jnp.zeros_like(gnw_ref)

        _, vjp = jax.vjp(_gdn_chunk, q_ref[...], z_ref[...], ba_ref[...], st_ref[0], dtb_ref[...], al_ref[...],
                         nw_ref[...])
        dq, dz, dba, dst, gdtb, gal, gnw = vjp((do_ref[...], dstate[...]))
        dq_ref[...] = dq
        dz_ref[...] = dz.astype(dz_ref.dtype)
        dba_ref[...] = dba.astype(dba_ref.dtype)
        dstate[...] = dst
        gdtb_ref[...] += gdtb
        gal_ref[...] += gal
        gnw_ref[...] += gnw

    rev = lambda i: nc - 1 - i
    vec = pl.BlockSpec((1, LANES), lambda i: (0, 0))
    vec64 = pl.BlockSpec((1, 64), lambda i: (0, 0))
    return pl.pallas_call(
        body, grid=(nc,), name=name,
        in_specs=[pl.BlockSpec((CHUNK, 768), lambda i: (rev(i), 0)),
                  pl.BlockSpec((CHUNK, 256), lambda i: (rev(i), PC_GZ // 256)),
                  pl.BlockSpec((CHUNK, 128), lambda i: (rev(i), PC_BA // 128)),
                  pl.BlockSpec((1, 256, 64), lambda i: (rev(i), 0, 0)), vec, vec, vec64,
                  pl.BlockSpec((CHUNK, 256), lambda i: (rev(i), db))],
        out_specs=[pl.BlockSpec((CHUNK, 768), lambda i: (rev(i), 0)), pl.BlockSpec((CHUNK, 256), lambda i: (rev(i), 0)),
                   pl.BlockSpec((CHUNK, 128), lambda i: (rev(i), 0)), vec, vec, vec64],
        out_shape=[S((t, 768), F32), S((t, 256), BF16), S((t, 128), BF16), S((1, LANES), F32), S((1, LANES), F32),
                   S((1, 64), F32)],
        scratch_shapes=[pltpu.VMEM((256, 64), F32)],
        compiler_params=_params(("arbitrary",)),
    )(qkv, proj, proj, states, dtb, alog, nw, dcat)


def _pad_cols(w):
    z = jnp.zeros((w.shape[0], 120), w.dtype)
    return jnp.concatenate([w[:, 2056:2824], w[:, 2824:3080], w[:, 1024:2048], w[:, 0:512], w[:, 512:1024],
                            w[:, 2048:2056], z, w[:, 3080:3088], z], axis=1)


def _unpad_cols(g):
    return jnp.concatenate([g[:, PC_ATT:PC_ATT + 512], g[:, PC_SZ:PC_SZ + 512], g[:, PC_XBC:PC_XBC + 1024],
                            g[:, PC_DT:PC_DT + 8], g[:, PC_GQKV:PC_GQKV + 768], g[:, PC_GZ:PC_GZ + 256],
                            g[:, PC_BA:PC_BA + 8]], axis=1)


def _vec128(v, at=0):
    return jnp.zeros((1, LANES), F32).at[0, at:at + v.shape[0]].set(v)


def _layer_weights(w, big, l):
    cols = lambda name: jnp.concatenate([big[name][k, l] for k in range(N_CHIPS)], axis=1)
    rows = lambda name: jnp.concatenate([big[name][k, l] for k in range(N_CHIPS)], axis=0)
    w_out = rows("w_out")
    return dict(
        pre_mix=w["pre_mix_norm"][l][None], post_mix=w["post_mix_norm"][l][None],
        pre_ffn=w["pre_ffn_norm"][l][None], post_ffn=w["post_ffn_norm"][l][None],
        w_in=_pad_cols(cols("w_in")),
        w_out=jnp.concatenate([w_out[256:768], w_out[0:256], w_out[768:1024]], axis=0),
        w_gu=jnp.concatenate([cols("ffn_w_gate"), cols("ffn_w_up")], axis=1),
        w_down=rows("ffn_w_down"),
        sinks=_vec128(w["attn_sinks"][l]),
        s_cw=w["ssd_conv_w"][l], s_cb=w["ssd_conv_b"][l][None],
        s_dtb=_vec128(w["ssd_dt_bias"][l]), s_alog=_vec128(w["ssd_A_log"][l]), s_d=_vec128(w["ssd_D"][l]),
        s_nw=w["ssd_norm_w"][l][None],
        g_cw=w["gdn_conv_w"][l], g_cb=jnp.zeros((1, 768), F32),
        g_dtb=_vec128(w["gdn_dt_bias"][l], 4), g_alog=_vec128(w["gdn_A_log"][l], 4), g_nw=w["gdn_norm_w"][l][None],
    )


RAW_GRADS = ("w_in_pad", "w_out_cat", "w_gu", "ffn_w_down")


def _local_step(x, target, lw):
    saved = []
    xin = x
    h = _prenorm(x, lw[0]["pre_mix"], "prenorm0")
    for l in range(DEPTH):
        p = lw[l]
        proj = _mm_nn(h, p["w_in"], 512, PC_TOT, F32, f"inproj{l}")
        xbc = _conv_fwd(proj, PC_XBC, 1024, p["s_cw"], p["s_cb"], f"ssd_conv{l}")
        gqkv = _conv_fwd(proj, PC_GQKV, 768, p["g_cw"], p["g_cb"], f"gdn_conv{l}")
        att = _swa_fwd(proj, p["sinks"], f"swa{l}")
        ssd, s_states = _ssd_fwd(proj, xbc, p["s_dtb"], p["s_alog"], p["s_d"], p["s_nw"], f"ssd{l}")
        gdn, g_states = _gdn_fwd(proj, gqkv, p["g_dtb"], p["g_alog"], p["g_nw"], f"gdn{l}")
        cat = jnp.concatenate([ssd, att, gdn], axis=1)
        mix = _mm_nn(cat, p["w_out"], 512, 1024, F32, f"outproj{l}")
        x1, h2 = _resid_norm(xin, mix, p["post_mix"], p["pre_ffn"], f"postmix{l}")
        gu = _mm_nn(h2, p["w_gu"], 512, FF, F32, f"ffn_gu{l}")
        act = _swiglu(gu, f"swiglu{l}")
        f = _mm_nn(act, p["w_down"], 512, 1024, F32, f"ffn_down{l}")
        saved.append(dict(xin=xin, h=h, proj=proj, xbc=xbc, gqkv=gqkv, s_states=s_states, g_states=g_states, cat=cat,
                          mix=mix, x1=x1, h2=h2, gu=gu, act=act, f=f))
        if l + 1 < DEPTH:
            xin, h = _resid_norm(x1, f, p["post_ffn"], lw[l + 1]["pre_mix"], f"postffn{l}")

    g = {k: [None] * DEPTH for k in SMALL + CONV + RAW_GRADS}
    last = saved[-1]
    d_x2, d_f, loss_part, g["post_ffn_norm"][DEPTH - 1] = _resid_loss(
        last["x1"], last["f"], lw[-1]["post_ffn"], target, "loss")
    for l in reversed(range(DEPTH)):
        p, s = lw[l], saved[l]
        d_act = _mm_nt(d_f, p["w_down"], 512, FF, F32, f"d_act{l}")
        g["ffn_w_down"][l] = _mm_tn(s["act"], d_f, 1408, 1024, 512, f"dw_down{l}")
        d_gu = _swiglu_bwd(s["gu"], d_act, f"d_swiglu{l}")
        d_h2 = _mm_nt(d_gu, p["w_gu"], 256, 1024, F32, f"d_h2{l}")
        g["w_gu"][l] = _mm_tn(s["h2"], d_gu, 512, FF, 512, f"dw_gu{l}")
        d_x1, d_mix, g["pre_ffn_norm"][l], g["post_mix_norm"][l] = _resid_norm_bwd(
            s["x1"], s["mix"], d_x2, d_h2, p["post_mix"], p["pre_ffn"], f"d_postmix{l}")
        d_cat = _mm_nt(d_mix, p["w_out"], 512, 1024, F32, f"d_cat{l}")
        g["w_out_cat"][l] = _mm_tn(s["cat"], d_mix, 512, 1024, 512, f"dw_out{l}")
        d_q, d_k, d_v, g_sinks = _swa_bwd(s["proj"], p["sinks"], d_cat, 512, f"d_swa{l}")
        d_sz, d_xbc, d_dt, g_dtb, g_alog, g_d, g["ssd_norm_w"][l] = _ssd_bwd(
            s["proj"], s["xbc"], s["s_states"], p["s_dtb"], p["s_alog"], p["s_d"], p["s_nw"], d_cat, 0, f"d_ssd{l}")
        d_gq, d_gz, d_ba, gg_dtb, gg_alog, g["gdn_norm_w"][l] = _gdn_bwd(
            s["proj"], s["gqkv"], s["g_states"], p["g_dtb"], p["g_alog"], p["g_nw"], d_cat, 768, f"d_gdn{l}")
        d_xbc_raw, g["ssd_conv_w"][l], g["ssd_conv_b"][l] = _conv_bwd(
            s["proj"], PC_XBC, 1024, p["s_cw"], p["s_cb"], d_xbc, f"d_ssd_conv{l}")
        d_gq_raw, g["gdn_conv_w"][l], _ = _conv_bwd(s["proj"], PC_GQKV, 768, p["g_cw"], p["g_cb"], d_gq, f"d_gdn_conv{l}")
        d_proj = jnp.concatenate([d_gq_raw, d_gz, d_xbc_raw, d_q, d_k, d_v, d_sz, d_dt, d_ba], axis=1)
        d_h = _mm_nt(d_proj, p["w_in"], 512, 1024, F32, f"d_h{l}")
        g["w_in_pad"][l] = _mm_tn(s["h"], d_proj, 512, PC_TOT // 2, 512, f"dw_in{l}")
        g["attn_sinks"][l] = g_sinks[0, :4]
        g["ssd_dt_bias"][l], g["ssd_A_log"][l], g["ssd_D"][l] = g_dtb[0, :8], g_alog[0, :8], g_d[0, :8]
        g["gdn_dt_bias"][l], g["gdn_A_log"][l] = gg_dtb[0, 4:8], gg_alog[0, 4:8]
        if l > 0:
            sp = saved[l - 1]
            d_x2, d_f, g["pre_mix_norm"][l], g["post_ffn_norm"][l - 1] = _resid_norm_bwd(
                s["xin"], sp["f"], d_x1, d_h, lw[l - 1]["post_ffn"], p["pre_mix"], f"d_postffn{l - 1}")
        else:
            grad_x, g["pre_mix_norm"][0] = _prenorm_bwd(s["xin"], d_x1, d_h, p["pre_mix"], "d_prenorm0")

    small = {k: jnp.stack([a.reshape(-1) for a in g[k]], axis=0) for k in SMALL + CONV}
    return loss_part, grad_x, small, {k: g[k] for k in RAW_GRADS}


BIG = (("w_in", 2), ("w_out", 1), ("ffn_w_gate", 2), ("ffn_w_up", 2), ("ffn_w_down", 1))
CONV = ("ssd_conv_w", "gdn_conv_w")
SMALL = ("pre_mix_norm", "post_mix_norm", "pre_ffn_norm", "post_ffn_norm", "attn_sinks", "ssd_conv_b", "ssd_dt_bias",
         "ssd_A_log", "ssd_D", "ssd_norm_w", "gdn_dt_bias", "gdn_A_log", "gdn_norm_w")


def _row_tile(rows, cap):
    best = rows
    for t in range(8, min(cap, rows) + 1, 8):
        if rows % t == 0:
            best = t
    return best


SMALL_UNIT = 8 * LANES


def _pack_small(vals):
    rows = []
    for a in vals:
        f = a.reshape(-1)
        pad = -f.shape[0] % SMALL_UNIT
        rows.append(jnp.concatenate([f, jnp.zeros((pad,), F32)]).reshape(-1, LANES))
    return jnp.concatenate(rows, axis=0)


def _unpack_small(mat, shapes):
    out, r = [], 0
    for shp in shapes:
        n = math.prod(shp)
        nr = -(-n // SMALL_UNIT) * 8
        out.append(mat[r:r + nr].reshape(-1)[:n].reshape(shp))
        r += nr
    return out


def _place():
    x, y, c = lax.axis_index("x"), lax.axis_index("y"), lax.axis_index("c")
    chips = [(1 - x, y), (x, 1 - y), (1 - x, 1 - y)]
    return x, y, c, chips


ANY = pl.BlockSpec(memory_space=pl.ANY)


def _remote(src, dst, sems, k, to):
    send_sems, recv_sems = sems
    return pltpu.make_async_remote_copy(src_ref=src, dst_ref=dst, send_sem=send_sems.at[k], recv_sem=recv_sems.at[k],
                                        device_id=to, device_id_type=MESH)


def _sem_pairs(n):
    return [pltpu.SemaphoreType.DMA((n,)), pltpu.SemaphoreType.DMA((n,))]


def _gather_weights(shards):
    n = len(shards)

    def body(*refs):
        s_refs, g_refs, sems = refs[:n], refs[n:2 * n], refs[2 * n:]
        x, y, c, chips = _place()
        sib = (x, y, 1 - c)
        me = 2 * x + y
        first = [_remote(s_refs[i].at[c], g_refs[i].at[me, c], sems, 6 * i + j, (px, py, c))
                 for i in range(n) for j, (px, py) in enumerate(chips)]
        for cp in first:
            cp.start()
        passed = []
        for j, (px, py) in enumerate(chips):
            for i in range(n):
                landed = g_refs[i].at[2 * px + py, c]
                _remote(landed, landed, sems, 6 * i + j, (px, py, c)).wait_recv()
                fw = _remote(landed, landed, sems, 6 * i + 3 + j, sib)
                fw.start()
                passed.append(fw)
        for j, (px, py) in enumerate(chips):
            for i in range(n):
                landed = g_refs[i].at[2 * px + py, 1 - c]
                _remote(landed, landed, sems, 6 * i + 3 + j, sib).wait_recv()
        for cp in first + passed:
            cp.wait_send()

    return pl.pallas_call(
        body, name="gather_weights", in_specs=[ANY] * n, out_specs=[ANY] * n,
        out_shape=[S((N_CHIPS,) + a.shape, a.dtype) for a in shards], scratch_shapes=_sem_pairs(6 * n),
    )(*shards)


def _exchange_halves(ds):
    n = len(ds)

    def body(*refs):
        d_refs, t_refs, sems = refs[:n], refs[n:2 * n], refs[2 * n:]
        x, y, c, _ = _place()
        cps = [_remote(d_refs[i].at[:, 1 - c], t_refs[i], sems, i, (x, y, 1 - c)) for i in range(n)]
        for cp in cps:
            cp.start()
        for cp in cps:
            cp.wait()

    return pl.pallas_call(
        body, name="exchange_halves", in_specs=[ANY] * n, out_specs=[ANY] * n,
        out_shape=[S((N_CHIPS,) + a.shape[2:], a.dtype) for a in ds], scratch_shapes=_sem_pairs(n),
    )(*ds)


def _scatter_chips(ps):
    n = len(ps)

    def body(*refs):
        p_refs, u_refs, sems = refs[:n], refs[n:2 * n], refs[2 * n:]
        x, y, c, chips = _place()
        cps = [_remote(p_refs[i].at[2 * px + py], u_refs[i].at[j], sems, 3 * i + j, (px, py, c))
               for j, (px, py) in enumerate(chips) for i in range(n)]
        for cp in cps:
            cp.start()
        for cp in cps:
            cp.wait()

    return pl.pallas_call(
        body, name="scatter_chips", in_specs=[ANY] * n, out_specs=[ANY] * n,
        out_shape=[S((3,) + a.shape[1:], a.dtype) for a in ps], scratch_shapes=_sem_pairs(3 * n),
    )(*ps)


def _join_halves(qs):
    n = len(qs)

    def body(*refs):
        o_refs, sems = refs[n:2 * n], refs[2 * n:]
        x, y, c, _ = _place()
        cps = [_remote(o_refs[i].at[c], o_refs[i].at[c], sems, i, (x, y, 1 - c)) for i in range(n)]
        for cp in cps:
            cp.start()
        for i in range(n):
            other = o_refs[i].at[1 - c]
            _remote(other, other, sems, i, (x, y, 1 - c)).wait_recv()
        for cp in cps:
            cp.wait_send()

    return pl.pallas_call(
        body, name="join_halves", in_specs=[ANY] * n, out_specs=[ANY] * n,
        out_shape=[S(a.shape, a.dtype) for a in qs], input_output_aliases={i: i for i in range(n)},
        scratch_shapes=_sem_pairs(n),
    )(*qs)


def _gather_small(v, name):
    def body(v_ref, o_ref, send_sems, recv_sems, local_sem):
        x, y, c, _ = _place()
        me = 4 * x + 2 * y + c
        mine = pltpu.make_async_copy(v_ref, o_ref.at[me], local_sem)
        mine.start()
        cps = []
        for k in range(1, N_DEV):
            fx, fy, fc = (k >> 2) & 1, (k >> 1) & 1, k & 1
            peer = (x ^ fx, y ^ fy, c ^ fc)
            cps.append(pltpu.make_async_remote_copy(
                src_ref=v_ref, dst_ref=o_ref.at[me], send_sem=send_sems.at[k - 1], recv_sem=recv_sems.at[k - 1],
                device_id=peer, device_id_type=MESH))
        for cp in cps:
            cp.start()
        for k in range(1, N_DEV):
            fx, fy, fc = (k >> 2) & 1, (k >> 1) & 1, k & 1
            dst = o_ref.at[4 * (x ^ fx) + 2 * (y ^ fy) + (c ^ fc)]
            pltpu.make_async_remote_copy(src_ref=dst, dst_ref=dst, send_sem=send_sems.at[k - 1],
                                         recv_sem=recv_sems.at[k - 1], device_id=(x, y, c),
                                         device_id_type=MESH).wait_recv()
        for cp in cps:
            cp.wait_send()
        mine.wait()

    return pl.pallas_call(
        body, name=name, in_specs=[ANY], out_specs=ANY, out_shape=S((N_DEV,) + v.shape, F32),
        scratch_shapes=[pltpu.SemaphoreType.DMA((N_DEV - 1,)), pltpu.SemaphoreType.DMA((N_DEV - 1,)),
                        pltpu.SemaphoreType.DMA],
    )(v)


def _sum_leading(a, name):
    n, rows, cols = a.shape
    tm = _row_tile(rows, 640)

    def body(a_ref, o_ref):
        acc = a_ref[0]
        for k in range(1, n):
            acc = acc + a_ref[k]
        o_ref[...] = acc

    return pl.pallas_call(
        body, grid=(rows // tm,), name=name, in_specs=[pl.BlockSpec((n, tm, cols), lambda i: (0, i, 0))],
        out_specs=pl.BlockSpec((tm, cols), lambda i: (i, 0)), out_shape=S((rows, cols), F32),
        compiler_params=_params(("arbitrary",)),
    )(a)


def _add_sibling(place, a, b, name):
    n, hr, cols = b.shape
    tm = _row_tile(hr, 512)

    def body(place_ref, a_ref, b_ref, o_ref, o16_ref):
        tot = a_ref[0] + b_ref[...]
        o_ref[...] = tot
        o16_ref[...] = tot.astype(BF16)

    spec = pl.BlockSpec((1, tm, cols), lambda k, i, pr: (k, i, 0))
    return pl.pallas_call(
        body, name=name, out_shape=[S((n, hr, cols), F32), S((n, hr, cols), BF16)],
        grid_spec=pltpu.PrefetchScalarGridSpec(
            num_scalar_prefetch=1, grid=(n, hr // tm),
            in_specs=[pl.BlockSpec((1, 1, tm, cols), lambda k, i, pr: (k, pr[0], i, 0)), spec], out_specs=[spec, spec]),
        compiler_params=_params(("arbitrary", "arbitrary")),
    )(place, a, b)


def _add_chips(place, sums, others, name):
    _, hr, cols = sums.shape
    tm = _row_tile(hr, 512)

    def body(place_ref, m_ref, o_ref, out_ref):
        acc = m_ref[0]
        for k in range(others.shape[0]):
            acc = acc + o_ref[k].astype(F32)
        out_ref[0] = acc

    return pl.pallas_call(
        body, name=name, out_shape=S((2, hr, cols), F32),
        grid_spec=pltpu.PrefetchScalarGridSpec(
            num_scalar_prefetch=1, grid=(hr // tm,),
            in_specs=[pl.BlockSpec((1, tm, cols), lambda i, pr: (pr[1], i, 0)),
                      pl.BlockSpec((others.shape[0], tm, cols), lambda i, pr: (0, i, 0))],
            out_specs=pl.BlockSpec((1, tm, cols), lambda i, pr: (pr[0], i, 0))),
        compiler_params=_params(("arbitrary",)),
    )(place, sums, others)


def _adamw(wt, g, m, v, name):
    shape = wt.shape
    cols = shape[-1]
    rows = math.prod(shape[:-1])
    tm = rows
    for cand in (512, 256, 128, 64, 32, 16, 8):
        if rows % cand == 0:
            tm = cand
            break
    c1 = 1.0 - ADAM_B1 ** ADAM_STEP
    c2 = 1.0 - ADAM_B2 ** ADAM_STEP

    def body(w_ref, g_ref, m_ref, v_ref, d_ref, nm_ref, nv_ref):
        gv = g_ref[...]
        nm = ADAM_B1 * m_ref[...] + (1.0 - ADAM_B1) * gv
        nv = ADAM_B2 * v_ref[...] + (1.0 - ADAM_B2) * (gv * gv)
        d_ref[...] = -ADAM_LR * ((nm / c1) / (jnp.sqrt(nv / c2) + ADAM_EPS) + ADAM_WD * w_ref[...])
        nm_ref[...] = nm
        nv_ref[...] = nv

    spec = pl.BlockSpec((tm, cols), lambda i: (i, 0))
    outs = pl.pallas_call(
        body, grid=(rows // tm,), name=name, in_specs=[spec] * 4, out_specs=[spec] * 3,
        out_shape=[S((rows, cols), F32)] * 3, compiler_params=_params(("arbitrary",)),
    )(*[a.reshape(rows, cols) for a in (wt, g, m, v)])
    return [o.reshape(shape) for o in outs]


WEIGHTS = ('pre_mix_norm', 'post_mix_norm', 'pre_ffn_norm', 'post_ffn_norm', 'w_in', 'w_out', 'attn_sinks', 'ssd_conv_w',
           'ssd_conv_b', 'ssd_dt_bias', 'ssd_A_log', 'ssd_D', 'ssd_norm_w', 'gdn_conv_w', 'gdn_dt_bias', 'gdn_A_log',
           'gdn_norm_w', 'ffn_w_gate', 'ffn_w_up', 'ffn_w_down')


def _chip_pieces(raw, shapes):
    def cut(per_layer, axis, width):
        return jnp.stack([jnp.stack([lax.slice_in_dim(g, k * width, (k + 1) * width, axis=axis) for g in per_layer])
                          for k in range(N_CHIPS)])

    w_out = [jnp.concatenate([g[512:768], g[0:512], g[768:1024]], axis=0) for g in raw["w_out_cat"]]
    return {
        "w_in": cut([_unpad_cols(g) for g in raw["w_in_pad"]], 1, shapes["w_in"][2]),
        "w_out": cut(w_out, 0, shapes["w_out"][1]),
        "ffn_w_gate": cut([g[:, :FF] for g in raw["w_gu"]], 1, shapes["ffn_w_gate"][2]),
        "ffn_w_up": cut([g[:, FF:] for g in raw["w_gu"]], 1, shapes["ffn_w_up"][2]),
        "ffn_w_down": cut(raw["ffn_w_down"], 0, shapes["ffn_w_down"][1]),
    }


def _step(x, target, wts, ms, vs):
    chip = 2 * lax.axis_index("x") + lax.axis_index("y")
    place = jnp.stack([lax.axis_index("c"), chip]).astype(jnp.int32)
    big_names = [k for k, _ in BIG]

    shards16 = [wts[k].astype(BF16) for k in big_names]
    gathered = _gather_weights(shards16)
    big = {k: lax.dynamic_update_index_in_dim(g, s, chip, 0) for k, g, s in zip(big_names, gathered, shards16)}
    conv = _gather_small(_pack_small([wts[k] for k in CONV]), "gather_conv_weights")
    conv = [_unpack_small(conv[2 * k], [wts[n].shape for n in CONV]) for k in range(N_CHIPS)]
    w_all = dict(wts)
    for i, n in enumerate(CONV):
        w_all[n] = jnp.concatenate([conv[k][i] for k in range(N_CHIPS)], axis=2)

    loss_part, grad_x, small_g, raw = _local_step(x[0], target[0], [_layer_weights(w_all, big, l) for l in range(DEPTH)])

    pieces = _chip_pieces(raw, {k: wts[k].shape for k in big_names})
    pieces = [pieces[k] for k in big_names]
    sums = [_add_sibling(place, p, t, f"add_sibling_{k}") for k, p, t in zip(big_names, pieces, _exchange_halves(pieces))]
    others = _scatter_chips([s16 for _, s16 in sums])
    halves = [_add_chips(place, s32, o, f"add_chips_{k}") for k, (s32, _), o in zip(big_names, sums, others)]
    g_all = dict(zip(big_names, _join_halves(halves)))

    names = SMALL + CONV
    packed = _pack_small([small_g[k] for k in names] + [loss_part])
    small_sum = _sum_leading(_gather_small(packed, "gather_small_grads"), "add_small")
    vals = _unpack_small(small_sum, [small_g[k].shape for k in names] + [(1, LANES)])
    loss = vals[-1][0, 0]
    for k, v in zip(names, vals[:-1]):
        if k in CONV:
            width = wts[k].shape[2]
            v = lax.dynamic_slice_in_dim(v.reshape(DEPTH, 4, -1), chip * width, width, axis=2)
        g_all[k] = v.reshape(wts[k].shape)

    shapes = [wts[k].shape for k in names]
    d_s, m_s, v_s = _adamw(_pack_small([wts[k] for k in names]), _pack_small([g_all[k] for k in names]),
                           _pack_small([ms[k] for k in names]), _pack_small([vs[k] for k in names]), "adamw_small")
    upd = dict(zip(names, zip(_unpack_small(d_s, shapes), _unpack_small(m_s, shapes), _unpack_small(v_s, shapes))))
    for k in big_names:
        upd[k] = _adamw(wts[k], g_all[k], ms[k], vs[k], f"adamw_{k}")
    return (loss, grad_x[None], *[g_all[k] for k in WEIGHTS], *[upd[k][0] for k in WEIGHTS],
            *[upd[k][1] for k in WEIGHTS], *[upd[k][2] for k in WEIGHTS])


def kernel(x, pre_mix_norm, post_mix_norm, pre_ffn_norm, post_ffn_norm, w_in, w_out, attn_sinks, ssd_conv_w, ssd_conv_b, ssd_dt_bias, ssd_A_log, ssd_D, ssd_norm_w, gdn_conv_w, gdn_dt_bias, gdn_A_log, gdn_norm_w, ffn_w_gate, ffn_w_up, ffn_w_down, loss_target, m_pre_mix_norm, m_post_mix_norm, m_pre_ffn_norm, m_post_ffn_norm, m_w_in, m_w_out, m_attn_sinks, m_ssd_conv_w, m_ssd_conv_b, m_ssd_dt_bias, m_ssd_A_log, m_ssd_D, m_ssd_norm_w, m_gdn_conv_w, m_gdn_dt_bias, m_gdn_A_log, m_gdn_norm_w, m_ffn_w_gate, m_ffn_w_up, m_ffn_w_down, v_pre_mix_norm, v_post_mix_norm, v_pre_ffn_norm, v_post_ffn_norm, v_w_in, v_w_out, v_attn_sinks, v_ssd_conv_w, v_ssd_conv_b, v_ssd_dt_bias, v_ssd_A_log, v_ssd_D, v_ssd_norm_w, v_gdn_conv_w, v_gdn_dt_bias, v_gdn_A_log, v_gdn_norm_w, v_ffn_w_gate, v_ffn_w_up, v_ffn_w_down):
    wts = dict(zip(WEIGHTS, (pre_mix_norm, post_mix_norm, pre_ffn_norm, post_ffn_norm, w_in, w_out, attn_sinks, ssd_conv_w, ssd_conv_b, ssd_dt_bias, ssd_A_log, ssd_D, ssd_norm_w, gdn_conv_w, gdn_dt_bias, gdn_A_log, gdn_norm_w, ffn_w_gate, ffn_w_up, ffn_w_down)))
    ms = dict(zip(WEIGHTS, (m_pre_mix_norm, m_post_mix_norm, m_pre_ffn_norm, m_post_ffn_norm, m_w_in, m_w_out, m_attn_sinks, m_ssd_conv_w, m_ssd_conv_b, m_ssd_dt_bias, m_ssd_A_log, m_ssd_D, m_ssd_norm_w, m_gdn_conv_w, m_gdn_dt_bias, m_gdn_A_log, m_gdn_norm_w, m_ffn_w_gate, m_ffn_w_up, m_ffn_w_down)))
    vs = dict(zip(WEIGHTS, (v_pre_mix_norm, v_post_mix_norm, v_pre_ffn_norm, v_post_ffn_norm, v_w_in, v_w_out, v_attn_sinks, v_ssd_conv_w, v_ssd_conv_b, v_ssd_dt_bias, v_ssd_A_log, v_ssd_D, v_ssd_norm_w, v_gdn_conv_w, v_gdn_dt_bias, v_gdn_A_log, v_gdn_norm_w, v_ffn_w_gate, v_ffn_w_up, v_ffn_w_down)))
    return _step(x, loss_target, wts, ms, vs)
```

```python
import functools
import math

import jax
import jax.numpy as jnp
from jax import lax
from jax.experimental import pallas as pl
from jax.experimental.pallas import tpu as pltpu

F32, BF16 = jnp.float32, jnp.bfloat16
HI = lax.Precision.HIGHEST
MESH = pl.DeviceIdType.MESH
S = jax.ShapeDtypeStruct

D_MODEL = 1024
DEPTH = 2
CHUNK = 64
EPS = 1e-6
FF = 2816
N_CHIPS = 4
N_DEV = 8
LANES = 128

VMEM_LIMIT_BYTES = 56 * 1024 * 1024

PC_GQKV, PC_GZ, PC_XBC, PC_ATT, PC_SZ, PC_DT, PC_BA, PC_TOT = 0, 768, 1024, 2048, 2560, 3072, 3200, 3328

ADAM_LR, ADAM_B1, ADAM_B2, ADAM_EPS, ADAM_WD, ADAM_STEP = 0.001, 0.9, 0.999, 1e-08, 0.01, 10

ALIBI_SLOPES = tuple(2.0 ** (-8.0 * (h + 1) / 4) for h in range(4))


def _params(sem=None, **kw):
    if sem is not None:
        kw["dimension_semantics"] = sem
    return pltpu.CompilerParams(vmem_limit_bytes=VMEM_LIMIT_BYTES, **kw)


def _dot(a, b, prec=None):
    return jnp.dot(a, b, precision=prec, preferred_element_type=F32)


def _dot_nt(a, b, prec=None):
    return lax.dot_general(a, b, (((1,), (1,)), ((), ())), precision=prec, preferred_element_type=F32)


def _dot_tn(a, b, prec=None):
    return lax.dot_general(a, b, (((0,), (0,)), ((), ())), precision=prec, preferred_element_type=F32)


def _iota2(n, m):
    return lax.broadcasted_iota(jnp.int32, (n, m), 0), lax.broadcasted_iota(jnp.int32, (n, m), 1)


def _pick_col(arr, idx):
    ci = lax.broadcasted_iota(jnp.int32, arr.shape, 1)
    return jnp.sum(jnp.where(ci == idx, arr, 0.0), axis=1, keepdims=True)


def _pick_row(arr, idx):
    ri = lax.broadcasted_iota(jnp.int32, arr.shape, 0)
    return jnp.sum(jnp.where(ri == idx, arr, 0.0), axis=0, keepdims=True)


def _col_to_row(col, eye):
    return jnp.sum(eye * col, axis=0, keepdims=True)


def _rms(x, w):
    return x * lax.rsqrt(jnp.mean(x * x, axis=-1, keepdims=True) + EPS) * w


def _mm_nn(a, b, tm, tn, out_dtype, name):
    m, k = a.shape
    n = b.shape[1]
    tm, tn = min(tm, m), min(tn, n)

    def body(a_ref, b_ref, o_ref):
        o_ref[...] = _dot(a_ref[...], b_ref[...]).astype(o_ref.dtype)

    return pl.pallas_call(
        body, grid=(n // tn, m // tm), name=name,
        in_specs=[pl.BlockSpec((tm, k), lambda j, i: (i, 0)), pl.BlockSpec((k, tn), lambda j, i: (0, j))],
        out_specs=pl.BlockSpec((tm, tn), lambda j, i: (i, j)),
        out_shape=S((m, n), out_dtype), compiler_params=_params(("arbitrary", "arbitrary")),
    )(a, b)


def _mm_nt(a, b, tm, tn, out_dtype, name):
    m, k = a.shape
    n = b.shape[0]
    tm, tn = min(tm, m), min(tn, n)

    def body(a_ref, b_ref, o_ref):
        o_ref[...] = _dot_nt(a_ref[...], b_ref[...]).astype(o_ref.dtype)

    return pl.pallas_call(
        body, grid=(n // tn, m // tm), name=name,
        in_specs=[pl.BlockSpec((tm, k), lambda j, i: (i, 0)), pl.BlockSpec((tn, k), lambda j, i: (j, 0))],
        out_specs=pl.BlockSpec((tm, tn), lambda j, i: (i, j)),
        out_shape=S((m, n), out_dtype), compiler_params=_params(("arbitrary", "arbitrary")),
    )(a, b)


def _mm_tn(a, b, tm, tn, tk, name):
    t, m = a.shape
    n = b.shape[1]
    tm, tn, tk = min(tm, m), min(tn, n), min(tk, t)

    def body(a_ref, b_ref, o_ref):
        part = _dot_tn(a_ref[...], b_ref[...])

        @pl.when(pl.program_id(2) == 0)
        def _():
            o_ref[...] = part

        @pl.when(pl.program_id(2) > 0)
        def _():
            o_ref[...] += part

    return pl.pallas_call(
        body, grid=(m // tm, n // tn, t // tk), name=name,
        in_specs=[pl.BlockSpec((tk, tm), lambda i, j, k: (k, i)), pl.BlockSpec((tk, tn), lambda i, j, k: (k, j))],
        out_specs=pl.BlockSpec((tm, tn), lambda i, j, k: (i, j)),
        out_shape=S((m, n), F32), compiler_params=_params(("arbitrary", "arbitrary", "arbitrary")),
    )(a, b)


def _rowcall(fn, rows, params, row_outs, acc_outs, name, tm=512):
    t = rows[0].shape[0]
    tm = min(tm, t)
    n_in = len(rows) + len(params)
    n_ro = len(row_outs)

    def body(*refs):
        ro, ao = fn(*[r[...] for r in refs[:n_in]])
        for ref, v in zip(refs[n_in:n_in + n_ro], ro):
            ref[...] = v.astype(ref.dtype)
        acc_refs = refs[n_in + n_ro:]
        if acc_refs:
            @pl.when(pl.program_id(0) == 0)
            def _():
                for ref, v in zip(acc_refs, ao):
                    ref[...] = v

            @pl.when(pl.program_id(0) > 0)
            def _():
                for ref, v in zip(acc_refs, ao):
                    ref[...] += v

    in_specs = [pl.BlockSpec((tm, r.shape[1]), lambda i: (i, 0)) for r in rows]
    in_specs += [pl.BlockSpec(p.shape, lambda i: (0, 0)) for p in params]
    out_specs = [pl.BlockSpec((tm, c), lambda i: (i, 0)) for c, _ in row_outs]
    out_specs += [pl.BlockSpec(shape, lambda i: (0, 0)) for shape in acc_outs]
    out_shape = [S((t, c), dt) for c, dt in row_outs] + [S(shape, F32) for shape in acc_outs]
    return pl.pallas_call(
        body, grid=(t // tm,), name=name, in_specs=in_specs, out_specs=out_specs, out_shape=out_shape,
        compiler_params=_params(("arbitrary",)),
    )(*rows, *params)


def _prenorm(x, w, name):
    def fn(x, w):
        return (_rms(x, w),), ()
    return _rowcall(fn, [x], [w], [(D_MODEL, BF16)], [], name)[0]


def _resid_norm(xin, m, w_post, w_next, name):
    def fn(xin, m, w_post, w_next):
        xo = xin + _rms(m, w_post)
        return (xo, _rms(xo, w_next)), ()
    return _rowcall(fn, [xin, m], [w_post, w_next], [(D_MODEL, F32), (D_MODEL, BF16)], [], name)


def _resid_loss(xin, m, w_post, target, name):
    def fn(xin, m, target, w_post):
        r, vjp = jax.vjp(_rms, m, w_post)
        err = xin + r - target
        dy = err * (1.0 / D_MODEL)
        dm, dw = vjp(dy)
        tot = jnp.sum(jnp.sum(err * err, axis=1, keepdims=True), axis=0, keepdims=True) * (0.5 / D_MODEL)
        lane = lax.broadcasted_iota(jnp.int32, (1, LANES), 1)
        return (dy, dm), (jnp.where(lane == 0, tot, 0.0), dw)
    return _rowcall(fn, [xin, m, target], [w_post], [(D_MODEL, F32), (D_MODEL, BF16)],
                    [(1, LANES), (1, D_MODEL)], name)


def _resid_norm_bwd(x_out, m, d_direct, dh, w_post, w_next, name):
    def fn(x_out, m, d_direct, dh, w_post, w_next):
        _, vjp_n = jax.vjp(_rms, x_out, w_next)
        dx, dwn = vjp_n(dh)
        d_total = d_direct + dx
        _, vjp_p = jax.vjp(_rms, m, w_post)
        dm, dwp = vjp_p(d_total)
        return (d_total, dm), (dwn, dwp)
    return _rowcall(fn, [x_out, m, d_direct, dh], [w_post, w_next], [(D_MODEL, F32), (D_MODEL, BF16)],
                    [(1, D_MODEL), (1, D_MODEL)], name)


def _prenorm_bwd(x, d_direct, dh, w, name):
    def fn(x, d_direct, dh, w):
        _, vjp = jax.vjp(_rms, x, w)
        dx, dw = vjp(dh)
        return (d_direct + dx,), (dw,)
    return _rowcall(fn, [x, d_direct, dh], [w], [(D_MODEL, F32)], [(1, D_MODEL)], name)


def _swiglu_fn(gu):
    return jax.nn.silu(gu[:, :FF]) * gu[:, FF:]


def _swiglu(gu, name):
    def fn(gu):
        return (_swiglu_fn(gu),), ()
    return _rowcall(fn, [gu], [], [(FF, BF16)], [], name, tm=256)[0]


def _swiglu_bwd(gu, da, name):
    def fn(gu, da):
        _, vjp = jax.vjp(_swiglu_fn, gu)
        return (vjp(da)[0],), ()
    return _rowcall(fn, [gu, da], [], [(2 * FF, BF16)], [], name, tm=256)[0]


def _conv_fwd(proj, col0, width, w, b, name, tm=512):
    t = proj.shape[0]
    tm = min(tm, t)
    cb = col0 // width

    def body(x_ref, w_ref, b_ref, o_ref, ext):
        @pl.when(pl.program_id(0) == 0)
        def _():
            ext[0:8, :] = jnp.zeros((8, width), F32)

        ext[8:8 + tm, :] = x_ref[...]
        y = b_ref[...] + w_ref[0:1, :] * ext[pl.ds(5, tm), :]
        for k in range(1, 4):
            y = y + w_ref[k:k + 1, :] * ext[pl.ds(5 + k, tm), :]
        o_ref[...] = jax.nn.silu(y)
        ext[0:8, :] = ext[tm:tm + 8, :]

    return pl.pallas_call(
        body, grid=(t // tm,), name=name,
        in_specs=[pl.BlockSpec((tm, width), lambda i: (i, cb)), pl.BlockSpec((4, width), lambda i: (0, 0)),
                  pl.BlockSpec((1, width), lambda i: (0, 0))],
        out_specs=pl.BlockSpec((tm, width), lambda i: (i, 0)),
        out_shape=S((t, width), F32), scratch_shapes=[pltpu.VMEM((tm + 8, width), F32)],
        compiler_params=_params(("arbitrary",)),
    )(proj, w, b)


def _conv_bwd(proj, col0, width, w, b, dact, name, tm=512):
    t = proj.shape[0]
    tm = min(tm, t)
    nb = t // tm
    cb = col0 // width
    hb = tm // 8

    def body(x_ref, halo_ref, d_ref, w_ref, b_ref, dx_ref, dw_ref, db_ref, extx, extd):
        i = pl.program_id(0)
        blk = nb - 1 - i

        @pl.when(i == 0)
        def _():
            extd[tm:tm + 8, :] = jnp.zeros((8, width), F32)
            dw_ref[...] = jnp.zeros((4, width), F32)
            db_ref[...] = jnp.zeros((1, width), F32)

        extx[0:8, :] = jnp.where(blk == 0, 0.0, halo_ref[...])
        extx[8:8 + tm, :] = x_ref[...]
        y = b_ref[...] + w_ref[0:1, :] * extx[pl.ds(5, tm), :]
        for k in range(1, 4):
            y = y + w_ref[k:k + 1, :] * extx[pl.ds(5 + k, tm), :]
        sig = jax.nn.sigmoid(y)
        dy = d_ref[...] * (sig * (1.0 + y * (1.0 - sig)))
        extd[0:tm, :] = dy
        dx = w_ref[0:1, :] * extd[pl.ds(3, tm), :]
        for k in range(1, 4):
            dx = dx + w_ref[k:k + 1, :] * extd[pl.ds(3 - k, tm), :]
        dx_ref[...] = dx.astype(dx_ref.dtype)
        for k in range(4):
            dw_ref[k:k + 1, :] += jnp.sum(dy * extx[pl.ds(5 + k, tm), :], axis=0, keepdims=True)
        db_ref[...] += jnp.sum(dy, axis=0, keepdims=True)
        extd[tm:tm + 8, :] = extd[0:8, :]

    return pl.pallas_call(
        body, grid=(nb,), name=name,
        in_specs=[pl.BlockSpec((tm, width), lambda i: (nb - 1 - i, cb)),
                  pl.BlockSpec((8, width), lambda i: (jnp.maximum((nb - 1 - i) * hb - 1, 0), cb)),
                  pl.BlockSpec((tm, width), lambda i: (nb - 1 - i, 0)),
                  pl.BlockSpec((4, width), lambda i: (0, 0)), pl.BlockSpec((1, width), lambda i: (0, 0))],
        out_specs=[pl.BlockSpec((tm, width), lambda i: (nb - 1 - i, 0)), pl.BlockSpec((4, width), lambda i: (0, 0)),
                   pl.BlockSpec((1, width), lambda i: (0, 0))],
        out_shape=[S((t, width), BF16), S((4, width), F32), S((1, width), F32)],
        scratch_shapes=[pltpu.VMEM((tm + 8, width), F32), pltpu.VMEM((tm + 8, width), F32)],
        compiler_params=_params(("arbitrary",)),
    )(proj, proj, dact, w, b)


SWA_BQ = 256
SWA_BACK = 128


def _swa_block(q, kw, vw, sinks, blk):
    nq, nk = SWA_BQ, SWA_BQ + SWA_BACK
    r, j = _iota2(nq, nk)
    rel = r // CHUNK + 2 - j // CHUNK
    valid = (rel >= 0) & (rel <= 2) & (blk * (SWA_BQ // CHUNK) + j // CHUNK - 2 >= 0)
    dist = jnp.abs(r + SWA_BACK - j).astype(F32)
    outs = []
    for h in range(4):
        kv = h // 2
        qh = q[:, 64 * h:64 * h + 64]
        kh = kw[:, 64 * kv:64 * kv + 64]
        vh = vw[:, 64 * kv:64 * kv + 64]
        s = _dot_nt(qh, kh) * 0.125 - ALIBI_SLOPES[h] * dist
        s = jnp.where(valid, s, -1e30)
        sink = _pick_col(sinks, h)
        m = jnp.maximum(jnp.max(s, axis=1, keepdims=True), sink)
        e = jnp.exp(s - m)
        den = jnp.sum(e, axis=1, keepdims=True) + jnp.exp(sink - m)
        outs.append(_dot(e / den, vh))
    return jnp.concatenate(outs, axis=1)


def _swa_fwd(proj, sinks, name):
    t = proj.shape[0]
    qb, kb = PC_ATT // 256, PC_ATT // 128 + 2
    win = SWA_BQ + SWA_BACK

    def body(q_ref, k_ref, v_ref, s_ref, o_ref, kp, vp):
        i = pl.program_id(0)

        @pl.when(i == 0)
        def _():
            kp[0:SWA_BACK, :] = jnp.zeros((SWA_BACK, 128), F32)
            vp[0:SWA_BACK, :] = jnp.zeros((SWA_BACK, 128), F32)
            kp[SWA_BACK:, :] = k_ref[...]
            vp[SWA_BACK:, :] = v_ref[...]

        start = pl.multiple_of(i * SWA_BQ, SWA_BQ)
        o = _swa_block(q_ref[...], kp[pl.ds(start, win), :], vp[pl.ds(start, win), :], s_ref[...], i)
        o_ref[...] = o.astype(o_ref.dtype)

    return pl.pallas_call(
        body, grid=(t // SWA_BQ,), name=name,
        in_specs=[pl.BlockSpec((SWA_BQ, 256), lambda i: (i, qb)), pl.BlockSpec((t, 128), lambda i: (0, kb)),
                  pl.BlockSpec((t, 128), lambda i: (0, kb + 1)), pl.BlockSpec((1, LANES), lambda i: (0, 0))],
        out_specs=pl.BlockSpec((SWA_BQ, 256), lambda i: (i, 0)),
        out_shape=S((t, 256), BF16),
        scratch_shapes=[pltpu.VMEM((t + SWA_BACK, 128), F32), pltpu.VMEM((t + SWA_BACK, 128), F32)],
        compiler_params=_params(("arbitrary",)),
    )(proj, proj, proj, sinks)


def _swa_bwd(proj, sinks, dcat, dcol0, name):
    t = proj.shape[0]
    nb = t // SWA_BQ
    qb, kb = PC_ATT // 256, PC_ATT // 128 + 2
    db = dcol0 // 256
    win = SWA_BQ + SWA_BACK

    def body(q_ref, k_ref, v_ref, s_ref, do_ref, dq_ref, dk_ref, dv_ref, ds_ref, kp, vp, dkp, dvp):
        i = pl.program_id(0)

        @pl.when(i == 0)
        def _():
            kp[0:SWA_BACK, :] = jnp.zeros((SWA_BACK, 128), F32)
            vp[0:SWA_BACK, :] = jnp.zeros((SWA_BACK, 128), F32)
            kp[SWA_BACK:, :] = k_ref[...]
            vp[SWA_BACK:, :] = v_ref[...]
            dkp[...] = jnp.zeros_like(dkp)
            dvp[...] = jnp.zeros_like(dvp)
            ds_ref[...] = jnp.zeros_like(ds_ref)

        start = pl.multiple_of(i * SWA_BQ, SWA_BQ)
        _, vjp = jax.vjp(functools.partial(_swa_block, blk=i), q_ref[...], kp[pl.ds(start, win), :],
                         vp[pl.ds(start, win), :], s_ref[...])
        dq, dkw, dvw, dsk = vjp(do_ref[...])
        dq_ref[...] = dq.astype(dq_ref.dtype)
        dkp[pl.ds(start, win), :] += dkw
        dvp[pl.ds(start, win), :] += dvw
        ds_ref[...] += dsk

        @pl.when(i == nb - 1)
        def _():
            dk_ref[...] = dkp[SWA_BACK:, :].astype(dk_ref.dtype)
            dv_ref[...] = dvp[SWA_BACK:, :].astype(dv_ref.dtype)

    return pl.pallas_call(
        body, grid=(nb,), name=name,
        in_specs=[pl.BlockSpec((SWA_BQ, 256), lambda i: (i, qb)), pl.BlockSpec((t, 128), lambda i: (0, kb)),
                  pl.BlockSpec((t, 128), lambda i: (0, kb + 1)), pl.BlockSpec((1, LANES), lambda i: (0, 0)),
                  pl.BlockSpec((SWA_BQ, 256), lambda i: (i, db))],
        out_specs=[pl.BlockSpec((SWA_BQ, 256), lambda i: (i, 0)), pl.BlockSpec((t, 128), lambda i: (0, 0)),
                   pl.BlockSpec((t, 128), lambda i: (0, 0)), pl.BlockSpec((1, LANES), lambda i: (0, 0))],
        out_shape=[S((t, 256), BF16), S((t, 128), BF16), S((t, 128), BF16), S((1, LANES), F32)],
        scratch_shapes=[pltpu.VMEM((t + SWA_BACK, 128), F32) for _ in range(4)],
        compiler_params=_params(("arbitrary",)),
    )(proj, proj, proj, sinks, dcat)


def _ssd_chunk(z, xbc, dt_raw, state, dtb, alog, dsk, nw):
    n = CHUNK
    r, c = _iota2(n, n)
    tril = r >= c
    eye = (r == c).astype(F32)
    dt = jax.nn.softplus(dt_raw + dtb)
    acs = _dot(tril.astype(F32), dt * (-jnp.exp(alog)), HI)
    xs, bm, cm = xbc[:, :512], xbc[:, 512:768], xbc[:, 768:1024]
    heads = range(8)
    bg = [bm[:, 128 * g:128 * g + 128] for g in range(2)]
    cg = [cm[:, 128 * g:128 * g + 128] for g in range(2)]
    cb = [_dot_nt(cg[g], bg[g]) for g in range(2)]
    dth = [_pick_col(dt, h) for h in heads]
    acol = [_pick_col(acs, h) for h in heads]
    arow = [_col_to_row(a, eye) for a in acol]
    lmat = [jnp.where(tril, jnp.exp(jnp.where(tril, a - b, 0.0)), 0.0) for a, b in zip(acol, arow)]
    xh = [xs[:, 64 * h:64 * h + 64] for h in heads]
    xc = [x * t for x, t in zip(xh, dth)]
    st = [state[64 * h:64 * h + 64, :] for h in heads]
    alast = [_pick_row(a, n - 1) for a in acol]
    y_in = [_dot(cb[h // 4] * lmat[h], xc[h]) for h in heads]
    y_st = [_dot_nt(cg[h // 4], st[h]) * jnp.exp(acol[h]) for h in heads]
    ys = [y_in[h] + y_st[h] + xh[h] * _pick_col(dsk, h) for h in heads]
    new_states = [st[h] * jnp.exp(alast[h]) + _dot_tn(xc[h] * jnp.exp(alast[h] - acol[h]), bg[h // 4]) for h in heads]
    gg = jnp.concatenate(ys, axis=1) * jax.nn.silu(z)
    outs = []
    for gi in range(2):
        gv = gg[:, 256 * gi:256 * gi + 256]
        outs.append(gv * lax.rsqrt(jnp.mean(gv * gv, axis=-1, keepdims=True) + EPS))
    return jnp.concatenate(outs, axis=1) * nw, jnp.concatenate(new_states, axis=0)


def _ssd_fwd(proj, xbc, dtb, alog, dsk, nw, name):
    t = proj.shape[0]
    nc = t // CHUNK

    def body(z_ref, x_ref, dt_ref, dtb_ref, al_ref, d_ref, nw_ref, o_ref, st_ref, state):
        @pl.when(pl.program_id(0) == 0)
        def _():
            state[...] = jnp.zeros_like(state)

        st_ref[0] = state[...]
        o, ns = _ssd_chunk(z_ref[...], x_ref[...], dt_ref[...], state[...], dtb_ref[...], al_ref[...], d_ref[...],
                           nw_ref[...])
        o_ref[...] = o.astype(o_ref.dtype)
        state[...] = ns

    vec = pl.BlockSpec((1, LANES), lambda i: (0, 0))
    return pl.pallas_call(
        body, grid=(nc,), name=name,
        in_specs=[pl.BlockSpec((CHUNK, 512), lambda i: (i, PC_SZ // 512)), pl.BlockSpec((CHUNK, 1024), lambda i: (i, 0)),
                  pl.BlockSpec((CHUNK, 128), lambda i: (i, PC_DT // 128)), vec, vec, vec,
                  pl.BlockSpec((1, 512), lambda i: (0, 0))],
        out_specs=[pl.BlockSpec((CHUNK, 512), lambda i: (i, 0)), pl.BlockSpec((1, 512, 128), lambda i: (i, 0, 0))],
        out_shape=[S((t, 512), BF16), S((nc, 512, 128), F32)],
        scratch_shapes=[pltpu.VMEM((512, 128), F32)],
        compiler_params=_params(("arbitrary",)),
    )(proj, xbc, proj, dtb, alog, dsk, nw)


def _ssd_bwd(proj, xbc, states, dtb, alog, dsk, nw, dcat, dcol0, name):
    t = proj.shape[0]
    nc = t // CHUNK
    db = dcol0 // 512

    def body(z_ref, x_ref, dt_ref, st_ref, dtb_ref, al_ref, d_ref, nw_ref, do_ref,
             dz_ref, dx_ref, ddt_ref, gdtb_ref, gal_ref, gd_ref, gnw_ref, dstate):
        @pl.when(pl.program_id(0) == 0)
        def _():
            dstate[...] = jnp.zeros_like(dstate)
            gdtb_ref[...] = jnp.zeros_like(gdtb_ref)
            gal_ref[...] = jnp.zeros_like(gal_ref)
            gd_ref[...] = jnp.zeros_like(gd_ref)
            gnw_ref[...] = jnp.zeros_like(gnw_ref)

        _, vjp = jax.vjp(_ssd_chunk, z_ref[...], x_ref[...], dt_ref[...], st_ref[0], dtb_ref[...], al_ref[...],
                         d_ref[...], nw_ref[...])
        dz, dx, ddt, dst, gdtb, gal, gd, gnw = vjp((do_ref[...], dstate[...]))
        dz_ref[...] = dz.astype(dz_ref.dtype)
        dx_ref[...] = dx
        ddt_ref[...] = ddt.astype(ddt_ref.dtype)
        dstate[...] = dst
        gdtb_ref[...] += gdtb
        gal_ref[...] += gal
        gd_ref[...] += gd
        gnw_ref[...] += gnw

    rev = lambda i: nc - 1 - i
    vec = pl.BlockSpec((1, LANES), lambda i: (0, 0))
    vec512 = pl.BlockSpec((1, 512), lambda i: (0, 0))
    return pl.pallas_call(
        body, grid=(nc,), name=name,
        in_specs=[pl.BlockSpec((CHUNK, 512), lambda i: (rev(i), PC_SZ // 512)),
                  pl.BlockSpec((CHUNK, 1024), lambda i: (rev(i), 0)),
                  pl.BlockSpec((CHUNK, 128), lambda i: (rev(i), PC_DT // 128)),
                  pl.BlockSpec((1, 512, 128), lambda i: (rev(i), 0, 0)), vec, vec, vec, vec512,
                  pl.BlockSpec((CHUNK, 512), lambda i: (rev(i), db))],
        out_specs=[pl.BlockSpec((CHUNK, 512), lambda i: (rev(i), 0)), pl.BlockSpec((CHUNK, 1024), lambda i: (rev(i), 0)),
                   pl.BlockSpec((CHUNK, 128), lambda i: (rev(i), 0)), vec, vec, vec, vec512],
        out_shape=[S((t, 512), BF16), S((t, 1024), F32), S((t, 128), BF16), S((1, LANES), F32), S((1, LANES), F32),
                   S((1, LANES), F32), S((1, 512), F32)],
        scratch_shapes=[pltpu.VMEM((512, 128), F32)],
        compiler_params=_params(("arbitrary",)),
    )(proj, xbc, proj, states, dtb, alog, dsk, nw, dcat)


SOLVE_PREC = lax.Precision.HIGH


def _unit_lower_inverses(nas, known=None):
    def compute(ns):
        if known is not None:
            return tuple(known)
        n = ns[0].shape[0]
        r, c = _iota2(n, n)
        eye = (r == c).astype(F32)
        tm, pw = [eye + a for a in ns], list(ns)
        for _ in range(5):
            pw = [_dot(p, p, SOLVE_PREC) for p in pw]
            tm = [t + _dot(t, p, SOLVE_PREC) for t, p in zip(tm, pw)]
        return tuple(tm)

    inv = jax.custom_vjp(compute)

    def fwd(ns):
        ts = compute(ns)
        return ts, ts

    def bwd(ts, gs):
        part = [_dot_nt(g, t, SOLVE_PREC) for g, t in zip(gs, ts)]
        return (tuple(_dot_tn(t, p, SOLVE_PREC) for t, p in zip(ts, part)),)

    inv.defvjp(fwd, bwd)
    return inv(nas)


def _gdn_chunk(qkv, z, ba, state, dtb, alog, nw, known_inverses=None):
    n = CHUNK
    r, c = _iota2(n, n)
    tril = r >= c
    stril = r > c
    eye = (r == c).astype(F32)
    beta_all = jax.nn.sigmoid(ba)
    gcs = _dot(tril.astype(F32), -jnp.exp(alog) * jax.nn.softplus(ba + dtb), HI)
    heads = range(4)
    qh = [qkv[:, 64 * h:64 * h + 64] for h in heads]
    kh = [qkv[:, 256 + 64 * h:256 + 64 * h + 64] for h in heads]
    vh = [qkv[:, 512 + 64 * h:512 + 64 * h + 64] for h in heads]
    qn = [q * lax.rsqrt(jnp.sum(q * q, axis=-1, keepdims=True) + EPS) * 0.125 for q in qh]
    kn = [k * lax.rsqrt(jnp.sum(k * k, axis=-1, keepdims=True) + EPS) for k in kh]
    beta = [_pick_col(beta_all, h) for h in heads]
    gcol = [_pick_col(gcs, 4 + h) for h in heads]
    grow = [_col_to_row(g, eye) for g in gcol]
    decay = [jnp.where(tril, jnp.exp(jnp.where(tril, gc - gr, 0.0)), 0.0) for gc, gr in zip(gcol, grow)]
    kbeta = [k * b for k, b in zip(kn, beta)]
    kk = [_dot_nt(kb, k) for kb, k in zip(kbeta, kn)]
    qk = [_dot_nt(q, k) * dc for q, k, dc in zip(qn, kn, decay)]
    known = None if known_inverses is None else [known_inverses[64 * h:64 * h + 64, :] for h in heads]
    tms = _unit_lower_inverses(tuple(-jnp.where(stril, x * dc, 0.0) for x, dc in zip(kk, decay)), known)
    rhs = [jnp.concatenate([v * b, kb * jnp.exp(g)], axis=1) for v, b, kb, g in zip(vh, beta, kbeta, gcol)]
    sol = [_dot(t, x, SOLVE_PREC) for t, x in zip(tms, rhs)]
    st = [state[64 * h:64 * h + 64, :] for h in heads]
    v_new = [s_[:, :64] - _dot(s_[:, 64:], s) for s_, s in zip(sol, st)]
    o = [_dot(q * jnp.exp(g), s) + _dot(x, vn) for q, g, s, x, vn in zip(qn, gcol, st, qk, v_new)]
    glast = [_pick_row(g, n - 1) for g in gcol]
    new_states = [s * jnp.exp(gl) + _dot_tn(k * jnp.exp(gl - g), vn)
                  for s, gl, k, g, vn in zip(st, glast, kn, gcol, v_new)]
    o = [x * lax.rsqrt(jnp.mean(x * x, axis=-1, keepdims=True) + EPS) * nw for x in o]
    outs = [x * jax.nn.silu(z[:, 64 * h:64 * h + 64]) for h, x in zip(heads, o)]
    return jnp.concatenate(outs, axis=1), jnp.concatenate(new_states, axis=0), jnp.concatenate(tms, axis=0)


class _Exchange:
    def __init__(self, arrays, out_shape, n_sems, start, finish):
        self.arrays, self.out_shape, self.n_sems, self.start, self.finish = arrays, out_shape, n_sems, start, finish


def _hosted(exchange, n_in, n_out, n_steps, body):
    if exchange is None:
        return body, [], [], [], []
    k_in, k_out = len(exchange.arrays), len(exchange.out_shape)

    def hosted_body(*refs):
        ins, refs = refs[:n_in + k_in], refs[n_in + k_in:]
        outs, scratch = refs[:n_out + k_out], refs[n_out + k_out:]
        sems = scratch[-2:]
        step = pl.program_id(0)

        @pl.when(step == 0)
        def _():
            exchange.start(ins[n_in:], outs[n_out:], sems)

        body(*ins[:n_in], *outs[:n_out], *scratch[:-2])

        @pl.when(step == n_steps - 1)
        def _():
            exchange.finish(ins[n_in:], outs[n_out:], sems)

    return hosted_body, [ANY] * k_in, [ANY] * k_out, list(exchange.out_shape), _sem_pairs(exchange.n_sems)


def _gdn_fwd(proj, qkv, dtb, alog, nw, name, exchange=None):
    t = proj.shape[0]
    nc = t // CHUNK

    def body(q_ref, z_ref, ba_ref, dtb_ref, al_ref, nw_ref, o_ref, st_ref, inv_ref, state):
        @pl.when(pl.program_id(0) == 0)
        def _():
            state[...] = jnp.zeros_like(state)

        st_ref[0] = state[...]
        o, ns, tms = _gdn_chunk(q_ref[...], z_ref[...], ba_ref[...], state[...], dtb_ref[...], al_ref[...], nw_ref[...])
        o_ref[...] = o.astype(o_ref.dtype)
        inv_ref[0] = tms
        state[...] = ns

    body, x_in, x_out, x_shape, x_sems = _hosted(exchange, 6, 3, nc, body)
    vec = pl.BlockSpec((1, LANES), lambda i: (0, 0))
    per_chunk = pl.BlockSpec((1, 256, 64), lambda i: (i, 0, 0))
    return pl.pallas_call(
        body, grid=(nc,), name=name,
        in_specs=[pl.BlockSpec((CHUNK, 768), lambda i: (i, 0)), pl.BlockSpec((CHUNK, 256), lambda i: (i, PC_GZ // 256)),
                  pl.BlockSpec((CHUNK, 128), lambda i: (i, PC_BA // 128)), vec, vec, pl.BlockSpec((1, 64), lambda i: (0, 0))]
        + x_in,
        out_specs=[pl.BlockSpec((CHUNK, 256), lambda i: (i, 0)), per_chunk, per_chunk] + x_out,
        out_shape=[S((t, 256), BF16), S((nc, 256, 64), F32), S((nc, 256, 64), F32)] + x_shape,
        scratch_shapes=[pltpu.VMEM((256, 64), F32)] + x_sems,
        compiler_params=_params(("arbitrary",)),
    )(qkv, proj, proj, dtb, alog, nw, *([] if exchange is None else exchange.arrays))


def _gdn_bwd(proj, qkv, states, inverses, dtb, alog, nw, dcat, dcol0, name, exchange=None):
    t = proj.shape[0]
    nc = t // CHUNK
    db = dcol0 // 256

    def body(q_ref, z_ref, ba_ref, st_ref, inv_ref, dtb_ref, al_ref, nw_ref, do_ref,
             dq_ref, dz_ref, dba_ref, gdtb_ref, gal_ref, gnw_ref, dstate):
        @pl.when(pl.program_id(0) == 0)
        def _():
            dstate[...] = jnp.zeros_like(dstate)
            gdtb_ref[...] = jnp.zeros_like(gdtb_ref)
            gal_ref[...] = jnp.zeros_like(gal_ref)
            gnw_ref[...] = jnp.zeros_like(gnw_ref)

        def chunk(*operands):
            return _gdn_chunk(*operands, known_inverses=inv_ref[0])[:2]

        _, vjp = jax.vjp(chunk, q_ref[...], z_ref[...], ba_ref[...], st_ref[0], dtb_ref[...], al_ref[...], nw_ref[...])
        dq, dz, dba, dst, gdtb, gal, gnw = vjp((do_ref[...], dstate[...]))
        dq_ref[...] = dq
        dz_ref[...] = dz.astype(dz_ref.dtype)
        dba_ref[...] = dba.astype(dba_ref.dtype)
        dstate[...] = dst
        gdtb_ref[...] += gdtb
        gal_ref[...] += gal
        gnw_ref[...] += gnw

    body, x_in, x_out, x_shape, x_sems = _hosted(exchange, 9, 6, nc, body)
    rev = lambda i: nc - 1 - i
    vec = pl.BlockSpec((1, LANES), lambda i: (0, 0))
    vec64 = pl.BlockSpec((1, 64), lambda i: (0, 0))
    per_chunk = pl.BlockSpec((1, 256, 64), lambda i: (rev(i), 0, 0))
    return pl.pallas_call(
        body, grid=(nc,), name=name,
        in_specs=[pl.BlockSpec((CHUNK, 768), lambda i: (rev(i), 0)),
                  pl.BlockSpec((CHUNK, 256), lambda i: (rev(i), PC_GZ // 256)),
                  pl.BlockSpec((CHUNK, 128), lambda i: (rev(i), PC_BA // 128)),
                  per_chunk, per_chunk, vec, vec, vec64,
                  pl.BlockSpec((CHUNK, 256), lambda i: (rev(i), db))] + x_in,
        out_specs=[pl.BlockSpec((CHUNK, 768), lambda i: (rev(i), 0)), pl.BlockSpec((CHUNK, 256), lambda i: (rev(i), 0)),
                   pl.BlockSpec((CHUNK, 128), lambda i: (rev(i), 0)), vec, vec, vec64] + x_out,
        out_shape=[S((t, 768), F32), S((t, 256), BF16), S((t, 128), BF16), S((1, LANES), F32), S((1, LANES), F32),
                   S((1, 64), F32)] + x_shape,
        scratch_shapes=[pltpu.VMEM((256, 64), F32)] + x_sems,
        compiler_params=_params(("arbitrary",)),
    )(qkv, proj, proj, states, inverses, dtb, alog, nw, dcat, *([] if exchange is None else exchange.arrays))


def _pad_cols(w):
    z = jnp.zeros((w.shape[0], 120), w.dtype)
    return jnp.concatenate([w[:, 2056:2824], w[:, 2824:3080], w[:, 1024:2048], w[:, 0:512], w[:, 512:1024],
                            w[:, 2048:2056], z, w[:, 3080:3088], z], axis=1)


def _unpad_cols(g):
    return jnp.concatenate([g[:, PC_ATT:PC_ATT + 512], g[:, PC_SZ:PC_SZ + 512], g[:, PC_XBC:PC_XBC + 1024],
                            g[:, PC_DT:PC_DT + 8], g[:, PC_GQKV:PC_GQKV + 768], g[:, PC_GZ:PC_GZ + 256],
                            g[:, PC_BA:PC_BA + 8]], axis=1)


def _vec128(v, at=0):
    return jnp.zeros((1, LANES), F32).at[0, at:at + v.shape[0]].set(v)


def _in_weight(gathered):
    return _pad_cols(jnp.concatenate([gathered[k].reshape(D_MODEL, -1) for k in range(N_CHIPS)], axis=1))


def _matmul_weights(w_in, gathered, l):
    cols = lambda name: jnp.concatenate([gathered[name][k, l] for k in range(N_CHIPS)], axis=1)
    rows = lambda name: jnp.concatenate([gathered[name][k, l] for k in range(N_CHIPS)], axis=0)
    w_out = rows("w_out")
    return dict(
        w_in=w_in,
        w_out=jnp.concatenate([w_out[256:768], w_out[0:256], w_out[768:1024]], axis=0),
        w_gu=jnp.concatenate([cols("ffn_w_gate"), cols("ffn_w_up")], axis=1),
        w_down=rows("ffn_w_down"))


def _small_operands(w, l):
    return dict(
        pre_mix=w["pre_mix_norm"][l][None], post_mix=w["post_mix_norm"][l][None],
        pre_ffn=w["pre_ffn_norm"][l][None], post_ffn=w["post_ffn_norm"][l][None],
        sinks=_vec128(w["attn_sinks"][l]),
        s_cw=w["ssd_conv_w"][l], s_cb=w["ssd_conv_b"][l][None],
        s_dtb=_vec128(w["ssd_dt_bias"][l]), s_alog=_vec128(w["ssd_A_log"][l]), s_d=_vec128(w["ssd_D"][l]),
        s_nw=w["ssd_norm_w"][l][None],
        g_cw=w["gdn_conv_w"][l], g_cb=jnp.zeros((1, 768), F32),
        g_dtb=_vec128(w["gdn_dt_bias"][l], 4), g_alog=_vec128(w["gdn_A_log"][l], 4), g_nw=w["gdn_norm_w"][l][None],
    )


RAW_GRADS = ("w_in_pad", "w_out_cat", "w_gu", "ffn_w_down")


def _local_step(x, target, lw, w_in0, later_weights, matmul_weights, reduce_early):
    saved = []
    xin = x
    h = _prenorm(x, lw[0]["pre_mix"], "prenorm0")
    for l in range(DEPTH):
        p = lw[l]
        proj = _mm_nn(h, w_in0 if l == 0 else p["w_in"], 512, PC_TOT, F32, f"inproj{l}")
        xbc = _conv_fwd(proj, PC_XBC, 1024, p["s_cw"], p["s_cb"], f"ssd_conv{l}")
        gqkv = _conv_fwd(proj, PC_GQKV, 768, p["g_cw"], p["g_cb"], f"gdn_conv{l}")
        att = _swa_fwd(proj, p["sinks"], f"swa{l}")
        ssd, s_states = _ssd_fwd(proj, xbc, p["s_dtb"], p["s_alog"], p["s_d"], p["s_nw"], f"ssd{l}")
        gdn, g_states, g_inv, *landed = _gdn_fwd(proj, gqkv, p["g_dtb"], p["g_alog"], p["g_nw"], f"gdn{l}",
                                                 later_weights if l == 0 else None)
        if l == 0:
            for q, mw in zip(lw, matmul_weights(landed)):
                q.update(mw)
        cat = jnp.concatenate([ssd, att, gdn], axis=1)
        mix = _mm_nn(cat, p["w_out"], 512, 1024, F32, f"outproj{l}")
        x1, h2 = _resid_norm(xin, mix, p["post_mix"], p["pre_ffn"], f"postmix{l}")
        gu = _mm_nn(h2, p["w_gu"], 512, FF, F32, f"ffn_gu{l}")
        act = _swiglu(gu, f"swiglu{l}")
        f = _mm_nn(act, p["w_down"], 512, 1024, F32, f"ffn_down{l}")
        saved.append(dict(xin=xin, h=h, proj=proj, xbc=xbc, gqkv=gqkv, s_states=s_states, g_states=g_states, g_inv=g_inv,
                          cat=cat, mix=mix, x1=x1, h2=h2, gu=gu, act=act, f=f))
        if l + 1 < DEPTH:
            xin, h = _resid_norm(x1, f, p["post_ffn"], lw[l + 1]["pre_mix"], f"postffn{l}")

    g = {k: [None] * DEPTH for k in SMALL + CONV + RAW_GRADS}
    last = saved[-1]
    d_x2, d_f, loss_part, g["post_ffn_norm"][DEPTH - 1] = _resid_loss(
        last["x1"], last["f"], lw[-1]["post_ffn"], target, "loss")
    carried, early = None, []
    for l in reversed(range(DEPTH)):
        p, s = lw[l], saved[l]
        d_act = _mm_nt(d_f, p["w_down"], 512, FF, F32, f"d_act{l}")
        g["ffn_w_down"][l] = _mm_tn(s["act"], d_f, 1408, 1024, 512, f"dw_down{l}")
        d_gu = _swiglu_bwd(s["gu"], d_act, f"d_swiglu{l}")
        d_h2 = _mm_nt(d_gu, p["w_gu"], 256, 1024, F32, f"d_h2{l}")
        g["w_gu"][l] = _mm_tn(s["h2"], d_gu, 512, FF, 512, f"dw_gu{l}")
        d_x1, d_mix, g["pre_ffn_norm"][l], g["post_mix_norm"][l] = _resid_norm_bwd(
            s["x1"], s["mix"], d_x2, d_h2, p["post_mix"], p["pre_ffn"], f"d_postmix{l}")
        d_cat = _mm_nt(d_mix, p["w_out"], 512, 1024, F32, f"d_cat{l}")
        g["w_out_cat"][l] = _mm_tn(s["cat"], d_mix, 512, 1024, 512, f"dw_out{l}")
        d_q, d_k, d_v, g_sinks = _swa_bwd(s["proj"], p["sinks"], d_cat, 512, f"d_swa{l}")
        d_sz, d_xbc, d_dt, g_dtb, g_alog, g_d, g["ssd_norm_w"][l] = _ssd_bwd(
            s["proj"], s["xbc"], s["s_states"], p["s_dtb"], p["s_alog"], p["s_d"], p["s_nw"], d_cat, 0, f"d_ssd{l}")
        d_gq, d_gz, d_ba, gg_dtb, gg_alog, g["gdn_norm_w"][l], *landed = _gdn_bwd(
            s["proj"], s["gqkv"], s["g_states"], s["g_inv"], p["g_dtb"], p["g_alog"], p["g_nw"], d_cat, 768, f"d_gdn{l}",
            carried if l == 0 else None)
        if l == 0:
            early = landed
        d_xbc_raw, g["ssd_conv_w"][l], g["ssd_conv_b"][l] = _conv_bwd(
            s["proj"], PC_XBC, 1024, p["s_cw"], p["s_cb"], d_xbc, f"d_ssd_conv{l}")
        d_gq_raw, g["gdn_conv_w"][l], _ = _conv_bwd(s["proj"], PC_GQKV, 768, p["g_cw"], p["g_cb"], d_gq, f"d_gdn_conv{l}")
        d_proj = jnp.concatenate([d_gq_raw, d_gz, d_xbc_raw, d_q, d_k, d_v, d_sz, d_dt, d_ba], axis=1)
        d_h = _mm_nt(d_proj, p["w_in"], 512, 1024, F32, f"d_h{l}")
        g["w_in_pad"][l] = _mm_tn(s["h"], d_proj, 512, PC_TOT // 2, 512, f"dw_in{l}")
        g["attn_sinks"][l] = g_sinks[0, :4]
        g["ssd_dt_bias"][l], g["ssd_A_log"][l], g["ssd_D"][l] = g_dtb[0, :8], g_alog[0, :8], g_d[0, :8]
        g["gdn_dt_bias"][l], g["gdn_A_log"][l] = gg_dtb[0, 4:8], gg_alog[0, 4:8]
        if l > 0:
            sp = saved[l - 1]
            d_x2, d_f, g["pre_mix_norm"][l], g["post_ffn_norm"][l - 1] = _resid_norm_bwd(
                s["xin"], sp["f"], d_x1, d_h, lw[l - 1]["post_ffn"], p["pre_mix"], f"d_postffn{l - 1}")
        else:
            grad_x, g["pre_mix_norm"][0] = _prenorm_bwd(s["xin"], d_x1, d_h, p["pre_mix"], "d_prenorm0")
        if l == DEPTH - 1:
            carried = reduce_early({k: g[k][l] for k in RAW_GRADS})

    small = {k: jnp.stack([a.reshape(-1) for a in g[k]], axis=0) for k in SMALL + CONV}
    return loss_part, grad_x, small, {k: g[k] for k in RAW_GRADS}, early


BIG = (("w_in", 2), ("w_out", 1), ("ffn_w_gate", 2), ("ffn_w_up", 2), ("ffn_w_down", 1))
CONV = ("ssd_conv_w", "gdn_conv_w")
SMALL = ("pre_mix_norm", "post_mix_norm", "pre_ffn_norm", "post_ffn_norm", "attn_sinks", "ssd_conv_b", "ssd_dt_bias",
         "ssd_A_log", "ssd_D", "ssd_norm_w", "gdn_dt_bias", "gdn_A_log", "gdn_norm_w")


def _row_tile(rows, cap):
    best = rows
    for t in range(8, min(cap, rows) + 1, 8):
        if rows % t == 0:
            best = t
    return best


SMALL_UNIT = 8 * LANES


def _pack_small(vals):
    rows = []
    for a in vals:
        f = a.reshape(-1)
        pad = -f.shape[0] % SMALL_UNIT
        rows.append(jnp.concatenate([f, jnp.zeros((pad,), F32)]).reshape(-1, LANES))
    return jnp.concatenate(rows, axis=0)


def _unpack_small(mat, shapes):
    out, r = [], 0
    for shp in shapes:
        n = math.prod(shp)
        nr = -(-n // SMALL_UNIT) * 8
        out.append(mat[r:r + nr].reshape(-1)[:n].reshape(shp))
        r += nr
    return out


def _place():
    x, y, c = lax.axis_index("x"), lax.axis_index("y"), lax.axis_index("c")
    chips = [(1 - x, y), (x, 1 - y), (1 - x, 1 - y)]
    return x, y, c, chips


ANY = pl.BlockSpec(memory_space=pl.ANY)


def _remote(src, dst, sems, k, to):
    send_sems, recv_sems = sems
    return pltpu.make_async_remote_copy(src_ref=src, dst_ref=dst, send_sem=send_sems.at[k], recv_sem=recv_sems.at[k],
                                        device_id=to, device_id_type=MESH)


def _sem_pairs(n):
    return [pltpu.SemaphoreType.DMA((n,)), pltpu.SemaphoreType.DMA((n,))]


def _run_exchange(exchange, name):
    k = len(exchange.arrays)

    def body(*refs):
        ins, outs, sems = refs[:k], refs[k:-2], refs[-2:]
        exchange.start(ins, outs, sems)
        exchange.finish(ins, outs, sems)

    return pl.pallas_call(
        body, name=name, in_specs=[ANY] * k, out_specs=[ANY] * len(exchange.out_shape),
        out_shape=list(exchange.out_shape), scratch_shapes=_sem_pairs(exchange.n_sems),
    )(*exchange.arrays)


def _gather_exchange(shards):
    n = len(shards)

    def sends(s_refs, g_refs, sems):
        x, y, c, chips = _place()
        return [_remote(s_refs[i].at[c], g_refs[i].at[2 * x + y, c], sems, 6 * i + j, (px, py, c))
                for i in range(n) for j, (px, py) in enumerate(chips)]

    def start(s_refs, g_refs, sems):
        for cp in sends(s_refs, g_refs, sems):
            cp.start()

    def finish(s_refs, g_refs, sems):
        x, y, c, chips = _place()
        sib = (x, y, 1 - c)
        passed = []
        for j, (px, py) in enumerate(chips):
            for i in range(n):
                landed = g_refs[i].at[2 * px + py, c]
                _remote(landed, landed, sems, 6 * i + j, (px, py, c)).wait_recv()
                fw = _remote(landed, landed, sems, 6 * i + 3 + j, sib)
                fw.start()
                passed.append(fw)
        for j, (px, py) in enumerate(chips):
            for i in range(n):
                landed = g_refs[i].at[2 * px + py, 1 - c]
                _remote(landed, landed, sems, 6 * i + 3 + j, sib).wait_recv()
        for cp in sends(s_refs, g_refs, sems) + passed:
            cp.wait_send()

    return _Exchange(shards, [S((N_CHIPS,) + a.shape, a.dtype) for a in shards], 6 * n, start, finish)


def _exchange_halves(ds, name):
    n = len(ds)

    def body(*refs):
        d_refs, t_refs, sems = refs[:n], refs[n:2 * n], refs[2 * n:]
        x, y, c, _ = _place()
        cps = [_remote(d_refs[i].at[:, 1 - c], t_refs[i], sems, i, (x, y, 1 - c)) for i in range(n)]
        for cp in cps:
            cp.start()
        for cp in cps:
            cp.wait()

    return pl.pallas_call(
        body, name=name, in_specs=[ANY] * n, out_specs=[ANY] * n,
        out_shape=[S((N_CHIPS,) + a.shape[2:], a.dtype) for a in ds], scratch_shapes=_sem_pairs(n),
    )(*ds)


def _scatter_exchange(ps):
    n = len(ps)

    def copies(p_refs, u_refs, sems):
        x, y, c, chips = _place()
        return [_remote(p_refs[i].at[2 * px + py], u_refs[i].at[j], sems, 3 * i + j, (px, py, c))
                for j, (px, py) in enumerate(chips) for i in range(n)]

    def start(p_refs, u_refs, sems):
        for cp in copies(p_refs, u_refs, sems):
            cp.start()

    def finish(p_refs, u_refs, sems):
        for cp in copies(p_refs, u_refs, sems):
            cp.wait()

    return _Exchange(ps, [S((3,) + a.shape[1:], a.dtype) for a in ps], 3 * n, start, finish)


def _join_halves(qs, name):
    n = len(qs)

    def body(*refs):
        o_refs, sems = refs[n:2 * n], refs[2 * n:]
        x, y, c, _ = _place()
        cps = [_remote(o_refs[i].at[c], o_refs[i].at[c], sems, i, (x, y, 1 - c)) for i in range(n)]
        for cp in cps:
            cp.start()
        for i in range(n):
            other = o_refs[i].at[1 - c]
            _remote(other, other, sems, i, (x, y, 1 - c)).wait_recv()
        for cp in cps:
            cp.wait_send()

    return pl.pallas_call(
        body, name=name, in_specs=[ANY] * n, out_specs=[ANY] * n,
        out_shape=[S(a.shape, a.dtype) for a in qs], input_output_aliases={i: i for i in range(n)},
        scratch_shapes=_sem_pairs(n),
    )(*qs)


def _gather_small(v, name):
    def body(v_ref, o_ref, send_sems, recv_sems, local_sem):
        x, y, c, _ = _place()
        me = 4 * x + 2 * y + c
        mine = pltpu.make_async_copy(v_ref, o_ref.at[me], local_sem)
        mine.start()
        cps = []
        for k in range(1, N_DEV):
            fx, fy, fc = (k >> 2) & 1, (k >> 1) & 1, k & 1
            peer = (x ^ fx, y ^ fy, c ^ fc)
            cps.append(pltpu.make_async_remote_copy(
                src_ref=v_ref, dst_ref=o_ref.at[me], send_sem=send_sems.at[k - 1], recv_sem=recv_sems.at[k - 1],
                device_id=peer, device_id_type=MESH))
        for cp in cps:
            cp.start()
        for k in range(1, N_DEV):
            fx, fy, fc = (k >> 2) & 1, (k >> 1) & 1, k & 1
            dst = o_ref.at[4 * (x ^ fx) + 2 * (y ^ fy) + (c ^ fc)]
            pltpu.make_async_remote_copy(src_ref=dst, dst_ref=dst, send_sem=send_sems.at[k - 1],
                                         recv_sem=recv_sems.at[k - 1], device_id=(x, y, c),
                                         device_id_type=MESH).wait_recv()
        for cp in cps:
            cp.wait_send()
        mine.wait()

    return pl.pallas_call(
        body, name=name, in_specs=[ANY], out_specs=ANY, out_shape=S((N_DEV,) + v.shape, F32),
        scratch_shapes=[pltpu.SemaphoreType.DMA((N_DEV - 1,)), pltpu.SemaphoreType.DMA((N_DEV - 1,)),
                        pltpu.SemaphoreType.DMA],
    )(v)


def _sum_leading(a, name):
    n, rows, cols = a.shape
    tm = _row_tile(rows, 640)

    def body(a_ref, o_ref):
        acc = a_ref[0]
        for k in range(1, n):
            acc = acc + a_ref[k]
        o_ref[...] = acc

    return pl.pallas_call(
        body, grid=(rows // tm,), name=name, in_specs=[pl.BlockSpec((n, tm, cols), lambda i: (0, i, 0))],
        out_specs=pl.BlockSpec((tm, cols), lambda i: (i, 0)), out_shape=S((rows, cols), F32),
        compiler_params=_params(("arbitrary",)),
    )(a)


def _add_sibling(place, a, b, name):
    n, hr, cols = b.shape
    tm = _row_tile(hr, 512)

    def body(place_ref, a_ref, b_ref, o_ref, o16_ref):
        tot = a_ref[0] + b_ref[...]
        o_ref[...] = tot
        o16_ref[...] = tot.astype(BF16)

    spec = pl.BlockSpec((1, tm, cols), lambda k, i, pr: (k, i, 0))
    return pl.pallas_call(
        body, name=name, out_shape=[S((n, hr, cols), F32), S((n, hr, cols), BF16)],
        grid_spec=pltpu.PrefetchScalarGridSpec(
            num_scalar_prefetch=1, grid=(n, hr // tm),
            in_specs=[pl.BlockSpec((1, 1, tm, cols), lambda k, i, pr: (k, pr[0], i, 0)), spec], out_specs=[spec, spec]),
        compiler_params=_params(("arbitrary", "arbitrary")),
    )(place, a, b)


def _add_chips(place, sums, others, name):
    _, hr, cols = sums.shape
    tm = _row_tile(hr, 512)

    def body(place_ref, m_ref, o_ref, out_ref):
        acc = m_ref[0]
        for k in range(others.shape[0]):
            acc = acc + o_ref[k].astype(F32)
        out_ref[0] = acc

    return pl.pallas_call(
        body, name=name, out_shape=S((2, hr, cols), F32),
        grid_spec=pltpu.PrefetchScalarGridSpec(
            num_scalar_prefetch=1, grid=(hr // tm,),
            in_specs=[pl.BlockSpec((1, tm, cols), lambda i, pr: (pr[1], i, 0)),
                      pl.BlockSpec((others.shape[0], tm, cols), lambda i, pr: (0, i, 0))],
            out_specs=pl.BlockSpec((1, tm, cols), lambda i, pr: (pr[0], i, 0))),
        compiler_params=_params(("arbitrary",)),
    )(place, sums, others)


def _adamw(wt, g, m, v, name):
    shape = wt.shape
    cols = shape[-1]
    rows = math.prod(shape[:-1])
    tm = rows
    for cand in (512, 256, 128, 64, 32, 16, 8):
        if rows % cand == 0:
            tm = cand
            break
    c1 = 1.0 - ADAM_B1 ** ADAM_STEP
    c2 = 1.0 - ADAM_B2 ** ADAM_STEP

    def body(w_ref, g_ref, m_ref, v_ref, d_ref, nm_ref, nv_ref):
        gv = g_ref[...]
        nm = ADAM_B1 * m_ref[...] + (1.0 - ADAM_B1) * gv
        nv = ADAM_B2 * v_ref[...] + (1.0 - ADAM_B2) * (gv * gv)
        d_ref[...] = -ADAM_LR * ((nm / c1) / (jnp.sqrt(nv / c2) + ADAM_EPS) + ADAM_WD * w_ref[...])
        nm_ref[...] = nm
        nv_ref[...] = nv

    spec = pl.BlockSpec((tm, cols), lambda i: (i, 0))
    outs = pl.pallas_call(
        body, grid=(rows // tm,), name=name, in_specs=[spec] * 4, out_specs=[spec] * 3,
        out_shape=[S((rows, cols), F32)] * 3, compiler_params=_params(("arbitrary",)),
    )(*[a.reshape(rows, cols) for a in (wt, g, m, v)])
    return [o.reshape(shape) for o in outs]


WEIGHTS = ('pre_mix_norm', 'post_mix_norm', 'pre_ffn_norm', 'post_ffn_norm', 'w_in', 'w_out', 'attn_sinks', 'ssd_conv_w',
           'ssd_conv_b', 'ssd_dt_bias', 'ssd_A_log', 'ssd_D', 'ssd_norm_w', 'gdn_conv_w', 'gdn_dt_bias', 'gdn_A_log',
           'gdn_norm_w', 'ffn_w_gate', 'ffn_w_up', 'ffn_w_down')


def _chip_pieces(raw, shapes):
    def cut(g, axis, shape):
        width = shape[1 + axis]
        parts = jnp.stack([lax.slice_in_dim(g, k * width, (k + 1) * width, axis=axis) for k in range(N_CHIPS)])
        return parts.reshape(N_CHIPS, 2, shape[1] // 2, shape[2])

    g = raw["w_out_cat"]
    grads = [(_unpad_cols(raw["w_in_pad"]), 1), (jnp.concatenate([g[512:768], g[0:512], g[768:1024]], axis=0), 0),
             (raw["w_gu"][:, :FF], 1), (raw["w_gu"][:, FF:], 1), (raw["ffn_w_down"], 0)]
    return [cut(a, axis, shp) for (a, axis), shp in zip(grads, shapes)]


def _step(x, target, wts, ms, vs):
    chip = 2 * lax.axis_index("x") + lax.axis_index("y")
    place = jnp.stack([lax.axis_index("c"), chip]).astype(jnp.int32)
    big_names = [k for k, _ in BIG]
    big_shapes = [wts[k].shape for k in big_names]
    own = lambda gathered, shard: lax.dynamic_update_index_in_dim(gathered, shard, chip, 0)
    halves = lambda a: a.reshape((2, a.shape[0] // 2) + a.shape[1:])

    w16 = {k: wts[k].astype(BF16) for k in big_names}
    first = halves(w16["w_in"][0])
    w_in0 = _in_weight(own(_run_exchange(_gather_exchange([first]), "gather_w_in0")[0], first))
    later = [halves(w16["w_in"][1])] + [w16[k] for k in big_names[1:]]

    def matmul_weights(landed):
        landed = [own(g, s) for g, s in zip(landed, later)]
        gathered = dict(zip(big_names[1:], landed[1:]))
        return [_matmul_weights(w_in0 if l == 0 else _in_weight(landed[0]), gathered, l) for l in range(DEPTH)]

    conv = _gather_small(_pack_small([wts[k] for k in CONV]), "gather_conv_weights")
    conv = [_unpack_small(conv[2 * k], [wts[n].shape for n in CONV]) for k in range(N_CHIPS)]
    w_all = dict(wts)
    for i, n in enumerate(CONV):
        w_all[n] = jnp.concatenate([conv[k][i] for k in range(N_CHIPS)], axis=2)

    def reduce_begin(l, raw):
        pieces = _chip_pieces(raw, big_shapes)
        sib = _exchange_halves(pieces, f"exchange_halves{l}")
        return [_add_sibling(place, p, t, f"add_sibling_{k}{l}") for k, p, t in zip(big_names, pieces, sib)]

    def reduce_end(l, sums, others):
        parts = [_add_chips(place, s32, o, f"add_chips_{k}{l}") for k, (s32, _), o in zip(big_names, sums, others)]
        return [q.reshape(shp[1:]) for q, shp in zip(_join_halves(parts, f"join_halves{l}"), big_shapes)]

    last_sums = []

    def reduce_early(raw):
        last_sums.extend(reduce_begin(DEPTH - 1, raw))
        return _scatter_exchange([s16 for _, s16 in last_sums])

    loss_part, grad_x, small_g, raw, early = _local_step(
        x[0], target[0], [_small_operands(w_all, l) for l in range(DEPTH)], w_in0, _gather_exchange(later),
        matmul_weights, reduce_early)

    per_layer = [None] * DEPTH
    per_layer[DEPTH - 1] = reduce_end(DEPTH - 1, last_sums, early)
    for l in range(DEPTH - 1):
        sums = reduce_begin(l, {k: raw[k][l] for k in RAW_GRADS})
        others = _run_exchange(_scatter_exchange([s16 for _, s16 in sums]), f"scatter_chips{l}")
        per_layer[l] = reduce_end(l, sums, others)
    g_all = {k: jnp.stack([per_layer[l][i] for l in range(DEPTH)]) for i, k in enumerate(big_names)}

    names = SMALL + CONV
    packed = _pack_small([small_g[k] for k in names] + [loss_part])
    small_sum = _sum_leading(_gather_small(packed, "gather_small_grads"), "add_small")
    vals = _unpack_small(small_sum, [small_g[k].shape for k in names] + [(1, LANES)])
    loss = vals[-1][0, 0]
    for k, v in zip(names, vals[:-1]):
        if k in CONV:
            width = wts[k].shape[2]
            v = lax.dynamic_slice_in_dim(v.reshape(DEPTH, 4, -1), chip * width, width, axis=2)
        g_all[k] = v.reshape(wts[k].shape)

    shapes = [wts[k].shape for k in names]
    d_s, m_s, v_s = _adamw(_pack_small([wts[k] for k in names]), _pack_small([g_all[k] for k in names]),
                           _pack_small([ms[k] for k in names]), _pack_small([vs[k] for k in names]), "adamw_small")
    upd = dict(zip(names, zip(_unpack_small(d_s, shapes), _unpack_small(m_s, shapes), _unpack_small(v_s, shapes))))
    for k in big_names:
        upd[k] = _adamw(wts[k], g_all[k], ms[k], vs[k], f"adamw_{k}")
    return (loss, grad_x[None], *[g_all[k] for k in WEIGHTS], *[upd[k][0] for k in WEIGHTS],
            *[upd[k][1] for k in WEIGHTS], *[upd[k][2] for k in WEIGHTS])


def kernel(x, pre_mix_norm, post_mix_norm, pre_ffn_norm, post_ffn_norm, w_in, w_out, attn_sinks, ssd_conv_w, ssd_conv_b, ssd_dt_bias, ssd_A_log, ssd_D, ssd_norm_w, gdn_conv_w, gdn_dt_bias, gdn_A_log, gdn_norm_w, ffn_w_gate, ffn_w_up, ffn_w_down, loss_target, m_pre_mix_norm, m_post_mix_norm, m_pre_ffn_norm, m_post_ffn_norm, m_w_in, m_w_out, m_attn_sinks, m_ssd_conv_w, m_ssd_conv_b, m_ssd_dt_bias, m_ssd_A_log, m_ssd_D, m_ssd_norm_w, m_gdn_conv_w, m_gdn_dt_bias, m_gdn_A_log, m_gdn_norm_w, m_ffn_w_gate, m_ffn_w_up, m_ffn_w_down, v_pre_mix_norm, v_post_mix_norm, v_pre_ffn_norm, v_post_ffn_norm, v_w_in, v_w_out, v_attn_sinks, v_ssd_conv_w, v_ssd_conv_b, v_ssd_dt_bias, v_ssd_A_log, v_ssd_D, v_ssd_norm_w, v_gdn_conv_w, v_gdn_dt_bias, v_gdn_A_log, v_gdn_norm_w, v_ffn_w_gate, v_ffn_w_up, v_ffn_w_down):
    wts = dict(zip(WEIGHTS, (pre_mix_norm, post_mix_norm, pre_ffn_norm, post_ffn_norm, w_in, w_out, attn_sinks, ssd_conv_w, ssd_conv_b, ssd_dt_bias, ssd_A_log, ssd_D, ssd_norm_w, gdn_conv_w, gdn_dt_bias, gdn_A_log, gdn_norm_w, ffn_w_gate, ffn_w_up, ffn_w_down)))
    ms = dict(zip(WEIGHTS, (m_pre_mix_norm, m_post_mix_norm, m_pre_ffn_norm, m_post_ffn_norm, m_w_in, m_w_out, m_attn_sinks, m_ssd_conv_w, m_ssd_conv_b, m_ssd_dt_bias, m_ssd_A_log, m_ssd_D, m_ssd_norm_w, m_gdn_conv_w, m_gdn_dt_bias, m_gdn_A_log, m_gdn_norm_w, m_ffn_w_gate, m_ffn_w_up, m_ffn_w_down)))
    vs = dict(zip(WEIGHTS, (v_pre_mix_norm, v_post_mix_norm, v_pre_ffn_norm, v_post_ffn_norm, v_w_in, v_w_out, v_attn_sinks, v_ssd_conv_w, v_ssd_conv_b, v_ssd_dt_bias, v_ssd_A_log, v_ssd_D, v_ssd_norm_w, v_gdn_conv_w, v_gdn_dt_bias, v_gdn_A_log, v_gdn_norm_w, v_ffn_w_gate, v_ffn_w_up, v_ffn_w_down)))
    return _step(x, loss_target, wts, ms, vs)
```

```python
import functools
import math

import jax
import jax.numpy as jnp
from jax import lax
from jax.experimental import pallas as pl
from jax.experimental.pallas import tpu as pltpu

F32, BF16 = jnp.float32, jnp.bfloat16
HI = lax.Precision.HIGHEST
MESH = pl.DeviceIdType.MESH
S = jax.ShapeDtypeStruct

D_MODEL = 1024
DEPTH = 2
CHUNK = 64
EPS = 1e-6
FF = 2816
N_CHIPS = 4
N_DEV = 8
LANES = 128

VMEM_LIMIT_BYTES = 56 * 1024 * 1024

PC_GQKV, PC_GZ, PC_XBC, PC_ATT, PC_SZ, PC_DT, PC_BA, PC_TOT = 0, 768, 1024, 2048, 2560, 3072, 3200, 3328

ADAM_LR, ADAM_B1, ADAM_B2, ADAM_EPS, ADAM_WD, ADAM_STEP = 0.001, 0.9, 0.999, 1e-08, 0.01, 10

ALIBI_SLOPES = tuple(2.0 ** (-8.0 * (h + 1) / 4) for h in range(4))


def _params(sem=None, **kw):
    if sem is not None:
        kw["dimension_semantics"] = sem
    return pltpu.CompilerParams(vmem_limit_bytes=VMEM_LIMIT_BYTES, **kw)


def _dot(a, b, prec=None):
    return jnp.dot(a, b, precision=prec, preferred_element_type=F32)


def _dot_nt(a, b, prec=None):
    return lax.dot_general(a, b, (((1,), (1,)), ((), ())), precision=prec, preferred_element_type=F32)


def _dot_tn(a, b, prec=None):
    return lax.dot_general(a, b, (((0,), (0,)), ((), ())), precision=prec, preferred_element_type=F32)


def _iota2(n, m):
    return lax.broadcasted_iota(jnp.int32, (n, m), 0), lax.broadcasted_iota(jnp.int32, (n, m), 1)


def _pick_col(arr, idx):
    ci = lax.broadcasted_iota(jnp.int32, arr.shape, 1)
    return jnp.sum(jnp.where(ci == idx, arr, 0.0), axis=1, keepdims=True)


def _pick_row(arr, idx):
    ri = lax.broadcasted_iota(jnp.int32, arr.shape, 0)
    return jnp.sum(jnp.where(ri == idx, arr, 0.0), axis=0, keepdims=True)


def _col_to_row(col, eye):
    return jnp.sum(eye * col, axis=0, keepdims=True)


def _rms(x, w):
    return x * lax.rsqrt(jnp.mean(x * x, axis=-1, keepdims=True) + EPS) * w


def _mm_nn(a, b, tm, tn, out_dtype, name):
    m, k = a.shape
    n = b.shape[1]
    tm, tn = min(tm, m), min(tn, n)

    def body(a_ref, b_ref, o_ref):
        o_ref[...] = _dot(a_ref[...], b_ref[...]).astype(o_ref.dtype)

    return pl.pallas_call(
        body, grid=(n // tn, m // tm), name=name,
        in_specs=[pl.BlockSpec((tm, k), lambda j, i: (i, 0)), pl.BlockSpec((k, tn), lambda j, i: (0, j))],
        out_specs=pl.BlockSpec((tm, tn), lambda j, i: (i, j)),
        out_shape=S((m, n), out_dtype), compiler_params=_params(("arbitrary", "arbitrary")),
    )(a, b)


def _mm_nt(a, b, tm, tn, out_dtype, name):
    m, k = a.shape
    n = b.shape[0]
    tm, tn = min(tm, m), min(tn, n)

    def body(a_ref, b_ref, o_ref):
        o_ref[...] = _dot_nt(a_ref[...], b_ref[...]).astype(o_ref.dtype)

    return pl.pallas_call(
        body, grid=(n // tn, m // tm), name=name,
        in_specs=[pl.BlockSpec((tm, k), lambda j, i: (i, 0)), pl.BlockSpec((tn, k), lambda j, i: (j, 0))],
        out_specs=pl.BlockSpec((tm, tn), lambda j, i: (i, j)),
        out_shape=S((m, n), out_dtype), compiler_params=_params(("arbitrary", "arbitrary")),
    )(a, b)


def _mm_tn(a, b, tm, tn, tk, name):
    t, m = a.shape
    n = b.shape[1]
    tm, tn, tk = min(tm, m), min(tn, n), min(tk, t)

    def body(a_ref, b_ref, o_ref):
        part = _dot_tn(a_ref[...], b_ref[...])

        @pl.when(pl.program_id(2) == 0)
        def _():
            o_ref[...] = part

        @pl.when(pl.program_id(2) > 0)
        def _():
            o_ref[...] += part

    return pl.pallas_call(
        body, grid=(m // tm, n // tn, t // tk), name=name,
        in_specs=[pl.BlockSpec((tk, tm), lambda i, j, k: (k, i)), pl.BlockSpec((tk, tn), lambda i, j, k: (k, j))],
        out_specs=pl.BlockSpec((tm, tn), lambda i, j, k: (i, j)),
        out_shape=S((m, n), F32), compiler_params=_params(("arbitrary", "arbitrary", "arbitrary")),
    )(a, b)


def _rowcall(fn, rows, params, row_outs, acc_outs, name, tm=512):
    t = rows[0].shape[0]
    tm = min(tm, t)
    n_in = len(rows) + len(params)
    n_ro = len(row_outs)

    def body(*refs):
        ro, ao = fn(*[r[...] for r in refs[:n_in]])
        for ref, v in zip(refs[n_in:n_in + n_ro], ro):
            ref[...] = v.astype(ref.dtype)
        acc_refs = refs[n_in + n_ro:]
        if acc_refs:
            @pl.when(pl.program_id(0) == 0)
            def _():
                for ref, v in zip(acc_refs, ao):
                    ref[...] = v

            @pl.when(pl.program_id(0) > 0)
            def _():
                for ref, v in zip(acc_refs, ao):
                    ref[...] += v

    in_specs = [pl.BlockSpec((tm, r.shape[1]), lambda i: (i, 0)) for r in rows]
    in_specs += [pl.BlockSpec(p.shape, lambda i: (0, 0)) for p in params]
    out_specs = [pl.BlockSpec((tm, c), lambda i: (i, 0)) for c, _ in row_outs]
    out_specs += [pl.BlockSpec(shape, lambda i: (0, 0)) for shape in acc_outs]
    out_shape = [S((t, c), dt) for c, dt in row_outs] + [S(shape, F32) for shape in acc_outs]
    return pl.pallas_call(
        body, grid=(t // tm,), name=name, in_specs=in_specs, out_specs=out_specs, out_shape=out_shape,
        compiler_params=_params(("arbitrary",)),
    )(*rows, *params)


def _prenorm(x, w, name):
    def fn(x, w):
        return (_rms(x, w),), ()
    return _rowcall(fn, [x], [w], [(D_MODEL, BF16)], [], name)[0]


def _resid_norm(xin, m, w_post, w_next, name):
    def fn(xin, m, w_post, w_next):
        xo = xin + _rms(m, w_post)
        return (xo, _rms(xo, w_next)), ()
    return _rowcall(fn, [xin, m], [w_post, w_next], [(D_MODEL, F32), (D_MODEL, BF16)], [], name)


def _resid_loss(xin, m, w_post, target, name):
    def fn(xin, m, target, w_post):
        r, vjp = jax.vjp(_rms, m, w_post)
        err = xin + r - target
        dy = err * (1.0 / D_MODEL)
        dm, dw = vjp(dy)
        tot = jnp.sum(jnp.sum(err * err, axis=1, keepdims=True), axis=0, keepdims=True) * (0.5 / D_MODEL)
        lane = lax.broadcasted_iota(jnp.int32, (1, LANES), 1)
        return (dy, dm), (jnp.where(lane == 0, tot, 0.0), dw)
    return _rowcall(fn, [xin, m, target], [w_post], [(D_MODEL, F32), (D_MODEL, BF16)],
                    [(1, LANES), (1, D_MODEL)], name)


def _resid_norm_bwd(x_out, m, d_direct, dh, w_post, w_next, name):
    def fn(x_out, m, d_direct, dh, w_post, w_next):
        _, vjp_n = jax.vjp(_rms, x_out, w_next)
        dx, dwn = vjp_n(dh)
        d_total = d_direct + dx
        _, vjp_p = jax.vjp(_rms, m, w_post)
        dm, dwp = vjp_p(d_total)
        return (d_total, dm), (dwn, dwp)
    return _rowcall(fn, [x_out, m, d_direct, dh], [w_post, w_next], [(D_MODEL, F32), (D_MODEL, BF16)],
                    [(1, D_MODEL), (1, D_MODEL)], name)


def _prenorm_bwd(x, d_direct, dh, w, name):
    def fn(x, d_direct, dh, w):
        _, vjp = jax.vjp(_rms, x, w)
        dx, dw = vjp(dh)
        return (d_direct + dx,), (dw,)
    return _rowcall(fn, [x, d_direct, dh], [w], [(D_MODEL, F32)], [(1, D_MODEL)], name)


FF_HALF = FF // 2


def _interleave_gu(gate, up):
    return jnp.concatenate([gate[:, :FF_HALF], up[:, :FF_HALF], gate[:, FF_HALF:], up[:, FF_HALF:]], axis=1)


def _split_gu(gu):
    return (jnp.concatenate([gu[:, :FF_HALF], gu[:, FF:FF + FF_HALF]], axis=1),
            jnp.concatenate([gu[:, FF_HALF:FF], gu[:, FF + FF_HALF:]], axis=1))


def _swiglu_pair(gu):
    n = gu.shape[1] // 2
    return jax.nn.silu(gu[:, :n]) * gu[:, n:]


def _ffn_up(h2, w_gu, name, tm=512):
    t, k = h2.shape
    tm = min(tm, t)

    def body(a_ref, b_ref, gu_ref, act_ref):
        gu = _dot(a_ref[...], b_ref[...])
        gu_ref[...] = gu.astype(gu_ref.dtype)
        act_ref[...] = _swiglu_pair(gu).astype(act_ref.dtype)

    return pl.pallas_call(
        body, grid=(2, t // tm), name=name,
        in_specs=[pl.BlockSpec((tm, k), lambda j, i: (i, 0)), pl.BlockSpec((k, FF), lambda j, i: (0, j))],
        out_specs=[pl.BlockSpec((tm, FF), lambda j, i: (i, j)), pl.BlockSpec((tm, FF_HALF), lambda j, i: (i, j))],
        out_shape=[S((t, 2 * FF), BF16), S((t, FF), BF16)], compiler_params=_params(("arbitrary", "arbitrary")),
    )(h2, w_gu)


def _swiglu_bwd(gu, da, name):
    def fn(gu, da):
        gu = gu.astype(F32)
        halves = []
        for b in range(2):
            _, vjp = jax.vjp(_swiglu_pair, gu[:, FF * b:FF * (b + 1)])
            halves.append(vjp(da[:, FF_HALF * b:FF_HALF * (b + 1)])[0])
        return (jnp.concatenate(halves, axis=1),), ()
    return _rowcall(fn, [gu, da], [], [(2 * FF, BF16)], [], name, tm=256)[0]


def _conv_fwd(proj, col0, width, w, b, name, tm=512):
    t = proj.shape[0]
    tm = min(tm, t)
    cb = col0 // width

    def body(x_ref, w_ref, b_ref, o_ref, ext):
        @pl.when(pl.program_id(0) == 0)
        def _():
            ext[0:8, :] = jnp.zeros((8, width), F32)

        ext[8:8 + tm, :] = x_ref[...]
        y = b_ref[...] + w_ref[0:1, :] * ext[pl.ds(5, tm), :]
        for k in range(1, 4):
            y = y + w_ref[k:k + 1, :] * ext[pl.ds(5 + k, tm), :]
        o_ref[...] = jax.nn.silu(y)
        ext[0:8, :] = ext[tm:tm + 8, :]

    return pl.pallas_call(
        body, grid=(t // tm,), name=name,
        in_specs=[pl.BlockSpec((tm, width), lambda i: (i, cb)), pl.BlockSpec((4, width), lambda i: (0, 0)),
                  pl.BlockSpec((1, width), lambda i: (0, 0))],
        out_specs=pl.BlockSpec((tm, width), lambda i: (i, 0)),
        out_shape=S((t, width), F32), scratch_shapes=[pltpu.VMEM((tm + 8, width), F32)],
        compiler_params=_params(("arbitrary",)),
    )(proj, w, b)


def _conv_bwd(proj, col0, width, w, b, dact, name, tm=512):
    t = proj.shape[0]
    tm = min(tm, t)
    nb = t // tm
    cb = col0 // width
    hb = tm // 8

    def body(x_ref, halo_ref, d_ref, w_ref, b_ref, dx_ref, dw_ref, db_ref, extx, extd):
        i = pl.program_id(0)
        blk = nb - 1 - i

        @pl.when(i == 0)
        def _():
            extd[tm:tm + 8, :] = jnp.zeros((8, width), F32)
            dw_ref[...] = jnp.zeros((4, width), F32)
            db_ref[...] = jnp.zeros((1, width), F32)

        extx[0:8, :] = jnp.where(blk == 0, 0.0, halo_ref[...])
        extx[8:8 + tm, :] = x_ref[...]
        y = b_ref[...] + w_ref[0:1, :] * extx[pl.ds(5, tm), :]
        for k in range(1, 4):
            y = y + w_ref[k:k + 1, :] * extx[pl.ds(5 + k, tm), :]
        sig = jax.nn.sigmoid(y)
        dy = d_ref[...] * (sig * (1.0 + y * (1.0 - sig)))
        extd[0:tm, :] = dy
        dx = w_ref[0:1, :] * extd[pl.ds(3, tm), :]
        for k in range(1, 4):
            dx = dx + w_ref[k:k + 1, :] * extd[pl.ds(3 - k, tm), :]
        dx_ref[...] = dx.astype(dx_ref.dtype)
        for k in range(4):
            dw_ref[k:k + 1, :] += jnp.sum(dy * extx[pl.ds(5 + k, tm), :], axis=0, keepdims=True)
        db_ref[...] += jnp.sum(dy, axis=0, keepdims=True)
        extd[tm:tm + 8, :] = extd[0:8, :]

    return pl.pallas_call(
        body, grid=(nb,), name=name,
        in_specs=[pl.BlockSpec((tm, width), lambda i: (nb - 1 - i, cb)),
                  pl.BlockSpec((8, width), lambda i: (jnp.maximum((nb - 1 - i) * hb - 1, 0), cb)),
                  pl.BlockSpec((tm, width), lambda i: (nb - 1 - i, 0)),
                  pl.BlockSpec((4, width), lambda i: (0, 0)), pl.BlockSpec((1, width), lambda i: (0, 0))],
        out_specs=[pl.BlockSpec((tm, width), lambda i: (nb - 1 - i, 0)), pl.BlockSpec((4, width), lambda i: (0, 0)),
                   pl.BlockSpec((1, width), lambda i: (0, 0))],
        out_shape=[S((t, width), BF16), S((4, width), F32), S((1, width), F32)],
        scratch_shapes=[pltpu.VMEM((tm + 8, width), F32), pltpu.VMEM((tm + 8, width), F32)],
        compiler_params=_params(("arbitrary",)),
    )(proj, proj, dact, w, b)


SWA_BQ = 256
SWA_BACK = 128


def _swa_block(q, kw, vw, sinks, blk):
    nq, nk = SWA_BQ, SWA_BQ + SWA_BACK
    r, j = _iota2(nq, nk)
    rel = r // CHUNK + 2 - j // CHUNK
    valid = (rel >= 0) & (rel <= 2) & (blk * (SWA_BQ // CHUNK) + j // CHUNK - 2 >= 0)
    dist = jnp.abs(r + SWA_BACK - j).astype(F32)
    outs = []
    for h in range(4):
        kv = h // 2
        qh = q[:, 64 * h:64 * h + 64]
        kh = kw[:, 64 * kv:64 * kv + 64]
        vh = vw[:, 64 * kv:64 * kv + 64]
        s = _dot_nt(qh, kh) * 0.125 - ALIBI_SLOPES[h] * dist
        s = jnp.where(valid, s, -1e30)
        sink = _pick_col(sinks, h)
        m = jnp.maximum(jnp.max(s, axis=1, keepdims=True), sink)
        e = jnp.exp(s - m)
        den = jnp.sum(e, axis=1, keepdims=True) + jnp.exp(sink - m)
        outs.append(_dot(e / den, vh))
    return jnp.concatenate(outs, axis=1)


def _swa_fwd(proj, sinks, name):
    t = proj.shape[0]
    qb, kb = PC_ATT // 256, PC_ATT // 128 + 2
    win = SWA_BQ + SWA_BACK

    def body(q_ref, k_ref, v_ref, s_ref, o_ref, kp, vp):
        i = pl.program_id(0)

        @pl.when(i == 0)
        def _():
            kp[0:SWA_BACK, :] = jnp.zeros((SWA_BACK, 128), F32)
            vp[0:SWA_BACK, :] = jnp.zeros((SWA_BACK, 128), F32)
            kp[SWA_BACK:, :] = k_ref[...]
            vp[SWA_BACK:, :] = v_ref[...]

        start = pl.multiple_of(i * SWA_BQ, SWA_BQ)
        o = _swa_block(q_ref[...], kp[pl.ds(start, win), :], vp[pl.ds(start, win), :], s_ref[...], i)
        o_ref[...] = o.astype(o_ref.dtype)

    return pl.pallas_call(
        body, grid=(t // SWA_BQ,), name=name,
        in_specs=[pl.BlockSpec((SWA_BQ, 256), lambda i: (i, qb)), pl.BlockSpec((t, 128), lambda i: (0, kb)),
                  pl.BlockSpec((t, 128), lambda i: (0, kb + 1)), pl.BlockSpec((1, LANES), lambda i: (0, 0))],
        out_specs=pl.BlockSpec((SWA_BQ, 256), lambda i: (i, 0)),
        out_shape=S((t, 256), BF16),
        scratch_shapes=[pltpu.VMEM((t + SWA_BACK, 128), F32), pltpu.VMEM((t + SWA_BACK, 128), F32)],
        compiler_params=_params(("arbitrary",)),
    )(proj, proj, proj, sinks)


def _swa_bwd(proj, sinks, dcat, dcol0, name):
    t = proj.shape[0]
    nb = t // SWA_BQ
    qb, kb = PC_ATT // 256, PC_ATT // 128 + 2
    db = dcol0 // 256
    win = SWA_BQ + SWA_BACK

    def body(q_ref, k_ref, v_ref, s_ref, do_ref, dq_ref, dk_ref, dv_ref, ds_ref, kp, vp, dkp, dvp):
        i = pl.program_id(0)

        @pl.when(i == 0)
        def _():
            kp[0:SWA_BACK, :] = jnp.zeros((SWA_BACK, 128), F32)
            vp[0:SWA_BACK, :] = jnp.zeros((SWA_BACK, 128), F32)
            kp[SWA_BACK:, :] = k_ref[...]
            vp[SWA_BACK:, :] = v_ref[...]
            dkp[...] = jnp.zeros_like(dkp)
            dvp[...] = jnp.zeros_like(dvp)
            ds_ref[...] = jnp.zeros_like(ds_ref)

        start = pl.multiple_of(i * SWA_BQ, SWA_BQ)
        _, vjp = jax.vjp(functools.partial(_swa_block, blk=i), q_ref[...], kp[pl.ds(start, win), :],
                         vp[pl.ds(start, win), :], s_ref[...])
        dq, dkw, dvw, dsk = vjp(do_ref[...])
        dq_ref[...] = dq.astype(dq_ref.dtype)
        dkp[pl.ds(start, win), :] += dkw
        dvp[pl.ds(start, win), :] += dvw
        ds_ref[...] += dsk

        @pl.when(i == nb - 1)
        def _():
            dk_ref[...] = dkp[SWA_BACK:, :].astype(dk_ref.dtype)
            dv_ref[...] = dvp[SWA_BACK:, :].astype(dv_ref.dtype)

    return pl.pallas_call(
        body, grid=(nb,), name=name,
        in_specs=[pl.BlockSpec((SWA_BQ, 256), lambda i: (i, qb)), pl.BlockSpec((t, 128), lambda i: (0, kb)),
                  pl.BlockSpec((t, 128), lambda i: (0, kb + 1)), pl.BlockSpec((1, LANES), lambda i: (0, 0)),
                  pl.BlockSpec((SWA_BQ, 256), lambda i: (i, db))],
        out_specs=[pl.BlockSpec((SWA_BQ, 256), lambda i: (i, 0)), pl.BlockSpec((t, 128), lambda i: (0, 0)),
                   pl.BlockSpec((t, 128), lambda i: (0, 0)), pl.BlockSpec((1, LANES), lambda i: (0, 0))],
        out_shape=[S((t, 256), BF16), S((t, 128), BF16), S((t, 128), BF16), S((1, LANES), F32)],
        scratch_shapes=[pltpu.VMEM((t + SWA_BACK, 128), F32) for _ in range(4)],
        compiler_params=_params(("arbitrary",)),
    )(proj, proj, proj, sinks, dcat)


def _ssd_chunk(z, xbc, dt_raw, state, dtb, alog, dsk, nw):
    n = CHUNK
    r, c = _iota2(n, n)
    tril = r >= c
    eye = (r == c).astype(F32)
    dt = jax.nn.softplus(dt_raw + dtb)
    acs = _dot(tril.astype(F32), dt * (-jnp.exp(alog)), HI)
    xs, bm, cm = xbc[:, :512], xbc[:, 512:768], xbc[:, 768:1024]
    heads = range(8)
    bg = [bm[:, 128 * g:128 * g + 128] for g in range(2)]
    cg = [cm[:, 128 * g:128 * g + 128] for g in range(2)]
    cb = [_dot_nt(cg[g], bg[g]) for g in range(2)]
    dth = [_pick_col(dt, h) for h in heads]
    acol = [_pick_col(acs, h) for h in heads]
    arow = [_col_to_row(a, eye) for a in acol]
    lmat = [jnp.where(tril, jnp.exp(jnp.where(tril, a - b, 0.0)), 0.0) for a, b in zip(acol, arow)]
    xh = [xs[:, 64 * h:64 * h + 64] for h in heads]
    xc = [x * t for x, t in zip(xh, dth)]
    st = [state[64 * h:64 * h + 64, :] for h in heads]
    alast = [_pick_row(a, n - 1) for a in acol]
    y_in = [_dot(cb[h // 4] * lmat[h], xc[h]) for h in heads]
    y_st = [_dot_nt(cg[h // 4], st[h]) * jnp.exp(acol[h]) for h in heads]
    ys = [y_in[h] + y_st[h] + xh[h] * _pick_col(dsk, h) for h in heads]
    new_states = [st[h] * jnp.exp(alast[h]) + _dot_tn(xc[h] * jnp.exp(alast[h] - acol[h]), bg[h // 4]) for h in heads]
    gg = jnp.concatenate(ys, axis=1) * jax.nn.silu(z)
    outs = []
    for gi in range(2):
        gv = gg[:, 256 * gi:256 * gi + 256]
        outs.append(gv * lax.rsqrt(jnp.mean(gv * gv, axis=-1, keepdims=True) + EPS))
    return jnp.concatenate(outs, axis=1) * nw, jnp.concatenate(new_states, axis=0)


def _ssd_fwd(proj, xbc, dtb, alog, dsk, nw, name, exchange=None):
    t = proj.shape[0]
    nc = t // CHUNK

    def body(z_ref, x_ref, dt_ref, dtb_ref, al_ref, d_ref, nw_ref, o_ref, st_ref, state):
        @pl.when(pl.program_id(0) == 0)
        def _():
            state[...] = jnp.zeros_like(state)

        st_ref[0] = state[...]
        o, ns = _ssd_chunk(z_ref[...], x_ref[...], dt_ref[...], state[...], dtb_ref[...], al_ref[...], d_ref[...],
                           nw_ref[...])
        o_ref[...] = o.astype(o_ref.dtype)
        state[...] = ns

    body, x_in, x_out, x_shape, x_sems = _hosted(exchange, 7, 2, nc, body)
    vec = pl.BlockSpec((1, LANES), lambda i: (0, 0))
    return pl.pallas_call(
        body, grid=(nc,), name=name,
        in_specs=[pl.BlockSpec((CHUNK, 512), lambda i: (i, PC_SZ // 512)), pl.BlockSpec((CHUNK, 1024), lambda i: (i, 0)),
                  pl.BlockSpec((CHUNK, 128), lambda i: (i, PC_DT // 128)), vec, vec, vec,
                  pl.BlockSpec((1, 512), lambda i: (0, 0))] + x_in,
        out_specs=[pl.BlockSpec((CHUNK, 512), lambda i: (i, 0)), pl.BlockSpec((1, 512, 128), lambda i: (i, 0, 0))] + x_out,
        out_shape=[S((t, 512), BF16), S((nc, 512, 128), F32)] + x_shape,
        scratch_shapes=[pltpu.VMEM((512, 128), F32)] + x_sems,
        compiler_params=_params(("arbitrary",)),
    )(proj, xbc, proj, dtb, alog, dsk, nw, *([] if exchange is None else exchange.arrays))


def _ssd_bwd(proj, xbc, states, dtb, alog, dsk, nw, dcat, dcol0, name):
    t = proj.shape[0]
    nc = t // CHUNK
    db = dcol0 // 512

    def body(z_ref, x_ref, dt_ref, st_ref, dtb_ref, al_ref, d_ref, nw_ref, do_ref,
             dz_ref, dx_ref, ddt_ref, gdtb_ref, gal_ref, gd_ref, gnw_ref, dstate):
        @pl.when(pl.program_id(0) == 0)
        def _():
            dstate[...] = jnp.zeros_like(dstate)
            gdtb_ref[...] = jnp.zeros_like(gdtb_ref)
            gal_ref[...] = jnp.zeros_like(gal_ref)
            gd_ref[...] = jnp.zeros_like(gd_ref)
            gnw_ref[...] = jnp.zeros_like(gnw_ref)

        _, vjp = jax.vjp(_ssd_chunk, z_ref[...], x_ref[...], dt_ref[...], st_ref[0], dtb_ref[...], al_ref[...],
                         d_ref[...], nw_ref[...])
        dz, dx, ddt, dst, gdtb, gal, gd, gnw = vjp((do_ref[...], dstate[...]))
        dz_ref[...] = dz.astype(dz_ref.dtype)
        dx_ref[...] = dx
        ddt_ref[...] = ddt.astype(ddt_ref.dtype)
        dstate[...] = dst
        gdtb_ref[...] += gdtb
        gal_ref[...] += gal
        gd_ref[...] += gd
        gnw_ref[...] += gnw

    rev = lambda i: nc - 1 - i
    vec = pl.BlockSpec((1, LANES), lambda i: (0, 0))
    vec512 = pl.BlockSpec((1, 512), lambda i: (0, 0))
    return pl.pallas_call(
        body, grid=(nc,), name=name,
        in_specs=[pl.BlockSpec((CHUNK, 512), lambda i: (rev(i), PC_SZ // 512)),
                  pl.BlockSpec((CHUNK, 1024), lambda i: (rev(i), 0)),
                  pl.BlockSpec((CHUNK, 128), lambda i: (rev(i), PC_DT // 128)),
                  pl.BlockSpec((1, 512, 128), lambda i: (rev(i), 0, 0)), vec, vec, vec, vec512,
                  pl.BlockSpec((CHUNK, 512), lambda i: (rev(i), db))],
        out_specs=[pl.BlockSpec((CHUNK, 512), lambda i: (rev(i), 0)), pl.BlockSpec((CHUNK, 1024), lambda i: (rev(i), 0)),
                   pl.BlockSpec((CHUNK, 128), lambda i: (rev(i), 0)), vec, vec, vec, vec512],
        out_shape=[S((t, 512), BF16), S((t, 1024), F32), S((t, 128), BF16), S((1, LANES), F32), S((1, LANES), F32),
                   S((1, LANES), F32), S((1, 512), F32)],
        scratch_shapes=[pltpu.VMEM((512, 128), F32)],
        compiler_params=_params(("arbitrary",)),
    )(proj, xbc, proj, states, dtb, alog, dsk, nw, dcat)


SOLVE_PREC = lax.Precision.HIGH


def _unit_lower_inverses(nas, known=None):
    def compute(ns):
        if known is not None:
            return tuple(known)
        n = ns[0].shape[0]
        r, c = _iota2(n, n)
        eye = (r == c).astype(F32)
        tm, pw = [eye + a for a in ns], list(ns)
        for _ in range(5):
            pw = [_dot(p, p, SOLVE_PREC) for p in pw]
            tm = [t + _dot(t, p, SOLVE_PREC) for t, p in zip(tm, pw)]
        return tuple(tm)

    inv = jax.custom_vjp(compute)

    def fwd(ns):
        ts = compute(ns)
        return ts, ts

    def bwd(ts, gs):
        part = [_dot_nt(g, t, SOLVE_PREC) for g, t in zip(gs, ts)]
        return (tuple(_dot_tn(t, p, SOLVE_PREC) for t, p in zip(ts, part)),)

    inv.defvjp(fwd, bwd)
    return inv(nas)


def _gdn_chunk(qkv, z, ba, state, dtb, alog, nw, known_inverses=None):
    n = CHUNK
    r, c = _iota2(n, n)
    tril = r >= c
    stril = r > c
    eye = (r == c).astype(F32)
    beta_all = jax.nn.sigmoid(ba)
    gcs = _dot(tril.astype(F32), -jnp.exp(alog) * jax.nn.softplus(ba + dtb), HI)
    heads = range(4)
    qh = [qkv[:, 64 * h:64 * h + 64] for h in heads]
    kh = [qkv[:, 256 + 64 * h:256 + 64 * h + 64] for h in heads]
    vh = [qkv[:, 512 + 64 * h:512 + 64 * h + 64] for h in heads]
    qn = [q * lax.rsqrt(jnp.sum(q * q, axis=-1, keepdims=True) + EPS) * 0.125 for q in qh]
    kn = [k * lax.rsqrt(jnp.sum(k * k, axis=-1, keepdims=True) + EPS) for k in kh]
    beta = [_pick_col(beta_all, h) for h in heads]
    gcol = [_pick_col(gcs, 4 + h) for h in heads]
    grow = [_col_to_row(g, eye) for g in gcol]
    decay = [jnp.where(tril, jnp.exp(jnp.where(tril, gc - gr, 0.0)), 0.0) for gc, gr in zip(gcol, grow)]
    kbeta = [k * b for k, b in zip(kn, beta)]
    kk = [_dot_nt(kb, k) for kb, k in zip(kbeta, kn)]
    qk = [_dot_nt(q, k) * dc for q, k, dc in zip(qn, kn, decay)]
    known = None if known_inverses is None else [known_inverses[64 * h:64 * h + 64, :] for h in heads]
    tms = _unit_lower_inverses(tuple(-jnp.where(stril, x * dc, 0.0) for x, dc in zip(kk, decay)), known)
    rhs = [jnp.concatenate([v * b, kb * jnp.exp(g)], axis=1) for v, b, kb, g in zip(vh, beta, kbeta, gcol)]
    sol = [_dot(t, x, SOLVE_PREC) for t, x in zip(tms, rhs)]
    st = [state[64 * h:64 * h + 64, :] for h in heads]
    v_new = [s_[:, :64] - _dot(s_[:, 64:], s) for s_, s in zip(sol, st)]
    o = [_dot(q * jnp.exp(g), s) + _dot(x, vn) for q, g, s, x, vn in zip(qn, gcol, st, qk, v_new)]
    glast = [_pick_row(g, n - 1) for g in gcol]
    new_states = [s * jnp.exp(gl) + _dot_tn(k * jnp.exp(gl - g), vn)
                  for s, gl, k, g, vn in zip(st, glast, kn, gcol, v_new)]
    o = [x * lax.rsqrt(jnp.mean(x * x, axis=-1, keepdims=True) + EPS) * nw for x in o]
    outs = [x * jax.nn.silu(z[:, 64 * h:64 * h + 64]) for h, x in zip(heads, o)]
    return jnp.concatenate(outs, axis=1), jnp.concatenate(new_states, axis=0), jnp.concatenate(tms, axis=0)


class _Exchange:
    def __init__(self, arrays, out_shape, n_sems, start, finish):
        self.arrays, self.out_shape, self.n_sems, self.start, self.finish = arrays, out_shape, n_sems, start, finish


def _hosted(exchange, n_in, n_out, n_steps, body):
    if exchange is None:
        return body, [], [], [], []
    k_in, k_out = len(exchange.arrays), len(exchange.out_shape)

    def hosted_body(*refs):
        ins, refs = refs[:n_in + k_in], refs[n_in + k_in:]
        outs, scratch = refs[:n_out + k_out], refs[n_out + k_out:]
        sems = scratch[-2:]
        step = pl.program_id(0)

        @pl.when(step == 0)
        def _():
            exchange.start(ins[n_in:], outs[n_out:], sems)

        body(*ins[:n_in], *outs[:n_out], *scratch[:-2])

        @pl.when(step == n_steps - 1)
        def _():
            exchange.finish(ins[n_in:], outs[n_out:], sems)

    return hosted_body, [ANY] * k_in, [ANY] * k_out, list(exchange.out_shape), _sem_pairs(exchange.n_sems)


def _gdn_fwd(proj, qkv, dtb, alog, nw, name, exchange=None):
    t = proj.shape[0]
    nc = t // CHUNK

    def body(q_ref, z_ref, ba_ref, dtb_ref, al_ref, nw_ref, o_ref, st_ref, inv_ref, state):
        @pl.when(pl.program_id(0) == 0)
        def _():
            state[...] = jnp.zeros_like(state)

        st_ref[0] = state[...]
        o, ns, tms = _gdn_chunk(q_ref[...], z_ref[...], ba_ref[...], state[...], dtb_ref[...], al_ref[...], nw_ref[...])
        o_ref[...] = o.astype(o_ref.dtype)
        inv_ref[0] = tms
        state[...] = ns

    body, x_in, x_out, x_shape, x_sems = _hosted(exchange, 6, 3, nc, body)
    vec = pl.BlockSpec((1, LANES), lambda i: (0, 0))
    per_chunk = pl.BlockSpec((1, 256, 64), lambda i: (i, 0, 0))
    return pl.pallas_call(
        body, grid=(nc,), name=name,
        in_specs=[pl.BlockSpec((CHUNK, 768), lambda i: (i, 0)), pl.BlockSpec((CHUNK, 256), lambda i: (i, PC_GZ // 256)),
                  pl.BlockSpec((CHUNK, 128), lambda i: (i, PC_BA // 128)), vec, vec, pl.BlockSpec((1, 64), lambda i: (0, 0))]
        + x_in,
        out_specs=[pl.BlockSpec((CHUNK, 256), lambda i: (i, 0)), per_chunk, per_chunk] + x_out,
        out_shape=[S((t, 256), BF16), S((nc, 256, 64), F32), S((nc, 256, 64), F32)] + x_shape,
        scratch_shapes=[pltpu.VMEM((256, 64), F32)] + x_sems,
        compiler_params=_params(("arbitrary",)),
    )(qkv, proj, proj, dtb, alog, nw, *([] if exchange is None else exchange.arrays))


def _gdn_bwd(proj, qkv, states, inverses, dtb, alog, nw, dcat, dcol0, name, exchange=None):
    t = proj.shape[0]
    nc = t // CHUNK
    db = dcol0 // 256

    def body(q_ref, z_ref, ba_ref, st_ref, inv_ref, dtb_ref, al_ref, nw_ref, do_ref,
             dq_ref, dz_ref, dba_ref, gdtb_ref, gal_ref, gnw_ref, dstate):
        @pl.when(pl.program_id(0) == 0)
        def _():
            dstate[...] = jnp.zeros_like(dstate)
            gdtb_ref[...] = jnp.zeros_like(gdtb_ref)
            gal_ref[...] = jnp.zeros_like(gal_ref)
            gnw_ref[...] = jnp.zeros_like(gnw_ref)

        def chunk(*operands):
            return _gdn_chunk(*operands, known_inverses=inv_ref[0])[:2]

        _, vjp = jax.vjp(chunk, q_ref[...], z_ref[...], ba_ref[...], st_ref[0], dtb_ref[...], al_ref[...], nw_ref[...])
        dq, dz, dba, dst, gdtb, gal, gnw = vjp((do_ref[...], dstate[...]))
        dq_ref[...] = dq
        dz_ref[...] = dz.astype(dz_ref.dtype)
        dba_ref[...] = dba.astype(dba_ref.dtype)
        dstate[...] = dst
        gdtb_ref[...] += gdtb
        gal_ref[...] += gal
        gnw_ref[...] += gnw

    body, x_in, x_out, x_shape, x_sems = _hosted(exchange, 9, 6, nc, body)
    rev = lambda i: nc - 1 - i
    vec = pl.BlockSpec((1, LANES), lambda i: (0, 0))
    vec64 = pl.BlockSpec((1, 64), lambda i: (0, 0))
    per_chunk = pl.BlockSpec((1, 256, 64), lambda i: (rev(i), 0, 0))
    return pl.pallas_call(
        body, grid=(nc,), name=name,
        in_specs=[pl.BlockSpec((CHUNK, 768), lambda i: (rev(i), 0)),
                  pl.BlockSpec((CHUNK, 256), lambda i: (rev(i), PC_GZ // 256)),
                  pl.BlockSpec((CHUNK, 128), lambda i: (rev(i), PC_BA // 128)),
                  per_chunk, per_chunk, vec, vec, vec64,
                  pl.BlockSpec((CHUNK, 256), lambda i: (rev(i), db))] + x_in,
        out_specs=[pl.BlockSpec((CHUNK, 768), lambda i: (rev(i), 0)), pl.BlockSpec((CHUNK, 256), lambda i: (rev(i), 0)),
                   pl.BlockSpec((CHUNK, 128), lambda i: (rev(i), 0)), vec, vec, vec64] + x_out,
        out_shape=[S((t, 768), F32), S((t, 256), BF16), S((t, 128), BF16), S((1, LANES), F32), S((1, LANES), F32),
                   S((1, 64), F32)] + x_shape,
        scratch_shapes=[pltpu.VMEM((256, 64), F32)] + x_sems,
        compiler_params=_params(("arbitrary",)),
    )(qkv, proj, proj, states, inverses, dtb, alog, nw, dcat, *([] if exchange is None else exchange.arrays))


def _pad_cols(w):
    z = jnp.zeros((w.shape[0], 120), w.dtype)
    return jnp.concatenate([w[:, 2056:2824], w[:, 2824:3080], w[:, 1024:2048], w[:, 0:512], w[:, 512:1024],
                            w[:, 2048:2056], z, w[:, 3080:3088], z], axis=1)


def _unpad_cols(g):
    return jnp.concatenate([g[:, PC_ATT:PC_ATT + 512], g[:, PC_SZ:PC_SZ + 512], g[:, PC_XBC:PC_XBC + 1024],
                            g[:, PC_DT:PC_DT + 8], g[:, PC_GQKV:PC_GQKV + 768], g[:, PC_GZ:PC_GZ + 256],
                            g[:, PC_BA:PC_BA + 8]], axis=1)


def _vec128(v, at=0):
    return jnp.zeros((1, LANES), F32).at[0, at:at + v.shape[0]].set(v)


def _in_weight(gathered):
    return _pad_cols(jnp.concatenate([gathered[k].reshape(D_MODEL, -1) for k in range(N_CHIPS)], axis=1))


def _matmul_weights(w_in, gathered, l):
    cols = lambda name: jnp.concatenate([gathered[name][k, l] for k in range(N_CHIPS)], axis=1)
    rows = lambda name: jnp.concatenate([gathered[name][k, l] for k in range(N_CHIPS)], axis=0)
    w_out = rows("w_out")
    return dict(
        w_in=w_in,
        w_out=jnp.concatenate([w_out[256:768], w_out[0:256], w_out[768:1024]], axis=0),
        w_gu=_interleave_gu(cols("ffn_w_gate"), cols("ffn_w_up")),
        w_down=rows("ffn_w_down"))


def _small_operands(w, l):
    return dict(
        pre_mix=w["pre_mix_norm"][l][None], post_mix=w["post_mix_norm"][l][None],
        pre_ffn=w["pre_ffn_norm"][l][None], post_ffn=w["post_ffn_norm"][l][None],
        sinks=_vec128(w["attn_sinks"][l]),
        s_cw=w["ssd_conv_w"][l], s_cb=w["ssd_conv_b"][l][None],
        s_dtb=_vec128(w["ssd_dt_bias"][l]), s_alog=_vec128(w["ssd_A_log"][l]), s_d=_vec128(w["ssd_D"][l]),
        s_nw=w["ssd_norm_w"][l][None],
        g_cw=w["gdn_conv_w"][l], g_cb=jnp.zeros((1, 768), F32),
        g_dtb=_vec128(w["gdn_dt_bias"][l], 4), g_alog=_vec128(w["gdn_A_log"][l], 4), g_nw=w["gdn_norm_w"][l][None],
    )


RAW_GRADS = ("w_in_pad", "w_out_cat", "w_gu", "ffn_w_down")
DW_ROWS = 4096


def _local_step(x, target, lw, w_in0, later_weights, matmul_weights, reduce_early):
    saved = []
    xin = x
    h = _prenorm(x, lw[0]["pre_mix"], "prenorm0")
    for l in range(DEPTH):
        p = lw[l]
        proj = _mm_nn(h, w_in0 if l == 0 else p["w_in"], 512, PC_TOT, F32, f"inproj{l}")
        xbc = _conv_fwd(proj, PC_XBC, 1024, p["s_cw"], p["s_cb"], f"ssd_conv{l}")
        gqkv = _conv_fwd(proj, PC_GQKV, 768, p["g_cw"], p["g_cb"], f"gdn_conv{l}")
        att = _swa_fwd(proj, p["sinks"], f"swa{l}")
        ssd, s_states, *landed_s = _ssd_fwd(proj, xbc, p["s_dtb"], p["s_alog"], p["s_d"], p["s_nw"], f"ssd{l}",
                                            later_weights[0] if l == 0 else None)
        gdn, g_states, g_inv, *landed_g = _gdn_fwd(proj, gqkv, p["g_dtb"], p["g_alog"], p["g_nw"], f"gdn{l}",
                                                   later_weights[1] if l == 0 else None)
        if l == 0:
            for q, mw in zip(lw, matmul_weights(landed_s + landed_g)):
                q.update(mw)
        cat = jnp.concatenate([ssd, att, gdn], axis=1)
        mix = _mm_nn(cat, p["w_out"], 512, 1024, F32, f"outproj{l}")
        x1, h2 = _resid_norm(xin, mix, p["post_mix"], p["pre_ffn"], f"postmix{l}")
        gu, act = _ffn_up(h2, p["w_gu"], f"ffn_gu{l}")
        f = _mm_nn(act, p["w_down"], 512, 1024, F32, f"ffn_down{l}")
        saved.append(dict(xin=xin, h=h, proj=proj, xbc=xbc, gqkv=gqkv, s_states=s_states, g_states=g_states, g_inv=g_inv,
                          cat=cat, mix=mix, x1=x1, h2=h2, gu=gu, act=act, f=f))
        if l + 1 < DEPTH:
            xin, h = _resid_norm(x1, f, p["post_ffn"], lw[l + 1]["pre_mix"], f"postffn{l}")

    g = {k: [None] * DEPTH for k in SMALL + CONV + RAW_GRADS}
    last = saved[-1]
    d_x2, d_f, loss_part, g["post_ffn_norm"][DEPTH - 1] = _resid_loss(
        last["x1"], last["f"], lw[-1]["post_ffn"], target, "loss")
    carried, early = None, []
    for l in reversed(range(DEPTH)):
        p, s = lw[l], saved[l]
        d_act = _mm_nt(d_f, p["w_down"], 512, FF, F32, f"d_act{l}")
        g["ffn_w_down"][l] = _mm_tn(s["act"], d_f, FF_HALF, 512, DW_ROWS, f"dw_down{l}")
        d_gu = _swiglu_bwd(s["gu"], d_act, f"d_swiglu{l}")
        d_h2 = _mm_nt(d_gu, p["w_gu"], 256, 1024, F32, f"d_h2{l}")
        g["w_gu"][l] = _mm_tn(s["h2"], d_gu, 512, FF_HALF, DW_ROWS, f"dw_gu{l}")
        d_x1, d_mix, g["pre_ffn_norm"][l], g["post_mix_norm"][l] = _resid_norm_bwd(
            s["x1"], s["mix"], d_x2, d_h2, p["post_mix"], p["pre_ffn"], f"d_postmix{l}")
        d_cat = _mm_nt(d_mix, p["w_out"], 512, 1024, F32, f"d_cat{l}")
        g["w_out_cat"][l] = _mm_tn(s["cat"], d_mix, 512, 1024, DW_ROWS, f"dw_out{l}")
        d_q, d_k, d_v, g_sinks = _swa_bwd(s["proj"], p["sinks"], d_cat, 512, f"d_swa{l}")
        d_sz, d_xbc, d_dt, g_dtb, g_alog, g_d, g["ssd_norm_w"][l] = _ssd_bwd(
            s["proj"], s["xbc"], s["s_states"], p["s_dtb"], p["s_alog"], p["s_d"], p["s_nw"], d_cat, 0, f"d_ssd{l}")
        if l == 0:
            carried = reduce_early(g)
        d_gq, d_gz, d_ba, gg_dtb, gg_alog, g["gdn_norm_w"][l], *landed = _gdn_bwd(
            s["proj"], s["gqkv"], s["g_states"], s["g_inv"], p["g_dtb"], p["g_alog"], p["g_nw"], d_cat, 768, f"d_gdn{l}",
            carried if l == 0 else None)
        if l == 0:
            early = landed
        d_xbc_raw, g["ssd_conv_w"][l], g["ssd_conv_b"][l] = _conv_bwd(
            s["proj"], PC_XBC, 1024, p["s_cw"], p["s_cb"], d_xbc, f"d_ssd_conv{l}")
        d_gq_raw, g["gdn_conv_w"][l], _ = _conv_bwd(s["proj"], PC_GQKV, 768, p["g_cw"], p["g_cb"], d_gq, f"d_gdn_conv{l}")
        d_proj = jnp.concatenate([d_gq_raw, d_gz, d_xbc_raw, d_q, d_k, d_v, d_sz, d_dt, d_ba], axis=1)
        d_h = _mm_nt(d_proj, p["w_in"], 512, 1024, F32, f"d_h{l}")
        g["w_in_pad"][l] = _mm_tn(s["h"], d_proj, 512, PC_TOT // 2, DW_ROWS, f"dw_in{l}")
        g["attn_sinks"][l] = g_sinks[0, :4]
        g["ssd_dt_bias"][l], g["ssd_A_log"][l], g["ssd_D"][l] = g_dtb[0, :8], g_alog[0, :8], g_d[0, :8]
        g["gdn_dt_bias"][l], g["gdn_A_log"][l] = gg_dtb[0, 4:8], gg_alog[0, 4:8]
        if l > 0:
            sp = saved[l - 1]
            d_x2, d_f, g["pre_mix_norm"][l], g["post_ffn_norm"][l - 1] = _resid_norm_bwd(
                s["xin"], sp["f"], d_x1, d_h, lw[l - 1]["post_ffn"], p["pre_mix"], f"d_postffn{l - 1}")
        else:
            grad_x, g["pre_mix_norm"][0] = _prenorm_bwd(s["xin"], d_x1, d_h, p["pre_mix"], "d_prenorm0")

    small = {k: jnp.stack([a.reshape(-1) for a in g[k]], axis=0) for k in SMALL + CONV}
    return loss_part, grad_x, small, {k: g[k] for k in RAW_GRADS}, early


BIG = (("w_in", 2), ("w_out", 1), ("ffn_w_gate", 2), ("ffn_w_up", 2), ("ffn_w_down", 1))
CONV = ("ssd_conv_w", "gdn_conv_w")
SMALL = ("pre_mix_norm", "post_mix_norm", "pre_ffn_norm", "post_ffn_norm", "attn_sinks", "ssd_conv_b", "ssd_dt_bias",
         "ssd_A_log", "ssd_D", "ssd_norm_w", "gdn_dt_bias", "gdn_A_log", "gdn_norm_w")


def _row_tile(rows, cap):
    best = rows
    for t in range(8, min(cap, rows) + 1, 8):
        if rows % t == 0:
            best = t
    return best


SMALL_UNIT = 8 * LANES


def _pack_small(vals):
    rows = []
    for a in vals:
        f = a.reshape(-1)
        pad = -f.shape[0] % SMALL_UNIT
        rows.append(jnp.concatenate([f, jnp.zeros((pad,), F32)]).reshape(-1, LANES))
    return jnp.concatenate(rows, axis=0)


def _unpack_small(mat, shapes):
    out, r = [], 0
    for shp in shapes:
        n = math.prod(shp)
        nr = -(-n // SMALL_UNIT) * 8
        out.append(mat[r:r + nr].reshape(-1)[:n].reshape(shp))
        r += nr
    return out


def _place():
    x, y, c = lax.axis_index("x"), lax.axis_index("y"), lax.axis_index("c")
    chips = [(1 - x, y), (x, 1 - y), (1 - x, 1 - y)]
    return x, y, c, chips


ANY = pl.BlockSpec(memory_space=pl.ANY)


def _remote(src, dst, sems, k, to):
    send_sems, recv_sems = sems
    return pltpu.make_async_remote_copy(src_ref=src, dst_ref=dst, send_sem=send_sems.at[k], recv_sem=recv_sems.at[k],
                                        device_id=to, device_id_type=MESH)


def _sem_pairs(n):
    return [pltpu.SemaphoreType.DMA((n,)), pltpu.SemaphoreType.DMA((n,))]


def _run_exchange(exchange, name):
    k = len(exchange.arrays)

    def body(*refs):
        ins, outs, sems = refs[:k], refs[k:-2], refs[-2:]
        exchange.start(ins, outs, sems)
        exchange.finish(ins, outs, sems)

    return pl.pallas_call(
        body, name=name, in_specs=[ANY] * k, out_specs=[ANY] * len(exchange.out_shape),
        out_shape=list(exchange.out_shape), scratch_shapes=_sem_pairs(exchange.n_sems),
    )(*exchange.arrays)


def _gather_exchange(shards):
    n = len(shards)

    def sends(s_refs, g_refs, sems):
        x, y, c, chips = _place()
        return [_remote(s_refs[i].at[c], g_refs[i].at[2 * x + y, c], sems, 6 * i + j, (px, py, c))
                for i in range(n) for j, (px, py) in enumerate(chips)]

    def start(s_refs, g_refs, sems):
        for cp in sends(s_refs, g_refs, sems):
            cp.start()

    def finish(s_refs, g_refs, sems):
        x, y, c, chips = _place()
        sib = (x, y, 1 - c)
        passed = []
        for j, (px, py) in enumerate(chips):
            for i in range(n):
                landed = g_refs[i].at[2 * px + py, c]
                _remote(landed, landed, sems, 6 * i + j, (px, py, c)).wait_recv()
                fw = _remote(landed, landed, sems, 6 * i + 3 + j, sib)
                fw.start()
                passed.append(fw)
        for j, (px, py) in enumerate(chips):
            for i in range(n):
                landed = g_refs[i].at[2 * px + py, 1 - c]
                _remote(landed, landed, sems, 6 * i + 3 + j, sib).wait_recv()
        for cp in sends(s_refs, g_refs, sems) + passed:
            cp.wait_send()

    return _Exchange(shards, [S((N_CHIPS,) + a.shape, a.dtype) for a in shards], 6 * n, start, finish)


def _exchange_halves(ds, name):
    n = len(ds)

    def body(*refs):
        d_refs, t_refs, sems = refs[:n], refs[n:2 * n], refs[2 * n:]
        x, y, c, _ = _place()
        cps = [_remote(d_refs[i].at[:, 1 - c], t_refs[i], sems, i, (x, y, 1 - c)) for i in range(n)]
        for cp in cps:
            cp.start()
        for cp in cps:
            cp.wait()

    return pl.pallas_call(
        body, name=name, in_specs=[ANY] * n, out_specs=[ANY] * n,
        out_shape=[S((N_CHIPS,) + a.shape[2:], a.dtype) for a in ds], scratch_shapes=_sem_pairs(n),
    )(*ds)


def _scatter_exchange(ps):
    n = len(ps)

    def copies(p_refs, u_refs, sems):
        x, y, c, chips = _place()
        return [_remote(p_refs[i].at[2 * px + py], u_refs[i].at[j], sems, 3 * i + j, (px, py, c))
                for j, (px, py) in enumerate(chips) for i in range(n)]

    def start(p_refs, u_refs, sems):
        for cp in copies(p_refs, u_refs, sems):
            cp.start()

    def finish(p_refs, u_refs, sems):
        for cp in copies(p_refs, u_refs, sems):
            cp.wait()

    return _Exchange(ps, [S((3,) + a.shape[1:], a.dtype) for a in ps], 3 * n, start, finish)


def _join_halves(qs, name):
    n = len(qs)

    def body(*refs):
        o_refs, sems = refs[n:2 * n], refs[2 * n:]
        x, y, c, _ = _place()
        cps = [_remote(o_refs[i].at[c], o_refs[i].at[c], sems, i, (x, y, 1 - c)) for i in range(n)]
        for cp in cps:
            cp.start()
        for i in range(n):
            other = o_refs[i].at[1 - c]
            _remote(other, other, sems, i, (x, y, 1 - c)).wait_recv()
        for cp in cps:
            cp.wait_send()

    return pl.pallas_call(
        body, name=name, in_specs=[ANY] * n, out_specs=[ANY] * n,
        out_shape=[S(a.shape, a.dtype) for a in qs], input_output_aliases={i: i for i in range(n)},
        scratch_shapes=_sem_pairs(n),
    )(*qs)


def _gather_small(v, name):
    def body(v_ref, o_ref, send_sems, recv_sems, local_sem):
        x, y, c, _ = _place()
        me = 4 * x + 2 * y + c
        mine = pltpu.make_async_copy(v_ref, o_ref.at[me], local_sem)
        mine.start()
        cps = []
        for k in range(1, N_DEV):
            fx, fy, fc = (k >> 2) & 1, (k >> 1) & 1, k & 1
            peer = (x ^ fx, y ^ fy, c ^ fc)
            cps.append(pltpu.make_async_remote_copy(
                src_ref=v_ref, dst_ref=o_ref.at[me], send_sem=send_sems.at[k - 1], recv_sem=recv_sems.at[k - 1],
                device_id=peer, device_id_type=MESH))
        for cp in cps:
            cp.start()
        for k in range(1, N_DEV):
            fx, fy, fc = (k >> 2) & 1, (k >> 1) & 1, k & 1
            dst = o_ref.at[4 * (x ^ fx) + 2 * (y ^ fy) + (c ^ fc)]
            pltpu.make_async_remote_copy(src_ref=dst, dst_ref=dst, send_sem=send_sems.at[k - 1],
                                         recv_sem=recv_sems.at[k - 1], device_id=(x, y, c),
                                         device_id_type=MESH).wait_recv()
        for cp in cps:
            cp.wait_send()
        mine.wait()

    return pl.pallas_call(
        body, name=name, in_specs=[ANY], out_specs=ANY, out_shape=S((N_DEV,) + v.shape, F32),
        scratch_shapes=[pltpu.SemaphoreType.DMA((N_DEV - 1,)), pltpu.SemaphoreType.DMA((N_DEV - 1,)),
                        pltpu.SemaphoreType.DMA],
    )(v)


def _sum_leading(a, name):
    n, rows, cols = a.shape
    tm = _row_tile(rows, 640)

    def body(a_ref, o_ref):
        acc = a_ref[0]
        for k in range(1, n):
            acc = acc + a_ref[k]
        o_ref[...] = acc

    return pl.pallas_call(
        body, grid=(rows // tm,), name=name, in_specs=[pl.BlockSpec((n, tm, cols), lambda i: (0, i, 0))],
        out_specs=pl.BlockSpec((tm, cols), lambda i: (i, 0)), out_shape=S((rows, cols), F32),
        compiler_params=_params(("arbitrary",)),
    )(a)


def _add_sibling(place, a, b, name):
    n, hr, cols = b.shape
    tm = _row_tile(hr, 512)

    def body(place_ref, a_ref, b_ref, o_ref, o16_ref):
        tot = a_ref[0] + b_ref[...]
        o_ref[...] = tot
        o16_ref[...] = tot.astype(BF16)

    spec = pl.BlockSpec((1, tm, cols), lambda k, i, pr: (k, i, 0))
    return pl.pallas_call(
        body, name=name, out_shape=[S((n, hr, cols), F32), S((n, hr, cols), BF16)],
        grid_spec=pltpu.PrefetchScalarGridSpec(
            num_scalar_prefetch=1, grid=(n, hr // tm),
            in_specs=[pl.BlockSpec((1, 1, tm, cols), lambda k, i, pr: (k, pr[0], i, 0)), spec], out_specs=[spec, spec]),
        compiler_params=_params(("arbitrary", "arbitrary")),
    )(place, a, b)


def _add_chips(place, sums, others, name):
    _, hr, cols = sums.shape
    tm = _row_tile(hr, 512)

    def body(place_ref, m_ref, o_ref, out_ref):
        acc = m_ref[0]
        for k in range(others.shape[0]):
            acc = acc + o_ref[k].astype(F32)
        out_ref[0] = acc

    return pl.pallas_call(
        body, name=name, out_shape=S((2, hr, cols), F32),
        grid_spec=pltpu.PrefetchScalarGridSpec(
            num_scalar_prefetch=1, grid=(hr // tm,),
            in_specs=[pl.BlockSpec((1, tm, cols), lambda i, pr: (pr[1], i, 0)),
                      pl.BlockSpec((others.shape[0], tm, cols), lambda i, pr: (0, i, 0))],
            out_specs=pl.BlockSpec((1, tm, cols), lambda i, pr: (pr[0], i, 0))),
        compiler_params=_params(("arbitrary",)),
    )(place, sums, others)


def _adamw(wt, g, m, v, name):
    shape = wt.shape
    cols = shape[-1]
    rows = math.prod(shape[:-1])
    tm = rows
    for cand in (512, 256, 128, 64, 32, 16, 8):
        if rows % cand == 0:
            tm = cand
            break
    c1 = 1.0 - ADAM_B1 ** ADAM_STEP
    c2 = 1.0 - ADAM_B2 ** ADAM_STEP

    def body(w_ref, g_ref, m_ref, v_ref, d_ref, nm_ref, nv_ref):
        gv = g_ref[...]
        nm = ADAM_B1 * m_ref[...] + (1.0 - ADAM_B1) * gv
        nv = ADAM_B2 * v_ref[...] + (1.0 - ADAM_B2) * (gv * gv)
        d_ref[...] = -ADAM_LR * ((nm / c1) / (jnp.sqrt(nv / c2) + ADAM_EPS) + ADAM_WD * w_ref[...])
        nm_ref[...] = nm
        nv_ref[...] = nv

    spec = pl.BlockSpec((tm, cols), lambda i: (i, 0))
    outs = pl.pallas_call(
        body, grid=(rows // tm,), name=name, in_specs=[spec] * 4, out_specs=[spec] * 3,
        out_shape=[S((rows, cols), F32)] * 3, compiler_params=_params(("arbitrary",)),
    )(*[a.reshape(rows, cols) for a in (wt, g, m, v)])
    return [o.reshape(shape) for o in outs]


WEIGHTS = ('pre_mix_norm', 'post_mix_norm', 'pre_ffn_norm', 'post_ffn_norm', 'w_in', 'w_out', 'attn_sinks', 'ssd_conv_w',
           'ssd_conv_b', 'ssd_dt_bias', 'ssd_A_log', 'ssd_D', 'ssd_norm_w', 'gdn_conv_w', 'gdn_dt_bias', 'gdn_A_log',
           'gdn_norm_w', 'ffn_w_gate', 'ffn_w_up', 'ffn_w_down')


def _chip_piece(i, raw, shape):
    if i == 0:
        g, axis = _unpad_cols(raw["w_in_pad"]), 1
    elif i == 1:
        g = raw["w_out_cat"]
        g, axis = jnp.concatenate([g[512:768], g[0:512], g[768:1024]], axis=0), 0
    elif i in (2, 3):
        g, axis = _split_gu(raw["w_gu"])[i - 2], 1
    else:
        g, axis = raw["ffn_w_down"], 0
    width = shape[1 + axis]
    parts = jnp.stack([lax.slice_in_dim(g, k * width, (k + 1) * width, axis=axis) for k in range(N_CHIPS)])
    return parts.reshape(N_CHIPS, 2, shape[1] // 2, shape[2])


def _step(x, target, wts, ms, vs):
    chip = 2 * lax.axis_index("x") + lax.axis_index("y")
    place = jnp.stack([lax.axis_index("c"), chip]).astype(jnp.int32)
    big_names = [k for k, _ in BIG]
    big_shapes = [wts[k].shape for k in big_names]
    own = lambda gathered, shard: lax.dynamic_update_index_in_dim(gathered, shard, chip, 0)
    halves = lambda a: a.reshape((2, a.shape[0] // 2) + a.shape[1:])

    w16 = {k: wts[k].astype(BF16) for k in big_names}
    first = halves(w16["w_in"][0])
    w_in0 = _in_weight(own(_run_exchange(_gather_exchange([first]), "gather_w_in0")[0], first))
    later = [halves(w16["w_in"][1])] + [w16[k] for k in big_names[1:]]
    with_ssd = 2

    def matmul_weights(landed):
        landed = [own(g, s) for g, s in zip(landed, later)]
        gathered = dict(zip(big_names[1:], landed[1:]))
        return [_matmul_weights(w_in0 if l == 0 else _in_weight(landed[0]), gathered, l) for l in range(DEPTH)]

    conv = _gather_small(_pack_small([wts[k] for k in CONV]), "gather_conv_weights")
    conv = [_unpack_small(conv[2 * k], [wts[n].shape for n in CONV]) for k in range(N_CHIPS)]
    w_all = dict(wts)
    for i, n in enumerate(CONV):
        w_all[n] = jnp.concatenate([conv[k][i] for k in range(N_CHIPS)], axis=2)

    early_keys = [(DEPTH - 1, 0)] + [(l, i) for l in reversed(range(DEPTH)) for i in range(1, len(BIG))]
    late_keys = [(l, 0) for l in range(DEPTH - 1)]

    def reduce_begin(tag, keys, g):
        pieces = [_chip_piece(i, {k: g[k][l] for k in RAW_GRADS}, big_shapes[i]) for l, i in keys]
        sib = _exchange_halves(pieces, f"exchange_halves_{tag}")
        return [_add_sibling(place, p, t, f"add_sibling_{tag}{n}") for n, (p, t) in enumerate(zip(pieces, sib))]

    def reduce_end(tag, keys, sums, others):
        parts = [_add_chips(place, s32, o, f"add_chips_{tag}{n}") for n, ((s32, _), o) in enumerate(zip(sums, others))]
        joined = _join_halves(parts, f"join_halves_{tag}")
        return {key: q.reshape(big_shapes[key[1]][1:]) for key, q in zip(keys, joined)}

    early_sums = []

    def reduce_early(g):
        early_sums.extend(reduce_begin("early", early_keys, g))
        return _scatter_exchange([s16 for _, s16 in early_sums])

    loss_part, grad_x, small_g, raw, early = _local_step(
        x[0], target[0], [_small_operands(w_all, l) for l in range(DEPTH)], w_in0,
        [_gather_exchange(later[:with_ssd]), _gather_exchange(later[with_ssd:])], matmul_weights, reduce_early)

    reduced = reduce_end("early", early_keys, early_sums, early)
    late_sums = reduce_begin("late", late_keys, raw)
    others = _run_exchange(_scatter_exchange([s16 for _, s16 in late_sums]), "scatter_chips_late")
    reduced.update(reduce_end("late", late_keys, late_sums, others))
    g_all = {k: jnp.stack([reduced[(l, i)] for l in range(DEPTH)]) for i, k in enumerate(big_names)}

    names = SMALL + CONV
    packed = _pack_small([small_g[k] for k in names] + [loss_part])
    small_sum = _sum_leading(_gather_small(packed, "gather_small_grads"), "add_small")
    vals = _unpack_small(small_sum, [small_g[k].shape for k in names] + [(1, LANES)])
    loss = vals[-1][0, 0]
    for k, v in zip(names, vals[:-1]):
        if k in CONV:
            width = wts[k].shape[2]
            v = lax.dynamic_slice_in_dim(v.reshape(DEPTH, 4, -1), chip * width, width, axis=2)
        g_all[k] = v.reshape(wts[k].shape)

    shapes = [wts[k].shape for k in names]
    d_s, m_s, v_s = _adamw(_pack_small([wts[k] for k in names]), _pack_small([g_all[k] for k in names]),
                           _pack_small([ms[k] for k in names]), _pack_small([vs[k] for k in names]), "adamw_small")
    upd = dict(zip(names, zip(_unpack_small(d_s, shapes), _unpack_small(m_s, shapes), _unpack_small(v_s, shapes))))
    for k in big_names:
        upd[k] = _adamw(wts[k], g_all[k], ms[k], vs[k], f"adamw_{k}")
    return (loss, grad_x[None], *[g_all[k] for k in WEIGHTS], *[upd[k][0] for k in WEIGHTS],
            *[upd[k][1] for k in WEIGHTS], *[upd[k][2] for k in WEIGHTS])


def kernel(x, pre_mix_norm, post_mix_norm, pre_ffn_norm, post_ffn_norm, w_in, w_out, attn_sinks, ssd_conv_w, ssd_conv_b, ssd_dt_bias, ssd_A_log, ssd_D, ssd_norm_w, gdn_conv_w, gdn_dt_bias, gdn_A_log, gdn_norm_w, ffn_w_gate, ffn_w_up, ffn_w_down, loss_target, m_pre_mix_norm, m_post_mix_norm, m_pre_ffn_norm, m_post_ffn_norm, m_w_in, m_w_out, m_attn_sinks, m_ssd_conv_w, m_ssd_conv_b, m_ssd_dt_bias, m_ssd_A_log, m_ssd_D, m_ssd_norm_w, m_gdn_conv_w, m_gdn_dt_bias, m_gdn_A_log, m_gdn_norm_w, m_ffn_w_gate, m_ffn_w_up, m_ffn_w_down, v_pre_mix_norm, v_post_mix_norm, v_pre_ffn_norm, v_post_ffn_norm, v_w_in, v_w_out, v_attn_sinks, v_ssd_conv_w, v_ssd_conv_b, v_ssd_dt_bias, v_ssd_A_log, v_ssd_D, v_ssd_norm_w, v_gdn_conv_w, v_gdn_dt_bias, v_gdn_A_log, v_gdn_norm_w, v_ffn_w_gate, v_ffn_w_up, v_ffn_w_down):
    wts = dict(zip(WEIGHTS, (pre_mix_norm, post_mix_norm, pre_ffn_norm, post_ffn_norm, w_in, w_out, attn_sinks, ssd_conv_w, ssd_conv_b, ssd_dt_bias, ssd_A_log, ssd_D, ssd_norm_w, gdn_conv_w, gdn_dt_bias, gdn_A_log, gdn_norm_w, ffn_w_gate, ffn_w_up, ffn_w_down)))
    ms = dict(zip(WEIGHTS, (m_pre_mix_norm, m_post_mix_norm, m_pre_ffn_norm, m_post_ffn_norm, m_w_in, m_w_out, m_attn_sinks, m_ssd_conv_w, m_ssd_conv_b, m_ssd_dt_bias, m_ssd_A_log, m_ssd_D, m_ssd_norm_w, m_gdn_conv_w, m_gdn_dt_bias, m_gdn_A_log, m_gdn_norm_w, m_ffn_w_gate, m_ffn_w_up, m_ffn_w_down)))
    vs = dict(zip(WEIGHTS, (v_pre_mix_norm, v_post_mix_norm, v_pre_ffn_norm, v_post_ffn_norm, v_w_in, v_w_out, v_attn_sinks, v_ssd_conv_w, v_ssd_conv_b, v_ssd_dt_bias, v_ssd_A_log, v_ssd_D, v_ssd_norm_w, v_gdn_conv_w, v_gdn_dt_bias, v_gdn_A_log, v_gdn_norm_w, v_ffn_w_gate, v_ffn_w_up, v_ffn_w_down)))
    return _step(x, loss_target, wts, ms, vs)
```

```python
import functools
import math

import jax
import jax.numpy as jnp
from jax import lax
from jax.experimental import pallas as pl
from jax.experimental.pallas import tpu as pltpu

F32, BF16 = jnp.float32, jnp.bfloat16
HI = lax.Precision.HIGHEST
MESH = pl.DeviceIdType.MESH
S = jax.ShapeDtypeStruct

D_MODEL = 1024
DEPTH = 2
CHUNK = 64
SSD_CHUNK = 256
GDN_CHUNK = 128
EPS = 1e-6
FF = 2816
N_CHIPS = 4
N_DEV = 8
LANES = 128

VMEM_LIMIT_BYTES = 56 * 1024 * 1024

PC_GQKV, PC_GZ, PC_XBC, PC_ATT, PC_SZ, PC_DT, PC_BA, PC_TOT = 0, 768, 1024, 2048, 2560, 3072, 3200, 3328

ADAM_LR, ADAM_B1, ADAM_B2, ADAM_EPS, ADAM_WD, ADAM_STEP = 0.001, 0.9, 0.999, 1e-08, 0.01, 10

ALIBI_SLOPES = tuple(2.0 ** (-8.0 * (h + 1) / 4) for h in range(4))


def _params(sem=None, **kw):
    if sem is not None:
        kw["dimension_semantics"] = sem
    return pltpu.CompilerParams(vmem_limit_bytes=VMEM_LIMIT_BYTES, **kw)


def _dot(a, b, prec=None):
    return jnp.dot(a, b, precision=prec, preferred_element_type=F32)


def _dot_nt(a, b, prec=None):
    return lax.dot_general(a, b, (((1,), (1,)), ((), ())), precision=prec, preferred_element_type=F32)


def _dot_tn(a, b, prec=None):
    return lax.dot_general(a, b, (((0,), (0,)), ((), ())), precision=prec, preferred_element_type=F32)


def _iota2(n, m):
    return lax.broadcasted_iota(jnp.int32, (n, m), 0), lax.broadcasted_iota(jnp.int32, (n, m), 1)


def _pick_col(arr, idx):
    ci = lax.broadcasted_iota(jnp.int32, arr.shape, 1)
    return jnp.sum(jnp.where(ci == idx, arr, 0.0), axis=1, keepdims=True)


def _pick_row(arr, idx):
    ri = lax.broadcasted_iota(jnp.int32, arr.shape, 0)
    return jnp.sum(jnp.where(ri == idx, arr, 0.0), axis=0, keepdims=True)


def _col_to_row(col, eye):
    return jnp.sum(eye * col, axis=0, keepdims=True)


def _rms(x, w):
    return x * lax.rsqrt(jnp.mean(x * x, axis=-1, keepdims=True) + EPS) * w


def _mm_nn(a, b, tm, tn, out_dtype, name):
    m, k = a.shape
    n = b.shape[1]
    tm, tn = min(tm, m), min(tn, n)

    def body(a_ref, b_ref, o_ref):
        o_ref[...] = _dot(a_ref[...], b_ref[...]).astype(o_ref.dtype)

    return pl.pallas_call(
        body, grid=(n // tn, m // tm), name=name,
        in_specs=[pl.BlockSpec((tm, k), lambda j, i: (i, 0)), pl.BlockSpec((k, tn), lambda j, i: (0, j))],
        out_specs=pl.BlockSpec((tm, tn), lambda j, i: (i, j)),
        out_shape=S((m, n), out_dtype), compiler_params=_params(("arbitrary", "arbitrary")),
    )(a, b)


def _mm_nt(a, b, tm, tn, out_dtype, name):
    m, k = a.shape
    n = b.shape[0]
    tm, tn = min(tm, m), min(tn, n)

    def body(a_ref, b_ref, o_ref):
        o_ref[...] = _dot_nt(a_ref[...], b_ref[...]).astype(o_ref.dtype)

    return pl.pallas_call(
        body, grid=(n // tn, m // tm), name=name,
        in_specs=[pl.BlockSpec((tm, k), lambda j, i: (i, 0)), pl.BlockSpec((tn, k), lambda j, i: (j, 0))],
        out_specs=pl.BlockSpec((tm, tn), lambda j, i: (i, j)),
        out_shape=S((m, n), out_dtype), compiler_params=_params(("arbitrary", "arbitrary")),
    )(a, b)


def _mm_tn(a, b, tm, tn, tk, name):
    t, m = a.shape
    n = b.shape[1]
    tm, tn, tk = min(tm, m), min(tn, n), min(tk, t)

    def body(a_ref, b_ref, o_ref):
        part = _dot_tn(a_ref[...], b_ref[...])

        @pl.when(pl.program_id(2) == 0)
        def _():
            o_ref[...] = part

        @pl.when(pl.program_id(2) > 0)
        def _():
            o_ref[...] += part

    return pl.pallas_call(
        body, grid=(m // tm, n // tn, t // tk), name=name,
        in_specs=[pl.BlockSpec((tk, tm), lambda i, j, k: (k, i)), pl.BlockSpec((tk, tn), lambda i, j, k: (k, j))],
        out_specs=pl.BlockSpec((tm, tn), lambda i, j, k: (i, j)),
        out_shape=S((m, n), F32), compiler_params=_params(("arbitrary", "arbitrary", "arbitrary")),
    )(a, b)


def _rowcall(fn, rows, params, row_outs, acc_outs, name, tm=512):
    t = rows[0].shape[0]
    tm = min(tm, t)
    n_in = len(rows) + len(params)
    n_ro = len(row_outs)

    def body(*refs):
        ro, ao = fn(*[r[...] for r in refs[:n_in]])
        for ref, v in zip(refs[n_in:n_in + n_ro], ro):
            ref[...] = v.astype(ref.dtype)
        acc_refs = refs[n_in + n_ro:]
        if acc_refs:
            @pl.when(pl.program_id(0) == 0)
            def _():
                for ref, v in zip(acc_refs, ao):
                    ref[...] = v

            @pl.when(pl.program_id(0) > 0)
            def _():
                for ref, v in zip(acc_refs, ao):
                    ref[...] += v

    in_specs = [pl.BlockSpec((tm, r.shape[1]), lambda i: (i, 0)) for r in rows]
    in_specs += [pl.BlockSpec(p.shape, lambda i: (0, 0)) for p in params]
    out_specs = [pl.BlockSpec((tm, c), lambda i: (i, 0)) for c, _ in row_outs]
    out_specs += [pl.BlockSpec(shape, lambda i: (0, 0)) for shape in acc_outs]
    out_shape = [S((t, c), dt) for c, dt in row_outs] + [S(shape, F32) for shape in acc_outs]
    return pl.pallas_call(
        body, grid=(t // tm,), name=name, in_specs=in_specs, out_specs=out_specs, out_shape=out_shape,
        compiler_params=_params(("arbitrary",)),
    )(*rows, *params)


def _prenorm(x, w, name):
    def fn(x, w):
        return (_rms(x, w),), ()
    return _rowcall(fn, [x], [w], [(D_MODEL, BF16)], [], name)[0]


def _resid_norm(xin, m, w_post, w_next, name):
    def fn(xin, m, w_post, w_next):
        xo = xin + _rms(m, w_post)
        return (xo, _rms(xo, w_next)), ()
    return _rowcall(fn, [xin, m], [w_post, w_next], [(D_MODEL, F32), (D_MODEL, BF16)], [], name)


def _resid_loss(xin, m, w_post, target, name):
    def fn(xin, m, target, w_post):
        r, vjp = jax.vjp(_rms, m, w_post)
        err = xin + r - target
        dy = err * (1.0 / D_MODEL)
        dm, dw = vjp(dy)
        tot = jnp.sum(jnp.sum(err * err, axis=1, keepdims=True), axis=0, keepdims=True) * (0.5 / D_MODEL)
        lane = lax.broadcasted_iota(jnp.int32, (1, LANES), 1)
        return (dy, dm), (jnp.where(lane == 0, tot, 0.0), dw)
    return _rowcall(fn, [xin, m, target], [w_post], [(D_MODEL, F32), (D_MODEL, BF16)],
                    [(1, LANES), (1, D_MODEL)], name)


def _resid_norm_bwd(x_out, m, d_direct, dh, w_post, w_next, name):
    def fn(x_out, m, d_direct, dh, w_post, w_next):
        _, vjp_n = jax.vjp(_rms, x_out, w_next)
        dx, dwn = vjp_n(dh)
        d_total = d_direct + dx
        _, vjp_p = jax.vjp(_rms, m, w_post)
        dm, dwp = vjp_p(d_total)
        return (d_total, dm), (dwn, dwp)
    return _rowcall(fn, [x_out, m, d_direct, dh], [w_post, w_next], [(D_MODEL, F32), (D_MODEL, BF16)],
                    [(1, D_MODEL), (1, D_MODEL)], name)


def _prenorm_bwd(x, d_direct, dh, w, name):
    def fn(x, d_direct, dh, w):
        _, vjp = jax.vjp(_rms, x, w)
        dx, dw = vjp(dh)
        return (d_direct + dx,), (dw,)
    return _rowcall(fn, [x, d_direct, dh], [w], [(D_MODEL, F32)], [(1, D_MODEL)], name)


FF_HALF = FF // 2


def _interleave_gu(gate, up):
    return jnp.concatenate([gate[:, :FF_HALF], up[:, :FF_HALF], gate[:, FF_HALF:], up[:, FF_HALF:]], axis=1)


def _split_gu(gu):
    return (jnp.concatenate([gu[:, :FF_HALF], gu[:, FF:FF + FF_HALF]], axis=1),
            jnp.concatenate([gu[:, FF_HALF:FF], gu[:, FF + FF_HALF:]], axis=1))


def _swiglu_pair(gu):
    n = gu.shape[1] // 2
    return jax.nn.silu(gu[:, :n]) * gu[:, n:]


def _ffn_up(h2, w_gu, name, tm=512):
    t, k = h2.shape
    tm = min(tm, t)

    def body(a_ref, b_ref, gu_ref, act_ref):
        gu = _dot(a_ref[...], b_ref[...])
        gu_ref[...] = gu.astype(gu_ref.dtype)
        act_ref[...] = _swiglu_pair(gu).astype(act_ref.dtype)

    return pl.pallas_call(
        body, grid=(2, t // tm), name=name,
        in_specs=[pl.BlockSpec((tm, k), lambda j, i: (i, 0)), pl.BlockSpec((k, FF), lambda j, i: (0, j))],
        out_specs=[pl.BlockSpec((tm, FF), lambda j, i: (i, j)), pl.BlockSpec((tm, FF_HALF), lambda j, i: (i, j))],
        out_shape=[S((t, 2 * FF), BF16), S((t, FF), BF16)], compiler_params=_params(("arbitrary", "arbitrary")),
    )(h2, w_gu)


def _swiglu_bwd(gu, da, name):
    def fn(gu, da):
        gu = gu.astype(F32)
        halves = []
        for b in range(2):
            _, vjp = jax.vjp(_swiglu_pair, gu[:, FF * b:FF * (b + 1)])
            halves.append(vjp(da[:, FF_HALF * b:FF_HALF * (b + 1)])[0])
        return (jnp.concatenate(halves, axis=1),), ()
    return _rowcall(fn, [gu, da], [], [(2 * FF, BF16)], [], name, tm=256)[0]


def _conv_fwd(proj, col0, width, w, b, name, tm=512):
    t = proj.shape[0]
    tm = min(tm, t)
    cb = col0 // width

    def body(x_ref, w_ref, b_ref, o_ref, ext):
        @pl.when(pl.program_id(0) == 0)
        def _():
            ext[0:8, :] = jnp.zeros((8, width), F32)

        ext[8:8 + tm, :] = x_ref[...]
        y = b_ref[...] + w_ref[0:1, :] * ext[pl.ds(5, tm), :]
        for k in range(1, 4):
            y = y + w_ref[k:k + 1, :] * ext[pl.ds(5 + k, tm), :]
        o_ref[...] = jax.nn.silu(y)
        ext[0:8, :] = ext[tm:tm + 8, :]

    return pl.pallas_call(
        body, grid=(t // tm,), name=name,
        in_specs=[pl.BlockSpec((tm, width), lambda i: (i, cb)), pl.BlockSpec((4, width), lambda i: (0, 0)),
                  pl.BlockSpec((1, width), lambda i: (0, 0))],
        out_specs=pl.BlockSpec((tm, width), lambda i: (i, 0)),
        out_shape=S((t, width), F32), scratch_shapes=[pltpu.VMEM((tm + 8, width), F32)],
        compiler_params=_params(("arbitrary",)),
    )(proj, w, b)


def _conv_bwd(proj, col0, width, w, b, dact, name, tm=512):
    t = proj.shape[0]
    tm = min(tm, t)
    nb = t // tm
    cb = col0 // width
    hb = tm // 8

    def body(x_ref, halo_ref, d_ref, w_ref, b_ref, dx_ref, dw_ref, db_ref, extx, extd):
        i = pl.program_id(0)
        blk = nb - 1 - i

        @pl.when(i == 0)
        def _():
            extd[tm:tm + 8, :] = jnp.zeros((8, width), F32)
            dw_ref[...] = jnp.zeros((4, width), F32)
            db_ref[...] = jnp.zeros((1, width), F32)

        extx[0:8, :] = jnp.where(blk == 0, 0.0, halo_ref[...])
        extx[8:8 + tm, :] = x_ref[...]
        y = b_ref[...] + w_ref[0:1, :] * extx[pl.ds(5, tm), :]
        for k in range(1, 4):
            y = y + w_ref[k:k + 1, :] * extx[pl.ds(5 + k, tm), :]
        sig = jax.nn.sigmoid(y)
        dy = d_ref[...] * (sig * (1.0 + y * (1.0 - sig)))
        extd[0:tm, :] = dy
        dx = w_ref[0:1, :] * extd[pl.ds(3, tm), :]
        for k in range(1, 4):
            dx = dx + w_ref[k:k + 1, :] * extd[pl.ds(3 - k, tm), :]
        dx_ref[...] = dx.astype(dx_ref.dtype)
        for k in range(4):
            dw_ref[k:k + 1, :] += jnp.sum(dy * extx[pl.ds(5 + k, tm), :], axis=0, keepdims=True)
        db_ref[...] += jnp.sum(dy, axis=0, keepdims=True)
        extd[tm:tm + 8, :] = extd[0:8, :]

    return pl.pallas_call(
        body, grid=(nb,), name=name,
        in_specs=[pl.BlockSpec((tm, width), lambda i: (nb - 1 - i, cb)),
                  pl.BlockSpec((8, width), lambda i: (jnp.maximum((nb - 1 - i) * hb - 1, 0), cb)),
                  pl.BlockSpec((tm, width), lambda i: (nb - 1 - i, 0)),
                  pl.BlockSpec((4, width), lambda i: (0, 0)), pl.BlockSpec((1, width), lambda i: (0, 0))],
        out_specs=[pl.BlockSpec((tm, width), lambda i: (nb - 1 - i, 0)), pl.BlockSpec((4, width), lambda i: (0, 0)),
                   pl.BlockSpec((1, width), lambda i: (0, 0))],
        out_shape=[S((t, width), BF16), S((4, width), F32), S((1, width), F32)],
        scratch_shapes=[pltpu.VMEM((tm + 8, width), F32), pltpu.VMEM((tm + 8, width), F32)],
        compiler_params=_params(("arbitrary",)),
    )(proj, proj, dact, w, b)


SWA_BQ = 256
SWA_BACK = 128


def _swa_block(q, kw, vw, sinks, blk):
    nq, nk = SWA_BQ, SWA_BQ + SWA_BACK
    r, j = _iota2(nq, nk)
    rel = r // CHUNK + 2 - j // CHUNK
    valid = (rel >= 0) & (rel <= 2) & (blk * (SWA_BQ // CHUNK) + j // CHUNK - 2 >= 0)
    dist = jnp.abs(r + SWA_BACK - j).astype(F32)
    outs = []
    for h in range(4):
        kv = h // 2
        qh = q[:, 64 * h:64 * h + 64]
        kh = kw[:, 64 * kv:64 * kv + 64]
        vh = vw[:, 64 * kv:64 * kv + 64]
        s = _dot_nt(qh, kh) * 0.125 - ALIBI_SLOPES[h] * dist
        s = jnp.where(valid, s, -1e30)
        sink = _pick_col(sinks, h)
        m = jnp.maximum(jnp.max(s, axis=1, keepdims=True), sink)
        e = jnp.exp(s - m)
        den = jnp.sum(e, axis=1, keepdims=True) + jnp.exp(sink - m)
        outs.append(_dot(e / den, vh))
    return jnp.concatenate(outs, axis=1)


def _swa_fwd(proj, sinks, name):
    t = proj.shape[0]
    qb, kb = PC_ATT // 256, PC_ATT // 128 + 2
    win = SWA_BQ + SWA_BACK

    def body(q_ref, k_ref, v_ref, s_ref, o_ref, kp, vp):
        i = pl.program_id(0)

        @pl.when(i == 0)
        def _():
            kp[0:SWA_BACK, :] = jnp.zeros((SWA_BACK, 128), F32)
            vp[0:SWA_BACK, :] = jnp.zeros((SWA_BACK, 128), F32)
            kp[SWA_BACK:, :] = k_ref[...]
            vp[SWA_BACK:, :] = v_ref[...]

        start = pl.multiple_of(i * SWA_BQ, SWA_BQ)
        o = _swa_block(q_ref[...], kp[pl.ds(start, win), :], vp[pl.ds(start, win), :], s_ref[...], i)
        o_ref[...] = o.astype(o_ref.dtype)

    return pl.pallas_call(
        body, grid=(t // SWA_BQ,), name=name,
        in_specs=[pl.BlockSpec((SWA_BQ, 256), lambda i: (i, qb)), pl.BlockSpec((t, 128), lambda i: (0, kb)),
                  pl.BlockSpec((t, 128), lambda i: (0, kb + 1)), pl.BlockSpec((1, LANES), lambda i: (0, 0))],
        out_specs=pl.BlockSpec((SWA_BQ, 256), lambda i: (i, 0)),
        out_shape=S((t, 256), BF16),
        scratch_shapes=[pltpu.VMEM((t + SWA_BACK, 128), F32), pltpu.VMEM((t + SWA_BACK, 128), F32)],
        compiler_params=_params(("arbitrary",)),
    )(proj, proj, proj, sinks)


def _swa_bwd(proj, sinks, dcat, dcol0, name):
    t = proj.shape[0]
    nb = t // SWA_BQ
    qb, kb = PC_ATT // 256, PC_ATT // 128 + 2
    db = dcol0 // 256
    win = SWA_BQ + SWA_BACK

    def body(q_ref, k_ref, v_ref, s_ref, do_ref, dq_ref, dk_ref, dv_ref, ds_ref, kp, vp, dkp, dvp):
        i = pl.program_id(0)

        @pl.when(i == 0)
        def _():
            kp[0:SWA_BACK, :] = jnp.zeros((SWA_BACK, 128), F32)
            vp[0:SWA_BACK, :] = jnp.zeros((SWA_BACK, 128), F32)
            kp[SWA_BACK:, :] = k_ref[...]
            vp[SWA_BACK:, :] = v_ref[...]
            dkp[...] = jnp.zeros_like(dkp)
            dvp[...] = jnp.zeros_like(dvp)
            ds_ref[...] = jnp.zeros_like(ds_ref)

        start = pl.multiple_of(i * SWA_BQ, SWA_BQ)
        _, vjp = jax.vjp(functools.partial(_swa_block, blk=i), q_ref[...], kp[pl.ds(start, win), :],
                         vp[pl.ds(start, win), :], s_ref[...])
        dq, dkw, dvw, dsk = vjp(do_ref[...])
        dq_ref[...] = dq.astype(dq_ref.dtype)
        dkp[pl.ds(start, win), :] += dkw
        dvp[pl.ds(start, win), :] += dvw
        ds_ref[...] += dsk

        @pl.when(i == nb - 1)
        def _():
            dk_ref[...] = dkp[SWA_BACK:, :].astype(dk_ref.dtype)
            dv_ref[...] = dvp[SWA_BACK:, :].astype(dv_ref.dtype)

    return pl.pallas_call(
        body, grid=(nb,), name=name,
        in_specs=[pl.BlockSpec((SWA_BQ, 256), lambda i: (i, qb)), pl.BlockSpec((t, 128), lambda i: (0, kb)),
                  pl.BlockSpec((t, 128), lambda i: (0, kb + 1)), pl.BlockSpec((1, LANES), lambda i: (0, 0)),
                  pl.BlockSpec((SWA_BQ, 256), lambda i: (i, db))],
        out_specs=[pl.BlockSpec((SWA_BQ, 256), lambda i: (i, 0)), pl.BlockSpec((t, 128), lambda i: (0, 0)),
                   pl.BlockSpec((t, 128), lambda i: (0, 0)), pl.BlockSpec((1, LANES), lambda i: (0, 0))],
        out_shape=[S((t, 256), BF16), S((t, 128), BF16), S((t, 128), BF16), S((1, LANES), F32)],
        scratch_shapes=[pltpu.VMEM((t + SWA_BACK, 128), F32) for _ in range(4)],
        compiler_params=_params(("arbitrary",)),
    )(proj, proj, proj, sinks, dcat)


def _ssd_chunk(z, xbc, dt_raw, state, dtb, alog, dsk, nw):
    n = z.shape[0]
    r, c = _iota2(n, n)
    tril = r >= c
    eye = (r == c).astype(F32)
    dt = jax.nn.softplus(dt_raw + dtb)
    acs = _dot(tril.astype(F32), dt * (-jnp.exp(alog)), HI)
    xs, bm, cm = xbc[:, :512], xbc[:, 512:768], xbc[:, 768:1024]
    heads = range(8)
    bg = [bm[:, 128 * g:128 * g + 128] for g in range(2)]
    cg = [cm[:, 128 * g:128 * g + 128] for g in range(2)]
    cb = [_dot_nt(cg[g], bg[g]) for g in range(2)]
    dth = [_pick_col(dt, h) for h in heads]
    acol = [_pick_col(acs, h) for h in heads]
    arow = [_col_to_row(a, eye) for a in acol]
    lmat = [jnp.where(tril, jnp.exp(jnp.where(tril, a - b, 0.0)), 0.0) for a, b in zip(acol, arow)]
    xh = [xs[:, 64 * h:64 * h + 64] for h in heads]
    xc = [x * t for x, t in zip(xh, dth)]
    st = [state[64 * h:64 * h + 64, :] for h in heads]
    alast = [_pick_row(a, n - 1) for a in acol]
    y_in = [_dot(cb[h // 4] * lmat[h], xc[h]) for h in heads]
    y_st = [_dot_nt(cg[h // 4], st[h]) * jnp.exp(acol[h]) for h in heads]
    ys = [y_in[h] + y_st[h] + xh[h] * _pick_col(dsk, h) for h in heads]
    new_states = [st[h] * jnp.exp(alast[h]) + _dot_tn(xc[h] * jnp.exp(alast[h] - acol[h]), bg[h // 4]) for h in heads]
    gg = jnp.concatenate(ys, axis=1) * jax.nn.silu(z)
    outs = []
    for gi in range(2):
        gv = gg[:, 256 * gi:256 * gi + 256]
        outs.append(gv * lax.rsqrt(jnp.mean(gv * gv, axis=-1, keepdims=True) + EPS))
    return jnp.concatenate(outs, axis=1) * nw, jnp.concatenate(new_states, axis=0)


def _ssd_fwd(proj, xbc, dtb, alog, dsk, nw, name, exchange=None):
    t = proj.shape[0]
    CHUNK = min(SSD_CHUNK, t)
    nc = t // CHUNK

    def body(z_ref, x_ref, dt_ref, dtb_ref, al_ref, d_ref, nw_ref, o_ref, st_ref, state):
        @pl.when(pl.program_id(0) == 0)
        def _():
            state[...] = jnp.zeros_like(state)

        st_ref[0] = state[...]
        o, ns = _ssd_chunk(z_ref[...], x_ref[...], dt_ref[...], state[...], dtb_ref[...], al_ref[...], d_ref[...],
                           nw_ref[...])
        o_ref[...] = o.astype(o_ref.dtype)
        state[...] = ns

    body, x_in, x_out, x_shape, x_sems = _hosted(exchange, 7, 2, nc, body)
    vec = pl.BlockSpec((1, LANES), lambda i: (0, 0))
    return pl.pallas_call(
        body, grid=(nc,), name=name,
        in_specs=[pl.BlockSpec((CHUNK, 512), lambda i: (i, PC_SZ // 512)), pl.BlockSpec((CHUNK, 1024), lambda i: (i, 0)),
                  pl.BlockSpec((CHUNK, 128), lambda i: (i, PC_DT // 128)), vec, vec, vec,
                  pl.BlockSpec((1, 512), lambda i: (0, 0))] + x_in,
        out_specs=[pl.BlockSpec((CHUNK, 512), lambda i: (i, 0)), pl.BlockSpec((1, 512, 128), lambda i: (i, 0, 0))] + x_out,
        out_shape=[S((t, 512), BF16), S((nc, 512, 128), F32)] + x_shape,
        scratch_shapes=[pltpu.VMEM((512, 128), F32)] + x_sems,
        compiler_params=_params(("arbitrary",)),
    )(proj, xbc, proj, dtb, alog, dsk, nw, *([] if exchange is None else exchange.arrays))


def _ssd_bwd(proj, xbc, states, dtb, alog, dsk, nw, dcat, dcol0, name):
    t = proj.shape[0]
    CHUNK = min(SSD_CHUNK, t)
    nc = t // CHUNK
    db = dcol0 // 512

    def body(z_ref, x_ref, dt_ref, st_ref, dtb_ref, al_ref, d_ref, nw_ref, do_ref,
             dz_ref, dx_ref, ddt_ref, gdtb_ref, gal_ref, gd_ref, gnw_ref, dstate):
        @pl.when(pl.program_id(0) == 0)
        def _():
            dstate[...] = jnp.zeros_like(dstate)
            gdtb_ref[...] = jnp.zeros_like(gdtb_ref)
            gal_ref[...] = jnp.zeros_like(gal_ref)
            gd_ref[...] = jnp.zeros_like(gd_ref)
            gnw_ref[...] = jnp.zeros_like(gnw_ref)

        _, vjp = jax.vjp(_ssd_chunk, z_ref[...], x_ref[...], dt_ref[...], st_ref[0], dtb_ref[...], al_ref[...],
                         d_ref[...], nw_ref[...])
        dz, dx, ddt, dst, gdtb, gal, gd, gnw = vjp((do_ref[...], dstate[...]))
        dz_ref[...] = dz.astype(dz_ref.dtype)
        dx_ref[...] = dx
        ddt_ref[...] = ddt.astype(ddt_ref.dtype)
        dstate[...] = dst
        gdtb_ref[...] += gdtb
        gal_ref[...] += gal
        gd_ref[...] += gd
        gnw_ref[...] += gnw

    rev = lambda i: nc - 1 - i
    vec = pl.BlockSpec((1, LANES), lambda i: (0, 0))
    vec512 = pl.BlockSpec((1, 512), lambda i: (0, 0))
    return pl.pallas_call(
        body, grid=(nc,), name=name,
        in_specs=[pl.BlockSpec((CHUNK, 512), lambda i: (rev(i), PC_SZ // 512)),
                  pl.BlockSpec((CHUNK, 1024), lambda i: (rev(i), 0)),
                  pl.BlockSpec((CHUNK, 128), lambda i: (rev(i), PC_DT // 128)),
                  pl.BlockSpec((1, 512, 128), lambda i: (rev(i), 0, 0)), vec, vec, vec, vec512,
                  pl.BlockSpec((CHUNK, 512), lambda i: (rev(i), db))],
        out_specs=[pl.BlockSpec((CHUNK, 512), lambda i: (rev(i), 0)), pl.BlockSpec((CHUNK, 1024), lambda i: (rev(i), 0)),
                   pl.BlockSpec((CHUNK, 128), lambda i: (rev(i), 0)), vec, vec, vec, vec512],
        out_shape=[S((t, 512), BF16), S((t, 1024), F32), S((t, 128), BF16), S((1, LANES), F32), S((1, LANES), F32),
                   S((1, LANES), F32), S((1, 512), F32)],
        scratch_shapes=[pltpu.VMEM((512, 128), F32)],
        compiler_params=_params(("arbitrary",)),
    )(proj, xbc, proj, states, dtb, alog, dsk, nw, dcat)


SOLVE_PREC = lax.Precision.HIGH


def _unit_lower_inverses(nas, known=None):
    def compute(ns):
        if known is not None:
            return tuple(known)
        n = ns[0].shape[0]
        r, c = _iota2(n, n)
        eye = (r == c).astype(F32)
        tm, pw = [eye + a for a in ns], list(ns)
        for _ in range(n.bit_length() - 2):
            pw = [_dot(p, p, SOLVE_PREC) for p in pw]
            tm = [t + _dot(t, p, SOLVE_PREC) for t, p in zip(tm, pw)]
        return tuple(tm)

    inv = jax.custom_vjp(compute)

    def fwd(ns):
        ts = compute(ns)
        return ts, ts

    def bwd(ts, gs):
        part = [_dot_nt(g, t, SOLVE_PREC) for g, t in zip(gs, ts)]
        return (tuple(_dot_tn(t, p, SOLVE_PREC) for t, p in zip(ts, part)),)

    inv.defvjp(fwd, bwd)
    return inv(nas)


def _gdn_chunk(qkv, z, ba, state, dtb, alog, nw, known_inverses=None):
    n = qkv.shape[0]
    r, c = _iota2(n, n)
    tril = r >= c
    stril = r > c
    eye = (r == c).astype(F32)
    beta_all = jax.nn.sigmoid(ba)
    gcs = _dot(tril.astype(F32), -jnp.exp(alog) * jax.nn.softplus(ba + dtb), HI)
    heads = range(4)
    qh = [qkv[:, 64 * h:64 * h + 64] for h in heads]
    kh = [qkv[:, 256 + 64 * h:256 + 64 * h + 64] for h in heads]
    vh = [qkv[:, 512 + 64 * h:512 + 64 * h + 64] for h in heads]
    qn = [q * lax.rsqrt(jnp.sum(q * q, axis=-1, keepdims=True) + EPS) * 0.125 for q in qh]
    kn = [k * lax.rsqrt(jnp.sum(k * k, axis=-1, keepdims=True) + EPS) for k in kh]
    beta = [_pick_col(beta_all, h) for h in heads]
    gcol = [_pick_col(gcs, 4 + h) for h in heads]
    grow = [_col_to_row(g, eye) for g in gcol]
    decay = [jnp.where(tril, jnp.exp(jnp.where(tril, gc - gr, 0.0)), 0.0) for gc, gr in zip(gcol, grow)]
    kbeta = [k * b for k, b in zip(kn, beta)]
    kk = [_dot_nt(kb, k) for kb, k in zip(kbeta, kn)]
    qk = [_dot_nt(q, k) * dc for q, k, dc in zip(qn, kn, decay)]
    known = None if known_inverses is None else [known_inverses[n * h:n * (h + 1), :] for h in heads]
    tms = _unit_lower_inverses(tuple(-jnp.where(stril, x * dc, 0.0) for x, dc in zip(kk, decay)), known)
    rhs = [jnp.concatenate([v * b, kb * jnp.exp(g)], axis=1) for v, b, kb, g in zip(vh, beta, kbeta, gcol)]
    sol = [_dot(t, x, SOLVE_PREC) for t, x in zip(tms, rhs)]
    st = [state[64 * h:64 * h + 64, :] for h in heads]
    v_new = [s_[:, :64] - _dot(s_[:, 64:], s) for s_, s in zip(sol, st)]
    o = [_dot(q * jnp.exp(g), s) + _dot(x, vn) for q, g, s, x, vn in zip(qn, gcol, st, qk, v_new)]
    glast = [_pick_row(g, n - 1) for g in gcol]
    new_states = [s * jnp.exp(gl) + _dot_tn(k * jnp.exp(gl - g), vn)
                  for s, gl, k, g, vn in zip(st, glast, kn, gcol, v_new)]
    o = [x * lax.rsqrt(jnp.mean(x * x, axis=-1, keepdims=True) + EPS) * nw for x in o]
    outs = [x * jax.nn.silu(z[:, 64 * h:64 * h + 64]) for h, x in zip(heads, o)]
    return jnp.concatenate(outs, axis=1), jnp.concatenate(new_states, axis=0), jnp.concatenate(tms, axis=0)


class _Exchange:
    def __init__(self, arrays, out_shape, n_sems, start, finish):
        self.arrays, self.out_shape, self.n_sems, self.start, self.finish = arrays, out_shape, n_sems, start, finish


def _hosted(exchange, n_in, n_out, n_steps, body):
    if exchange is None:
        return body, [], [], [], []
    k_in, k_out = len(exchange.arrays), len(exchange.out_shape)

    def hosted_body(*refs):
        ins, refs = refs[:n_in + k_in], refs[n_in + k_in:]
        outs, scratch = refs[:n_out + k_out], refs[n_out + k_out:]
        sems = scratch[-2:]
        step = pl.program_id(0)

        @pl.when(step == 0)
        def _():
            exchange.start(ins[n_in:], outs[n_out:], sems)

        body(*ins[:n_in], *outs[:n_out], *scratch[:-2])

        @pl.when(step == n_steps - 1)
        def _():
            exchange.finish(ins[n_in:], outs[n_out:], sems)

    return hosted_body, [ANY] * k_in, [ANY] * k_out, list(exchange.out_shape), _sem_pairs(exchange.n_sems)


def _gdn_fwd(proj, qkv, dtb, alog, nw, name, exchange=None):
    t = proj.shape[0]
    CHUNK = min(GDN_CHUNK, t)
    nc = t // CHUNK

    def body(q_ref, z_ref, ba_ref, dtb_ref, al_ref, nw_ref, o_ref, st_ref, inv_ref, state):
        @pl.when(pl.program_id(0) == 0)
        def _():
            state[...] = jnp.zeros_like(state)

        st_ref[0] = state[...]
        o, ns, tms = _gdn_chunk(q_ref[...], z_ref[...], ba_ref[...], state[...], dtb_ref[...], al_ref[...], nw_ref[...])
        o_ref[...] = o.astype(o_ref.dtype)
        inv_ref[0] = tms
        state[...] = ns

    body, x_in, x_out, x_shape, x_sems = _hosted(exchange, 6, 3, nc, body)
    vec = pl.BlockSpec((1, LANES), lambda i: (0, 0))
    per_chunk = pl.BlockSpec((1, 256, 64), lambda i: (i, 0, 0))
    return pl.pallas_call(
        body, grid=(nc,), name=name,
        in_specs=[pl.BlockSpec((CHUNK, 768), lambda i: (i, 0)), pl.BlockSpec((CHUNK, 256), lambda i: (i, PC_GZ // 256)),
                  pl.BlockSpec((CHUNK, 128), lambda i: (i, PC_BA // 128)), vec, vec, pl.BlockSpec((1, 64), lambda i: (0, 0))]
        + x_in,
        out_specs=[pl.BlockSpec((CHUNK, 256), lambda i: (i, 0)), per_chunk,
                   pl.BlockSpec((1, 4 * CHUNK, CHUNK), lambda i: (i, 0, 0))] + x_out,
        out_shape=[S((t, 256), BF16), S((nc, 256, 64), F32), S((nc, 4 * CHUNK, CHUNK), F32)] + x_shape,
        scratch_shapes=[pltpu.VMEM((256, 64), F32)] + x_sems,
        compiler_params=_params(("arbitrary",)),
    )(qkv, proj, proj, dtb, alog, nw, *([] if exchange is None else exchange.arrays))


def _gdn_bwd(proj, qkv, states, inverses, dtb, alog, nw, dcat, dcol0, name, exchange=None):
    t = proj.shape[0]
    CHUNK = min(GDN_CHUNK, t)
    nc = t // CHUNK
    db = dcol0 // 256

    def body(q_ref, z_ref, ba_ref, st_ref, inv_ref, dtb_ref, al_ref, nw_ref, do_ref,
             dq_ref, dz_ref, dba_ref, gdtb_ref, gal_ref, gnw_ref, dstate):
        @pl.when(pl.program_id(0) == 0)
        def _():
            dstate[...] = jnp.zeros_like(dstate)
            gdtb_ref[...] = jnp.zeros_like(gdtb_ref)
            gal_ref[...] = jnp.zeros_like(gal_ref)
            gnw_ref[...] = jnp.zeros_like(gnw_ref)

        def chunk(*operands):
            return _gdn_chunk(*operands, known_inverses=inv_ref[0])[:2]

        _, vjp = jax.vjp(chunk, q_ref[...], z_ref[...], ba_ref[...], st_ref[0], dtb_ref[...], al_ref[...], nw_ref[...])
        dq, dz, dba, dst, gdtb, gal, gnw = vjp((do_ref[...], dstate[...]))
        dq_ref[...] = dq
        dz_ref[...] = dz.astype(dz_ref.dtype)
        dba_ref[...] = dba.astype(dba_ref.dtype)
        dstate[...] = dst
        gdtb_ref[...] += gdtb
        gal_ref[...] += gal
        gnw_ref[...] += gnw

    body, x_in, x_out, x_shape, x_sems = _hosted(exchange, 9, 6, nc, body)
    rev = lambda i: nc - 1 - i
    vec = pl.BlockSpec((1, LANES), lambda i: (0, 0))
    vec64 = pl.BlockSpec((1, 64), lambda i: (0, 0))
    per_chunk = pl.BlockSpec((1, 256, 64), lambda i: (rev(i), 0, 0))
    return pl.pallas_call(
        body, grid=(nc,), name=name,
        in_specs=[pl.BlockSpec((CHUNK, 768), lambda i: (rev(i), 0)),
                  pl.BlockSpec((CHUNK, 256), lambda i: (rev(i), PC_GZ // 256)),
                  pl.BlockSpec((CHUNK, 128), lambda i: (rev(i), PC_BA // 128)),
                  per_chunk, pl.BlockSpec((1, 4 * CHUNK, CHUNK), lambda i: (rev(i), 0, 0)), vec, vec, vec64,
                  pl.BlockSpec((CHUNK, 256), lambda i: (rev(i), db))] + x_in,
        out_specs=[pl.BlockSpec((CHUNK, 768), lambda i: (rev(i), 0)), pl.BlockSpec((CHUNK, 256), lambda i: (rev(i), 0)),
                   pl.BlockSpec((CHUNK, 128), lambda i: (rev(i), 0)), vec, vec, vec64] + x_out,
        out_shape=[S((t, 768), F32), S((t, 256), BF16), S((t, 128), BF16), S((1, LANES), F32), S((1, LANES), F32),
                   S((1, 64), F32)] + x_shape,
        scratch_shapes=[pltpu.VMEM((256, 64), F32)] + x_sems,
        compiler_params=_params(("arbitrary",)),
    )(qkv, proj, proj, states, inverses, dtb, alog, nw, dcat, *([] if exchange is None else exchange.arrays))


def _pad_cols(w):
    z = jnp.zeros((w.shape[0], 120), w.dtype)
    return jnp.concatenate([w[:, 2056:2824], w[:, 2824:3080], w[:, 1024:2048], w[:, 0:512], w[:, 512:1024],
                            w[:, 2048:2056], z, w[:, 3080:3088], z], axis=1)


def _unpad_cols(g):
    return jnp.concatenate([g[:, PC_ATT:PC_ATT + 512], g[:, PC_SZ:PC_SZ + 512], g[:, PC_XBC:PC_XBC + 1024],
                            g[:, PC_DT:PC_DT + 8], g[:, PC_GQKV:PC_GQKV + 768], g[:, PC_GZ:PC_GZ + 256],
                            g[:, PC_BA:PC_BA + 8]], axis=1)


def _vec128(v, at=0):
    return jnp.zeros((1, LANES), F32).at[0, at:at + v.shape[0]].set(v)


def _in_weight(gathered):
    return _pad_cols(jnp.concatenate([gathered[k].reshape(D_MODEL, -1) for k in range(N_CHIPS)], axis=1))


def _matmul_weights(w_in, gathered, l):
    cols = lambda name: jnp.concatenate([gathered[name][k, l] for k in range(N_CHIPS)], axis=1)
    rows = lambda name: jnp.concatenate([gathered[name][k, l] for k in range(N_CHIPS)], axis=0)
    w_out = rows("w_out")
    return dict(
        w_in=w_in,
        w_out=jnp.concatenate([w_out[256:768], w_out[0:256], w_out[768:1024]], axis=0),
        w_gu=_interleave_gu(cols("ffn_w_gate"), cols("ffn_w_up")),
        w_down=rows("ffn_w_down"))


def _small_operands(w, l):
    return dict(
        pre_mix=w["pre_mix_norm"][l][None], post_mix=w["post_mix_norm"][l][None],
        pre_ffn=w["pre_ffn_norm"][l][None], post_ffn=w["post_ffn_norm"][l][None],
        sinks=_vec128(w["attn_sinks"][l]),
        s_cw=w["ssd_conv_w"][l], s_cb=w["ssd_conv_b"][l][None],
        s_dtb=_vec128(w["ssd_dt_bias"][l]), s_alog=_vec128(w["ssd_A_log"][l]), s_d=_vec128(w["ssd_D"][l]),
        s_nw=w["ssd_norm_w"][l][None],
        g_cw=w["gdn_conv_w"][l], g_cb=jnp.zeros((1, 768), F32),
        g_dtb=_vec128(w["gdn_dt_bias"][l], 4), g_alog=_vec128(w["gdn_A_log"][l], 4), g_nw=w["gdn_norm_w"][l][None],
    )


RAW_GRADS = ("w_in_pad", "w_out_cat", "w_gu", "ffn_w_down")
DW_ROWS = 4096


def _local_step(x, target, lw, w_in0, later_weights, matmul_weights, reduce_early):
    saved = []
    xin = x
    h = _prenorm(x, lw[0]["pre_mix"], "prenorm0")
    for l in range(DEPTH):
        p = lw[l]
        proj = _mm_nn(h, w_in0 if l == 0 else p["w_in"], 512, PC_TOT, F32, f"inproj{l}")
        xbc = _conv_fwd(proj, PC_XBC, 1024, p["s_cw"], p["s_cb"], f"ssd_conv{l}")
        gqkv = _conv_fwd(proj, PC_GQKV, 768, p["g_cw"], p["g_cb"], f"gdn_conv{l}")
        att = _swa_fwd(proj, p["sinks"], f"swa{l}")
        ssd, s_states, *landed_s = _ssd_fwd(proj, xbc, p["s_dtb"], p["s_alog"], p["s_d"], p["s_nw"], f"ssd{l}",
                                            later_weights[0] if l == 0 else None)
        gdn, g_states, g_inv, *landed_g = _gdn_fwd(proj, gqkv, p["g_dtb"], p["g_alog"], p["g_nw"], f"gdn{l}",
                                                   later_weights[1] if l == 0 else None)
        if l == 0:
            for q, mw in zip(lw, matmul_weights(landed_s + landed_g)):
                q.update(mw)
        cat = jnp.concatenate([ssd, att, gdn], axis=1)
        mix = _mm_nn(cat, p["w_out"], 512, 1024, F32, f"outproj{l}")
        x1, h2 = _resid_norm(xin, mix, p["post_mix"], p["pre_ffn"], f"postmix{l}")
        gu, act = _ffn_up(h2, p["w_gu"], f"ffn_gu{l}")
        f = _mm_nn(act, p["w_down"], 512, 1024, F32, f"ffn_down{l}")
        saved.append(dict(xin=xin, h=h, proj=proj, xbc=xbc, gqkv=gqkv, s_states=s_states, g_states=g_states, g_inv=g_inv,
                          cat=cat, mix=mix, x1=x1, h2=h2, gu=gu, act=act, f=f))
        if l + 1 < DEPTH:
            xin, h = _resid_norm(x1, f, p["post_ffn"], lw[l + 1]["pre_mix"], f"postffn{l}")

    g = {k: [None] * DEPTH for k in SMALL + CONV + RAW_GRADS}
    last = saved[-1]
    d_x2, d_f, loss_part, g["post_ffn_norm"][DEPTH - 1] = _resid_loss(
        last["x1"], last["f"], lw[-1]["post_ffn"], target, "loss")
    carried, early = None, []
    for l in reversed(range(DEPTH)):
        p, s = lw[l], saved[l]
        d_act = _mm_nt(d_f, p["w_down"], 512, FF, F32, f"d_act{l}")
        g["ffn_w_down"][l] = _mm_tn(s["act"], d_f, FF_HALF, 512, DW_ROWS, f"dw_down{l}")
        d_gu = _swiglu_bwd(s["gu"], d_act, f"d_swiglu{l}")
        d_h2 = _mm_nt(d_gu, p["w_gu"], 256, 1024, F32, f"d_h2{l}")
        g["w_gu"][l] = _mm_tn(s["h2"], d_gu, 512, FF_HALF, DW_ROWS, f"dw_gu{l}")
        d_x1, d_mix, g["pre_ffn_norm"][l], g["post_mix_norm"][l] = _resid_norm_bwd(
            s["x1"], s["mix"], d_x2, d_h2, p["post_mix"], p["pre_ffn"], f"d_postmix{l}")
        d_cat = _mm_nt(d_mix, p["w_out"], 512, 1024, F32, f"d_cat{l}")
        g["w_out_cat"][l] = _mm_tn(s["cat"], d_mix, 512, 1024, DW_ROWS, f"dw_out{l}")
        d_q, d_k, d_v, g_sinks = _swa_bwd(s["proj"], p["sinks"], d_cat, 512, f"d_swa{l}")
        d_sz, d_xbc, d_dt, g_dtb, g_alog, g_d, g["ssd_norm_w"][l] = _ssd_bwd(
            s["proj"], s["xbc"], s["s_states"], p["s_dtb"], p["s_alog"], p["s_d"], p["s_nw"], d_cat, 0, f"d_ssd{l}")
        if l == 0:
            carried = reduce_early(g)
        d_gq, d_gz, d_ba, gg_dtb, gg_alog, g["gdn_norm_w"][l], *landed = _gdn_bwd(
            s["proj"], s["gqkv"], s["g_states"], s["g_inv"], p["g_dtb"], p["g_alog"], p["g_nw"], d_cat, 768, f"d_gdn{l}",
            carried if l == 0 else None)
        if l == 0:
            early = landed
        d_xbc_raw, g["ssd_conv_w"][l], g["ssd_conv_b"][l] = _conv_bwd(
            s["proj"], PC_XBC, 1024, p["s_cw"], p["s_cb"], d_xbc, f"d_ssd_conv{l}")
        d_gq_raw, g["gdn_conv_w"][l], _ = _conv_bwd(s["proj"], PC_GQKV, 768, p["g_cw"], p["g_cb"], d_gq, f"d_gdn_conv{l}")
        d_proj = jnp.concatenate([d_gq_raw, d_gz, d_xbc_raw, d_q, d_k, d_v, d_sz, d_dt, d_ba], axis=1)
        d_h = _mm_nt(d_proj, p["w_in"], 512, 1024, F32, f"d_h{l}")
        g["w_in_pad"][l] = _mm_tn(s["h"], d_proj, 512, PC_TOT // 2, DW_ROWS, f"dw_in{l}")
        g["attn_sinks"][l] = g_sinks[0, :4]
        g["ssd_dt_bias"][l], g["ssd_A_log"][l], g["ssd_D"][l] = g_dtb[0, :8], g_alog[0, :8], g_d[0, :8]
        g["gdn_dt_bias"][l], g["gdn_A_log"][l] = gg_dtb[0, 4:8], gg_alog[0, 4:8]
        if l > 0:
            sp = saved[l - 1]
            d_x2, d_f, g["pre_mix_norm"][l], g["post_ffn_norm"][l - 1] = _resid_norm_bwd(
                s["xin"], sp["f"], d_x1, d_h, lw[l - 1]["post_ffn"], p["pre_mix"], f"d_postffn{l - 1}")
        else:
            grad_x, g["pre_mix_norm"][0] = _prenorm_bwd(s["xin"], d_x1, d_h, p["pre_mix"], "d_prenorm0")

    small = {k: jnp.stack([a.reshape(-1) for a in g[k]], axis=0) for k in SMALL + CONV}
    return loss_part, grad_x, small, {k: g[k] for k in RAW_GRADS}, early


BIG = (("w_in", 2), ("w_out", 1), ("ffn_w_gate", 2), ("ffn_w_up", 2), ("ffn_w_down", 1))
CONV = ("ssd_conv_w", "gdn_conv_w")
SMALL = ("pre_mix_norm", "post_mix_norm", "pre_ffn_norm", "post_ffn_norm", "attn_sinks", "ssd_conv_b", "ssd_dt_bias",
         "ssd_A_log", "ssd_D", "ssd_norm_w", "gdn_dt_bias", "gdn_A_log", "gdn_norm_w")


def _row_tile(rows, cap):
    best = rows
    for t in range(8, min(cap, rows) + 1, 8):
        if rows % t == 0:
            best = t
    return best


SMALL_UNIT = 8 * LANES


def _pack_small(vals):
    rows = []
    for a in vals:
        f = a.reshape(-1)
        pad = -f.shape[0] % SMALL_UNIT
        rows.append(jnp.concatenate([f, jnp.zeros((pad,), F32)]).reshape(-1, LANES))
    return jnp.concatenate(rows, axis=0)


def _unpack_small(mat, shapes):
    out, r = [], 0
    for shp in shapes:
        n = math.prod(shp)
        nr = -(-n // SMALL_UNIT) * 8
        out.append(mat[r:r + nr].reshape(-1)[:n].reshape(shp))
        r += nr
    return out


def _place():
    x, y, c = lax.axis_index("x"), lax.axis_index("y"), lax.axis_index("c")
    chips = [(1 - x, y), (x, 1 - y), (1 - x, 1 - y)]
    return x, y, c, chips


ANY = pl.BlockSpec(memory_space=pl.ANY)


def _remote(src, dst, sems, k, to):
    send_sems, recv_sems = sems
    return pltpu.make_async_remote_copy(src_ref=src, dst_ref=dst, send_sem=send_sems.at[k], recv_sem=recv_sems.at[k],
                                        device_id=to, device_id_type=MESH)


def _sem_pairs(n):
    return [pltpu.SemaphoreType.DMA((n,)), pltpu.SemaphoreType.DMA((n,))]


def _run_exchange(exchange, name):
    k = len(exchange.arrays)

    def body(*refs):
        ins, outs, sems = refs[:k], refs[k:-2], refs[-2:]
        exchange.start(ins, outs, sems)
        exchange.finish(ins, outs, sems)

    return pl.pallas_call(
        body, name=name, in_specs=[ANY] * k, out_specs=[ANY] * len(exchange.out_shape),
        out_shape=list(exchange.out_shape), scratch_shapes=_sem_pairs(exchange.n_sems),
    )(*exchange.arrays)


def _gather_exchange(shards):
    n = len(shards)

    def sends(s_refs, g_refs, sems):
        x, y, c, chips = _place()
        return [_remote(s_refs[i].at[c], g_refs[i].at[2 * x + y, c], sems, 6 * i + j, (px, py, c))
                for i in range(n) for j, (px, py) in enumerate(chips)]

    def start(s_refs, g_refs, sems):
        for cp in sends(s_refs, g_refs, sems):
            cp.start()

    def finish(s_refs, g_refs, sems):
        x, y, c, chips = _place()
        sib = (x, y, 1 - c)
        passed = []
        for j, (px, py) in enumerate(chips):
            for i in range(n):
                landed = g_refs[i].at[2 * px + py, c]
                _remote(landed, landed, sems, 6 * i + j, (px, py, c)).wait_recv()
                fw = _remote(landed, landed, sems, 6 * i + 3 + j, sib)
                fw.start()
                passed.append(fw)
        for j, (px, py) in enumerate(chips):
            for i in range(n):
                landed = g_refs[i].at[2 * px + py, 1 - c]
                _remote(landed, landed, sems, 6 * i + 3 + j, sib).wait_recv()
        for cp in sends(s_refs, g_refs, sems) + passed:
            cp.wait_send()

    return _Exchange(shards, [S((N_CHIPS,) + a.shape, a.dtype) for a in shards], 6 * n, start, finish)


def _exchange_halves(ds, name):
    n = len(ds)

    def body(*refs):
        d_refs, t_refs, sems = refs[:n], refs[n:2 * n], refs[2 * n:]
        x, y, c, _ = _place()
        cps = [_remote(d_refs[i].at[:, 1 - c], t_refs[i], sems, i, (x, y, 1 - c)) for i in range(n)]
        for cp in cps:
            cp.start()
        for cp in cps:
            cp.wait()

    return pl.pallas_call(
        body, name=name, in_specs=[ANY] * n, out_specs=[ANY] * n,
        out_shape=[S((N_CHIPS,) + a.shape[2:], a.dtype) for a in ds], scratch_shapes=_sem_pairs(n),
    )(*ds)


def _scatter_exchange(ps):
    n = len(ps)

    def copies(p_refs, u_refs, sems):
        x, y, c, chips = _place()
        return [_remote(p_refs[i].at[2 * px + py], u_refs[i].at[j], sems, 3 * i + j, (px, py, c))
                for j, (px, py) in enumerate(chips) for i in range(n)]

    def start(p_refs, u_refs, sems):
        for cp in copies(p_refs, u_refs, sems):
            cp.start()

    def finish(p_refs, u_refs, sems):
        for cp in copies(p_refs, u_refs, sems):
            cp.wait()

    return _Exchange(ps, [S((3,) + a.shape[1:], a.dtype) for a in ps], 3 * n, start, finish)


def _join_halves(qs, name):
    n = len(qs)

    def body(*refs):
        o_refs, sems = refs[n:2 * n], refs[2 * n:]
        x, y, c, _ = _place()
        cps = [_remote(o_refs[i].at[c], o_refs[i].at[c], sems, i, (x, y, 1 - c)) for i in range(n)]
        for cp in cps:
            cp.start()
        for i in range(n):
            other = o_refs[i].at[1 - c]
            _remote(other, other, sems, i, (x, y, 1 - c)).wait_recv()
        for cp in cps:
            cp.wait_send()

    return pl.pallas_call(
        body, name=name, in_specs=[ANY] * n, out_specs=[ANY] * n,
        out_shape=[S(a.shape, a.dtype) for a in qs], input_output_aliases={i: i for i in range(n)},
        scratch_shapes=_sem_pairs(n),
    )(*qs)


def _gather_small(v, name):
    def body(v_ref, o_ref, send_sems, recv_sems, local_sem):
        x, y, c, _ = _place()
        me = 4 * x + 2 * y + c
        mine = pltpu.make_async_copy(v_ref, o_ref.at[me], local_sem)
        mine.start()
        cps = []
        for k in range(1, N_DEV):
            fx, fy, fc = (k >> 2) & 1, (k >> 1) & 1, k & 1
            peer = (x ^ fx, y ^ fy, c ^ fc)
            cps.append(pltpu.make_async_remote_copy(
                src_ref=v_ref, dst_ref=o_ref.at[me], send_sem=send_sems.at[k - 1], recv_sem=recv_sems.at[k - 1],
                device_id=peer, device_id_type=MESH))
        for cp in cps:
            cp.start()
        for k in range(1, N_DEV):
            fx, fy, fc = (k >> 2) & 1, (k >> 1) & 1, k & 1
            dst = o_ref.at[4 * (x ^ fx) + 2 * (y ^ fy) + (c ^ fc)]
            pltpu.make_async_remote_copy(src_ref=dst, dst_ref=dst, send_sem=send_sems.at[k - 1],
                                         recv_sem=recv_sems.at[k - 1], device_id=(x, y, c),
                                         device_id_type=MESH).wait_recv()
        for cp in cps:
            cp.wait_send()
        mine.wait()

    return pl.pallas_call(
        body, name=name, in_specs=[ANY], out_specs=ANY, out_shape=S((N_DEV,) + v.shape, F32),
        scratch_shapes=[pltpu.SemaphoreType.DMA((N_DEV - 1,)), pltpu.SemaphoreType.DMA((N_DEV - 1,)),
                        pltpu.SemaphoreType.DMA],
    )(v)


def _sum_leading(a, name):
    n, rows, cols = a.shape
    tm = _row_tile(rows, 640)

    def body(a_ref, o_ref):
        acc = a_ref[0]
        for k in range(1, n):
            acc = acc + a_ref[k]
        o_ref[...] = acc

    return pl.pallas_call(
        body, grid=(rows // tm,), name=name, in_specs=[pl.BlockSpec((n, tm, cols), lambda i: (0, i, 0))],
        out_specs=pl.BlockSpec((tm, cols), lambda i: (i, 0)), out_shape=S((rows, cols), F32),
        compiler_params=_params(("arbitrary",)),
    )(a)


def _add_sibling(place, a, b, name):
    n, hr, cols = b.shape
    tm = _row_tile(hr, 512)

    def body(place_ref, a_ref, b_ref, o_ref, o16_ref):
        tot = a_ref[0] + b_ref[...]
        o_ref[...] = tot
        o16_ref[...] = tot.astype(BF16)

    spec = pl.BlockSpec((1, tm, cols), lambda k, i, pr: (k, i, 0))
    return pl.pallas_call(
        body, name=name, out_shape=[S((n, hr, cols), F32), S((n, hr, cols), BF16)],
        grid_spec=pltpu.PrefetchScalarGridSpec(
            num_scalar_prefetch=1, grid=(n, hr // tm),
            in_specs=[pl.BlockSpec((1, 1, tm, cols), lambda k, i, pr: (k, pr[0], i, 0)), spec], out_specs=[spec, spec]),
        compiler_params=_params(("arbitrary", "arbitrary")),
    )(place, a, b)


def _add_chips(place, sums, others, name):
    _, hr, cols = sums.shape
    tm = _row_tile(hr, 512)

    def body(place_ref, m_ref, o_ref, out_ref):
        acc = m_ref[0]
        for k in range(others.shape[0]):
            acc = acc + o_ref[k].astype(F32)
        out_ref[0] = acc

    return pl.pallas_call(
        body, name=name, out_shape=S((2, hr, cols), F32),
        grid_spec=pltpu.PrefetchScalarGridSpec(
            num_scalar_prefetch=1, grid=(hr // tm,),
            in_specs=[pl.BlockSpec((1, tm, cols), lambda i, pr: (pr[1], i, 0)),
                      pl.BlockSpec((others.shape[0], tm, cols), lambda i, pr: (0, i, 0))],
            out_specs=pl.BlockSpec((1, tm, cols), lambda i, pr: (pr[0], i, 0))),
        compiler_params=_params(("arbitrary",)),
    )(place, sums, others)


def _adamw(wt, g, m, v, name):
    shape = wt.shape
    cols = shape[-1]
    rows = math.prod(shape[:-1])
    tm = rows
    for cand in (512, 256, 128, 64, 32, 16, 8):
        if rows % cand == 0:
            tm = cand
            break
    c1 = 1.0 - ADAM_B1 ** ADAM_STEP
    c2 = 1.0 - ADAM_B2 ** ADAM_STEP

    def body(w_ref, g_ref, m_ref, v_ref, d_ref, nm_ref, nv_ref):
        gv = g_ref[...]
        nm = ADAM_B1 * m_ref[...] + (1.0 - ADAM_B1) * gv
        nv = ADAM_B2 * v_ref[...] + (1.0 - ADAM_B2) * (gv * gv)
        d_ref[...] = -ADAM_LR * ((nm / c1) / (jnp.sqrt(nv / c2) + ADAM_EPS) + ADAM_WD * w_ref[...])
        nm_ref[...] = nm
        nv_ref[...] = nv

    spec = pl.BlockSpec((tm, cols), lambda i: (i, 0))
    outs = pl.pallas_call(
        body, grid=(rows // tm,), name=name, in_specs=[spec] * 4, out_specs=[spec] * 3,
        out_shape=[S((rows, cols), F32)] * 3, compiler_params=_params(("arbitrary",)),
    )(*[a.reshape(rows, cols) for a in (wt, g, m, v)])
    return [o.reshape(shape) for o in outs]


WEIGHTS = ('pre_mix_norm', 'post_mix_norm', 'pre_ffn_norm', 'post_ffn_norm', 'w_in', 'w_out', 'attn_sinks', 'ssd_conv_w',
           'ssd_conv_b', 'ssd_dt_bias', 'ssd_A_log', 'ssd_D', 'ssd_norm_w', 'gdn_conv_w', 'gdn_dt_bias', 'gdn_A_log',
           'gdn_norm_w', 'ffn_w_gate', 'ffn_w_up', 'ffn_w_down')


def _chip_piece(i, raw, shape):
    if i == 0:
        g, axis = _unpad_cols(raw["w_in_pad"]), 1
    elif i == 1:
        g = raw["w_out_cat"]
        g, axis = jnp.concatenate([g[512:768], g[0:512], g[768:1024]], axis=0), 0
    elif i in (2, 3):
        g, axis = _split_gu(raw["w_gu"])[i - 2], 1
    else:
        g, axis = raw["ffn_w_down"], 0
    if axis == 1:
        width = shape[2]
        g = jnp.stack([g[:, k * width:(k + 1) * width] for k in range(N_CHIPS)])
    return g.reshape(N_CHIPS, 2, shape[1] // 2, shape[2])


def _step(x, target, wts, ms, vs):
    chip = 2 * lax.axis_index("x") + lax.axis_index("y")
    place = jnp.stack([lax.axis_index("c"), chip]).astype(jnp.int32)
    big_names = [k for k, _ in BIG]
    big_shapes = [wts[k].shape for k in big_names]
    own = lambda gathered, shard: lax.dynamic_update_index_in_dim(gathered, shard, chip, 0)
    halves = lambda a: a.reshape((2, a.shape[0] // 2) + a.shape[1:])

    w16 = {k: wts[k].astype(BF16) for k in big_names}
    first = halves(w16["w_in"][0])
    w_in0 = _in_weight(own(_run_exchange(_gather_exchange([first]), "gather_w_in0")[0], first))
    later = [halves(w16["w_in"][1])] + [w16[k] for k in big_names[1:]]
    with_ssd = 2

    def matmul_weights(landed):
        landed = [own(g, s) for g, s in zip(landed, later)]
        gathered = dict(zip(big_names[1:], landed[1:]))
        return [_matmul_weights(w_in0 if l == 0 else _in_weight(landed[0]), gathered, l) for l in range(DEPTH)]

    conv = _gather_small(_pack_small([wts[k] for k in CONV]), "gather_conv_weights")
    conv = [_unpack_small(conv[2 * k], [wts[n].shape for n in CONV]) for k in range(N_CHIPS)]
    w_all = dict(wts)
    for i, n in enumerate(CONV):
        w_all[n] = jnp.concatenate([conv[k][i] for k in range(N_CHIPS)], axis=2)

    early_keys = [(DEPTH - 1, 0)] + [(l, i) for l in reversed(range(DEPTH)) for i in range(1, len(BIG))]
    late_keys = [(l, 0) for l in range(DEPTH - 1)]

    def reduce_begin(tag, keys, g):
        pieces = [_chip_piece(i, {k: g[k][l] for k in RAW_GRADS}, big_shapes[i]) for l, i in keys]
        sib = _exchange_halves(pieces, f"exchange_halves_{tag}")
        return [_add_sibling(place, p, t, f"add_sibling_{tag}{n}") for n, (p, t) in enumerate(zip(pieces, sib))]

    def reduce_end(tag, keys, sums, others):
        parts = [_add_chips(place, s32, o, f"add_chips_{tag}{n}") for n, ((s32, _), o) in enumerate(zip(sums, others))]
        joined = _join_halves(parts, f"join_halves_{tag}")
        return {key: q.reshape(big_shapes[key[1]][1:]) for key, q in zip(keys, joined)}

    early_sums = []

    def reduce_early(g):
        early_sums.extend(reduce_begin("early", early_keys, g))
        return _scatter_exchange([s16 for _, s16 in early_sums])

    loss_part, grad_x, small_g, raw, early = _local_step(
        x[0], target[0], [_small_operands(w_all, l) for l in range(DEPTH)], w_in0,
        [_gather_exchange(later[:with_ssd]), _gather_exchange(later[with_ssd:])], matmul_weights, reduce_early)

    reduced = reduce_end("early", early_keys, early_sums, early)
    late_sums = reduce_begin("late", late_keys, raw)
    others = _run_exchange(_scatter_exchange([s16 for _, s16 in late_sums]), "scatter_chips_late")
    reduced.update(reduce_end("late", late_keys, late_sums, others))
    g_all = {k: jnp.stack([reduced[(l, i)] for l in range(DEPTH)]) for i, k in enumerate(big_names)}

    names = SMALL + CONV
    packed = _pack_small([small_g[k] for k in names] + [loss_part])
    small_sum = _sum_leading(_gather_small(packed, "gather_small_grads"), "add_small")
    vals = _unpack_small(small_sum, [small_g[k].shape for k in names] + [(1, LANES)])
    loss = vals[-1][0, 0]
    for k, v in zip(names, vals[:-1]):
        if k in CONV:
            width = wts[k].shape[2]
            v = lax.dynamic_slice_in_dim(v.reshape(DEPTH, 4, -1), chip * width, width, axis=2)
        g_all[k] = v.reshape(wts[k].shape)

    shapes = [wts[k].shape for k in names]
    d_s, m_s, v_s = _adamw(_pack_small([wts[k] for k in names]), _pack_small([g_all[k] for k in names]),
                           _pack_small([ms[k] for k in names]), _pack_small([vs[k] for k in names]), "adamw_small")
    upd = dict(zip(names, zip(_unpack_small(d_s, shapes), _unpack_small(m_s, shapes), _unpack_small(v_s, shapes))))
    for k in big_names:
        upd[k] = _adamw(wts[k], g_all[k], ms[k], vs[k], f"adamw_{k}")
    return (loss, grad_x[None], *[g_all[k] for k in WEIGHTS], *[upd[k][0] for k in WEIGHTS],
            *[upd[k][1] for k in WEIGHTS], *[upd[k][2] for k in WEIGHTS])


def kernel(x, pre_mix_norm, post_mix_norm, pre_ffn_norm, post_ffn_norm, w_in, w_out, attn_sinks, ssd_conv_w, ssd_conv_b, ssd_dt_bias, ssd_A_log, ssd_D, ssd_norm_w, gdn_conv_w, gdn_dt_bias, gdn_A_log, gdn_norm_w, ffn_w_gate, ffn_w_up, ffn_w_down, loss_target, m_pre_mix_norm, m_post_mix_norm, m_pre_ffn_norm, m_post_ffn_norm, m_w_in, m_w_out, m_attn_sinks, m_ssd_conv_w, m_ssd_conv_b, m_ssd_dt_bias, m_ssd_A_log, m_ssd_D, m_ssd_norm_w, m_gdn_conv_w, m_gdn_dt_bias, m_gdn_A_log, m_gdn_norm_w, m_ffn_w_gate, m_ffn_w_up, m_ffn_w_down, v_pre_mix_norm, v_post_mix_norm, v_pre_ffn_norm, v_post_ffn_norm, v_w_in, v_w_out, v_attn_sinks, v_ssd_conv_w, v_ssd_conv_b, v_ssd_dt_bias, v_ssd_A_log, v_ssd_D, v_ssd_norm_w, v_gdn_conv_w, v_gdn_dt_bias, v_gdn_A_log, v_gdn_norm_w, v_ffn_w_gate, v_ffn_w_up, v_ffn_w_down):
    wts = dict(zip(WEIGHTS, (pre_mix_norm, post_mix_norm, pre_ffn_norm, post_ffn_norm, w_in, w_out, attn_sinks, ssd_conv_w, ssd_conv_b, ssd_dt_bias, ssd_A_log, ssd_D, ssd_norm_w, gdn_conv_w, gdn_dt_bias, gdn_A_log, gdn_norm_w, ffn_w_gate, ffn_w_up, ffn_w_down)))
    ms = dict(zip(WEIGHTS, (m_pre_mix_norm, m_post_mix_norm, m_pre_ffn_norm, m_post_ffn_norm, m_w_in, m_w_out, m_attn_sinks, m_ssd_conv_w, m_ssd_conv_b, m_ssd_dt_bias, m_ssd_A_log, m_ssd_D, m_ssd_norm_w, m_gdn_conv_w, m_gdn_dt_bias, m_gdn_A_log, m_gdn_norm_w, m_ffn_w_gate, m_ffn_w_up, m_ffn_w_down)))
    vs = dict(zip(WEIGHTS, (v_pre_mix_norm, v_post_mix_norm, v_pre_ffn_norm, v_post_ffn_norm, v_w_in, v_w_out, v_attn_sinks, v_ssd_conv_w, v_ssd_conv_b, v_ssd_dt_bias, v_ssd_A_log, v_ssd_D, v_ssd_norm_w, v_gdn_conv_w, v_gdn_dt_bias, v_gdn_A_log, v_gdn_norm_w, v_ffn_w_gate, v_ffn_w_up, v_ffn_w_down)))
    return _step(x, loss_target, wts, ms, vs)
```

```python
import functools
import math

import jax
import jax.numpy as jnp
from jax import lax
from jax.experimental import pallas as pl
from jax.experimental.pallas import tpu as pltpu

F32, BF16 = jnp.float32, jnp.bfloat16
HI = lax.Precision.HIGHEST
MESH = pl.DeviceIdType.MESH
S = jax.ShapeDtypeStruct

D_MODEL = 1024
DEPTH = 2
CHUNK = 64
SSD_CHUNK = 256
GDN_CHUNK = 128
EPS = 1e-6
FF = 2816
N_CHIPS = 4
N_DEV = 8
LANES = 128

VMEM_LIMIT_BYTES = 56 * 1024 * 1024

PC_GQKV, PC_GZ, PC_XBC, PC_ATT, PC_SZ, PC_DT, PC_BA, PC_TOT = 0, 768, 1024, 2048, 2560, 3072, 3200, 3328

ADAM_LR, ADAM_B1, ADAM_B2, ADAM_EPS, ADAM_WD, ADAM_STEP = 0.001, 0.9, 0.999, 1e-08, 0.01, 10

ALIBI_SLOPES = tuple(2.0 ** (-8.0 * (h + 1) / 4) for h in range(4))


def _params(sem=None, **kw):
    if sem is not None:
        kw["dimension_semantics"] = sem
    return pltpu.CompilerParams(vmem_limit_bytes=VMEM_LIMIT_BYTES, **kw)


def _dot(a, b, prec=None):
    return jnp.dot(a, b, precision=prec, preferred_element_type=F32)


def _dot_nt(a, b, prec=None):
    return lax.dot_general(a, b, (((1,), (1,)), ((), ())), precision=prec, preferred_element_type=F32)


def _dot_tn(a, b, prec=None):
    return lax.dot_general(a, b, (((0,), (0,)), ((), ())), precision=prec, preferred_element_type=F32)


def _iota2(n, m):
    return lax.broadcasted_iota(jnp.int32, (n, m), 0), lax.broadcasted_iota(jnp.int32, (n, m), 1)


def _pick_col(arr, idx):
    ci = lax.broadcasted_iota(jnp.int32, arr.shape, 1)
    return jnp.sum(jnp.where(ci == idx, arr, 0.0), axis=1, keepdims=True)


def _pick_row(arr, idx):
    ri = lax.broadcasted_iota(jnp.int32, arr.shape, 0)
    return jnp.sum(jnp.where(ri == idx, arr, 0.0), axis=0, keepdims=True)


def _col_to_row(col, eye):
    return jnp.sum(eye * col, axis=0, keepdims=True)


def _rms(x, w):
    return x * lax.rsqrt(jnp.mean(x * x, axis=-1, keepdims=True) + EPS) * w


def _mm_nn(a, b, tm, tn, out_dtype, name, exchange=None):
    m, k = a.shape
    n = b.shape[1]
    tm, tn = min(tm, m), min(tn, n)
    grid = (n // tn, m // tm)

    def body(a_ref, b_ref, o_ref):
        o_ref[...] = _dot(a_ref[...], b_ref[...]).astype(o_ref.dtype)

    body, x_in, x_out, x_shape, x_sems = _hosted(exchange, 2, 1, grid, body)
    outs = pl.pallas_call(
        body, grid=grid, name=name,
        in_specs=[pl.BlockSpec((tm, k), lambda j, i: (i, 0)), pl.BlockSpec((k, tn), lambda j, i: (0, j))] + x_in,
        out_specs=[pl.BlockSpec((tm, tn), lambda j, i: (i, j))] + x_out,
        out_shape=[S((m, n), out_dtype)] + x_shape, scratch_shapes=x_sems,
        compiler_params=_params(("arbitrary", "arbitrary")),
    )(a, b, *([] if exchange is None else exchange.arrays))
    return outs[0] if exchange is None else outs


def _mm_nt(a, b, tm, tn, out_dtype, name):
    m, k = a.shape
    n = b.shape[0]
    tm, tn = min(tm, m), min(tn, n)

    def body(a_ref, b_ref, o_ref):
        o_ref[...] = _dot_nt(a_ref[...], b_ref[...]).astype(o_ref.dtype)

    return pl.pallas_call(
        body, grid=(n // tn, m // tm), name=name,
        in_specs=[pl.BlockSpec((tm, k), lambda j, i: (i, 0)), pl.BlockSpec((tn, k), lambda j, i: (j, 0))],
        out_specs=pl.BlockSpec((tm, tn), lambda j, i: (i, j)),
        out_shape=S((m, n), out_dtype), compiler_params=_params(("arbitrary", "arbitrary")),
    )(a, b)


def _mm_tn(a, b, tm, tn, tk, name):
    t, m = a.shape
    n = b.shape[1]
    tm, tn, tk = min(tm, m), min(tn, n), min(tk, t)

    def body(a_ref, b_ref, o_ref):
        part = _dot_tn(a_ref[...], b_ref[...])

        @pl.when(pl.program_id(2) == 0)
        def _():
            o_ref[...] = part

        @pl.when(pl.program_id(2) > 0)
        def _():
            o_ref[...] += part

    return pl.pallas_call(
        body, grid=(m // tm, n // tn, t // tk), name=name,
        in_specs=[pl.BlockSpec((tk, tm), lambda i, j, k: (k, i)), pl.BlockSpec((tk, tn), lambda i, j, k: (k, j))],
        out_specs=pl.BlockSpec((tm, tn), lambda i, j, k: (i, j)),
        out_shape=S((m, n), F32), compiler_params=_params(("arbitrary", "arbitrary", "arbitrary")),
    )(a, b)


def _rowcall(fn, rows, params, row_outs, acc_outs, name, tm=512):
    t = rows[0].shape[0]
    tm = min(tm, t)
    n_in = len(rows) + len(params)
    n_ro = len(row_outs)

    def body(*refs):
        ro, ao = fn(*[r[...] for r in refs[:n_in]])
        for ref, v in zip(refs[n_in:n_in + n_ro], ro):
            ref[...] = v.astype(ref.dtype)
        acc_refs = refs[n_in + n_ro:]
        if acc_refs:
            @pl.when(pl.program_id(0) == 0)
            def _():
                for ref, v in zip(acc_refs, ao):
                    ref[...] = v

            @pl.when(pl.program_id(0) > 0)
            def _():
                for ref, v in zip(acc_refs, ao):
                    ref[...] += v

    in_specs = [pl.BlockSpec((tm, r.shape[1]), lambda i: (i, 0)) for r in rows]
    in_specs += [pl.BlockSpec(p.shape, lambda i: (0, 0)) for p in params]
    out_specs = [pl.BlockSpec((tm, c), lambda i: (i, 0)) for c, _ in row_outs]
    out_specs += [pl.BlockSpec(shape, lambda i: (0, 0)) for shape in acc_outs]
    out_shape = [S((t, c), dt) for c, dt in row_outs] + [S(shape, F32) for shape in acc_outs]
    return pl.pallas_call(
        body, grid=(t // tm,), name=name, in_specs=in_specs, out_specs=out_specs, out_shape=out_shape,
        compiler_params=_params(("arbitrary",)),
    )(*rows, *params)


def _prenorm(x, w, name):
    def fn(x, w):
        return (_rms(x, w),), ()
    return _rowcall(fn, [x], [w], [(D_MODEL, BF16)], [], name)[0]


def _resid_norm(xin, m, w_post, w_next, name):
    def fn(xin, m, w_post, w_next):
        xo = xin + _rms(m, w_post)
        return (xo, _rms(xo, w_next)), ()
    return _rowcall(fn, [xin, m], [w_post, w_next], [(D_MODEL, F32), (D_MODEL, BF16)], [], name)


def _resid_loss(xin, m, w_post, target, name):
    def fn(xin, m, target, w_post):
        r, vjp = jax.vjp(_rms, m, w_post)
        err = xin + r - target
        dy = err * (1.0 / D_MODEL)
        dm, dw = vjp(dy)
        tot = jnp.sum(jnp.sum(err * err, axis=1, keepdims=True), axis=0, keepdims=True) * (0.5 / D_MODEL)
        lane = lax.broadcasted_iota(jnp.int32, (1, LANES), 1)
        return (dy, dm), (jnp.where(lane == 0, tot, 0.0), dw)
    return _rowcall(fn, [xin, m, target], [w_post], [(D_MODEL, F32), (D_MODEL, BF16)],
                    [(1, LANES), (1, D_MODEL)], name)


def _resid_norm_bwd(x_out, m, d_direct, dh, w_post, w_next, name):
    def fn(x_out, m, d_direct, dh, w_post, w_next):
        _, vjp_n = jax.vjp(_rms, x_out, w_next)
        dx, dwn = vjp_n(dh)
        d_total = d_direct + dx
        _, vjp_p = jax.vjp(_rms, m, w_post)
        dm, dwp = vjp_p(d_total)
        return (d_total, dm), (dwn, dwp)
    return _rowcall(fn, [x_out, m, d_direct, dh], [w_post, w_next], [(D_MODEL, F32), (D_MODEL, BF16)],
                    [(1, D_MODEL), (1, D_MODEL)], name)


def _prenorm_bwd(x, d_direct, dh, w, name):
    def fn(x, d_direct, dh, w):
        _, vjp = jax.vjp(_rms, x, w)
        dx, dw = vjp(dh)
        return (d_direct + dx,), (dw,)
    return _rowcall(fn, [x, d_direct, dh], [w], [(D_MODEL, F32)], [(1, D_MODEL)], name)


FF_HALF = FF // 2


def _interleave_gu(gate, up):
    return jnp.concatenate([gate[:, :FF_HALF], up[:, :FF_HALF], gate[:, FF_HALF:], up[:, FF_HALF:]], axis=1)


def _split_gu(gu):
    return (jnp.concatenate([gu[:, :FF_HALF], gu[:, FF:FF + FF_HALF]], axis=1),
            jnp.concatenate([gu[:, FF_HALF:FF], gu[:, FF + FF_HALF:]], axis=1))


def _swiglu_pair(gu):
    n = gu.shape[1] // 2
    return jax.nn.silu(gu[:, :n]) * gu[:, n:]


def _ffn_up(h2, w_gu, name, tm=512, exchange=None):
    t, k = h2.shape
    tm = min(tm, t)
    grid = (2, t // tm)

    def body(a_ref, b_ref, gu_ref, act_ref):
        gu = _dot(a_ref[...], b_ref[...])
        gu_ref[...] = gu.astype(gu_ref.dtype)
        act_ref[...] = _swiglu_pair(gu).astype(act_ref.dtype)

    body, x_in, x_out, x_shape, x_sems = _hosted(exchange, 2, 2, grid, body)
    return pl.pallas_call(
        body, grid=grid, name=name,
        in_specs=[pl.BlockSpec((tm, k), lambda j, i: (i, 0)), pl.BlockSpec((k, FF), lambda j, i: (0, j))] + x_in,
        out_specs=[pl.BlockSpec((tm, FF), lambda j, i: (i, j)), pl.BlockSpec((tm, FF_HALF), lambda j, i: (i, j))] + x_out,
        out_shape=[S((t, 2 * FF), BF16), S((t, FF), BF16)] + x_shape, scratch_shapes=x_sems,
        compiler_params=_params(("arbitrary", "arbitrary")),
    )(h2, w_gu, *([] if exchange is None else exchange.arrays))


def _ffn_down_bwd(d_f, w_down, gu, name, tm=512):
    t, k = d_f.shape
    tm = min(tm, t)

    def body(a_ref, b_ref, gu_ref, o_ref):
        d_act = _dot_nt(a_ref[...], b_ref[...])
        _, vjp = jax.vjp(_swiglu_pair, gu_ref[...].astype(F32))
        o_ref[...] = vjp(d_act)[0].astype(o_ref.dtype)

    return pl.pallas_call(
        body, grid=(2, t // tm), name=name,
        in_specs=[pl.BlockSpec((tm, k), lambda j, i: (i, 0)), pl.BlockSpec((FF_HALF, k), lambda j, i: (j, 0)),
                  pl.BlockSpec((tm, FF), lambda j, i: (i, j))],
        out_specs=pl.BlockSpec((tm, FF), lambda j, i: (i, j)),
        out_shape=S((t, 2 * FF), BF16), compiler_params=_params(("arbitrary", "arbitrary")),
    )(d_f, w_down, gu)


def _conv_fwd(proj, col0, width, w, b, name, tm=512):
    t = proj.shape[0]
    tm = min(tm, t)
    cb = col0 // width

    def body(x_ref, w_ref, b_ref, o_ref, ext):
        @pl.when(pl.program_id(0) == 0)
        def _():
            ext[0:8, :] = jnp.zeros((8, width), F32)

        ext[8:8 + tm, :] = x_ref[...]
        y = b_ref[...] + w_ref[0:1, :] * ext[pl.ds(5, tm), :]
        for k in range(1, 4):
            y = y + w_ref[k:k + 1, :] * ext[pl.ds(5 + k, tm), :]
        o_ref[...] = jax.nn.silu(y)
        ext[0:8, :] = ext[tm:tm + 8, :]

    return pl.pallas_call(
        body, grid=(t // tm,), name=name,
        in_specs=[pl.BlockSpec((tm, width), lambda i: (i, cb)), pl.BlockSpec((4, width), lambda i: (0, 0)),
                  pl.BlockSpec((1, width), lambda i: (0, 0))],
        out_specs=pl.BlockSpec((tm, width), lambda i: (i, 0)),
        out_shape=S((t, width), F32), scratch_shapes=[pltpu.VMEM((tm + 8, width), F32)],
        compiler_params=_params(("arbitrary",)),
    )(proj, w, b)


def _conv_bwd(proj, col0, width, w, b, dact, name, tm=512):
    t = proj.shape[0]
    tm = min(tm, t)
    nb = t // tm
    cb = col0 // width
    hb = tm // 8

    def body(x_ref, halo_ref, d_ref, w_ref, b_ref, dx_ref, dw_ref, db_ref, extx, extd):
        i = pl.program_id(0)
        blk = nb - 1 - i

        @pl.when(i == 0)
        def _():
            extd[tm:tm + 8, :] = jnp.zeros((8, width), F32)
            dw_ref[...] = jnp.zeros((4, width), F32)
            db_ref[...] = jnp.zeros((1, width), F32)

        extx[0:8, :] = jnp.where(blk == 0, 0.0, halo_ref[...])
        extx[8:8 + tm, :] = x_ref[...]
        y = b_ref[...] + w_ref[0:1, :] * extx[pl.ds(5, tm), :]
        for k in range(1, 4):
            y = y + w_ref[k:k + 1, :] * extx[pl.ds(5 + k, tm), :]
        sig = jax.nn.sigmoid(y)
        dy = d_ref[...] * (sig * (1.0 + y * (1.0 - sig)))
        extd[0:tm, :] = dy
        dx = w_ref[0:1, :] * extd[pl.ds(3, tm), :]
        for k in range(1, 4):
            dx = dx + w_ref[k:k + 1, :] * extd[pl.ds(3 - k, tm), :]
        dx_ref[...] = dx.astype(dx_ref.dtype)
        for k in range(4):
            dw_ref[k:k + 1, :] += jnp.sum(dy * extx[pl.ds(5 + k, tm), :], axis=0, keepdims=True)
        db_ref[...] += jnp.sum(dy, axis=0, keepdims=True)
        extd[tm:tm + 8, :] = extd[0:8, :]

    return pl.pallas_call(
        body, grid=(nb,), name=name,
        in_specs=[pl.BlockSpec((tm, width), lambda i: (nb - 1 - i, cb)),
                  pl.BlockSpec((8, width), lambda i: (jnp.maximum((nb - 1 - i) * hb - 1, 0), cb)),
                  pl.BlockSpec((tm, width), lambda i: (nb - 1 - i, 0)),
                  pl.BlockSpec((4, width), lambda i: (0, 0)), pl.BlockSpec((1, width), lambda i: (0, 0))],
        out_specs=[pl.BlockSpec((tm, width), lambda i: (nb - 1 - i, 0)), pl.BlockSpec((4, width), lambda i: (0, 0)),
                   pl.BlockSpec((1, width), lambda i: (0, 0))],
        out_shape=[S((t, width), BF16), S((4, width), F32), S((1, width), F32)],
        scratch_shapes=[pltpu.VMEM((tm + 8, width), F32), pltpu.VMEM((tm + 8, width), F32)],
        compiler_params=_params(("arbitrary",)),
    )(proj, proj, dact, w, b)


SWA_BQ = 256
SWA_BACK = 128


def _swa_block(q, kw, vw, sinks, blk):
    nq, nk = SWA_BQ, SWA_BQ + SWA_BACK
    r, j = _iota2(nq, nk)
    rel = r // CHUNK + 2 - j // CHUNK
    valid = (rel >= 0) & (rel <= 2) & (blk * (SWA_BQ // CHUNK) + j // CHUNK - 2 >= 0)
    dist = jnp.abs(r + SWA_BACK - j).astype(F32)
    outs = []
    for h in range(4):
        kv = h // 2
        qh = q[:, 64 * h:64 * h + 64]
        kh = kw[:, 64 * kv:64 * kv + 64]
        vh = vw[:, 64 * kv:64 * kv + 64]
        s = _dot_nt(qh, kh) * 0.125 - ALIBI_SLOPES[h] * dist
        s = jnp.where(valid, s, -1e30)
        sink = _pick_col(sinks, h)
        m = jnp.maximum(jnp.max(s, axis=1, keepdims=True), sink)
        e = jnp.exp(s - m)
        den = jnp.sum(e, axis=1, keepdims=True) + jnp.exp(sink - m)
        outs.append(_dot(e / den, vh))
    return jnp.concatenate(outs, axis=1)


def _swa_fwd(proj, sinks, name):
    t = proj.shape[0]
    qb, kb = PC_ATT // 256, PC_ATT // 128 + 2
    win = SWA_BQ + SWA_BACK

    def body(q_ref, k_ref, v_ref, s_ref, o_ref, kp, vp):
        i = pl.program_id(0)

        @pl.when(i == 0)
        def _():
            kp[0:SWA_BACK, :] = jnp.zeros((SWA_BACK, 128), F32)
            vp[0:SWA_BACK, :] = jnp.zeros((SWA_BACK, 128), F32)
            kp[SWA_BACK:, :] = k_ref[...]
            vp[SWA_BACK:, :] = v_ref[...]

        start = pl.multiple_of(i * SWA_BQ, SWA_BQ)
        o = _swa_block(q_ref[...], kp[pl.ds(start, win), :], vp[pl.ds(start, win), :], s_ref[...], i)
        o_ref[...] = o.astype(o_ref.dtype)

    return pl.pallas_call(
        body, grid=(t // SWA_BQ,), name=name,
        in_specs=[pl.BlockSpec((SWA_BQ, 256), lambda i: (i, qb)), pl.BlockSpec((t, 128), lambda i: (0, kb)),
                  pl.BlockSpec((t, 128), lambda i: (0, kb + 1)), pl.BlockSpec((1, LANES), lambda i: (0, 0))],
        out_specs=pl.BlockSpec((SWA_BQ, 256), lambda i: (i, 0)),
        out_shape=S((t, 256), BF16),
        scratch_shapes=[pltpu.VMEM((t + SWA_BACK, 128), F32), pltpu.VMEM((t + SWA_BACK, 128), F32)],
        compiler_params=_params(("arbitrary",)),
    )(proj, proj, proj, sinks)


def _swa_bwd(proj, sinks, dcat, dcol0, name):
    t = proj.shape[0]
    nb = t // SWA_BQ
    qb, kb = PC_ATT // 256, PC_ATT // 128 + 2
    db = dcol0 // 256
    win = SWA_BQ + SWA_BACK

    def body(q_ref, k_ref, v_ref, s_ref, do_ref, dq_ref, dk_ref, dv_ref, ds_ref, kp, vp, dkp, dvp):
        i = pl.program_id(0)

        @pl.when(i == 0)
        def _():
            kp[0:SWA_BACK, :] = jnp.zeros((SWA_BACK, 128), F32)
            vp[0:SWA_BACK, :] = jnp.zeros((SWA_BACK, 128), F32)
            kp[SWA_BACK:, :] = k_ref[...]
            vp[SWA_BACK:, :] = v_ref[...]
            dkp[...] = jnp.zeros_like(dkp)
            dvp[...] = jnp.zeros_like(dvp)
            ds_ref[...] = jnp.zeros_like(ds_ref)

        start = pl.multiple_of(i * SWA_BQ, SWA_BQ)
        _, vjp = jax.vjp(functools.partial(_swa_block, blk=i), q_ref[...], kp[pl.ds(start, win), :],
                         vp[pl.ds(start, win), :], s_ref[...])
        dq, dkw, dvw, dsk = vjp(do_ref[...])
        dq_ref[...] = dq.astype(dq_ref.dtype)
        dkp[pl.ds(start, win), :] += dkw
        dvp[pl.ds(start, win), :] += dvw
        ds_ref[...] += dsk

        @pl.when(i == nb - 1)
        def _():
            dk_ref[...] = dkp[SWA_BACK:, :].astype(dk_ref.dtype)
            dv_ref[...] = dvp[SWA_BACK:, :].astype(dv_ref.dtype)

    return pl.pallas_call(
        body, grid=(nb,), name=name,
        in_specs=[pl.BlockSpec((SWA_BQ, 256), lambda i: (i, qb)), pl.BlockSpec((t, 128), lambda i: (0, kb)),
                  pl.BlockSpec((t, 128), lambda i: (0, kb + 1)), pl.BlockSpec((1, LANES), lambda i: (0, 0)),
                  pl.BlockSpec((SWA_BQ, 256), lambda i: (i, db))],
        out_specs=[pl.BlockSpec((SWA_BQ, 256), lambda i: (i, 0)), pl.BlockSpec((t, 128), lambda i: (0, 0)),
                   pl.BlockSpec((t, 128), lambda i: (0, 0)), pl.BlockSpec((1, LANES), lambda i: (0, 0))],
        out_shape=[S((t, 256), BF16), S((t, 128), BF16), S((t, 128), BF16), S((1, LANES), F32)],
        scratch_shapes=[pltpu.VMEM((t + SWA_BACK, 128), F32) for _ in range(4)],
        compiler_params=_params(("arbitrary",)),
    )(proj, proj, proj, sinks, dcat)


def _ssd_chunk(z, xbc, dt_raw, state, dtb, alog, dsk, nw):
    n = z.shape[0]
    r, c = _iota2(n, n)
    tril = r >= c
    eye = (r == c).astype(F32)
    dt = jax.nn.softplus(dt_raw + dtb)
    acs = _dot(tril.astype(F32), dt * (-jnp.exp(alog)), HI)
    xs, bm, cm = xbc[:, :512], xbc[:, 512:768], xbc[:, 768:1024]
    heads = range(8)
    bg = [bm[:, 128 * g:128 * g + 128] for g in range(2)]
    cg = [cm[:, 128 * g:128 * g + 128] for g in range(2)]
    cb = [_dot_nt(cg[g], bg[g]) for g in range(2)]
    dth = [_pick_col(dt, h) for h in heads]
    acol = [_pick_col(acs, h) for h in heads]
    arow = [_col_to_row(a, eye) for a in acol]
    lmat = [jnp.where(tril, jnp.exp(jnp.where(tril, a - b, 0.0)), 0.0) for a, b in zip(acol, arow)]
    xh = [xs[:, 64 * h:64 * h + 64] for h in heads]
    xc = [x * t for x, t in zip(xh, dth)]
    st = [state[64 * h:64 * h + 64, :] for h in heads]
    alast = [_pick_row(a, n - 1) for a in acol]
    y_in = [_dot(cb[h // 4] * lmat[h], xc[h]) for h in heads]
    y_st = [_dot_nt(cg[h // 4], st[h]) * jnp.exp(acol[h]) for h in heads]
    ys = [y_in[h] + y_st[h] + xh[h] * _pick_col(dsk, h) for h in heads]
    new_states = [st[h] * jnp.exp(alast[h]) + _dot_tn(xc[h] * jnp.exp(alast[h] - acol[h]), bg[h // 4]) for h in heads]
    gg = jnp.concatenate(ys, axis=1) * jax.nn.silu(z)
    outs = []
    for gi in range(2):
        gv = gg[:, 256 * gi:256 * gi + 256]
        outs.append(gv * lax.rsqrt(jnp.mean(gv * gv, axis=-1, keepdims=True) + EPS))
    return jnp.concatenate(outs, axis=1) * nw, jnp.concatenate(new_states, axis=0)


def _ssd_fwd(proj, xbc, dtb, alog, dsk, nw, name, exchange=None):
    t = proj.shape[0]
    CHUNK = min(SSD_CHUNK, t)
    nc = t // CHUNK

    def body(z_ref, x_ref, dt_ref, dtb_ref, al_ref, d_ref, nw_ref, o_ref, st_ref, state):
        @pl.when(pl.program_id(0) == 0)
        def _():
            state[...] = jnp.zeros_like(state)

        st_ref[0] = state[...]
        o, ns = _ssd_chunk(z_ref[...], x_ref[...], dt_ref[...], state[...], dtb_ref[...], al_ref[...], d_ref[...],
                           nw_ref[...])
        o_ref[...] = o.astype(o_ref.dtype)
        state[...] = ns

    body, x_in, x_out, x_shape, x_sems = _hosted(exchange, 7, 2, nc, body)
    vec = pl.BlockSpec((1, LANES), lambda i: (0, 0))
    return pl.pallas_call(
        body, grid=(nc,), name=name,
        in_specs=[pl.BlockSpec((CHUNK, 512), lambda i: (i, PC_SZ // 512)), pl.BlockSpec((CHUNK, 1024), lambda i: (i, 0)),
                  pl.BlockSpec((CHUNK, 128), lambda i: (i, PC_DT // 128)), vec, vec, vec,
                  pl.BlockSpec((1, 512), lambda i: (0, 0))] + x_in,
        out_specs=[pl.BlockSpec((CHUNK, 512), lambda i: (i, 0)), pl.BlockSpec((1, 512, 128), lambda i: (i, 0, 0))] + x_out,
        out_shape=[S((t, 512), BF16), S((nc, 512, 128), F32)] + x_shape,
        scratch_shapes=[pltpu.VMEM((512, 128), F32)] + x_sems,
        compiler_params=_params(("arbitrary",)),
    )(proj, xbc, proj, dtb, alog, dsk, nw, *([] if exchange is None else exchange.arrays))


def _ssd_bwd(proj, xbc, states, dtb, alog, dsk, nw, dcat, dcol0, name, exchange=None):
    t = proj.shape[0]
    CHUNK = min(SSD_CHUNK, t)
    nc = t // CHUNK
    db = dcol0 // 512

    def body(z_ref, x_ref, dt_ref, st_ref, dtb_ref, al_ref, d_ref, nw_ref, do_ref,
             dz_ref, dx_ref, ddt_ref, gdtb_ref, gal_ref, gd_ref, gnw_ref, dstate):
        @pl.when(pl.program_id(0) == 0)
        def _():
            dstate[...] = jnp.zeros_like(dstate)
            gdtb_ref[...] = jnp.zeros_like(gdtb_ref)
            gal_ref[...] = jnp.zeros_like(gal_ref)
            gd_ref[...] = jnp.zeros_like(gd_ref)
            gnw_ref[...] = jnp.zeros_like(gnw_ref)

        _, vjp = jax.vjp(_ssd_chunk, z_ref[...], x_ref[...], dt_ref[...], st_ref[0], dtb_ref[...], al_ref[...],
                         d_ref[...], nw_ref[...])
        dz, dx, ddt, dst, gdtb, gal, gd, gnw = vjp((do_ref[...], dstate[...]))
        dz_ref[...] = dz.astype(dz_ref.dtype)
        dx_ref[...] = dx
        ddt_ref[...] = ddt.astype(ddt_ref.dtype)
        dstate[...] = dst
        gdtb_ref[...] += gdtb
        gal_ref[...] += gal
        gd_ref[...] += gd
        gnw_ref[...] += gnw

    body, x_in, x_out, x_shape, x_sems = _hosted(exchange, 9, 7, nc, body)
    rev = lambda i: nc - 1 - i
    vec = pl.BlockSpec((1, LANES), lambda i: (0, 0))
    vec512 = pl.BlockSpec((1, 512), lambda i: (0, 0))
    return pl.pallas_call(
        body, grid=(nc,), name=name,
        in_specs=[pl.BlockSpec((CHUNK, 512), lambda i: (rev(i), PC_SZ // 512)),
                  pl.BlockSpec((CHUNK, 1024), lambda i: (rev(i), 0)),
                  pl.BlockSpec((CHUNK, 128), lambda i: (rev(i), PC_DT // 128)),
                  pl.BlockSpec((1, 512, 128), lambda i: (rev(i), 0, 0)), vec, vec, vec, vec512,
                  pl.BlockSpec((CHUNK, 512), lambda i: (rev(i), db))] + x_in,
        out_specs=[pl.BlockSpec((CHUNK, 512), lambda i: (rev(i), 0)), pl.BlockSpec((CHUNK, 1024), lambda i: (rev(i), 0)),
                   pl.BlockSpec((CHUNK, 128), lambda i: (rev(i), 0)), vec, vec, vec, vec512] + x_out,
        out_shape=[S((t, 512), BF16), S((t, 1024), F32), S((t, 128), BF16), S((1, LANES), F32), S((1, LANES), F32),
                   S((1, LANES), F32), S((1, 512), F32)] + x_shape,
        scratch_shapes=[pltpu.VMEM((512, 128), F32)] + x_sems,
        compiler_params=_params(("arbitrary",)),
    )(proj, xbc, proj, states, dtb, alog, dsk, nw, dcat, *([] if exchange is None else exchange.arrays))


SOLVE_PREC = lax.Precision.HIGH


def _unit_lower_inverses(nas, known=None):
    def compute(ns):
        if known is not None:
            return tuple(known)
        n = ns[0].shape[0]
        r, c = _iota2(n, n)
        eye = (r == c).astype(F32)
        tm, pw = [eye + a for a in ns], list(ns)
        for _ in range(n.bit_length() - 2):
            pw = [_dot(p, p, SOLVE_PREC) for p in pw]
            tm = [t + _dot(t, p, SOLVE_PREC) for t, p in zip(tm, pw)]
        return tuple(tm)

    inv = jax.custom_vjp(compute)

    def fwd(ns):
        ts = compute(ns)
        return ts, ts

    def bwd(ts, gs):
        part = [_dot_nt(g, t, SOLVE_PREC) for g, t in zip(gs, ts)]
        return (tuple(_dot_tn(t, p, SOLVE_PREC) for t, p in zip(ts, part)),)

    inv.defvjp(fwd, bwd)
    return inv(nas)


def _gdn_chunk(qkv, z, ba, state, dtb, alog, nw, known_inverses=None):
    n = qkv.shape[0]
    r, c = _iota2(n, n)
    tril = r >= c
    stril = r > c
    eye = (r == c).astype(F32)
    beta_all = jax.nn.sigmoid(ba)
    gcs = _dot(tril.astype(F32), -jnp.exp(alog) * jax.nn.softplus(ba + dtb), HI)
    heads = range(4)
    qh = [qkv[:, 64 * h:64 * h + 64] for h in heads]
    kh = [qkv[:, 256 + 64 * h:256 + 64 * h + 64] for h in heads]
    vh = [qkv[:, 512 + 64 * h:512 + 64 * h + 64] for h in heads]
    qn = [q * lax.rsqrt(jnp.sum(q * q, axis=-1, keepdims=True) + EPS) * 0.125 for q in qh]
    kn = [k * lax.rsqrt(jnp.sum(k * k, axis=-1, keepdims=True) + EPS) for k in kh]
    beta = [_pick_col(beta_all, h) for h in heads]
    gcol = [_pick_col(gcs, 4 + h) for h in heads]
    grow = [_col_to_row(g, eye) for g in gcol]
    decay = [jnp.where(tril, jnp.exp(jnp.where(tril, gc - gr, 0.0)), 0.0) for gc, gr in zip(gcol, grow)]
    kbeta = [k * b for k, b in zip(kn, beta)]
    kk = [_dot_nt(kb, k) for kb, k in zip(kbeta, kn)]
    qk = [_dot_nt(q, k) * dc for q, k, dc in zip(qn, kn, decay)]
    known = None if known_inverses is None else [known_inverses[n * h:n * (h + 1), :] for h in heads]
    tms = _unit_lower_inverses(tuple(-jnp.where(stril, x * dc, 0.0) for x, dc in zip(kk, decay)), known)
    rhs = [jnp.concatenate([v * b, kb * jnp.exp(g)], axis=1) for v, b, kb, g in zip(vh, beta, kbeta, gcol)]
    sol = [_dot(t, x, SOLVE_PREC) for t, x in zip(tms, rhs)]
    st = [state[64 * h:64 * h + 64, :] for h in heads]
    v_new = [s_[:, :64] - _dot(s_[:, 64:], s) for s_, s in zip(sol, st)]
    o = [_dot(q * jnp.exp(g), s) + _dot(x, vn) for q, g, s, x, vn in zip(qn, gcol, st, qk, v_new)]
    glast = [_pick_row(g, n - 1) for g in gcol]
    new_states = [s * jnp.exp(gl) + _dot_tn(k * jnp.exp(gl - g), vn)
                  for s, gl, k, g, vn in zip(st, glast, kn, gcol, v_new)]
    o = [x * lax.rsqrt(jnp.mean(x * x, axis=-1, keepdims=True) + EPS) * nw for x in o]
    outs = [x * jax.nn.silu(z[:, 64 * h:64 * h + 64]) for h, x in zip(heads, o)]
    return jnp.concatenate(outs, axis=1), jnp.concatenate(new_states, axis=0), jnp.concatenate(tms, axis=0)


class _Exchange:
    def __init__(self, arrays, out_shape, n_sems, start, finish):
        self.arrays, self.out_shape, self.n_sems, self.start, self.finish = arrays, out_shape, n_sems, start, finish


def _hosted(exchange, n_in, n_out, grid, body):
    if exchange is None:
        return body, [], [], [], []
    k_in, k_out = len(exchange.arrays), len(exchange.out_shape)
    grid = (grid,) if isinstance(grid, int) else tuple(grid)

    def hosted_body(*refs):
        ins, refs = refs[:n_in + k_in], refs[n_in + k_in:]
        outs, scratch = refs[:n_out + k_out], refs[n_out + k_out:]
        sems = scratch[-2:]
        first, last = True, True
        for axis, steps in enumerate(grid):
            first = first & (pl.program_id(axis) == 0)
            last = last & (pl.program_id(axis) == steps - 1)

        @pl.when(first)
        def _():
            exchange.start(ins[n_in:], outs[n_out:], sems)

        body(*ins[:n_in], *outs[:n_out], *scratch[:-2])

        @pl.when(last)
        def _():
            exchange.finish(ins[n_in:], outs[n_out:], sems)

    return hosted_body, [ANY] * k_in, [ANY] * k_out, list(exchange.out_shape), _sem_pairs(exchange.n_sems)


def _gdn_fwd(proj, qkv, dtb, alog, nw, name, exchange=None):
    t = proj.shape[0]
    CHUNK = min(GDN_CHUNK, t)
    nc = t // CHUNK

    def body(q_ref, z_ref, ba_ref, dtb_ref, al_ref, nw_ref, o_ref, st_ref, inv_ref, state):
        @pl.when(pl.program_id(0) == 0)
        def _():
            state[...] = jnp.zeros_like(state)

        st_ref[0] = state[...]
        o, ns, tms = _gdn_chunk(q_ref[...], z_ref[...], ba_ref[...], state[...], dtb_ref[...], al_ref[...], nw_ref[...])
        o_ref[...] = o.astype(o_ref.dtype)
        inv_ref[0] = tms
        state[...] = ns

    body, x_in, x_out, x_shape, x_sems = _hosted(exchange, 6, 3, nc, body)
    vec = pl.BlockSpec((1, LANES), lambda i: (0, 0))
    per_chunk = pl.BlockSpec((1, 256, 64), lambda i: (i, 0, 0))
    return pl.pallas_call(
        body, grid=(nc,), name=name,
        in_specs=[pl.BlockSpec((CHUNK, 768), lambda i: (i, 0)), pl.BlockSpec((CHUNK, 256), lambda i: (i, PC_GZ // 256)),
                  pl.BlockSpec((CHUNK, 128), lambda i: (i, PC_BA // 128)), vec, vec, pl.BlockSpec((1, 64), lambda i: (0, 0))]
        + x_in,
        out_specs=[pl.BlockSpec((CHUNK, 256), lambda i: (i, 0)), per_chunk,
                   pl.BlockSpec((1, 4 * CHUNK, CHUNK), lambda i: (i, 0, 0))] + x_out,
        out_shape=[S((t, 256), BF16), S((nc, 256, 64), F32), S((nc, 4 * CHUNK, CHUNK), F32)] + x_shape,
        scratch_shapes=[pltpu.VMEM((256, 64), F32)] + x_sems,
        compiler_params=_params(("arbitrary",)),
    )(qkv, proj, proj, dtb, alog, nw, *([] if exchange is None else exchange.arrays))


def _gdn_bwd(proj, qkv, states, inverses, dtb, alog, nw, dcat, dcol0, name, exchange=None):
    t = proj.shape[0]
    CHUNK = min(GDN_CHUNK, t)
    nc = t // CHUNK
    db = dcol0 // 256

    def body(q_ref, z_ref, ba_ref, st_ref, inv_ref, dtb_ref, al_ref, nw_ref, do_ref,
             dq_ref, dz_ref, dba_ref, gdtb_ref, gal_ref, gnw_ref, dstate):
        @pl.when(pl.program_id(0) == 0)
        def _():
            dstate[...] = jnp.zeros_like(dstate)
            gdtb_ref[...] = jnp.zeros_like(gdtb_ref)
            gal_ref[...] = jnp.zeros_like(gal_ref)
            gnw_ref[...] = jnp.zeros_like(gnw_ref)

        def chunk(*operands):
            return _gdn_chunk(*operands, known_inverses=inv_ref[0])[:2]

        _, vjp = jax.vjp(chunk, q_ref[...], z_ref[...], ba_ref[...], st_ref[0], dtb_ref[...], al_ref[...], nw_ref[...])
        dq, dz, dba, dst, gdtb, gal, gnw = vjp((do_ref[...], dstate[...]))
        dq_ref[...] = dq
        dz_ref[...] = dz.astype(dz_ref.dtype)
        dba_ref[...] = dba.astype(dba_ref.dtype)
        dstate[...] = dst
        gdtb_ref[...] += gdtb
        gal_ref[...] += gal
        gnw_ref[...] += gnw

    body, x_in, x_out, x_shape, x_sems = _hosted(exchange, 9, 6, nc, body)
    rev = lambda i: nc - 1 - i
    vec = pl.BlockSpec((1, LANES), lambda i: (0, 0))
    vec64 = pl.BlockSpec((1, 64), lambda i: (0, 0))
    per_chunk = pl.BlockSpec((1, 256, 64), lambda i: (rev(i), 0, 0))
    return pl.pallas_call(
        body, grid=(nc,), name=name,
        in_specs=[pl.BlockSpec((CHUNK, 768), lambda i: (rev(i), 0)),
                  pl.BlockSpec((CHUNK, 256), lambda i: (rev(i), PC_GZ // 256)),
                  pl.BlockSpec((CHUNK, 128), lambda i: (rev(i), PC_BA // 128)),
                  per_chunk, pl.BlockSpec((1, 4 * CHUNK, CHUNK), lambda i: (rev(i), 0, 0)), vec, vec, vec64,
                  pl.BlockSpec((CHUNK, 256), lambda i: (rev(i), db))] + x_in,
        out_specs=[pl.BlockSpec((CHUNK, 768), lambda i: (rev(i), 0)), pl.BlockSpec((CHUNK, 256), lambda i: (rev(i), 0)),
                   pl.BlockSpec((CHUNK, 128), lambda i: (rev(i), 0)), vec, vec, vec64] + x_out,
        out_shape=[S((t, 768), F32), S((t, 256), BF16), S((t, 128), BF16), S((1, LANES), F32), S((1, LANES), F32),
                   S((1, 64), F32)] + x_shape,
        scratch_shapes=[pltpu.VMEM((256, 64), F32)] + x_sems,
        compiler_params=_params(("arbitrary",)),
    )(qkv, proj, proj, states, inverses, dtb, alog, nw, dcat, *([] if exchange is None else exchange.arrays))


def _pad_cols(w):
    z = jnp.zeros((w.shape[0], 120), w.dtype)
    return jnp.concatenate([w[:, 2056:2824], w[:, 2824:3080], w[:, 1024:2048], w[:, 0:512], w[:, 512:1024],
                            w[:, 2048:2056], z, w[:, 3080:3088], z], axis=1)


def _unpad_cols(g):
    return jnp.concatenate([g[:, PC_ATT:PC_ATT + 512], g[:, PC_SZ:PC_SZ + 512], g[:, PC_XBC:PC_XBC + 1024],
                            g[:, PC_DT:PC_DT + 8], g[:, PC_GQKV:PC_GQKV + 768], g[:, PC_GZ:PC_GZ + 256],
                            g[:, PC_BA:PC_BA + 8]], axis=1)


def _vec128(v, at=0):
    return jnp.zeros((1, LANES), F32).at[0, at:at + v.shape[0]].set(v)


def _in_weight(gathered):
    return _pad_cols(jnp.concatenate([gathered[k].reshape(D_MODEL, -1) for k in range(N_CHIPS)], axis=1))


def _matmul_weights(w_in, gathered):
    cols = lambda name: jnp.concatenate([gathered[name][k] for k in range(N_CHIPS)], axis=1)
    rows = lambda name: jnp.concatenate([gathered[name][k] for k in range(N_CHIPS)], axis=0)
    w_out = rows("w_out")
    return dict(
        w_in=w_in,
        w_out=jnp.concatenate([w_out[256:768], w_out[0:256], w_out[768:1024]], axis=0),
        w_gu=_interleave_gu(cols("ffn_w_gate"), cols("ffn_w_up")),
        w_down=rows("ffn_w_down"))


def _small_operands(w, l):
    return dict(
        pre_mix=w["pre_mix_norm"][l][None], post_mix=w["post_mix_norm"][l][None],
        pre_ffn=w["pre_ffn_norm"][l][None], post_ffn=w["post_ffn_norm"][l][None],
        sinks=_vec128(w["attn_sinks"][l]),
        s_cw=w["ssd_conv_w"][l], s_cb=w["ssd_conv_b"][l][None],
        s_dtb=_vec128(w["ssd_dt_bias"][l]), s_alog=_vec128(w["ssd_A_log"][l]), s_d=_vec128(w["ssd_D"][l]),
        s_nw=w["ssd_norm_w"][l][None],
        g_cw=w["gdn_conv_w"][l], g_cb=jnp.zeros((1, 768), F32),
        g_dtb=_vec128(w["gdn_dt_bias"][l], 4), g_alog=_vec128(w["gdn_A_log"][l], 4), g_nw=w["gdn_norm_w"][l][None],
    )


RAW_GRADS = ("w_in_pad", "w_out_cat", "w_gu", "ffn_w_down")
DW_ROWS = 4096


def _local_step(x, target, lw, w_in0, gathers, matmul_weights, reducer):
    saved, landed = [], {}
    xin = x
    h = _prenorm(x, lw[0]["pre_mix"], "prenorm0")
    for l in range(DEPTH):
        p = lw[l]
        carry = (lambda kind: gathers[kind]) if l == 0 else (lambda kind: None)
        proj = _mm_nn(h, w_in0 if l == 0 else p["w_in"], 512, PC_TOT, F32, f"inproj{l}")
        xbc = _conv_fwd(proj, PC_XBC, 1024, p["s_cw"], p["s_cb"], f"ssd_conv{l}")
        gqkv = _conv_fwd(proj, PC_GQKV, 768, p["g_cw"], p["g_cb"], f"gdn_conv{l}")
        att = _swa_fwd(proj, p["sinks"], f"swa{l}")
        ssd, s_states, *landed_s = _ssd_fwd(proj, xbc, p["s_dtb"], p["s_alog"], p["s_d"], p["s_nw"], f"ssd{l}",
                                            carry("ssd"))
        gdn, g_states, g_inv, *landed_g = _gdn_fwd(proj, gqkv, p["g_dtb"], p["g_alog"], p["g_nw"], f"gdn{l}",
                                                   carry("gdn"))
        if l == 0:
            landed.update(ssd=landed_s, gdn=landed_g)
            p.update(matmul_weights(0, landed))
        cat = jnp.concatenate([ssd, att, gdn], axis=1)
        mix = _mm_nn(cat, p["w_out"], 512, 1024, F32, f"outproj{l}")
        x1, h2 = _resid_norm(xin, mix, p["post_mix"], p["pre_ffn"], f"postmix{l}")
        gu, act, *landed_u = _ffn_up(h2, p["w_gu"], f"ffn_gu{l}", exchange=carry("ffn_gu"))
        if l == 0:
            f, *landed_d = _mm_nn(act, p["w_down"], 512, 1024, F32, f"ffn_down{l}", carry("ffn_down"))
            landed.update(ffn_gu=landed_u, ffn_down=landed_d)
            lw[1].update(matmul_weights(1, landed))
        else:
            f = _mm_nn(act, p["w_down"], 512, 1024, F32, f"ffn_down{l}")
        saved.append(dict(xin=xin, h=h, proj=proj, xbc=xbc, gqkv=gqkv, s_states=s_states, g_states=g_states, g_inv=g_inv,
                          cat=cat, mix=mix, x1=x1, h2=h2, gu=gu, act=act, f=f))
        if l + 1 < DEPTH:
            xin, h = _resid_norm(x1, f, p["post_ffn"], lw[l + 1]["pre_mix"], f"postffn{l}")

    g = {k: [None] * DEPTH for k in SMALL + CONV + RAW_GRADS}
    last = saved[-1]
    d_x2, d_f, loss_part, g["post_ffn_norm"][DEPTH - 1] = _resid_loss(
        last["x1"], last["f"], lw[-1]["post_ffn"], target, "loss")
    early = []
    for l in reversed(range(DEPTH)):
        p, s = lw[l], saved[l]
        d_gu = _ffn_down_bwd(d_f, p["w_down"], s["gu"], f"d_act{l}")
        g["ffn_w_down"][l] = _mm_tn(s["act"], d_f, FF_HALF, 512, DW_ROWS, f"dw_down{l}")
        d_h2 = _mm_nt(d_gu, p["w_gu"], 256, 1024, F32, f"d_h2{l}")
        g["w_gu"][l] = _mm_tn(s["h2"], d_gu, 512, FF_HALF, DW_ROWS, f"dw_gu{l}")
        d_x1, d_mix, g["pre_ffn_norm"][l], g["post_mix_norm"][l] = _resid_norm_bwd(
            s["x1"], s["mix"], d_x2, d_h2, p["post_mix"], p["pre_ffn"], f"d_postmix{l}")
        d_cat = _mm_nt(d_mix, p["w_out"], 512, 1024, F32, f"d_cat{l}")
        g["w_out_cat"][l] = _mm_tn(s["cat"], d_mix, 512, 1024, DW_ROWS, f"dw_out{l}")
        d_q, d_k, d_v, g_sinks = _swa_bwd(s["proj"], p["sinks"], d_cat, 512, f"d_swa{l}")
        d_sz, d_xbc, d_dt, g_dtb, g_alog, g_d, g["ssd_norm_w"][l], *siblings = _ssd_bwd(
            s["proj"], s["xbc"], s["s_states"], p["s_dtb"], p["s_alog"], p["s_d"], p["s_nw"], d_cat, 0, f"d_ssd{l}",
            reducer.exchange(g) if l == 0 else None)
        d_gq, d_gz, d_ba, gg_dtb, gg_alog, g["gdn_norm_w"][l], *landed_b = _gdn_bwd(
            s["proj"], s["gqkv"], s["g_states"], s["g_inv"], p["g_dtb"], p["g_alog"], p["g_nw"], d_cat, 768, f"d_gdn{l}",
            reducer.scatter(siblings) if l == 0 else None)
        if l == 0:
            early = landed_b
        d_xbc_raw, g["ssd_conv_w"][l], g["ssd_conv_b"][l] = _conv_bwd(
            s["proj"], PC_XBC, 1024, p["s_cw"], p["s_cb"], d_xbc, f"d_ssd_conv{l}")
        d_gq_raw, g["gdn_conv_w"][l], _ = _conv_bwd(s["proj"], PC_GQKV, 768, p["g_cw"], p["g_cb"], d_gq, f"d_gdn_conv{l}")
        d_proj = jnp.concatenate([d_gq_raw, d_gz, d_xbc_raw, d_q, d_k, d_v, d_sz, d_dt, d_ba], axis=1)
        d_h = _mm_nt(d_proj, p["w_in"], 512, 1024, F32, f"d_h{l}")
        g["w_in_pad"][l] = _mm_tn(s["h"], d_proj, 512, PC_TOT // 2, DW_ROWS, f"dw_in{l}")
        g["attn_sinks"][l] = g_sinks[0, :4]
        g["ssd_dt_bias"][l], g["ssd_A_log"][l], g["ssd_D"][l] = g_dtb[0, :8], g_alog[0, :8], g_d[0, :8]
        g["gdn_dt_bias"][l], g["gdn_A_log"][l] = gg_dtb[0, 4:8], gg_alog[0, 4:8]
        if l > 0:
            sp = saved[l - 1]
            d_x2, d_f, g["pre_mix_norm"][l], g["post_ffn_norm"][l - 1] = _resid_norm_bwd(
                s["xin"], sp["f"], d_x1, d_h, lw[l - 1]["post_ffn"], p["pre_mix"], f"d_postffn{l - 1}")
        else:
            grad_x, g["pre_mix_norm"][0] = _prenorm_bwd(s["xin"], d_x1, d_h, p["pre_mix"], "d_prenorm0")

    small = {k: jnp.stack([a.reshape(-1) for a in g[k]], axis=0) for k in SMALL + CONV}
    return loss_part, grad_x, small, {k: g[k] for k in RAW_GRADS}, early


BIG = (("w_in", 2), ("w_out", 1), ("ffn_w_gate", 2), ("ffn_w_up", 2), ("ffn_w_down", 1))
CONV = ("ssd_conv_w", "gdn_conv_w")
SMALL = ("pre_mix_norm", "post_mix_norm", "pre_ffn_norm", "post_ffn_norm", "attn_sinks", "ssd_conv_b", "ssd_dt_bias",
         "ssd_A_log", "ssd_D", "ssd_norm_w", "gdn_dt_bias", "gdn_A_log", "gdn_norm_w")


def _row_tile(rows, cap):
    best = rows
    for t in range(8, min(cap, rows) + 1, 8):
        if rows % t == 0:
            best = t
    return best


SMALL_UNIT = 8 * LANES


def _pack_small(vals):
    rows = []
    for a in vals:
        f = a.reshape(-1)
        pad = -f.shape[0] % SMALL_UNIT
        rows.append(jnp.concatenate([f, jnp.zeros((pad,), F32)]).reshape(-1, LANES))
    return jnp.concatenate(rows, axis=0)


def _unpack_small(mat, shapes):
    out, r = [], 0
    for shp in shapes:
        n = math.prod(shp)
        nr = -(-n // SMALL_UNIT) * 8
        out.append(mat[r:r + nr].reshape(-1)[:n].reshape(shp))
        r += nr
    return out


def _place():
    x, y, c = lax.axis_index("x"), lax.axis_index("y"), lax.axis_index("c")
    chips = [(1 - x, y), (x, 1 - y), (1 - x, 1 - y)]
    return x, y, c, chips


ANY = pl.BlockSpec(memory_space=pl.ANY)


def _remote(src, dst, sems, k, to):
    send_sems, recv_sems = sems
    return pltpu.make_async_remote_copy(src_ref=src, dst_ref=dst, send_sem=send_sems.at[k], recv_sem=recv_sems.at[k],
                                        device_id=to, device_id_type=MESH)


def _sem_pairs(n):
    return [pltpu.SemaphoreType.DMA((n,)), pltpu.SemaphoreType.DMA((n,))]


def _run_exchange(exchange, name):
    k = len(exchange.arrays)

    def body(*refs):
        ins, outs, sems = refs[:k], refs[k:-2], refs[-2:]
        exchange.start(ins, outs, sems)
        exchange.finish(ins, outs, sems)

    return pl.pallas_call(
        body, name=name, in_specs=[ANY] * k, out_specs=[ANY] * len(exchange.out_shape),
        out_shape=list(exchange.out_shape), scratch_shapes=_sem_pairs(exchange.n_sems),
    )(*exchange.arrays)


def _gather_exchange(shards):
    n = len(shards)

    def sends(s_refs, g_refs, sems):
        x, y, c, chips = _place()
        return [_remote(s_refs[i].at[c], g_refs[i].at[2 * x + y, c], sems, 6 * i + j, (px, py, c))
                for i in range(n) for j, (px, py) in enumerate(chips)]

    def start(s_refs, g_refs, sems):
        for cp in sends(s_refs, g_refs, sems):
            cp.start()

    def finish(s_refs, g_refs, sems):
        x, y, c, chips = _place()
        sib = (x, y, 1 - c)
        passed = []
        for j, (px, py) in enumerate(chips):
            for i in range(n):
                landed = g_refs[i].at[2 * px + py, c]
                _remote(landed, landed, sems, 6 * i + j, (px, py, c)).wait_recv()
                fw = _remote(landed, landed, sems, 6 * i + 3 + j, sib)
                fw.start()
                passed.append(fw)
        for j, (px, py) in enumerate(chips):
            for i in range(n):
                landed = g_refs[i].at[2 * px + py, 1 - c]
                _remote(landed, landed, sems, 6 * i + 3 + j, sib).wait_recv()
        for cp in sends(s_refs, g_refs, sems) + passed:
            cp.wait_send()

    return _Exchange(shards, [S((N_CHIPS,) + a.shape, a.dtype) for a in shards], 6 * n, start, finish)


def _halves_exchange(ds):
    n = len(ds)

    def copies(d_refs, t_refs, sems):
        x, y, c, _ = _place()
        return [_remote(d_refs[i].at[:, 1 - c], t_refs[i], sems, i, (x, y, 1 - c)) for i in range(n)]

    def start(d_refs, t_refs, sems):
        for cp in copies(d_refs, t_refs, sems):
            cp.start()

    def finish(d_refs, t_refs, sems):
        for cp in copies(d_refs, t_refs, sems):
            cp.wait()

    return _Exchange(ds, [S((N_CHIPS,) + a.shape[2:], a.dtype) for a in ds], n, start, finish)


def _scatter_exchange(ps):
    n = len(ps)

    def copies(p_refs, u_refs, sems):
        x, y, c, chips = _place()
        return [_remote(p_refs[i].at[2 * px + py], u_refs[i].at[j], sems, 3 * i + j, (px, py, c))
                for j, (px, py) in enumerate(chips) for i in range(n)]

    def start(p_refs, u_refs, sems):
        for cp in copies(p_refs, u_refs, sems):
            cp.start()

    def finish(p_refs, u_refs, sems):
        for cp in copies(p_refs, u_refs, sems):
            cp.wait()

    return _Exchange(ps, [S((3,) + a.shape[1:], a.dtype) for a in ps], 3 * n, start, finish)


def _join_halves(qs, name):
    n = len(qs)

    def body(*refs):
        o_refs, sems = refs[n:2 * n], refs[2 * n:]
        x, y, c, _ = _place()
        cps = [_remote(o_refs[i].at[c], o_refs[i].at[c], sems, i, (x, y, 1 - c)) for i in range(n)]
        for cp in cps:
            cp.start()
        for i in range(n):
            other = o_refs[i].at[1 - c]
            _remote(other, other, sems, i, (x, y, 1 - c)).wait_recv()
        for cp in cps:
            cp.wait_send()

    return pl.pallas_call(
        body, name=name, in_specs=[ANY] * n, out_specs=[ANY] * n,
        out_shape=[S(a.shape, a.dtype) for a in qs], input_output_aliases={i: i for i in range(n)},
        scratch_shapes=_sem_pairs(n),
    )(*qs)


def _gather_small(v, name):
    def body(v_ref, o_ref, send_sems, recv_sems, local_sem):
        x, y, c, _ = _place()
        me = 4 * x + 2 * y + c
        mine = pltpu.make_async_copy(v_ref, o_ref.at[me], local_sem)
        mine.start()
        cps = []
        for k in range(1, N_DEV):
            fx, fy, fc = (k >> 2) & 1, (k >> 1) & 1, k & 1
            peer = (x ^ fx, y ^ fy, c ^ fc)
            cps.append(pltpu.make_async_remote_copy(
                src_ref=v_ref, dst_ref=o_ref.at[me], send_sem=send_sems.at[k - 1], recv_sem=recv_sems.at[k - 1],
                device_id=peer, device_id_type=MESH))
        for cp in cps:
            cp.start()
        for k in range(1, N_DEV):
            fx, fy, fc = (k >> 2) & 1, (k >> 1) & 1, k & 1
            dst = o_ref.at[4 * (x ^ fx) + 2 * (y ^ fy) + (c ^ fc)]
            pltpu.make_async_remote_copy(src_ref=dst, dst_ref=dst, send_sem=send_sems.at[k - 1],
                                         recv_sem=recv_sems.at[k - 1], device_id=(x, y, c),
                                         device_id_type=MESH).wait_recv()
        for cp in cps:
            cp.wait_send()
        mine.wait()

    return pl.pallas_call(
        body, name=name, in_specs=[ANY], out_specs=ANY, out_shape=S((N_DEV,) + v.shape, F32),
        scratch_shapes=[pltpu.SemaphoreType.DMA((N_DEV - 1,)), pltpu.SemaphoreType.DMA((N_DEV - 1,)),
                        pltpu.SemaphoreType.DMA],
    )(v)


def _sum_leading(a, name):
    n, rows, cols = a.shape
    tm = _row_tile(rows, 640)

    def body(a_ref, o_ref):
        acc = a_ref[0]
        for k in range(1, n):
            acc = acc + a_ref[k]
        o_ref[...] = acc

    return pl.pallas_call(
        body, grid=(rows // tm,), name=name, in_specs=[pl.BlockSpec((n, tm, cols), lambda i: (0, i, 0))],
        out_specs=pl.BlockSpec((tm, cols), lambda i: (i, 0)), out_shape=S((rows, cols), F32),
        compiler_params=_params(("arbitrary",)),
    )(a)


def _add_sibling(place, a, b, name):
    n, hr, cols = b.shape
    tm = _row_tile(hr, 512)

    def body(place_ref, a_ref, b_ref, o_ref, o16_ref):
        tot = a_ref[0] + b_ref[...]
        o_ref[...] = tot
        o16_ref[...] = tot.astype(BF16)

    spec = pl.BlockSpec((1, tm, cols), lambda k, i, pr: (k, i, 0))
    return pl.pallas_call(
        body, name=name, out_shape=[S((n, hr, cols), F32), S((n, hr, cols), BF16)],
        grid_spec=pltpu.PrefetchScalarGridSpec(
            num_scalar_prefetch=1, grid=(n, hr // tm),
            in_specs=[pl.BlockSpec((1, 1, tm, cols), lambda k, i, pr: (k, pr[0], i, 0)), spec], out_specs=[spec, spec]),
        compiler_params=_params(("arbitrary", "arbitrary")),
    )(place, a, b)


def _add_chips(place, sums, others, name):
    _, hr, cols = sums.shape
    tm = _row_tile(hr, 512)

    def body(place_ref, m_ref, o_ref, out_ref):
        acc = m_ref[0]
        for k in range(others.shape[0]):
            acc = acc + o_ref[k].astype(F32)
        out_ref[0] = acc

    return pl.pallas_call(
        body, name=name, out_shape=S((2, hr, cols), F32),
        grid_spec=pltpu.PrefetchScalarGridSpec(
            num_scalar_prefetch=1, grid=(hr // tm,),
            in_specs=[pl.BlockSpec((1, tm, cols), lambda i, pr: (pr[1], i, 0)),
                      pl.BlockSpec((others.shape[0], tm, cols), lambda i, pr: (0, i, 0))],
            out_specs=pl.BlockSpec((1, tm, cols), lambda i, pr: (pr[0], i, 0))),
        compiler_params=_params(("arbitrary",)),
    )(place, sums, others)


def _adamw(wt, g, m, v, name):
    shape = wt.shape
    cols = shape[-1]
    rows = math.prod(shape[:-1])
    tm = rows
    for cand in (512, 256, 128, 64, 32, 16, 8):
        if rows % cand == 0:
            tm = cand
            break
    c1 = 1.0 - ADAM_B1 ** ADAM_STEP
    c2 = 1.0 - ADAM_B2 ** ADAM_STEP

    def body(w_ref, g_ref, m_ref, v_ref, d_ref, nm_ref, nv_ref):
        gv = g_ref[...]
        nm = ADAM_B1 * m_ref[...] + (1.0 - ADAM_B1) * gv
        nv = ADAM_B2 * v_ref[...] + (1.0 - ADAM_B2) * (gv * gv)
        d_ref[...] = -ADAM_LR * ((nm / c1) / (jnp.sqrt(nv / c2) + ADAM_EPS) + ADAM_WD * w_ref[...])
        nm_ref[...] = nm
        nv_ref[...] = nv

    spec = pl.BlockSpec((tm, cols), lambda i: (i, 0))
    outs = pl.pallas_call(
        body, grid=(rows // tm,), name=name, in_specs=[spec] * 4, out_specs=[spec] * 3,
        out_shape=[S((rows, cols), F32)] * 3, compiler_params=_params(("arbitrary",)),
    )(*[a.reshape(rows, cols) for a in (wt, g, m, v)])
    return [o.reshape(shape) for o in outs]


WEIGHTS = ('pre_mix_norm', 'post_mix_norm', 'pre_ffn_norm', 'post_ffn_norm', 'w_in', 'w_out', 'attn_sinks', 'ssd_conv_w',
           'ssd_conv_b', 'ssd_dt_bias', 'ssd_A_log', 'ssd_D', 'ssd_norm_w', 'gdn_conv_w', 'gdn_dt_bias', 'gdn_A_log',
           'gdn_norm_w', 'ffn_w_gate', 'ffn_w_up', 'ffn_w_down')


def _chip_piece(i, raw, shape):
    if i == 0:
        g, axis = _unpad_cols(raw["w_in_pad"]), 1
    elif i == 1:
        g = raw["w_out_cat"]
        g, axis = jnp.concatenate([g[512:768], g[0:512], g[768:1024]], axis=0), 0
    elif i in (2, 3):
        g, axis = _split_gu(raw["w_gu"])[i - 2], 1
    else:
        g, axis = raw["ffn_w_down"], 0
    if axis == 1:
        width = shape[2]
        g = jnp.stack([g[:, k * width:(k + 1) * width] for k in range(N_CHIPS)])
    return g.reshape(N_CHIPS, 2, shape[1] // 2, shape[2])


def _step(x, target, wts, ms, vs):
    chip = 2 * lax.axis_index("x") + lax.axis_index("y")
    place = jnp.stack([lax.axis_index("c"), chip]).astype(jnp.int32)
    big_names = [k for k, _ in BIG]
    big_shapes = [wts[k].shape for k in big_names]
    own = lambda gathered, shard: lax.dynamic_update_index_in_dim(gathered, shard, chip, 0)
    halves = lambda a: a.reshape((2, a.shape[0] // 2) + a.shape[1:])

    shard = lambda name, l: halves(wts[name][l].astype(BF16))
    first = shard("w_in", 0)
    w_in0 = _in_weight(own(_run_exchange(_gather_exchange([first]), "gather_w_in0")[0], first))
    carried_by = {"ssd": [("w_out", 0), ("ffn_w_gate", 0)],
                  "gdn": [("ffn_w_up", 0), ("ffn_w_down", 0), ("w_in", 1), ("w_out", 1)],
                  "ffn_gu": [("ffn_w_gate", 1), ("ffn_w_up", 1)],
                  "ffn_down": [("ffn_w_down", 1)]}
    gathers = {kind: _gather_exchange([shard(*key) for key in keys]) for kind, keys in carried_by.items()}

    def matmul_weights(l, landed):
        gathered = {}
        for kind, keys in carried_by.items():
            for key, sent, got in zip(keys, gathers[kind].arrays, landed.get(kind, [])):
                if key[1] == l:
                    gathered[key[0]] = own(got, sent).reshape((N_CHIPS,) + wts[key[0]].shape[1:])
        return _matmul_weights(w_in0 if l == 0 else _in_weight(gathered["w_in"]), gathered)

    conv = _gather_small(_pack_small([wts[k] for k in CONV]), "gather_conv_weights")
    conv = [_unpack_small(conv[2 * k], [wts[n].shape for n in CONV]) for k in range(N_CHIPS)]
    w_all = dict(wts)
    for i, n in enumerate(CONV):
        w_all[n] = jnp.concatenate([conv[k][i] for k in range(N_CHIPS)], axis=2)

    early_keys = [(DEPTH - 1, 0)] + [(l, i) for l in reversed(range(DEPTH)) for i in range(1, len(BIG))]
    late_keys = [(l, 0) for l in range(DEPTH - 1)]

    def pieces_of(keys, g):
        return [_chip_piece(i, {k: g[k][l] for k in RAW_GRADS}, big_shapes[i]) for l, i in keys]

    def add_siblings(tag, pieces, siblings):
        return [_add_sibling(place, p, t, f"add_sibling_{tag}{n}") for n, (p, t) in enumerate(zip(pieces, siblings))]

    def reduce_end(tag, keys, sums, others):
        parts = [_add_chips(place, s32, o, f"add_chips_{tag}{n}") for n, ((s32, _), o) in enumerate(zip(sums, others))]
        joined = _join_halves(parts, f"join_halves_{tag}")
        return {key: q.reshape(big_shapes[key[1]][1:]) for key, q in zip(keys, joined)}

    class Reducer:
        pieces, sums = [], []

        def exchange(self, g):
            self.pieces = pieces_of(early_keys, g)
            return _halves_exchange(self.pieces)

        def scatter(self, siblings):
            self.sums = add_siblings("early", self.pieces, siblings)
            return _scatter_exchange([s16 for _, s16 in self.sums])

    reducer = Reducer()

    loss_part, grad_x, small_g, raw, early = _local_step(
        x[0], target[0], [_small_operands(w_all, l) for l in range(DEPTH)], w_in0, gathers, matmul_weights, reducer)

    reduced = reduce_end("early", early_keys, reducer.sums, early)
    late = pieces_of(late_keys, raw)
    late_sums = add_siblings("late", late, _run_exchange(_halves_exchange(late), "exchange_halves_late"))
    others = _run_exchange(_scatter_exchange([s16 for _, s16 in late_sums]), "scatter_chips_late")
    reduced.update(reduce_end("late", late_keys, late_sums, others))
    g_all = {k: jnp.stack([reduced[(l, i)] for l in range(DEPTH)]) for i, k in enumerate(big_names)}

    names = SMALL + CONV
    packed = _pack_small([small_g[k] for k in names] + [loss_part])
    small_sum = _sum_leading(_gather_small(packed, "gather_small_grads"), "add_small")
    vals = _unpack_small(small_sum, [small_g[k].shape for k in names] + [(1, LANES)])
    loss = vals[-1][0, 0]
    for k, v in zip(names, vals[:-1]):
        if k in CONV:
            width = wts[k].shape[2]
            v = lax.dynamic_slice_in_dim(v.reshape(DEPTH, 4, -1), chip * width, width, axis=2)
        g_all[k] = v.reshape(wts[k].shape)

    shapes = [wts[k].shape for k in names]
    d_s, m_s, v_s = _adamw(_pack_small([wts[k] for k in names]), _pack_small([g_all[k] for k in names]),
                           _pack_small([ms[k] for k in names]), _pack_small([vs[k] for k in names]), "adamw_small")
    upd = dict(zip(names, zip(_unpack_small(d_s, shapes), _unpack_small(m_s, shapes), _unpack_small(v_s, shapes))))
    for k in big_names:
        upd[k] = _adamw(wts[k], g_all[k], ms[k], vs[k], f"adamw_{k}")
    return (loss, grad_x[None], *[g_all[k] for k in WEIGHTS], *[upd[k][0] for k in WEIGHTS],
            *[upd[k][1] for k in WEIGHTS], *[upd[k][2] for k in WEIGHTS])


def kernel(x, pre_mix_norm, post_mix_norm, pre_ffn_norm, post_ffn_norm, w_in, w_out, attn_sinks, ssd_conv_w, ssd_conv_b, ssd_dt_bias, ssd_A_log, ssd_D, ssd_norm_w, gdn_conv_w, gdn_dt_bias, gdn_A_log, gdn_norm_w, ffn_w_gate, ffn_w_up, ffn_w_down, loss_target, m_pre_mix_norm, m_post_mix_norm, m_pre_ffn_norm, m_post_ffn_norm, m_w_in, m_w_out, m_attn_sinks, m_ssd_conv_w, m_ssd_conv_b, m_ssd_dt_bias, m_ssd_A_log, m_ssd_D, m_ssd_norm_w, m_gdn_conv_w, m_gdn_dt_bias, m_gdn_A_log, m_gdn_norm_w, m_ffn_w_gate, m_ffn_w_up, m_ffn_w_down, v_pre_mix_norm, v_post_mix_norm, v_pre_ffn_norm, v_post_ffn_norm, v_w_in, v_w_out, v_attn_sinks, v_ssd_conv_w, v_ssd_conv_b, v_ssd_dt_bias, v_ssd_A_log, v_ssd_D, v_ssd_norm_w, v_gdn_conv_w, v_gdn_dt_bias, v_gdn_A_log, v_gdn_norm_w, v_ffn_w_gate, v_ffn_w_up, v_ffn_w_down):
    wts = dict(zip(WEIGHTS, (pre_mix_norm, post_mix_norm, pre_ffn_norm, post_ffn_norm, w_in, w_out, attn_sinks, ssd_conv_w, ssd_conv_b, ssd_dt_bias, ssd_A_log, ssd_D, ssd_norm_w, gdn_conv_w, gdn_dt_bias, gdn_A_log, gdn_norm_w, ffn_w_gate, ffn_w_up, ffn_w_down)))
    ms = dict(zip(WEIGHTS, (m_pre_mix_norm, m_post_mix_norm, m_pre_ffn_norm, m_post_ffn_norm, m_w_in, m_w_out, m_attn_sinks, m_ssd_conv_w, m_ssd_conv_b, m_ssd_dt_bias, m_ssd_A_log, m_ssd_D, m_ssd_norm_w, m_gdn_conv_w, m_gdn_dt_bias, m_gdn_A_log, m_gdn_norm_w, m_ffn_w_gate, m_ffn_w_up, m_ffn_w_down)))
    vs = dict(zip(WEIGHTS, (v_pre_mix_norm, v_post_mix_norm, v_pre_ffn_norm, v_post_ffn_norm, v_w_in, v_w_out, v_attn_sinks, v_ssd_conv_w, v_ssd_conv_b, v_ssd_dt_bias, v_ssd_A_log, v_ssd_D, v_ssd_norm_w, v_gdn_conv_w, v_gdn_dt_bias, v_gdn_A_log, v_gdn_norm_w, v_ffn_w_gate, v_ffn_w_up, v_ffn_w_down)))
    return _step(x, loss_target, wts, ms, vs)
```

```python
import functools
import math

import jax
import jax.numpy as jnp
from jax import lax
from jax.experimental import pallas as pl
from jax.experimental.pallas import tpu as pltpu

F32, BF16 = jnp.float32, jnp.bfloat16
HI = lax.Precision.HIGHEST
MESH = pl.DeviceIdType.MESH
S = jax.ShapeDtypeStruct

D_MODEL = 1024
DEPTH = 2
CHUNK = 64
SSD_CHUNK = 256
GDN_CHUNK = 128
EPS = 1e-6
FF = 2816
N_CHIPS = 4
N_DEV = 8
LANES = 128

VMEM_LIMIT_BYTES = 56 * 1024 * 1024

PC_GQKV, PC_GZ, PC_XBC, PC_ATT, PC_SZ, PC_DT, PC_BA, PC_TOT = 0, 768, 1024, 2048, 2560, 3072, 3200, 3328

ADAM_LR, ADAM_B1, ADAM_B2, ADAM_EPS, ADAM_WD, ADAM_STEP = 0.001, 0.9, 0.999, 1e-08, 0.01, 10

ALIBI_SLOPES = tuple(2.0 ** (-8.0 * (h + 1) / 4) for h in range(4))


def _params(sem=None, **kw):
    if sem is not None:
        kw["dimension_semantics"] = sem
    return pltpu.CompilerParams(vmem_limit_bytes=VMEM_LIMIT_BYTES, **kw)


def _dot(a, b, prec=None):
    return jnp.dot(a, b, precision=prec, preferred_element_type=F32)


def _dot_nt(a, b, prec=None):
    return lax.dot_general(a, b, (((1,), (1,)), ((), ())), precision=prec, preferred_element_type=F32)


def _dot_tn(a, b, prec=None):
    return lax.dot_general(a, b, (((0,), (0,)), ((), ())), precision=prec, preferred_element_type=F32)


def _iota2(n, m):
    return lax.broadcasted_iota(jnp.int32, (n, m), 0), lax.broadcasted_iota(jnp.int32, (n, m), 1)


def _pick_col(arr, idx):
    ci = lax.broadcasted_iota(jnp.int32, arr.shape, 1)
    return jnp.sum(jnp.where(ci == idx, arr, 0.0), axis=1, keepdims=True)


def _pick_row(arr, idx):
    ri = lax.broadcasted_iota(jnp.int32, arr.shape, 0)
    return jnp.sum(jnp.where(ri == idx, arr, 0.0), axis=0, keepdims=True)


def _col_to_row(col, eye):
    return jnp.sum(eye * col, axis=0, keepdims=True)


def _rms(x, w):
    return x * lax.rsqrt(jnp.mean(x * x, axis=-1, keepdims=True) + EPS) * w


def _mm_nn(a, b, tm, tn, out_dtype, name, exchange=None):
    m, k = a.shape
    n = b.shape[1]
    tm, tn = min(tm, m), min(tn, n)
    grid = (n // tn, m // tm)

    def body(a_ref, b_ref, o_ref):
        o_ref[...] = _dot(a_ref[...], b_ref[...]).astype(o_ref.dtype)

    body, x_in, x_out, x_shape, x_sems = _hosted(exchange, 2, 1, grid, body)
    outs = pl.pallas_call(
        body, grid=grid, name=name,
        in_specs=[pl.BlockSpec((tm, k), lambda j, i: (i, 0)), pl.BlockSpec((k, tn), lambda j, i: (0, j))] + x_in,
        out_specs=[pl.BlockSpec((tm, tn), lambda j, i: (i, j))] + x_out,
        out_shape=[S((m, n), out_dtype)] + x_shape, scratch_shapes=x_sems,
        compiler_params=_params(("arbitrary", "arbitrary")),
    )(a, b, *([] if exchange is None else exchange.arrays))
    return outs[0] if exchange is None else outs


def _mm_nt(a, b, tm, tn, out_dtype, name):
    m, k = a.shape
    n = b.shape[0]
    tm, tn = min(tm, m), min(tn, n)

    def body(a_ref, b_ref, o_ref):
        o_ref[...] = _dot_nt(a_ref[...], b_ref[...]).astype(o_ref.dtype)

    return pl.pallas_call(
        body, grid=(n // tn, m // tm), name=name,
        in_specs=[pl.BlockSpec((tm, k), lambda j, i: (i, 0)), pl.BlockSpec((tn, k), lambda j, i: (j, 0))],
        out_specs=pl.BlockSpec((tm, tn), lambda j, i: (i, j)),
        out_shape=S((m, n), out_dtype), compiler_params=_params(("arbitrary", "arbitrary")),
    )(a, b)


def _mm_tn(a, b, tm, tn, tk, name):
    t, m = a.shape
    n = b.shape[1]
    tm, tn, tk = min(tm, m), min(tn, n), min(tk, t)

    def body(a_ref, b_ref, o_ref):
        part = _dot_tn(a_ref[...], b_ref[...])

        @pl.when(pl.program_id(2) == 0)
        def _():
            o_ref[...] = part

        @pl.when(pl.program_id(2) > 0)
        def _():
            o_ref[...] += part

    return pl.pallas_call(
        body, grid=(m // tm, n // tn, t // tk), name=name,
        in_specs=[pl.BlockSpec((tk, tm), lambda i, j, k: (k, i)), pl.BlockSpec((tk, tn), lambda i, j, k: (k, j))],
        out_specs=pl.BlockSpec((tm, tn), lambda i, j, k: (i, j)),
        out_shape=S((m, n), F32), compiler_params=_params(("arbitrary", "arbitrary", "arbitrary")),
    )(a, b)


def _rowcall(fn, rows, params, row_outs, acc_outs, name, tm=512):
    t = rows[0].shape[0]
    tm = min(tm, t)
    n_in = len(rows) + len(params)
    n_ro = len(row_outs)

    def body(*refs):
        ro, ao = fn(*[r[...] for r in refs[:n_in]])
        for ref, v in zip(refs[n_in:n_in + n_ro], ro):
            ref[...] = v.astype(ref.dtype)
        acc_refs = refs[n_in + n_ro:]
        if acc_refs:
            @pl.when(pl.program_id(0) == 0)
            def _():
                for ref, v in zip(acc_refs, ao):
                    ref[...] = v

            @pl.when(pl.program_id(0) > 0)
            def _():
                for ref, v in zip(acc_refs, ao):
                    ref[...] += v

    in_specs = [pl.BlockSpec((tm, r.shape[1]), lambda i: (i, 0)) for r in rows]
    in_specs += [pl.BlockSpec(p.shape, lambda i: (0, 0)) for p in params]
    out_specs = [pl.BlockSpec((tm, c), lambda i: (i, 0)) for c, _ in row_outs]
    out_specs += [pl.BlockSpec(shape, lambda i: (0, 0)) for shape in acc_outs]
    out_shape = [S((t, c), dt) for c, dt in row_outs] + [S(shape, F32) for shape in acc_outs]
    return pl.pallas_call(
        body, grid=(t // tm,), name=name, in_specs=in_specs, out_specs=out_specs, out_shape=out_shape,
        compiler_params=_params(("arbitrary",)),
    )(*rows, *params)


def _prenorm(x, w, name):
    def fn(x, w):
        return (_rms(x, w),), ()
    return _rowcall(fn, [x], [w], [(D_MODEL, BF16)], [], name)[0]


def _resid_norm(xin, m, w_post, w_next, name):
    def fn(xin, m, w_post, w_next):
        xo = xin + _rms(m, w_post)
        return (xo, _rms(xo, w_next)), ()
    return _rowcall(fn, [xin, m], [w_post, w_next], [(D_MODEL, F32), (D_MODEL, BF16)], [], name)


def _resid_loss(xin, m, w_post, target, name):
    def fn(xin, m, target, w_post):
        r, vjp = jax.vjp(_rms, m, w_post)
        err = xin + r - target
        dy = err * (1.0 / D_MODEL)
        dm, dw = vjp(dy)
        tot = jnp.sum(jnp.sum(err * err, axis=1, keepdims=True), axis=0, keepdims=True) * (0.5 / D_MODEL)
        lane = lax.broadcasted_iota(jnp.int32, (1, LANES), 1)
        return (dy, dm), (jnp.where(lane == 0, tot, 0.0), dw)
    return _rowcall(fn, [xin, m, target], [w_post], [(D_MODEL, F32), (D_MODEL, BF16)],
                    [(1, LANES), (1, D_MODEL)], name)


def _resid_norm_bwd(x_out, m, d_direct, dh, w_post, w_next, name):
    def fn(x_out, m, d_direct, dh, w_post, w_next):
        _, vjp_n = jax.vjp(_rms, x_out, w_next)
        dx, dwn = vjp_n(dh)
        d_total = d_direct + dx
        _, vjp_p = jax.vjp(_rms, m, w_post)
        dm, dwp = vjp_p(d_total)
        return (d_total, dm), (dwn, dwp)
    return _rowcall(fn, [x_out, m, d_direct, dh], [w_post, w_next], [(D_MODEL, F32), (D_MODEL, BF16)],
                    [(1, D_MODEL), (1, D_MODEL)], name)


def _prenorm_bwd(x, d_direct, dh, w, name):
    def fn(x, d_direct, dh, w):
        _, vjp = jax.vjp(_rms, x, w)
        dx, dw = vjp(dh)
        return (d_direct + dx,), (dw,)
    return _rowcall(fn, [x, d_direct, dh], [w], [(D_MODEL, F32)], [(1, D_MODEL)], name)


FF_HALF = FF // 2


def _interleave_gu(gate_t, up_t):
    return jnp.concatenate([gate_t[:FF_HALF], up_t[:FF_HALF], gate_t[FF_HALF:], up_t[FF_HALF:]], axis=0)


def _split_gu(gu_t):
    return (jnp.concatenate([gu_t[:FF_HALF], gu_t[FF:FF + FF_HALF]], axis=0),
            jnp.concatenate([gu_t[FF_HALF:FF], gu_t[FF + FF_HALF:]], axis=0))


def _swiglu_pair(gu):
    n = gu.shape[1] // 2
    return jax.nn.silu(gu[:, :n]) * gu[:, n:]


def _ffn_up(h2, w_gu, name, tm=512, exchange=None):
    t, k = h2.shape
    tm = min(tm, t)
    grid = (2, t // tm)

    def body(a_ref, b_ref, gu_ref, act_ref):
        gu = _dot_nt(a_ref[...], b_ref[...])
        gu_ref[...] = gu.astype(gu_ref.dtype)
        act_ref[...] = _swiglu_pair(gu).astype(act_ref.dtype)

    body, x_in, x_out, x_shape, x_sems = _hosted(exchange, 2, 2, grid, body)
    return pl.pallas_call(
        body, grid=grid, name=name,
        in_specs=[pl.BlockSpec((tm, k), lambda j, i: (i, 0)), pl.BlockSpec((FF, k), lambda j, i: (j, 0))] + x_in,
        out_specs=[pl.BlockSpec((tm, FF), lambda j, i: (i, j)), pl.BlockSpec((tm, FF_HALF), lambda j, i: (i, j))] + x_out,
        out_shape=[S((t, 2 * FF), BF16), S((t, FF), BF16)] + x_shape, scratch_shapes=x_sems,
        compiler_params=_params(("arbitrary", "arbitrary")),
    )(h2, w_gu, *([] if exchange is None else exchange.arrays))


def _ffn_down_bwd(d_f, w_down, gu, name, tm=512):
    t, k = d_f.shape
    tm = min(tm, t)

    def body(a_ref, b_ref, gu_ref, o_ref):
        d_act = _dot_nt(a_ref[...], b_ref[...])
        _, vjp = jax.vjp(_swiglu_pair, gu_ref[...].astype(F32))
        o_ref[...] = vjp(d_act)[0].astype(o_ref.dtype)

    return pl.pallas_call(
        body, grid=(2, t // tm), name=name,
        in_specs=[pl.BlockSpec((tm, k), lambda j, i: (i, 0)), pl.BlockSpec((FF_HALF, k), lambda j, i: (j, 0)),
                  pl.BlockSpec((tm, FF), lambda j, i: (i, j))],
        out_specs=pl.BlockSpec((tm, FF), lambda j, i: (i, j)),
        out_shape=S((t, 2 * FF), BF16), compiler_params=_params(("arbitrary", "arbitrary")),
    )(d_f, w_down, gu)


def _conv_fwd(proj, col0, width, w, b, name, tm=512):
    t = proj.shape[0]
    tm = min(tm, t)
    cb = col0 // width

    def body(x_ref, w_ref, b_ref, o_ref, ext):
        @pl.when(pl.program_id(0) == 0)
        def _():
            ext[0:8, :] = jnp.zeros((8, width), F32)

        ext[8:8 + tm, :] = x_ref[...]
        y = b_ref[...] + w_ref[0:1, :] * ext[pl.ds(5, tm), :]
        for k in range(1, 4):
            y = y + w_ref[k:k + 1, :] * ext[pl.ds(5 + k, tm), :]
        o_ref[...] = jax.nn.silu(y)
        ext[0:8, :] = ext[tm:tm + 8, :]

    return pl.pallas_call(
        body, grid=(t // tm,), name=name,
        in_specs=[pl.BlockSpec((tm, width), lambda i: (i, cb)), pl.BlockSpec((4, width), lambda i: (0, 0)),
                  pl.BlockSpec((1, width), lambda i: (0, 0))],
        out_specs=pl.BlockSpec((tm, width), lambda i: (i, 0)),
        out_shape=S((t, width), F32), scratch_shapes=[pltpu.VMEM((tm + 8, width), F32)],
        compiler_params=_params(("arbitrary",)),
    )(proj, w, b)


def _conv_bwd(proj, col0, width, w, b, dact, name, tm=512):
    t = proj.shape[0]
    tm = min(tm, t)
    nb = t // tm
    cb = col0 // width
    hb = tm // 8

    def body(x_ref, halo_ref, d_ref, w_ref, b_ref, dx_ref, dw_ref, db_ref, extx, extd):
        i = pl.program_id(0)
        blk = nb - 1 - i

        @pl.when(i == 0)
        def _():
            extd[tm:tm + 8, :] = jnp.zeros((8, width), F32)
            dw_ref[...] = jnp.zeros((4, width), F32)
            db_ref[...] = jnp.zeros((1, width), F32)

        extx[0:8, :] = jnp.where(blk == 0, 0.0, halo_ref[...])
        extx[8:8 + tm, :] = x_ref[...]
        y = b_ref[...] + w_ref[0:1, :] * extx[pl.ds(5, tm), :]
        for k in range(1, 4):
            y = y + w_ref[k:k + 1, :] * extx[pl.ds(5 + k, tm), :]
        sig = jax.nn.sigmoid(y)
        dy = d_ref[...] * (sig * (1.0 + y * (1.0 - sig)))
        extd[0:tm, :] = dy
        dx = w_ref[0:1, :] * extd[pl.ds(3, tm), :]
        for k in range(1, 4):
            dx = dx + w_ref[k:k + 1, :] * extd[pl.ds(3 - k, tm), :]
        dx_ref[...] = dx.astype(dx_ref.dtype)
        for k in range(4):
            dw_ref[k:k + 1, :] += jnp.sum(dy * extx[pl.ds(5 + k, tm), :], axis=0, keepdims=True)
        db_ref[...] += jnp.sum(dy, axis=0, keepdims=True)
        extd[tm:tm + 8, :] = extd[0:8, :]

    return pl.pallas_call(
        body, grid=(nb,), name=name,
        in_specs=[pl.BlockSpec((tm, width), lambda i: (nb - 1 - i, cb)),
                  pl.BlockSpec((8, width), lambda i: (jnp.maximum((nb - 1 - i) * hb - 1, 0), cb)),
                  pl.BlockSpec((tm, width), lambda i: (nb - 1 - i, 0)),
                  pl.BlockSpec((4, width), lambda i: (0, 0)), pl.BlockSpec((1, width), lambda i: (0, 0))],
        out_specs=[pl.BlockSpec((tm, width), lambda i: (nb - 1 - i, 0)), pl.BlockSpec((4, width), lambda i: (0, 0)),
                   pl.BlockSpec((1, width), lambda i: (0, 0))],
        out_shape=[S((t, width), BF16), S((4, width), F32), S((1, width), F32)],
        scratch_shapes=[pltpu.VMEM((tm + 8, width), F32), pltpu.VMEM((tm + 8, width), F32)],
        compiler_params=_params(("arbitrary",)),
    )(proj, proj, dact, w, b)


SWA_BQ = 256
SWA_BACK = 128


def _swa_block(q, kw, vw, sinks, blk):
    nq, nk = SWA_BQ, SWA_BQ + SWA_BACK
    r, j = _iota2(nq, nk)
    rel = r // CHUNK + 2 - j // CHUNK
    valid = (rel >= 0) & (rel <= 2) & (blk * (SWA_BQ // CHUNK) + j // CHUNK - 2 >= 0)
    dist = jnp.abs(r + SWA_BACK - j).astype(F32)
    outs = []
    for h in range(4):
        kv = h // 2
        qh = q[:, 64 * h:64 * h + 64]
        kh = kw[:, 64 * kv:64 * kv + 64]
        vh = vw[:, 64 * kv:64 * kv + 64]
        s = _dot_nt(qh, kh) * 0.125 - ALIBI_SLOPES[h] * dist
        s = jnp.where(valid, s, -1e30)
        sink = _pick_col(sinks, h)
        m = jnp.maximum(jnp.max(s, axis=1, keepdims=True), sink)
        e = jnp.exp(s - m)
        den = jnp.sum(e, axis=1, keepdims=True) + jnp.exp(sink - m)
        outs.append(_dot(e / den, vh))
    return jnp.concatenate(outs, axis=1)


def _swa_fwd(proj, sinks, name):
    t = proj.shape[0]
    qb, kb = PC_ATT // 256, PC_ATT // 128 + 2
    win = SWA_BQ + SWA_BACK

    def body(q_ref, k_ref, v_ref, s_ref, o_ref, kp, vp):
        i = pl.program_id(0)

        @pl.when(i == 0)
        def _():
            kp[0:SWA_BACK, :] = jnp.zeros((SWA_BACK, 128), F32)
            vp[0:SWA_BACK, :] = jnp.zeros((SWA_BACK, 128), F32)
            kp[SWA_BACK:, :] = k_ref[...]
            vp[SWA_BACK:, :] = v_ref[...]

        start = pl.multiple_of(i * SWA_BQ, SWA_BQ)
        o = _swa_block(q_ref[...], kp[pl.ds(start, win), :], vp[pl.ds(start, win), :], s_ref[...], i)
        o_ref[...] = o.astype(o_ref.dtype)

    return pl.pallas_call(
        body, grid=(t // SWA_BQ,), name=name,
        in_specs=[pl.BlockSpec((SWA_BQ, 256), lambda i: (i, qb)), pl.BlockSpec((t, 128), lambda i: (0, kb)),
                  pl.BlockSpec((t, 128), lambda i: (0, kb + 1)), pl.BlockSpec((1, LANES), lambda i: (0, 0))],
        out_specs=pl.BlockSpec((SWA_BQ, 256), lambda i: (i, 0)),
        out_shape=S((t, 256), BF16),
        scratch_shapes=[pltpu.VMEM((t + SWA_BACK, 128), F32), pltpu.VMEM((t + SWA_BACK, 128), F32)],
        compiler_params=_params(("arbitrary",)),
    )(proj, proj, proj, sinks)


def _swa_bwd(proj, sinks, dcat, dcol0, name):
    t = proj.shape[0]
    nb = t // SWA_BQ
    qb, kb = PC_ATT // 256, PC_ATT // 128 + 2
    db = dcol0 // 256
    win = SWA_BQ + SWA_BACK

    def body(q_ref, k_ref, v_ref, s_ref, do_ref, dq_ref, dk_ref, dv_ref, ds_ref, kp, vp, dkp, dvp):
        i = pl.program_id(0)

        @pl.when(i == 0)
        def _():
            kp[0:SWA_BACK, :] = jnp.zeros((SWA_BACK, 128), F32)
            vp[0:SWA_BACK, :] = jnp.zeros((SWA_BACK, 128), F32)
            kp[SWA_BACK:, :] = k_ref[...]
            vp[SWA_BACK:, :] = v_ref[...]
            dkp[...] = jnp.zeros_like(dkp)
            dvp[...] = jnp.zeros_like(dvp)
            ds_ref[...] = jnp.zeros_like(ds_ref)

        start = pl.multiple_of(i * SWA_BQ, SWA_BQ)
        _, vjp = jax.vjp(functools.partial(_swa_block, blk=i), q_ref[...], kp[pl.ds(start, win), :],
                         vp[pl.ds(start, win), :], s_ref[...])
        dq, dkw, dvw, dsk = vjp(do_ref[...])
        dq_ref[...] = dq.astype(dq_ref.dtype)
        dkp[pl.ds(start, win), :] += dkw
        dvp[pl.ds(start, win), :] += dvw
        ds_ref[...] += dsk

        @pl.when(i == nb - 1)
        def _():
            dk_ref[...] = dkp[SWA_BACK:, :].astype(dk_ref.dtype)
            dv_ref[...] = dvp[SWA_BACK:, :].astype(dv_ref.dtype)

    return pl.pallas_call(
        body, grid=(nb,), name=name,
        in_specs=[pl.BlockSpec((SWA_BQ, 256), lambda i: (i, qb)), pl.BlockSpec((t, 128), lambda i: (0, kb)),
                  pl.BlockSpec((t, 128), lambda i: (0, kb + 1)), pl.BlockSpec((1, LANES), lambda i: (0, 0)),
                  pl.BlockSpec((SWA_BQ, 256), lambda i: (i, db))],
        out_specs=[pl.BlockSpec((SWA_BQ, 256), lambda i: (i, 0)), pl.BlockSpec((t, 128), lambda i: (0, 0)),
                   pl.BlockSpec((t, 128), lambda i: (0, 0)), pl.BlockSpec((1, LANES), lambda i: (0, 0))],
        out_shape=[S((t, 256), BF16), S((t, 128), BF16), S((t, 128), BF16), S((1, LANES), F32)],
        scratch_shapes=[pltpu.VMEM((t + SWA_BACK, 128), F32) for _ in range(4)],
        compiler_params=_params(("arbitrary",)),
    )(proj, proj, proj, sinks, dcat)


def _ssd_chunk(z, xbc, dt_raw, state, dtb, alog, dsk, nw):
    n = z.shape[0]
    r, c = _iota2(n, n)
    tril = r >= c
    eye = (r == c).astype(F32)
    dt = jax.nn.softplus(dt_raw + dtb)
    acs = _dot(tril.astype(F32), dt * (-jnp.exp(alog)), HI)
    xs, bm, cm = xbc[:, :512], xbc[:, 512:768], xbc[:, 768:1024]
    heads = range(8)
    bg = [bm[:, 128 * g:128 * g + 128] for g in range(2)]
    cg = [cm[:, 128 * g:128 * g + 128] for g in range(2)]
    cb = [_dot_nt(cg[g], bg[g]) for g in range(2)]
    dth = [_pick_col(dt, h) for h in heads]
    acol = [_pick_col(acs, h) for h in heads]
    arow = [_col_to_row(a, eye) for a in acol]
    lmat = [jnp.where(tril, jnp.exp(jnp.where(tril, a - b, 0.0)), 0.0) for a, b in zip(acol, arow)]
    xh = [xs[:, 64 * h:64 * h + 64] for h in heads]
    xc = [x * t for x, t in zip(xh, dth)]
    st = [state[64 * h:64 * h + 64, :] for h in heads]
    alast = [_pick_row(a, n - 1) for a in acol]
    y_in = [_dot(cb[h // 4] * lmat[h], xc[h]) for h in heads]
    y_st = [_dot_nt(cg[h // 4], st[h]) * jnp.exp(acol[h]) for h in heads]
    ys = [y_in[h] + y_st[h] + xh[h] * _pick_col(dsk, h) for h in heads]
    new_states = [st[h] * jnp.exp(alast[h]) + _dot_tn(xc[h] * jnp.exp(alast[h] - acol[h]), bg[h // 4]) for h in heads]
    gg = jnp.concatenate(ys, axis=1) * jax.nn.silu(z)
    outs = []
    for gi in range(2):
        gv = gg[:, 256 * gi:256 * gi + 256]
        outs.append(gv * lax.rsqrt(jnp.mean(gv * gv, axis=-1, keepdims=True) + EPS))
    return jnp.concatenate(outs, axis=1) * nw, jnp.concatenate(new_states, axis=0)


def _ssd_fwd(proj, xbc, dtb, alog, dsk, nw, name, exchange=None):
    t = proj.shape[0]
    CHUNK = min(SSD_CHUNK, t)
    nc = t // CHUNK

    def body(z_ref, x_ref, dt_ref, dtb_ref, al_ref, d_ref, nw_ref, o_ref, st_ref, state):
        @pl.when(pl.program_id(0) == 0)
        def _():
            state[...] = jnp.zeros_like(state)

        st_ref[0] = state[...]
        o, ns = _ssd_chunk(z_ref[...], x_ref[...], dt_ref[...], state[...], dtb_ref[...], al_ref[...], d_ref[...],
                           nw_ref[...])
        o_ref[...] = o.astype(o_ref.dtype)
        state[...] = ns

    body, x_in, x_out, x_shape, x_sems = _hosted(exchange, 7, 2, nc, body)
    vec = pl.BlockSpec((1, LANES), lambda i: (0, 0))
    return pl.pallas_call(
        body, grid=(nc,), name=name,
        in_specs=[pl.BlockSpec((CHUNK, 512), lambda i: (i, PC_SZ // 512)), pl.BlockSpec((CHUNK, 1024), lambda i: (i, 0)),
                  pl.BlockSpec((CHUNK, 128), lambda i: (i, PC_DT // 128)), vec, vec, vec,
                  pl.BlockSpec((1, 512), lambda i: (0, 0))] + x_in,
        out_specs=[pl.BlockSpec((CHUNK, 512), lambda i: (i, 0)), pl.BlockSpec((1, 512, 128), lambda i: (i, 0, 0))] + x_out,
        out_shape=[S((t, 512), BF16), S((nc, 512, 128), F32)] + x_shape,
        scratch_shapes=[pltpu.VMEM((512, 128), F32)] + x_sems,
        compiler_params=_params(("arbitrary",)),
    )(proj, xbc, proj, dtb, alog, dsk, nw, *([] if exchange is None else exchange.arrays))


def _ssd_bwd(proj, xbc, states, dtb, alog, dsk, nw, dcat, dcol0, name, exchange=None):
    t = proj.shape[0]
    CHUNK = min(SSD_CHUNK, t)
    nc = t // CHUNK
    db = dcol0 // 512

    def body(z_ref, x_ref, dt_ref, st_ref, dtb_ref, al_ref, d_ref, nw_ref, do_ref,
             dz_ref, dx_ref, ddt_ref, gdtb_ref, gal_ref, gd_ref, gnw_ref, dstate):
        @pl.when(pl.program_id(0) == 0)
        def _():
            dstate[...] = jnp.zeros_like(dstate)
            gdtb_ref[...] = jnp.zeros_like(gdtb_ref)
            gal_ref[...] = jnp.zeros_like(gal_ref)
            gd_ref[...] = jnp.zeros_like(gd_ref)
            gnw_ref[...] = jnp.zeros_like(gnw_ref)

        _, vjp = jax.vjp(_ssd_chunk, z_ref[...], x_ref[...], dt_ref[...], st_ref[0], dtb_ref[...], al_ref[...],
                         d_ref[...], nw_ref[...])
        dz, dx, ddt, dst, gdtb, gal, gd, gnw = vjp((do_ref[...], dstate[...]))
        dz_ref[...] = dz.astype(dz_ref.dtype)
        dx_ref[...] = dx
        ddt_ref[...] = ddt.astype(ddt_ref.dtype)
        dstate[...] = dst
        gdtb_ref[...] += gdtb
        gal_ref[...] += gal
        gd_ref[...] += gd
        gnw_ref[...] += gnw

    body, x_in, x_out, x_shape, x_sems = _hosted(exchange, 9, 7, nc, body)
    rev = lambda i: nc - 1 - i
    vec = pl.BlockSpec((1, LANES), lambda i: (0, 0))
    vec512 = pl.BlockSpec((1, 512), lambda i: (0, 0))
    return pl.pallas_call(
        body, grid=(nc,), name=name,
        in_specs=[pl.BlockSpec((CHUNK, 512), lambda i: (rev(i), PC_SZ // 512)),
                  pl.BlockSpec((CHUNK, 1024), lambda i: (rev(i), 0)),
                  pl.BlockSpec((CHUNK, 128), lambda i: (rev(i), PC_DT // 128)),
                  pl.BlockSpec((1, 512, 128), lambda i: (rev(i), 0, 0)), vec, vec, vec, vec512,
                  pl.BlockSpec((CHUNK, 512), lambda i: (rev(i), db))] + x_in,
        out_specs=[pl.BlockSpec((CHUNK, 512), lambda i: (rev(i), 0)), pl.BlockSpec((CHUNK, 1024), lambda i: (rev(i), 0)),
                   pl.BlockSpec((CHUNK, 128), lambda i: (rev(i), 0)), vec, vec, vec, vec512] + x_out,
        out_shape=[S((t, 512), BF16), S((t, 1024), F32), S((t, 128), BF16), S((1, LANES), F32), S((1, LANES), F32),
                   S((1, LANES), F32), S((1, 512), F32)] + x_shape,
        scratch_shapes=[pltpu.VMEM((512, 128), F32)] + x_sems,
        compiler_params=_params(("arbitrary",)),
    )(proj, xbc, proj, states, dtb, alog, dsk, nw, dcat, *([] if exchange is None else exchange.arrays))


SOLVE_PREC = lax.Precision.HIGH


def _unit_lower_inverses(nas, known=None):
    def compute(ns):
        if known is not None:
            return tuple(known)
        n = ns[0].shape[0]
        r, c = _iota2(n, n)
        eye = (r == c).astype(F32)
        tm, pw = [eye + a for a in ns], list(ns)
        for _ in range(n.bit_length() - 2):
            pw = [_dot(p, p, SOLVE_PREC) for p in pw]
            tm = [t + _dot(t, p, SOLVE_PREC) for t, p in zip(tm, pw)]
        return tuple(tm)

    inv = jax.custom_vjp(compute)

    def fwd(ns):
        ts = compute(ns)
        return ts, ts

    def bwd(ts, gs):
        part = [_dot_nt(g, t, SOLVE_PREC) for g, t in zip(gs, ts)]
        return (tuple(_dot_tn(t, p, SOLVE_PREC) for t, p in zip(ts, part)),)

    inv.defvjp(fwd, bwd)
    return inv(nas)


def _gdn_chunk(qkv, z, ba, state, dtb, alog, nw, known_inverses=None):
    n = qkv.shape[0]
    r, c = _iota2(n, n)
    tril = r >= c
    stril = r > c
    eye = (r == c).astype(F32)
    beta_all = jax.nn.sigmoid(ba)
    gcs = _dot(tril.astype(F32), -jnp.exp(alog) * jax.nn.softplus(ba + dtb), HI)
    heads = range(4)
    qh = [qkv[:, 64 * h:64 * h + 64] for h in heads]
    kh = [qkv[:, 256 + 64 * h:256 + 64 * h + 64] for h in heads]
    vh = [qkv[:, 512 + 64 * h:512 + 64 * h + 64] for h in heads]
    qn = [q * lax.rsqrt(jnp.sum(q * q, axis=-1, keepdims=True) + EPS) * 0.125 for q in qh]
    kn = [k * lax.rsqrt(jnp.sum(k * k, axis=-1, keepdims=True) + EPS) for k in kh]
    beta = [_pick_col(beta_all, h) for h in heads]
    gcol = [_pick_col(gcs, 4 + h) for h in heads]
    grow = [_col_to_row(g, eye) for g in gcol]
    decay = [jnp.where(tril, jnp.exp(jnp.where(tril, gc - gr, 0.0)), 0.0) for gc, gr in zip(gcol, grow)]
    kbeta = [k * b for k, b in zip(kn, beta)]
    kk = [_dot_nt(kb, k) for kb, k in zip(kbeta, kn)]
    qk = [_dot_nt(q, k) * dc for q, k, dc in zip(qn, kn, decay)]
    known = None if known_inverses is None else [known_inverses[n * h:n * (h + 1), :] for h in heads]
    tms = _unit_lower_inverses(tuple(-jnp.where(stril, x * dc, 0.0) for x, dc in zip(kk, decay)), known)
    rhs = [jnp.concatenate([v * b, kb * jnp.exp(g)], axis=1) for v, b, kb, g in zip(vh, beta, kbeta, gcol)]
    sol = [_dot(t, x, SOLVE_PREC) for t, x in zip(tms, rhs)]
    st = [state[64 * h:64 * h + 64, :] for h in heads]
    v_new = [s_[:, :64] - _dot(s_[:, 64:], s) for s_, s in zip(sol, st)]
    o = [_dot(q * jnp.exp(g), s) + _dot(x, vn) for q, g, s, x, vn in zip(qn, gcol, st, qk, v_new)]
    glast = [_pick_row(g, n - 1) for g in gcol]
    new_states = [s * jnp.exp(gl) + _dot_tn(k * jnp.exp(gl - g), vn)
                  for s, gl, k, g, vn in zip(st, glast, kn, gcol, v_new)]
    o = [x * lax.rsqrt(jnp.mean(x * x, axis=-1, keepdims=True) + EPS) * nw for x in o]
    outs = [x * jax.nn.silu(z[:, 64 * h:64 * h + 64]) for h, x in zip(heads, o)]
    return jnp.concatenate(outs, axis=1), jnp.concatenate(new_states, axis=0), jnp.concatenate(tms, axis=0)


class _Exchange:
    def __init__(self, arrays, out_shape, n_sems, start, finish):
        self.arrays, self.out_shape, self.n_sems, self.start, self.finish = arrays, out_shape, n_sems, start, finish


def _hosted(exchange, n_in, n_out, grid, body):
    if exchange is None:
        return body, [], [], [], []
    k_in, k_out = len(exchange.arrays), len(exchange.out_shape)
    grid = (grid,) if isinstance(grid, int) else tuple(grid)

    def hosted_body(*refs):
        ins, refs = refs[:n_in + k_in], refs[n_in + k_in:]
        outs, scratch = refs[:n_out + k_out], refs[n_out + k_out:]
        sems = scratch[-2:]
        first, last = True, True
        for axis, steps in enumerate(grid):
            first = first & (pl.program_id(axis) == 0)
            last = last & (pl.program_id(axis) == steps - 1)

        @pl.when(first)
        def _():
            exchange.start(ins[n_in:], outs[n_out:], sems)

        body(*ins[:n_in], *outs[:n_out], *scratch[:-2])

        @pl.when(last)
        def _():
            exchange.finish(ins[n_in:], outs[n_out:], sems)

    return hosted_body, [ANY] * k_in, [ANY] * k_out, list(exchange.out_shape), _sem_pairs(exchange.n_sems)


def _gdn_fwd(proj, qkv, dtb, alog, nw, name, exchange=None):
    t = proj.shape[0]
    CHUNK = min(GDN_CHUNK, t)
    nc = t // CHUNK

    def body(q_ref, z_ref, ba_ref, dtb_ref, al_ref, nw_ref, o_ref, st_ref, inv_ref, state):
        @pl.when(pl.program_id(0) == 0)
        def _():
            state[...] = jnp.zeros_like(state)

        st_ref[0] = state[...]
        o, ns, tms = _gdn_chunk(q_ref[...], z_ref[...], ba_ref[...], state[...], dtb_ref[...], al_ref[...], nw_ref[...])
        o_ref[...] = o.astype(o_ref.dtype)
        inv_ref[0] = tms
        state[...] = ns

    body, x_in, x_out, x_shape, x_sems = _hosted(exchange, 6, 3, nc, body)
    vec = pl.BlockSpec((1, LANES), lambda i: (0, 0))
    per_chunk = pl.BlockSpec((1, 256, 64), lambda i: (i, 0, 0))
    return pl.pallas_call(
        body, grid=(nc,), name=name,
        in_specs=[pl.BlockSpec((CHUNK, 768), lambda i: (i, 0)), pl.BlockSpec((CHUNK, 256), lambda i: (i, PC_GZ // 256)),
                  pl.BlockSpec((CHUNK, 128), lambda i: (i, PC_BA // 128)), vec, vec, pl.BlockSpec((1, 64), lambda i: (0, 0))]
        + x_in,
        out_specs=[pl.BlockSpec((CHUNK, 256), lambda i: (i, 0)), per_chunk,
                   pl.BlockSpec((1, 4 * CHUNK, CHUNK), lambda i: (i, 0, 0))] + x_out,
        out_shape=[S((t, 256), BF16), S((nc, 256, 64), F32), S((nc, 4 * CHUNK, CHUNK), F32)] + x_shape,
        scratch_shapes=[pltpu.VMEM((256, 64), F32)] + x_sems,
        compiler_params=_params(("arbitrary",)),
    )(qkv, proj, proj, dtb, alog, nw, *([] if exchange is None else exchange.arrays))


def _gdn_bwd(proj, qkv, states, inverses, dtb, alog, nw, dcat, dcol0, name, exchange=None):
    t = proj.shape[0]
    CHUNK = min(GDN_CHUNK, t)
    nc = t // CHUNK
    db = dcol0 // 256

    def body(q_ref, z_ref, ba_ref, st_ref, inv_ref, dtb_ref, al_ref, nw_ref, do_ref,
             dq_ref, dz_ref, dba_ref, gdtb_ref, gal_ref, gnw_ref, dstate):
        @pl.when(pl.program_id(0) == 0)
        def _():
            dstate[...] = jnp.zeros_like(dstate)
            gdtb_ref[...] = jnp.zeros_like(gdtb_ref)
            gal_ref[...] = jnp.zeros_like(gal_ref)
            gnw_ref[...] = jnp.zeros_like(gnw_ref)

        def chunk(*operands):
            return _gdn_chunk(*operands, known_inverses=inv_ref[0])[:2]

        _, vjp = jax.vjp(chunk, q_ref[...], z_ref[...], ba_ref[...], st_ref[0], dtb_ref[...], al_ref[...], nw_ref[...])
        dq, dz, dba, dst, gdtb, gal, gnw = vjp((do_ref[...], dstate[...]))
        dq_ref[...] = dq
        dz_ref[...] = dz.astype(dz_ref.dtype)
        dba_ref[...] = dba.astype(dba_ref.dtype)
        dstate[...] = dst
        gdtb_ref[...] += gdtb
        gal_ref[...] += gal
        gnw_ref[...] += gnw

    body, x_in, x_out, x_shape, x_sems = _hosted(exchange, 9, 6, nc, body)
    rev = lambda i: nc - 1 - i
    vec = pl.BlockSpec((1, LANES), lambda i: (0, 0))
    vec64 = pl.BlockSpec((1, 64), lambda i: (0, 0))
    per_chunk = pl.BlockSpec((1, 256, 64), lambda i: (rev(i), 0, 0))
    return pl.pallas_call(
        body, grid=(nc,), name=name,
        in_specs=[pl.BlockSpec((CHUNK, 768), lambda i: (rev(i), 0)),
                  pl.BlockSpec((CHUNK, 256), lambda i: (rev(i), PC_GZ // 256)),
                  pl.BlockSpec((CHUNK, 128), lambda i: (rev(i), PC_BA // 128)),
                  per_chunk, pl.BlockSpec((1, 4 * CHUNK, CHUNK), lambda i: (rev(i), 0, 0)), vec, vec, vec64,
                  pl.BlockSpec((CHUNK, 256), lambda i: (rev(i), db))] + x_in,
        out_specs=[pl.BlockSpec((CHUNK, 768), lambda i: (rev(i), 0)), pl.BlockSpec((CHUNK, 256), lambda i: (rev(i), 0)),
                   pl.BlockSpec((CHUNK, 128), lambda i: (rev(i), 0)), vec, vec, vec64] + x_out,
        out_shape=[S((t, 768), F32), S((t, 256), BF16), S((t, 128), BF16), S((1, LANES), F32), S((1, LANES), F32),
                   S((1, 64), F32)] + x_shape,
        scratch_shapes=[pltpu.VMEM((256, 64), F32)] + x_sems,
        compiler_params=_params(("arbitrary",)),
    )(qkv, proj, proj, states, inverses, dtb, alog, nw, dcat, *([] if exchange is None else exchange.arrays))


def _pad_cols(w):
    z = jnp.zeros((w.shape[0], 120), w.dtype)
    return jnp.concatenate([w[:, 2056:2824], w[:, 2824:3080], w[:, 1024:2048], w[:, 0:512], w[:, 512:1024],
                            w[:, 2048:2056], z, w[:, 3080:3088], z], axis=1)


def _unpad_cols(g):
    return jnp.concatenate([g[:, PC_ATT:PC_ATT + 512], g[:, PC_SZ:PC_SZ + 512], g[:, PC_XBC:PC_XBC + 1024],
                            g[:, PC_DT:PC_DT + 8], g[:, PC_GQKV:PC_GQKV + 768], g[:, PC_GZ:PC_GZ + 256],
                            g[:, PC_BA:PC_BA + 8]], axis=1)


def _vec128(v, at=0):
    return jnp.zeros((1, LANES), F32).at[0, at:at + v.shape[0]].set(v)


def _in_weight(gathered):
    return _pad_cols(jnp.concatenate([gathered[k].reshape(D_MODEL, -1) for k in range(N_CHIPS)], axis=1))


def _matmul_weights(w_in, gathered):
    rows = lambda name: gathered[name].reshape(-1, gathered[name].shape[-1])
    w_out = rows("w_out")
    return dict(
        w_in=w_in,
        w_out=jnp.concatenate([w_out[256:768], w_out[0:256], w_out[768:1024]], axis=0),
        w_gu=_interleave_gu(rows("ffn_w_gate"), rows("ffn_w_up")),
        w_down=rows("ffn_w_down"))


def _small_operands(w, l):
    return dict(
        pre_mix=w["pre_mix_norm"][l][None], post_mix=w["post_mix_norm"][l][None],
        pre_ffn=w["pre_ffn_norm"][l][None], post_ffn=w["post_ffn_norm"][l][None],
        sinks=_vec128(w["attn_sinks"][l]),
        s_cw=w["ssd_conv_w"][l], s_cb=w["ssd_conv_b"][l][None],
        s_dtb=_vec128(w["ssd_dt_bias"][l]), s_alog=_vec128(w["ssd_A_log"][l]), s_d=_vec128(w["ssd_D"][l]),
        s_nw=w["ssd_norm_w"][l][None],
        g_cw=w["gdn_conv_w"][l], g_cb=jnp.zeros((1, 768), F32),
        g_dtb=_vec128(w["gdn_dt_bias"][l], 4), g_alog=_vec128(w["gdn_A_log"][l], 4), g_nw=w["gdn_norm_w"][l][None],
    )


RAW_GRADS = ("w_in_pad", "w_out_cat", "w_gu", "ffn_w_down")
DW_ROWS = 4096


def _local_step(x, target, lw, w_in0, gathers, matmul_weights, reducer):
    saved, landed = [], {}
    xin = x
    h = _prenorm(x, lw[0]["pre_mix"], "prenorm0")
    for l in range(DEPTH):
        p = lw[l]
        carry = (lambda kind: gathers[kind]) if l == 0 else (lambda kind: None)
        proj = _mm_nn(h, w_in0 if l == 0 else p["w_in"], 512, PC_TOT, F32, f"inproj{l}")
        xbc = _conv_fwd(proj, PC_XBC, 1024, p["s_cw"], p["s_cb"], f"ssd_conv{l}")
        gqkv = _conv_fwd(proj, PC_GQKV, 768, p["g_cw"], p["g_cb"], f"gdn_conv{l}")
        att = _swa_fwd(proj, p["sinks"], f"swa{l}")
        ssd, s_states, *landed_s = _ssd_fwd(proj, xbc, p["s_dtb"], p["s_alog"], p["s_d"], p["s_nw"], f"ssd{l}",
                                            carry("ssd"))
        gdn, g_states, g_inv, *landed_g = _gdn_fwd(proj, gqkv, p["g_dtb"], p["g_alog"], p["g_nw"], f"gdn{l}",
                                                   carry("gdn"))
        if l == 0:
            landed.update(ssd=landed_s, gdn=landed_g)
            p.update(matmul_weights(0, landed))
        cat = jnp.concatenate([ssd, att, gdn], axis=1)
        mix = _mm_nn(cat, p["w_out"], 512, 1024, F32, f"outproj{l}")
        x1, h2 = _resid_norm(xin, mix, p["post_mix"], p["pre_ffn"], f"postmix{l}")
        gu, act, *landed_u = _ffn_up(h2, p["w_gu"], f"ffn_gu{l}", exchange=carry("ffn_gu"))
        if l == 0:
            f, *landed_d = _mm_nn(act, p["w_down"], 512, 1024, F32, f"ffn_down{l}", carry("ffn_down"))
            landed.update(ffn_gu=landed_u, ffn_down=landed_d)
            lw[1].update(matmul_weights(1, landed))
        else:
            f = _mm_nn(act, p["w_down"], 512, 1024, F32, f"ffn_down{l}")
        saved.append(dict(xin=xin, h=h, proj=proj, xbc=xbc, gqkv=gqkv, s_states=s_states, g_states=g_states, g_inv=g_inv,
                          cat=cat, mix=mix, x1=x1, h2=h2, gu=gu, act=act, f=f))
        if l + 1 < DEPTH:
            xin, h = _resid_norm(x1, f, p["post_ffn"], lw[l + 1]["pre_mix"], f"postffn{l}")

    g = {k: [None] * DEPTH for k in SMALL + CONV + RAW_GRADS}
    last = saved[-1]
    d_x2, d_f, loss_part, g["post_ffn_norm"][DEPTH - 1] = _resid_loss(
        last["x1"], last["f"], lw[-1]["post_ffn"], target, "loss")
    early = []
    for l in reversed(range(DEPTH)):
        p, s = lw[l], saved[l]
        d_gu = _ffn_down_bwd(d_f, p["w_down"], s["gu"], f"d_act{l}")
        g["ffn_w_down"][l] = _mm_tn(s["act"], d_f, FF_HALF, 512, DW_ROWS, f"dw_down{l}")
        d_h2 = _mm_nn(d_gu, p["w_gu"], 256, 1024, F32, f"d_h2{l}")
        g["w_gu"][l] = _mm_tn(d_gu, s["h2"], FF_HALF, 512, DW_ROWS, f"dw_gu{l}")
        d_x1, d_mix, g["pre_ffn_norm"][l], g["post_mix_norm"][l] = _resid_norm_bwd(
            s["x1"], s["mix"], d_x2, d_h2, p["post_mix"], p["pre_ffn"], f"d_postmix{l}")
        d_cat = _mm_nt(d_mix, p["w_out"], 512, 1024, F32, f"d_cat{l}")
        g["w_out_cat"][l] = _mm_tn(s["cat"], d_mix, 512, 1024, DW_ROWS, f"dw_out{l}")
        d_q, d_k, d_v, g_sinks = _swa_bwd(s["proj"], p["sinks"], d_cat, 512, f"d_swa{l}")
        d_sz, d_xbc, d_dt, g_dtb, g_alog, g_d, g["ssd_norm_w"][l], *siblings = _ssd_bwd(
            s["proj"], s["xbc"], s["s_states"], p["s_dtb"], p["s_alog"], p["s_d"], p["s_nw"], d_cat, 0, f"d_ssd{l}",
            reducer.exchange(g) if l == 0 else None)
        d_gq, d_gz, d_ba, gg_dtb, gg_alog, g["gdn_norm_w"][l], *landed_b = _gdn_bwd(
            s["proj"], s["gqkv"], s["g_states"], s["g_inv"], p["g_dtb"], p["g_alog"], p["g_nw"], d_cat, 768, f"d_gdn{l}",
            reducer.scatter(siblings) if l == 0 else None)
        if l == 0:
            early = landed_b
        d_xbc_raw, g["ssd_conv_w"][l], g["ssd_conv_b"][l] = _conv_bwd(
            s["proj"], PC_XBC, 1024, p["s_cw"], p["s_cb"], d_xbc, f"d_ssd_conv{l}")
        d_gq_raw, g["gdn_conv_w"][l], _ = _conv_bwd(s["proj"], PC_GQKV, 768, p["g_cw"], p["g_cb"], d_gq, f"d_gdn_conv{l}")
        d_proj = jnp.concatenate([d_gq_raw, d_gz, d_xbc_raw, d_q, d_k, d_v, d_sz, d_dt, d_ba], axis=1)
        d_h = _mm_nt(d_proj, p["w_in"], 512, 1024, F32, f"d_h{l}")
        g["w_in_pad"][l] = _mm_tn(s["h"], d_proj, 512, PC_TOT // 2, DW_ROWS, f"dw_in{l}")
        g["attn_sinks"][l] = g_sinks[0, :4]
        g["ssd_dt_bias"][l], g["ssd_A_log"][l], g["ssd_D"][l] = g_dtb[0, :8], g_alog[0, :8], g_d[0, :8]
        g["gdn_dt_bias"][l], g["gdn_A_log"][l] = gg_dtb[0, 4:8], gg_alog[0, 4:8]
        if l > 0:
            sp = saved[l - 1]
            d_x2, d_f, g["pre_mix_norm"][l], g["post_ffn_norm"][l - 1] = _resid_norm_bwd(
                s["xin"], sp["f"], d_x1, d_h, lw[l - 1]["post_ffn"], p["pre_mix"], f"d_postffn{l - 1}")
        else:
            grad_x, g["pre_mix_norm"][0] = _prenorm_bwd(s["xin"], d_x1, d_h, p["pre_mix"], "d_prenorm0")

    small = {k: jnp.stack([a.reshape(-1) for a in g[k]], axis=0) for k in SMALL + CONV}
    return loss_part, grad_x, small, {k: g[k] for k in RAW_GRADS}, early


BIG = (("w_in", 2), ("w_out", 1), ("ffn_w_gate", 2), ("ffn_w_up", 2), ("ffn_w_down", 1))
CONV = ("ssd_conv_w", "gdn_conv_w")
TRANSPOSED = ("ffn_w_gate", "ffn_w_up")
SMALL = ("pre_mix_norm", "post_mix_norm", "pre_ffn_norm", "post_ffn_norm", "attn_sinks", "ssd_conv_b", "ssd_dt_bias",
         "ssd_A_log", "ssd_D", "ssd_norm_w", "gdn_dt_bias", "gdn_A_log", "gdn_norm_w")


def _row_tile(rows, cap):
    best = rows
    for t in range(8, min(cap, rows) + 1, 8):
        if rows % t == 0:
            best = t
    return best


SMALL_UNIT = 8 * LANES


def _pack_small(vals):
    rows = []
    for a in vals:
        f = a.reshape(-1)
        pad = -f.shape[0] % SMALL_UNIT
        rows.append(jnp.concatenate([f, jnp.zeros((pad,), F32)]).reshape(-1, LANES))
    return jnp.concatenate(rows, axis=0)


def _unpack_small(mat, shapes):
    out, r = [], 0
    for shp in shapes:
        n = math.prod(shp)
        nr = -(-n // SMALL_UNIT) * 8
        out.append(mat[r:r + nr].reshape(-1)[:n].reshape(shp))
        r += nr
    return out


def _place():
    x, y, c = lax.axis_index("x"), lax.axis_index("y"), lax.axis_index("c")
    chips = [(1 - x, y), (x, 1 - y), (1 - x, 1 - y)]
    return x, y, c, chips


ANY = pl.BlockSpec(memory_space=pl.ANY)


def _remote(src, dst, sems, k, to):
    send_sems, recv_sems = sems
    return pltpu.make_async_remote_copy(src_ref=src, dst_ref=dst, send_sem=send_sems.at[k], recv_sem=recv_sems.at[k],
                                        device_id=to, device_id_type=MESH)


def _sem_pairs(n):
    return [pltpu.SemaphoreType.DMA((n,)), pltpu.SemaphoreType.DMA((n,))]


def _run_exchange(exchange, name):
    k = len(exchange.arrays)

    def body(*refs):
        ins, outs, sems = refs[:k], refs[k:-2], refs[-2:]
        exchange.start(ins, outs, sems)
        exchange.finish(ins, outs, sems)

    return pl.pallas_call(
        body, name=name, in_specs=[ANY] * k, out_specs=[ANY] * len(exchange.out_shape),
        out_shape=list(exchange.out_shape), scratch_shapes=_sem_pairs(exchange.n_sems),
    )(*exchange.arrays)


def _gather_exchange(shards):
    n = len(shards)

    def sends(s_refs, g_refs, sems):
        x, y, c, chips = _place()
        return [_remote(s_refs[i].at[c], g_refs[i].at[2 * x + y, c], sems, 6 * i + j, (px, py, c))
                for i in range(n) for j, (px, py) in enumerate(chips)]

    def start(s_refs, g_refs, sems):
        for cp in sends(s_refs, g_refs, sems):
            cp.start()

    def finish(s_refs, g_refs, sems):
        x, y, c, chips = _place()
        sib = (x, y, 1 - c)
        passed = []
        for j, (px, py) in enumerate(chips):
            for i in range(n):
                landed = g_refs[i].at[2 * px + py, c]
                _remote(landed, landed, sems, 6 * i + j, (px, py, c)).wait_recv()
                fw = _remote(landed, landed, sems, 6 * i + 3 + j, sib)
                fw.start()
                passed.append(fw)
        for j, (px, py) in enumerate(chips):
            for i in range(n):
                landed = g_refs[i].at[2 * px + py, 1 - c]
                _remote(landed, landed, sems, 6 * i + 3 + j, sib).wait_recv()
        for cp in sends(s_refs, g_refs, sems) + passed:
            cp.wait_send()

    return _Exchange(shards, [S((N_CHIPS,) + a.shape, a.dtype) for a in shards], 6 * n, start, finish)


def _halves_exchange(ds):
    n = len(ds)

    def copies(d_refs, t_refs, sems):
        x, y, c, _ = _place()
        return [_remote(d_refs[i].at[:, 1 - c], t_refs[i], sems, i, (x, y, 1 - c)) for i in range(n)]

    def start(d_refs, t_refs, sems):
        for cp in copies(d_refs, t_refs, sems):
            cp.start()

    def finish(d_refs, t_refs, sems):
        for cp in copies(d_refs, t_refs, sems):
            cp.wait()

    return _Exchange(ds, [S((N_CHIPS,) + a.shape[2:], a.dtype) for a in ds], n, start, finish)


def _scatter_exchange(ps):
    n = len(ps)

    def copies(p_refs, u_refs, sems):
        x, y, c, chips = _place()
        return [_remote(p_refs[i].at[2 * px + py], u_refs[i].at[j], sems, 3 * i + j, (px, py, c))
                for j, (px, py) in enumerate(chips) for i in range(n)]

    def start(p_refs, u_refs, sems):
        for cp in copies(p_refs, u_refs, sems):
            cp.start()

    def finish(p_refs, u_refs, sems):
        for cp in copies(p_refs, u_refs, sems):
            cp.wait()

    return _Exchange(ps, [S((3,) + a.shape[1:], a.dtype) for a in ps], 3 * n, start, finish)


def _join_halves(qs, name):
    n = len(qs)

    def body(*refs):
        o_refs, sems = refs[n:2 * n], refs[2 * n:]
        x, y, c, _ = _place()
        cps = [_remote(o_refs[i].at[c], o_refs[i].at[c], sems, i, (x, y, 1 - c)) for i in range(n)]
        for cp in cps:
            cp.start()
        for i in range(n):
            other = o_refs[i].at[1 - c]
            _remote(other, other, sems, i, (x, y, 1 - c)).wait_recv()
        for cp in cps:
            cp.wait_send()

    return pl.pallas_call(
        body, name=name, in_specs=[ANY] * n, out_specs=[ANY] * n,
        out_shape=[S(a.shape, a.dtype) for a in qs], input_output_aliases={i: i for i in range(n)},
        scratch_shapes=_sem_pairs(n),
    )(*qs)


def _gather_small(v, name):
    def body(v_ref, o_ref, send_sems, recv_sems, local_sem):
        x, y, c, _ = _place()
        me = 4 * x + 2 * y + c
        mine = pltpu.make_async_copy(v_ref, o_ref.at[me], local_sem)
        mine.start()
        cps = []
        for k in range(1, N_DEV):
            fx, fy, fc = (k >> 2) & 1, (k >> 1) & 1, k & 1
            peer = (x ^ fx, y ^ fy, c ^ fc)
            cps.append(pltpu.make_async_remote_copy(
                src_ref=v_ref, dst_ref=o_ref.at[me], send_sem=send_sems.at[k - 1], recv_sem=recv_sems.at[k - 1],
                device_id=peer, device_id_type=MESH))
        for cp in cps:
            cp.start()
        for k in range(1, N_DEV):
            fx, fy, fc = (k >> 2) & 1, (k >> 1) & 1, k & 1
            dst = o_ref.at[4 * (x ^ fx) + 2 * (y ^ fy) + (c ^ fc)]
            pltpu.make_async_remote_copy(src_ref=dst, dst_ref=dst, send_sem=send_sems.at[k - 1],
                                         recv_sem=recv_sems.at[k - 1], device_id=(x, y, c),
                                         device_id_type=MESH).wait_recv()
        for cp in cps:
            cp.wait_send()
        mine.wait()

    return pl.pallas_call(
        body, name=name, in_specs=[ANY], out_specs=ANY, out_shape=S((N_DEV,) + v.shape, F32),
        scratch_shapes=[pltpu.SemaphoreType.DMA((N_DEV - 1,)), pltpu.SemaphoreType.DMA((N_DEV - 1,)),
                        pltpu.SemaphoreType.DMA],
    )(v)


def _sum_leading(a, name):
    n, rows, cols = a.shape
    tm = _row_tile(rows, 640)

    def body(a_ref, o_ref):
        acc = a_ref[0]
        for k in range(1, n):
            acc = acc + a_ref[k]
        o_ref[...] = acc

    return pl.pallas_call(
        body, grid=(rows // tm,), name=name, in_specs=[pl.BlockSpec((n, tm, cols), lambda i: (0, i, 0))],
        out_specs=pl.BlockSpec((tm, cols), lambda i: (i, 0)), out_shape=S((rows, cols), F32),
        compiler_params=_params(("arbitrary",)),
    )(a)


def _add_sibling(place, a, b, name):
    n, hr, cols = b.shape
    tm = _row_tile(hr, 512)

    def body(place_ref, a_ref, b_ref, o_ref, o16_ref):
        tot = a_ref[0] + b_ref[...]
        o_ref[...] = tot
        o16_ref[...] = tot.astype(BF16)

    spec = pl.BlockSpec((1, tm, cols), lambda k, i, pr: (k, i, 0))
    return pl.pallas_call(
        body, name=name, out_shape=[S((n, hr, cols), F32), S((n, hr, cols), BF16)],
        grid_spec=pltpu.PrefetchScalarGridSpec(
            num_scalar_prefetch=1, grid=(n, hr // tm),
            in_specs=[pl.BlockSpec((1, 1, tm, cols), lambda k, i, pr: (k, pr[0], i, 0)), spec], out_specs=[spec, spec]),
        compiler_params=_params(("arbitrary", "arbitrary")),
    )(place, a, b)


def _add_chips(place, sums, others, name):
    _, hr, cols = sums.shape
    tm = _row_tile(hr, 512)

    def body(place_ref, m_ref, o_ref, out_ref):
        acc = m_ref[0]
        for k in range(others.shape[0]):
            acc = acc + o_ref[k].astype(F32)
        out_ref[0] = acc

    return pl.pallas_call(
        body, name=name, out_shape=S((2, hr, cols), F32),
        grid_spec=pltpu.PrefetchScalarGridSpec(
            num_scalar_prefetch=1, grid=(hr // tm,),
            in_specs=[pl.BlockSpec((1, tm, cols), lambda i, pr: (pr[1], i, 0)),
                      pl.BlockSpec((others.shape[0], tm, cols), lambda i, pr: (0, i, 0))],
            out_specs=pl.BlockSpec((1, tm, cols), lambda i, pr: (pr[0], i, 0))),
        compiler_params=_params(("arbitrary",)),
    )(place, sums, others)


def _adamw(wt, g, m, v, name):
    shape = wt.shape
    cols = shape[-1]
    rows = math.prod(shape[:-1])
    tm = rows
    for cand in (512, 256, 128, 64, 32, 16, 8):
        if rows % cand == 0:
            tm = cand
            break
    c1 = 1.0 - ADAM_B1 ** ADAM_STEP
    c2 = 1.0 - ADAM_B2 ** ADAM_STEP

    def body(w_ref, g_ref, m_ref, v_ref, d_ref, nm_ref, nv_ref):
        gv = g_ref[...]
        nm = ADAM_B1 * m_ref[...] + (1.0 - ADAM_B1) * gv
        nv = ADAM_B2 * v_ref[...] + (1.0 - ADAM_B2) * (gv * gv)
        d_ref[...] = -ADAM_LR * ((nm / c1) / (jnp.sqrt(nv / c2) + ADAM_EPS) + ADAM_WD * w_ref[...])
        nm_ref[...] = nm
        nv_ref[...] = nv

    spec = pl.BlockSpec((tm, cols), lambda i: (i, 0))
    outs = pl.pallas_call(
        body, grid=(rows // tm,), name=name, in_specs=[spec] * 4, out_specs=[spec] * 3,
        out_shape=[S((rows, cols), F32)] * 3, compiler_params=_params(("arbitrary",)),
    )(*[a.reshape(rows, cols) for a in (wt, g, m, v)])
    return [o.reshape(shape) for o in outs]


WEIGHTS = ('pre_mix_norm', 'post_mix_norm', 'pre_ffn_norm', 'post_ffn_norm', 'w_in', 'w_out', 'attn_sinks', 'ssd_conv_w',
           'ssd_conv_b', 'ssd_dt_bias', 'ssd_A_log', 'ssd_D', 'ssd_norm_w', 'gdn_conv_w', 'gdn_dt_bias', 'gdn_A_log',
           'gdn_norm_w', 'ffn_w_gate', 'ffn_w_up', 'ffn_w_down')


def _chip_piece(i, raw, shape):
    if i == 0:
        g, axis = _unpad_cols(raw["w_in_pad"]), 1
    elif i == 1:
        g = raw["w_out_cat"]
        g, axis = jnp.concatenate([g[512:768], g[0:512], g[768:1024]], axis=0), 0
    elif i in (2, 3):
        g, axis = _split_gu(raw["w_gu"])[i - 2], 0
    else:
        g, axis = raw["ffn_w_down"], 0
    if axis == 1:
        width = shape[2]
        g = jnp.stack([g[:, k * width:(k + 1) * width] for k in range(N_CHIPS)])
    return g.reshape(N_CHIPS, 2, shape[1] // 2, shape[2])


def _step(x, target, wts, ms, vs):
    chip = 2 * lax.axis_index("x") + lax.axis_index("y")
    place = jnp.stack([lax.axis_index("c"), chip]).astype(jnp.int32)
    big_names = [k for k, _ in BIG]
    flip = lambda k, a: jnp.swapaxes(a, 1, 2) if k in TRANSPOSED else a
    wts, ms, vs = ({k: flip(k, a) for k, a in d.items()} for d in (wts, ms, vs))
    big_shapes = [wts[k].shape for k in big_names]
    own = lambda gathered, shard: lax.dynamic_update_index_in_dim(gathered, shard, chip, 0)
    halves = lambda a: a.reshape((2, a.shape[0] // 2) + a.shape[1:])

    shard = lambda name, l: halves(wts[name][l].astype(BF16))
    first = shard("w_in", 0)
    w_in0 = _in_weight(own(_run_exchange(_gather_exchange([first]), "gather_w_in0")[0], first))
    carried_by = {"ssd": [("w_out", 0), ("ffn_w_gate", 0)],
                  "gdn": [("ffn_w_up", 0), ("ffn_w_down", 0), ("w_in", 1), ("w_out", 1)],
                  "ffn_gu": [("ffn_w_gate", 1), ("ffn_w_up", 1)],
                  "ffn_down": [("ffn_w_down", 1)]}
    gathers = {kind: _gather_exchange([shard(*key) for key in keys]) for kind, keys in carried_by.items()}

    def matmul_weights(l, landed):
        gathered = {}
        for kind, keys in carried_by.items():
            for key, sent, got in zip(keys, gathers[kind].arrays, landed.get(kind, [])):
                if key[1] == l:
                    gathered[key[0]] = own(got, sent).reshape((N_CHIPS,) + wts[key[0]].shape[1:])
        return _matmul_weights(w_in0 if l == 0 else _in_weight(gathered["w_in"]), gathered)

    conv = _gather_small(_pack_small([wts[k] for k in CONV]), "gather_conv_weights")
    conv = [_unpack_small(conv[2 * k], [wts[n].shape for n in CONV]) for k in range(N_CHIPS)]
    w_all = dict(wts)
    for i, n in enumerate(CONV):
        w_all[n] = jnp.concatenate([conv[k][i] for k in range(N_CHIPS)], axis=2)

    early_keys = [(DEPTH - 1, 0)] + [(l, i) for l in reversed(range(DEPTH)) for i in range(1, len(BIG))]
    late_keys = [(l, 0) for l in range(DEPTH - 1)]

    def pieces_of(keys, g):
        return [_chip_piece(i, {k: g[k][l] for k in RAW_GRADS}, big_shapes[i]) for l, i in keys]

    def add_siblings(tag, pieces, siblings):
        return [_add_sibling(place, p, t, f"add_sibling_{tag}{n}") for n, (p, t) in enumerate(zip(pieces, siblings))]

    def reduce_end(tag, keys, sums, others):
        parts = [_add_chips(place, s32, o, f"add_chips_{tag}{n}") for n, ((s32, _), o) in enumerate(zip(sums, others))]
        joined = _join_halves(parts, f"join_halves_{tag}")
        return {key: q.reshape(big_shapes[key[1]][1:]) for key, q in zip(keys, joined)}

    class Reducer:
        pieces, sums = [], []

        def exchange(self, g):
            self.pieces = pieces_of(early_keys, g)
            return _halves_exchange(self.pieces)

        def scatter(self, siblings):
            self.sums = add_siblings("early", self.pieces, siblings)
            return _scatter_exchange([s16 for _, s16 in self.sums])

    reducer = Reducer()

    loss_part, grad_x, small_g, raw, early = _local_step(
        x[0], target[0], [_small_operands(w_all, l) for l in range(DEPTH)], w_in0, gathers, matmul_weights, reducer)

    reduced = reduce_end("early", early_keys, reducer.sums, early)
    late = pieces_of(late_keys, raw)
    late_sums = add_siblings("late", late, _run_exchange(_halves_exchange(late), "exchange_halves_late"))
    others = _run_exchange(_scatter_exchange([s16 for _, s16 in late_sums]), "scatter_chips_late")
    reduced.update(reduce_end("late", late_keys, late_sums, others))
    g_all = {k: jnp.stack([reduced[(l, i)] for l in range(DEPTH)]) for i, k in enumerate(big_names)}

    names = SMALL + CONV
    packed = _pack_small([small_g[k] for k in names] + [loss_part])
    small_sum = _sum_leading(_gather_small(packed, "gather_small_grads"), "add_small")
    vals = _unpack_small(small_sum, [small_g[k].shape for k in names] + [(1, LANES)])
    loss = vals[-1][0, 0]
    for k, v in zip(names, vals[:-1]):
        if k in CONV:
            width = wts[k].shape[2]
            v = lax.dynamic_slice_in_dim(v.reshape(DEPTH, 4, -1), chip * width, width, axis=2)
        g_all[k] = v.reshape(wts[k].shape)

    shapes = [wts[k].shape for k in names]
    d_s, m_s, v_s = _adamw(_pack_small([wts[k] for k in names]), _pack_small([g_all[k] for k in names]),
                           _pack_small([ms[k] for k in names]), _pack_small([vs[k] for k in names]), "adamw_small")
    upd = dict(zip(names, zip(_unpack_small(d_s, shapes), _unpack_small(m_s, shapes), _unpack_small(v_s, shapes))))
    for k in big_names:
        upd[k] = _adamw(wts[k], g_all[k], ms[k], vs[k], f"adamw_{k}")
    return (loss, grad_x[None], *[flip(k, g_all[k]) for k in WEIGHTS], *[flip(k, upd[k][0]) for k in WEIGHTS],
            *[flip(k, upd[k][1]) for k in WEIGHTS], *[flip(k, upd[k][2]) for k in WEIGHTS])


def kernel(x, pre_mix_norm, post_mix_norm, pre_ffn_norm, post_ffn_norm, w_in, w_out, attn_sinks, ssd_conv_w, ssd_conv_b, ssd_dt_bias, ssd_A_log, ssd_D, ssd_norm_w, gdn_conv_w, gdn_dt_bias, gdn_A_log, gdn_norm_w, ffn_w_gate, ffn_w_up, ffn_w_down, loss_target, m_pre_mix_norm, m_post_mix_norm, m_pre_ffn_norm, m_post_ffn_norm, m_w_in, m_w_out, m_attn_sinks, m_ssd_conv_w, m_ssd_conv_b, m_ssd_dt_bias, m_ssd_A_log, m_ssd_D, m_ssd_norm_w, m_gdn_conv_w, m_gdn_dt_bias, m_gdn_A_log, m_gdn_norm_w, m_ffn_w_gate, m_ffn_w_up, m_ffn_w_down, v_pre_mix_norm, v_post_mix_norm, v_pre_ffn_norm, v_post_ffn_norm, v_w_in, v_w_out, v_attn_sinks, v_ssd_conv_w, v_ssd_conv_b, v_ssd_dt_bias, v_ssd_A_log, v_ssd_D, v_ssd_norm_w, v_gdn_conv_w, v_gdn_dt_bias, v_gdn_A_log, v_gdn_norm_w, v_ffn_w_gate, v_ffn_w_up, v_ffn_w_down):
    wts = dict(zip(WEIGHTS, (pre_mix_norm, post_mix_norm, pre_ffn_norm, post_ffn_norm, w_in, w_out, attn_sinks, ssd_conv_w, ssd_conv_b, ssd_dt_bias, ssd_A_log, ssd_D, ssd_norm_w, gdn_conv_w, gdn_dt_bias, gdn_A_log, gdn_norm_w, ffn_w_gate, ffn_w_up, ffn_w_down)))
    ms = dict(zip(WEIGHTS, (m_pre_mix_norm, m_post_mix_norm, m_pre_ffn_norm, m_post_ffn_norm, m_w_in, m_w_out, m_attn_sinks, m_ssd_conv_w, m_ssd_conv_b, m_ssd_dt_bias, m_ssd_A_log, m_ssd_D, m_ssd_norm_w, m_gdn_conv_w, m_gdn_dt_bias, m_gdn_A_log, m_gdn_norm_w, m_ffn_w_gate, m_ffn_w_up, m_ffn_w_down)))
    vs = dict(zip(WEIGHTS, (v_pre_mix_norm, v_post_mix_norm, v_pre_ffn_norm, v_post_ffn_norm, v_w_in, v_w_out, v_attn_sinks, v_ssd_conv_w, v_ssd_conv_b, v_ssd_dt_bias, v_ssd_A_log, v_ssd_D, v_ssd_norm_w, v_gdn_conv_w, v_gdn_dt_bias, v_gdn_A_log, v_gdn_norm_w, v_ffn_w_gate, v_ffn_w_up, v_ffn_w_down)))
    return _step(x, loss_target, wts, ms, vs)
```

```python
import functools
import math

import jax
import jax.numpy as jnp
from jax import lax
from jax.experimental import pallas as pl
from jax.experimental.pallas import tpu as pltpu

F32, BF16 = jnp.float32, jnp.bfloat16
HI = lax.Precision.HIGHEST
MESH = pl.DeviceIdType.MESH
S = jax.ShapeDtypeStruct

D_MODEL = 1024
DEPTH = 2
CHUNK = 64
SSD_CHUNK = 256
GDN_CHUNK = 128
EPS = 1e-6
FF = 2816
N_CHIPS = 4
N_DEV = 8
LANES = 128

VMEM_LIMIT_BYTES = 56 * 1024 * 1024

PC_GQKV, PC_GZ, PC_XBC, PC_ATT, PC_SZ, PC_DT, PC_BA, PC_TOT = 0, 768, 1024, 2048, 2560, 3072, 3200, 3328

ADAM_LR, ADAM_B1, ADAM_B2, ADAM_EPS, ADAM_WD, ADAM_STEP = 0.001, 0.9, 0.999, 1e-08, 0.01, 10

ALIBI_SLOPES = tuple(2.0 ** (-8.0 * (h + 1) / 4) for h in range(4))


def _params(sem=None, **kw):
    if sem is not None:
        kw["dimension_semantics"] = sem
    return pltpu.CompilerParams(vmem_limit_bytes=VMEM_LIMIT_BYTES, **kw)


def _dot(a, b, prec=None):
    return jnp.dot(a, b, precision=prec, preferred_element_type=F32)


def _dot_nt(a, b, prec=None):
    return lax.dot_general(a, b, (((1,), (1,)), ((), ())), precision=prec, preferred_element_type=F32)


def _dot_tn(a, b, prec=None):
    return lax.dot_general(a, b, (((0,), (0,)), ((), ())), precision=prec, preferred_element_type=F32)


def _iota2(n, m):
    return lax.broadcasted_iota(jnp.int32, (n, m), 0), lax.broadcasted_iota(jnp.int32, (n, m), 1)


def _pick_col(arr, idx):
    ci = lax.broadcasted_iota(jnp.int32, arr.shape, 1)
    return jnp.sum(jnp.where(ci == idx, arr, 0.0), axis=1, keepdims=True)


def _pick_row(arr, idx):
    ri = lax.broadcasted_iota(jnp.int32, arr.shape, 0)
    return jnp.sum(jnp.where(ri == idx, arr, 0.0), axis=0, keepdims=True)


def _col_to_row(col, eye):
    return jnp.sum(eye * col, axis=0, keepdims=True)


def _rms(x, w):
    return x * lax.rsqrt(jnp.mean(x * x, axis=-1, keepdims=True) + EPS) * w


def _mm_nn(a, b, tm, tn, out_dtype, name, exchange=None):
    m, k = a.shape
    n = b.shape[1]
    tm, tn = min(tm, m), min(tn, n)
    grid = (n // tn, m // tm)

    def body(a_ref, b_ref, o_ref):
        o_ref[...] = _dot(a_ref[...], b_ref[...]).astype(o_ref.dtype)

    body, x_in, x_out, x_shape, x_sems = _hosted(exchange, 2, 1, grid, body)
    outs = pl.pallas_call(
        body, grid=grid, name=name,
        in_specs=[pl.BlockSpec((tm, k), lambda j, i: (i, 0)), pl.BlockSpec((k, tn), lambda j, i: (0, j))] + x_in,
        out_specs=[pl.BlockSpec((tm, tn), lambda j, i: (i, j))] + x_out,
        out_shape=[S((m, n), out_dtype)] + x_shape, scratch_shapes=x_sems,
        compiler_params=_params(("arbitrary", "arbitrary")),
    )(a, b, *([] if exchange is None else exchange.arrays))
    return outs[0] if exchange is None else outs


def _mm_nt(a, b, tm, tn, out_dtype, name, exchange=None):
    m, k = a.shape
    n = b.shape[0]
    tm, tn = min(tm, m), min(tn, n)
    grid = (n // tn, m // tm)

    def body(a_ref, b_ref, o_ref):
        o_ref[...] = _dot_nt(a_ref[...], b_ref[...]).astype(o_ref.dtype)

    body, x_in, x_out, x_shape, x_sems = _hosted(exchange, 2, 1, grid, body)
    outs = pl.pallas_call(
        body, grid=grid, name=name,
        in_specs=[pl.BlockSpec((tm, k), lambda j, i: (i, 0)), pl.BlockSpec((tn, k), lambda j, i: (j, 0))] + x_in,
        out_specs=[pl.BlockSpec((tm, tn), lambda j, i: (i, j))] + x_out,
        out_shape=[S((m, n), out_dtype)] + x_shape, scratch_shapes=x_sems,
        compiler_params=_params(("arbitrary", "arbitrary")),
    )(a, b, *([] if exchange is None else exchange.arrays))
    return outs[0] if exchange is None else outs


def _mm_tn(a, b, tm, tn, tk, name):
    t, m = a.shape
    n = b.shape[1]
    tm, tn, tk = min(tm, m), min(tn, n), min(tk, t)

    def body(a_ref, b_ref, o_ref):
        part = _dot_tn(a_ref[...], b_ref[...])

        @pl.when(pl.program_id(2) == 0)
        def _():
            o_ref[...] = part

        @pl.when(pl.program_id(2) > 0)
        def _():
            o_ref[...] += part

    return pl.pallas_call(
        body, grid=(m // tm, n // tn, t // tk), name=name,
        in_specs=[pl.BlockSpec((tk, tm), lambda i, j, k: (k, i)), pl.BlockSpec((tk, tn), lambda i, j, k: (k, j))],
        out_specs=pl.BlockSpec((tm, tn), lambda i, j, k: (i, j)),
        out_shape=S((m, n), F32), compiler_params=_params(("arbitrary", "arbitrary", "arbitrary")),
    )(a, b)


def _rowcall(fn, rows, params, row_outs, acc_outs, name, tm=512):
    t = rows[0].shape[0]
    tm = min(tm, t)
    n_in = len(rows) + len(params)
    n_ro = len(row_outs)

    def body(*refs):
        ro, ao = fn(*[r[...] for r in refs[:n_in]])
        for ref, v in zip(refs[n_in:n_in + n_ro], ro):
            ref[...] = v.astype(ref.dtype)
        acc_refs = refs[n_in + n_ro:]
        if acc_refs:
            @pl.when(pl.program_id(0) == 0)
            def _():
                for ref, v in zip(acc_refs, ao):
                    ref[...] = v

            @pl.when(pl.program_id(0) > 0)
            def _():
                for ref, v in zip(acc_refs, ao):
                    ref[...] += v

    in_specs = [pl.BlockSpec((tm, r.shape[1]), lambda i: (i, 0)) for r in rows]
    in_specs += [pl.BlockSpec(p.shape, lambda i: (0, 0)) for p in params]
    out_specs = [pl.BlockSpec((tm, c), lambda i: (i, 0)) for c, _ in row_outs]
    out_specs += [pl.BlockSpec(shape, lambda i: (0, 0)) for shape in acc_outs]
    out_shape = [S((t, c), dt) for c, dt in row_outs] + [S(shape, F32) for shape in acc_outs]
    return pl.pallas_call(
        body, grid=(t // tm,), name=name, in_specs=in_specs, out_specs=out_specs, out_shape=out_shape,
        compiler_params=_params(("arbitrary",)),
    )(*rows, *params)


def _prenorm(x, w, name):
    def fn(x, w):
        return (_rms(x, w),), ()
    return _rowcall(fn, [x], [w], [(D_MODEL, BF16)], [], name)[0]


def _resid_norm(xin, m, w_post, w_next, name):
    def fn(xin, m, w_post, w_next):
        xo = xin + _rms(m, w_post)
        return (xo, _rms(xo, w_next)), ()
    return _rowcall(fn, [xin, m], [w_post, w_next], [(D_MODEL, F32), (D_MODEL, BF16)], [], name)


def _resid_loss(xin, m, w_post, target, name):
    def fn(xin, m, target, w_post):
        r, vjp = jax.vjp(_rms, m, w_post)
        err = xin + r - target
        dy = err * (1.0 / D_MODEL)
        dm, dw = vjp(dy)
        tot = jnp.sum(jnp.sum(err * err, axis=1, keepdims=True), axis=0, keepdims=True) * (0.5 / D_MODEL)
        lane = lax.broadcasted_iota(jnp.int32, (1, LANES), 1)
        return (dy, dm), (jnp.where(lane == 0, tot, 0.0), dw)
    return _rowcall(fn, [xin, m, target], [w_post], [(D_MODEL, F32), (D_MODEL, BF16)],
                    [(1, LANES), (1, D_MODEL)], name)


def _resid_norm_bwd(x_out, m, d_direct, dh, w_post, w_next, name):
    def fn(x_out, m, d_direct, dh, w_post, w_next):
        _, vjp_n = jax.vjp(_rms, x_out, w_next)
        dx, dwn = vjp_n(dh)
        d_total = d_direct + dx
        _, vjp_p = jax.vjp(_rms, m, w_post)
        dm, dwp = vjp_p(d_total)
        return (d_total, dm), (dwn, dwp)
    return _rowcall(fn, [x_out, m, d_direct, dh], [w_post, w_next], [(D_MODEL, F32), (D_MODEL, BF16)],
                    [(1, D_MODEL), (1, D_MODEL)], name)


def _prenorm_bwd(x, d_direct, dh, w, name):
    def fn(x, d_direct, dh, w):
        _, vjp = jax.vjp(_rms, x, w)
        dx, dw = vjp(dh)
        return (d_direct + dx,), (dw,)
    return _rowcall(fn, [x, d_direct, dh], [w], [(D_MODEL, F32)], [(1, D_MODEL)], name)


FF_HALF = FF // 2


def _interleave_gu(gate_t, up_t):
    return jnp.concatenate([gate_t[:FF_HALF], up_t[:FF_HALF], gate_t[FF_HALF:], up_t[FF_HALF:]], axis=0)


def _split_gu(gu_t):
    return (jnp.concatenate([gu_t[:FF_HALF], gu_t[FF:FF + FF_HALF]], axis=0),
            jnp.concatenate([gu_t[FF_HALF:FF], gu_t[FF + FF_HALF:]], axis=0))


def _swiglu_pair(gu):
    n = gu.shape[1] // 2
    return jax.nn.silu(gu[:, :n]) * gu[:, n:]


def _ffn_up(h2, w_gu, name, tm=512, exchange=None):
    t, k = h2.shape
    tm = min(tm, t)
    grid = (2, t // tm)

    def body(a_ref, b_ref, gu_ref, act_ref):
        gu = _dot_nt(a_ref[...], b_ref[...])
        gu_ref[...] = gu.astype(gu_ref.dtype)
        act_ref[...] = _swiglu_pair(gu).astype(act_ref.dtype)

    body, x_in, x_out, x_shape, x_sems = _hosted(exchange, 2, 2, grid, body)
    return pl.pallas_call(
        body, grid=grid, name=name,
        in_specs=[pl.BlockSpec((tm, k), lambda j, i: (i, 0)), pl.BlockSpec((FF, k), lambda j, i: (j, 0))] + x_in,
        out_specs=[pl.BlockSpec((tm, FF), lambda j, i: (i, j)), pl.BlockSpec((tm, FF_HALF), lambda j, i: (i, j))] + x_out,
        out_shape=[S((t, 2 * FF), BF16), S((t, FF), BF16)] + x_shape, scratch_shapes=x_sems,
        compiler_params=_params(("arbitrary", "arbitrary")),
    )(h2, w_gu, *([] if exchange is None else exchange.arrays))


def _ffn_down_bwd(d_f, w_down, gu, name, tm=512):
    t, k = d_f.shape
    tm = min(tm, t)

    def body(a_ref, b_ref, gu_ref, o_ref):
        d_act = _dot_nt(a_ref[...], b_ref[...])
        _, vjp = jax.vjp(_swiglu_pair, gu_ref[...].astype(F32))
        o_ref[...] = vjp(d_act)[0].astype(o_ref.dtype)

    return pl.pallas_call(
        body, grid=(2, t // tm), name=name,
        in_specs=[pl.BlockSpec((tm, k), lambda j, i: (i, 0)), pl.BlockSpec((FF_HALF, k), lambda j, i: (j, 0)),
                  pl.BlockSpec((tm, FF), lambda j, i: (i, j))],
        out_specs=pl.BlockSpec((tm, FF), lambda j, i: (i, j)),
        out_shape=S((t, 2 * FF), BF16), compiler_params=_params(("arbitrary", "arbitrary")),
    )(d_f, w_down, gu)


def _conv_fwd(proj, col0, width, w, b, name, tm=512):
    t = proj.shape[0]
    tm = min(tm, t)
    cb = col0 // width

    def body(x_ref, w_ref, b_ref, o_ref, ext):
        @pl.when(pl.program_id(0) == 0)
        def _():
            ext[0:8, :] = jnp.zeros((8, width), F32)

        ext[8:8 + tm, :] = x_ref[...]
        y = b_ref[...] + w_ref[0:1, :] * ext[pl.ds(5, tm), :]
        for k in range(1, 4):
            y = y + w_ref[k:k + 1, :] * ext[pl.ds(5 + k, tm), :]
        o_ref[...] = jax.nn.silu(y)
        ext[0:8, :] = ext[tm:tm + 8, :]

    return pl.pallas_call(
        body, grid=(t // tm,), name=name,
        in_specs=[pl.BlockSpec((tm, width), lambda i: (i, cb)), pl.BlockSpec((4, width), lambda i: (0, 0)),
                  pl.BlockSpec((1, width), lambda i: (0, 0))],
        out_specs=pl.BlockSpec((tm, width), lambda i: (i, 0)),
        out_shape=S((t, width), F32), scratch_shapes=[pltpu.VMEM((tm + 8, width), F32)],
        compiler_params=_params(("arbitrary",)),
    )(proj, w, b)


def _conv_bwd(proj, col0, width, w, b, dact, name, tm=512):
    t = proj.shape[0]
    tm = min(tm, t)
    nb = t // tm
    cb = col0 // width
    hb = tm // 8

    def body(x_ref, halo_ref, d_ref, w_ref, b_ref, dx_ref, dw_ref, db_ref, extx, extd):
        i = pl.program_id(0)
        blk = nb - 1 - i

        @pl.when(i == 0)
        def _():
            extd[tm:tm + 8, :] = jnp.zeros((8, width), F32)
            dw_ref[...] = jnp.zeros((4, width), F32)
            db_ref[...] = jnp.zeros((1, width), F32)

        extx[0:8, :] = jnp.where(blk == 0, 0.0, halo_ref[...])
        extx[8:8 + tm, :] = x_ref[...]
        y = b_ref[...] + w_ref[0:1, :] * extx[pl.ds(5, tm), :]
        for k in range(1, 4):
            y = y + w_ref[k:k + 1, :] * extx[pl.ds(5 + k, tm), :]
        sig = jax.nn.sigmoid(y)
        dy = d_ref[...] * (sig * (1.0 + y * (1.0 - sig)))
        extd[0:tm, :] = dy
        dx = w_ref[0:1, :] * extd[pl.ds(3, tm), :]
        for k in range(1, 4):
            dx = dx + w_ref[k:k + 1, :] * extd[pl.ds(3 - k, tm), :]
        dx_ref[...] = dx.astype(dx_ref.dtype)
        for k in range(4):
            dw_ref[k:k + 1, :] += jnp.sum(dy * extx[pl.ds(5 + k, tm), :], axis=0, keepdims=True)
        db_ref[...] += jnp.sum(dy, axis=0, keepdims=True)
        extd[tm:tm + 8, :] = extd[0:8, :]

    return pl.pallas_call(
        body, grid=(nb,), name=name,
        in_specs=[pl.BlockSpec((tm, width), lambda i: (nb - 1 - i, cb)),
                  pl.BlockSpec((8, width), lambda i: (jnp.maximum((nb - 1 - i) * hb - 1, 0), cb)),
                  pl.BlockSpec((tm, width), lambda i: (nb - 1 - i, 0)),
                  pl.BlockSpec((4, width), lambda i: (0, 0)), pl.BlockSpec((1, width), lambda i: (0, 0))],
        out_specs=[pl.BlockSpec((tm, width), lambda i: (nb - 1 - i, 0)), pl.BlockSpec((4, width), lambda i: (0, 0)),
                   pl.BlockSpec((1, width), lambda i: (0, 0))],
        out_shape=[S((t, width), BF16), S((4, width), F32), S((1, width), F32)],
        scratch_shapes=[pltpu.VMEM((tm + 8, width), F32), pltpu.VMEM((tm + 8, width), F32)],
        compiler_params=_params(("arbitrary",)),
    )(proj, proj, dact, w, b)


SWA_BQ = 256
SWA_BACK = 128


def _swa_block(q, kw, vw, sinks, blk):
    nq, nk = SWA_BQ, SWA_BQ + SWA_BACK
    r, j = _iota2(nq, nk)
    rel = r // CHUNK + 2 - j // CHUNK
    valid = (rel >= 0) & (rel <= 2) & (blk * (SWA_BQ // CHUNK) + j // CHUNK - 2 >= 0)
    dist = jnp.abs(r + SWA_BACK - j).astype(F32)
    outs = []
    for h in range(4):
        kv = h // 2
        qh = q[:, 64 * h:64 * h + 64]
        kh = kw[:, 64 * kv:64 * kv + 64]
        vh = vw[:, 64 * kv:64 * kv + 64]
        s = _dot_nt(qh, kh) * 0.125 - ALIBI_SLOPES[h] * dist
        s = jnp.where(valid, s, -1e30)
        sink = _pick_col(sinks, h)
        m = jnp.maximum(jnp.max(s, axis=1, keepdims=True), sink)
        e = jnp.exp(s - m)
        den = jnp.sum(e, axis=1, keepdims=True) + jnp.exp(sink - m)
        outs.append(_dot(e / den, vh))
    return jnp.concatenate(outs, axis=1)


def _swa_fwd(proj, sinks, name):
    t = proj.shape[0]
    qb, kb = PC_ATT // 256, PC_ATT // 128 + 2
    win = SWA_BQ + SWA_BACK

    def body(q_ref, k_ref, v_ref, s_ref, o_ref, kp, vp):
        i = pl.program_id(0)

        @pl.when(i == 0)
        def _():
            kp[0:SWA_BACK, :] = jnp.zeros((SWA_BACK, 128), F32)
            vp[0:SWA_BACK, :] = jnp.zeros((SWA_BACK, 128), F32)
            kp[SWA_BACK:, :] = k_ref[...]
            vp[SWA_BACK:, :] = v_ref[...]

        start = pl.multiple_of(i * SWA_BQ, SWA_BQ)
        o = _swa_block(q_ref[...], kp[pl.ds(start, win), :], vp[pl.ds(start, win), :], s_ref[...], i)
        o_ref[...] = o.astype(o_ref.dtype)

    return pl.pallas_call(
        body, grid=(t // SWA_BQ,), name=name,
        in_specs=[pl.BlockSpec((SWA_BQ, 256), lambda i: (i, qb)), pl.BlockSpec((t, 128), lambda i: (0, kb)),
                  pl.BlockSpec((t, 128), lambda i: (0, kb + 1)), pl.BlockSpec((1, LANES), lambda i: (0, 0))],
        out_specs=pl.BlockSpec((SWA_BQ, 256), lambda i: (i, 0)),
        out_shape=S((t, 256), BF16),
        scratch_shapes=[pltpu.VMEM((t + SWA_BACK, 128), F32), pltpu.VMEM((t + SWA_BACK, 128), F32)],
        compiler_params=_params(("arbitrary",)),
    )(proj, proj, proj, sinks)


def _swa_bwd(proj, sinks, dcat, dcol0, name):
    t = proj.shape[0]
    nb = t // SWA_BQ
    qb, kb = PC_ATT // 256, PC_ATT // 128 + 2
    db = dcol0 // 256
    win = SWA_BQ + SWA_BACK

    def body(q_ref, k_ref, v_ref, s_ref, do_ref, dq_ref, dk_ref, dv_ref, ds_ref, kp, vp, dkp, dvp):
        i = pl.program_id(0)

        @pl.when(i == 0)
        def _():
            kp[0:SWA_BACK, :] = jnp.zeros((SWA_BACK, 128), F32)
            vp[0:SWA_BACK, :] = jnp.zeros((SWA_BACK, 128), F32)
            kp[SWA_BACK:, :] = k_ref[...]
            vp[SWA_BACK:, :] = v_ref[...]
            dkp[...] = jnp.zeros_like(dkp)
            dvp[...] = jnp.zeros_like(dvp)
            ds_ref[...] = jnp.zeros_like(ds_ref)

        start = pl.multiple_of(i * SWA_BQ, SWA_BQ)
        _, vjp = jax.vjp(functools.partial(_swa_block, blk=i), q_ref[...], kp[pl.ds(start, win), :],
                         vp[pl.ds(start, win), :], s_ref[...])
        dq, dkw, dvw, dsk = vjp(do_ref[...])
        dq_ref[...] = dq.astype(dq_ref.dtype)
        dkp[pl.ds(start, win), :] += dkw
        dvp[pl.ds(start, win), :] += dvw
        ds_ref[...] += dsk

        @pl.when(i == nb - 1)
        def _():
            dk_ref[...] = dkp[SWA_BACK:, :].astype(dk_ref.dtype)
            dv_ref[...] = dvp[SWA_BACK:, :].astype(dv_ref.dtype)

    return pl.pallas_call(
        body, grid=(nb,), name=name,
        in_specs=[pl.BlockSpec((SWA_BQ, 256), lambda i: (i, qb)), pl.BlockSpec((t, 128), lambda i: (0, kb)),
                  pl.BlockSpec((t, 128), lambda i: (0, kb + 1)), pl.BlockSpec((1, LANES), lambda i: (0, 0)),
                  pl.BlockSpec((SWA_BQ, 256), lambda i: (i, db))],
        out_specs=[pl.BlockSpec((SWA_BQ, 256), lambda i: (i, 0)), pl.BlockSpec((t, 128), lambda i: (0, 0)),
                   pl.BlockSpec((t, 128), lambda i: (0, 0)), pl.BlockSpec((1, LANES), lambda i: (0, 0))],
        out_shape=[S((t, 256), BF16), S((t, 128), BF16), S((t, 128), BF16), S((1, LANES), F32)],
        scratch_shapes=[pltpu.VMEM((t + SWA_BACK, 128), F32) for _ in range(4)],
        compiler_params=_params(("arbitrary",)),
    )(proj, proj, proj, sinks, dcat)


def _ssd_chunk(z, xbc, dt_raw, state, dtb, alog, dsk, nw):
    n = z.shape[0]
    r, c = _iota2(n, n)
    tril = r >= c
    eye = (r == c).astype(F32)
    dt = jax.nn.softplus(dt_raw + dtb)
    acs = _dot(tril.astype(F32), dt * (-jnp.exp(alog)), HI)
    xs, bm, cm = xbc[:, :512], xbc[:, 512:768], xbc[:, 768:1024]
    heads = range(8)
    bg = [bm[:, 128 * g:128 * g + 128] for g in range(2)]
    cg = [cm[:, 128 * g:128 * g + 128] for g in range(2)]
    cb = [_dot_nt(cg[g], bg[g]) for g in range(2)]
    dth = [_pick_col(dt, h) for h in heads]
    acol = [_pick_col(acs, h) for h in heads]
    arow = [_col_to_row(a, eye) for a in acol]
    lmat = [jnp.where(tril, jnp.exp(jnp.where(tril, a - b, 0.0)), 0.0) for a, b in zip(acol, arow)]
    xh = [xs[:, 64 * h:64 * h + 64] for h in heads]
    xc = [x * t for x, t in zip(xh, dth)]
    st = [state[64 * h:64 * h + 64, :] for h in heads]
    alast = [_pick_row(a, n - 1) for a in acol]
    y_in = [_dot(cb[h // 4] * lmat[h], xc[h]) for h in heads]
    y_st = [_dot_nt(cg[h // 4], st[h]) * jnp.exp(acol[h]) for h in heads]
    ys = [y_in[h] + y_st[h] + xh[h] * _pick_col(dsk, h) for h in heads]
    new_states = [st[h] * jnp.exp(alast[h]) + _dot_tn(xc[h] * jnp.exp(alast[h] - acol[h]), bg[h // 4]) for h in heads]
    gg = jnp.concatenate(ys, axis=1) * jax.nn.silu(z)
    outs = []
    for gi in range(2):
        gv = gg[:, 256 * gi:256 * gi + 256]
        outs.append(gv * lax.rsqrt(jnp.mean(gv * gv, axis=-1, keepdims=True) + EPS))
    return jnp.concatenate(outs, axis=1) * nw, jnp.concatenate(new_states, axis=0)


def _ssd_fwd(proj, xbc, dtb, alog, dsk, nw, name, exchange=None):
    t = proj.shape[0]
    CHUNK = min(SSD_CHUNK, t)
    nc = t // CHUNK

    def body(z_ref, x_ref, dt_ref, dtb_ref, al_ref, d_ref, nw_ref, o_ref, st_ref, state):
        @pl.when(pl.program_id(0) == 0)
        def _():
            state[...] = jnp.zeros_like(state)

        st_ref[0] = state[...]
        o, ns = _ssd_chunk(z_ref[...], x_ref[...], dt_ref[...], state[...], dtb_ref[...], al_ref[...], d_ref[...],
                           nw_ref[...])
        o_ref[...] = o.astype(o_ref.dtype)
        state[...] = ns

    body, x_in, x_out, x_shape, x_sems = _hosted(exchange, 7, 2, nc, body)
    vec = pl.BlockSpec((1, LANES), lambda i: (0, 0))
    return pl.pallas_call(
        body, grid=(nc,), name=name,
        in_specs=[pl.BlockSpec((CHUNK, 512), lambda i: (i, PC_SZ // 512)), pl.BlockSpec((CHUNK, 1024), lambda i: (i, 0)),
                  pl.BlockSpec((CHUNK, 128), lambda i: (i, PC_DT // 128)), vec, vec, vec,
                  pl.BlockSpec((1, 512), lambda i: (0, 0))] + x_in,
        out_specs=[pl.BlockSpec((CHUNK, 512), lambda i: (i, 0)), pl.BlockSpec((1, 512, 128), lambda i: (i, 0, 0))] + x_out,
        out_shape=[S((t, 512), BF16), S((nc, 512, 128), F32)] + x_shape,
        scratch_shapes=[pltpu.VMEM((512, 128), F32)] + x_sems,
        compiler_params=_params(("arbitrary",)),
    )(proj, xbc, proj, dtb, alog, dsk, nw, *([] if exchange is None else exchange.arrays))


def _ssd_bwd(proj, xbc, states, dtb, alog, dsk, nw, dcat, dcol0, name, exchange=None):
    t = proj.shape[0]
    CHUNK = min(SSD_CHUNK, t)
    nc = t // CHUNK
    db = dcol0 // 512

    def body(z_ref, x_ref, dt_ref, st_ref, dtb_ref, al_ref, d_ref, nw_ref, do_ref,
             dz_ref, dx_ref, ddt_ref, gdtb_ref, gal_ref, gd_ref, gnw_ref, dstate):
        @pl.when(pl.program_id(0) == 0)
        def _():
            dstate[...] = jnp.zeros_like(dstate)
            gdtb_ref[...] = jnp.zeros_like(gdtb_ref)
            gal_ref[...] = jnp.zeros_like(gal_ref)
            gd_ref[...] = jnp.zeros_like(gd_ref)
            gnw_ref[...] = jnp.zeros_like(gnw_ref)

        _, vjp = jax.vjp(_ssd_chunk, z_ref[...], x_ref[...], dt_ref[...], st_ref[0], dtb_ref[...], al_ref[...],
                         d_ref[...], nw_ref[...])
        dz, dx, ddt, dst, gdtb, gal, gd, gnw = vjp((do_ref[...], dstate[...]))
        dz_ref[...] = dz.astype(dz_ref.dtype)
        dx_ref[...] = dx
        ddt_ref[...] = ddt.astype(ddt_ref.dtype)
        dstate[...] = dst
        gdtb_ref[...] += gdtb
        gal_ref[...] += gal
        gd_ref[...] += gd
        gnw_ref[...] += gnw

    body, x_in, x_out, x_shape, x_sems = _hosted(exchange, 9, 7, nc, body)
    rev = lambda i: nc - 1 - i
    vec = pl.BlockSpec((1, LANES), lambda i: (0, 0))
    vec512 = pl.BlockSpec((1, 512), lambda i: (0, 0))
    return pl.pallas_call(
        body, grid=(nc,), name=name,
        in_specs=[pl.BlockSpec((CHUNK, 512), lambda i: (rev(i), PC_SZ // 512)),
                  pl.BlockSpec((CHUNK, 1024), lambda i: (rev(i), 0)),
                  pl.BlockSpec((CHUNK, 128), lambda i: (rev(i), PC_DT // 128)),
                  pl.BlockSpec((1, 512, 128), lambda i: (rev(i), 0, 0)), vec, vec, vec, vec512,
                  pl.BlockSpec((CHUNK, 512), lambda i: (rev(i), db))] + x_in,
        out_specs=[pl.BlockSpec((CHUNK, 512), lambda i: (rev(i), 0)), pl.BlockSpec((CHUNK, 1024), lambda i: (rev(i), 0)),
                   pl.BlockSpec((CHUNK, 128), lambda i: (rev(i), 0)), vec, vec, vec, vec512] + x_out,
        out_shape=[S((t, 512), BF16), S((t, 1024), F32), S((t, 128), BF16), S((1, LANES), F32), S((1, LANES), F32),
                   S((1, LANES), F32), S((1, 512), F32)] + x_shape,
        scratch_shapes=[pltpu.VMEM((512, 128), F32)] + x_sems,
        compiler_params=_params(("arbitrary",)),
    )(proj, xbc, proj, states, dtb, alog, dsk, nw, dcat, *([] if exchange is None else exchange.arrays))


SOLVE_PREC = lax.Precision.HIGH


def _unit_lower_inverses(nas, known=None):
    def compute(ns):
        if known is not None:
            return tuple(known)
        n = ns[0].shape[0]
        r, c = _iota2(n, n)
        eye = (r == c).astype(F32)
        tm, pw = [eye + a for a in ns], list(ns)
        for _ in range(n.bit_length() - 2):
            pw = [_dot(p, p, SOLVE_PREC) for p in pw]
            tm = [t + _dot(t, p, SOLVE_PREC) for t, p in zip(tm, pw)]
        return tuple(tm)

    inv = jax.custom_vjp(compute)

    def fwd(ns):
        ts = compute(ns)
        return ts, ts

    def bwd(ts, gs):
        part = [_dot_nt(g, t, SOLVE_PREC) for g, t in zip(gs, ts)]
        return (tuple(_dot_tn(t, p, SOLVE_PREC) for t, p in zip(ts, part)),)

    inv.defvjp(fwd, bwd)
    return inv(nas)


def _gdn_chunk(qkv, z, ba, state, dtb, alog, nw, known_inverses=None):
    n = qkv.shape[0]
    r, c = _iota2(n, n)
    tril = r >= c
    stril = r > c
    eye = (r == c).astype(F32)
    beta_all = jax.nn.sigmoid(ba)
    gcs = _dot(tril.astype(F32), -jnp.exp(alog) * jax.nn.softplus(ba + dtb), HI)
    heads = range(4)
    qh = [qkv[:, 64 * h:64 * h + 64] for h in heads]
    kh = [qkv[:, 256 + 64 * h:256 + 64 * h + 64] for h in heads]
    vh = [qkv[:, 512 + 64 * h:512 + 64 * h + 64] for h in heads]
    qn = [q * lax.rsqrt(jnp.sum(q * q, axis=-1, keepdims=True) + EPS) * 0.125 for q in qh]
    kn = [k * lax.rsqrt(jnp.sum(k * k, axis=-1, keepdims=True) + EPS) for k in kh]
    beta = [_pick_col(beta_all, h) for h in heads]
    gcol = [_pick_col(gcs, 4 + h) for h in heads]
    grow = [_col_to_row(g, eye) for g in gcol]
    decay = [jnp.where(tril, jnp.exp(jnp.where(tril, gc - gr, 0.0)), 0.0) for gc, gr in zip(gcol, grow)]
    kbeta = [k * b for k, b in zip(kn, beta)]
    kk = [_dot_nt(kb, k) for kb, k in zip(kbeta, kn)]
    qk = [_dot_nt(q, k) * dc for q, k, dc in zip(qn, kn, decay)]
    known = None if known_inverses is None else [known_inverses[n * h:n * (h + 1), :] for h in heads]
    tms = _unit_lower_inverses(tuple(-jnp.where(stril, x * dc, 0.0) for x, dc in zip(kk, decay)), known)
    rhs = [jnp.concatenate([v * b, kb * jnp.exp(g)], axis=1) for v, b, kb, g in zip(vh, beta, kbeta, gcol)]
    sol = [_dot(t, x, SOLVE_PREC) for t, x in zip(tms, rhs)]
    st = [state[64 * h:64 * h + 64, :] for h in heads]
    v_new = [s_[:, :64] - _dot(s_[:, 64:], s) for s_, s in zip(sol, st)]
    o = [_dot(q * jnp.exp(g), s) + _dot(x, vn) for q, g, s, x, vn in zip(qn, gcol, st, qk, v_new)]
    glast = [_pick_row(g, n - 1) for g in gcol]
    new_states = [s * jnp.exp(gl) + _dot_tn(k * jnp.exp(gl - g), vn)
                  for s, gl, k, g, vn in zip(st, glast, kn, gcol, v_new)]
    o = [x * lax.rsqrt(jnp.mean(x * x, axis=-1, keepdims=True) + EPS) * nw for x in o]
    outs = [x * jax.nn.silu(z[:, 64 * h:64 * h + 64]) for h, x in zip(heads, o)]
    return jnp.concatenate(outs, axis=1), jnp.concatenate(new_states, axis=0), jnp.concatenate(tms, axis=0)


class _Exchange:
    def __init__(self, arrays, out_shape, n_sems, start, finish):
        self.arrays, self.out_shape, self.n_sems, self.start, self.finish = arrays, out_shape, n_sems, start, finish


def _hosted(exchange, n_in, n_out, grid, body):
    if exchange is None:
        return body, [], [], [], []
    k_in, k_out = len(exchange.arrays), len(exchange.out_shape)
    grid = (grid,) if isinstance(grid, int) else tuple(grid)

    def hosted_body(*refs):
        ins, refs = refs[:n_in + k_in], refs[n_in + k_in:]
        outs, scratch = refs[:n_out + k_out], refs[n_out + k_out:]
        sems = scratch[-2:]
        first, last = True, True
        for axis, steps in enumerate(grid):
            first = first & (pl.program_id(axis) == 0)
            last = last & (pl.program_id(axis) == steps - 1)

        @pl.when(first)
        def _():
            exchange.start(ins[n_in:], outs[n_out:], sems)

        body(*ins[:n_in], *outs[:n_out], *scratch[:-2])

        @pl.when(last)
        def _():
            exchange.finish(ins[n_in:], outs[n_out:], sems)

    return hosted_body, [ANY] * k_in, [ANY] * k_out, list(exchange.out_shape), _sem_pairs(exchange.n_sems)


def _gdn_fwd(proj, qkv, dtb, alog, nw, name, exchange=None):
    t = proj.shape[0]
    CHUNK = min(GDN_CHUNK, t)
    nc = t // CHUNK

    def body(q_ref, z_ref, ba_ref, dtb_ref, al_ref, nw_ref, o_ref, st_ref, inv_ref, state):
        @pl.when(pl.program_id(0) == 0)
        def _():
            state[...] = jnp.zeros_like(state)

        st_ref[0] = state[...]
        o, ns, tms = _gdn_chunk(q_ref[...], z_ref[...], ba_ref[...], state[...], dtb_ref[...], al_ref[...], nw_ref[...])
        o_ref[...] = o.astype(o_ref.dtype)
        inv_ref[0] = tms
        state[...] = ns

    body, x_in, x_out, x_shape, x_sems = _hosted(exchange, 6, 3, nc, body)
    vec = pl.BlockSpec((1, LANES), lambda i: (0, 0))
    per_chunk = pl.BlockSpec((1, 256, 64), lambda i: (i, 0, 0))
    return pl.pallas_call(
        body, grid=(nc,), name=name,
        in_specs=[pl.BlockSpec((CHUNK, 768), lambda i: (i, 0)), pl.BlockSpec((CHUNK, 256), lambda i: (i, PC_GZ // 256)),
                  pl.BlockSpec((CHUNK, 128), lambda i: (i, PC_BA // 128)), vec, vec, pl.BlockSpec((1, 64), lambda i: (0, 0))]
        + x_in,
        out_specs=[pl.BlockSpec((CHUNK, 256), lambda i: (i, 0)), per_chunk,
                   pl.BlockSpec((1, 4 * CHUNK, CHUNK), lambda i: (i, 0, 0))] + x_out,
        out_shape=[S((t, 256), BF16), S((nc, 256, 64), F32), S((nc, 4 * CHUNK, CHUNK), F32)] + x_shape,
        scratch_shapes=[pltpu.VMEM((256, 64), F32)] + x_sems,
        compiler_params=_params(("arbitrary",)),
    )(qkv, proj, proj, dtb, alog, nw, *([] if exchange is None else exchange.arrays))


def _gdn_bwd(proj, qkv, states, inverses, dtb, alog, nw, dcat, dcol0, name, exchange=None):
    t = proj.shape[0]
    CHUNK = min(GDN_CHUNK, t)
    nc = t // CHUNK
    db = dcol0 // 256

    def body(q_ref, z_ref, ba_ref, st_ref, inv_ref, dtb_ref, al_ref, nw_ref, do_ref,
             dq_ref, dz_ref, dba_ref, gdtb_ref, gal_ref, gnw_ref, dstate):
        @pl.when(pl.program_id(0) == 0)
        def _():
            dstate[...] = jnp.zeros_like(dstate)
            gdtb_ref[...] = jnp.zeros_like(gdtb_ref)
            gal_ref[...] = jnp.zeros_like(gal_ref)
            gnw_ref[...] = jnp.zeros_like(gnw_ref)

        def chunk(*operands):
            return _gdn_chunk(*operands, known_inverses=inv_ref[0])[:2]

        _, vjp = jax.vjp(chunk, q_ref[...], z_ref[...], ba_ref[...], st_ref[0], dtb_ref[...], al_ref[...], nw_ref[...])
        dq, dz, dba, dst, gdtb, gal, gnw = vjp((do_ref[...], dstate[...]))
        dq_ref[...] = dq
        dz_ref[...] = dz.astype(dz_ref.dtype)
        dba_ref[...] = dba.astype(dba_ref.dtype)
        dstate[...] = dst
        gdtb_ref[...] += gdtb
        gal_ref[...] += gal
        gnw_ref[...] += gnw

    body, x_in, x_out, x_shape, x_sems = _hosted(exchange, 9, 6, nc, body)
    rev = lambda i: nc - 1 - i
    vec = pl.BlockSpec((1, LANES), lambda i: (0, 0))
    vec64 = pl.BlockSpec((1, 64), lambda i: (0, 0))
    per_chunk = pl.BlockSpec((1, 256, 64), lambda i: (rev(i), 0, 0))
    return pl.pallas_call(
        body, grid=(nc,), name=name,
        in_specs=[pl.BlockSpec((CHUNK, 768), lambda i: (rev(i), 0)),
                  pl.BlockSpec((CHUNK, 256), lambda i: (rev(i), PC_GZ // 256)),
                  pl.BlockSpec((CHUNK, 128), lambda i: (rev(i), PC_BA // 128)),
                  per_chunk, pl.BlockSpec((1, 4 * CHUNK, CHUNK), lambda i: (rev(i), 0, 0)), vec, vec, vec64,
                  pl.BlockSpec((CHUNK, 256), lambda i: (rev(i), db))] + x_in,
        out_specs=[pl.BlockSpec((CHUNK, 768), lambda i: (rev(i), 0)), pl.BlockSpec((CHUNK, 256), lambda i: (rev(i), 0)),
                   pl.BlockSpec((CHUNK, 128), lambda i: (rev(i), 0)), vec, vec, vec64] + x_out,
        out_shape=[S((t, 768), F32), S((t, 256), BF16), S((t, 128), BF16), S((1, LANES), F32), S((1, LANES), F32),
                   S((1, 64), F32)] + x_shape,
        scratch_shapes=[pltpu.VMEM((256, 64), F32)] + x_sems,
        compiler_params=_params(("arbitrary",)),
    )(qkv, proj, proj, states, inverses, dtb, alog, nw, dcat, *([] if exchange is None else exchange.arrays))


def _pad_cols(w):
    z = jnp.zeros((w.shape[0], 120), w.dtype)
    return jnp.concatenate([w[:, 2056:2824], w[:, 2824:3080], w[:, 1024:2048], w[:, 0:512], w[:, 512:1024],
                            w[:, 2048:2056], z, w[:, 3080:3088], z], axis=1)


def _unpad_cols(g):
    return jnp.concatenate([g[:, PC_ATT:PC_ATT + 512], g[:, PC_SZ:PC_SZ + 512], g[:, PC_XBC:PC_XBC + 1024],
                            g[:, PC_DT:PC_DT + 8], g[:, PC_GQKV:PC_GQKV + 768], g[:, PC_GZ:PC_GZ + 256],
                            g[:, PC_BA:PC_BA + 8]], axis=1)


def _vec128(v, at=0):
    return jnp.zeros((1, LANES), F32).at[0, at:at + v.shape[0]].set(v)


def _in_weight(gathered):
    return _pad_cols(jnp.concatenate([gathered[k].reshape(D_MODEL, -1) for k in range(N_CHIPS)], axis=1))


def _matmul_weights(w_in, gathered):
    rows = lambda name: gathered[name].reshape(-1, gathered[name].shape[-1])
    w_out = rows("w_out")
    return dict(
        w_in=w_in,
        w_out=jnp.concatenate([w_out[256:768], w_out[0:256], w_out[768:1024]], axis=0),
        w_gu=_interleave_gu(rows("ffn_w_gate"), rows("ffn_w_up")),
        w_down=rows("ffn_w_down"))


def _small_operands(w, l):
    return dict(
        pre_mix=w["pre_mix_norm"][l][None], post_mix=w["post_mix_norm"][l][None],
        pre_ffn=w["pre_ffn_norm"][l][None], post_ffn=w["post_ffn_norm"][l][None],
        sinks=_vec128(w["attn_sinks"][l]),
        s_cw=w["ssd_conv_w"][l], s_cb=w["ssd_conv_b"][l][None],
        s_dtb=_vec128(w["ssd_dt_bias"][l]), s_alog=_vec128(w["ssd_A_log"][l]), s_d=_vec128(w["ssd_D"][l]),
        s_nw=w["ssd_norm_w"][l][None],
        g_cw=w["gdn_conv_w"][l], g_cb=jnp.zeros((1, 768), F32),
        g_dtb=_vec128(w["gdn_dt_bias"][l], 4), g_alog=_vec128(w["gdn_A_log"][l], 4), g_nw=w["gdn_norm_w"][l][None],
    )


RAW_GRADS = ("w_in_pad", "w_out_cat", "w_gu", "ffn_w_down")
DW_ROWS = 4096


def _local_step(x, target, lw, w_in0, gathers, matmul_weights, reducer):
    saved, landed = [], {}
    xin = x
    h = _prenorm(x, lw[0]["pre_mix"], "prenorm0")
    for l in range(DEPTH):
        p = lw[l]
        carry = (lambda kind: gathers[kind]) if l == 0 else (lambda kind: None)
        proj = _mm_nn(h, w_in0 if l == 0 else p["w_in"], 512, PC_TOT, F32, f"inproj{l}")
        xbc = _conv_fwd(proj, PC_XBC, 1024, p["s_cw"], p["s_cb"], f"ssd_conv{l}")
        gqkv = _conv_fwd(proj, PC_GQKV, 768, p["g_cw"], p["g_cb"], f"gdn_conv{l}")
        att = _swa_fwd(proj, p["sinks"], f"swa{l}")
        ssd, s_states, *landed_s = _ssd_fwd(proj, xbc, p["s_dtb"], p["s_alog"], p["s_d"], p["s_nw"], f"ssd{l}",
                                            carry("ssd"))
        gdn, g_states, g_inv, *landed_g = _gdn_fwd(proj, gqkv, p["g_dtb"], p["g_alog"], p["g_nw"], f"gdn{l}",
                                                   carry("gdn"))
        if l == 0:
            landed.update(ssd=landed_s, gdn=landed_g)
            p.update(matmul_weights(0, landed))
        cat = jnp.concatenate([ssd, att, gdn], axis=1)
        mix = _mm_nn(cat, p["w_out"], 512, 1024, F32, f"outproj{l}")
        x1, h2 = _resid_norm(xin, mix, p["post_mix"], p["pre_ffn"], f"postmix{l}")
        gu, act, *landed_u = _ffn_up(h2, p["w_gu"], f"ffn_gu{l}", exchange=carry("ffn_gu"))
        if l == 0:
            f, *landed_d = _mm_nn(act, p["w_down"], 512, 1024, F32, f"ffn_down{l}", carry("ffn_down"))
            landed.update(ffn_gu=landed_u, ffn_down=landed_d)
            lw[1].update(matmul_weights(1, landed))
        else:
            f = _mm_nn(act, p["w_down"], 512, 1024, F32, f"ffn_down{l}")
        saved.append(dict(xin=xin, h=h, proj=proj, xbc=xbc, gqkv=gqkv, s_states=s_states, g_states=g_states, g_inv=g_inv,
                          cat=cat, mix=mix, x1=x1, h2=h2, gu=gu, act=act, f=f))
        if l + 1 < DEPTH:
            xin, h = _resid_norm(x1, f, p["post_ffn"], lw[l + 1]["pre_mix"], f"postffn{l}")

    g = {k: [None] * DEPTH for k in SMALL + CONV + RAW_GRADS}
    last = saved[-1]
    d_x2, d_f, loss_part, g["post_ffn_norm"][DEPTH - 1] = _resid_loss(
        last["x1"], last["f"], lw[-1]["post_ffn"], target, "loss")
    early, late = [], []
    for l in reversed(range(DEPTH)):
        p, s = lw[l], saved[l]
        d_gu = _ffn_down_bwd(d_f, p["w_down"], s["gu"], f"d_act{l}")
        g["ffn_w_down"][l] = _mm_tn(s["act"], d_f, FF_HALF, 512, DW_ROWS, f"dw_down{l}")
        d_h2 = _mm_nn(d_gu, p["w_gu"], 256, 1024, F32, f"d_h2{l}")
        g["w_gu"][l] = _mm_tn(d_gu, s["h2"], FF_HALF, 512, DW_ROWS, f"dw_gu{l}")
        d_x1, d_mix, g["pre_ffn_norm"][l], g["post_mix_norm"][l] = _resid_norm_bwd(
            s["x1"], s["mix"], d_x2, d_h2, p["post_mix"], p["pre_ffn"], f"d_postmix{l}")
        d_cat = _mm_nt(d_mix, p["w_out"], 512, 1024, F32, f"d_cat{l}")
        g["w_out_cat"][l] = _mm_tn(s["cat"], d_mix, 512, 1024, DW_ROWS, f"dw_out{l}")
        d_q, d_k, d_v, g_sinks = _swa_bwd(s["proj"], p["sinks"], d_cat, 512, f"d_swa{l}")
        d_sz, d_xbc, d_dt, g_dtb, g_alog, g_d, g["ssd_norm_w"][l], *siblings = _ssd_bwd(
            s["proj"], s["xbc"], s["s_states"], p["s_dtb"], p["s_alog"], p["s_d"], p["s_nw"], d_cat, 0, f"d_ssd{l}",
            reducer.exchange(g) if l == 0 else None)
        d_gq, d_gz, d_ba, gg_dtb, gg_alog, g["gdn_norm_w"][l], *landed_b = _gdn_bwd(
            s["proj"], s["gqkv"], s["g_states"], s["g_inv"], p["g_dtb"], p["g_alog"], p["g_nw"], d_cat, 768, f"d_gdn{l}",
            reducer.scatter(siblings) if l == 0 else None)
        if l == 0:
            early = landed_b
        d_xbc_raw, g["ssd_conv_w"][l], g["ssd_conv_b"][l] = _conv_bwd(
            s["proj"], PC_XBC, 1024, p["s_cw"], p["s_cb"], d_xbc, f"d_ssd_conv{l}")
        d_gq_raw, g["gdn_conv_w"][l], _ = _conv_bwd(s["proj"], PC_GQKV, 768, p["g_cw"], p["g_cb"], d_gq, f"d_gdn_conv{l}")
        d_proj = jnp.concatenate([d_gq_raw, d_gz, d_xbc_raw, d_q, d_k, d_v, d_sz, d_dt, d_ba], axis=1)
        g["w_in_pad"][l] = _mm_tn(s["h"], d_proj, 512, PC_TOT // 2, DW_ROWS, f"dw_in{l}")
        if l == 0:
            d_h, *late = _mm_nt(d_proj, p["w_in"], 512, 1024, F32, f"d_h{l}", reducer.late(g))
        else:
            d_h = _mm_nt(d_proj, p["w_in"], 512, 1024, F32, f"d_h{l}")
        g["attn_sinks"][l] = g_sinks[0, :4]
        g["ssd_dt_bias"][l], g["ssd_A_log"][l], g["ssd_D"][l] = g_dtb[0, :8], g_alog[0, :8], g_d[0, :8]
        g["gdn_dt_bias"][l], g["gdn_A_log"][l] = gg_dtb[0, 4:8], gg_alog[0, 4:8]
        if l > 0:
            sp = saved[l - 1]
            d_x2, d_f, g["pre_mix_norm"][l], g["post_ffn_norm"][l - 1] = _resid_norm_bwd(
                s["xin"], sp["f"], d_x1, d_h, lw[l - 1]["post_ffn"], p["pre_mix"], f"d_postffn{l - 1}")
        else:
            grad_x, g["pre_mix_norm"][0] = _prenorm_bwd(s["xin"], d_x1, d_h, p["pre_mix"], "d_prenorm0")

    small = {k: jnp.stack([a.reshape(-1) for a in g[k]], axis=0) for k in SMALL + CONV}
    return loss_part, grad_x, small, {k: g[k] for k in RAW_GRADS}, early, late


BIG = (("w_in", 2), ("w_out", 1), ("ffn_w_gate", 2), ("ffn_w_up", 2), ("ffn_w_down", 1))
CONV = ("ssd_conv_w", "gdn_conv_w")
TRANSPOSED = ("ffn_w_gate", "ffn_w_up")
SMALL = ("pre_mix_norm", "post_mix_norm", "pre_ffn_norm", "post_ffn_norm", "attn_sinks", "ssd_conv_b", "ssd_dt_bias",
         "ssd_A_log", "ssd_D", "ssd_norm_w", "gdn_dt_bias", "gdn_A_log", "gdn_norm_w")


def _row_tile(rows, cap):
    best = rows
    for t in range(8, min(cap, rows) + 1, 8):
        if rows % t == 0:
            best = t
    return best


SMALL_UNIT = 8 * LANES


def _pack_small(vals):
    rows = []
    for a in vals:
        f = a.reshape(-1)
        pad = -f.shape[0] % SMALL_UNIT
        rows.append(jnp.concatenate([f, jnp.zeros((pad,), F32)]).reshape(-1, LANES))
    return jnp.concatenate(rows, axis=0)


def _unpack_small(mat, shapes):
    out, r = [], 0
    for shp in shapes:
        n = math.prod(shp)
        nr = -(-n // SMALL_UNIT) * 8
        out.append(mat[r:r + nr].reshape(-1)[:n].reshape(shp))
        r += nr
    return out


def _place():
    x, y, c = lax.axis_index("x"), lax.axis_index("y"), lax.axis_index("c")
    chips = [(1 - x, y), (x, 1 - y), (1 - x, 1 - y)]
    return x, y, c, chips


ANY = pl.BlockSpec(memory_space=pl.ANY)


def _remote(src, dst, sems, k, to):
    send_sems, recv_sems = sems
    return pltpu.make_async_remote_copy(src_ref=src, dst_ref=dst, send_sem=send_sems.at[k], recv_sem=recv_sems.at[k],
                                        device_id=to, device_id_type=MESH)


def _sem_pairs(n):
    return [pltpu.SemaphoreType.DMA((n,)), pltpu.SemaphoreType.DMA((n,))]


def _run_exchange(exchange, name):
    k = len(exchange.arrays)

    def body(*refs):
        ins, outs, sems = refs[:k], refs[k:-2], refs[-2:]
        exchange.start(ins, outs, sems)
        exchange.finish(ins, outs, sems)

    return pl.pallas_call(
        body, name=name, in_specs=[ANY] * k, out_specs=[ANY] * len(exchange.out_shape),
        out_shape=list(exchange.out_shape), scratch_shapes=_sem_pairs(exchange.n_sems),
    )(*exchange.arrays)


def _gather_exchange(shards):
    n = len(shards)

    def sends(s_refs, g_refs, sems):
        x, y, c, chips = _place()
        return [_remote(s_refs[i].at[c], g_refs[i].at[2 * x + y, c], sems, 6 * i + j, (px, py, c))
                for i in range(n) for j, (px, py) in enumerate(chips)]

    def start(s_refs, g_refs, sems):
        for cp in sends(s_refs, g_refs, sems):
            cp.start()

    def finish(s_refs, g_refs, sems):
        x, y, c, chips = _place()
        sib = (x, y, 1 - c)
        passed = []
        for j, (px, py) in enumerate(chips):
            for i in range(n):
                landed = g_refs[i].at[2 * px + py, c]
                _remote(landed, landed, sems, 6 * i + j, (px, py, c)).wait_recv()
                fw = _remote(landed, landed, sems, 6 * i + 3 + j, sib)
                fw.start()
                passed.append(fw)
        for j, (px, py) in enumerate(chips):
            for i in range(n):
                landed = g_refs[i].at[2 * px + py, 1 - c]
                _remote(landed, landed, sems, 6 * i + 3 + j, sib).wait_recv()
        for cp in sends(s_refs, g_refs, sems) + passed:
            cp.wait_send()

    return _Exchange(shards, [S((N_CHIPS,) + a.shape, a.dtype) for a in shards], 6 * n, start, finish)


def _halves_exchange(ds):
    n = len(ds)

    def copies(d_refs, t_refs, sems):
        x, y, c, _ = _place()
        return [_remote(d_refs[i].at[:, 1 - c], t_refs[i], sems, i, (x, y, 1 - c)) for i in range(n)]

    def start(d_refs, t_refs, sems):
        for cp in copies(d_refs, t_refs, sems):
            cp.start()

    def finish(d_refs, t_refs, sems):
        for cp in copies(d_refs, t_refs, sems):
            cp.wait()

    return _Exchange(ds, [S((N_CHIPS,) + a.shape[2:], a.dtype) for a in ds], n, start, finish)


def _scatter_exchange(ps):
    n = len(ps)

    def copies(p_refs, u_refs, sems):
        x, y, c, chips = _place()
        return [_remote(p_refs[i].at[2 * px + py], u_refs[i].at[j], sems, 3 * i + j, (px, py, c))
                for j, (px, py) in enumerate(chips) for i in range(n)]

    def start(p_refs, u_refs, sems):
        for cp in copies(p_refs, u_refs, sems):
            cp.start()

    def finish(p_refs, u_refs, sems):
        for cp in copies(p_refs, u_refs, sems):
            cp.wait()

    return _Exchange(ps, [S((3,) + a.shape[1:], a.dtype) for a in ps], 3 * n, start, finish)


def _join_halves(qs, name):
    n = len(qs)

    def body(*refs):
        o_refs, sems = refs[n:2 * n], refs[2 * n:]
        x, y, c, _ = _place()
        cps = [_remote(o_refs[i].at[:, c], o_refs[i].at[:, c], sems, i, (x, y, 1 - c)) for i in range(n)]
        for cp in cps:
            cp.start()
        for i in range(n):
            other = o_refs[i].at[:, 1 - c]
            _remote(other, other, sems, i, (x, y, 1 - c)).wait_recv()
        for cp in cps:
            cp.wait_send()

    return pl.pallas_call(
        body, name=name, in_specs=[ANY] * n, out_specs=[ANY] * n,
        out_shape=[S(a.shape, a.dtype) for a in qs], input_output_aliases={i: i for i in range(n)},
        scratch_shapes=_sem_pairs(n),
    )(*qs)


def _gather_small(v, name):
    def body(v_ref, o_ref, send_sems, recv_sems, local_sem):
        x, y, c, _ = _place()
        me = 4 * x + 2 * y + c
        mine = pltpu.make_async_copy(v_ref, o_ref.at[me], local_sem)
        mine.start()
        cps = []
        for k in range(1, N_DEV):
            fx, fy, fc = (k >> 2) & 1, (k >> 1) & 1, k & 1
            peer = (x ^ fx, y ^ fy, c ^ fc)
            cps.append(pltpu.make_async_remote_copy(
                src_ref=v_ref, dst_ref=o_ref.at[me], send_sem=send_sems.at[k - 1], recv_sem=recv_sems.at[k - 1],
                device_id=peer, device_id_type=MESH))
        for cp in cps:
            cp.start()
        for k in range(1, N_DEV):
            fx, fy, fc = (k >> 2) & 1, (k >> 1) & 1, k & 1
            dst = o_ref.at[4 * (x ^ fx) + 2 * (y ^ fy) + (c ^ fc)]
            pltpu.make_async_remote_copy(src_ref=dst, dst_ref=dst, send_sem=send_sems.at[k - 1],
                                         recv_sem=recv_sems.at[k - 1], device_id=(x, y, c),
                                         device_id_type=MESH).wait_recv()
        for cp in cps:
            cp.wait_send()
        mine.wait()

    return pl.pallas_call(
        body, name=name, in_specs=[ANY], out_specs=ANY, out_shape=S((N_DEV,) + v.shape, F32),
        scratch_shapes=[pltpu.SemaphoreType.DMA((N_DEV - 1,)), pltpu.SemaphoreType.DMA((N_DEV - 1,)),
                        pltpu.SemaphoreType.DMA],
    )(v)


def _sum_leading(a, name):
    n, rows, cols = a.shape
    tm = _row_tile(rows, 640)

    def body(a_ref, o_ref):
        acc = a_ref[0]
        for k in range(1, n):
            acc = acc + a_ref[k]
        o_ref[...] = acc

    return pl.pallas_call(
        body, grid=(rows // tm,), name=name, in_specs=[pl.BlockSpec((n, tm, cols), lambda i: (0, i, 0))],
        out_specs=pl.BlockSpec((tm, cols), lambda i: (i, 0)), out_shape=S((rows, cols), F32),
        compiler_params=_params(("arbitrary",)),
    )(a)


def _add_sibling(place, a, b, name):
    n, hr, cols = b.shape
    tm = _row_tile(hr, 512)

    def body(place_ref, a_ref, b_ref, o_ref, o16_ref):
        tot = a_ref[0] + b_ref[...]
        o_ref[...] = tot
        o16_ref[...] = tot.astype(BF16)

    spec = pl.BlockSpec((1, tm, cols), lambda k, i, pr: (k, i, 0))
    return pl.pallas_call(
        body, name=name, out_shape=[S((n, hr, cols), F32), S((n, hr, cols), BF16)],
        grid_spec=pltpu.PrefetchScalarGridSpec(
            num_scalar_prefetch=1, grid=(n, hr // tm),
            in_specs=[pl.BlockSpec((1, 1, tm, cols), lambda k, i, pr: (k, pr[0], i, 0)), spec], out_specs=[spec, spec]),
        compiler_params=_params(("arbitrary", "arbitrary")),
    )(place, a, b)


def _add_chips(place, sums, others, layer, into, name):
    _, hr, cols = sums.shape
    tm = _row_tile(hr, 512)

    def body(place_ref, m_ref, o_ref, *rest):
        acc = m_ref[0]
        for k in range(others.shape[0]):
            acc = acc + o_ref[k].astype(F32)
        rest[-1][0, 0] = acc

    kept = [] if into is None else [into]
    return pl.pallas_call(
        body, name=name, out_shape=S((DEPTH, 2, hr, cols), F32),
        grid_spec=pltpu.PrefetchScalarGridSpec(
            num_scalar_prefetch=1, grid=(hr // tm,),
            in_specs=[pl.BlockSpec((1, tm, cols), lambda i, pr: (pr[1], i, 0)),
                      pl.BlockSpec((others.shape[0], tm, cols), lambda i, pr: (0, i, 0))] + [ANY] * len(kept),
            out_specs=pl.BlockSpec((1, 1, tm, cols), lambda i, pr: (layer, pr[0], i, 0))),
        input_output_aliases={3: 0} if kept else {},
        compiler_params=_params(("arbitrary",)),
    )(place, sums, others, *kept)


def _adamw(wt, g, m, v, name):
    shape = wt.shape
    cols = shape[-1]
    rows = math.prod(shape[:-1])
    tm = rows
    for cand in (512, 256, 128, 64, 32, 16, 8):
        if rows % cand == 0:
            tm = cand
            break
    c1 = 1.0 - ADAM_B1 ** ADAM_STEP
    c2 = 1.0 - ADAM_B2 ** ADAM_STEP

    def body(w_ref, g_ref, m_ref, v_ref, d_ref, nm_ref, nv_ref):
        gv = g_ref[...]
        nm = ADAM_B1 * m_ref[...] + (1.0 - ADAM_B1) * gv
        nv = ADAM_B2 * v_ref[...] + (1.0 - ADAM_B2) * (gv * gv)
        d_ref[...] = -ADAM_LR * ((nm / c1) / (jnp.sqrt(nv / c2) + ADAM_EPS) + ADAM_WD * w_ref[...])
        nm_ref[...] = nm
        nv_ref[...] = nv

    spec = pl.BlockSpec((tm, cols), lambda i: (i, 0))
    outs = pl.pallas_call(
        body, grid=(rows // tm,), name=name, in_specs=[spec] * 4, out_specs=[spec] * 3,
        out_shape=[S((rows, cols), F32)] * 3, compiler_params=_params(("arbitrary",)),
    )(*[a.reshape(rows, cols) for a in (wt, g, m, v)])
    return [o.reshape(shape) for o in outs]


WEIGHTS = ('pre_mix_norm', 'post_mix_norm', 'pre_ffn_norm', 'post_ffn_norm', 'w_in', 'w_out', 'attn_sinks', 'ssd_conv_w',
           'ssd_conv_b', 'ssd_dt_bias', 'ssd_A_log', 'ssd_D', 'ssd_norm_w', 'gdn_conv_w', 'gdn_dt_bias', 'gdn_A_log',
           'gdn_norm_w', 'ffn_w_gate', 'ffn_w_up', 'ffn_w_down')


def _chip_piece(i, raw, shape):
    if i == 0:
        g, axis = _unpad_cols(raw["w_in_pad"]), 1
    elif i == 1:
        g = raw["w_out_cat"]
        g, axis = jnp.concatenate([g[512:768], g[0:512], g[768:1024]], axis=0), 0
    elif i in (2, 3):
        g, axis = _split_gu(raw["w_gu"])[i - 2], 0
    else:
        g, axis = raw["ffn_w_down"], 0
    if axis == 1:
        width = shape[2]
        g = jnp.stack([g[:, k * width:(k + 1) * width] for k in range(N_CHIPS)])
    return g.reshape(N_CHIPS, 2, shape[1] // 2, shape[2])


def _step(x, target, wts, ms, vs):
    chip = 2 * lax.axis_index("x") + lax.axis_index("y")
    place = jnp.stack([lax.axis_index("c"), chip]).astype(jnp.int32)
    big_names = [k for k, _ in BIG]
    flip = lambda k, a: jnp.swapaxes(a, 1, 2) if k in TRANSPOSED else a
    wts, ms, vs = ({k: flip(k, a) for k, a in d.items()} for d in (wts, ms, vs))
    big_shapes = [wts[k].shape for k in big_names]
    own = lambda gathered, shard: lax.dynamic_update_index_in_dim(gathered, shard, chip, 0)
    halves = lambda a: a.reshape((2, a.shape[0] // 2) + a.shape[1:])

    shard = lambda name, l: halves(wts[name][l].astype(BF16))
    first = shard("w_in", 0)
    w_in0 = _in_weight(own(_run_exchange(_gather_exchange([first]), "gather_w_in0")[0], first))
    carried_by = {"ssd": [("w_out", 0), ("ffn_w_gate", 0)],
                  "gdn": [("ffn_w_up", 0), ("ffn_w_down", 0), ("w_in", 1), ("w_out", 1)],
                  "ffn_gu": [("ffn_w_gate", 1), ("ffn_w_up", 1)],
                  "ffn_down": [("ffn_w_down", 1)]}
    gathers = {kind: _gather_exchange([shard(*key) for key in keys]) for kind, keys in carried_by.items()}

    def matmul_weights(l, landed):
        gathered = {}
        for kind, keys in carried_by.items():
            for key, sent, got in zip(keys, gathers[kind].arrays, landed.get(kind, [])):
                if key[1] == l:
                    gathered[key[0]] = own(got, sent).reshape((N_CHIPS,) + wts[key[0]].shape[1:])
        return _matmul_weights(w_in0 if l == 0 else _in_weight(gathered["w_in"]), gathered)

    conv = _gather_small(_pack_small([wts[k] for k in CONV]), "gather_conv_weights")
    conv = [_unpack_small(conv[2 * k], [wts[n].shape for n in CONV]) for k in range(N_CHIPS)]
    w_all = dict(wts)
    for i, n in enumerate(CONV):
        w_all[n] = jnp.concatenate([conv[k][i] for k in range(N_CHIPS)], axis=2)

    early_keys = [(DEPTH - 1, 0)] + [(l, i) for l in reversed(range(DEPTH)) for i in range(1, len(BIG))]
    late_keys = [(l, 0) for l in range(DEPTH - 1)]

    def pieces_of(keys, g):
        return [_chip_piece(i, {k: g[k][l] for k in RAW_GRADS}, big_shapes[i]) for l, i in keys]

    def add_siblings(tag, pieces, siblings):
        return [_add_sibling(place, p, t, f"add_sibling_{tag}{n}") for n, (p, t) in enumerate(zip(pieces, siblings))]

    class Reducer:
        pieces, sums, late_sums = [], [], []

        def exchange(self, g):
            self.pieces = pieces_of(early_keys, g)
            return _halves_exchange(self.pieces)

        def scatter(self, siblings):
            self.sums = add_siblings("early", self.pieces, siblings)
            return _scatter_exchange([s16 for _, s16 in self.sums])

        def late(self, g):
            pieces = pieces_of(late_keys, g)
            siblings = _run_exchange(_halves_exchange(pieces), "exchange_halves_late")
            self.late_sums = add_siblings("late", pieces, siblings)
            return _scatter_exchange([s16 for _, s16 in self.late_sums])

    reducer = Reducer()

    loss_part, grad_x, small_g, raw, early, late = _local_step(
        x[0], target[0], [_small_operands(w_all, l) for l in range(DEPTH)], w_in0, gathers, matmul_weights, reducer)

    reduced = {}
    for tag, keys, sums, others in (("early", early_keys, reducer.sums, early), ("late", late_keys, reducer.late_sums, late)):
        for n, ((l, i), (s32, _), o) in enumerate(zip(keys, sums, others)):
            reduced[i] = _add_chips(place, s32, o, l, reduced.get(i), f"add_chips_{tag}{n}")
    joined = _join_halves([reduced[i] for i in range(len(BIG))], "join_halves")
    g_all = {k: q.reshape(shp) for k, q, shp in zip(big_names, joined, big_shapes)}

    names = SMALL + CONV
    packed = _pack_small([small_g[k] for k in names] + [loss_part])
    small_sum = _sum_leading(_gather_small(packed, "gather_small_grads"), "add_small")
    vals = _unpack_small(small_sum, [small_g[k].shape for k in names] + [(1, LANES)])
    loss = vals[-1][0, 0]
    for k, v in zip(names, vals[:-1]):
        if k in CONV:
            width = wts[k].shape[2]
            v = lax.dynamic_slice_in_dim(v.reshape(DEPTH, 4, -1), chip * width, width, axis=2)
        g_all[k] = v.reshape(wts[k].shape)

    shapes = [wts[k].shape for k in names]
    d_s, m_s, v_s = _adamw(_pack_small([wts[k] for k in names]), _pack_small([g_all[k] for k in names]),
                           _pack_small([ms[k] for k in names]), _pack_small([vs[k] for k in names]), "adamw_small")
    upd = dict(zip(names, zip(_unpack_small(d_s, shapes), _unpack_small(m_s, shapes), _unpack_small(v_s, shapes))))
    for k in big_names:
        upd[k] = _adamw(wts[k], g_all[k], ms[k], vs[k], f"adamw_{k}")
    return (loss, grad_x[None], *[flip(k, g_all[k]) for k in WEIGHTS], *[flip(k, upd[k][0]) for k in WEIGHTS],
            *[flip(k, upd[k][1]) for k in WEIGHTS], *[flip(k, upd[k][2]) for k in WEIGHTS])


def kernel(x, pre_mix_norm, post_mix_norm, pre_ffn_norm, post_ffn_norm, w_in, w_out, attn_sinks, ssd_conv_w, ssd_conv_b, ssd_dt_bias, ssd_A_log, ssd_D, ssd_norm_w, gdn_conv_w, gdn_dt_bias, gdn_A_log, gdn_norm_w, ffn_w_gate, ffn_w_up, ffn_w_down, loss_target, m_pre_mix_norm, m_post_mix_norm, m_pre_ffn_norm, m_post_ffn_norm, m_w_in, m_w_out, m_attn_sinks, m_ssd_conv_w, m_ssd_conv_b, m_ssd_dt_bias, m_ssd_A_log, m_ssd_D, m_ssd_norm_w, m_gdn_conv_w, m_gdn_dt_bias, m_gdn_A_log, m_gdn_norm_w, m_ffn_w_gate, m_ffn_w_up, m_ffn_w_down, v_pre_mix_norm, v_post_mix_norm, v_pre_ffn_norm, v_post_ffn_norm, v_w_in, v_w_out, v_attn_sinks, v_ssd_conv_w, v_ssd_conv_b, v_ssd_dt_bias, v_ssd_A_log, v_ssd_D, v_ssd_norm_w, v_gdn_conv_w, v_gdn_dt_bias, v_gdn_A_log, v_gdn_norm_w, v_ffn_w_gate, v_ffn_w_up, v_ffn_w_down):
    wts = dict(zip(WEIGHTS, (pre_mix_norm, post_mix_norm, pre_ffn_norm, post_ffn_norm, w_in, w_out, attn_sinks, ssd_conv_w, ssd_conv_b, ssd_dt_bias, ssd_A_log, ssd_D, ssd_norm_w, gdn_conv_w, gdn_dt_bias, gdn_A_log, gdn_norm_w, ffn_w_gate, ffn_w_up, ffn_w_down)))
    ms = dict(zip(WEIGHTS, (m_pre_mix_norm, m_post_mix_norm, m_pre_ffn_norm, m_post_ffn_norm, m_w_in, m_w_out, m_attn_sinks, m_ssd_conv_w, m_ssd_conv_b, m_ssd_dt_bias, m_ssd_A_log, m_ssd_D, m_ssd_norm_w, m_gdn_conv_w, m_gdn_dt_bias, m_gdn_A_log, m_gdn_norm_w, m_ffn_w_gate, m_ffn_w_up, m_ffn_w_down)))
    vs = dict(zip(WEIGHTS, (v_pre_mix_norm, v_post_mix_norm, v_pre_ffn_norm, v_post_ffn_norm, v_w_in, v_w_out, v_attn_sinks, v_ssd_conv_w, v_ssd_conv_b, v_ssd_dt_bias, v_ssd_A_log, v_ssd_D, v_ssd_norm_w, v_gdn_conv_w, v_gdn_dt_bias, v_gdn_A_log, v_gdn_norm_w, v_ffn_w_gate, v_ffn_w_up, v_ffn_w_down)))
    return _step(x, loss_target, wts, ms, vs)
```

```python
import functools
import math

import jax
import jax.numpy as jnp
from jax import lax
from jax.experimental import pallas as pl
from jax.experimental.pallas import tpu as pltpu

F32, BF16 = jnp.float32, jnp.bfloat16
HI = lax.Precision.HIGHEST
MESH = pl.DeviceIdType.MESH
S = jax.ShapeDtypeStruct

D_MODEL = 1024
DEPTH = 2
CHUNK = 64
SSD_CHUNK = 256
GDN_CHUNK = 128
EPS = 1e-6
FF = 2816
N_CHIPS = 4
N_DEV = 8
LANES = 128

VMEM_LIMIT_BYTES = 56 * 1024 * 1024

PC_GQKV, PC_GZ, PC_XBC, PC_ATT, PC_SZ, PC_DT, PC_BA, PC_TOT = 0, 768, 1024, 2048, 2560, 3072, 3200, 3328

ADAM_LR, ADAM_B1, ADAM_B2, ADAM_EPS, ADAM_WD, ADAM_STEP = 0.001, 0.9, 0.999, 1e-08, 0.01, 10

ALIBI_SLOPES = tuple(2.0 ** (-8.0 * (h + 1) / 4) for h in range(4))


def _params(sem=None, **kw):
    if sem is not None:
        kw["dimension_semantics"] = sem
    return pltpu.CompilerParams(vmem_limit_bytes=VMEM_LIMIT_BYTES, **kw)


def _dot(a, b, prec=None):
    return jnp.dot(a, b, precision=prec, preferred_element_type=F32)


def _dot_nt(a, b, prec=None):
    return lax.dot_general(a, b, (((1,), (1,)), ((), ())), precision=prec, preferred_element_type=F32)


def _dot_tn(a, b, prec=None):
    return lax.dot_general(a, b, (((0,), (0,)), ((), ())), precision=prec, preferred_element_type=F32)


def _iota2(n, m):
    return lax.broadcasted_iota(jnp.int32, (n, m), 0), lax.broadcasted_iota(jnp.int32, (n, m), 1)


def _pick_col(arr, idx):
    ci = lax.broadcasted_iota(jnp.int32, arr.shape, 1)
    return jnp.sum(jnp.where(ci == idx, arr, 0.0), axis=1, keepdims=True)


def _pick_row(arr, idx):
    ri = lax.broadcasted_iota(jnp.int32, arr.shape, 0)
    return jnp.sum(jnp.where(ri == idx, arr, 0.0), axis=0, keepdims=True)


def _col_to_row(col, eye):
    return jnp.sum(eye * col, axis=0, keepdims=True)


def _rms(x, w):
    return x * lax.rsqrt(jnp.mean(x * x, axis=-1, keepdims=True) + EPS) * w


def _mm_nn(a, b, tm, tn, out_dtype, name, exchange=None):
    m, k = a.shape
    n = b.shape[1]
    tm, tn = min(tm, m), min(tn, n)
    grid = (n // tn, m // tm)

    def body(a_ref, b_ref, o_ref):
        o_ref[...] = _dot(a_ref[...], b_ref[...]).astype(o_ref.dtype)

    body, x_in, x_out, x_shape, x_sems = _hosted(exchange, 2, 1, grid, body)
    outs = pl.pallas_call(
        body, grid=grid, name=name,
        in_specs=[pl.BlockSpec((tm, k), lambda j, i: (i, 0)), pl.BlockSpec((k, tn), lambda j, i: (0, j))] + x_in,
        out_specs=[pl.BlockSpec((tm, tn), lambda j, i: (i, j))] + x_out,
        out_shape=[S((m, n), out_dtype)] + x_shape, scratch_shapes=x_sems,
        compiler_params=_params(("arbitrary", "arbitrary")),
    )(a, b, *([] if exchange is None else exchange.arrays))
    return outs[0] if exchange is None else outs


def _mm_nt(a, b, tm, tn, out_dtype, name, exchange=None):
    m, k = a.shape
    n = b.shape[0]
    tm, tn = min(tm, m), min(tn, n)
    grid = (n // tn, m // tm)

    def body(a_ref, b_ref, o_ref):
        o_ref[...] = _dot_nt(a_ref[...], b_ref[...]).astype(o_ref.dtype)

    body, x_in, x_out, x_shape, x_sems = _hosted(exchange, 2, 1, grid, body)
    outs = pl.pallas_call(
        body, grid=grid, name=name,
        in_specs=[pl.BlockSpec((tm, k), lambda j, i: (i, 0)), pl.BlockSpec((tn, k), lambda j, i: (j, 0))] + x_in,
        out_specs=[pl.BlockSpec((tm, tn), lambda j, i: (i, j))] + x_out,
        out_shape=[S((m, n), out_dtype)] + x_shape, scratch_shapes=x_sems,
        compiler_params=_params(("arbitrary", "arbitrary")),
    )(a, b, *([] if exchange is None else exchange.arrays))
    return outs[0] if exchange is None else outs


def _mm_tn(a, b, tm, tn, tk, name):
    t, m = a.shape
    n = b.shape[1]
    tm, tn, tk = min(tm, m), min(tn, n), min(tk, t)

    def body(a_ref, b_ref, o_ref):
        part = _dot_tn(a_ref[...], b_ref[...])

        @pl.when(pl.program_id(2) == 0)
        def _():
            o_ref[...] = part

        @pl.when(pl.program_id(2) > 0)
        def _():
            o_ref[...] += part

    return pl.pallas_call(
        body, grid=(m // tm, n // tn, t // tk), name=name,
        in_specs=[pl.BlockSpec((tk, tm), lambda i, j, k: (k, i)), pl.BlockSpec((tk, tn), lambda i, j, k: (k, j))],
        out_specs=pl.BlockSpec((tm, tn), lambda i, j, k: (i, j)),
        out_shape=S((m, n), F32), compiler_params=_params(("arbitrary", "arbitrary", "arbitrary")),
    )(a, b)


def _rowcall(fn, rows, params, row_outs, acc_outs, name, tm=512):
    t = rows[0].shape[0]
    tm = min(tm, t)
    n_in = len(rows) + len(params)
    n_ro = len(row_outs)

    def body(*refs):
        ro, ao = fn(*[r[...] for r in refs[:n_in]])
        for ref, v in zip(refs[n_in:n_in + n_ro], ro):
            ref[...] = v.astype(ref.dtype)
        acc_refs = refs[n_in + n_ro:]
        if acc_refs:
            @pl.when(pl.program_id(0) == 0)
            def _():
                for ref, v in zip(acc_refs, ao):
                    ref[...] = v

            @pl.when(pl.program_id(0) > 0)
            def _():
                for ref, v in zip(acc_refs, ao):
                    ref[...] += v

    in_specs = [pl.BlockSpec((tm, r.shape[1]), lambda i: (i, 0)) for r in rows]
    in_specs += [pl.BlockSpec(p.shape, lambda i: (0, 0)) for p in params]
    out_specs = [pl.BlockSpec((tm, c), lambda i: (i, 0)) for c, _ in row_outs]
    out_specs += [pl.BlockSpec(shape, lambda i: (0, 0)) for shape in acc_outs]
    out_shape = [S((t, c), dt) for c, dt in row_outs] + [S(shape, F32) for shape in acc_outs]
    return pl.pallas_call(
        body, grid=(t // tm,), name=name, in_specs=in_specs, out_specs=out_specs, out_shape=out_shape,
        compiler_params=_params(("arbitrary",)),
    )(*rows, *params)


def _prenorm(x, w, name):
    def fn(x, w):
        return (_rms(x, w),), ()
    return _rowcall(fn, [x], [w], [(D_MODEL, BF16)], [], name)[0]


def _resid_norm(xin, m, w_post, w_next, name):
    def fn(xin, m, w_post, w_next):
        xo = xin + _rms(m, w_post)
        return (xo, _rms(xo, w_next)), ()
    return _rowcall(fn, [xin, m], [w_post, w_next], [(D_MODEL, F32), (D_MODEL, BF16)], [], name)


def _resid_loss(xin, m, w_post, target, name):
    def fn(xin, m, target, w_post):
        r, vjp = jax.vjp(_rms, m, w_post)
        err = xin + r - target
        dy = err * (1.0 / D_MODEL)
        dm, dw = vjp(dy)
        tot = jnp.sum(jnp.sum(err * err, axis=1, keepdims=True), axis=0, keepdims=True) * (0.5 / D_MODEL)
        lane = lax.broadcasted_iota(jnp.int32, (1, LANES), 1)
        return (dy, dm), (jnp.where(lane == 0, tot, 0.0), dw)
    return _rowcall(fn, [xin, m, target], [w_post], [(D_MODEL, F32), (D_MODEL, BF16)],
                    [(1, LANES), (1, D_MODEL)], name)


def _resid_norm_bwd(x_out, m, d_direct, dh, w_post, w_next, name):
    def fn(x_out, m, d_direct, dh, w_post, w_next):
        _, vjp_n = jax.vjp(_rms, x_out, w_next)
        dx, dwn = vjp_n(dh)
        d_total = d_direct + dx
        _, vjp_p = jax.vjp(_rms, m, w_post)
        dm, dwp = vjp_p(d_total)
        return (d_total, dm), (dwn, dwp)
    return _rowcall(fn, [x_out, m, d_direct, dh], [w_post, w_next], [(D_MODEL, F32), (D_MODEL, BF16)],
                    [(1, D_MODEL), (1, D_MODEL)], name)


def _prenorm_bwd(x, d_direct, dh, w, name):
    def fn(x, d_direct, dh, w):
        _, vjp = jax.vjp(_rms, x, w)
        dx, dw = vjp(dh)
        return (d_direct + dx,), (dw,)
    return _rowcall(fn, [x, d_direct, dh], [w], [(D_MODEL, F32)], [(1, D_MODEL)], name)


FF_HALF = FF // 2


def _interleave_gu(gate_t, up_t):
    return jnp.concatenate([gate_t[:FF_HALF], up_t[:FF_HALF], gate_t[FF_HALF:], up_t[FF_HALF:]], axis=0)


def _split_gu(gu_t):
    return (jnp.concatenate([gu_t[:FF_HALF], gu_t[FF:FF + FF_HALF]], axis=0),
            jnp.concatenate([gu_t[FF_HALF:FF], gu_t[FF + FF_HALF:]], axis=0))


def _swiglu_pair(gu):
    n = gu.shape[1] // 2
    return jax.nn.silu(gu[:, :n]) * gu[:, n:]


def _ffn_up(h2, w_gu, name, tm=512, exchange=None):
    t, k = h2.shape
    tm = min(tm, t)
    grid = (2, t // tm)

    def body(a_ref, b_ref, gu_ref, act_ref):
        gu = _dot_nt(a_ref[...], b_ref[...])
        gu_ref[...] = gu.astype(gu_ref.dtype)
        act_ref[...] = _swiglu_pair(gu).astype(act_ref.dtype)

    body, x_in, x_out, x_shape, x_sems = _hosted(exchange, 2, 2, grid, body)
    return pl.pallas_call(
        body, grid=grid, name=name,
        in_specs=[pl.BlockSpec((tm, k), lambda j, i: (i, 0)), pl.BlockSpec((FF, k), lambda j, i: (j, 0))] + x_in,
        out_specs=[pl.BlockSpec((tm, FF), lambda j, i: (i, j)), pl.BlockSpec((tm, FF_HALF), lambda j, i: (i, j))] + x_out,
        out_shape=[S((t, 2 * FF), BF16), S((t, FF), BF16)] + x_shape, scratch_shapes=x_sems,
        compiler_params=_params(("arbitrary", "arbitrary")),
    )(h2, w_gu, *([] if exchange is None else exchange.arrays))


def _ffn_down_bwd(d_f, w_down, gu, name, tm=512):
    t, k = d_f.shape
    tm = min(tm, t)

    def body(a_ref, b_ref, gu_ref, o_ref):
        d_act = _dot_nt(a_ref[...], b_ref[...])
        _, vjp = jax.vjp(_swiglu_pair, gu_ref[...].astype(F32))
        o_ref[...] = vjp(d_act)[0].astype(o_ref.dtype)

    return pl.pallas_call(
        body, grid=(2, t // tm), name=name,
        in_specs=[pl.BlockSpec((tm, k), lambda j, i: (i, 0)), pl.BlockSpec((FF_HALF, k), lambda j, i: (j, 0)),
                  pl.BlockSpec((tm, FF), lambda j, i: (i, j))],
        out_specs=pl.BlockSpec((tm, FF), lambda j, i: (i, j)),
        out_shape=S((t, 2 * FF), BF16), compiler_params=_params(("arbitrary", "arbitrary")),
    )(d_f, w_down, gu)


def _conv_fwd(proj, col0, width, w, b, name, tm=512):
    t = proj.shape[0]
    tm = min(tm, t)
    cb = col0 // width

    def body(x_ref, w_ref, b_ref, o_ref, ext):
        @pl.when(pl.program_id(0) == 0)
        def _():
            ext[0:8, :] = jnp.zeros((8, width), F32)

        ext[8:8 + tm, :] = x_ref[...]
        y = b_ref[...] + w_ref[0:1, :] * ext[pl.ds(5, tm), :]
        for k in range(1, 4):
            y = y + w_ref[k:k + 1, :] * ext[pl.ds(5 + k, tm), :]
        o_ref[...] = jax.nn.silu(y)
        ext[0:8, :] = ext[tm:tm + 8, :]

    return pl.pallas_call(
        body, grid=(t // tm,), name=name,
        in_specs=[pl.BlockSpec((tm, width), lambda i: (i, cb)), pl.BlockSpec((4, width), lambda i: (0, 0)),
                  pl.BlockSpec((1, width), lambda i: (0, 0))],
        out_specs=pl.BlockSpec((tm, width), lambda i: (i, 0)),
        out_shape=S((t, width), F32), scratch_shapes=[pltpu.VMEM((tm + 8, width), F32)],
        compiler_params=_params(("arbitrary",)),
    )(proj, w, b)


def _conv_bwd(proj, col0, width, w, b, dact, name, tm=512):
    t = proj.shape[0]
    tm = min(tm, t)
    nb = t // tm
    cb = col0 // width
    hb = tm // 8

    def body(x_ref, halo_ref, d_ref, w_ref, b_ref, dx_ref, dw_ref, db_ref, extx, extd):
        i = pl.program_id(0)
        blk = nb - 1 - i

        @pl.when(i == 0)
        def _():
            extd[tm:tm + 8, :] = jnp.zeros((8, width), F32)
            dw_ref[...] = jnp.zeros((4, width), F32)
            db_ref[...] = jnp.zeros((1, width), F32)

        extx[0:8, :] = jnp.where(blk == 0, 0.0, halo_ref[...])
        extx[8:8 + tm, :] = x_ref[...]
        y = b_ref[...] + w_ref[0:1, :] * extx[pl.ds(5, tm), :]
        for k in range(1, 4):
            y = y + w_ref[k:k + 1, :] * extx[pl.ds(5 + k, tm), :]
        sig = jax.nn.sigmoid(y)
        dy = d_ref[...] * (sig * (1.0 + y * (1.0 - sig)))
        extd[0:tm, :] = dy
        dx = w_ref[0:1, :] * extd[pl.ds(3, tm), :]
        for k in range(1, 4):
            dx = dx + w_ref[k:k + 1, :] * extd[pl.ds(3 - k, tm), :]
        dx_ref[...] = dx.astype(dx_ref.dtype)
        for k in range(4):
            dw_ref[k:k + 1, :] += jnp.sum(dy * extx[pl.ds(5 + k, tm), :], axis=0, keepdims=True)
        db_ref[...] += jnp.sum(dy, axis=0, keepdims=True)
        extd[tm:tm + 8, :] = extd[0:8, :]

    return pl.pallas_call(
        body, grid=(nb,), name=name,
        in_specs=[pl.BlockSpec((tm, width), lambda i: (nb - 1 - i, cb)),
                  pl.BlockSpec((8, width), lambda i: (jnp.maximum((nb - 1 - i) * hb - 1, 0), cb)),
                  pl.BlockSpec((tm, width), lambda i: (nb - 1 - i, 0)),
                  pl.BlockSpec((4, width), lambda i: (0, 0)), pl.BlockSpec((1, width), lambda i: (0, 0))],
        out_specs=[pl.BlockSpec((tm, width), lambda i: (nb - 1 - i, 0)), pl.BlockSpec((4, width), lambda i: (0, 0)),
                   pl.BlockSpec((1, width), lambda i: (0, 0))],
        out_shape=[S((t, width), BF16), S((4, width), F32), S((1, width), F32)],
        scratch_shapes=[pltpu.VMEM((tm + 8, width), F32), pltpu.VMEM((tm + 8, width), F32)],
        compiler_params=_params(("arbitrary",)),
    )(proj, proj, dact, w, b)


SWA_BQ = 256
SWA_BACK = 128


def _swa_block(q, kw, vw, sinks, blk):
    nq, nk = SWA_BQ, SWA_BQ + SWA_BACK
    r, j = _iota2(nq, nk)
    rel = r // CHUNK + 2 - j // CHUNK
    valid = (rel >= 0) & (rel <= 2) & (blk * (SWA_BQ // CHUNK) + j // CHUNK - 2 >= 0)
    dist = jnp.abs(r + SWA_BACK - j).astype(F32)
    outs = []
    for h in range(4):
        kv = h // 2
        qh = q[:, 64 * h:64 * h + 64]
        kh = kw[:, 64 * kv:64 * kv + 64]
        vh = vw[:, 64 * kv:64 * kv + 64]
        s = _dot_nt(qh, kh) * 0.125 - ALIBI_SLOPES[h] * dist
        s = jnp.where(valid, s, -1e30)
        sink = _pick_col(sinks, h)
        m = jnp.maximum(jnp.max(s, axis=1, keepdims=True), sink)
        e = jnp.exp(s - m)
        den = jnp.sum(e, axis=1, keepdims=True) + jnp.exp(sink - m)
        outs.append(_dot(e / den, vh))
    return jnp.concatenate(outs, axis=1)


def _swa_fwd(proj, sinks, name):
    t = proj.shape[0]
    qb, kb = PC_ATT // 256, PC_ATT // 128 + 2
    win = SWA_BQ + SWA_BACK

    def body(q_ref, k_ref, v_ref, s_ref, o_ref, kp, vp):
        i = pl.program_id(0)

        @pl.when(i == 0)
        def _():
            kp[0:SWA_BACK, :] = jnp.zeros((SWA_BACK, 128), F32)
            vp[0:SWA_BACK, :] = jnp.zeros((SWA_BACK, 128), F32)
            kp[SWA_BACK:, :] = k_ref[...]
            vp[SWA_BACK:, :] = v_ref[...]

        start = pl.multiple_of(i * SWA_BQ, SWA_BQ)
        o = _swa_block(q_ref[...], kp[pl.ds(start, win), :], vp[pl.ds(start, win), :], s_ref[...], i)
        o_ref[...] = o.astype(o_ref.dtype)

    return pl.pallas_call(
        body, grid=(t // SWA_BQ,), name=name,
        in_specs=[pl.BlockSpec((SWA_BQ, 256), lambda i: (i, qb)), pl.BlockSpec((t, 128), lambda i: (0, kb)),
                  pl.BlockSpec((t, 128), lambda i: (0, kb + 1)), pl.BlockSpec((1, LANES), lambda i: (0, 0))],
        out_specs=pl.BlockSpec((SWA_BQ, 256), lambda i: (i, 0)),
        out_shape=S((t, 256), BF16),
        scratch_shapes=[pltpu.VMEM((t + SWA_BACK, 128), F32), pltpu.VMEM((t + SWA_BACK, 128), F32)],
        compiler_params=_params(("arbitrary",)),
    )(proj, proj, proj, sinks)


def _swa_bwd(proj, sinks, dcat, dcol0, name):
    t = proj.shape[0]
    nb = t // SWA_BQ
    qb, kb = PC_ATT // 256, PC_ATT // 128 + 2
    db = dcol0 // 256
    win = SWA_BQ + SWA_BACK

    def body(q_ref, k_ref, v_ref, s_ref, do_ref, dq_ref, dk_ref, dv_ref, ds_ref, kp, vp, dkp, dvp):
        i = pl.program_id(0)

        @pl.when(i == 0)
        def _():
            kp[0:SWA_BACK, :] = jnp.zeros((SWA_BACK, 128), F32)
            vp[0:SWA_BACK, :] = jnp.zeros((SWA_BACK, 128), F32)
            kp[SWA_BACK:, :] = k_ref[...]
            vp[SWA_BACK:, :] = v_ref[...]
            dkp[...] = jnp.zeros_like(dkp)
            dvp[...] = jnp.zeros_like(dvp)
            ds_ref[...] = jnp.zeros_like(ds_ref)

        start = pl.multiple_of(i * SWA_BQ, SWA_BQ)
        _, vjp = jax.vjp(functools.partial(_swa_block, blk=i), q_ref[...], kp[pl.ds(start, win), :],
                         vp[pl.ds(start, win), :], s_ref[...])
        dq, dkw, dvw, dsk = vjp(do_ref[...])
        dq_ref[...] = dq.astype(dq_ref.dtype)
        dkp[pl.ds(start, win), :] += dkw
        dvp[pl.ds(start, win), :] += dvw
        ds_ref[...] += dsk

        @pl.when(i == nb - 1)
        def _():
            dk_ref[...] = dkp[SWA_BACK:, :].astype(dk_ref.dtype)
            dv_ref[...] = dvp[SWA_BACK:, :].astype(dv_ref.dtype)

    return pl.pallas_call(
        body, grid=(nb,), name=name,
        in_specs=[pl.BlockSpec((SWA_BQ, 256), lambda i: (i, qb)), pl.BlockSpec((t, 128), lambda i: (0, kb)),
                  pl.BlockSpec((t, 128), lambda i: (0, kb + 1)), pl.BlockSpec((1, LANES), lambda i: (0, 0)),
                  pl.BlockSpec((SWA_BQ, 256), lambda i: (i, db))],
        out_specs=[pl.BlockSpec((SWA_BQ, 256), lambda i: (i, 0)), pl.BlockSpec((t, 128), lambda i: (0, 0)),
                   pl.BlockSpec((t, 128), lambda i: (0, 0)), pl.BlockSpec((1, LANES), lambda i: (0, 0))],
        out_shape=[S((t, 256), BF16), S((t, 128), BF16), S((t, 128), BF16), S((1, LANES), F32)],
        scratch_shapes=[pltpu.VMEM((t + SWA_BACK, 128), F32) for _ in range(4)],
        compiler_params=_params(("arbitrary",)),
    )(proj, proj, proj, sinks, dcat)


def _ssd_chunk(z, xbc, dt_raw, state, dtb, alog, dsk, nw, bf16_operands=False):
    n = z.shape[0]
    op = (lambda a: a.astype(BF16)) if bf16_operands else (lambda a: a)
    r, c = _iota2(n, n)
    tril = r >= c
    eye = (r == c).astype(F32)
    dt = jax.nn.softplus(dt_raw + dtb)
    acs = _dot(tril.astype(F32), dt * (-jnp.exp(alog)), HI)
    xs, bm, cm = xbc[:, :512], xbc[:, 512:768], xbc[:, 768:1024]
    heads = range(8)
    bg = [bm[:, 128 * g:128 * g + 128] for g in range(2)]
    cg = [cm[:, 128 * g:128 * g + 128] for g in range(2)]
    cb = [_dot_nt(op(cg[g]), op(bg[g])) for g in range(2)]
    dth = [_pick_col(dt, h) for h in heads]
    acol = [_pick_col(acs, h) for h in heads]
    arow = [_col_to_row(a, eye) for a in acol]
    lmat = [jnp.where(tril, jnp.exp(jnp.where(tril, a - b, 0.0)), 0.0) for a, b in zip(acol, arow)]
    xh = [xs[:, 64 * h:64 * h + 64] for h in heads]
    xc = [x * t for x, t in zip(xh, dth)]
    st = [state[64 * h:64 * h + 64, :] for h in heads]
    alast = [_pick_row(a, n - 1) for a in acol]
    y_in = [_dot(op(cb[h // 4] * lmat[h]), op(xc[h])) for h in heads]
    y_st = [_dot_nt(op(cg[h // 4]), op(st[h])) * jnp.exp(acol[h]) for h in heads]
    ys = [y_in[h] + y_st[h] + xh[h] * _pick_col(dsk, h) for h in heads]
    new_states = [st[h] * jnp.exp(alast[h]) + _dot_tn(op(xc[h] * jnp.exp(alast[h] - acol[h])), op(bg[h // 4]))
                  for h in heads]
    gg = jnp.concatenate(ys, axis=1) * jax.nn.silu(z)
    outs = []
    for gi in range(2):
        gv = gg[:, 256 * gi:256 * gi + 256]
        outs.append(gv * lax.rsqrt(jnp.mean(gv * gv, axis=-1, keepdims=True) + EPS))
    return jnp.concatenate(outs, axis=1) * nw, jnp.concatenate(new_states, axis=0)


def _ssd_fwd(proj, xbc, dtb, alog, dsk, nw, name, exchange=None):
    t = proj.shape[0]
    CHUNK = min(SSD_CHUNK, t)
    nc = t // CHUNK

    def body(z_ref, x_ref, dt_ref, dtb_ref, al_ref, d_ref, nw_ref, o_ref, st_ref, state):
        @pl.when(pl.program_id(0) == 0)
        def _():
            state[...] = jnp.zeros_like(state)

        st_ref[0] = state[...]
        o, ns = _ssd_chunk(z_ref[...], x_ref[...], dt_ref[...], state[...], dtb_ref[...], al_ref[...], d_ref[...],
                           nw_ref[...])
        o_ref[...] = o.astype(o_ref.dtype)
        state[...] = ns

    body, x_in, x_out, x_shape, x_sems = _hosted(exchange, 7, 2, nc, body)
    vec = pl.BlockSpec((1, LANES), lambda i: (0, 0))
    return pl.pallas_call(
        body, grid=(nc,), name=name,
        in_specs=[pl.BlockSpec((CHUNK, 512), lambda i: (i, PC_SZ // 512)), pl.BlockSpec((CHUNK, 1024), lambda i: (i, 0)),
                  pl.BlockSpec((CHUNK, 128), lambda i: (i, PC_DT // 128)), vec, vec, vec,
                  pl.BlockSpec((1, 512), lambda i: (0, 0))] + x_in,
        out_specs=[pl.BlockSpec((CHUNK, 512), lambda i: (i, 0)), pl.BlockSpec((1, 512, 128), lambda i: (i, 0, 0))] + x_out,
        out_shape=[S((t, 512), BF16), S((nc, 512, 128), F32)] + x_shape,
        scratch_shapes=[pltpu.VMEM((512, 128), F32)] + x_sems,
        compiler_params=_params(("arbitrary",)),
    )(proj, xbc, proj, dtb, alog, dsk, nw, *([] if exchange is None else exchange.arrays))


def _ssd_bwd(proj, xbc, states, dtb, alog, dsk, nw, dcat, dcol0, name, exchange=None):
    t = proj.shape[0]
    CHUNK = min(SSD_CHUNK, t)
    nc = t // CHUNK
    db = dcol0 // 512

    def body(z_ref, x_ref, dt_ref, st_ref, dtb_ref, al_ref, d_ref, nw_ref, do_ref,
             dz_ref, dx_ref, ddt_ref, gdtb_ref, gal_ref, gd_ref, gnw_ref, dstate):
        @pl.when(pl.program_id(0) == 0)
        def _():
            dstate[...] = jnp.zeros_like(dstate)
            gdtb_ref[...] = jnp.zeros_like(gdtb_ref)
            gal_ref[...] = jnp.zeros_like(gal_ref)
            gd_ref[...] = jnp.zeros_like(gd_ref)
            gnw_ref[...] = jnp.zeros_like(gnw_ref)

        _, vjp = jax.vjp(functools.partial(_ssd_chunk, bf16_operands=True), z_ref[...], x_ref[...], dt_ref[...],
                         st_ref[0], dtb_ref[...], al_ref[...], d_ref[...], nw_ref[...])
        dz, dx, ddt, dst, gdtb, gal, gd, gnw = vjp((do_ref[...], dstate[...]))
        dz_ref[...] = dz.astype(dz_ref.dtype)
        dx_ref[...] = dx
        ddt_ref[...] = ddt.astype(ddt_ref.dtype)
        dstate[...] = dst
        gdtb_ref[...] += gdtb
        gal_ref[...] += gal
        gd_ref[...] += gd
        gnw_ref[...] += gnw

    body, x_in, x_out, x_shape, x_sems = _hosted(exchange, 9, 7, nc, body)
    rev = lambda i: nc - 1 - i
    vec = pl.BlockSpec((1, LANES), lambda i: (0, 0))
    vec512 = pl.BlockSpec((1, 512), lambda i: (0, 0))
    return pl.pallas_call(
        body, grid=(nc,), name=name,
        in_specs=[pl.BlockSpec((CHUNK, 512), lambda i: (rev(i), PC_SZ // 512)),
                  pl.BlockSpec((CHUNK, 1024), lambda i: (rev(i), 0)),
                  pl.BlockSpec((CHUNK, 128), lambda i: (rev(i), PC_DT // 128)),
                  pl.BlockSpec((1, 512, 128), lambda i: (rev(i), 0, 0)), vec, vec, vec, vec512,
                  pl.BlockSpec((CHUNK, 512), lambda i: (rev(i), db))] + x_in,
        out_specs=[pl.BlockSpec((CHUNK, 512), lambda i: (rev(i), 0)), pl.BlockSpec((CHUNK, 1024), lambda i: (rev(i), 0)),
                   pl.BlockSpec((CHUNK, 128), lambda i: (rev(i), 0)), vec, vec, vec, vec512] + x_out,
        out_shape=[S((t, 512), BF16), S((t, 1024), F32), S((t, 128), BF16), S((1, LANES), F32), S((1, LANES), F32),
                   S((1, LANES), F32), S((1, 512), F32)] + x_shape,
        scratch_shapes=[pltpu.VMEM((512, 128), F32)] + x_sems,
        compiler_params=_params(("arbitrary",)),
    )(proj, xbc, proj, states, dtb, alog, dsk, nw, dcat, *([] if exchange is None else exchange.arrays))


SOLVE_PREC = lax.Precision.HIGH


def _unit_lower_inverses(nas, known=None):
    def compute(ns):
        if known is not None:
            return tuple(known)
        n = ns[0].shape[0]
        r, c = _iota2(n, n)
        eye = (r == c).astype(F32)
        tm, pw = [eye + a for a in ns], list(ns)
        for _ in range(n.bit_length() - 2):
            pw = [_dot(p, p, SOLVE_PREC) for p in pw]
            tm = [t + _dot(t, p, SOLVE_PREC) for t, p in zip(tm, pw)]
        return tuple(tm)

    inv = jax.custom_vjp(compute)

    def fwd(ns):
        ts = compute(ns)
        return ts, ts

    def bwd(ts, gs):
        part = [_dot_nt(g, t, SOLVE_PREC) for g, t in zip(gs, ts)]
        return (tuple(_dot_tn(t, p, SOLVE_PREC) for t, p in zip(ts, part)),)

    inv.defvjp(fwd, bwd)
    return inv(nas)


def _gdn_chunk(qkv, z, ba, state, dtb, alog, nw, known_inverses=None):
    n = qkv.shape[0]
    r, c = _iota2(n, n)
    tril = r >= c
    stril = r > c
    eye = (r == c).astype(F32)
    beta_all = jax.nn.sigmoid(ba)
    gcs = _dot(tril.astype(F32), -jnp.exp(alog) * jax.nn.softplus(ba + dtb), HI)
    heads = range(4)
    qh = [qkv[:, 64 * h:64 * h + 64] for h in heads]
    kh = [qkv[:, 256 + 64 * h:256 + 64 * h + 64] for h in heads]
    vh = [qkv[:, 512 + 64 * h:512 + 64 * h + 64] for h in heads]
    qn = [q * lax.rsqrt(jnp.sum(q * q, axis=-1, keepdims=True) + EPS) * 0.125 for q in qh]
    kn = [k * lax.rsqrt(jnp.sum(k * k, axis=-1, keepdims=True) + EPS) for k in kh]
    beta = [_pick_col(beta_all, h) for h in heads]
    gcol = [_pick_col(gcs, 4 + h) for h in heads]
    grow = [_col_to_row(g, eye) for g in gcol]
    decay = [jnp.where(tril, jnp.exp(jnp.where(tril, gc - gr, 0.0)), 0.0) for gc, gr in zip(gcol, grow)]
    kbeta = [k * b for k, b in zip(kn, beta)]
    kk = [_dot_nt(kb, k) for kb, k in zip(kbeta, kn)]
    qk = [_dot_nt(q, k) * dc for q, k, dc in zip(qn, kn, decay)]
    known = None if known_inverses is None else [known_inverses[n * h:n * (h + 1), :] for h in heads]
    tms = _unit_lower_inverses(tuple(-jnp.where(stril, x * dc, 0.0) for x, dc in zip(kk, decay)), known)
    rhs = [jnp.concatenate([v * b, kb * jnp.exp(g)], axis=1) for v, b, kb, g in zip(vh, beta, kbeta, gcol)]
    sol = [_dot(t, x, SOLVE_PREC) for t, x in zip(tms, rhs)]
    st = [state[64 * h:64 * h + 64, :] for h in heads]
    v_new = [s_[:, :64] - _dot(s_[:, 64:], s) for s_, s in zip(sol, st)]
    o = [_dot(q * jnp.exp(g), s) + _dot(x, vn) for q, g, s, x, vn in zip(qn, gcol, st, qk, v_new)]
    glast = [_pick_row(g, n - 1) for g in gcol]
    new_states = [s * jnp.exp(gl) + _dot_tn(k * jnp.exp(gl - g), vn)
                  for s, gl, k, g, vn in zip(st, glast, kn, gcol, v_new)]
    o = [x * lax.rsqrt(jnp.mean(x * x, axis=-1, keepdims=True) + EPS) * nw for x in o]
    outs = [x * jax.nn.silu(z[:, 64 * h:64 * h + 64]) for h, x in zip(heads, o)]
    return jnp.concatenate(outs, axis=1), jnp.concatenate(new_states, axis=0), jnp.concatenate(tms, axis=0)


class _Exchange:
    def __init__(self, arrays, out_shape, n_sems, start, finish):
        self.arrays, self.out_shape, self.n_sems, self.start, self.finish = arrays, out_shape, n_sems, start, finish


def _hosted(exchange, n_in, n_out, grid, body):
    if exchange is None:
        return body, [], [], [], []
    k_in, k_out = len(exchange.arrays), len(exchange.out_shape)
    grid = (grid,) if isinstance(grid, int) else tuple(grid)

    def hosted_body(*refs):
        ins, refs = refs[:n_in + k_in], refs[n_in + k_in:]
        outs, scratch = refs[:n_out + k_out], refs[n_out + k_out:]
        sems = scratch[-2:]
        first, last = True, True
        for axis, steps in enumerate(grid):
            first = first & (pl.program_id(axis) == 0)
            last = last & (pl.program_id(axis) == steps - 1)

        @pl.when(first)
        def _():
            exchange.start(ins[n_in:], outs[n_out:], sems)

        body(*ins[:n_in], *outs[:n_out], *scratch[:-2])

        @pl.when(last)
        def _():
            exchange.finish(ins[n_in:], outs[n_out:], sems)

    return hosted_body, [ANY] * k_in, [ANY] * k_out, list(exchange.out_shape), _sem_pairs(exchange.n_sems)


def _gdn_fwd(proj, qkv, dtb, alog, nw, name, exchange=None):
    t = proj.shape[0]
    CHUNK = min(GDN_CHUNK, t)
    nc = t // CHUNK

    def body(q_ref, z_ref, ba_ref, dtb_ref, al_ref, nw_ref, o_ref, st_ref, inv_ref, state):
        @pl.when(pl.program_id(0) == 0)
        def _():
            state[...] = jnp.zeros_like(state)

        st_ref[0] = state[...]
        o, ns, tms = _gdn_chunk(q_ref[...], z_ref[...], ba_ref[...], state[...], dtb_ref[...], al_ref[...], nw_ref[...])
        o_ref[...] = o.astype(o_ref.dtype)
        inv_ref[0] = tms
        state[...] = ns

    body, x_in, x_out, x_shape, x_sems = _hosted(exchange, 6, 3, nc, body)
    vec = pl.BlockSpec((1, LANES), lambda i: (0, 0))
    per_chunk = pl.BlockSpec((1, 256, 64), lambda i: (i, 0, 0))
    return pl.pallas_call(
        body, grid=(nc,), name=name,
        in_specs=[pl.BlockSpec((CHUNK, 768), lambda i: (i, 0)), pl.BlockSpec((CHUNK, 256), lambda i: (i, PC_GZ // 256)),
                  pl.BlockSpec((CHUNK, 128), lambda i: (i, PC_BA // 128)), vec, vec, pl.BlockSpec((1, 64), lambda i: (0, 0))]
        + x_in,
        out_specs=[pl.BlockSpec((CHUNK, 256), lambda i: (i, 0)), per_chunk,
                   pl.BlockSpec((1, 4 * CHUNK, CHUNK), lambda i: (i, 0, 0))] + x_out,
        out_shape=[S((t, 256), BF16), S((nc, 256, 64), F32), S((nc, 4 * CHUNK, CHUNK), F32)] + x_shape,
        scratch_shapes=[pltpu.VMEM((256, 64), F32)] + x_sems,
        compiler_params=_params(("arbitrary",)),
    )(qkv, proj, proj, dtb, alog, nw, *([] if exchange is None else exchange.arrays))


def _gdn_bwd(proj, qkv, states, inverses, dtb, alog, nw, dcat, dcol0, name, exchange=None):
    t = proj.shape[0]
    CHUNK = min(GDN_CHUNK, t)
    nc = t // CHUNK
    db = dcol0 // 256

    def body(q_ref, z_ref, ba_ref, st_ref, inv_ref, dtb_ref, al_ref, nw_ref, do_ref,
             dq_ref, dz_ref, dba_ref, gdtb_ref, gal_ref, gnw_ref, dstate):
        @pl.when(pl.program_id(0) == 0)
        def _():
            dstate[...] = jnp.zeros_like(dstate)
            gdtb_ref[...] = jnp.zeros_like(gdtb_ref)
            gal_ref[...] = jnp.zeros_like(gal_ref)
            gnw_ref[...] = jnp.zeros_like(gnw_ref)

        def chunk(*operands):
            return _gdn_chunk(*operands, known_inverses=inv_ref[0])[:2]

        _, vjp = jax.vjp(chunk, q_ref[...], z_ref[...], ba_ref[...], st_ref[0], dtb_ref[...], al_ref[...], nw_ref[...])
        dq, dz, dba, dst, gdtb, gal, gnw = vjp((do_ref[...], dstate[...]))
        dq_ref[...] = dq
        dz_ref[...] = dz.astype(dz_ref.dtype)
        dba_ref[...] = dba.astype(dba_ref.dtype)
        dstate[...] = dst
        gdtb_ref[...] += gdtb
        gal_ref[...] += gal
        gnw_ref[...] += gnw

    body, x_in, x_out, x_shape, x_sems = _hosted(exchange, 9, 6, nc, body)
    rev = lambda i: nc - 1 - i
    vec = pl.BlockSpec((1, LANES), lambda i: (0, 0))
    vec64 = pl.BlockSpec((1, 64), lambda i: (0, 0))
    per_chunk = pl.BlockSpec((1, 256, 64), lambda i: (rev(i), 0, 0))
    return pl.pallas_call(
        body, grid=(nc,), name=name,
        in_specs=[pl.BlockSpec((CHUNK, 768), lambda i: (rev(i), 0)),
                  pl.BlockSpec((CHUNK, 256), lambda i: (rev(i), PC_GZ // 256)),
                  pl.BlockSpec((CHUNK, 128), lambda i: (rev(i), PC_BA // 128)),
                  per_chunk, pl.BlockSpec((1, 4 * CHUNK, CHUNK), lambda i: (rev(i), 0, 0)), vec, vec, vec64,
                  pl.BlockSpec((CHUNK, 256), lambda i: (rev(i), db))] + x_in,
        out_specs=[pl.BlockSpec((CHUNK, 768), lambda i: (rev(i), 0)), pl.BlockSpec((CHUNK, 256), lambda i: (rev(i), 0)),
                   pl.BlockSpec((CHUNK, 128), lambda i: (rev(i), 0)), vec, vec, vec64] + x_out,
        out_shape=[S((t, 768), F32), S((t, 256), BF16), S((t, 128), BF16), S((1, LANES), F32), S((1, LANES), F32),
                   S((1, 64), F32)] + x_shape,
        scratch_shapes=[pltpu.VMEM((256, 64), F32)] + x_sems,
        compiler_params=_params(("arbitrary",)),
    )(qkv, proj, proj, states, inverses, dtb, alog, nw, dcat, *([] if exchange is None else exchange.arrays))


def _pad_cols(w):
    z = jnp.zeros((w.shape[0], 120), w.dtype)
    return jnp.concatenate([w[:, 2056:2824], w[:, 2824:3080], w[:, 1024:2048], w[:, 0:512], w[:, 512:1024],
                            w[:, 2048:2056], z, w[:, 3080:3088], z], axis=1)


def _unpad_cols(g):
    return jnp.concatenate([g[:, PC_ATT:PC_ATT + 512], g[:, PC_SZ:PC_SZ + 512], g[:, PC_XBC:PC_XBC + 1024],
                            g[:, PC_DT:PC_DT + 8], g[:, PC_GQKV:PC_GQKV + 768], g[:, PC_GZ:PC_GZ + 256],
                            g[:, PC_BA:PC_BA + 8]], axis=1)


def _vec128(v, at=0):
    return jnp.zeros((1, LANES), F32).at[0, at:at + v.shape[0]].set(v)


def _in_weight(gathered):
    return _pad_cols(jnp.concatenate([gathered[k].reshape(D_MODEL, -1) for k in range(N_CHIPS)], axis=1))


def _matmul_weights(w_in, gathered):
    rows = lambda name: gathered[name].reshape(-1, gathered[name].shape[-1])
    w_out = rows("w_out")
    return dict(
        w_in=w_in,
        w_out=jnp.concatenate([w_out[256:768], w_out[0:256], w_out[768:1024]], axis=0),
        w_gu=_interleave_gu(rows("ffn_w_gate"), rows("ffn_w_up")),
        w_down=rows("ffn_w_down"))


def _small_operands(w, l):
    return dict(
        pre_mix=w["pre_mix_norm"][l][None], post_mix=w["post_mix_norm"][l][None],
        pre_ffn=w["pre_ffn_norm"][l][None], post_ffn=w["post_ffn_norm"][l][None],
        sinks=_vec128(w["attn_sinks"][l]),
        s_cw=w["ssd_conv_w"][l], s_cb=w["ssd_conv_b"][l][None],
        s_dtb=_vec128(w["ssd_dt_bias"][l]), s_alog=_vec128(w["ssd_A_log"][l]), s_d=_vec128(w["ssd_D"][l]),
        s_nw=w["ssd_norm_w"][l][None],
        g_cw=w["gdn_conv_w"][l], g_cb=jnp.zeros((1, 768), F32),
        g_dtb=_vec128(w["gdn_dt_bias"][l], 4), g_alog=_vec128(w["gdn_A_log"][l], 4), g_nw=w["gdn_norm_w"][l][None],
    )


RAW_GRADS = ("w_in_pad", "w_out_cat", "w_gu", "ffn_w_down")
DW_ROWS = 4096


def _local_step(x, target, lw, w_in0, gathers, matmul_weights, reducer):
    saved, landed = [], {}
    xin = x
    h = _prenorm(x, lw[0]["pre_mix"], "prenorm0")
    for l in range(DEPTH):
        p = lw[l]
        carry = (lambda kind: gathers[kind]) if l == 0 else (lambda kind: None)
        proj = _mm_nn(h, w_in0 if l == 0 else p["w_in"], 512, PC_TOT, F32, f"inproj{l}")
        xbc = _conv_fwd(proj, PC_XBC, 1024, p["s_cw"], p["s_cb"], f"ssd_conv{l}")
        gqkv = _conv_fwd(proj, PC_GQKV, 768, p["g_cw"], p["g_cb"], f"gdn_conv{l}")
        att = _swa_fwd(proj, p["sinks"], f"swa{l}")
        ssd, s_states, *landed_s = _ssd_fwd(proj, xbc, p["s_dtb"], p["s_alog"], p["s_d"], p["s_nw"], f"ssd{l}",
                                            carry("ssd"))
        gdn, g_states, g_inv, *landed_g = _gdn_fwd(proj, gqkv, p["g_dtb"], p["g_alog"], p["g_nw"], f"gdn{l}",
                                                   carry("gdn"))
        if l == 0:
            landed.update(ssd=landed_s, gdn=landed_g)
            p.update(matmul_weights(0, landed))
        cat = jnp.concatenate([ssd, att, gdn], axis=1)
        mix = _mm_nn(cat, p["w_out"], 512, 1024, F32, f"outproj{l}")
        x1, h2 = _resid_norm(xin, mix, p["post_mix"], p["pre_ffn"], f"postmix{l}")
        gu, act, *landed_u = _ffn_up(h2, p["w_gu"], f"ffn_gu{l}", exchange=carry("ffn_gu"))
        if l == 0:
            f, *landed_d = _mm_nn(act, p["w_down"], 512, 1024, F32, f"ffn_down{l}", carry("ffn_down"))
            landed.update(ffn_gu=landed_u, ffn_down=landed_d)
            lw[1].update(matmul_weights(1, landed))
        else:
            f = _mm_nn(act, p["w_down"], 512, 1024, F32, f"ffn_down{l}")
        saved.append(dict(xin=xin, h=h, proj=proj, xbc=xbc, gqkv=gqkv, s_states=s_states, g_states=g_states, g_inv=g_inv,
                          cat=cat, mix=mix, x1=x1, h2=h2, gu=gu, act=act, f=f))
        if l + 1 < DEPTH:
            xin, h = _resid_norm(x1, f, p["post_ffn"], lw[l + 1]["pre_mix"], f"postffn{l}")

    g = {k: [None] * DEPTH for k in SMALL + CONV + RAW_GRADS}
    last = saved[-1]
    d_x2, d_f, loss_part, g["post_ffn_norm"][DEPTH - 1] = _resid_loss(
        last["x1"], last["f"], lw[-1]["post_ffn"], target, "loss")
    early, late = [], []
    for l in reversed(range(DEPTH)):
        p, s = lw[l], saved[l]
        d_gu = _ffn_down_bwd(d_f, p["w_down"], s["gu"], f"d_act{l}")
        g["ffn_w_down"][l] = _mm_tn(s["act"], d_f, FF_HALF, 512, DW_ROWS, f"dw_down{l}")
        d_h2 = _mm_nn(d_gu, p["w_gu"], 256, 1024, F32, f"d_h2{l}")
        g["w_gu"][l] = _mm_tn(d_gu, s["h2"], FF_HALF, 512, DW_ROWS, f"dw_gu{l}")
        d_x1, d_mix, g["pre_ffn_norm"][l], g["post_mix_norm"][l] = _resid_norm_bwd(
            s["x1"], s["mix"], d_x2, d_h2, p["post_mix"], p["pre_ffn"], f"d_postmix{l}")
        d_cat = _mm_nt(d_mix, p["w_out"], 512, 1024, F32, f"d_cat{l}")
        g["w_out_cat"][l] = _mm_tn(s["cat"], d_mix, 512, 1024, DW_ROWS, f"dw_out{l}")
        d_q, d_k, d_v, g_sinks = _swa_bwd(s["proj"], p["sinks"], d_cat, 512, f"d_swa{l}")
        d_sz, d_xbc, d_dt, g_dtb, g_alog, g_d, g["ssd_norm_w"][l], *siblings = _ssd_bwd(
            s["proj"], s["xbc"], s["s_states"], p["s_dtb"], p["s_alog"], p["s_d"], p["s_nw"], d_cat, 0, f"d_ssd{l}",
            reducer.exchange(g) if l == 0 else None)
        d_gq, d_gz, d_ba, gg_dtb, gg_alog, g["gdn_norm_w"][l], *landed_b = _gdn_bwd(
            s["proj"], s["gqkv"], s["g_states"], s["g_inv"], p["g_dtb"], p["g_alog"], p["g_nw"], d_cat, 768, f"d_gdn{l}",
            reducer.scatter(siblings) if l == 0 else None)
        if l == 0:
            early = landed_b
        d_xbc_raw, g["ssd_conv_w"][l], g["ssd_conv_b"][l] = _conv_bwd(
            s["proj"], PC_XBC, 1024, p["s_cw"], p["s_cb"], d_xbc, f"d_ssd_conv{l}")
        d_gq_raw, g["gdn_conv_w"][l], _ = _conv_bwd(s["proj"], PC_GQKV, 768, p["g_cw"], p["g_cb"], d_gq, f"d_gdn_conv{l}")
        d_proj = jnp.concatenate([d_gq_raw, d_gz, d_xbc_raw, d_q, d_k, d_v, d_sz, d_dt, d_ba], axis=1)
        g["w_in_pad"][l] = _mm_tn(s["h"], d_proj, 512, PC_TOT // 2, DW_ROWS, f"dw_in{l}")
        if l == 0:
            d_h, *late = _mm_nt(d_proj, p["w_in"], 512, 1024, F32, f"d_h{l}", reducer.late(g))
        else:
            d_h = _mm_nt(d_proj, p["w_in"], 512, 1024, F32, f"d_h{l}")
        g["attn_sinks"][l] = g_sinks[0, :4]
        g["ssd_dt_bias"][l], g["ssd_A_log"][l], g["ssd_D"][l] = g_dtb[0, :8], g_alog[0, :8], g_d[0, :8]
        g["gdn_dt_bias"][l], g["gdn_A_log"][l] = gg_dtb[0, 4:8], gg_alog[0, 4:8]
        if l > 0:
            sp = saved[l - 1]
            d_x2, d_f, g["pre_mix_norm"][l], g["post_ffn_norm"][l - 1] = _resid_norm_bwd(
                s["xin"], sp["f"], d_x1, d_h, lw[l - 1]["post_ffn"], p["pre_mix"], f"d_postffn{l - 1}")
        else:
            grad_x, g["pre_mix_norm"][0] = _prenorm_bwd(s["xin"], d_x1, d_h, p["pre_mix"], "d_prenorm0")

    small = {k: jnp.stack([a.reshape(-1) for a in g[k]], axis=0) for k in SMALL + CONV}
    return loss_part, grad_x, small, {k: g[k] for k in RAW_GRADS}, early, late


BIG = (("w_in", 2), ("w_out", 1), ("ffn_w_gate", 2), ("ffn_w_up", 2), ("ffn_w_down", 1))
CONV = ("ssd_conv_w", "gdn_conv_w")
TRANSPOSED = ("ffn_w_gate", "ffn_w_up")
SMALL = ("pre_mix_norm", "post_mix_norm", "pre_ffn_norm", "post_ffn_norm", "attn_sinks", "ssd_conv_b", "ssd_dt_bias",
         "ssd_A_log", "ssd_D", "ssd_norm_w", "gdn_dt_bias", "gdn_A_log", "gdn_norm_w")


def _row_tile(rows, cap):
    best = rows
    for t in range(8, min(cap, rows) + 1, 8):
        if rows % t == 0:
            best = t
    return best


SMALL_UNIT = 8 * LANES


def _pack_small(vals):
    rows = []
    for a in vals:
        f = a.reshape(-1)
        pad = -f.shape[0] % SMALL_UNIT
        rows.append(jnp.concatenate([f, jnp.zeros((pad,), F32)]).reshape(-1, LANES))
    return jnp.concatenate(rows, axis=0)


def _unpack_small(mat, shapes):
    out, r = [], 0
    for shp in shapes:
        n = math.prod(shp)
        nr = -(-n // SMALL_UNIT) * 8
        out.append(mat[r:r + nr].reshape(-1)[:n].reshape(shp))
        r += nr
    return out


def _place():
    x, y, c = lax.axis_index("x"), lax.axis_index("y"), lax.axis_index("c")
    chips = [(1 - x, y), (x, 1 - y), (1 - x, 1 - y)]
    return x, y, c, chips


ANY = pl.BlockSpec(memory_space=pl.ANY)


def _remote(src, dst, sems, k, to):
    send_sems, recv_sems = sems
    return pltpu.make_async_remote_copy(src_ref=src, dst_ref=dst, send_sem=send_sems.at[k], recv_sem=recv_sems.at[k],
                                        device_id=to, device_id_type=MESH)


def _sem_pairs(n):
    return [pltpu.SemaphoreType.DMA((n,)), pltpu.SemaphoreType.DMA((n,))]


def _run_exchange(exchange, name):
    k = len(exchange.arrays)

    def body(*refs):
        ins, outs, sems = refs[:k], refs[k:-2], refs[-2:]
        exchange.start(ins, outs, sems)
        exchange.finish(ins, outs, sems)

    return pl.pallas_call(
        body, name=name, in_specs=[ANY] * k, out_specs=[ANY] * len(exchange.out_shape),
        out_shape=list(exchange.out_shape), scratch_shapes=_sem_pairs(exchange.n_sems),
    )(*exchange.arrays)


def _gather_exchange(shards):
    n = len(shards)

    def sends(s_refs, g_refs, sems):
        x, y, c, chips = _place()
        me = 2 * x + y
        over_ici = [_remote(s_refs[i].at[c], g_refs[i].at[me, c], sems, 7 * i + j, (px, py, c))
                    for i in range(n) for j, (px, py) in enumerate(chips)]
        return over_ici + [_remote(s_refs[i], g_refs[i].at[me], sems, 7 * i + 6, (x, y, 1 - c)) for i in range(n)]

    def start(s_refs, g_refs, sems):
        for cp in sends(s_refs, g_refs, sems):
            cp.start()

    def finish(s_refs, g_refs, sems):
        x, y, c, chips = _place()
        sib = (x, y, 1 - c)
        passed = []
        for j, (px, py) in enumerate(chips):
            for i in range(n):
                landed = g_refs[i].at[2 * px + py, c]
                _remote(landed, landed, sems, 7 * i + j, (px, py, c)).wait_recv()
                fw = _remote(landed, landed, sems, 7 * i + 3 + j, sib)
                fw.start()
                passed.append(fw)
        for j, (px, py) in enumerate(chips):
            for i in range(n):
                landed = g_refs[i].at[2 * px + py, 1 - c]
                _remote(landed, landed, sems, 7 * i + 3 + j, sib).wait_recv()
        for i in range(n):
            mine = g_refs[i].at[2 * x + y]
            _remote(mine, mine, sems, 7 * i + 6, sib).wait_recv()
        for cp in sends(s_refs, g_refs, sems) + passed:
            cp.wait_send()

    return _Exchange(shards, [S((N_CHIPS,) + a.shape, a.dtype) for a in shards], 7 * n, start, finish)


def _halves_exchange(ds):
    n = len(ds)

    def copies(d_refs, t_refs, sems):
        x, y, c, _ = _place()
        return [_remote(d_refs[i].at[:, 1 - c], t_refs[i], sems, i, (x, y, 1 - c)) for i in range(n)]

    def start(d_refs, t_refs, sems):
        for cp in copies(d_refs, t_refs, sems):
            cp.start()

    def finish(d_refs, t_refs, sems):
        for cp in copies(d_refs, t_refs, sems):
            cp.wait()

    return _Exchange(ds, [S((N_CHIPS,) + a.shape[2:], a.dtype) for a in ds], n, start, finish)


def _scatter_exchange(ps):
    n = len(ps)

    def copies(p_refs, u_refs, sems):
        x, y, c, chips = _place()
        return [_remote(p_refs[i].at[2 * px + py], u_refs[i].at[j], sems, 3 * i + j, (px, py, c))
                for j, (px, py) in enumerate(chips) for i in range(n)]

    def start(p_refs, u_refs, sems):
        for cp in copies(p_refs, u_refs, sems):
            cp.start()

    def finish(p_refs, u_refs, sems):
        for cp in copies(p_refs, u_refs, sems):
            cp.wait()

    return _Exchange(ps, [S((3,) + a.shape[1:], a.dtype) for a in ps], 3 * n, start, finish)


def _join_halves(qs, name):
    n = len(qs)

    def body(*refs):
        o_refs, sems = refs[n:2 * n], refs[2 * n:]
        x, y, c, _ = _place()
        cps = [_remote(o_refs[i].at[:, c], o_refs[i].at[:, c], sems, i, (x, y, 1 - c)) for i in range(n)]
        for cp in cps:
            cp.start()
        for i in range(n):
            other = o_refs[i].at[:, 1 - c]
            _remote(other, other, sems, i, (x, y, 1 - c)).wait_recv()
        for cp in cps:
            cp.wait_send()

    return pl.pallas_call(
        body, name=name, in_specs=[ANY] * n, out_specs=[ANY] * n,
        out_shape=[S(a.shape, a.dtype) for a in qs], input_output_aliases={i: i for i in range(n)},
        scratch_shapes=_sem_pairs(n),
    )(*qs)


def _gather_small(v, name):
    def body(v_ref, o_ref, send_sems, recv_sems, local_sem):
        x, y, c, _ = _place()
        me = 4 * x + 2 * y + c
        mine = pltpu.make_async_copy(v_ref, o_ref.at[me], local_sem)
        mine.start()
        cps = []
        for k in range(1, N_DEV):
            fx, fy, fc = (k >> 2) & 1, (k >> 1) & 1, k & 1
            peer = (x ^ fx, y ^ fy, c ^ fc)
            cps.append(pltpu.make_async_remote_copy(
                src_ref=v_ref, dst_ref=o_ref.at[me], send_sem=send_sems.at[k - 1], recv_sem=recv_sems.at[k - 1],
                device_id=peer, device_id_type=MESH))
        for cp in cps:
            cp.start()
        for k in range(1, N_DEV):
            fx, fy, fc = (k >> 2) & 1, (k >> 1) & 1, k & 1
            dst = o_ref.at[4 * (x ^ fx) + 2 * (y ^ fy) + (c ^ fc)]
            pltpu.make_async_remote_copy(src_ref=dst, dst_ref=dst, send_sem=send_sems.at[k - 1],
                                         recv_sem=recv_sems.at[k - 1], device_id=(x, y, c),
                                         device_id_type=MESH).wait_recv()
        for cp in cps:
            cp.wait_send()
        mine.wait()

    return pl.pallas_call(
        body, name=name, in_specs=[ANY], out_specs=ANY, out_shape=S((N_DEV,) + v.shape, F32),
        scratch_shapes=[pltpu.SemaphoreType.DMA((N_DEV - 1,)), pltpu.SemaphoreType.DMA((N_DEV - 1,)),
                        pltpu.SemaphoreType.DMA],
    )(v)


def _sum_leading(a, name):
    n, rows, cols = a.shape
    tm = _row_tile(rows, 640)

    def body(a_ref, o_ref):
        acc = a_ref[0]
        for k in range(1, n):
            acc = acc + a_ref[k]
        o_ref[...] = acc

    return pl.pallas_call(
        body, grid=(rows // tm,), name=name, in_specs=[pl.BlockSpec((n, tm, cols), lambda i: (0, i, 0))],
        out_specs=pl.BlockSpec((tm, cols), lambda i: (i, 0)), out_shape=S((rows, cols), F32),
        compiler_params=_params(("arbitrary",)),
    )(a)


def _add_sibling(place, a, b, name):
    n, hr, cols = b.shape
    tm = _row_tile(hr, 512)

    def body(place_ref, a_ref, b_ref, o_ref, o16_ref):
        tot = a_ref[0] + b_ref[...]
        o_ref[...] = tot
        o16_ref[...] = tot.astype(BF16)

    spec = pl.BlockSpec((1, tm, cols), lambda k, i, pr: (k, i, 0))
    return pl.pallas_call(
        body, name=name, out_shape=[S((n, hr, cols), F32), S((n, hr, cols), BF16)],
        grid_spec=pltpu.PrefetchScalarGridSpec(
            num_scalar_prefetch=1, grid=(n, hr // tm),
            in_specs=[pl.BlockSpec((1, 1, tm, cols), lambda k, i, pr: (k, pr[0], i, 0)), spec], out_specs=[spec, spec]),
        compiler_params=_params(("arbitrary", "arbitrary")),
    )(place, a, b)


def _add_chips(place, sums, others, layer, into, name):
    _, hr, cols = sums.shape
    tm = _row_tile(hr, 512)

    def body(place_ref, m_ref, o_ref, *rest):
        acc = m_ref[0]
        for k in range(others.shape[0]):
            acc = acc + o_ref[k].astype(F32)
        rest[-1][0, 0] = acc

    kept = [] if into is None else [into]
    return pl.pallas_call(
        body, name=name, out_shape=S((DEPTH, 2, hr, cols), F32),
        grid_spec=pltpu.PrefetchScalarGridSpec(
            num_scalar_prefetch=1, grid=(hr // tm,),
            in_specs=[pl.BlockSpec((1, tm, cols), lambda i, pr: (pr[1], i, 0)),
                      pl.BlockSpec((others.shape[0], tm, cols), lambda i, pr: (0, i, 0))] + [ANY] * len(kept),
            out_specs=pl.BlockSpec((1, 1, tm, cols), lambda i, pr: (layer, pr[0], i, 0))),
        input_output_aliases={3: 0} if kept else {},
        compiler_params=_params(("arbitrary",)),
    )(place, sums, others, *kept)


def _adamw(wt, g, m, v, name):
    shape = wt.shape
    cols = shape[-1]
    rows = math.prod(shape[:-1])
    tm = rows
    for cand in (512, 256, 128, 64, 32, 16, 8):
        if rows % cand == 0:
            tm = cand
            break
    c1 = 1.0 - ADAM_B1 ** ADAM_STEP
    c2 = 1.0 - ADAM_B2 ** ADAM_STEP

    def body(w_ref, g_ref, m_ref, v_ref, d_ref, nm_ref, nv_ref):
        gv = g_ref[...]
        nm = ADAM_B1 * m_ref[...] + (1.0 - ADAM_B1) * gv
        nv = ADAM_B2 * v_ref[...] + (1.0 - ADAM_B2) * (gv * gv)
        d_ref[...] = -ADAM_LR * ((nm / c1) / (jnp.sqrt(nv / c2) + ADAM_EPS) + ADAM_WD * w_ref[...])
        nm_ref[...] = nm
        nv_ref[...] = nv

    spec = pl.BlockSpec((tm, cols), lambda i: (i, 0))
    outs = pl.pallas_call(
        body, grid=(rows // tm,), name=name, in_specs=[spec] * 4, out_specs=[spec] * 3,
        out_shape=[S((rows, cols), F32)] * 3, compiler_params=_params(("arbitrary",)),
    )(*[a.reshape(rows, cols) for a in (wt, g, m, v)])
    return [o.reshape(shape) for o in outs]


WEIGHTS = ('pre_mix_norm', 'post_mix_norm', 'pre_ffn_norm', 'post_ffn_norm', 'w_in', 'w_out', 'attn_sinks', 'ssd_conv_w',
           'ssd_conv_b', 'ssd_dt_bias', 'ssd_A_log', 'ssd_D', 'ssd_norm_w', 'gdn_conv_w', 'gdn_dt_bias', 'gdn_A_log',
           'gdn_norm_w', 'ffn_w_gate', 'ffn_w_up', 'ffn_w_down')


def _chip_piece(i, raw, shape):
    if i == 0:
        g, axis = _unpad_cols(raw["w_in_pad"]), 1
    elif i == 1:
        g = raw["w_out_cat"]
        g, axis = jnp.concatenate([g[512:768], g[0:512], g[768:1024]], axis=0), 0
    elif i in (2, 3):
        g, axis = _split_gu(raw["w_gu"])[i - 2], 0
    else:
        g, axis = raw["ffn_w_down"], 0
    if axis == 1:
        width = shape[2]
        g = jnp.stack([g[:, k * width:(k + 1) * width] for k in range(N_CHIPS)])
    return g.reshape(N_CHIPS, 2, shape[1] // 2, shape[2])


def _step(x, target, wts, ms, vs):
    chip = 2 * lax.axis_index("x") + lax.axis_index("y")
    place = jnp.stack([lax.axis_index("c"), chip]).astype(jnp.int32)
    big_names = [k for k, _ in BIG]
    flip = lambda k, a: jnp.swapaxes(a, 1, 2) if k in TRANSPOSED else a
    wts, ms, vs = ({k: flip(k, a) for k, a in d.items()} for d in (wts, ms, vs))
    big_shapes = [wts[k].shape for k in big_names]
    halves = lambda a: a.reshape((2, a.shape[0] // 2) + a.shape[1:])

    shard = lambda name, l: halves(wts[name][l].astype(BF16))
    first = shard("w_in", 0)
    w_in0 = _in_weight(_run_exchange(_gather_exchange([first]), "gather_w_in0")[0])
    carried_by = {"ssd": [("w_out", 0), ("ffn_w_gate", 0)],
                  "gdn": [("ffn_w_up", 0), ("ffn_w_down", 0), ("w_in", 1), ("w_out", 1)],
                  "ffn_gu": [("ffn_w_gate", 1), ("ffn_w_up", 1)],
                  "ffn_down": [("ffn_w_down", 1)]}
    gathers = {kind: _gather_exchange([shard(*key) for key in keys]) for kind, keys in carried_by.items()}

    def matmul_weights(l, landed):
        gathered = {}
        for kind, keys in carried_by.items():
            for key, got in zip(keys, landed.get(kind, [])):
                if key[1] == l:
                    gathered[key[0]] = got.reshape((N_CHIPS,) + wts[key[0]].shape[1:])
        return _matmul_weights(w_in0 if l == 0 else _in_weight(gathered["w_in"]), gathered)

    conv = _gather_small(_pack_small([wts[k] for k in CONV]), "gather_conv_weights")
    conv = [_unpack_small(conv[2 * k], [wts[n].shape for n in CONV]) for k in range(N_CHIPS)]
    w_all = dict(wts)
    for i, n in enumerate(CONV):
        w_all[n] = jnp.concatenate([conv[k][i] for k in range(N_CHIPS)], axis=2)

    early_keys = [(DEPTH - 1, 0)] + [(l, i) for l in reversed(range(DEPTH)) for i in range(1, len(BIG))]
    late_keys = [(l, 0) for l in range(DEPTH - 1)]

    def pieces_of(keys, g):
        return [_chip_piece(i, {k: g[k][l] for k in RAW_GRADS}, big_shapes[i]) for l, i in keys]

    def add_siblings(tag, pieces, siblings):
        return [_add_sibling(place, p, t, f"add_sibling_{tag}{n}") for n, (p, t) in enumerate(zip(pieces, siblings))]

    class Reducer:
        pieces, sums, late_sums = [], [], []

        def exchange(self, g):
            self.pieces = pieces_of(early_keys, g)
            return _halves_exchange(self.pieces)

        def scatter(self, siblings):
            self.sums = add_siblings("early", self.pieces, siblings)
            return _scatter_exchange([s16 for _, s16 in self.sums])

        def late(self, g):
            pieces = pieces_of(late_keys, g)
            siblings = _run_exchange(_halves_exchange(pieces), "exchange_halves_late")
            self.late_sums = add_siblings("late", pieces, siblings)
            return _scatter_exchange([s16 for _, s16 in self.late_sums])

    reducer = Reducer()

    loss_part, grad_x, small_g, raw, early, late = _local_step(
        x[0], target[0], [_small_operands(w_all, l) for l in range(DEPTH)], w_in0, gathers, matmul_weights, reducer)

    reduced = {}
    for tag, keys, sums, others in (("early", early_keys, reducer.sums, early), ("late", late_keys, reducer.late_sums, late)):
        for n, ((l, i), (s32, _), o) in enumerate(zip(keys, sums, others)):
            reduced[i] = _add_chips(place, s32, o, l, reduced.get(i), f"add_chips_{tag}{n}")
    joined = _join_halves([reduced[i] for i in range(len(BIG))], "join_halves")
    g_all = {k: q.reshape(shp) for k, q, shp in zip(big_names, joined, big_shapes)}

    names = SMALL + CONV
    packed = _pack_small([small_g[k] for k in names] + [loss_part])
    small_sum = _sum_leading(_gather_small(packed, "gather_small_grads"), "add_small")
    vals = _unpack_small(small_sum, [small_g[k].shape for k in names] + [(1, LANES)])
    loss = vals[-1][0, 0]
    for k, v in zip(names, vals[:-1]):
        if k in CONV:
            width = wts[k].shape[2]
            v = lax.dynamic_slice_in_dim(v.reshape(DEPTH, 4, -1), chip * width, width, axis=2)
        g_all[k] = v.reshape(wts[k].shape)

    shapes = [wts[k].shape for k in names]
    d_s, m_s, v_s = _adamw(_pack_small([wts[k] for k in names]), _pack_small([g_all[k] for k in names]),
                           _pack_small([ms[k] for k in names]), _pack_small([vs[k] for k in names]), "adamw_small")
    upd = dict(zip(names, zip(_unpack_small(d_s, shapes), _unpack_small(m_s, shapes), _unpack_small(v_s, shapes))))
    for k in big_names:
        upd[k] = _adamw(wts[k], g_all[k], ms[k], vs[k], f"adamw_{k}")
    return (loss, grad_x[None], *[flip(k, g_all[k]) for k in WEIGHTS], *[flip(k, upd[k][0]) for k in WEIGHTS],
            *[flip(k, upd[k][1]) for k in WEIGHTS], *[flip(k, upd[k][2]) for k in WEIGHTS])


def kernel(x, pre_mix_norm, post_mix_norm, pre_ffn_norm, post_ffn_norm, w_in, w_out, attn_sinks, ssd_conv_w, ssd_conv_b, ssd_dt_bias, ssd_A_log, ssd_D, ssd_norm_w, gdn_conv_w, gdn_dt_bias, gdn_A_log, gdn_norm_w, ffn_w_gate, ffn_w_up, ffn_w_down, loss_target, m_pre_mix_norm, m_post_mix_norm, m_pre_ffn_norm, m_post_ffn_norm, m_w_in, m_w_out, m_attn_sinks, m_ssd_conv_w, m_ssd_conv_b, m_ssd_dt_bias, m_ssd_A_log, m_ssd_D, m_ssd_norm_w, m_gdn_conv_w, m_gdn_dt_bias, m_gdn_A_log, m_gdn_norm_w, m_ffn_w_gate, m_ffn_w_up, m_ffn_w_down, v_pre_mix_norm, v_post_mix_norm, v_pre_ffn_norm, v_post_ffn_norm, v_w_in, v_w_out, v_attn_sinks, v_ssd_conv_w, v_ssd_conv_b, v_ssd_dt_bias, v_ssd_A_log, v_ssd_D, v_ssd_norm_w, v_gdn_conv_w, v_gdn_dt_bias, v_gdn_A_log, v_gdn_norm_w, v_ffn_w_gate, v_ffn_w_up, v_ffn_w_down):
    wts = dict(zip(WEIGHTS, (pre_mix_norm, post_mix_norm, pre_ffn_norm, post_ffn_norm, w_in, w_out, attn_sinks, ssd_conv_w, ssd_conv_b, ssd_dt_bias, ssd_A_log, ssd_D, ssd_norm_w, gdn_conv_w, gdn_dt_bias, gdn_A_log, gdn_norm_w, ffn_w_gate, ffn_w_up, ffn_w_down)))
    ms = dict(zip(WEIGHTS, (m_pre_mix_norm, m_post_mix_norm, m_pre_ffn_norm, m_post_ffn_norm, m_w_in, m_w_out, m_attn_sinks, m_ssd_conv_w, m_ssd_conv_b, m_ssd_dt_bias, m_ssd_A_log, m_ssd_D, m_ssd_norm_w, m_gdn_conv_w, m_gdn_dt_bias, m_gdn_A_log, m_gdn_norm_w, m_ffn_w_gate, m_ffn_w_up, m_ffn_w_down)))
    vs = dict(zip(WEIGHTS, (v_pre_mix_norm, v_post_mix_norm, v_pre_ffn_norm, v_post_ffn_norm, v_w_in, v_w_out, v_attn_sinks, v_ssd_conv_w, v_ssd_conv_b, v_ssd_dt_bias, v_ssd_A_log, v_ssd_D, v_ssd_norm_w, v_gdn_conv_w, v_gdn_dt_bias, v_gdn_A_log, v_gdn_norm_w, v_ffn_w_gate, v_ffn_w_up, v_ffn_w_down)))
    return _step(x, loss_target, wts, ms, vs)
```

```python
import functools
import math

import jax
import jax.numpy as jnp
from jax import lax
from jax.experimental import pallas as pl
from jax.experimental.pallas import tpu as pltpu

F32, BF16 = jnp.float32, jnp.bfloat16
HI = lax.Precision.HIGHEST
MESH = pl.DeviceIdType.MESH
S = jax.ShapeDtypeStruct

D_MODEL = 1024
DEPTH = 2
CHUNK = 64
SSD_CHUNK = 256
GDN_CHUNK = 128
EPS = 1e-6
FF = 2816
N_CHIPS = 4
N_DEV = 8
LANES = 128

VMEM_LIMIT_BYTES = 56 * 1024 * 1024

PC_GQKV, PC_GZ, PC_XBC, PC_ATT, PC_SZ, PC_DT, PC_BA, PC_TOT = 0, 768, 1024, 2048, 2560, 3072, 3200, 3328

ADAM_LR, ADAM_B1, ADAM_B2, ADAM_EPS, ADAM_WD, ADAM_STEP = 0.001, 0.9, 0.999, 1e-08, 0.01, 10

ALIBI_SLOPES = tuple(2.0 ** (-8.0 * (h + 1) / 4) for h in range(4))


def _params(sem=None, **kw):
    if sem is not None:
        kw["dimension_semantics"] = sem
    return pltpu.CompilerParams(vmem_limit_bytes=VMEM_LIMIT_BYTES, **kw)


def _dot(a, b, prec=None):
    return jnp.dot(a, b, precision=prec, preferred_element_type=F32)


def _dot_nt(a, b, prec=None):
    return lax.dot_general(a, b, (((1,), (1,)), ((), ())), precision=prec, preferred_element_type=F32)


def _dot_tn(a, b, prec=None):
    return lax.dot_general(a, b, (((0,), (0,)), ((), ())), precision=prec, preferred_element_type=F32)


def _iota2(n, m):
    return lax.broadcasted_iota(jnp.int32, (n, m), 0), lax.broadcasted_iota(jnp.int32, (n, m), 1)


def _pick_col(arr, idx):
    ci = lax.broadcasted_iota(jnp.int32, arr.shape, 1)
    return jnp.sum(jnp.where(ci == idx, arr, 0.0), axis=1, keepdims=True)


def _pick_row(arr, idx):
    ri = lax.broadcasted_iota(jnp.int32, arr.shape, 0)
    return jnp.sum(jnp.where(ri == idx, arr, 0.0), axis=0, keepdims=True)


def _col_to_row(col, eye):
    return jnp.sum(eye * col, axis=0, keepdims=True)


def _rms(x, w):
    return x * lax.rsqrt(jnp.mean(x * x, axis=-1, keepdims=True) + EPS) * w


def _mm_nn(a, b, tm, tn, out_dtype, name, exchange=None):
    m, k = a.shape
    n = b.shape[1]
    tm, tn = min(tm, m), min(tn, n)
    grid = (n // tn, m // tm)

    def body(a_ref, b_ref, o_ref):
        o_ref[...] = _dot(a_ref[...], b_ref[...]).astype(o_ref.dtype)

    body, x_in, x_out, x_shape, x_sems = _hosted(exchange, 2, 1, grid, body)
    outs = pl.pallas_call(
        body, grid=grid, name=name,
        in_specs=[pl.BlockSpec((tm, k), lambda j, i: (i, 0)), pl.BlockSpec((k, tn), lambda j, i: (0, j))] + x_in,
        out_specs=[pl.BlockSpec((tm, tn), lambda j, i: (i, j))] + x_out,
        out_shape=[S((m, n), out_dtype)] + x_shape, scratch_shapes=x_sems,
        compiler_params=_params(("arbitrary", "arbitrary")),
    )(a, b, *([] if exchange is None else exchange.arrays))
    return outs[0] if exchange is None else outs


def _mm_nt(a, b, tm, tn, out_dtype, name, exchange=None):
    m, k = a.shape
    n = b.shape[0]
    tm, tn = min(tm, m), min(tn, n)
    grid = (n // tn, m // tm)

    def body(a_ref, b_ref, o_ref):
        o_ref[...] = _dot_nt(a_ref[...], b_ref[...]).astype(o_ref.dtype)

    body, x_in, x_out, x_shape, x_sems = _hosted(exchange, 2, 1, grid, body)
    outs = pl.pallas_call(
        body, grid=grid, name=name,
        in_specs=[pl.BlockSpec((tm, k), lambda j, i: (i, 0)), pl.BlockSpec((tn, k), lambda j, i: (j, 0))] + x_in,
        out_specs=[pl.BlockSpec((tm, tn), lambda j, i: (i, j))] + x_out,
        out_shape=[S((m, n), out_dtype)] + x_shape, scratch_shapes=x_sems,
        compiler_params=_params(("arbitrary", "arbitrary")),
    )(a, b, *([] if exchange is None else exchange.arrays))
    return outs[0] if exchange is None else outs


def _mm_tn(a, b, tm, tn, tk, name):
    t, m = a.shape
    n = b.shape[1]
    tm, tn, tk = min(tm, m), min(tn, n), min(tk, t)

    def body(a_ref, b_ref, o_ref):
        part = _dot_tn(a_ref[...], b_ref[...])

        @pl.when(pl.program_id(2) == 0)
        def _():
            o_ref[...] = part

        @pl.when(pl.program_id(2) > 0)
        def _():
            o_ref[...] += part

    return pl.pallas_call(
        body, grid=(m // tm, n // tn, t // tk), name=name,
        in_specs=[pl.BlockSpec((tk, tm), lambda i, j, k: (k, i)), pl.BlockSpec((tk, tn), lambda i, j, k: (k, j))],
        out_specs=pl.BlockSpec((tm, tn), lambda i, j, k: (i, j)),
        out_shape=S((m, n), F32), compiler_params=_params(("arbitrary", "arbitrary", "arbitrary")),
    )(a, b)


def _rowcall(fn, rows, params, row_outs, acc_outs, name, tm=512):
    t = rows[0].shape[0]
    tm = min(tm, t)
    n_in = len(rows) + len(params)
    n_ro = len(row_outs)

    def body(*refs):
        ro, ao = fn(*[r[...] for r in refs[:n_in]])
        for ref, v in zip(refs[n_in:n_in + n_ro], ro):
            ref[...] = v.astype(ref.dtype)
        acc_refs = refs[n_in + n_ro:]
        if acc_refs:
            @pl.when(pl.program_id(0) == 0)
            def _():
                for ref, v in zip(acc_refs, ao):
                    ref[...] = v

            @pl.when(pl.program_id(0) > 0)
            def _():
                for ref, v in zip(acc_refs, ao):
                    ref[...] += v

    in_specs = [pl.BlockSpec((tm, r.shape[1]), lambda i: (i, 0)) for r in rows]
    in_specs += [pl.BlockSpec(p.shape, lambda i: (0, 0)) for p in params]
    out_specs = [pl.BlockSpec((tm, c), lambda i: (i, 0)) for c, _ in row_outs]
    out_specs += [pl.BlockSpec(shape, lambda i: (0, 0)) for shape in acc_outs]
    out_shape = [S((t, c), dt) for c, dt in row_outs] + [S(shape, F32) for shape in acc_outs]
    return pl.pallas_call(
        body, grid=(t // tm,), name=name, in_specs=in_specs, out_specs=out_specs, out_shape=out_shape,
        compiler_params=_params(("arbitrary",)),
    )(*rows, *params)


def _prenorm(x, w, name):
    def fn(x, w):
        return (_rms(x, w),), ()
    return _rowcall(fn, [x], [w], [(D_MODEL, BF16)], [], name)[0]


def _resid_norm(xin, m, w_post, w_next, name):
    def fn(xin, m, w_post, w_next):
        xo = xin + _rms(m, w_post)
        return (xo, _rms(xo, w_next)), ()
    return _rowcall(fn, [xin, m], [w_post, w_next], [(D_MODEL, F32), (D_MODEL, BF16)], [], name)


def _resid_loss(xin, m, w_post, target, name):
    def fn(xin, m, target, w_post):
        r, vjp = jax.vjp(_rms, m, w_post)
        err = xin + r - target
        dy = err * (1.0 / D_MODEL)
        dm, dw = vjp(dy)
        tot = jnp.sum(jnp.sum(err * err, axis=1, keepdims=True), axis=0, keepdims=True) * (0.5 / D_MODEL)
        lane = lax.broadcasted_iota(jnp.int32, (1, LANES), 1)
        return (dy, dm), (jnp.where(lane == 0, tot, 0.0), dw)
    return _rowcall(fn, [xin, m, target], [w_post], [(D_MODEL, F32), (D_MODEL, BF16)],
                    [(1, LANES), (1, D_MODEL)], name)


def _resid_norm_bwd(x_out, m, d_direct, dh, w_post, w_next, name):
    def fn(x_out, m, d_direct, dh, w_post, w_next):
        _, vjp_n = jax.vjp(_rms, x_out, w_next)
        dx, dwn = vjp_n(dh)
        d_total = d_direct + dx
        _, vjp_p = jax.vjp(_rms, m, w_post)
        dm, dwp = vjp_p(d_total)
        return (d_total, dm), (dwn, dwp)
    return _rowcall(fn, [x_out, m, d_direct, dh], [w_post, w_next], [(D_MODEL, F32), (D_MODEL, BF16)],
                    [(1, D_MODEL), (1, D_MODEL)], name)


def _prenorm_bwd(x, d_direct, dh, w, name):
    def fn(x, d_direct, dh, w):
        _, vjp = jax.vjp(_rms, x, w)
        dx, dw = vjp(dh)
        return (d_direct + dx,), (dw,)
    return _rowcall(fn, [x, d_direct, dh], [w], [(D_MODEL, F32)], [(1, D_MODEL)], name)


FF_HALF = FF // 2


def _interleave_gu(gate_t, up_t):
    return jnp.concatenate([gate_t[:FF_HALF], up_t[:FF_HALF], gate_t[FF_HALF:], up_t[FF_HALF:]], axis=0)


def _swiglu_pair(gu):
    n = gu.shape[1] // 2
    return jax.nn.silu(gu[:, :n]) * gu[:, n:]


def _ffn_up(h2, w_gu, name, tm=512, exchange=None):
    t, k = h2.shape
    tm = min(tm, t)
    grid = (2, t // tm)

    def body(a_ref, b_ref, gu_ref, act_ref):
        gu = _dot_nt(a_ref[...], b_ref[...])
        gu_ref[...] = gu.astype(gu_ref.dtype)
        act_ref[...] = _swiglu_pair(gu).astype(act_ref.dtype)

    body, x_in, x_out, x_shape, x_sems = _hosted(exchange, 2, 2, grid, body)
    return pl.pallas_call(
        body, grid=grid, name=name,
        in_specs=[pl.BlockSpec((tm, k), lambda j, i: (i, 0)), pl.BlockSpec((FF, k), lambda j, i: (j, 0))] + x_in,
        out_specs=[pl.BlockSpec((tm, FF), lambda j, i: (i, j)), pl.BlockSpec((tm, FF_HALF), lambda j, i: (i, j))] + x_out,
        out_shape=[S((t, 2 * FF), BF16), S((t, FF), BF16)] + x_shape, scratch_shapes=x_sems,
        compiler_params=_params(("arbitrary", "arbitrary")),
    )(h2, w_gu, *([] if exchange is None else exchange.arrays))


def _ffn_down_bwd(d_f, w_down, gu, name, tm=512):
    t, k = d_f.shape
    tm = min(tm, t)

    def body(a_ref, b_ref, gu_ref, o_ref):
        d_act = _dot_nt(a_ref[...], b_ref[...])
        _, vjp = jax.vjp(_swiglu_pair, gu_ref[...].astype(F32))
        o_ref[...] = vjp(d_act)[0].astype(o_ref.dtype)

    return pl.pallas_call(
        body, grid=(2, t // tm), name=name,
        in_specs=[pl.BlockSpec((tm, k), lambda j, i: (i, 0)), pl.BlockSpec((FF_HALF, k), lambda j, i: (j, 0)),
                  pl.BlockSpec((tm, FF), lambda j, i: (i, j))],
        out_specs=pl.BlockSpec((tm, FF), lambda j, i: (i, j)),
        out_shape=S((t, 2 * FF), BF16), compiler_params=_params(("arbitrary", "arbitrary")),
    )(d_f, w_down, gu)


def _conv_fwd(proj, col0, width, w, b, name, tm=512):
    t = proj.shape[0]
    tm = min(tm, t)
    cb = col0 // width

    def body(x_ref, w_ref, b_ref, o_ref, ext):
        @pl.when(pl.program_id(0) == 0)
        def _():
            ext[0:8, :] = jnp.zeros((8, width), F32)

        ext[8:8 + tm, :] = x_ref[...]
        y = b_ref[...] + w_ref[0:1, :] * ext[pl.ds(5, tm), :]
        for k in range(1, 4):
            y = y + w_ref[k:k + 1, :] * ext[pl.ds(5 + k, tm), :]
        o_ref[...] = jax.nn.silu(y)
        ext[0:8, :] = ext[tm:tm + 8, :]

    return pl.pallas_call(
        body, grid=(t // tm,), name=name,
        in_specs=[pl.BlockSpec((tm, width), lambda i: (i, cb)), pl.BlockSpec((4, width), lambda i: (0, 0)),
                  pl.BlockSpec((1, width), lambda i: (0, 0))],
        out_specs=pl.BlockSpec((tm, width), lambda i: (i, 0)),
        out_shape=S((t, width), F32), scratch_shapes=[pltpu.VMEM((tm + 8, width), F32)],
        compiler_params=_params(("arbitrary",)),
    )(proj, w, b)


def _conv_bwd(proj, col0, width, w, b, dact, name, tm=512):
    t = proj.shape[0]
    tm = min(tm, t)
    nb = t // tm
    cb = col0 // width
    hb = tm // 8

    def body(x_ref, halo_ref, d_ref, w_ref, b_ref, dx_ref, dw_ref, db_ref, extx, extd):
        i = pl.program_id(0)
        blk = nb - 1 - i

        @pl.when(i == 0)
        def _():
            extd[tm:tm + 8, :] = jnp.zeros((8, width), F32)
            dw_ref[...] = jnp.zeros((4, width), F32)
            db_ref[...] = jnp.zeros((1, width), F32)

        extx[0:8, :] = jnp.where(blk == 0, 0.0, halo_ref[...])
        extx[8:8 + tm, :] = x_ref[...]
        y = b_ref[...] + w_ref[0:1, :] * extx[pl.ds(5, tm), :]
        for k in range(1, 4):
            y = y + w_ref[k:k + 1, :] * extx[pl.ds(5 + k, tm), :]
        sig = jax.nn.sigmoid(y)
        dy = d_ref[...] * (sig * (1.0 + y * (1.0 - sig)))
        extd[0:tm, :] = dy
        dx = w_ref[0:1, :] * extd[pl.ds(3, tm), :]
        for k in range(1, 4):
            dx = dx + w_ref[k:k + 1, :] * extd[pl.ds(3 - k, tm), :]
        dx_ref[...] = dx.astype(dx_ref.dtype)
        for k in range(4):
            dw_ref[k:k + 1, :] += jnp.sum(dy * extx[pl.ds(5 + k, tm), :], axis=0, keepdims=True)
        db_ref[...] += jnp.sum(dy, axis=0, keepdims=True)
        extd[tm:tm + 8, :] = extd[0:8, :]

    return pl.pallas_call(
        body, grid=(nb,), name=name,
        in_specs=[pl.BlockSpec((tm, width), lambda i: (nb - 1 - i, cb)),
                  pl.BlockSpec((8, width), lambda i: (jnp.maximum((nb - 1 - i) * hb - 1, 0), cb)),
                  pl.BlockSpec((tm, width), lambda i: (nb - 1 - i, 0)),
                  pl.BlockSpec((4, width), lambda i: (0, 0)), pl.BlockSpec((1, width), lambda i: (0, 0))],
        out_specs=[pl.BlockSpec((tm, width), lambda i: (nb - 1 - i, 0)), pl.BlockSpec((4, width), lambda i: (0, 0)),
                   pl.BlockSpec((1, width), lambda i: (0, 0))],
        out_shape=[S((t, width), BF16), S((4, width), F32), S((1, width), F32)],
        scratch_shapes=[pltpu.VMEM((tm + 8, width), F32), pltpu.VMEM((tm + 8, width), F32)],
        compiler_params=_params(("arbitrary",)),
    )(proj, proj, dact, w, b)


SWA_BQ = 256
SWA_BACK = 128


def _swa_block(q, kw, vw, sinks, blk):
    nq, nk = SWA_BQ, SWA_BQ + SWA_BACK
    r, j = _iota2(nq, nk)
    rel = r // CHUNK + 2 - j // CHUNK
    valid = (rel >= 0) & (rel <= 2) & (blk * (SWA_BQ // CHUNK) + j // CHUNK - 2 >= 0)
    dist = jnp.abs(r + SWA_BACK - j).astype(F32)
    outs = []
    for h in range(4):
        kv = h // 2
        qh = q[:, 64 * h:64 * h + 64]
        kh = kw[:, 64 * kv:64 * kv + 64]
        vh = vw[:, 64 * kv:64 * kv + 64]
        s = _dot_nt(qh, kh) * 0.125 - ALIBI_SLOPES[h] * dist
        s = jnp.where(valid, s, -1e30)
        sink = _pick_col(sinks, h)
        m = jnp.maximum(jnp.max(s, axis=1, keepdims=True), sink)
        e = jnp.exp(s - m)
        den = jnp.sum(e, axis=1, keepdims=True) + jnp.exp(sink - m)
        outs.append(_dot(e / den, vh))
    return jnp.concatenate(outs, axis=1)


def _swa_fwd(proj, sinks, name):
    t = proj.shape[0]
    qb, kb = PC_ATT // 256, PC_ATT // 128 + 2
    win = SWA_BQ + SWA_BACK

    def body(q_ref, k_ref, v_ref, s_ref, o_ref, kp, vp):
        i = pl.program_id(0)

        @pl.when(i == 0)
        def _():
            kp[0:SWA_BACK, :] = jnp.zeros((SWA_BACK, 128), F32)
            vp[0:SWA_BACK, :] = jnp.zeros((SWA_BACK, 128), F32)
            kp[SWA_BACK:, :] = k_ref[...]
            vp[SWA_BACK:, :] = v_ref[...]

        start = pl.multiple_of(i * SWA_BQ, SWA_BQ)
        o = _swa_block(q_ref[...], kp[pl.ds(start, win), :], vp[pl.ds(start, win), :], s_ref[...], i)
        o_ref[...] = o.astype(o_ref.dtype)

    return pl.pallas_call(
        body, grid=(t // SWA_BQ,), name=name,
        in_specs=[pl.BlockSpec((SWA_BQ, 256), lambda i: (i, qb)), pl.BlockSpec((t, 128), lambda i: (0, kb)),
                  pl.BlockSpec((t, 128), lambda i: (0, kb + 1)), pl.BlockSpec((1, LANES), lambda i: (0, 0))],
        out_specs=pl.BlockSpec((SWA_BQ, 256), lambda i: (i, 0)),
        out_shape=S((t, 256), BF16),
        scratch_shapes=[pltpu.VMEM((t + SWA_BACK, 128), F32), pltpu.VMEM((t + SWA_BACK, 128), F32)],
        compiler_params=_params(("arbitrary",)),
    )(proj, proj, proj, sinks)


def _swa_bwd(proj, sinks, dcat, dcol0, name):
    t = proj.shape[0]
    nb = t // SWA_BQ
    qb, kb = PC_ATT // 256, PC_ATT // 128 + 2
    db = dcol0 // 256
    win = SWA_BQ + SWA_BACK

    def body(q_ref, k_ref, v_ref, s_ref, do_ref, dq_ref, dk_ref, dv_ref, ds_ref, kp, vp, dkp, dvp):
        i = pl.program_id(0)

        @pl.when(i == 0)
        def _():
            kp[0:SWA_BACK, :] = jnp.zeros((SWA_BACK, 128), F32)
            vp[0:SWA_BACK, :] = jnp.zeros((SWA_BACK, 128), F32)
            kp[SWA_BACK:, :] = k_ref[...]
            vp[SWA_BACK:, :] = v_ref[...]
            dkp[...] = jnp.zeros_like(dkp)
            dvp[...] = jnp.zeros_like(dvp)
            ds_ref[...] = jnp.zeros_like(ds_ref)

        start = pl.multiple_of(i * SWA_BQ, SWA_BQ)
        _, vjp = jax.vjp(functools.partial(_swa_block, blk=i), q_ref[...], kp[pl.ds(start, win), :],
                         vp[pl.ds(start, win), :], s_ref[...])
        dq, dkw, dvw, dsk = vjp(do_ref[...])
        dq_ref[...] = dq.astype(dq_ref.dtype)
        dkp[pl.ds(start, win), :] += dkw
        dvp[pl.ds(start, win), :] += dvw
        ds_ref[...] += dsk

        @pl.when(i == nb - 1)
        def _():
            dk_ref[...] = dkp[SWA_BACK:, :].astype(dk_ref.dtype)
            dv_ref[...] = dvp[SWA_BACK:, :].astype(dv_ref.dtype)

    return pl.pallas_call(
        body, grid=(nb,), name=name,
        in_specs=[pl.BlockSpec((SWA_BQ, 256), lambda i: (i, qb)), pl.BlockSpec((t, 128), lambda i: (0, kb)),
                  pl.BlockSpec((t, 128), lambda i: (0, kb + 1)), pl.BlockSpec((1, LANES), lambda i: (0, 0)),
                  pl.BlockSpec((SWA_BQ, 256), lambda i: (i, db))],
        out_specs=[pl.BlockSpec((SWA_BQ, 256), lambda i: (i, 0)), pl.BlockSpec((t, 128), lambda i: (0, 0)),
                   pl.BlockSpec((t, 128), lambda i: (0, 0)), pl.BlockSpec((1, LANES), lambda i: (0, 0))],
        out_shape=[S((t, 256), BF16), S((t, 128), BF16), S((t, 128), BF16), S((1, LANES), F32)],
        scratch_shapes=[pltpu.VMEM((t + SWA_BACK, 128), F32) for _ in range(4)],
        compiler_params=_params(("arbitrary",)),
    )(proj, proj, proj, sinks, dcat)


def _ssd_chunk(z, xbc, dt_raw, state, dtb, alog, dsk, nw, bf16_operands=False):
    n = z.shape[0]
    op = (lambda a: a.astype(BF16)) if bf16_operands else (lambda a: a)
    r, c = _iota2(n, n)
    tril = r >= c
    eye = (r == c).astype(F32)
    dt = jax.nn.softplus(dt_raw + dtb)
    acs = _dot(tril.astype(F32), dt * (-jnp.exp(alog)), HI)
    xs, bm, cm = xbc[:, :512], xbc[:, 512:768], xbc[:, 768:1024]
    heads = range(8)
    bg = [bm[:, 128 * g:128 * g + 128] for g in range(2)]
    cg = [cm[:, 128 * g:128 * g + 128] for g in range(2)]
    cb = [_dot_nt(op(cg[g]), op(bg[g])) for g in range(2)]
    dth = [_pick_col(dt, h) for h in heads]
    acol = [_pick_col(acs, h) for h in heads]
    arow = [_col_to_row(a, eye) for a in acol]
    lmat = [jnp.where(tril, jnp.exp(jnp.where(tril, a - b, 0.0)), 0.0) for a, b in zip(acol, arow)]
    xh = [xs[:, 64 * h:64 * h + 64] for h in heads]
    xc = [x * t for x, t in zip(xh, dth)]
    st = [state[64 * h:64 * h + 64, :] for h in heads]
    alast = [_pick_row(a, n - 1) for a in acol]
    y_in = [_dot(op(cb[h // 4] * lmat[h]), op(xc[h])) for h in heads]
    y_st = [_dot_nt(op(cg[h // 4]), op(st[h])) * jnp.exp(acol[h]) for h in heads]
    ys = [y_in[h] + y_st[h] + xh[h] * _pick_col(dsk, h) for h in heads]
    new_states = [st[h] * jnp.exp(alast[h]) + _dot_tn(op(xc[h] * jnp.exp(alast[h] - acol[h])), op(bg[h // 4]))
                  for h in heads]
    gg = jnp.concatenate(ys, axis=1) * jax.nn.silu(z)
    outs = []
    for gi in range(2):
        gv = gg[:, 256 * gi:256 * gi + 256]
        outs.append(gv * lax.rsqrt(jnp.mean(gv * gv, axis=-1, keepdims=True) + EPS))
    return jnp.concatenate(outs, axis=1) * nw, jnp.concatenate(new_states, axis=0)


def _ssd_fwd(proj, xbc, dtb, alog, dsk, nw, name, exchange=None):
    t = proj.shape[0]
    CHUNK = min(SSD_CHUNK, t)
    nc = t // CHUNK

    def body(z_ref, x_ref, dt_ref, dtb_ref, al_ref, d_ref, nw_ref, o_ref, st_ref, state):
        @pl.when(pl.program_id(0) == 0)
        def _():
            state[...] = jnp.zeros_like(state)

        st_ref[0] = state[...]
        o, ns = _ssd_chunk(z_ref[...], x_ref[...], dt_ref[...], state[...], dtb_ref[...], al_ref[...], d_ref[...],
                           nw_ref[...])
        o_ref[...] = o.astype(o_ref.dtype)
        state[...] = ns

    body, x_in, x_out, x_shape, x_sems = _hosted(exchange, 7, 2, nc, body)
    vec = pl.BlockSpec((1, LANES), lambda i: (0, 0))
    return pl.pallas_call(
        body, grid=(nc,), name=name,
        in_specs=[pl.BlockSpec((CHUNK, 512), lambda i: (i, PC_SZ // 512)), pl.BlockSpec((CHUNK, 1024), lambda i: (i, 0)),
                  pl.BlockSpec((CHUNK, 128), lambda i: (i, PC_DT // 128)), vec, vec, vec,
                  pl.BlockSpec((1, 512), lambda i: (0, 0))] + x_in,
        out_specs=[pl.BlockSpec((CHUNK, 512), lambda i: (i, 0)), pl.BlockSpec((1, 512, 128), lambda i: (i, 0, 0))] + x_out,
        out_shape=[S((t, 512), BF16), S((nc, 512, 128), F32)] + x_shape,
        scratch_shapes=[pltpu.VMEM((512, 128), F32)] + x_sems,
        compiler_params=_params(("arbitrary",)),
    )(proj, xbc, proj, dtb, alog, dsk, nw, *([] if exchange is None else exchange.arrays))


def _ssd_bwd(proj, xbc, states, dtb, alog, dsk, nw, dcat, dcol0, name, exchange=None):
    t = proj.shape[0]
    CHUNK = min(SSD_CHUNK, t)
    nc = t // CHUNK
    db = dcol0 // 512

    def body(z_ref, x_ref, dt_ref, st_ref, dtb_ref, al_ref, d_ref, nw_ref, do_ref,
             dz_ref, dx_ref, ddt_ref, gdtb_ref, gal_ref, gd_ref, gnw_ref, dstate):
        @pl.when(pl.program_id(0) == 0)
        def _():
            dstate[...] = jnp.zeros_like(dstate)
            gdtb_ref[...] = jnp.zeros_like(gdtb_ref)
            gal_ref[...] = jnp.zeros_like(gal_ref)
            gd_ref[...] = jnp.zeros_like(gd_ref)
            gnw_ref[...] = jnp.zeros_like(gnw_ref)

        _, vjp = jax.vjp(functools.partial(_ssd_chunk, bf16_operands=True), z_ref[...], x_ref[...], dt_ref[...],
                         st_ref[0], dtb_ref[...], al_ref[...], d_ref[...], nw_ref[...])
        dz, dx, ddt, dst, gdtb, gal, gd, gnw = vjp((do_ref[...], dstate[...]))
        dz_ref[...] = dz.astype(dz_ref.dtype)
        dx_ref[...] = dx
        ddt_ref[...] = ddt.astype(ddt_ref.dtype)
        dstate[...] = dst
        gdtb_ref[...] += gdtb
        gal_ref[...] += gal
        gd_ref[...] += gd
        gnw_ref[...] += gnw

    body, x_in, x_out, x_shape, x_sems = _hosted(exchange, 9, 7, nc, body)
    rev = lambda i: nc - 1 - i
    vec = pl.BlockSpec((1, LANES), lambda i: (0, 0))
    vec512 = pl.BlockSpec((1, 512), lambda i: (0, 0))
    return pl.pallas_call(
        body, grid=(nc,), name=name,
        in_specs=[pl.BlockSpec((CHUNK, 512), lambda i: (rev(i), PC_SZ // 512)),
                  pl.BlockSpec((CHUNK, 1024), lambda i: (rev(i), 0)),
                  pl.BlockSpec((CHUNK, 128), lambda i: (rev(i), PC_DT // 128)),
                  pl.BlockSpec((1, 512, 128), lambda i: (rev(i), 0, 0)), vec, vec, vec, vec512,
                  pl.BlockSpec((CHUNK, 512), lambda i: (rev(i), db))] + x_in,
        out_specs=[pl.BlockSpec((CHUNK, 512), lambda i: (rev(i), 0)), pl.BlockSpec((CHUNK, 1024), lambda i: (rev(i), 0)),
                   pl.BlockSpec((CHUNK, 128), lambda i: (rev(i), 0)), vec, vec, vec, vec512] + x_out,
        out_shape=[S((t, 512), BF16), S((t, 1024), F32), S((t, 128), BF16), S((1, LANES), F32), S((1, LANES), F32),
                   S((1, LANES), F32), S((1, 512), F32)] + x_shape,
        scratch_shapes=[pltpu.VMEM((512, 128), F32)] + x_sems,
        compiler_params=_params(("arbitrary",)),
    )(proj, xbc, proj, states, dtb, alog, dsk, nw, dcat, *([] if exchange is None else exchange.arrays))


SOLVE_PREC = lax.Precision.HIGH


def _unit_lower_inverses(nas, known=None):
    def compute(ns):
        if known is not None:
            return tuple(known)
        n = ns[0].shape[0]
        r, c = _iota2(n, n)
        eye = (r == c).astype(F32)
        tm, pw = [eye + a for a in ns], list(ns)
        for _ in range(n.bit_length() - 2):
            pw = [_dot(p, p, SOLVE_PREC) for p in pw]
            tm = [t + _dot(t, p, SOLVE_PREC) for t, p in zip(tm, pw)]
        return tuple(tm)

    inv = jax.custom_vjp(compute)

    def fwd(ns):
        ts = compute(ns)
        return ts, ts

    def bwd(ts, gs):
        part = [_dot_nt(g, t, SOLVE_PREC) for g, t in zip(gs, ts)]
        return (tuple(_dot_tn(t, p, SOLVE_PREC) for t, p in zip(ts, part)),)

    inv.defvjp(fwd, bwd)
    return inv(nas)


def _gdn_chunk(qkv, z, ba, state, dtb, alog, nw, known_inverses=None):
    n = qkv.shape[0]
    r, c = _iota2(n, n)
    tril = r >= c
    stril = r > c
    eye = (r == c).astype(F32)
    beta_all = jax.nn.sigmoid(ba)
    gcs = _dot(tril.astype(F32), -jnp.exp(alog) * jax.nn.softplus(ba + dtb), HI)
    heads = range(4)
    qh = [qkv[:, 64 * h:64 * h + 64] for h in heads]
    kh = [qkv[:, 256 + 64 * h:256 + 64 * h + 64] for h in heads]
    vh = [qkv[:, 512 + 64 * h:512 + 64 * h + 64] for h in heads]
    qn = [q * lax.rsqrt(jnp.sum(q * q, axis=-1, keepdims=True) + EPS) * 0.125 for q in qh]
    kn = [k * lax.rsqrt(jnp.sum(k * k, axis=-1, keepdims=True) + EPS) for k in kh]
    beta = [_pick_col(beta_all, h) for h in heads]
    gcol = [_pick_col(gcs, 4 + h) for h in heads]
    grow = [_col_to_row(g, eye) for g in gcol]
    decay = [jnp.where(tril, jnp.exp(jnp.where(tril, gc - gr, 0.0)), 0.0) for gc, gr in zip(gcol, grow)]
    kbeta = [k * b for k, b in zip(kn, beta)]
    kk = [_dot_nt(kb, k) for kb, k in zip(kbeta, kn)]
    qk = [_dot_nt(q, k) * dc for q, k, dc in zip(qn, kn, decay)]
    known = None if known_inverses is None else [known_inverses[n * h:n * (h + 1), :] for h in heads]
    tms = _unit_lower_inverses(tuple(-jnp.where(stril, x * dc, 0.0) for x, dc in zip(kk, decay)), known)
    rhs = [jnp.concatenate([v * b, kb * jnp.exp(g)], axis=1) for v, b, kb, g in zip(vh, beta, kbeta, gcol)]
    sol = [_dot(t, x, SOLVE_PREC) for t, x in zip(tms, rhs)]
    st = [state[64 * h:64 * h + 64, :] for h in heads]
    v_new = [s_[:, :64] - _dot(s_[:, 64:], s) for s_, s in zip(sol, st)]
    o = [_dot(q * jnp.exp(g), s) + _dot(x, vn) for q, g, s, x, vn in zip(qn, gcol, st, qk, v_new)]
    glast = [_pick_row(g, n - 1) for g in gcol]
    new_states = [s * jnp.exp(gl) + _dot_tn(k * jnp.exp(gl - g), vn)
                  for s, gl, k, g, vn in zip(st, glast, kn, gcol, v_new)]
    o = [x * lax.rsqrt(jnp.mean(x * x, axis=-1, keepdims=True) + EPS) * nw for x in o]
    outs = [x * jax.nn.silu(z[:, 64 * h:64 * h + 64]) for h, x in zip(heads, o)]
    return jnp.concatenate(outs, axis=1), jnp.concatenate(new_states, axis=0), jnp.concatenate(tms, axis=0)


class _Exchange:
    def __init__(self, arrays, out_shape, n_sems, start, finish):
        self.arrays, self.out_shape, self.n_sems, self.start, self.finish = arrays, out_shape, n_sems, start, finish


def _hosted(exchange, n_in, n_out, grid, body):
    if exchange is None:
        return body, [], [], [], []
    k_in, k_out = len(exchange.arrays), len(exchange.out_shape)
    grid = (grid,) if isinstance(grid, int) else tuple(grid)

    def hosted_body(*refs):
        ins, refs = refs[:n_in + k_in], refs[n_in + k_in:]
        outs, scratch = refs[:n_out + k_out], refs[n_out + k_out:]
        sems = scratch[-2:]
        first, last = True, True
        for axis, steps in enumerate(grid):
            first = first & (pl.program_id(axis) == 0)
            last = last & (pl.program_id(axis) == steps - 1)

        @pl.when(first)
        def _():
            exchange.start(ins[n_in:], outs[n_out:], sems)

        body(*ins[:n_in], *outs[:n_out], *scratch[:-2])

        @pl.when(last)
        def _():
            exchange.finish(ins[n_in:], outs[n_out:], sems)

    return hosted_body, [ANY] * k_in, [ANY] * k_out, list(exchange.out_shape), _sem_pairs(exchange.n_sems)


def _gdn_fwd(proj, qkv, dtb, alog, nw, name, exchange=None):
    t = proj.shape[0]
    CHUNK = min(GDN_CHUNK, t)
    nc = t // CHUNK

    def body(q_ref, z_ref, ba_ref, dtb_ref, al_ref, nw_ref, o_ref, st_ref, inv_ref, state):
        @pl.when(pl.program_id(0) == 0)
        def _():
            state[...] = jnp.zeros_like(state)

        st_ref[0] = state[...]
        o, ns, tms = _gdn_chunk(q_ref[...], z_ref[...], ba_ref[...], state[...], dtb_ref[...], al_ref[...], nw_ref[...])
        o_ref[...] = o.astype(o_ref.dtype)
        inv_ref[0] = tms
        state[...] = ns

    body, x_in, x_out, x_shape, x_sems = _hosted(exchange, 6, 3, nc, body)
    vec = pl.BlockSpec((1, LANES), lambda i: (0, 0))
    per_chunk = pl.BlockSpec((1, 256, 64), lambda i: (i, 0, 0))
    return pl.pallas_call(
        body, grid=(nc,), name=name,
        in_specs=[pl.BlockSpec((CHUNK, 768), lambda i: (i, 0)), pl.BlockSpec((CHUNK, 256), lambda i: (i, PC_GZ // 256)),
                  pl.BlockSpec((CHUNK, 128), lambda i: (i, PC_BA // 128)), vec, vec, pl.BlockSpec((1, 64), lambda i: (0, 0))]
        + x_in,
        out_specs=[pl.BlockSpec((CHUNK, 256), lambda i: (i, 0)), per_chunk,
                   pl.BlockSpec((1, 4 * CHUNK, CHUNK), lambda i: (i, 0, 0))] + x_out,
        out_shape=[S((t, 256), BF16), S((nc, 256, 64), F32), S((nc, 4 * CHUNK, CHUNK), F32)] + x_shape,
        scratch_shapes=[pltpu.VMEM((256, 64), F32)] + x_sems,
        compiler_params=_params(("arbitrary",)),
    )(qkv, proj, proj, dtb, alog, nw, *([] if exchange is None else exchange.arrays))


def _gdn_bwd(proj, qkv, states, inverses, dtb, alog, nw, dcat, dcol0, name, exchange=None):
    t = proj.shape[0]
    CHUNK = min(GDN_CHUNK, t)
    nc = t // CHUNK
    db = dcol0 // 256

    def body(q_ref, z_ref, ba_ref, st_ref, inv_ref, dtb_ref, al_ref, nw_ref, do_ref,
             dq_ref, dz_ref, dba_ref, gdtb_ref, gal_ref, gnw_ref, dstate):
        @pl.when(pl.program_id(0) == 0)
        def _():
            dstate[...] = jnp.zeros_like(dstate)
            gdtb_ref[...] = jnp.zeros_like(gdtb_ref)
            gal_ref[...] = jnp.zeros_like(gal_ref)
            gnw_ref[...] = jnp.zeros_like(gnw_ref)

        def chunk(*operands):
            return _gdn_chunk(*operands, known_inverses=inv_ref[0])[:2]

        _, vjp = jax.vjp(chunk, q_ref[...], z_ref[...], ba_ref[...], st_ref[0], dtb_ref[...], al_ref[...], nw_ref[...])
        dq, dz, dba, dst, gdtb, gal, gnw = vjp((do_ref[...], dstate[...]))
        dq_ref[...] = dq
        dz_ref[...] = dz.astype(dz_ref.dtype)
        dba_ref[...] = dba.astype(dba_ref.dtype)
        dstate[...] = dst
        gdtb_ref[...] += gdtb
        gal_ref[...] += gal
        gnw_ref[...] += gnw

    body, x_in, x_out, x_shape, x_sems = _hosted(exchange, 9, 6, nc, body)
    rev = lambda i: nc - 1 - i
    vec = pl.BlockSpec((1, LANES), lambda i: (0, 0))
    vec64 = pl.BlockSpec((1, 64), lambda i: (0, 0))
    per_chunk = pl.BlockSpec((1, 256, 64), lambda i: (rev(i), 0, 0))
    return pl.pallas_call(
        body, grid=(nc,), name=name,
        in_specs=[pl.BlockSpec((CHUNK, 768), lambda i: (rev(i), 0)),
                  pl.BlockSpec((CHUNK, 256), lambda i: (rev(i), PC_GZ // 256)),
                  pl.BlockSpec((CHUNK, 128), lambda i: (rev(i), PC_BA // 128)),
                  per_chunk, pl.BlockSpec((1, 4 * CHUNK, CHUNK), lambda i: (rev(i), 0, 0)), vec, vec, vec64,
                  pl.BlockSpec((CHUNK, 256), lambda i: (rev(i), db))] + x_in,
        out_specs=[pl.BlockSpec((CHUNK, 768), lambda i: (rev(i), 0)), pl.BlockSpec((CHUNK, 256), lambda i: (rev(i), 0)),
                   pl.BlockSpec((CHUNK, 128), lambda i: (rev(i), 0)), vec, vec, vec64] + x_out,
        out_shape=[S((t, 768), F32), S((t, 256), BF16), S((t, 128), BF16), S((1, LANES), F32), S((1, LANES), F32),
                   S((1, 64), F32)] + x_shape,
        scratch_shapes=[pltpu.VMEM((256, 64), F32)] + x_sems,
        compiler_params=_params(("arbitrary",)),
    )(qkv, proj, proj, states, inverses, dtb, alog, nw, dcat, *([] if exchange is None else exchange.arrays))


def _pad_cols(w):
    z = jnp.zeros((w.shape[0], 120), w.dtype)
    return jnp.concatenate([w[:, 2056:2824], w[:, 2824:3080], w[:, 1024:2048], w[:, 0:512], w[:, 512:1024],
                            w[:, 2048:2056], z, w[:, 3080:3088], z], axis=1)


def _unpad_cols(g):
    return jnp.concatenate([g[:, PC_ATT:PC_ATT + 512], g[:, PC_SZ:PC_SZ + 512], g[:, PC_XBC:PC_XBC + 1024],
                            g[:, PC_DT:PC_DT + 8], g[:, PC_GQKV:PC_GQKV + 768], g[:, PC_GZ:PC_GZ + 256],
                            g[:, PC_BA:PC_BA + 8]], axis=1)


def _vec128(v, at=0):
    return jnp.zeros((1, LANES), F32).at[0, at:at + v.shape[0]].set(v)


def _in_weight(gathered):
    return _pad_cols(jnp.concatenate([gathered[k].reshape(D_MODEL, -1) for k in range(N_CHIPS)], axis=1))


def _matmul_weights(w_in, gathered):
    rows = lambda name: gathered[name].reshape(-1, gathered[name].shape[-1])
    w_out = rows("w_out")
    return dict(
        w_in=w_in,
        w_out=jnp.concatenate([w_out[256:768], w_out[0:256], w_out[768:1024]], axis=0),
        w_gu=_interleave_gu(rows("ffn_w_gate"), rows("ffn_w_up")),
        w_down=rows("ffn_w_down"))


def _small_operands(w, l):
    return dict(
        pre_mix=w["pre_mix_norm"][l][None], post_mix=w["post_mix_norm"][l][None],
        pre_ffn=w["pre_ffn_norm"][l][None], post_ffn=w["post_ffn_norm"][l][None],
        sinks=_vec128(w["attn_sinks"][l]),
        s_cw=w["ssd_conv_w"][l], s_cb=w["ssd_conv_b"][l][None],
        s_dtb=_vec128(w["ssd_dt_bias"][l]), s_alog=_vec128(w["ssd_A_log"][l]), s_d=_vec128(w["ssd_D"][l]),
        s_nw=w["ssd_norm_w"][l][None],
        g_cw=w["gdn_conv_w"][l], g_cb=jnp.zeros((1, 768), F32),
        g_dtb=_vec128(w["gdn_dt_bias"][l], 4), g_alog=_vec128(w["gdn_A_log"][l], 4), g_nw=w["gdn_norm_w"][l][None],
    )


RAW_GRADS = ("w_in_pad", "w_out_cat", "w_gu", "ffn_w_down")
DW_ROWS = 4096


def _local_step(x, target, lw, w_in0, gathers, matmul_weights, reducer):
    saved, landed = [], {}
    xin = x
    h = _prenorm(x, lw[0]["pre_mix"], "prenorm0")
    for l in range(DEPTH):
        p = lw[l]
        carry = (lambda kind: gathers[kind]) if l == 0 else (lambda kind: None)
        proj = _mm_nn(h, w_in0 if l == 0 else p["w_in"], 512, PC_TOT, F32, f"inproj{l}")
        xbc = _conv_fwd(proj, PC_XBC, 1024, p["s_cw"], p["s_cb"], f"ssd_conv{l}")
        gqkv = _conv_fwd(proj, PC_GQKV, 768, p["g_cw"], p["g_cb"], f"gdn_conv{l}")
        att = _swa_fwd(proj, p["sinks"], f"swa{l}")
        ssd, s_states, *landed_s = _ssd_fwd(proj, xbc, p["s_dtb"], p["s_alog"], p["s_d"], p["s_nw"], f"ssd{l}",
                                            carry("ssd"))
        gdn, g_states, g_inv, *landed_g = _gdn_fwd(proj, gqkv, p["g_dtb"], p["g_alog"], p["g_nw"], f"gdn{l}",
                                                   carry("gdn"))
        if l == 0:
            landed.update(ssd=landed_s, gdn=landed_g)
            p.update(matmul_weights(0, landed))
        cat = jnp.concatenate([ssd, att, gdn], axis=1)
        mix = _mm_nn(cat, p["w_out"], 512, 1024, F32, f"outproj{l}")
        x1, h2 = _resid_norm(xin, mix, p["post_mix"], p["pre_ffn"], f"postmix{l}")
        gu, act, *landed_u = _ffn_up(h2, p["w_gu"], f"ffn_gu{l}", exchange=carry("ffn_gu"))
        if l == 0:
            f, *landed_d = _mm_nn(act, p["w_down"], 512, 1024, F32, f"ffn_down{l}", carry("ffn_down"))
            landed.update(ffn_gu=landed_u, ffn_down=landed_d)
            lw[1].update(matmul_weights(1, landed))
        else:
            f = _mm_nn(act, p["w_down"], 512, 1024, F32, f"ffn_down{l}")
        saved.append(dict(xin=xin, h=h, proj=proj, xbc=xbc, gqkv=gqkv, s_states=s_states, g_states=g_states, g_inv=g_inv,
                          cat=cat, mix=mix, x1=x1, h2=h2, gu=gu, act=act, f=f))
        if l + 1 < DEPTH:
            xin, h = _resid_norm(x1, f, p["post_ffn"], lw[l + 1]["pre_mix"], f"postffn{l}")

    g = {k: [None] * DEPTH for k in SMALL + CONV + RAW_GRADS}
    last = saved[-1]
    d_x2, d_f, loss_part, g["post_ffn_norm"][DEPTH - 1] = _resid_loss(
        last["x1"], last["f"], lw[-1]["post_ffn"], target, "loss")
    early, late = [], []
    for l in reversed(range(DEPTH)):
        p, s = lw[l], saved[l]
        d_gu = _ffn_down_bwd(d_f, p["w_down"], s["gu"], f"d_act{l}")
        g["ffn_w_down"][l] = _mm_tn(s["act"], d_f, FF_HALF, 512, DW_ROWS, f"dw_down{l}")
        d_h2 = _mm_nn(d_gu, p["w_gu"], 512, 1024, F32, f"d_h2{l}")
        g["w_gu"][l] = _mm_tn(d_gu, s["h2"], FF_HALF, 512, DW_ROWS, f"dw_gu{l}")
        d_x1, d_mix, g["pre_ffn_norm"][l], g["post_mix_norm"][l] = _resid_norm_bwd(
            s["x1"], s["mix"], d_x2, d_h2, p["post_mix"], p["pre_ffn"], f"d_postmix{l}")
        d_cat = _mm_nt(d_mix, p["w_out"], 512, 1024, F32, f"d_cat{l}")
        g["w_out_cat"][l] = _mm_tn(s["cat"], d_mix, 512, 1024, DW_ROWS, f"dw_out{l}")
        d_q, d_k, d_v, g_sinks = _swa_bwd(s["proj"], p["sinks"], d_cat, 512, f"d_swa{l}")
        d_sz, d_xbc, d_dt, g_dtb, g_alog, g_d, g["ssd_norm_w"][l], *siblings = _ssd_bwd(
            s["proj"], s["xbc"], s["s_states"], p["s_dtb"], p["s_alog"], p["s_d"], p["s_nw"], d_cat, 0, f"d_ssd{l}",
            reducer.exchange(g) if l == 0 else None)
        d_gq, d_gz, d_ba, gg_dtb, gg_alog, g["gdn_norm_w"][l], *landed_b = _gdn_bwd(
            s["proj"], s["gqkv"], s["g_states"], s["g_inv"], p["g_dtb"], p["g_alog"], p["g_nw"], d_cat, 768, f"d_gdn{l}",
            reducer.scatter(siblings) if l == 0 else None)
        if l == 0:
            early = landed_b
        d_xbc_raw, g["ssd_conv_w"][l], g["ssd_conv_b"][l] = _conv_bwd(
            s["proj"], PC_XBC, 1024, p["s_cw"], p["s_cb"], d_xbc, f"d_ssd_conv{l}")
        d_gq_raw, g["gdn_conv_w"][l], _ = _conv_bwd(s["proj"], PC_GQKV, 768, p["g_cw"], p["g_cb"], d_gq, f"d_gdn_conv{l}")
        d_proj = jnp.concatenate([d_gq_raw, d_gz, d_xbc_raw, d_q, d_k, d_v, d_sz, d_dt, d_ba], axis=1)
        g["w_in_pad"][l] = _mm_tn(s["h"], d_proj, 512, PC_TOT // 2, DW_ROWS, f"dw_in{l}")
        if l == 0:
            d_h, *late = _mm_nt(d_proj, p["w_in"], 512, 1024, F32, f"d_h{l}", reducer.late(g))
        else:
            d_h = _mm_nt(d_proj, p["w_in"], 512, 1024, F32, f"d_h{l}")
        g["attn_sinks"][l] = g_sinks[0, :4]
        g["ssd_dt_bias"][l], g["ssd_A_log"][l], g["ssd_D"][l] = g_dtb[0, :8], g_alog[0, :8], g_d[0, :8]
        g["gdn_dt_bias"][l], g["gdn_A_log"][l] = gg_dtb[0, 4:8], gg_alog[0, 4:8]
        if l > 0:
            sp = saved[l - 1]
            d_x2, d_f, g["pre_mix_norm"][l], g["post_ffn_norm"][l - 1] = _resid_norm_bwd(
                s["xin"], sp["f"], d_x1, d_h, lw[l - 1]["post_ffn"], p["pre_mix"], f"d_postffn{l - 1}")
        else:
            grad_x, g["pre_mix_norm"][0] = _prenorm_bwd(s["xin"], d_x1, d_h, p["pre_mix"], "d_prenorm0")

    small = {k: jnp.stack([a.reshape(-1) for a in g[k]], axis=0) for k in SMALL + CONV}
    return loss_part, grad_x, small, {k: g[k] for k in RAW_GRADS}, early, late


BIG = (("w_in", 2), ("w_out", 1), ("ffn_w_gate", 2), ("ffn_w_up", 2), ("ffn_w_down", 1))
CONV = ("ssd_conv_w", "gdn_conv_w")
TRANSPOSED = ("ffn_w_gate", "ffn_w_up")
SMALL = ("pre_mix_norm", "post_mix_norm", "pre_ffn_norm", "post_ffn_norm", "attn_sinks", "ssd_conv_b", "ssd_dt_bias",
         "ssd_A_log", "ssd_D", "ssd_norm_w", "gdn_dt_bias", "gdn_A_log", "gdn_norm_w")


def _row_tile(rows, cap):
    best = rows
    for t in range(8, min(cap, rows) + 1, 8):
        if rows % t == 0:
            best = t
    return best


SMALL_UNIT = 8 * LANES


def _pack_small(vals):
    rows = []
    for a in vals:
        f = a.reshape(-1)
        pad = -f.shape[0] % SMALL_UNIT
        rows.append(jnp.concatenate([f, jnp.zeros((pad,), F32)]).reshape(-1, LANES))
    return jnp.concatenate(rows, axis=0)


def _unpack_small(mat, shapes):
    out, r = [], 0
    for shp in shapes:
        n = math.prod(shp)
        nr = -(-n // SMALL_UNIT) * 8
        out.append(mat[r:r + nr].reshape(-1)[:n].reshape(shp))
        r += nr
    return out


def _place():
    x, y, c = lax.axis_index("x"), lax.axis_index("y"), lax.axis_index("c")
    chips = [(1 - x, y), (x, 1 - y), (1 - x, 1 - y)]
    return x, y, c, chips


ANY = pl.BlockSpec(memory_space=pl.ANY)


def _remote(src, dst, sems, k, to):
    send_sems, recv_sems = sems
    return pltpu.make_async_remote_copy(src_ref=src, dst_ref=dst, send_sem=send_sems.at[k], recv_sem=recv_sems.at[k],
                                        device_id=to, device_id_type=MESH)


def _sem_pairs(n):
    return [pltpu.SemaphoreType.DMA((n,)), pltpu.SemaphoreType.DMA((n,))]


def _run_exchange(exchange, name):
    k = len(exchange.arrays)

    def body(*refs):
        ins, outs, sems = refs[:k], refs[k:-2], refs[-2:]
        exchange.start(ins, outs, sems)
        exchange.finish(ins, outs, sems)

    return pl.pallas_call(
        body, name=name, in_specs=[ANY] * k, out_specs=[ANY] * len(exchange.out_shape),
        out_shape=list(exchange.out_shape), scratch_shapes=_sem_pairs(exchange.n_sems),
    )(*exchange.arrays)


def _gather_exchange(shards):
    n = len(shards)

    def sends(s_refs, g_refs, sems):
        x, y, c, chips = _place()
        me = 2 * x + y
        over_ici = [_remote(s_refs[i].at[c], g_refs[i].at[me, c], sems, 7 * i + j, (px, py, c))
                    for i in range(n) for j, (px, py) in enumerate(chips)]
        return over_ici + [_remote(s_refs[i], g_refs[i].at[me], sems, 7 * i + 6, (x, y, 1 - c)) for i in range(n)]

    def start(s_refs, g_refs, sems):
        for cp in sends(s_refs, g_refs, sems):
            cp.start()

    def finish(s_refs, g_refs, sems):
        x, y, c, chips = _place()
        sib = (x, y, 1 - c)
        passed = []
        for j, (px, py) in enumerate(chips):
            for i in range(n):
                landed = g_refs[i].at[2 * px + py, c]
                _remote(landed, landed, sems, 7 * i + j, (px, py, c)).wait_recv()
                fw = _remote(landed, landed, sems, 7 * i + 3 + j, sib)
                fw.start()
                passed.append(fw)
        for j, (px, py) in enumerate(chips):
            for i in range(n):
                landed = g_refs[i].at[2 * px + py, 1 - c]
                _remote(landed, landed, sems, 7 * i + 3 + j, sib).wait_recv()
        for i in range(n):
            mine = g_refs[i].at[2 * x + y]
            _remote(mine, mine, sems, 7 * i + 6, sib).wait_recv()
        for cp in sends(s_refs, g_refs, sems) + passed:
            cp.wait_send()

    return _Exchange(shards, [S((N_CHIPS,) + a.shape, a.dtype) for a in shards], 7 * n, start, finish)


def _halves_exchange(pieces):
    n = len(pieces)
    ds = [a for a, _ in pieces]

    def copies(d_refs, t_refs, sems):
        x, y, c, _ = _place()
        return [_remote(d_refs[i].at[:, pieces[i][1], :, 1 - c], t_refs[i], sems, i, (x, y, 1 - c)) for i in range(n)]

    def start(d_refs, t_refs, sems):
        for cp in copies(d_refs, t_refs, sems):
            cp.start()

    def finish(d_refs, t_refs, sems):
        for cp in copies(d_refs, t_refs, sems):
            cp.wait()

    return _Exchange(ds, [S((2, 2) + a.shape[4:], a.dtype) for a in ds], n, start, finish)


def _scatter_exchange(ps):
    n = len(ps)

    def copies(p_refs, u_refs, sems):
        x, y, c, chips = _place()
        return [_remote(p_refs[i].at[2 * px + py], u_refs[i].at[j], sems, 3 * i + j, (px, py, c))
                for j, (px, py) in enumerate(chips) for i in range(n)]

    def start(p_refs, u_refs, sems):
        for cp in copies(p_refs, u_refs, sems):
            cp.start()

    def finish(p_refs, u_refs, sems):
        for cp in copies(p_refs, u_refs, sems):
            cp.wait()

    return _Exchange(ps, [S((3,) + a.shape[1:], a.dtype) for a in ps], 3 * n, start, finish)


def _join_halves(qs, name):
    n = len(qs)

    def body(*refs):
        o_refs, sems = refs[n:2 * n], refs[2 * n:]
        x, y, c, _ = _place()
        cps = [_remote(o_refs[i].at[:, c], o_refs[i].at[:, c], sems, i, (x, y, 1 - c)) for i in range(n)]
        for cp in cps:
            cp.start()
        for i in range(n):
            other = o_refs[i].at[:, 1 - c]
            _remote(other, other, sems, i, (x, y, 1 - c)).wait_recv()
        for cp in cps:
            cp.wait_send()

    return pl.pallas_call(
        body, name=name, in_specs=[ANY] * n, out_specs=[ANY] * n,
        out_shape=[S(a.shape, a.dtype) for a in qs], input_output_aliases={i: i for i in range(n)},
        scratch_shapes=_sem_pairs(n),
    )(*qs)


def _gather_small(v, name):
    def body(v_ref, o_ref, send_sems, recv_sems, local_sem):
        x, y, c, _ = _place()
        me = 4 * x + 2 * y + c
        mine = pltpu.make_async_copy(v_ref, o_ref.at[me], local_sem)
        mine.start()
        cps = []
        for k in range(1, N_DEV):
            fx, fy, fc = (k >> 2) & 1, (k >> 1) & 1, k & 1
            peer = (x ^ fx, y ^ fy, c ^ fc)
            cps.append(pltpu.make_async_remote_copy(
                src_ref=v_ref, dst_ref=o_ref.at[me], send_sem=send_sems.at[k - 1], recv_sem=recv_sems.at[k - 1],
                device_id=peer, device_id_type=MESH))
        for cp in cps:
            cp.start()
        for k in range(1, N_DEV):
            fx, fy, fc = (k >> 2) & 1, (k >> 1) & 1, k & 1
            dst = o_ref.at[4 * (x ^ fx) + 2 * (y ^ fy) + (c ^ fc)]
            pltpu.make_async_remote_copy(src_ref=dst, dst_ref=dst, send_sem=send_sems.at[k - 1],
                                         recv_sem=recv_sems.at[k - 1], device_id=(x, y, c),
                                         device_id_type=MESH).wait_recv()
        for cp in cps:
            cp.wait_send()
        mine.wait()

    return pl.pallas_call(
        body, name=name, in_specs=[ANY], out_specs=ANY, out_shape=S((N_DEV,) + v.shape, F32),
        scratch_shapes=[pltpu.SemaphoreType.DMA((N_DEV - 1,)), pltpu.SemaphoreType.DMA((N_DEV - 1,)),
                        pltpu.SemaphoreType.DMA],
    )(v)


def _sum_leading(a, name):
    n, rows, cols = a.shape
    tm = _row_tile(rows, 640)

    def body(a_ref, o_ref):
        acc = a_ref[0]
        for k in range(1, n):
            acc = acc + a_ref[k]
        o_ref[...] = acc

    return pl.pallas_call(
        body, grid=(rows // tm,), name=name, in_specs=[pl.BlockSpec((n, tm, cols), lambda i: (0, i, 0))],
        out_specs=pl.BlockSpec((tm, cols), lambda i: (i, 0)), out_shape=S((rows, cols), F32),
        compiler_params=_params(("arbitrary",)),
    )(a)


def _add_sibling(place, piece, b, name):
    a, pick = piece
    n, hr, cols = b.shape
    tm = _row_tile(hr, 512)

    def body(place_ref, a_ref, b_ref, o_ref, o16_ref):
        tot = a_ref[0, 0, 0] + b_ref[...]
        o_ref[...] = tot
        o16_ref[...] = tot.astype(BF16)

    spec = pl.BlockSpec((1, tm, cols), lambda k, i, pr: (k, i, 0))
    return pl.pallas_call(
        body, name=name, out_shape=[S((n, hr, cols), F32), S((n, hr, cols), BF16)],
        grid_spec=pltpu.PrefetchScalarGridSpec(
            num_scalar_prefetch=1, grid=(n, hr // tm),
            in_specs=[pl.BlockSpec((1, 1, 1, 1, tm, cols), lambda k, i, pr: (k // 2, pick, k % 2, pr[0], i, 0)), spec],
            out_specs=[spec, spec]),
        compiler_params=_params(("arbitrary", "arbitrary")),
    )(place, a, b)


def _add_chips(place, sums, others, layer, into, name):
    _, hr, cols = sums.shape
    tm = _row_tile(hr, 512)

    def body(place_ref, m_ref, o_ref, *rest):
        acc = m_ref[0]
        for k in range(others.shape[0]):
            acc = acc + o_ref[k].astype(F32)
        rest[-1][0, 0] = acc

    kept = [] if into is None else [into]
    return pl.pallas_call(
        body, name=name, out_shape=S((DEPTH, 2, hr, cols), F32),
        grid_spec=pltpu.PrefetchScalarGridSpec(
            num_scalar_prefetch=1, grid=(hr // tm,),
            in_specs=[pl.BlockSpec((1, tm, cols), lambda i, pr: (pr[1], i, 0)),
                      pl.BlockSpec((others.shape[0], tm, cols), lambda i, pr: (0, i, 0))] + [ANY] * len(kept),
            out_specs=pl.BlockSpec((1, 1, tm, cols), lambda i, pr: (layer, pr[0], i, 0))),
        input_output_aliases={3: 0} if kept else {},
        compiler_params=_params(("arbitrary",)),
    )(place, sums, others, *kept)


def _adamw(wt, g, m, v, name):
    shape = wt.shape
    cols = shape[-1]
    rows = math.prod(shape[:-1])
    tm = rows
    for cand in (512, 256, 128, 64, 32, 16, 8):
        if rows % cand == 0:
            tm = cand
            break
    c1 = 1.0 - ADAM_B1 ** ADAM_STEP
    c2 = 1.0 - ADAM_B2 ** ADAM_STEP

    def body(w_ref, g_ref, m_ref, v_ref, d_ref, nm_ref, nv_ref):
        gv = g_ref[...]
        nm = ADAM_B1 * m_ref[...] + (1.0 - ADAM_B1) * gv
        nv = ADAM_B2 * v_ref[...] + (1.0 - ADAM_B2) * (gv * gv)
        d_ref[...] = -ADAM_LR * ((nm / c1) / (jnp.sqrt(nv / c2) + ADAM_EPS) + ADAM_WD * w_ref[...])
        nm_ref[...] = nm
        nv_ref[...] = nv

    spec = pl.BlockSpec((tm, cols), lambda i: (i, 0))
    outs = pl.pallas_call(
        body, grid=(rows // tm,), name=name, in_specs=[spec] * 4, out_specs=[spec] * 3,
        out_shape=[S((rows, cols), F32)] * 3, compiler_params=_params(("arbitrary",)),
    )(*[a.reshape(rows, cols) for a in (wt, g, m, v)])
    return [o.reshape(shape) for o in outs]


WEIGHTS = ('pre_mix_norm', 'post_mix_norm', 'pre_ffn_norm', 'post_ffn_norm', 'w_in', 'w_out', 'attn_sinks', 'ssd_conv_w',
           'ssd_conv_b', 'ssd_dt_bias', 'ssd_A_log', 'ssd_D', 'ssd_norm_w', 'gdn_conv_w', 'gdn_dt_bias', 'gdn_A_log',
           'gdn_norm_w', 'ffn_w_gate', 'ffn_w_up', 'ffn_w_down')


def _chip_piece(i, raw, shape):
    half = (2, shape[1] // 2, shape[2])
    if i in (2, 3):
        return raw["w_gu"].reshape((2, 2, 2) + half), i - 2
    if i == 0:
        g, width = _unpad_cols(raw["w_in_pad"]), shape[2]
        g = jnp.stack([g[:, k * width:(k + 1) * width] for k in range(N_CHIPS)])
    elif i == 1:
        g = raw["w_out_cat"]
        g = jnp.concatenate([g[512:768], g[0:512], g[768:1024]], axis=0)
    else:
        g = raw["ffn_w_down"]
    return g.reshape((2, 1, 2) + half), 0


def _step(x, target, wts, ms, vs):
    chip = 2 * lax.axis_index("x") + lax.axis_index("y")
    place = jnp.stack([lax.axis_index("c"), chip]).astype(jnp.int32)
    big_names = [k for k, _ in BIG]
    flip = lambda k, a: jnp.swapaxes(a, 1, 2) if k in TRANSPOSED else a
    wts, ms, vs = ({k: flip(k, a) for k, a in d.items()} for d in (wts, ms, vs))
    big_shapes = [wts[k].shape for k in big_names]
    halves = lambda a: a.reshape((2, a.shape[0] // 2) + a.shape[1:])

    shard = lambda name, l: halves(wts[name][l].astype(BF16))
    first = shard("w_in", 0)
    w_in0 = _in_weight(_run_exchange(_gather_exchange([first]), "gather_w_in0")[0])
    carried_by = {"ssd": [("w_out", 0), ("ffn_w_gate", 0)],
                  "gdn": [("ffn_w_up", 0), ("ffn_w_down", 0), ("w_in", 1), ("w_out", 1)],
                  "ffn_gu": [("ffn_w_gate", 1), ("ffn_w_up", 1)],
                  "ffn_down": [("ffn_w_down", 1)]}
    gathers = {kind: _gather_exchange([shard(*key) for key in keys]) for kind, keys in carried_by.items()}

    def matmul_weights(l, landed):
        gathered = {}
        for kind, keys in carried_by.items():
            for key, got in zip(keys, landed.get(kind, [])):
                if key[1] == l:
                    gathered[key[0]] = got.reshape((N_CHIPS,) + wts[key[0]].shape[1:])
        return _matmul_weights(w_in0 if l == 0 else _in_weight(gathered["w_in"]), gathered)

    conv = _gather_small(_pack_small([wts[k] for k in CONV]), "gather_conv_weights")
    conv = [_unpack_small(conv[2 * k], [wts[n].shape for n in CONV]) for k in range(N_CHIPS)]
    w_all = dict(wts)
    for i, n in enumerate(CONV):
        w_all[n] = jnp.concatenate([conv[k][i] for k in range(N_CHIPS)], axis=2)

    early_keys = [(DEPTH - 1, 0)] + [(l, i) for l in reversed(range(DEPTH)) for i in range(1, len(BIG))]
    late_keys = [(l, 0) for l in range(DEPTH - 1)]

    def pieces_of(keys, g):
        return [_chip_piece(i, {k: g[k][l] for k in RAW_GRADS}, big_shapes[i]) for l, i in keys]

    def add_siblings(tag, pieces, siblings):
        return [_add_sibling(place, p, t.reshape((N_CHIPS,) + t.shape[2:]), f"add_sibling_{tag}{n}")
                for n, (p, t) in enumerate(zip(pieces, siblings))]

    class Reducer:
        pieces, sums, late_sums = [], [], []

        def exchange(self, g):
            self.pieces = pieces_of(early_keys, g)
            return _halves_exchange(self.pieces)

        def scatter(self, siblings):
            self.sums = add_siblings("early", self.pieces, siblings)
            return _scatter_exchange([s16 for _, s16 in self.sums])

        def late(self, g):
            pieces = pieces_of(late_keys, g)
            siblings = _run_exchange(_halves_exchange(pieces), "exchange_halves_late")
            self.late_sums = add_siblings("late", pieces, siblings)
            return _scatter_exchange([s16 for _, s16 in self.late_sums])

    reducer = Reducer()

    loss_part, grad_x, small_g, raw, early, late = _local_step(
        x[0], target[0], [_small_operands(w_all, l) for l in range(DEPTH)], w_in0, gathers, matmul_weights, reducer)

    reduced = {}
    for tag, keys, sums, others in (("early", early_keys, reducer.sums, early), ("late", late_keys, reducer.late_sums, late)):
        for n, ((l, i), (s32, _), o) in enumerate(zip(keys, sums, others)):
            reduced[i] = _add_chips(place, s32, o, l, reduced.get(i), f"add_chips_{tag}{n}")
    joined = _join_halves([reduced[i] for i in range(len(BIG))], "join_halves")
    g_all = {k: q.reshape(shp) for k, q, shp in zip(big_names, joined, big_shapes)}

    names = SMALL + CONV
    packed = _pack_small([small_g[k] for k in names] + [loss_part])
    small_sum = _sum_leading(_gather_small(packed, "gather_small_grads"), "add_small")
    vals = _unpack_small(small_sum, [small_g[k].shape for k in names] + [(1, LANES)])
    loss = vals[-1][0, 0]
    for k, v in zip(names, vals[:-1]):
        if k in CONV:
            width = wts[k].shape[2]
            v = lax.dynamic_slice_in_dim(v.reshape(DEPTH, 4, -1), chip * width, width, axis=2)
        g_all[k] = v.reshape(wts[k].shape)

    shapes = [wts[k].shape for k in names]
    d_s, m_s, v_s = _adamw(_pack_small([wts[k] for k in names]), _pack_small([g_all[k] for k in names]),
                           _pack_small([ms[k] for k in names]), _pack_small([vs[k] for k in names]), "adamw_small")
    upd = dict(zip(names, zip(_unpack_small(d_s, shapes), _unpack_small(m_s, shapes), _unpack_small(v_s, shapes))))
    for k in big_names:
        upd[k] = _adamw(wts[k], g_all[k], ms[k], vs[k], f"adamw_{k}")
    return (loss, grad_x[None], *[flip(k, g_all[k]) for k in WEIGHTS], *[flip(k, upd[k][0]) for k in WEIGHTS],
            *[flip(k, upd[k][1]) for k in WEIGHTS], *[flip(k, upd[k][2]) for k in WEIGHTS])


def kernel(x, pre_mix_norm, post_mix_norm, pre_ffn_norm, post_ffn_norm, w_in, w_out, attn_sinks, ssd_conv_w, ssd_conv_b, ssd_dt_bias, ssd_A_log, ssd_D, ssd_norm_w, gdn_conv_w, gdn_dt_bias, gdn_A_log, gdn_norm_w, ffn_w_gate, ffn_w_up, ffn_w_down, loss_target, m_pre_mix_norm, m_post_mix_norm, m_pre_ffn_norm, m_post_ffn_norm, m_w_in, m_w_out, m_attn_sinks, m_ssd_conv_w, m_ssd_conv_b, m_ssd_dt_bias, m_ssd_A_log, m_ssd_D, m_ssd_norm_w, m_gdn_conv_w, m_gdn_dt_bias, m_gdn_A_log, m_gdn_norm_w, m_ffn_w_gate, m_ffn_w_up, m_ffn_w_down, v_pre_mix_norm, v_post_mix_norm, v_pre_ffn_norm, v_post_ffn_norm, v_w_in, v_w_out, v_attn_sinks, v_ssd_conv_w, v_ssd_conv_b, v_ssd_dt_bias, v_ssd_A_log, v_ssd_D, v_ssd_norm_w, v_gdn_conv_w, v_gdn_dt_bias, v_gdn_A_log, v_gdn_norm_w, v_ffn_w_gate, v_ffn_w_up, v_ffn_w_down):
    wts = dict(zip(WEIGHTS, (pre_mix_norm, post_mix_norm, pre_ffn_norm, post_ffn_norm, w_in, w_out, attn_sinks, ssd_conv_w, ssd_conv_b, ssd_dt_bias, ssd_A_log, ssd_D, ssd_norm_w, gdn_conv_w, gdn_dt_bias, gdn_A_log, gdn_norm_w, ffn_w_gate, ffn_w_up, ffn_w_down)))
    ms = dict(zip(WEIGHTS, (m_pre_mix_norm, m_post_mix_norm, m_pre_ffn_norm, m_post_ffn_norm, m_w_in, m_w_out, m_attn_sinks, m_ssd_conv_w, m_ssd_conv_b, m_ssd_dt_bias, m_ssd_A_log, m_ssd_D, m_ssd_norm_w, m_gdn_conv_w, m_gdn_dt_bias, m_gdn_A_log, m_gdn_norm_w, m_ffn_w_gate, m_ffn_w_up, m_ffn_w_down)))
    vs = dict(zip(WEIGHTS, (v_pre_mix_norm, v_post_mix_norm, v_pre_ffn_norm, v_post_ffn_norm, v_w_in, v_w_out, v_attn_sinks, v_ssd_conv_w, v_ssd_conv_b, v_ssd_dt_bias, v_ssd_A_log, v_ssd_D, v_ssd_norm_w, v_gdn_conv_w, v_gdn_dt_bias, v_gdn_A_log, v_gdn_norm_w, v_ffn_w_gate, v_ffn_w_up, v_ffn_w_down)))
    return _step(x, loss_target, wts, ms, vs)
```

```python
import functools
import math

import jax
import jax.numpy as jnp
from jax import lax
from jax.experimental import pallas as pl
from jax.experimental.pallas import tpu as pltpu

F32, BF16 = jnp.float32, jnp.bfloat16
HI = lax.Precision.HIGHEST
MESH = pl.DeviceIdType.MESH
S = jax.ShapeDtypeStruct

D_MODEL = 1024
DEPTH = 2
CHUNK = 64
SSD_CHUNK = 256
GDN_CHUNK = 128
EPS = 1e-6
FF = 2816
N_CHIPS = 4
N_DEV = 8
LANES = 128

VMEM_LIMIT_BYTES = 56 * 1024 * 1024

PC_GQKV, PC_GZ, PC_XBC, PC_ATT, PC_SZ, PC_DT, PC_BA, PC_TOT = 0, 768, 1024, 2048, 2560, 3072, 3200, 3328

ADAM_LR, ADAM_B1, ADAM_B2, ADAM_EPS, ADAM_WD, ADAM_STEP = 0.001, 0.9, 0.999, 1e-08, 0.01, 10

ALIBI_SLOPES = tuple(2.0 ** (-8.0 * (h + 1) / 4) for h in range(4))


def _params(sem=None, **kw):
    if sem is not None:
        kw["dimension_semantics"] = sem
    return pltpu.CompilerParams(vmem_limit_bytes=VMEM_LIMIT_BYTES, **kw)


def _dot(a, b, prec=None):
    return jnp.dot(a, b, precision=prec, preferred_element_type=F32)


def _dot_nt(a, b, prec=None):
    return lax.dot_general(a, b, (((1,), (1,)), ((), ())), precision=prec, preferred_element_type=F32)


def _dot_tn(a, b, prec=None):
    return lax.dot_general(a, b, (((0,), (0,)), ((), ())), precision=prec, preferred_element_type=F32)


def _iota2(n, m):
    return lax.broadcasted_iota(jnp.int32, (n, m), 0), lax.broadcasted_iota(jnp.int32, (n, m), 1)


def _pick_col(arr, idx):
    ci = lax.broadcasted_iota(jnp.int32, arr.shape, 1)
    return jnp.sum(jnp.where(ci == idx, arr, 0.0), axis=1, keepdims=True)


def _pick_row(arr, idx):
    ri = lax.broadcasted_iota(jnp.int32, arr.shape, 0)
    return jnp.sum(jnp.where(ri == idx, arr, 0.0), axis=0, keepdims=True)


def _col_to_row(col, eye):
    return jnp.sum(eye * col, axis=0, keepdims=True)


def _rms(x, w):
    return x * lax.rsqrt(jnp.mean(x * x, axis=-1, keepdims=True) + EPS) * w


def _mm_nn(a, b, tm, tn, out_dtype, name, exchange=None):
    m, k = a.shape
    n = b.shape[1]
    tm, tn = min(tm, m), min(tn, n)
    grid = (n // tn, m // tm)

    def body(a_ref, b_ref, o_ref):
        o_ref[...] = _dot(a_ref[...], b_ref[...]).astype(o_ref.dtype)

    body, x_in, x_out, x_shape, x_sems = _hosted(exchange, 2, 1, grid, body)
    outs = pl.pallas_call(
        body, grid=grid, name=name,
        in_specs=[pl.BlockSpec((tm, k), lambda j, i: (i, 0)), pl.BlockSpec((k, tn), lambda j, i: (0, j))] + x_in,
        out_specs=[pl.BlockSpec((tm, tn), lambda j, i: (i, j))] + x_out,
        out_shape=[S((m, n), out_dtype)] + x_shape, scratch_shapes=x_sems,
        compiler_params=_params(("arbitrary", "arbitrary")),
    )(a, b, *([] if exchange is None else exchange.arrays))
    return outs[0] if exchange is None else outs


def _mm_nt(a, b, tm, tn, out_dtype, name, exchange=None):
    m, k = a.shape
    n = b.shape[0]
    tm, tn = min(tm, m), min(tn, n)
    grid = (n // tn, m // tm)

    def body(a_ref, b_ref, o_ref):
        o_ref[...] = _dot_nt(a_ref[...], b_ref[...]).astype(o_ref.dtype)

    body, x_in, x_out, x_shape, x_sems = _hosted(exchange, 2, 1, grid, body)
    outs = pl.pallas_call(
        body, grid=grid, name=name,
        in_specs=[pl.BlockSpec((tm, k), lambda j, i: (i, 0)), pl.BlockSpec((tn, k), lambda j, i: (j, 0))] + x_in,
        out_specs=[pl.BlockSpec((tm, tn), lambda j, i: (i, j))] + x_out,
        out_shape=[S((m, n), out_dtype)] + x_shape, scratch_shapes=x_sems,
        compiler_params=_params(("arbitrary", "arbitrary")),
    )(a, b, *([] if exchange is None else exchange.arrays))
    return outs[0] if exchange is None else outs


def _mm_tn(a, b, tm, tn, tk, name):
    t, m = a.shape
    n = b.shape[1]
    tm, tn, tk = min(tm, m), min(tn, n), min(tk, t)

    def body(a_ref, b_ref, o_ref):
        part = _dot_tn(a_ref[...], b_ref[...])

        @pl.when(pl.program_id(2) == 0)
        def _():
            o_ref[...] = part

        @pl.when(pl.program_id(2) > 0)
        def _():
            o_ref[...] += part

    return pl.pallas_call(
        body, grid=(m // tm, n // tn, t // tk), name=name,
        in_specs=[pl.BlockSpec((tk, tm), lambda i, j, k: (k, i)), pl.BlockSpec((tk, tn), lambda i, j, k: (k, j))],
        out_specs=pl.BlockSpec((tm, tn), lambda i, j, k: (i, j)),
        out_shape=S((m, n), F32), compiler_params=_params(("arbitrary", "arbitrary", "arbitrary")),
    )(a, b)


def _rowcall(fn, rows, params, row_outs, acc_outs, name, tm=512):
    t = rows[0].shape[0]
    tm = min(tm, t)
    n_in = len(rows) + len(params)
    n_ro = len(row_outs)

    def body(*refs):
        ro, ao = fn(*[r[...] for r in refs[:n_in]])
        for ref, v in zip(refs[n_in:n_in + n_ro], ro):
            ref[...] = v.astype(ref.dtype)
        acc_refs = refs[n_in + n_ro:]
        if acc_refs:
            @pl.when(pl.program_id(0) == 0)
            def _():
                for ref, v in zip(acc_refs, ao):
                    ref[...] = v

            @pl.when(pl.program_id(0) > 0)
            def _():
                for ref, v in zip(acc_refs, ao):
                    ref[...] += v

    in_specs = [pl.BlockSpec((tm, r.shape[1]), lambda i: (i, 0)) for r in rows]
    in_specs += [pl.BlockSpec(p.shape, lambda i: (0, 0)) for p in params]
    out_specs = [pl.BlockSpec((tm, c), lambda i: (i, 0)) for c, _ in row_outs]
    out_specs += [pl.BlockSpec(shape, lambda i: (0, 0)) for shape in acc_outs]
    out_shape = [S((t, c), dt) for c, dt in row_outs] + [S(shape, F32) for shape in acc_outs]
    return pl.pallas_call(
        body, grid=(t // tm,), name=name, in_specs=in_specs, out_specs=out_specs, out_shape=out_shape,
        compiler_params=_params(("arbitrary",)),
    )(*rows, *params)


def _prenorm(x, w, name):
    def fn(x, w):
        return (_rms(x, w),), ()
    return _rowcall(fn, [x], [w], [(D_MODEL, BF16)], [], name)[0]


def _resid_norm(xin, m, w_post, w_next, name):
    def fn(xin, m, w_post, w_next):
        xo = xin + _rms(m, w_post)
        return (xo, _rms(xo, w_next)), ()
    return _rowcall(fn, [xin, m], [w_post, w_next], [(D_MODEL, F32), (D_MODEL, BF16)], [], name)


def _resid_loss(xin, m, w_post, target, name):
    def fn(xin, m, target, w_post):
        r, vjp = jax.vjp(_rms, m, w_post)
        err = xin + r - target
        dy = err * (1.0 / D_MODEL)
        dm, dw = vjp(dy)
        tot = jnp.sum(jnp.sum(err * err, axis=1, keepdims=True), axis=0, keepdims=True) * (0.5 / D_MODEL)
        lane = lax.broadcasted_iota(jnp.int32, (1, LANES), 1)
        return (dy, dm), (jnp.where(lane == 0, tot, 0.0), dw)
    return _rowcall(fn, [xin, m, target], [w_post], [(D_MODEL, F32), (D_MODEL, BF16)],
                    [(1, LANES), (1, D_MODEL)], name)


def _resid_norm_bwd(x_out, m, d_direct, dh, w_post, w_next, name):
    def fn(x_out, m, d_direct, dh, w_post, w_next):
        _, vjp_n = jax.vjp(_rms, x_out, w_next)
        dx, dwn = vjp_n(dh)
        d_total = d_direct + dx
        _, vjp_p = jax.vjp(_rms, m, w_post)
        dm, dwp = vjp_p(d_total)
        return (d_total, dm), (dwn, dwp)
    return _rowcall(fn, [x_out, m, d_direct, dh], [w_post, w_next], [(D_MODEL, F32), (D_MODEL, BF16)],
                    [(1, D_MODEL), (1, D_MODEL)], name)


def _prenorm_bwd(x, d_direct, dh, w, name):
    def fn(x, d_direct, dh, w):
        _, vjp = jax.vjp(_rms, x, w)
        dx, dw = vjp(dh)
        return (d_direct + dx,), (dw,)
    return _rowcall(fn, [x, d_direct, dh], [w], [(D_MODEL, F32)], [(1, D_MODEL)], name)


FF_HALF = FF // 2


def _interleave_gu(gate_t, up_t):
    return jnp.concatenate([gate_t[:FF_HALF], up_t[:FF_HALF], gate_t[FF_HALF:], up_t[FF_HALF:]], axis=0)


def _swiglu_pair(gu):
    n = gu.shape[1] // 2
    return jax.nn.silu(gu[:, :n]) * gu[:, n:]


def _ffn_up(h2, w_gu, name, tm=512, exchange=None):
    t, k = h2.shape
    tm = min(tm, t)
    grid = (2, t // tm)

    def body(a_ref, b_ref, gu_ref, act_ref):
        gu = _dot_nt(a_ref[...], b_ref[...])
        gu_ref[...] = gu.astype(gu_ref.dtype)
        act_ref[...] = _swiglu_pair(gu).astype(act_ref.dtype)

    body, x_in, x_out, x_shape, x_sems = _hosted(exchange, 2, 2, grid, body)
    return pl.pallas_call(
        body, grid=grid, name=name,
        in_specs=[pl.BlockSpec((tm, k), lambda j, i: (i, 0)), pl.BlockSpec((FF, k), lambda j, i: (j, 0))] + x_in,
        out_specs=[pl.BlockSpec((tm, FF), lambda j, i: (i, j)), pl.BlockSpec((tm, FF_HALF), lambda j, i: (i, j))] + x_out,
        out_shape=[S((t, 2 * FF), BF16), S((t, FF), BF16)] + x_shape, scratch_shapes=x_sems,
        compiler_params=_params(("arbitrary", "arbitrary")),
    )(h2, w_gu, *([] if exchange is None else exchange.arrays))


def _ffn_down_bwd(d_f, w_down, gu, name, tm=512):
    t, k = d_f.shape
    tm = min(tm, t)

    def body(a_ref, b_ref, gu_ref, o_ref):
        d_act = _dot_nt(a_ref[...], b_ref[...])
        _, vjp = jax.vjp(_swiglu_pair, gu_ref[...].astype(F32))
        o_ref[...] = vjp(d_act)[0].astype(o_ref.dtype)

    return pl.pallas_call(
        body, grid=(2, t // tm), name=name,
        in_specs=[pl.BlockSpec((tm, k), lambda j, i: (i, 0)), pl.BlockSpec((FF_HALF, k), lambda j, i: (j, 0)),
                  pl.BlockSpec((tm, FF), lambda j, i: (i, j))],
        out_specs=pl.BlockSpec((tm, FF), lambda j, i: (i, j)),
        out_shape=S((t, 2 * FF), BF16), compiler_params=_params(("arbitrary", "arbitrary")),
    )(d_f, w_down, gu)


def _conv_fwd(proj, col0, width, w, b, name, tm=512):
    t = proj.shape[0]
    tm = min(tm, t)
    cb = col0 // width

    def body(x_ref, w_ref, b_ref, o_ref, ext):
        @pl.when(pl.program_id(0) == 0)
        def _():
            ext[0:8, :] = jnp.zeros((8, width), F32)

        ext[8:8 + tm, :] = x_ref[...]
        y = b_ref[...] + w_ref[0:1, :] * ext[pl.ds(5, tm), :]
        for k in range(1, 4):
            y = y + w_ref[k:k + 1, :] * ext[pl.ds(5 + k, tm), :]
        o_ref[...] = jax.nn.silu(y)
        ext[0:8, :] = ext[tm:tm + 8, :]

    return pl.pallas_call(
        body, grid=(t // tm,), name=name,
        in_specs=[pl.BlockSpec((tm, width), lambda i: (i, cb)), pl.BlockSpec((4, width), lambda i: (0, 0)),
                  pl.BlockSpec((1, width), lambda i: (0, 0))],
        out_specs=pl.BlockSpec((tm, width), lambda i: (i, 0)),
        out_shape=S((t, width), F32), scratch_shapes=[pltpu.VMEM((tm + 8, width), F32)],
        compiler_params=_params(("arbitrary",)),
    )(proj, w, b)


def _conv_bwd(proj, col0, width, w, b, dact, name, tm=512):
    t = proj.shape[0]
    tm = min(tm, t)
    nb = t // tm
    cb = col0 // width
    hb = tm // 8

    def body(x_ref, halo_ref, d_ref, w_ref, b_ref, dx_ref, dw_ref, db_ref, extx, extd):
        i = pl.program_id(0)
        blk = nb - 1 - i

        @pl.when(i == 0)
        def _():
            extd[tm:tm + 8, :] = jnp.zeros((8, width), F32)
            dw_ref[...] = jnp.zeros((4, width), F32)
            db_ref[...] = jnp.zeros((1, width), F32)

        extx[0:8, :] = jnp.where(blk == 0, 0.0, halo_ref[...])
        extx[8:8 + tm, :] = x_ref[...]
        y = b_ref[...] + w_ref[0:1, :] * extx[pl.ds(5, tm), :]
        for k in range(1, 4):
            y = y + w_ref[k:k + 1, :] * extx[pl.ds(5 + k, tm), :]
        sig = jax.nn.sigmoid(y)
        dy = d_ref[...] * (sig * (1.0 + y * (1.0 - sig)))
        extd[0:tm, :] = dy
        dx = w_ref[0:1, :] * extd[pl.ds(3, tm), :]
        for k in range(1, 4):
            dx = dx + w_ref[k:k + 1, :] * extd[pl.ds(3 - k, tm), :]
        dx_ref[...] = dx.astype(dx_ref.dtype)
        for k in range(4):
            dw_ref[k:k + 1, :] += jnp.sum(dy * extx[pl.ds(5 + k, tm), :], axis=0, keepdims=True)
        db_ref[...] += jnp.sum(dy, axis=0, keepdims=True)
        extd[tm:tm + 8, :] = extd[0:8, :]

    return pl.pallas_call(
        body, grid=(nb,), name=name,
        in_specs=[pl.BlockSpec((tm, width), lambda i: (nb - 1 - i, cb)),
                  pl.BlockSpec((8, width), lambda i: (jnp.maximum((nb - 1 - i) * hb - 1, 0), cb)),
                  pl.BlockSpec((tm, width), lambda i: (nb - 1 - i, 0)),
                  pl.BlockSpec((4, width), lambda i: (0, 0)), pl.BlockSpec((1, width), lambda i: (0, 0))],
        out_specs=[pl.BlockSpec((tm, width), lambda i: (nb - 1 - i, 0)), pl.BlockSpec((4, width), lambda i: (0, 0)),
                   pl.BlockSpec((1, width), lambda i: (0, 0))],
        out_shape=[S((t, width), BF16), S((4, width), F32), S((1, width), F32)],
        scratch_shapes=[pltpu.VMEM((tm + 8, width), F32), pltpu.VMEM((tm + 8, width), F32)],
        compiler_params=_params(("arbitrary",)),
    )(proj, proj, dact, w, b)


SWA_BQ = 256
SWA_BACK = 128


def _swa_block(q, kw, vw, sinks, blk, bf16_operands=False):
    op = (lambda a: a.astype(BF16)) if bf16_operands else (lambda a: a)
    nq, nk = SWA_BQ, SWA_BQ + SWA_BACK
    r, j = _iota2(nq, nk)
    rel = r // CHUNK + 2 - j // CHUNK
    valid = (rel >= 0) & (rel <= 2) & (blk * (SWA_BQ // CHUNK) + j // CHUNK - 2 >= 0)
    dist = jnp.abs(r + SWA_BACK - j).astype(F32)
    outs = []
    for h in range(4):
        kv = h // 2
        qh = q[:, 64 * h:64 * h + 64]
        kh = kw[:, 64 * kv:64 * kv + 64]
        vh = vw[:, 64 * kv:64 * kv + 64]
        s = _dot_nt(op(qh), op(kh)) * 0.125 - ALIBI_SLOPES[h] * dist
        s = jnp.where(valid, s, -1e30)
        sink = _pick_col(sinks, h)
        m = jnp.maximum(jnp.max(s, axis=1, keepdims=True), sink)
        e = jnp.exp(s - m)
        den = jnp.sum(e, axis=1, keepdims=True) + jnp.exp(sink - m)
        outs.append(_dot(op(e / den), op(vh)))
    return jnp.concatenate(outs, axis=1)


def _swa_fwd(proj, sinks, name):
    t = proj.shape[0]
    qb, kb = PC_ATT // 256, PC_ATT // 128 + 2
    win = SWA_BQ + SWA_BACK

    def body(q_ref, k_ref, v_ref, s_ref, o_ref, kp, vp):
        i = pl.program_id(0)

        @pl.when(i == 0)
        def _():
            kp[0:SWA_BACK, :] = jnp.zeros((SWA_BACK, 128), F32)
            vp[0:SWA_BACK, :] = jnp.zeros((SWA_BACK, 128), F32)
            kp[SWA_BACK:, :] = k_ref[...]
            vp[SWA_BACK:, :] = v_ref[...]

        start = pl.multiple_of(i * SWA_BQ, SWA_BQ)
        o = _swa_block(q_ref[...], kp[pl.ds(start, win), :], vp[pl.ds(start, win), :], s_ref[...], i)
        o_ref[...] = o.astype(o_ref.dtype)

    return pl.pallas_call(
        body, grid=(t // SWA_BQ,), name=name,
        in_specs=[pl.BlockSpec((SWA_BQ, 256), lambda i: (i, qb)), pl.BlockSpec((t, 128), lambda i: (0, kb)),
                  pl.BlockSpec((t, 128), lambda i: (0, kb + 1)), pl.BlockSpec((1, LANES), lambda i: (0, 0))],
        out_specs=pl.BlockSpec((SWA_BQ, 256), lambda i: (i, 0)),
        out_shape=S((t, 256), BF16),
        scratch_shapes=[pltpu.VMEM((t + SWA_BACK, 128), F32), pltpu.VMEM((t + SWA_BACK, 128), F32)],
        compiler_params=_params(("arbitrary",)),
    )(proj, proj, proj, sinks)


def _swa_bwd(proj, sinks, dcat, dcol0, name):
    t = proj.shape[0]
    nb = t // SWA_BQ
    qb, kb = PC_ATT // 256, PC_ATT // 128 + 2
    db = dcol0 // 256
    win = SWA_BQ + SWA_BACK

    def body(q_ref, k_ref, v_ref, s_ref, do_ref, dq_ref, dk_ref, dv_ref, ds_ref, kp, vp, dkp, dvp):
        i = pl.program_id(0)

        @pl.when(i == 0)
        def _():
            kp[0:SWA_BACK, :] = jnp.zeros((SWA_BACK, 128), F32)
            vp[0:SWA_BACK, :] = jnp.zeros((SWA_BACK, 128), F32)
            kp[SWA_BACK:, :] = k_ref[...]
            vp[SWA_BACK:, :] = v_ref[...]
            dkp[...] = jnp.zeros_like(dkp)
            dvp[...] = jnp.zeros_like(dvp)
            ds_ref[...] = jnp.zeros_like(ds_ref)

        start = pl.multiple_of(i * SWA_BQ, SWA_BQ)
        _, vjp = jax.vjp(functools.partial(_swa_block, blk=i, bf16_operands=True), q_ref[...], kp[pl.ds(start, win), :],
                         vp[pl.ds(start, win), :], s_ref[...])
        dq, dkw, dvw, dsk = vjp(do_ref[...])
        dq_ref[...] = dq.astype(dq_ref.dtype)
        dkp[pl.ds(start, win), :] += dkw
        dvp[pl.ds(start, win), :] += dvw
        ds_ref[...] += dsk

        @pl.when(i == nb - 1)
        def _():
            dk_ref[...] = dkp[SWA_BACK:, :].astype(dk_ref.dtype)
            dv_ref[...] = dvp[SWA_BACK:, :].astype(dv_ref.dtype)

    return pl.pallas_call(
        body, grid=(nb,), name=name,
        in_specs=[pl.BlockSpec((SWA_BQ, 256), lambda i: (i, qb)), pl.BlockSpec((t, 128), lambda i: (0, kb)),
                  pl.BlockSpec((t, 128), lambda i: (0, kb + 1)), pl.BlockSpec((1, LANES), lambda i: (0, 0)),
                  pl.BlockSpec((SWA_BQ, 256), lambda i: (i, db))],
        out_specs=[pl.BlockSpec((SWA_BQ, 256), lambda i: (i, 0)), pl.BlockSpec((t, 128), lambda i: (0, 0)),
                   pl.BlockSpec((t, 128), lambda i: (0, 0)), pl.BlockSpec((1, LANES), lambda i: (0, 0))],
        out_shape=[S((t, 256), BF16), S((t, 128), BF16), S((t, 128), BF16), S((1, LANES), F32)],
        scratch_shapes=[pltpu.VMEM((t + SWA_BACK, 128), F32) for _ in range(4)],
        compiler_params=_params(("arbitrary",)),
    )(proj, proj, proj, sinks, dcat)


def _ssd_chunk(z, xbc, dt_raw, state, dtb, alog, dsk, nw, bf16_operands=False):
    n = z.shape[0]
    op = (lambda a: a.astype(BF16)) if bf16_operands else (lambda a: a)
    r, c = _iota2(n, n)
    tril = r >= c
    eye = (r == c).astype(F32)
    dt = jax.nn.softplus(dt_raw + dtb)
    acs = _dot(tril.astype(F32), dt * (-jnp.exp(alog)), HI)
    xs, bm, cm = xbc[:, :512], xbc[:, 512:768], xbc[:, 768:1024]
    heads = range(8)
    bg = [bm[:, 128 * g:128 * g + 128] for g in range(2)]
    cg = [cm[:, 128 * g:128 * g + 128] for g in range(2)]
    cb = [_dot_nt(op(cg[g]), op(bg[g])) for g in range(2)]
    dth = [_pick_col(dt, h) for h in heads]
    acol = [_pick_col(acs, h) for h in heads]
    arow = [_col_to_row(a, eye) for a in acol]
    lmat = [jnp.where(tril, jnp.exp(jnp.where(tril, a - b, 0.0)), 0.0) for a, b in zip(acol, arow)]
    xh = [xs[:, 64 * h:64 * h + 64] for h in heads]
    xc = [x * t for x, t in zip(xh, dth)]
    st = [state[64 * h:64 * h + 64, :] for h in heads]
    alast = [_pick_row(a, n - 1) for a in acol]
    y_in = [_dot(op(cb[h // 4] * lmat[h]), op(xc[h])) for h in heads]
    y_st = [_dot_nt(op(cg[h // 4]), op(st[h])) * jnp.exp(acol[h]) for h in heads]
    ys = [y_in[h] + y_st[h] + xh[h] * _pick_col(dsk, h) for h in heads]
    new_states = [st[h] * jnp.exp(alast[h]) + _dot_tn(op(xc[h] * jnp.exp(alast[h] - acol[h])), op(bg[h // 4]))
                  for h in heads]
    gg = jnp.concatenate(ys, axis=1) * jax.nn.silu(z)
    outs = []
    for gi in range(2):
        gv = gg[:, 256 * gi:256 * gi + 256]
        outs.append(gv * lax.rsqrt(jnp.mean(gv * gv, axis=-1, keepdims=True) + EPS))
    return jnp.concatenate(outs, axis=1) * nw, jnp.concatenate(new_states, axis=0)


def _ssd_fwd(proj, xbc, dtb, alog, dsk, nw, name, exchange=None):
    t = proj.shape[0]
    rows = min(SSD_CHUNK, t)
    nc = t // rows

    def body(z_ref, x_ref, dt_ref, dtb_ref, al_ref, d_ref, nw_ref, o_ref, st_ref, state):
        @pl.when(pl.program_id(0) == 0)
        def _():
            state[...] = jnp.zeros_like(state)

        st_ref[0] = state[...]
        o, ns = _ssd_chunk(z_ref[...], x_ref[...], dt_ref[...], state[...], dtb_ref[...], al_ref[...], d_ref[...],
                           nw_ref[...])
        o_ref[...] = o.astype(o_ref.dtype)
        state[...] = ns

    body, x_in, x_out, x_shape, x_sems = _hosted(exchange, 7, 2, nc, body)
    vec = pl.BlockSpec((1, LANES), lambda i: (0, 0))
    return pl.pallas_call(
        body, grid=(nc,), name=name,
        in_specs=[pl.BlockSpec((rows, 512), lambda i: (i, PC_SZ // 512)), pl.BlockSpec((rows, 1024), lambda i: (i, 0)),
                  pl.BlockSpec((rows, 128), lambda i: (i, PC_DT // 128)), vec, vec, vec,
                  pl.BlockSpec((1, 512), lambda i: (0, 0))] + x_in,
        out_specs=[pl.BlockSpec((rows, 512), lambda i: (i, 0)), pl.BlockSpec((1, 512, 128), lambda i: (i, 0, 0))] + x_out,
        out_shape=[S((t, 512), BF16), S((nc, 512, 128), F32)] + x_shape,
        scratch_shapes=[pltpu.VMEM((512, 128), F32)] + x_sems,
        compiler_params=_params(("arbitrary",)),
    )(proj, xbc, proj, dtb, alog, dsk, nw, *([] if exchange is None else exchange.arrays))


def _ssd_bwd(proj, xbc, states, dtb, alog, dsk, nw, dcat, dcol0, name, exchange=None):
    t = proj.shape[0]
    rows = min(SSD_CHUNK, t)
    nc = t // rows
    db = dcol0 // 512

    def body(z_ref, x_ref, dt_ref, st_ref, dtb_ref, al_ref, d_ref, nw_ref, do_ref,
             dz_ref, dx_ref, ddt_ref, gdtb_ref, gal_ref, gd_ref, gnw_ref, dstate):
        @pl.when(pl.program_id(0) == 0)
        def _():
            dstate[...] = jnp.zeros_like(dstate)
            gdtb_ref[...] = jnp.zeros_like(gdtb_ref)
            gal_ref[...] = jnp.zeros_like(gal_ref)
            gd_ref[...] = jnp.zeros_like(gd_ref)
            gnw_ref[...] = jnp.zeros_like(gnw_ref)

        _, vjp = jax.vjp(functools.partial(_ssd_chunk, bf16_operands=True), z_ref[...], x_ref[...], dt_ref[...],
                         st_ref[0], dtb_ref[...], al_ref[...], d_ref[...], nw_ref[...])
        dz, dx, ddt, dst, gdtb, gal, gd, gnw = vjp((do_ref[...], dstate[...]))
        dz_ref[...] = dz.astype(dz_ref.dtype)
        dx_ref[...] = dx
        ddt_ref[...] = ddt.astype(ddt_ref.dtype)
        dstate[...] = dst
        gdtb_ref[...] += gdtb
        gal_ref[...] += gal
        gd_ref[...] += gd
        gnw_ref[...] += gnw

    body, x_in, x_out, x_shape, x_sems = _hosted(exchange, 9, 7, nc, body)
    rev = lambda i: nc - 1 - i
    vec = pl.BlockSpec((1, LANES), lambda i: (0, 0))
    vec512 = pl.BlockSpec((1, 512), lambda i: (0, 0))
    return pl.pallas_call(
        body, grid=(nc,), name=name,
        in_specs=[pl.BlockSpec((rows, 512), lambda i: (rev(i), PC_SZ // 512)),
                  pl.BlockSpec((rows, 1024), lambda i: (rev(i), 0)),
                  pl.BlockSpec((rows, 128), lambda i: (rev(i), PC_DT // 128)),
                  pl.BlockSpec((1, 512, 128), lambda i: (rev(i), 0, 0)), vec, vec, vec, vec512,
                  pl.BlockSpec((rows, 512), lambda i: (rev(i), db))] + x_in,
        out_specs=[pl.BlockSpec((rows, 512), lambda i: (rev(i), 0)), pl.BlockSpec((rows, 1024), lambda i: (rev(i), 0)),
                   pl.BlockSpec((rows, 128), lambda i: (rev(i), 0)), vec, vec, vec, vec512] + x_out,
        out_shape=[S((t, 512), BF16), S((t, 1024), F32), S((t, 128), BF16), S((1, LANES), F32), S((1, LANES), F32),
                   S((1, LANES), F32), S((1, 512), F32)] + x_shape,
        scratch_shapes=[pltpu.VMEM((512, 128), F32)] + x_sems,
        compiler_params=_params(("arbitrary",)),
    )(proj, xbc, proj, states, dtb, alog, dsk, nw, dcat, *([] if exchange is None else exchange.arrays))


SOLVE_PREC = lax.Precision.HIGH


def _unit_lower_inverses(nas, known=None):
    def compute(ns):
        if known is not None:
            return tuple(known)
        n = ns[0].shape[0]
        r, c = _iota2(n, n)
        eye = (r == c).astype(F32)
        tm, pw = [eye + a for a in ns], list(ns)
        for _ in range(n.bit_length() - 2):
            pw = [_dot(p, p, SOLVE_PREC) for p in pw]
            tm = [t + _dot(t, p, SOLVE_PREC) for t, p in zip(tm, pw)]
        return tuple(tm)

    inv = jax.custom_vjp(compute)

    def fwd(ns):
        ts = compute(ns)
        return ts, ts

    def bwd(ts, gs):
        part = [_dot_nt(g, t, SOLVE_PREC) for g, t in zip(gs, ts)]
        return (tuple(_dot_tn(t, p, SOLVE_PREC) for t, p in zip(ts, part)),)

    inv.defvjp(fwd, bwd)
    return inv(nas)


def _gdn_chunk(qkv, z, ba, state, dtb, alog, nw, known_inverses=None, bf16_operands=False):
    n = qkv.shape[0]
    op = (lambda a: a.astype(BF16)) if bf16_operands else (lambda a: a)
    r, c = _iota2(n, n)
    tril = r >= c
    stril = r > c
    eye = (r == c).astype(F32)
    beta_all = jax.nn.sigmoid(ba)
    gcs = _dot(tril.astype(F32), -jnp.exp(alog) * jax.nn.softplus(ba + dtb), HI)
    heads = range(4)
    qh = [qkv[:, 64 * h:64 * h + 64] for h in heads]
    kh = [qkv[:, 256 + 64 * h:256 + 64 * h + 64] for h in heads]
    vh = [qkv[:, 512 + 64 * h:512 + 64 * h + 64] for h in heads]
    qn = [q * lax.rsqrt(jnp.sum(q * q, axis=-1, keepdims=True) + EPS) * 0.125 for q in qh]
    kn = [k * lax.rsqrt(jnp.sum(k * k, axis=-1, keepdims=True) + EPS) for k in kh]
    beta = [_pick_col(beta_all, h) for h in heads]
    gcol = [_pick_col(gcs, 4 + h) for h in heads]
    grow = [_col_to_row(g, eye) for g in gcol]
    decay = [jnp.where(tril, jnp.exp(jnp.where(tril, gc - gr, 0.0)), 0.0) for gc, gr in zip(gcol, grow)]
    kbeta = [k * b for k, b in zip(kn, beta)]
    kk = [_dot_nt(op(kb), op(k)) for kb, k in zip(kbeta, kn)]
    qk = [_dot_nt(op(q), op(k)) * dc for q, k, dc in zip(qn, kn, decay)]
    known = None if known_inverses is None else [known_inverses[n * h:n * (h + 1), :] for h in heads]
    tms = _unit_lower_inverses(tuple(-jnp.where(stril, x * dc, 0.0) for x, dc in zip(kk, decay)), known)
    rhs = [jnp.concatenate([v * b, kb * jnp.exp(g)], axis=1) for v, b, kb, g in zip(vh, beta, kbeta, gcol)]
    sol = [_dot(t, x, SOLVE_PREC) for t, x in zip(tms, rhs)]
    st = [state[64 * h:64 * h + 64, :] for h in heads]
    v_new = [s_[:, :64] - _dot(op(s_[:, 64:]), op(s)) for s_, s in zip(sol, st)]
    o = [_dot(op(q * jnp.exp(g)), op(s)) + _dot(op(x), op(vn)) for q, g, s, x, vn in zip(qn, gcol, st, qk, v_new)]
    glast = [_pick_row(g, n - 1) for g in gcol]
    new_states = [s * jnp.exp(gl) + _dot_tn(op(k * jnp.exp(gl - g)), op(vn))
                  for s, gl, k, g, vn in zip(st, glast, kn, gcol, v_new)]
    o = [x * lax.rsqrt(jnp.mean(x * x, axis=-1, keepdims=True) + EPS) * nw for x in o]
    outs = [x * jax.nn.silu(z[:, 64 * h:64 * h + 64]) for h, x in zip(heads, o)]
    return jnp.concatenate(outs, axis=1), jnp.concatenate(new_states, axis=0), jnp.concatenate(tms, axis=0)


class _Exchange:
    def __init__(self, arrays, out_shape, n_sems, start, finish):
        self.arrays, self.out_shape, self.n_sems, self.start, self.finish = arrays, out_shape, n_sems, start, finish


def _hosted(exchange, n_in, n_out, grid, body):
    if exchange is None:
        return body, [], [], [], []
    k_in, k_out = len(exchange.arrays), len(exchange.out_shape)
    grid = (grid,) if isinstance(grid, int) else tuple(grid)

    def hosted_body(*refs):
        ins, refs = refs[:n_in + k_in], refs[n_in + k_in:]
        outs, scratch = refs[:n_out + k_out], refs[n_out + k_out:]
        sems = scratch[-2:]
        first, last = True, True
        for axis, steps in enumerate(grid):
            first = first & (pl.program_id(axis) == 0)
            last = last & (pl.program_id(axis) == steps - 1)

        @pl.when(first)
        def _():
            exchange.start(ins[n_in:], outs[n_out:], sems)

        body(*ins[:n_in], *outs[:n_out], *scratch[:-2])

        @pl.when(last)
        def _():
            exchange.finish(ins[n_in:], outs[n_out:], sems)

    return hosted_body, [ANY] * k_in, [ANY] * k_out, list(exchange.out_shape), _sem_pairs(exchange.n_sems)


def _gdn_fwd(proj, qkv, dtb, alog, nw, name, exchange=None):
    t = proj.shape[0]
    rows = min(GDN_CHUNK, t)
    nc = t // rows

    def body(q_ref, z_ref, ba_ref, dtb_ref, al_ref, nw_ref, o_ref, st_ref, inv_ref, state):
        @pl.when(pl.program_id(0) == 0)
        def _():
            state[...] = jnp.zeros_like(state)

        st_ref[0] = state[...]
        o, ns, tms = _gdn_chunk(q_ref[...], z_ref[...], ba_ref[...], state[...], dtb_ref[...], al_ref[...], nw_ref[...],
                                bf16_operands=True)
        o_ref[...] = o.astype(o_ref.dtype)
        inv_ref[0] = tms
        state[...] = ns

    body, x_in, x_out, x_shape, x_sems = _hosted(exchange, 6, 3, nc, body)
    vec = pl.BlockSpec((1, LANES), lambda i: (0, 0))
    per_chunk = pl.BlockSpec((1, 256, 64), lambda i: (i, 0, 0))
    return pl.pallas_call(
        body, grid=(nc,), name=name,
        in_specs=[pl.BlockSpec((rows, 768), lambda i: (i, 0)), pl.BlockSpec((rows, 256), lambda i: (i, PC_GZ // 256)),
                  pl.BlockSpec((rows, 128), lambda i: (i, PC_BA // 128)), vec, vec, pl.BlockSpec((1, 64), lambda i: (0, 0))]
        + x_in,
        out_specs=[pl.BlockSpec((rows, 256), lambda i: (i, 0)), per_chunk,
                   pl.BlockSpec((1, 4 * rows, rows), lambda i: (i, 0, 0))] + x_out,
        out_shape=[S((t, 256), BF16), S((nc, 256, 64), F32), S((nc, 4 * rows, rows), F32)] + x_shape,
        scratch_shapes=[pltpu.VMEM((256, 64), F32)] + x_sems,
        compiler_params=_params(("arbitrary",)),
    )(qkv, proj, proj, dtb, alog, nw, *([] if exchange is None else exchange.arrays))


def _gdn_bwd(proj, qkv, states, inverses, dtb, alog, nw, dcat, dcol0, name, exchange=None):
    t = proj.shape[0]
    rows = min(GDN_CHUNK, t)
    nc = t // rows
    db = dcol0 // 256

    def body(q_ref, z_ref, ba_ref, st_ref, inv_ref, dtb_ref, al_ref, nw_ref, do_ref,
             dq_ref, dz_ref, dba_ref, gdtb_ref, gal_ref, gnw_ref, dstate):
        @pl.when(pl.program_id(0) == 0)
        def _():
            dstate[...] = jnp.zeros_like(dstate)
            gdtb_ref[...] = jnp.zeros_like(gdtb_ref)
            gal_ref[...] = jnp.zeros_like(gal_ref)
            gnw_ref[...] = jnp.zeros_like(gnw_ref)

        def chunk(*operands):
            return _gdn_chunk(*operands, known_inverses=inv_ref[0])[:2]

        _, vjp = jax.vjp(chunk, q_ref[...], z_ref[...], ba_ref[...], st_ref[0], dtb_ref[...], al_ref[...], nw_ref[...])
        dq, dz, dba, dst, gdtb, gal, gnw = vjp((do_ref[...], dstate[...]))
        dq_ref[...] = dq
        dz_ref[...] = dz.astype(dz_ref.dtype)
        dba_ref[...] = dba.astype(dba_ref.dtype)
        dstate[...] = dst
        gdtb_ref[...] += gdtb
        gal_ref[...] += gal
        gnw_ref[...] += gnw

    body, x_in, x_out, x_shape, x_sems = _hosted(exchange, 9, 6, nc, body)
    rev = lambda i: nc - 1 - i
    vec = pl.BlockSpec((1, LANES), lambda i: (0, 0))
    vec64 = pl.BlockSpec((1, 64), lambda i: (0, 0))
    per_chunk = pl.BlockSpec((1, 256, 64), lambda i: (rev(i), 0, 0))
    return pl.pallas_call(
        body, grid=(nc,), name=name,
        in_specs=[pl.BlockSpec((rows, 768), lambda i: (rev(i), 0)),
                  pl.BlockSpec((rows, 256), lambda i: (rev(i), PC_GZ // 256)),
                  pl.BlockSpec((rows, 128), lambda i: (rev(i), PC_BA // 128)),
                  per_chunk, pl.BlockSpec((1, 4 * rows, rows), lambda i: (rev(i), 0, 0)), vec, vec, vec64,
                  pl.BlockSpec((rows, 256), lambda i: (rev(i), db))] + x_in,
        out_specs=[pl.BlockSpec((rows, 768), lambda i: (rev(i), 0)), pl.BlockSpec((rows, 256), lambda i: (rev(i), 0)),
                   pl.BlockSpec((rows, 128), lambda i: (rev(i), 0)), vec, vec, vec64] + x_out,
        out_shape=[S((t, 768), F32), S((t, 256), BF16), S((t, 128), BF16), S((1, LANES), F32), S((1, LANES), F32),
                   S((1, 64), F32)] + x_shape,
        scratch_shapes=[pltpu.VMEM((256, 64), F32)] + x_sems,
        compiler_params=_params(("arbitrary",)),
    )(qkv, proj, proj, states, inverses, dtb, alog, nw, dcat, *([] if exchange is None else exchange.arrays))


def _pad_cols(w):
    z = jnp.zeros((w.shape[0], 120), w.dtype)
    return jnp.concatenate([w[:, 2056:2824], w[:, 2824:3080], w[:, 1024:2048], w[:, 0:512], w[:, 512:1024],
                            w[:, 2048:2056], z, w[:, 3080:3088], z], axis=1)


def _unpad_cols(g):
    return jnp.concatenate([g[:, PC_ATT:PC_ATT + 512], g[:, PC_SZ:PC_SZ + 512], g[:, PC_XBC:PC_XBC + 1024],
                            g[:, PC_DT:PC_DT + 8], g[:, PC_GQKV:PC_GQKV + 768], g[:, PC_GZ:PC_GZ + 256],
                            g[:, PC_BA:PC_BA + 8]], axis=1)


def _vec128(v, at=0):
    return jnp.zeros((1, LANES), F32).at[0, at:at + v.shape[0]].set(v)


def _in_weight(gathered):
    return _pad_cols(jnp.concatenate([gathered[k].reshape(D_MODEL, -1) for k in range(N_CHIPS)], axis=1))


def _matmul_weights(w_in, gathered):
    rows = lambda name: gathered[name].reshape(-1, gathered[name].shape[-1])
    w_out = rows("w_out")
    return dict(
        w_in=w_in,
        w_out=jnp.concatenate([w_out[256:768], w_out[0:256], w_out[768:1024]], axis=0),
        w_gu=_interleave_gu(rows("ffn_w_gate"), rows("ffn_w_up")),
        w_down=rows("ffn_w_down"))


def _small_operands(w, l):
    return dict(
        pre_mix=w["pre_mix_norm"][l][None], post_mix=w["post_mix_norm"][l][None],
        pre_ffn=w["pre_ffn_norm"][l][None], post_ffn=w["post_ffn_norm"][l][None],
        sinks=_vec128(w["attn_sinks"][l]),
        s_cw=w["ssd_conv_w"][l], s_cb=w["ssd_conv_b"][l][None],
        s_dtb=_vec128(w["ssd_dt_bias"][l]), s_alog=_vec128(w["ssd_A_log"][l]), s_d=_vec128(w["ssd_D"][l]),
        s_nw=w["ssd_norm_w"][l][None],
        g_cw=w["gdn_conv_w"][l], g_cb=jnp.zeros((1, 768), F32),
        g_dtb=_vec128(w["gdn_dt_bias"][l], 4), g_alog=_vec128(w["gdn_A_log"][l], 4), g_nw=w["gdn_norm_w"][l][None],
    )


RAW_GRADS = ("w_in_pad", "w_out_cat", "w_gu", "ffn_w_down")
DW_ROWS = 4096


def _local_step(x, target, lw, w_in0, gathers, matmul_weights, reducer):
    saved, landed = [], {}
    xin = x
    h = _prenorm(x, lw[0]["pre_mix"], "prenorm0")
    for l in range(DEPTH):
        p = lw[l]
        carry = (lambda kind: gathers[kind]) if l == 0 else (lambda kind: None)
        proj = _mm_nn(h, w_in0 if l == 0 else p["w_in"], 512, PC_TOT, F32, f"inproj{l}")
        xbc = _conv_fwd(proj, PC_XBC, 1024, p["s_cw"], p["s_cb"], f"ssd_conv{l}")
        gqkv = _conv_fwd(proj, PC_GQKV, 768, p["g_cw"], p["g_cb"], f"gdn_conv{l}")
        att = _swa_fwd(proj, p["sinks"], f"swa{l}")
        ssd, s_states, *landed_s = _ssd_fwd(proj, xbc, p["s_dtb"], p["s_alog"], p["s_d"], p["s_nw"], f"ssd{l}",
                                            carry("ssd"))
        gdn, g_states, g_inv, *landed_g = _gdn_fwd(proj, gqkv, p["g_dtb"], p["g_alog"], p["g_nw"], f"gdn{l}",
                                                   carry("gdn"))
        if l == 0:
            landed.update(ssd=landed_s, gdn=landed_g)
            p.update(matmul_weights(0, landed))
        cat = jnp.concatenate([ssd, att, gdn], axis=1)
        mix = _mm_nn(cat, p["w_out"], 512, 1024, F32, f"outproj{l}")
        x1, h2 = _resid_norm(xin, mix, p["post_mix"], p["pre_ffn"], f"postmix{l}")
        gu, act, *landed_u = _ffn_up(h2, p["w_gu"], f"ffn_gu{l}", exchange=carry("ffn_gu"))
        if l == 0:
            f, *landed_d = _mm_nn(act, p["w_down"], 512, 1024, F32, f"ffn_down{l}", carry("ffn_down"))
            landed.update(ffn_gu=landed_u, ffn_down=landed_d)
            lw[1].update(matmul_weights(1, landed))
        else:
            f = _mm_nn(act, p["w_down"], 512, 1024, F32, f"ffn_down{l}")
        saved.append(dict(xin=xin, h=h, proj=proj, xbc=xbc, gqkv=gqkv, s_states=s_states, g_states=g_states, g_inv=g_inv,
                          cat=cat, mix=mix, x1=x1, h2=h2, gu=gu, act=act, f=f))
        if l + 1 < DEPTH:
            xin, h = _resid_norm(x1, f, p["post_ffn"], lw[l + 1]["pre_mix"], f"postffn{l}")

    g = {k: [None] * DEPTH for k in SMALL + CONV + RAW_GRADS}
    last = saved[-1]
    d_x2, d_f, loss_part, g["post_ffn_norm"][DEPTH - 1] = _resid_loss(
        last["x1"], last["f"], lw[-1]["post_ffn"], target, "loss")
    early, late = [], []
    for l in reversed(range(DEPTH)):
        p, s = lw[l], saved[l]
        d_gu = _ffn_down_bwd(d_f, p["w_down"], s["gu"], f"d_act{l}")
        g["ffn_w_down"][l] = _mm_tn(s["act"], d_f, FF_HALF, 512, DW_ROWS, f"dw_down{l}")
        d_h2 = _mm_nn(d_gu, p["w_gu"], 512, 1024, F32, f"d_h2{l}")
        g["w_gu"][l] = _mm_tn(d_gu, s["h2"], FF_HALF, 512, DW_ROWS, f"dw_gu{l}")
        d_x1, d_mix, g["pre_ffn_norm"][l], g["post_mix_norm"][l] = _resid_norm_bwd(
            s["x1"], s["mix"], d_x2, d_h2, p["post_mix"], p["pre_ffn"], f"d_postmix{l}")
        d_cat = _mm_nt(d_mix, p["w_out"], 512, 1024, F32, f"d_cat{l}")
        g["w_out_cat"][l] = _mm_tn(s["cat"], d_mix, 512, 1024, DW_ROWS, f"dw_out{l}")
        d_q, d_k, d_v, g_sinks = _swa_bwd(s["proj"], p["sinks"], d_cat, 512, f"d_swa{l}")
        d_sz, d_xbc, d_dt, g_dtb, g_alog, g_d, g["ssd_norm_w"][l], *siblings = _ssd_bwd(
            s["proj"], s["xbc"], s["s_states"], p["s_dtb"], p["s_alog"], p["s_d"], p["s_nw"], d_cat, 0, f"d_ssd{l}",
            reducer.exchange(g) if l == 0 else None)
        d_gq, d_gz, d_ba, gg_dtb, gg_alog, g["gdn_norm_w"][l], *landed_b = _gdn_bwd(
            s["proj"], s["gqkv"], s["g_states"], s["g_inv"], p["g_dtb"], p["g_alog"], p["g_nw"], d_cat, 768, f"d_gdn{l}",
            reducer.scatter(siblings) if l == 0 else None)
        if l == 0:
            early = landed_b
        d_xbc_raw, g["ssd_conv_w"][l], g["ssd_conv_b"][l] = _conv_bwd(
            s["proj"], PC_XBC, 1024, p["s_cw"], p["s_cb"], d_xbc, f"d_ssd_conv{l}")
        d_gq_raw, g["gdn_conv_w"][l], _ = _conv_bwd(s["proj"], PC_GQKV, 768, p["g_cw"], p["g_cb"], d_gq, f"d_gdn_conv{l}")
        d_proj = jnp.concatenate([d_gq_raw, d_gz, d_xbc_raw, d_q, d_k, d_v, d_sz, d_dt, d_ba], axis=1)
        g["w_in_pad"][l] = _mm_tn(s["h"], d_proj, 512, PC_TOT // 2, DW_ROWS, f"dw_in{l}")
        if l == 0:
            d_h, *late = _mm_nt(d_proj, p["w_in"], 512, 1024, F32, f"d_h{l}", reducer.late(g))
        else:
            d_h = _mm_nt(d_proj, p["w_in"], 512, 1024, F32, f"d_h{l}")
        g["attn_sinks"][l] = g_sinks[0, :4]
        g["ssd_dt_bias"][l], g["ssd_A_log"][l], g["ssd_D"][l] = g_dtb[0, :8], g_alog[0, :8], g_d[0, :8]
        g["gdn_dt_bias"][l], g["gdn_A_log"][l] = gg_dtb[0, 4:8], gg_alog[0, 4:8]
        if l > 0:
            sp = saved[l - 1]
            d_x2, d_f, g["pre_mix_norm"][l], g["post_ffn_norm"][l - 1] = _resid_norm_bwd(
                s["xin"], sp["f"], d_x1, d_h, lw[l - 1]["post_ffn"], p["pre_mix"], f"d_postffn{l - 1}")
        else:
            grad_x, g["pre_mix_norm"][0] = _prenorm_bwd(s["xin"], d_x1, d_h, p["pre_mix"], "d_prenorm0")

    small = {k: jnp.stack([a.reshape(-1) for a in g[k]], axis=0) for k in SMALL + CONV}
    return loss_part, grad_x, small, {k: g[k] for k in RAW_GRADS}, early, late


BIG = ("w_in", "w_out", "ffn_w_gate", "ffn_w_up", "ffn_w_down")
CONV = ("ssd_conv_w", "gdn_conv_w")
TRANSPOSED = ("ffn_w_gate", "ffn_w_up")
SMALL = ("pre_mix_norm", "post_mix_norm", "pre_ffn_norm", "post_ffn_norm", "attn_sinks", "ssd_conv_b", "ssd_dt_bias",
         "ssd_A_log", "ssd_D", "ssd_norm_w", "gdn_dt_bias", "gdn_A_log", "gdn_norm_w")


def _row_tile(rows, cap):
    best = rows
    for t in range(8, min(cap, rows) + 1, 8):
        if rows % t == 0:
            best = t
    return best


SMALL_UNIT = 8 * LANES


def _pack_small(vals):
    rows = []
    for a in vals:
        f = a.reshape(-1)
        pad = -f.shape[0] % SMALL_UNIT
        rows.append(jnp.concatenate([f, jnp.zeros((pad,), F32)]).reshape(-1, LANES))
    return jnp.concatenate(rows, axis=0)


def _unpack_small(mat, shapes):
    out, r = [], 0
    for shp in shapes:
        n = math.prod(shp)
        nr = -(-n // SMALL_UNIT) * 8
        out.append(mat[r:r + nr].reshape(-1)[:n].reshape(shp))
        r += nr
    return out


def _place():
    x, y, c = lax.axis_index("x"), lax.axis_index("y"), lax.axis_index("c")
    chips = [(1 - x, y), (x, 1 - y), (1 - x, 1 - y)]
    return x, y, c, chips


ANY = pl.BlockSpec(memory_space=pl.ANY)


def _remote(src, dst, sems, k, to):
    send_sems, recv_sems = sems
    return pltpu.make_async_remote_copy(src_ref=src, dst_ref=dst, send_sem=send_sems.at[k], recv_sem=recv_sems.at[k],
                                        device_id=to, device_id_type=MESH)


def _sem_pairs(n):
    return [pltpu.SemaphoreType.DMA((n,)), pltpu.SemaphoreType.DMA((n,))]


def _run_exchange(exchange, name):
    k = len(exchange.arrays)

    def body(*refs):
        ins, outs, sems = refs[:k], refs[k:-2], refs[-2:]
        exchange.start(ins, outs, sems)
        exchange.finish(ins, outs, sems)

    return pl.pallas_call(
        body, name=name, in_specs=[ANY] * k, out_specs=[ANY] * len(exchange.out_shape),
        out_shape=list(exchange.out_shape), scratch_shapes=_sem_pairs(exchange.n_sems),
    )(*exchange.arrays)


def _gather_exchange(shards):
    n = len(shards)

    def sends(s_refs, g_refs, sems):
        x, y, c, chips = _place()
        me = 2 * x + y
        over_ici = [_remote(s_refs[i].at[c], g_refs[i].at[me, c], sems, 7 * i + j, (px, py, c))
                    for i in range(n) for j, (px, py) in enumerate(chips)]
        return over_ici + [_remote(s_refs[i], g_refs[i].at[me], sems, 7 * i + 6, (x, y, 1 - c)) for i in range(n)]

    def start(s_refs, g_refs, sems):
        for cp in sends(s_refs, g_refs, sems):
            cp.start()

    def finish(s_refs, g_refs, sems):
        x, y, c, chips = _place()
        sib = (x, y, 1 - c)
        passed = []
        for j, (px, py) in enumerate(chips):
            for i in range(n):
                landed = g_refs[i].at[2 * px + py, c]
                _remote(landed, landed, sems, 7 * i + j, (px, py, c)).wait_recv()
                fw = _remote(landed, landed, sems, 7 * i + 3 + j, sib)
                fw.start()
                passed.append(fw)
        for j, (px, py) in enumerate(chips):
            for i in range(n):
                landed = g_refs[i].at[2 * px + py, 1 - c]
                _remote(landed, landed, sems, 7 * i + 3 + j, sib).wait_recv()
        for i in range(n):
            mine = g_refs[i].at[2 * x + y]
            _remote(mine, mine, sems, 7 * i + 6, sib).wait_recv()
        for cp in sends(s_refs, g_refs, sems) + passed:
            cp.wait_send()

    return _Exchange(shards, [S((N_CHIPS,) + a.shape, a.dtype) for a in shards], 7 * n, start, finish)


def _halves_exchange(pieces):
    n = len(pieces)
    ds = [a for a, _ in pieces]

    def copies(d_refs, t_refs, sems):
        x, y, c, _ = _place()
        return [_remote(d_refs[i].at[:, pieces[i][1], :, 1 - c], t_refs[i], sems, i, (x, y, 1 - c)) for i in range(n)]

    def start(d_refs, t_refs, sems):
        for cp in copies(d_refs, t_refs, sems):
            cp.start()

    def finish(d_refs, t_refs, sems):
        for cp in copies(d_refs, t_refs, sems):
            cp.wait()

    return _Exchange(ds, [S((2, 2) + a.shape[4:], a.dtype) for a in ds], n, start, finish)


def _scatter_exchange(ps):
    n = len(ps)

    def copies(p_refs, u_refs, sems):
        x, y, c, chips = _place()
        return [_remote(p_refs[i].at[2 * px + py], u_refs[i].at[j], sems, 3 * i + j, (px, py, c))
                for j, (px, py) in enumerate(chips) for i in range(n)]

    def start(p_refs, u_refs, sems):
        for cp in copies(p_refs, u_refs, sems):
            cp.start()

    def finish(p_refs, u_refs, sems):
        for cp in copies(p_refs, u_refs, sems):
            cp.wait()

    return _Exchange(ps, [S((3,) + a.shape[1:], a.dtype) for a in ps], 3 * n, start, finish)


def _join_halves(qs, name):
    n = len(qs)

    def body(*refs):
        o_refs, sems = refs[n:2 * n], refs[2 * n:]
        x, y, c, _ = _place()
        cps = [_remote(o_refs[i].at[:, c], o_refs[i].at[:, c], sems, i, (x, y, 1 - c)) for i in range(n)]
        for cp in cps:
            cp.start()
        for i in range(n):
            other = o_refs[i].at[:, 1 - c]
            _remote(other, other, sems, i, (x, y, 1 - c)).wait_recv()
        for cp in cps:
            cp.wait_send()

    return pl.pallas_call(
        body, name=name, in_specs=[ANY] * n, out_specs=[ANY] * n,
        out_shape=[S(a.shape, a.dtype) for a in qs], input_output_aliases={i: i for i in range(n)},
        scratch_shapes=_sem_pairs(n),
    )(*qs)


def _gather_small(v, name):
    def body(v_ref, o_ref, send_sems, recv_sems, local_sem):
        x, y, c, _ = _place()
        me = 4 * x + 2 * y + c
        mine = pltpu.make_async_copy(v_ref, o_ref.at[me], local_sem)
        mine.start()
        cps = []
        for k in range(1, N_DEV):
            fx, fy, fc = (k >> 2) & 1, (k >> 1) & 1, k & 1
            peer = (x ^ fx, y ^ fy, c ^ fc)
            cps.append(pltpu.make_async_remote_copy(
                src_ref=v_ref, dst_ref=o_ref.at[me], send_sem=send_sems.at[k - 1], recv_sem=recv_sems.at[k - 1],
                device_id=peer, device_id_type=MESH))
        for cp in cps:
            cp.start()
        for k in range(1, N_DEV):
            fx, fy, fc = (k >> 2) & 1, (k >> 1) & 1, k & 1
            dst = o_ref.at[4 * (x ^ fx) + 2 * (y ^ fy) + (c ^ fc)]
            pltpu.make_async_remote_copy(src_ref=dst, dst_ref=dst, send_sem=send_sems.at[k - 1],
                                         recv_sem=recv_sems.at[k - 1], device_id=(x, y, c),
                                         device_id_type=MESH).wait_recv()
        for cp in cps:
            cp.wait_send()
        mine.wait()

    return pl.pallas_call(
        body, name=name, in_specs=[ANY], out_specs=ANY, out_shape=S((N_DEV,) + v.shape, F32),
        scratch_shapes=[pltpu.SemaphoreType.DMA((N_DEV - 1,)), pltpu.SemaphoreType.DMA((N_DEV - 1,)),
                        pltpu.SemaphoreType.DMA],
    )(v)


def _sum_leading(a, name):
    n, rows, cols = a.shape
    tm = _row_tile(rows, 640)

    def body(a_ref, o_ref):
        acc = a_ref[0]
        for k in range(1, n):
            acc = acc + a_ref[k]
        o_ref[...] = acc

    return pl.pallas_call(
        body, grid=(rows // tm,), name=name, in_specs=[pl.BlockSpec((n, tm, cols), lambda i: (0, i, 0))],
        out_specs=pl.BlockSpec((tm, cols), lambda i: (i, 0)), out_shape=S((rows, cols), F32),
        compiler_params=_params(("arbitrary",)),
    )(a)


def _add_sibling(place, piece, b, name):
    a, pick = piece
    n, hr, cols = b.shape
    tm = _row_tile(hr, 512)

    def body(place_ref, a_ref, b_ref, o_ref, o16_ref):
        tot = a_ref[0, 0, 0] + b_ref[...]
        o_ref[...] = tot
        o16_ref[...] = tot.astype(BF16)

    spec = pl.BlockSpec((1, tm, cols), lambda k, i, pr: (k, i, 0))
    return pl.pallas_call(
        body, name=name, out_shape=[S((n, hr, cols), F32), S((n, hr, cols), BF16)],
        grid_spec=pltpu.PrefetchScalarGridSpec(
            num_scalar_prefetch=1, grid=(n, hr // tm),
            in_specs=[pl.BlockSpec((1, 1, 1, 1, tm, cols), lambda k, i, pr: (k // 2, pick, k % 2, pr[0], i, 0)), spec],
            out_specs=[spec, spec]),
        compiler_params=_params(("arbitrary", "arbitrary")),
    )(place, a, b)


def _add_chips(place, sums, others, layer, into, name):
    _, hr, cols = sums.shape
    tm = _row_tile(hr, 512)

    def body(place_ref, m_ref, o_ref, *rest):
        acc = m_ref[0]
        for k in range(others.shape[0]):
            acc = acc + o_ref[k].astype(F32)
        rest[-1][0, 0] = acc

    kept = [] if into is None else [into]
    return pl.pallas_call(
        body, name=name, out_shape=S((DEPTH, 2, hr, cols), F32),
        grid_spec=pltpu.PrefetchScalarGridSpec(
            num_scalar_prefetch=1, grid=(hr // tm,),
            in_specs=[pl.BlockSpec((1, tm, cols), lambda i, pr: (pr[1], i, 0)),
                      pl.BlockSpec((others.shape[0], tm, cols), lambda i, pr: (0, i, 0))] + [ANY] * len(kept),
            out_specs=pl.BlockSpec((1, 1, tm, cols), lambda i, pr: (layer, pr[0], i, 0))),
        input_output_aliases={3: 0} if kept else {},
        compiler_params=_params(("arbitrary",)),
    )(place, sums, others, *kept)


def _adamw(wt, g, m, v, name):
    shape = wt.shape
    cols = shape[-1]
    rows = math.prod(shape[:-1])
    tm = rows
    for cand in (512, 256, 128, 64, 32, 16, 8):
        if rows % cand == 0:
            tm = cand
            break
    c1 = 1.0 - ADAM_B1 ** ADAM_STEP
    c2 = 1.0 - ADAM_B2 ** ADAM_STEP

    def body(w_ref, g_ref, m_ref, v_ref, d_ref, nm_ref, nv_ref):
        gv = g_ref[...]
        nm = ADAM_B1 * m_ref[...] + (1.0 - ADAM_B1) * gv
        nv = ADAM_B2 * v_ref[...] + (1.0 - ADAM_B2) * (gv * gv)
        d_ref[...] = -ADAM_LR * ((nm / c1) / (jnp.sqrt(nv / c2) + ADAM_EPS) + ADAM_WD * w_ref[...])
        nm_ref[...] = nm
        nv_ref[...] = nv

    spec = pl.BlockSpec((tm, cols), lambda i: (i, 0))
    outs = pl.pallas_call(
        body, grid=(rows // tm,), name=name, in_specs=[spec] * 4, out_specs=[spec] * 3,
        out_shape=[S((rows, cols), F32)] * 3, compiler_params=_params(("arbitrary",)),
    )(*[a.reshape(rows, cols) for a in (wt, g, m, v)])
    return [o.reshape(shape) for o in outs]


WEIGHTS = ('pre_mix_norm', 'post_mix_norm', 'pre_ffn_norm', 'post_ffn_norm', 'w_in', 'w_out', 'attn_sinks', 'ssd_conv_w',
           'ssd_conv_b', 'ssd_dt_bias', 'ssd_A_log', 'ssd_D', 'ssd_norm_w', 'gdn_conv_w', 'gdn_dt_bias', 'gdn_A_log',
           'gdn_norm_w', 'ffn_w_gate', 'ffn_w_up', 'ffn_w_down')


def _chip_piece(i, raw, shape):
    half = (2, shape[1] // 2, shape[2])
    if i in (2, 3):
        return raw["w_gu"].reshape((2, 2, 2) + half), i - 2
    if i == 0:
        g, width = _unpad_cols(raw["w_in_pad"]), shape[2]
        g = jnp.stack([g[:, k * width:(k + 1) * width] for k in range(N_CHIPS)])
    elif i == 1:
        g = raw["w_out_cat"]
        g = jnp.concatenate([g[512:768], g[0:512], g[768:1024]], axis=0)
    else:
        g = raw["ffn_w_down"]
    return g.reshape((2, 1, 2) + half), 0


def _step(x, target, wts, ms, vs):
    chip = 2 * lax.axis_index("x") + lax.axis_index("y")
    place = jnp.stack([lax.axis_index("c"), chip]).astype(jnp.int32)
    big_names = list(BIG)
    flip = lambda k, a: jnp.swapaxes(a, 1, 2) if k in TRANSPOSED else a
    wts, ms, vs = ({k: flip(k, a) for k, a in d.items()} for d in (wts, ms, vs))
    big_shapes = [wts[k].shape for k in big_names]
    halves = lambda a: a.reshape((2, a.shape[0] // 2) + a.shape[1:])

    shard = lambda name, l: halves(wts[name][l].astype(BF16))
    first = shard("w_in", 0)
    w_in0 = _in_weight(_run_exchange(_gather_exchange([first]), "gather_w_in0")[0])
    carried_by = {"ssd": [("w_out", 0), ("ffn_w_gate", 0)],
                  "gdn": [("ffn_w_up", 0), ("ffn_w_down", 0), ("w_in", 1), ("w_out", 1)],
                  "ffn_gu": [("ffn_w_gate", 1), ("ffn_w_up", 1)],
                  "ffn_down": [("ffn_w_down", 1)]}
    gathers = {kind: _gather_exchange([shard(*key) for key in keys]) for kind, keys in carried_by.items()}

    def matmul_weights(l, landed):
        gathered = {}
        for kind, keys in carried_by.items():
            for key, got in zip(keys, landed.get(kind, [])):
                if key[1] == l:
                    gathered[key[0]] = got.reshape((N_CHIPS,) + wts[key[0]].shape[1:])
        return _matmul_weights(w_in0 if l == 0 else _in_weight(gathered["w_in"]), gathered)

    conv = _gather_small(_pack_small([wts[k] for k in CONV]), "gather_conv_weights")
    conv = [_unpack_small(conv[2 * k], [wts[n].shape for n in CONV]) for k in range(N_CHIPS)]
    w_all = dict(wts)
    for i, n in enumerate(CONV):
        w_all[n] = jnp.concatenate([conv[k][i] for k in range(N_CHIPS)], axis=2)

    early_keys = [(DEPTH - 1, 0)] + [(l, i) for l in reversed(range(DEPTH)) for i in range(1, len(BIG))]
    late_keys = [(l, 0) for l in range(DEPTH - 1)]

    def pieces_of(keys, g):
        return [_chip_piece(i, {k: g[k][l] for k in RAW_GRADS}, big_shapes[i]) for l, i in keys]

    def add_siblings(tag, pieces, siblings):
        return [_add_sibling(place, p, t.reshape((N_CHIPS,) + t.shape[2:]), f"add_sibling_{tag}{n}")
                for n, (p, t) in enumerate(zip(pieces, siblings))]

    class Reducer:
        pieces, sums, late_sums = [], [], []

        def exchange(self, g):
            self.pieces = pieces_of(early_keys, g)
            return _halves_exchange(self.pieces)

        def scatter(self, siblings):
            self.sums = add_siblings("early", self.pieces, siblings)
            return _scatter_exchange([s16 for _, s16 in self.sums])

        def late(self, g):
            pieces = pieces_of(late_keys, g)
            siblings = _run_exchange(_halves_exchange(pieces), "exchange_halves_late")
            self.late_sums = add_siblings("late", pieces, siblings)
            return _scatter_exchange([s16 for _, s16 in self.late_sums])

    reducer = Reducer()

    loss_part, grad_x, small_g, raw, early, late = _local_step(
        x[0], target[0], [_small_operands(w_all, l) for l in range(DEPTH)], w_in0, gathers, matmul_weights, reducer)

    reduced = {}
    for tag, keys, sums, others in (("early", early_keys, reducer.sums, early), ("late", late_keys, reducer.late_sums, late)):
        for n, ((l, i), (s32, _), o) in enumerate(zip(keys, sums, others)):
            reduced[i] = _add_chips(place, s32, o, l, reduced.get(i), f"add_chips_{tag}{n}")
    joined = _join_halves([reduced[i] for i in range(len(BIG))], "join_halves")
    g_all = {k: q.reshape(shp) for k, q, shp in zip(big_names, joined, big_shapes)}

    names = SMALL + CONV
    packed = _pack_small([small_g[k] for k in names] + [loss_part])
    small_sum = _sum_leading(_gather_small(packed, "gather_small_grads"), "add_small")
    vals = _unpack_small(small_sum, [small_g[k].shape for k in names] + [(1, LANES)])
    loss = vals[-1][0, 0]
    for k, v in zip(names, vals[:-1]):
        if k in CONV:
            width = wts[k].shape[2]
            v = lax.dynamic_slice_in_dim(v.reshape(DEPTH, 4, -1), chip * width, width, axis=2)
        g_all[k] = v.reshape(wts[k].shape)

    shapes = [wts[k].shape for k in names]
    d_s, m_s, v_s = _adamw(_pack_small([wts[k] for k in names]), _pack_small([g_all[k] for k in names]),
                           _pack_small([ms[k] for k in names]), _pack_small([vs[k] for k in names]), "adamw_small")
    upd = dict(zip(names, zip(_unpack_small(d_s, shapes), _unpack_small(m_s, shapes), _unpack_small(v_s, shapes))))
    for k in big_names:
        upd[k] = _adamw(wts[k], g_all[k], ms[k], vs[k], f"adamw_{k}")
    return (loss, grad_x[None], *[flip(k, g_all[k]) for k in WEIGHTS], *[flip(k, upd[k][0]) for k in WEIGHTS],
            *[flip(k, upd[k][1]) for k in WEIGHTS], *[flip(k, upd[k][2]) for k in WEIGHTS])


def kernel(x, pre_mix_norm, post_mix_norm, pre_ffn_norm, post_ffn_norm, w_in, w_out, attn_sinks, ssd_conv_w, ssd_conv_b, ssd_dt_bias, ssd_A_log, ssd_D, ssd_norm_w, gdn_conv_w, gdn_dt_bias, gdn_A_log, gdn_norm_w, ffn_w_gate, ffn_w_up, ffn_w_down, loss_target, m_pre_mix_norm, m_post_mix_norm, m_pre_ffn_norm, m_post_ffn_norm, m_w_in, m_w_out, m_attn_sinks, m_ssd_conv_w, m_ssd_conv_b, m_ssd_dt_bias, m_ssd_A_log, m_ssd_D, m_ssd_norm_w, m_gdn_conv_w, m_gdn_dt_bias, m_gdn_A_log, m_gdn_norm_w, m_ffn_w_gate, m_ffn_w_up, m_ffn_w_down, v_pre_mix_norm, v_post_mix_norm, v_pre_ffn_norm, v_post_ffn_norm, v_w_in, v_w_out, v_attn_sinks, v_ssd_conv_w, v_ssd_conv_b, v_ssd_dt_bias, v_ssd_A_log, v_ssd_D, v_ssd_norm_w, v_gdn_conv_w, v_gdn_dt_bias, v_gdn_A_log, v_gdn_norm_w, v_ffn_w_gate, v_ffn_w_up, v_ffn_w_down):
    wts = dict(zip(WEIGHTS, (pre_mix_norm, post_mix_norm, pre_ffn_norm, post_ffn_norm, w_in, w_out, attn_sinks, ssd_conv_w, ssd_conv_b, ssd_dt_bias, ssd_A_log, ssd_D, ssd_norm_w, gdn_conv_w, gdn_dt_bias, gdn_A_log, gdn_norm_w, ffn_w_gate, ffn_w_up, ffn_w_down)))
    ms = dict(zip(WEIGHTS, (m_pre_mix_norm, m_post_mix_norm, m_pre_ffn_norm, m_post_ffn_norm, m_w_in, m_w_out, m_attn_sinks, m_ssd_conv_w, m_ssd_conv_b, m_ssd_dt_bias, m_ssd_A_log, m_ssd_D, m_ssd_norm_w, m_gdn_conv_w, m_gdn_dt_bias, m_gdn_A_log, m_gdn_norm_w, m_ffn_w_gate, m_ffn_w_up, m_ffn_w_down)))
    vs = dict(zip(WEIGHTS, (v_pre_mix_norm, v_post_mix_norm, v_pre_ffn_norm, v_post_ffn_norm, v_w_in, v_w_out, v_attn_sinks, v_ssd_conv_w, v_ssd_conv_b, v_ssd_dt_bias, v_ssd_A_log, v_ssd_D, v_ssd_norm_w, v_gdn_conv_w, v_gdn_dt_bias, v_gdn_A_log, v_gdn_norm_w, v_ffn_w_gate, v_ffn_w_up, v_ffn_w_down)))
    return _step(x, loss_target, wts, ms, vs)
```

```python
import functools
import math

import jax
import jax.numpy as jnp
from jax import lax
from jax.experimental import pallas as pl
from jax.experimental.pallas import tpu as pltpu

F32, BF16 = jnp.float32, jnp.bfloat16
HI = lax.Precision.HIGHEST
MESH = pl.DeviceIdType.MESH
S = jax.ShapeDtypeStruct

D_MODEL = 1024
DEPTH = 2
CHUNK = 64
SSD_CHUNK = 256
GDN_CHUNK = 128
EPS = 1e-6
FF = 2816
N_CHIPS = 4
N_DEV = 8
LANES = 128

VMEM_LIMIT_BYTES = 56 * 1024 * 1024

PC_GQKV, PC_GZ, PC_XBC, PC_ATT, PC_SZ, PC_DT, PC_BA, PC_TOT = 0, 768, 1024, 2048, 2560, 3072, 3200, 3328

ADAM_LR, ADAM_B1, ADAM_B2, ADAM_EPS, ADAM_WD, ADAM_STEP = 0.001, 0.9, 0.999, 1e-08, 0.01, 10

ALIBI_SLOPES = tuple(2.0 ** (-8.0 * (h + 1) / 4) for h in range(4))


def _params(sem=None, **kw):
    if sem is not None:
        kw["dimension_semantics"] = sem
    return pltpu.CompilerParams(vmem_limit_bytes=VMEM_LIMIT_BYTES, **kw)


def _dot(a, b, prec=None):
    return jnp.dot(a, b, precision=prec, preferred_element_type=F32)


def _dot_nt(a, b, prec=None):
    return lax.dot_general(a, b, (((1,), (1,)), ((), ())), precision=prec, preferred_element_type=F32)


def _dot_tn(a, b, prec=None):
    return lax.dot_general(a, b, (((0,), (0,)), ((), ())), precision=prec, preferred_element_type=F32)


def _iota2(n, m):
    return lax.broadcasted_iota(jnp.int32, (n, m), 0), lax.broadcasted_iota(jnp.int32, (n, m), 1)


def _pick_col(arr, idx):
    ci = lax.broadcasted_iota(jnp.int32, arr.shape, 1)
    return jnp.sum(jnp.where(ci == idx, arr, 0.0), axis=1, keepdims=True)


def _pick_row(arr, idx):
    ri = lax.broadcasted_iota(jnp.int32, arr.shape, 0)
    return jnp.sum(jnp.where(ri == idx, arr, 0.0), axis=0, keepdims=True)


def _col_to_row(col, eye):
    return jnp.sum(eye * col, axis=0, keepdims=True)


def _rms(x, w):
    return x * lax.rsqrt(jnp.mean(x * x, axis=-1, keepdims=True) + EPS) * w


def _mm_nn(a, b, tm, tn, out_dtype, name, exchange=None):
    m, k = a.shape
    n = b.shape[1]
    tm, tn = min(tm, m), min(tn, n)
    grid = (n // tn, m // tm)

    def body(a_ref, b_ref, o_ref):
        o_ref[...] = _dot(a_ref[...], b_ref[...]).astype(o_ref.dtype)

    body, x_in, x_out, x_shape, x_sems = _hosted(exchange, 2, 1, grid, body)
    outs = pl.pallas_call(
        body, grid=grid, name=name,
        in_specs=[pl.BlockSpec((tm, k), lambda j, i: (i, 0)), pl.BlockSpec((k, tn), lambda j, i: (0, j))] + x_in,
        out_specs=[pl.BlockSpec((tm, tn), lambda j, i: (i, j))] + x_out,
        out_shape=[S((m, n), out_dtype)] + x_shape, scratch_shapes=x_sems,
        compiler_params=_params(("arbitrary", "arbitrary")),
    )(a, b, *([] if exchange is None else exchange.arrays))
    return outs[0] if exchange is None else outs


def _mm_nt(a, b, tm, tn, out_dtype, name, exchange=None):
    m, k = a.shape
    n = b.shape[0]
    tm, tn = min(tm, m), min(tn, n)
    grid = (n // tn, m // tm)

    def body(a_ref, b_ref, o_ref):
        o_ref[...] = _dot_nt(a_ref[...], b_ref[...]).astype(o_ref.dtype)

    body, x_in, x_out, x_shape, x_sems = _hosted(exchange, 2, 1, grid, body)
    outs = pl.pallas_call(
        body, grid=grid, name=name,
        in_specs=[pl.BlockSpec((tm, k), lambda j, i: (i, 0)), pl.BlockSpec((tn, k), lambda j, i: (j, 0))] + x_in,
        out_specs=[pl.BlockSpec((tm, tn), lambda j, i: (i, j))] + x_out,
        out_shape=[S((m, n), out_dtype)] + x_shape, scratch_shapes=x_sems,
        compiler_params=_params(("arbitrary", "arbitrary")),
    )(a, b, *([] if exchange is None else exchange.arrays))
    return outs[0] if exchange is None else outs


def _mm_tn(a, b, tm, tn, tk, name):
    t, m = a.shape
    n = b.shape[1]
    tm, tn, tk = min(tm, m), min(tn, n), min(tk, t)

    def body(a_ref, b_ref, o_ref):
        part = _dot_tn(a_ref[...], b_ref[...])

        @pl.when(pl.program_id(2) == 0)
        def _():
            o_ref[...] = part

        @pl.when(pl.program_id(2) > 0)
        def _():
            o_ref[...] += part

    return pl.pallas_call(
        body, grid=(m // tm, n // tn, t // tk), name=name,
        in_specs=[pl.BlockSpec((tk, tm), lambda i, j, k: (k, i)), pl.BlockSpec((tk, tn), lambda i, j, k: (k, j))],
        out_specs=pl.BlockSpec((tm, tn), lambda i, j, k: (i, j)),
        out_shape=S((m, n), F32), compiler_params=_params(("arbitrary", "arbitrary", "arbitrary")),
    )(a, b)


def _rowcall(fn, rows, params, row_outs, acc_outs, name, tm=512):
    t = rows[0].shape[0]
    tm = min(tm, t)
    n_in = len(rows) + len(params)
    n_ro = len(row_outs)

    def body(*refs):
        ro, ao = fn(*[r[...] for r in refs[:n_in]])
        for ref, v in zip(refs[n_in:n_in + n_ro], ro):
            ref[...] = v.astype(ref.dtype)
        acc_refs = refs[n_in + n_ro:]
        if acc_refs:
            @pl.when(pl.program_id(0) == 0)
            def _():
                for ref, v in zip(acc_refs, ao):
                    ref[...] = v

            @pl.when(pl.program_id(0) > 0)
            def _():
                for ref, v in zip(acc_refs, ao):
                    ref[...] += v

    in_specs = [pl.BlockSpec((tm, r.shape[1]), lambda i: (i, 0)) for r in rows]
    in_specs += [pl.BlockSpec(p.shape, lambda i: (0, 0)) for p in params]
    out_specs = [pl.BlockSpec((tm, c), lambda i: (i, 0)) for c, _ in row_outs]
    out_specs += [pl.BlockSpec(shape, lambda i: (0, 0)) for shape in acc_outs]
    out_shape = [S((t, c), dt) for c, dt in row_outs] + [S(shape, F32) for shape in acc_outs]
    return pl.pallas_call(
        body, grid=(t // tm,), name=name, in_specs=in_specs, out_specs=out_specs, out_shape=out_shape,
        compiler_params=_params(("arbitrary",)),
    )(*rows, *params)


def _prenorm(x, w, name):
    def fn(x, w):
        return (_rms(x, w),), ()
    return _rowcall(fn, [x], [w], [(D_MODEL, BF16)], [], name)[0]


def _resid_norm(xin, m, w_post, w_next, name):
    def fn(xin, m, w_post, w_next):
        xo = xin + _rms(m, w_post)
        return (xo, _rms(xo, w_next)), ()
    return _rowcall(fn, [xin, m], [w_post, w_next], [(D_MODEL, F32), (D_MODEL, BF16)], [], name)


def _resid_loss(xin, m, w_post, target, name):
    def fn(xin, m, target, w_post):
        r, vjp = jax.vjp(_rms, m, w_post)
        err = xin + r - target
        dy = err * (1.0 / D_MODEL)
        dm, dw = vjp(dy)
        tot = jnp.sum(jnp.sum(err * err, axis=1, keepdims=True), axis=0, keepdims=True) * (0.5 / D_MODEL)
        lane = lax.broadcasted_iota(jnp.int32, (1, LANES), 1)
        return (dy, dm), (jnp.where(lane == 0, tot, 0.0), dw)
    return _rowcall(fn, [xin, m, target], [w_post], [(D_MODEL, F32), (D_MODEL, BF16)],
                    [(1, LANES), (1, D_MODEL)], name)


def _resid_norm_bwd(x_out, m, d_direct, dh, w_post, w_next, name):
    def fn(x_out, m, d_direct, dh, w_post, w_next):
        _, vjp_n = jax.vjp(_rms, x_out, w_next)
        dx, dwn = vjp_n(dh)
        d_total = d_direct + dx
        _, vjp_p = jax.vjp(_rms, m, w_post)
        dm, dwp = vjp_p(d_total)
        return (d_total, dm), (dwn, dwp)
    return _rowcall(fn, [x_out, m, d_direct, dh], [w_post, w_next], [(D_MODEL, F32), (D_MODEL, BF16)],
                    [(1, D_MODEL), (1, D_MODEL)], name)


def _prenorm_bwd(x, d_direct, dh, w, name):
    def fn(x, d_direct, dh, w):
        _, vjp = jax.vjp(_rms, x, w)
        dx, dw = vjp(dh)
        return (d_direct + dx,), (dw,)
    return _rowcall(fn, [x, d_direct, dh], [w], [(D_MODEL, F32)], [(1, D_MODEL)], name)


FF_HALF = FF // 2


def _interleave_gu(gate_t, up_t):
    return jnp.concatenate([gate_t[:FF_HALF], up_t[:FF_HALF], gate_t[FF_HALF:], up_t[FF_HALF:]], axis=0)


def _swiglu_pair(gu):
    n = gu.shape[1] // 2
    return jax.nn.silu(gu[:, :n]) * gu[:, n:]


def _ffn_up(h2, w_gu, name, tm=512, exchange=None):
    t, k = h2.shape
    tm = min(tm, t)
    grid = (2, t // tm)

    def body(a_ref, b_ref, gu_ref, act_ref):
        gu = _dot_nt(a_ref[...], b_ref[...])
        gu_ref[...] = gu.astype(gu_ref.dtype)
        act_ref[...] = _swiglu_pair(gu).astype(act_ref.dtype)

    body, x_in, x_out, x_shape, x_sems = _hosted(exchange, 2, 2, grid, body)
    return pl.pallas_call(
        body, grid=grid, name=name,
        in_specs=[pl.BlockSpec((tm, k), lambda j, i: (i, 0)), pl.BlockSpec((FF, k), lambda j, i: (j, 0))] + x_in,
        out_specs=[pl.BlockSpec((tm, FF), lambda j, i: (i, j)), pl.BlockSpec((tm, FF_HALF), lambda j, i: (i, j))] + x_out,
        out_shape=[S((t, 2 * FF), BF16), S((t, FF), BF16)] + x_shape, scratch_shapes=x_sems,
        compiler_params=_params(("arbitrary", "arbitrary")),
    )(h2, w_gu, *([] if exchange is None else exchange.arrays))


def _ffn_down_bwd(d_f, w_down, gu, name, tm=512):
    t, k = d_f.shape
    tm = min(tm, t)

    def body(a_ref, b_ref, gu_ref, o_ref):
        d_act = _dot_nt(a_ref[...], b_ref[...])
        _, vjp = jax.vjp(_swiglu_pair, gu_ref[...].astype(F32))
        o_ref[...] = vjp(d_act)[0].astype(o_ref.dtype)

    return pl.pallas_call(
        body, grid=(2, t // tm), name=name,
        in_specs=[pl.BlockSpec((tm, k), lambda j, i: (i, 0)), pl.BlockSpec((FF_HALF, k), lambda j, i: (j, 0)),
                  pl.BlockSpec((tm, FF), lambda j, i: (i, j))],
        out_specs=pl.BlockSpec((tm, FF), lambda j, i: (i, j)),
        out_shape=S((t, 2 * FF), BF16), compiler_params=_params(("arbitrary", "arbitrary")),
    )(d_f, w_down, gu)


def _conv_fwd(proj, col0, width, w, b, name, tm=512):
    t = proj.shape[0]
    tm = min(tm, t)
    cb = col0 // width

    def body(x_ref, w_ref, b_ref, o_ref, ext):
        @pl.when(pl.program_id(0) == 0)
        def _():
            ext[0:8, :] = jnp.zeros((8, width), F32)

        ext[8:8 + tm, :] = x_ref[...]
        y = b_ref[...] + w_ref[0:1, :] * ext[pl.ds(5, tm), :]
        for k in range(1, 4):
            y = y + w_ref[k:k + 1, :] * ext[pl.ds(5 + k, tm), :]
        o_ref[...] = jax.nn.silu(y)
        ext[0:8, :] = ext[tm:tm + 8, :]

    return pl.pallas_call(
        body, grid=(t // tm,), name=name,
        in_specs=[pl.BlockSpec((tm, width), lambda i: (i, cb)), pl.BlockSpec((4, width), lambda i: (0, 0)),
                  pl.BlockSpec((1, width), lambda i: (0, 0))],
        out_specs=pl.BlockSpec((tm, width), lambda i: (i, 0)),
        out_shape=S((t, width), F32), scratch_shapes=[pltpu.VMEM((tm + 8, width), F32)],
        compiler_params=_params(("arbitrary",)),
    )(proj, w, b)


def _conv_bwd(proj, col0, width, w, b, dact, name, tm=512):
    t = proj.shape[0]
    tm = min(tm, t)
    nb = t // tm
    cb = col0 // width
    hb = tm // 8

    def body(x_ref, halo_ref, d_ref, w_ref, b_ref, dx_ref, dw_ref, db_ref, extx, extd):
        i = pl.program_id(0)
        blk = nb - 1 - i

        @pl.when(i == 0)
        def _():
            extd[tm:tm + 8, :] = jnp.zeros((8, width), F32)
            dw_ref[...] = jnp.zeros((4, width), F32)
            db_ref[...] = jnp.zeros((1, width), F32)

        extx[0:8, :] = jnp.where(blk == 0, 0.0, halo_ref[...])
        extx[8:8 + tm, :] = x_ref[...]
        y = b_ref[...] + w_ref[0:1, :] * extx[pl.ds(5, tm), :]
        for k in range(1, 4):
            y = y + w_ref[k:k + 1, :] * extx[pl.ds(5 + k, tm), :]
        sig = jax.nn.sigmoid(y)
        dy = d_ref[...] * (sig * (1.0 + y * (1.0 - sig)))
        extd[0:tm, :] = dy
        dx = w_ref[0:1, :] * extd[pl.ds(3, tm), :]
        for k in range(1, 4):
            dx = dx + w_ref[k:k + 1, :] * extd[pl.ds(3 - k, tm), :]
        dx_ref[...] = dx.astype(dx_ref.dtype)
        for k in range(4):
            dw_ref[k:k + 1, :] += jnp.sum(dy * extx[pl.ds(5 + k, tm), :], axis=0, keepdims=True)
        db_ref[...] += jnp.sum(dy, axis=0, keepdims=True)
        extd[tm:tm + 8, :] = extd[0:8, :]

    return pl.pallas_call(
        body, grid=(nb,), name=name,
        in_specs=[pl.BlockSpec((tm, width), lambda i: (nb - 1 - i, cb)),
                  pl.BlockSpec((8, width), lambda i: (jnp.maximum((nb - 1 - i) * hb - 1, 0), cb)),
                  pl.BlockSpec((tm, width), lambda i: (nb - 1 - i, 0)),
                  pl.BlockSpec((4, width), lambda i: (0, 0)), pl.BlockSpec((1, width), lambda i: (0, 0))],
        out_specs=[pl.BlockSpec((tm, width), lambda i: (nb - 1 - i, 0)), pl.BlockSpec((4, width), lambda i: (0, 0)),
                   pl.BlockSpec((1, width), lambda i: (0, 0))],
        out_shape=[S((t, width), BF16), S((4, width), F32), S((1, width), F32)],
        scratch_shapes=[pltpu.VMEM((tm + 8, width), F32), pltpu.VMEM((tm + 8, width), F32)],
        compiler_params=_params(("arbitrary",)),
    )(proj, proj, dact, w, b)


SWA_BQ = 256
SWA_BACK = 128


def _swa_block(q, kw, vw, sinks, blk, bf16_operands=False):
    op = (lambda a: a.astype(BF16)) if bf16_operands else (lambda a: a)
    nq, nk = SWA_BQ, SWA_BQ + SWA_BACK
    r, j = _iota2(nq, nk)
    rel = r // CHUNK + 2 - j // CHUNK
    valid = (rel >= 0) & (rel <= 2) & (blk * (SWA_BQ // CHUNK) + j // CHUNK - 2 >= 0)
    dist = jnp.abs(r + SWA_BACK - j).astype(F32)
    outs = []
    for h in range(4):
        kv = h // 2
        qh = q[:, 64 * h:64 * h + 64]
        kh = kw[:, 64 * kv:64 * kv + 64]
        vh = vw[:, 64 * kv:64 * kv + 64]
        s = _dot_nt(op(qh), op(kh)) * 0.125 - ALIBI_SLOPES[h] * dist
        s = jnp.where(valid, s, -1e30)
        sink = _pick_col(sinks, h)
        m = jnp.maximum(jnp.max(s, axis=1, keepdims=True), sink)
        e = jnp.exp(s - m)
        den = jnp.sum(e, axis=1, keepdims=True) + jnp.exp(sink - m)
        outs.append(_dot(op(e / den), op(vh)))
    return jnp.concatenate(outs, axis=1)


def _swa_fwd(proj, sinks, name):
    t = proj.shape[0]
    qb, kb = PC_ATT // 256, PC_ATT // 128 + 2
    win = SWA_BQ + SWA_BACK

    def body(q_ref, k_ref, v_ref, s_ref, o_ref, kp, vp):
        i = pl.program_id(0)

        @pl.when(i == 0)
        def _():
            kp[0:SWA_BACK, :] = jnp.zeros((SWA_BACK, 128), F32)
            vp[0:SWA_BACK, :] = jnp.zeros((SWA_BACK, 128), F32)
            kp[SWA_BACK:, :] = k_ref[...]
            vp[SWA_BACK:, :] = v_ref[...]

        start = pl.multiple_of(i * SWA_BQ, SWA_BQ)
        o = _swa_block(q_ref[...], kp[pl.ds(start, win), :], vp[pl.ds(start, win), :], s_ref[...], i)
        o_ref[...] = o.astype(o_ref.dtype)

    return pl.pallas_call(
        body, grid=(t // SWA_BQ,), name=name,
        in_specs=[pl.BlockSpec((SWA_BQ, 256), lambda i: (i, qb)), pl.BlockSpec((t, 128), lambda i: (0, kb)),
                  pl.BlockSpec((t, 128), lambda i: (0, kb + 1)), pl.BlockSpec((1, LANES), lambda i: (0, 0))],
        out_specs=pl.BlockSpec((SWA_BQ, 256), lambda i: (i, 0)),
        out_shape=S((t, 256), BF16),
        scratch_shapes=[pltpu.VMEM((t + SWA_BACK, 128), F32), pltpu.VMEM((t + SWA_BACK, 128), F32)],
        compiler_params=_params(("arbitrary",)),
    )(proj, proj, proj, sinks)


def _swa_bwd(proj, sinks, dcat, dcol0, name):
    t = proj.shape[0]
    nb = t // SWA_BQ
    qb, kb = PC_ATT // 256, PC_ATT // 128 + 2
    db = dcol0 // 256
    win = SWA_BQ + SWA_BACK

    def body(q_ref, k_ref, v_ref, s_ref, do_ref, dq_ref, dk_ref, dv_ref, ds_ref, kp, vp, dkp, dvp):
        i = pl.program_id(0)

        @pl.when(i == 0)
        def _():
            kp[0:SWA_BACK, :] = jnp.zeros((SWA_BACK, 128), F32)
            vp[0:SWA_BACK, :] = jnp.zeros((SWA_BACK, 128), F32)
            kp[SWA_BACK:, :] = k_ref[...]
            vp[SWA_BACK:, :] = v_ref[...]
            dkp[...] = jnp.zeros_like(dkp)
            dvp[...] = jnp.zeros_like(dvp)
            ds_ref[...] = jnp.zeros_like(ds_ref)

        start = pl.multiple_of(i * SWA_BQ, SWA_BQ)
        _, vjp = jax.vjp(functools.partial(_swa_block, blk=i, bf16_operands=True), q_ref[...], kp[pl.ds(start, win), :],
                         vp[pl.ds(start, win), :], s_ref[...])
        dq, dkw, dvw, dsk = vjp(do_ref[...])
        dq_ref[...] = dq.astype(dq_ref.dtype)
        dkp[pl.ds(start, win), :] += dkw
        dvp[pl.ds(start, win), :] += dvw
        ds_ref[...] += dsk

        @pl.when(i == nb - 1)
        def _():
            dk_ref[...] = dkp[SWA_BACK:, :].astype(dk_ref.dtype)
            dv_ref[...] = dvp[SWA_BACK:, :].astype(dv_ref.dtype)

    return pl.pallas_call(
        body, grid=(nb,), name=name,
        in_specs=[pl.BlockSpec((SWA_BQ, 256), lambda i: (i, qb)), pl.BlockSpec((t, 128), lambda i: (0, kb)),
                  pl.BlockSpec((t, 128), lambda i: (0, kb + 1)), pl.BlockSpec((1, LANES), lambda i: (0, 0)),
                  pl.BlockSpec((SWA_BQ, 256), lambda i: (i, db))],
        out_specs=[pl.BlockSpec((SWA_BQ, 256), lambda i: (i, 0)), pl.BlockSpec((t, 128), lambda i: (0, 0)),
                   pl.BlockSpec((t, 128), lambda i: (0, 0)), pl.BlockSpec((1, LANES), lambda i: (0, 0))],
        out_shape=[S((t, 256), BF16), S((t, 128), BF16), S((t, 128), BF16), S((1, LANES), F32)],
        scratch_shapes=[pltpu.VMEM((t + SWA_BACK, 128), F32) for _ in range(4)],
        compiler_params=_params(("arbitrary",)),
    )(proj, proj, proj, sinks, dcat)


def _ssd_chunk(z, xbc, dt_raw, state, dtb, alog, dsk, nw, bf16_operands=False):
    n = z.shape[0]
    op = (lambda a: a.astype(BF16)) if bf16_operands else (lambda a: a)
    r, c = _iota2(n, n)
    tril = r >= c
    eye = (r == c).astype(F32)
    dt = jax.nn.softplus(dt_raw + dtb)
    acs = _dot(tril.astype(F32), dt * (-jnp.exp(alog)), HI)
    xs, bm, cm = xbc[:, :512], xbc[:, 512:768], xbc[:, 768:1024]
    heads = range(8)
    bg = [bm[:, 128 * g:128 * g + 128] for g in range(2)]
    cg = [cm[:, 128 * g:128 * g + 128] for g in range(2)]
    cb = [_dot_nt(op(cg[g]), op(bg[g])) for g in range(2)]
    dth = [_pick_col(dt, h) for h in heads]
    acol = [_pick_col(acs, h) for h in heads]
    arow = [_col_to_row(a, eye) for a in acol]
    lmat = [jnp.where(tril, jnp.exp(jnp.where(tril, a - b, 0.0)), 0.0) for a, b in zip(acol, arow)]
    xh = [xs[:, 64 * h:64 * h + 64] for h in heads]
    xc = [x * t for x, t in zip(xh, dth)]
    st = [state[64 * h:64 * h + 64, :] for h in heads]
    alast = [_pick_row(a, n - 1) for a in acol]
    y_in = [_dot(op(cb[h // 4] * lmat[h]), op(xc[h])) for h in heads]
    y_st = [_dot_nt(op(cg[h // 4]), op(st[h])) * jnp.exp(acol[h]) for h in heads]
    ys = [y_in[h] + y_st[h] + xh[h] * _pick_col(dsk, h) for h in heads]
    new_states = [st[h] * jnp.exp(alast[h]) + _dot_tn(op(xc[h] * jnp.exp(alast[h] - acol[h])), op(bg[h // 4]))
                  for h in heads]
    gg = jnp.concatenate(ys, axis=1) * jax.nn.silu(z)
    outs = []
    for gi in range(2):
        gv = gg[:, 256 * gi:256 * gi + 256]
        outs.append(gv * lax.rsqrt(jnp.mean(gv * gv, axis=-1, keepdims=True) + EPS))
    return jnp.concatenate(outs, axis=1) * nw, jnp.concatenate(new_states, axis=0)


def _ssd_fwd(proj, xbc, dtb, alog, dsk, nw, name, exchange=None):
    t = proj.shape[0]
    rows = min(SSD_CHUNK, t)
    nc = t // rows

    def body(z_ref, x_ref, dt_ref, dtb_ref, al_ref, d_ref, nw_ref, o_ref, st_ref, state):
        @pl.when(pl.program_id(0) == 0)
        def _():
            state[...] = jnp.zeros_like(state)

        st_ref[0] = state[...]
        o, ns = _ssd_chunk(z_ref[...], x_ref[...], dt_ref[...], state[...], dtb_ref[...], al_ref[...], d_ref[...],
                           nw_ref[...])
        o_ref[...] = o.astype(o_ref.dtype)
        state[...] = ns

    body, x_in, x_out, x_shape, x_sems = _hosted(exchange, 7, 2, nc, body)
    vec = pl.BlockSpec((1, LANES), lambda i: (0, 0))
    return pl.pallas_call(
        body, grid=(nc,), name=name,
        in_specs=[pl.BlockSpec((rows, 512), lambda i: (i, PC_SZ // 512)), pl.BlockSpec((rows, 1024), lambda i: (i, 0)),
                  pl.BlockSpec((rows, 128), lambda i: (i, PC_DT // 128)), vec, vec, vec,
                  pl.BlockSpec((1, 512), lambda i: (0, 0))] + x_in,
        out_specs=[pl.BlockSpec((rows, 512), lambda i: (i, 0)), pl.BlockSpec((1, 512, 128), lambda i: (i, 0, 0))] + x_out,
        out_shape=[S((t, 512), BF16), S((nc, 512, 128), F32)] + x_shape,
        scratch_shapes=[pltpu.VMEM((512, 128), F32)] + x_sems,
        compiler_params=_params(("arbitrary",)),
    )(proj, xbc, proj, dtb, alog, dsk, nw, *([] if exchange is None else exchange.arrays))


def _ssd_bwd(proj, xbc, states, dtb, alog, dsk, nw, dcat, dcol0, name, exchange=None):
    t = proj.shape[0]
    rows = min(SSD_CHUNK, t)
    nc = t // rows
    db = dcol0 // 512

    def body(z_ref, x_ref, dt_ref, st_ref, dtb_ref, al_ref, d_ref, nw_ref, do_ref,
             dz_ref, dx_ref, ddt_ref, gdtb_ref, gal_ref, gd_ref, gnw_ref, dstate):
        @pl.when(pl.program_id(0) == 0)
        def _():
            dstate[...] = jnp.zeros_like(dstate)
            gdtb_ref[...] = jnp.zeros_like(gdtb_ref)
            gal_ref[...] = jnp.zeros_like(gal_ref)
            gd_ref[...] = jnp.zeros_like(gd_ref)
            gnw_ref[...] = jnp.zeros_like(gnw_ref)

        _, vjp = jax.vjp(functools.partial(_ssd_chunk, bf16_operands=True), z_ref[...], x_ref[...], dt_ref[...],
                         st_ref[0], dtb_ref[...], al_ref[...], d_ref[...], nw_ref[...])
        dz, dx, ddt, dst, gdtb, gal, gd, gnw = vjp((do_ref[...], dstate[...]))
        dz_ref[...] = dz.astype(dz_ref.dtype)
        dx_ref[...] = dx
        ddt_ref[...] = ddt.astype(ddt_ref.dtype)
        dstate[...] = dst
        gdtb_ref[...] += gdtb
        gal_ref[...] += gal
        gd_ref[...] += gd
        gnw_ref[...] += gnw

    body, x_in, x_out, x_shape, x_sems = _hosted(exchange, 9, 7, nc, body)
    rev = lambda i: nc - 1 - i
    vec = pl.BlockSpec((1, LANES), lambda i: (0, 0))
    vec512 = pl.BlockSpec((1, 512), lambda i: (0, 0))
    return pl.pallas_call(
        body, grid=(nc,), name=name,
        in_specs=[pl.BlockSpec((rows, 512), lambda i: (rev(i), PC_SZ // 512)),
                  pl.BlockSpec((rows, 1024), lambda i: (rev(i), 0)),
                  pl.BlockSpec((rows, 128), lambda i: (rev(i), PC_DT // 128)),
                  pl.BlockSpec((1, 512, 128), lambda i: (rev(i), 0, 0)), vec, vec, vec, vec512,
                  pl.BlockSpec((rows, 512), lambda i: (rev(i), db))] + x_in,
        out_specs=[pl.BlockSpec((rows, 512), lambda i: (rev(i), 0)), pl.BlockSpec((rows, 1024), lambda i: (rev(i), 0)),
                   pl.BlockSpec((rows, 128), lambda i: (rev(i), 0)), vec, vec, vec, vec512] + x_out,
        out_shape=[S((t, 512), BF16), S((t, 1024), F32), S((t, 128), BF16), S((1, LANES), F32), S((1, LANES), F32),
                   S((1, LANES), F32), S((1, 512), F32)] + x_shape,
        scratch_shapes=[pltpu.VMEM((512, 128), F32)] + x_sems,
        compiler_params=_params(("arbitrary",)),
    )(proj, xbc, proj, states, dtb, alog, dsk, nw, dcat, *([] if exchange is None else exchange.arrays))


SOLVE_PREC = lax.Precision.HIGH


def _unit_lower_inverses(nas, known=None):
    def compute(ns):
        if known is not None:
            return tuple(known)
        n = ns[0].shape[0]
        r, c = _iota2(n, n)
        eye = (r == c).astype(F32)
        tm, pw = [eye + a for a in ns], list(ns)
        for _ in range(n.bit_length() - 2):
            pw = [_dot(p, p, SOLVE_PREC) for p in pw]
            tm = [t + _dot(t, p, SOLVE_PREC) for t, p in zip(tm, pw)]
        return tuple(tm)

    inv = jax.custom_vjp(compute)

    def fwd(ns):
        ts = compute(ns)
        return ts, ts

    def bwd(ts, gs):
        part = [_dot_nt(g, t, SOLVE_PREC) for g, t in zip(gs, ts)]
        return (tuple(_dot_tn(t, p, SOLVE_PREC) for t, p in zip(ts, part)),)

    inv.defvjp(fwd, bwd)
    return inv(nas)


def _gdn_chunk(qkv, z, ba, state, dtb, alog, nw, known_inverses=None, bf16_operands=False):
    n = qkv.shape[0]
    op = (lambda a: a.astype(BF16)) if bf16_operands else (lambda a: a)
    r, c = _iota2(n, n)
    tril = r >= c
    stril = r > c
    eye = (r == c).astype(F32)
    beta_all = jax.nn.sigmoid(ba)
    gcs = _dot(tril.astype(F32), -jnp.exp(alog) * jax.nn.softplus(ba + dtb), HI)
    heads = range(4)
    qh = [qkv[:, 64 * h:64 * h + 64] for h in heads]
    kh = [qkv[:, 256 + 64 * h:256 + 64 * h + 64] for h in heads]
    vh = [qkv[:, 512 + 64 * h:512 + 64 * h + 64] for h in heads]
    qn = [q * lax.rsqrt(jnp.sum(q * q, axis=-1, keepdims=True) + EPS) * 0.125 for q in qh]
    kn = [k * lax.rsqrt(jnp.sum(k * k, axis=-1, keepdims=True) + EPS) for k in kh]
    beta = [_pick_col(beta_all, h) for h in heads]
    gcol = [_pick_col(gcs, 4 + h) for h in heads]
    grow = [_col_to_row(g, eye) for g in gcol]
    decay = [jnp.where(tril, jnp.exp(jnp.where(tril, gc - gr, 0.0)), 0.0) for gc, gr in zip(gcol, grow)]
    kbeta = [k * b for k, b in zip(kn, beta)]
    kk = [_dot_nt(op(kb), op(k)) for kb, k in zip(kbeta, kn)]
    qk = [_dot_nt(op(q), op(k)) * dc for q, k, dc in zip(qn, kn, decay)]
    known = None if known_inverses is None else [known_inverses[n * h:n * (h + 1), :] for h in heads]
    tms = _unit_lower_inverses(tuple(-jnp.where(stril, x * dc, 0.0) for x, dc in zip(kk, decay)), known)
    rhs = [jnp.concatenate([v * b, kb * jnp.exp(g)], axis=1) for v, b, kb, g in zip(vh, beta, kbeta, gcol)]
    sol = [_dot(t, x, SOLVE_PREC) for t, x in zip(tms, rhs)]
    st = [state[64 * h:64 * h + 64, :] for h in heads]
    v_new = [s_[:, :64] - _dot(op(s_[:, 64:]), op(s)) for s_, s in zip(sol, st)]
    o = [_dot(op(q * jnp.exp(g)), op(s)) + _dot(op(x), op(vn)) for q, g, s, x, vn in zip(qn, gcol, st, qk, v_new)]
    glast = [_pick_row(g, n - 1) for g in gcol]
    new_states = [s * jnp.exp(gl) + _dot_tn(op(k * jnp.exp(gl - g)), op(vn))
                  for s, gl, k, g, vn in zip(st, glast, kn, gcol, v_new)]
    o = [x * lax.rsqrt(jnp.mean(x * x, axis=-1, keepdims=True) + EPS) * nw for x in o]
    outs = [x * jax.nn.silu(z[:, 64 * h:64 * h + 64]) for h, x in zip(heads, o)]
    return jnp.concatenate(outs, axis=1), jnp.concatenate(new_states, axis=0), jnp.concatenate(tms, axis=0)


class _Exchange:
    def __init__(self, arrays, out_shape, n_sems, start, finish):
        self.arrays, self.out_shape, self.n_sems, self.start, self.finish = arrays, out_shape, n_sems, start, finish


def _hosted(exchange, n_in, n_out, grid, body):
    if exchange is None:
        return body, [], [], [], []
    k_in, k_out = len(exchange.arrays), len(exchange.out_shape)
    grid = (grid,) if isinstance(grid, int) else tuple(grid)

    def hosted_body(*refs):
        ins, refs = refs[:n_in + k_in], refs[n_in + k_in:]
        outs, scratch = refs[:n_out + k_out], refs[n_out + k_out:]
        sems = scratch[-2:]
        first, last = True, True
        for axis, steps in enumerate(grid):
            first = first & (pl.program_id(axis) == 0)
            last = last & (pl.program_id(axis) == steps - 1)

        @pl.when(first)
        def _():
            exchange.start(ins[n_in:], outs[n_out:], sems)

        body(*ins[:n_in], *outs[:n_out], *scratch[:-2])

        @pl.when(last)
        def _():
            exchange.finish(ins[n_in:], outs[n_out:], sems)

    return hosted_body, [ANY] * k_in, [ANY] * k_out, list(exchange.out_shape), _sem_pairs(exchange.n_sems)


def _gdn_fwd(proj, qkv, dtb, alog, nw, name, exchange=None):
    t = proj.shape[0]
    rows = min(GDN_CHUNK, t)
    nc = t // rows

    def body(q_ref, z_ref, ba_ref, dtb_ref, al_ref, nw_ref, o_ref, st_ref, inv_ref, state):
        @pl.when(pl.program_id(0) == 0)
        def _():
            state[...] = jnp.zeros_like(state)

        st_ref[0] = state[...]
        o, ns, tms = _gdn_chunk(q_ref[...], z_ref[...], ba_ref[...], state[...], dtb_ref[...], al_ref[...], nw_ref[...],
                                bf16_operands=True)
        o_ref[...] = o.astype(o_ref.dtype)
        inv_ref[0] = tms
        state[...] = ns

    body, x_in, x_out, x_shape, x_sems = _hosted(exchange, 6, 3, nc, body)
    vec = pl.BlockSpec((1, LANES), lambda i: (0, 0))
    per_chunk = pl.BlockSpec((1, 256, 64), lambda i: (i, 0, 0))
    return pl.pallas_call(
        body, grid=(nc,), name=name,
        in_specs=[pl.BlockSpec((rows, 768), lambda i: (i, 0)), pl.BlockSpec((rows, 256), lambda i: (i, PC_GZ // 256)),
                  pl.BlockSpec((rows, 128), lambda i: (i, PC_BA // 128)), vec, vec, pl.BlockSpec((1, 64), lambda i: (0, 0))]
        + x_in,
        out_specs=[pl.BlockSpec((rows, 256), lambda i: (i, 0)), per_chunk,
                   pl.BlockSpec((1, 4 * rows, rows), lambda i: (i, 0, 0))] + x_out,
        out_shape=[S((t, 256), BF16), S((nc, 256, 64), F32), S((nc, 4 * rows, rows), F32)] + x_shape,
        scratch_shapes=[pltpu.VMEM((256, 64), F32)] + x_sems,
        compiler_params=_params(("arbitrary",)),
    )(qkv, proj, proj, dtb, alog, nw, *([] if exchange is None else exchange.arrays))


def _gdn_bwd(proj, qkv, states, inverses, dtb, alog, nw, dcat, dcol0, name, exchange=None):
    t = proj.shape[0]
    rows = min(GDN_CHUNK, t)
    nc = t // rows
    db = dcol0 // 256

    def body(q_ref, z_ref, ba_ref, st_ref, inv_ref, dtb_ref, al_ref, nw_ref, do_ref,
             dq_ref, dz_ref, dba_ref, gdtb_ref, gal_ref, gnw_ref, dstate):
        @pl.when(pl.program_id(0) == 0)
        def _():
            dstate[...] = jnp.zeros_like(dstate)
            gdtb_ref[...] = jnp.zeros_like(gdtb_ref)
            gal_ref[...] = jnp.zeros_like(gal_ref)
            gnw_ref[...] = jnp.zeros_like(gnw_ref)

        def chunk(*operands):
            return _gdn_chunk(*operands, known_inverses=inv_ref[0])[:2]

        _, vjp = jax.vjp(chunk, q_ref[...], z_ref[...], ba_ref[...], st_ref[0], dtb_ref[...], al_ref[...], nw_ref[...])
        dq, dz, dba, dst, gdtb, gal, gnw = vjp((do_ref[...], dstate[...]))
        dq_ref[...] = dq
        dz_ref[...] = dz.astype(dz_ref.dtype)
        dba_ref[...] = dba.astype(dba_ref.dtype)
        dstate[...] = dst
        gdtb_ref[...] += gdtb
        gal_ref[...] += gal
        gnw_ref[...] += gnw

    body, x_in, x_out, x_shape, x_sems = _hosted(exchange, 9, 6, nc, body)
    rev = lambda i: nc - 1 - i
    vec = pl.BlockSpec((1, LANES), lambda i: (0, 0))
    vec64 = pl.BlockSpec((1, 64), lambda i: (0, 0))
    per_chunk = pl.BlockSpec((1, 256, 64), lambda i: (rev(i), 0, 0))
    return pl.pallas_call(
        body, grid=(nc,), name=name,
        in_specs=[pl.BlockSpec((rows, 768), lambda i: (rev(i), 0)),
                  pl.BlockSpec((rows, 256), lambda i: (rev(i), PC_GZ // 256)),
                  pl.BlockSpec((rows, 128), lambda i: (rev(i), PC_BA // 128)),
                  per_chunk, pl.BlockSpec((1, 4 * rows, rows), lambda i: (rev(i), 0, 0)), vec, vec, vec64,
                  pl.BlockSpec((rows, 256), lambda i: (rev(i), db))] + x_in,
        out_specs=[pl.BlockSpec((rows, 768), lambda i: (rev(i), 0)), pl.BlockSpec((rows, 256), lambda i: (rev(i), 0)),
                   pl.BlockSpec((rows, 128), lambda i: (rev(i), 0)), vec, vec, vec64] + x_out,
        out_shape=[S((t, 768), F32), S((t, 256), BF16), S((t, 128), BF16), S((1, LANES), F32), S((1, LANES), F32),
                   S((1, 64), F32)] + x_shape,
        scratch_shapes=[pltpu.VMEM((256, 64), F32)] + x_sems,
        compiler_params=_params(("arbitrary",)),
    )(qkv, proj, proj, states, inverses, dtb, alog, nw, dcat, *([] if exchange is None else exchange.arrays))


def _pad_cols(w):
    z = jnp.zeros((w.shape[0], 120), w.dtype)
    return jnp.concatenate([w[:, 2056:2824], w[:, 2824:3080], w[:, 1024:2048], w[:, 0:512], w[:, 512:1024],
                            w[:, 2048:2056], z, w[:, 3080:3088], z], axis=1)


def _unpad_cols(g):
    return jnp.concatenate([g[:, PC_ATT:PC_ATT + 512], g[:, PC_SZ:PC_SZ + 512], g[:, PC_XBC:PC_XBC + 1024],
                            g[:, PC_DT:PC_DT + 8], g[:, PC_GQKV:PC_GQKV + 768], g[:, PC_GZ:PC_GZ + 256],
                            g[:, PC_BA:PC_BA + 8]], axis=1)


def _vec128(v, at=0):
    return jnp.zeros((1, LANES), F32).at[0, at:at + v.shape[0]].set(v)


def _in_weight(gathered):
    return _pad_cols(jnp.concatenate([gathered[k].reshape(D_MODEL, -1) for k in range(N_CHIPS)], axis=1))


def _matmul_weights(w_in, gathered):
    rows = lambda name: gathered[name].reshape(-1, gathered[name].shape[-1])
    w_out = rows("w_out")
    return dict(
        w_in=w_in,
        w_out=jnp.concatenate([w_out[256:768], w_out[0:256], w_out[768:1024]], axis=0),
        w_gu=_interleave_gu(rows("ffn_w_gate"), rows("ffn_w_up")),
        w_down=rows("ffn_w_down"))


def _small_operands(w, l):
    return dict(
        pre_mix=w["pre_mix_norm"][l][None], post_mix=w["post_mix_norm"][l][None],
        pre_ffn=w["pre_ffn_norm"][l][None], post_ffn=w["post_ffn_norm"][l][None],
        sinks=_vec128(w["attn_sinks"][l]),
        s_cw=w["ssd_conv_w"][l], s_cb=w["ssd_conv_b"][l][None],
        s_dtb=_vec128(w["ssd_dt_bias"][l]), s_alog=_vec128(w["ssd_A_log"][l]), s_d=_vec128(w["ssd_D"][l]),
        s_nw=w["ssd_norm_w"][l][None],
        g_cw=w["gdn_conv_w"][l], g_cb=jnp.zeros((1, 768), F32),
        g_dtb=_vec128(w["gdn_dt_bias"][l], 4), g_alog=_vec128(w["gdn_A_log"][l], 4), g_nw=w["gdn_norm_w"][l][None],
    )


RAW_GRADS = ("w_in_pad", "w_out_cat", "w_gu", "ffn_w_down")
DW_ROWS = 4096


def _local_step(x, target, lw, w_in0, gathers, matmul_weights, reducer):
    saved, landed = [], {}
    xin = x
    h = _prenorm(x, lw[0]["pre_mix"], "prenorm0")
    for l in range(DEPTH):
        p = lw[l]
        carry = (lambda kind: gathers[kind]) if l == 0 else (lambda kind: None)
        proj = _mm_nn(h, w_in0 if l == 0 else p["w_in"], 512, PC_TOT, F32, f"inproj{l}")
        xbc = _conv_fwd(proj, PC_XBC, 1024, p["s_cw"], p["s_cb"], f"ssd_conv{l}")
        gqkv = _conv_fwd(proj, PC_GQKV, 768, p["g_cw"], p["g_cb"], f"gdn_conv{l}")
        att = _swa_fwd(proj, p["sinks"], f"swa{l}")
        ssd, s_states, *landed_s = _ssd_fwd(proj, xbc, p["s_dtb"], p["s_alog"], p["s_d"], p["s_nw"], f"ssd{l}",
                                            carry("ssd"))
        gdn, g_states, g_inv, *landed_g = _gdn_fwd(proj, gqkv, p["g_dtb"], p["g_alog"], p["g_nw"], f"gdn{l}",
                                                   carry("gdn"))
        if l == 0:
            landed.update(ssd=landed_s, gdn=landed_g)
            p.update(matmul_weights(0, landed))
        cat = jnp.concatenate([ssd, att, gdn], axis=1)
        mix = _mm_nn(cat, p["w_out"], 512, 1024, F32, f"outproj{l}")
        x1, h2 = _resid_norm(xin, mix, p["post_mix"], p["pre_ffn"], f"postmix{l}")
        gu, act, *landed_u = _ffn_up(h2, p["w_gu"], f"ffn_gu{l}", exchange=carry("ffn_gu"))
        if l == 0:
            f, *landed_d = _mm_nn(act, p["w_down"], 512, 1024, F32, f"ffn_down{l}", carry("ffn_down"))
            landed.update(ffn_gu=landed_u, ffn_down=landed_d)
            lw[1].update(matmul_weights(1, landed))
        else:
            f = _mm_nn(act, p["w_down"], 512, 1024, F32, f"ffn_down{l}")
        saved.append(dict(xin=xin, h=h, proj=proj, xbc=xbc, gqkv=gqkv, s_states=s_states, g_states=g_states, g_inv=g_inv,
                          cat=cat, mix=mix, x1=x1, h2=h2, gu=gu, act=act, f=f))
        if l + 1 < DEPTH:
            xin, h = _resid_norm(x1, f, p["post_ffn"], lw[l + 1]["pre_mix"], f"postffn{l}")

    g = {k: [None] * DEPTH for k in SMALL + CONV + RAW_GRADS}
    last = saved[-1]
    d_x2, d_f, loss_part, g["post_ffn_norm"][DEPTH - 1] = _resid_loss(
        last["x1"], last["f"], lw[-1]["post_ffn"], target, "loss")
    early, late = [], []
    for l in reversed(range(DEPTH)):
        p, s = lw[l], saved[l]
        d_gu = _ffn_down_bwd(d_f, p["w_down"], s["gu"], f"d_act{l}")
        g["ffn_w_down"][l] = _mm_tn(s["act"], d_f, FF_HALF, 512, DW_ROWS, f"dw_down{l}")
        d_h2 = _mm_nn(d_gu, p["w_gu"], 512, 1024, F32, f"d_h2{l}")
        g["w_gu"][l] = _mm_tn(d_gu, s["h2"], FF_HALF, 512, DW_ROWS, f"dw_gu{l}")
        d_x1, d_mix, g["pre_ffn_norm"][l], g["post_mix_norm"][l] = _resid_norm_bwd(
            s["x1"], s["mix"], d_x2, d_h2, p["post_mix"], p["pre_ffn"], f"d_postmix{l}")
        d_cat = _mm_nt(d_mix, p["w_out"], 512, 1024, F32, f"d_cat{l}")
        g["w_out_cat"][l] = _mm_tn(s["cat"], d_mix, 512, 1024, DW_ROWS, f"dw_out{l}")
        d_q, d_k, d_v, g_sinks = _swa_bwd(s["proj"], p["sinks"], d_cat, 512, f"d_swa{l}")
        d_sz, d_xbc, d_dt, g_dtb, g_alog, g_d, g["ssd_norm_w"][l], *siblings = _ssd_bwd(
            s["proj"], s["xbc"], s["s_states"], p["s_dtb"], p["s_alog"], p["s_d"], p["s_nw"], d_cat, 0, f"d_ssd{l}",
            reducer.exchange(g) if l == 0 else None)
        d_gq, d_gz, d_ba, gg_dtb, gg_alog, g["gdn_norm_w"][l], *landed_b = _gdn_bwd(
            s["proj"], s["gqkv"], s["g_states"], s["g_inv"], p["g_dtb"], p["g_alog"], p["g_nw"], d_cat, 768, f"d_gdn{l}",
            reducer.scatter(siblings) if l == 0 else None)
        if l == 0:
            early = landed_b
        d_xbc_raw, g["ssd_conv_w"][l], g["ssd_conv_b"][l] = _conv_bwd(
            s["proj"], PC_XBC, 1024, p["s_cw"], p["s_cb"], d_xbc, f"d_ssd_conv{l}")
        d_gq_raw, g["gdn_conv_w"][l], _ = _conv_bwd(s["proj"], PC_GQKV, 768, p["g_cw"], p["g_cb"], d_gq, f"d_gdn_conv{l}")
        d_proj = jnp.concatenate([d_gq_raw, d_gz, d_xbc_raw, d_q, d_k, d_v, d_sz, d_dt, d_ba], axis=1)
        g["w_in_pad"][l] = _mm_tn(s["h"], d_proj, 512, PC_TOT // 2, DW_ROWS, f"dw_in{l}")
        if l == 0:
            d_h, *late = _mm_nt(d_proj, p["w_in"], 512, 1024, F32, f"d_h{l}", reducer.late(g))
        else:
            d_h = _mm_nt(d_proj, p["w_in"], 512, 1024, F32, f"d_h{l}")
        g["attn_sinks"][l] = g_sinks[0, :4]
        g["ssd_dt_bias"][l], g["ssd_A_log"][l], g["ssd_D"][l] = g_dtb[0, :8], g_alog[0, :8], g_d[0, :8]
        g["gdn_dt_bias"][l], g["gdn_A_log"][l] = gg_dtb[0, 4:8], gg_alog[0, 4:8]
        if l > 0:
            sp = saved[l - 1]
            d_x2, d_f, g["pre_mix_norm"][l], g["post_ffn_norm"][l - 1] = _resid_norm_bwd(
                s["xin"], sp["f"], d_x1, d_h, lw[l - 1]["post_ffn"], p["pre_mix"], f"d_postffn{l - 1}")
        else:
            grad_x, g["pre_mix_norm"][0] = _prenorm_bwd(s["xin"], d_x1, d_h, p["pre_mix"], "d_prenorm0")

    small = {k: jnp.stack([a.reshape(-1) for a in g[k]], axis=0) for k in SMALL + CONV}
    return loss_part, grad_x, small, {k: g[k] for k in RAW_GRADS}, early, late


BIG = ("w_in", "w_out", "ffn_w_gate", "ffn_w_up", "ffn_w_down")
CONV = ("ssd_conv_w", "gdn_conv_w")
TRANSPOSED = ("ffn_w_gate", "ffn_w_up")
SMALL = ("pre_mix_norm", "post_mix_norm", "pre_ffn_norm", "post_ffn_norm", "attn_sinks", "ssd_conv_b", "ssd_dt_bias",
         "ssd_A_log", "ssd_D", "ssd_norm_w", "gdn_dt_bias", "gdn_A_log", "gdn_norm_w")


def _row_tile(rows, cap):
    best = rows
    for t in range(8, min(cap, rows) + 1, 8):
        if rows % t == 0:
            best = t
    return best


SMALL_UNIT = 8 * LANES


def _pack_small(vals):
    rows = []
    for a in vals:
        f = a.reshape(-1)
        pad = -f.shape[0] % SMALL_UNIT
        rows.append(jnp.concatenate([f, jnp.zeros((pad,), F32)]).reshape(-1, LANES))
    return jnp.concatenate(rows, axis=0)


def _unpack_small(mat, shapes):
    out, r = [], 0
    for shp in shapes:
        n = math.prod(shp)
        nr = -(-n // SMALL_UNIT) * 8
        out.append(mat[r:r + nr].reshape(-1)[:n].reshape(shp))
        r += nr
    return out


def _place():
    x, y, c = lax.axis_index("x"), lax.axis_index("y"), lax.axis_index("c")
    chips = [(1 - x, y), (x, 1 - y), (1 - x, 1 - y)]
    return x, y, c, chips


ANY = pl.BlockSpec(memory_space=pl.ANY)


def _remote(src, dst, sems, k, to):
    send_sems, recv_sems = sems
    return pltpu.make_async_remote_copy(src_ref=src, dst_ref=dst, send_sem=send_sems.at[k], recv_sem=recv_sems.at[k],
                                        device_id=to, device_id_type=MESH)


def _sem_pairs(n):
    return [pltpu.SemaphoreType.DMA((n,)), pltpu.SemaphoreType.DMA((n,))]


def _run_exchange(exchange, name):
    k = len(exchange.arrays)

    def body(*refs):
        ins, outs, sems = refs[:k], refs[k:-2], refs[-2:]
        exchange.start(ins, outs, sems)
        exchange.finish(ins, outs, sems)

    return pl.pallas_call(
        body, name=name, in_specs=[ANY] * k, out_specs=[ANY] * len(exchange.out_shape),
        out_shape=list(exchange.out_shape), scratch_shapes=_sem_pairs(exchange.n_sems),
    )(*exchange.arrays)


def _gather_exchange(shards):
    n = len(shards)

    def sends(s_refs, g_refs, sems):
        x, y, c, chips = _place()
        me = 2 * x + y
        over_ici = [_remote(s_refs[i].at[c], g_refs[i].at[me, c], sems, 7 * i + j, (px, py, c))
                    for i in range(n) for j, (px, py) in enumerate(chips)]
        return over_ici + [_remote(s_refs[i], g_refs[i].at[me], sems, 7 * i + 6, (x, y, 1 - c)) for i in range(n)]

    def start(s_refs, g_refs, sems):
        for cp in sends(s_refs, g_refs, sems):
            cp.start()

    def finish(s_refs, g_refs, sems):
        x, y, c, chips = _place()
        sib = (x, y, 1 - c)
        passed = []
        for j, (px, py) in enumerate(chips):
            for i in range(n):
                landed = g_refs[i].at[2 * px + py, c]
                _remote(landed, landed, sems, 7 * i + j, (px, py, c)).wait_recv()
                fw = _remote(landed, landed, sems, 7 * i + 3 + j, sib)
                fw.start()
                passed.append(fw)
        for j, (px, py) in enumerate(chips):
            for i in range(n):
                landed = g_refs[i].at[2 * px + py, 1 - c]
                _remote(landed, landed, sems, 7 * i + 3 + j, sib).wait_recv()
        for i in range(n):
            mine = g_refs[i].at[2 * x + y]
            _remote(mine, mine, sems, 7 * i + 6, sib).wait_recv()
        for cp in sends(s_refs, g_refs, sems) + passed:
            cp.wait_send()

    return _Exchange(shards, [S((N_CHIPS,) + a.shape, a.dtype) for a in shards], 7 * n, start, finish)


def _halves_exchange(pieces):
    n = len(pieces)
    ds = [a for a, _ in pieces]

    def copies(d_refs, t_refs, sems):
        x, y, c, _ = _place()
        return [_remote(d_refs[i].at[:, pieces[i][1], :, 1 - c], t_refs[i], sems, i, (x, y, 1 - c)) for i in range(n)]

    def start(d_refs, t_refs, sems):
        for cp in copies(d_refs, t_refs, sems):
            cp.start()

    def finish(d_refs, t_refs, sems):
        for cp in copies(d_refs, t_refs, sems):
            cp.wait()

    return _Exchange(ds, [S((2, 2) + a.shape[4:], a.dtype) for a in ds], n, start, finish)


def _scatter_exchange(ps):
    n = len(ps)

    def copies(p_refs, u_refs, sems):
        x, y, c, chips = _place()
        return [_remote(p_refs[i].at[2 * px + py], u_refs[i].at[j], sems, 3 * i + j, (px, py, c))
                for j, (px, py) in enumerate(chips) for i in range(n)]

    def start(p_refs, u_refs, sems):
        for cp in copies(p_refs, u_refs, sems):
            cp.start()

    def finish(p_refs, u_refs, sems):
        for cp in copies(p_refs, u_refs, sems):
            cp.wait()

    return _Exchange(ps, [S((3,) + a.shape[1:], a.dtype) for a in ps], 3 * n, start, finish)


def _join_halves(qs, name):
    n = len(qs)

    def body(*refs):
        o_refs, sems = refs[n:2 * n], refs[2 * n:]
        x, y, c, _ = _place()
        cps = [_remote(o_refs[i].at[:, c], o_refs[i].at[:, c], sems, i, (x, y, 1 - c)) for i in range(n)]
        for cp in cps:
            cp.start()
        for i in range(n):
            other = o_refs[i].at[:, 1 - c]
            _remote(other, other, sems, i, (x, y, 1 - c)).wait_recv()
        for cp in cps:
            cp.wait_send()

    return pl.pallas_call(
        body, name=name, in_specs=[ANY] * n, out_specs=[ANY] * n,
        out_shape=[S(a.shape, a.dtype) for a in qs], input_output_aliases={i: i for i in range(n)},
        scratch_shapes=_sem_pairs(n),
    )(*qs)


def _gather_small(v, name):
    def body(v_ref, o_ref, send_sems, recv_sems, local_sem):
        x, y, c, _ = _place()
        me = 4 * x + 2 * y + c
        mine = pltpu.make_async_copy(v_ref, o_ref.at[me], local_sem)
        mine.start()
        cps = []
        for k in range(1, N_DEV):
            fx, fy, fc = (k >> 2) & 1, (k >> 1) & 1, k & 1
            peer = (x ^ fx, y ^ fy, c ^ fc)
            cps.append(pltpu.make_async_remote_copy(
                src_ref=v_ref, dst_ref=o_ref.at[me], send_sem=send_sems.at[k - 1], recv_sem=recv_sems.at[k - 1],
                device_id=peer, device_id_type=MESH))
        for cp in cps:
            cp.start()
        for k in range(1, N_DEV):
            fx, fy, fc = (k >> 2) & 1, (k >> 1) & 1, k & 1
            dst = o_ref.at[4 * (x ^ fx) + 2 * (y ^ fy) + (c ^ fc)]
            pltpu.make_async_remote_copy(src_ref=dst, dst_ref=dst, send_sem=send_sems.at[k - 1],
                                         recv_sem=recv_sems.at[k - 1], device_id=(x, y, c),
                                         device_id_type=MESH).wait_recv()
        for cp in cps:
            cp.wait_send()
        mine.wait()

    return pl.pallas_call(
        body, name=name, in_specs=[ANY], out_specs=ANY, out_shape=S((N_DEV,) + v.shape, F32),
        scratch_shapes=[pltpu.SemaphoreType.DMA((N_DEV - 1,)), pltpu.SemaphoreType.DMA((N_DEV - 1,)),
                        pltpu.SemaphoreType.DMA],
    )(v)


def _sum_leading(a, name):
    n, rows, cols = a.shape
    tm = _row_tile(rows, 640)

    def body(a_ref, o_ref):
        acc = a_ref[0]
        for k in range(1, n):
            acc = acc + a_ref[k]
        o_ref[...] = acc

    return pl.pallas_call(
        body, grid=(rows // tm,), name=name, in_specs=[pl.BlockSpec((n, tm, cols), lambda i: (0, i, 0))],
        out_specs=pl.BlockSpec((tm, cols), lambda i: (i, 0)), out_shape=S((rows, cols), F32),
        compiler_params=_params(("arbitrary",)),
    )(a)


def _add_sibling(place, piece, b, name):
    a, pick = piece
    n, hr, cols = b.shape
    tm = _row_tile(hr, 512)

    def body(place_ref, a_ref, b_ref, o_ref, o16_ref):
        tot = a_ref[0, 0, 0] + b_ref[...]
        o16_ref[...] = tot.astype(BF16)

        @pl.when(pl.program_id(1) == place_ref[1])
        def _():
            o_ref[...] = tot[0]

    spec = pl.BlockSpec((1, tm, cols), lambda i, k, pr: (k, i, 0))
    return pl.pallas_call(
        body, name=name, out_shape=[S((hr, cols), F32), S((n, hr, cols), BF16)],
        grid_spec=pltpu.PrefetchScalarGridSpec(
            num_scalar_prefetch=1, grid=(hr // tm, n),
            in_specs=[pl.BlockSpec((1, 1, 1, 1, tm, cols), lambda i, k, pr: (k // 2, pick, k % 2, pr[0], i, 0)), spec],
            out_specs=[pl.BlockSpec((tm, cols), lambda i, k, pr: (i, 0)), spec]),
        compiler_params=_params(("arbitrary", "arbitrary")),
    )(place, a, b)


def _add_chips(place, sums, others, layer, into, name):
    hr, cols = sums.shape
    tm = _row_tile(hr, 512)

    def body(place_ref, m_ref, o_ref, *rest):
        acc = m_ref[...]
        for k in range(others.shape[0]):
            acc = acc + o_ref[k].astype(F32)
        rest[-1][0, 0] = acc

    kept = [] if into is None else [into]
    return pl.pallas_call(
        body, name=name, out_shape=S((DEPTH, 2, hr, cols), F32),
        grid_spec=pltpu.PrefetchScalarGridSpec(
            num_scalar_prefetch=1, grid=(hr // tm,),
            in_specs=[pl.BlockSpec((tm, cols), lambda i, pr: (i, 0)),
                      pl.BlockSpec((others.shape[0], tm, cols), lambda i, pr: (0, i, 0))] + [ANY] * len(kept),
            out_specs=pl.BlockSpec((1, 1, tm, cols), lambda i, pr: (layer, pr[0], i, 0))),
        input_output_aliases={3: 0} if kept else {},
        compiler_params=_params(("arbitrary",)),
    )(place, sums, others, *kept)


def _adamw(wt, g, m, v, name):
    shape = wt.shape
    cols = shape[-1]
    rows = math.prod(shape[:-1])
    tm = rows
    for cand in (512, 256, 128, 64, 32, 16, 8):
        if rows % cand == 0:
            tm = cand
            break
    c1 = 1.0 - ADAM_B1 ** ADAM_STEP
    c2 = 1.0 - ADAM_B2 ** ADAM_STEP

    def body(w_ref, g_ref, m_ref, v_ref, d_ref, nm_ref, nv_ref):
        gv = g_ref[...]
        nm = ADAM_B1 * m_ref[...] + (1.0 - ADAM_B1) * gv
        nv = ADAM_B2 * v_ref[...] + (1.0 - ADAM_B2) * (gv * gv)
        d_ref[...] = -ADAM_LR * ((nm / c1) / (jnp.sqrt(nv / c2) + ADAM_EPS) + ADAM_WD * w_ref[...])
        nm_ref[...] = nm
        nv_ref[...] = nv

    spec = pl.BlockSpec((tm, cols), lambda i: (i, 0))
    outs = pl.pallas_call(
        body, grid=(rows // tm,), name=name, in_specs=[spec] * 4, out_specs=[spec] * 3,
        out_shape=[S((rows, cols), F32)] * 3, compiler_params=_params(("arbitrary",)),
    )(*[a.reshape(rows, cols) for a in (wt, g, m, v)])
    return [o.reshape(shape) for o in outs]


WEIGHTS = ('pre_mix_norm', 'post_mix_norm', 'pre_ffn_norm', 'post_ffn_norm', 'w_in', 'w_out', 'attn_sinks', 'ssd_conv_w',
           'ssd_conv_b', 'ssd_dt_bias', 'ssd_A_log', 'ssd_D', 'ssd_norm_w', 'gdn_conv_w', 'gdn_dt_bias', 'gdn_A_log',
           'gdn_norm_w', 'ffn_w_gate', 'ffn_w_up', 'ffn_w_down')


def _chip_piece(i, raw, shape):
    half = (2, shape[1] // 2, shape[2])
    if i in (2, 3):
        return raw["w_gu"].reshape((2, 2, 2) + half), i - 2
    if i == 0:
        g, width = _unpad_cols(raw["w_in_pad"]), shape[2]
        g = jnp.stack([g[:, k * width:(k + 1) * width] for k in range(N_CHIPS)])
    elif i == 1:
        g = raw["w_out_cat"]
        g = jnp.concatenate([g[512:768], g[0:512], g[768:1024]], axis=0)
    else:
        g = raw["ffn_w_down"]
    return g.reshape((2, 1, 2) + half), 0


def _step(x, target, wts, ms, vs):
    chip = 2 * lax.axis_index("x") + lax.axis_index("y")
    place = jnp.stack([lax.axis_index("c"), chip]).astype(jnp.int32)
    big_names = list(BIG)
    flip = lambda k, a: jnp.swapaxes(a, 1, 2) if k in TRANSPOSED else a
    wts, ms, vs = ({k: flip(k, a) for k, a in d.items()} for d in (wts, ms, vs))
    big_shapes = [wts[k].shape for k in big_names]
    halves = lambda a: a.reshape((2, a.shape[0] // 2) + a.shape[1:])

    shard = lambda name, l: halves(wts[name][l].astype(BF16))
    first = shard("w_in", 0)
    w_in0 = _in_weight(_run_exchange(_gather_exchange([first]), "gather_w_in0")[0])
    carried_by = {"ssd": [("w_out", 0), ("ffn_w_gate", 0)],
                  "gdn": [("ffn_w_up", 0), ("ffn_w_down", 0), ("w_in", 1), ("w_out", 1)],
                  "ffn_gu": [("ffn_w_gate", 1), ("ffn_w_up", 1)],
                  "ffn_down": [("ffn_w_down", 1)]}
    gathers = {kind: _gather_exchange([shard(*key) for key in keys]) for kind, keys in carried_by.items()}

    def matmul_weights(l, landed):
        gathered = {}
        for kind, keys in carried_by.items():
            for key, got in zip(keys, landed.get(kind, [])):
                if key[1] == l:
                    gathered[key[0]] = got.reshape((N_CHIPS,) + wts[key[0]].shape[1:])
        return _matmul_weights(w_in0 if l == 0 else _in_weight(gathered["w_in"]), gathered)

    conv = _gather_small(_pack_small([wts[k] for k in CONV]), "gather_conv_weights")
    conv = [_unpack_small(conv[2 * k], [wts[n].shape for n in CONV]) for k in range(N_CHIPS)]
    w_all = dict(wts)
    for i, n in enumerate(CONV):
        w_all[n] = jnp.concatenate([conv[k][i] for k in range(N_CHIPS)], axis=2)

    early_keys = [(DEPTH - 1, 0)] + [(l, i) for l in reversed(range(DEPTH)) for i in range(1, len(BIG))]
    late_keys = [(l, 0) for l in range(DEPTH - 1)]

    def pieces_of(keys, g):
        return [_chip_piece(i, {k: g[k][l] for k in RAW_GRADS}, big_shapes[i]) for l, i in keys]

    def add_siblings(tag, pieces, siblings):
        return [_add_sibling(place, p, t.reshape((N_CHIPS,) + t.shape[2:]), f"add_sibling_{tag}{n}")
                for n, (p, t) in enumerate(zip(pieces, siblings))]

    class Reducer:
        pieces, sums, late_sums = [], [], []

        def exchange(self, g):
            self.pieces = pieces_of(early_keys, g)
            return _halves_exchange(self.pieces)

        def scatter(self, siblings):
            self.sums = add_siblings("early", self.pieces, siblings)
            return _scatter_exchange([s16 for _, s16 in self.sums])

        def late(self, g):
            pieces = pieces_of(late_keys, g)
            siblings = _run_exchange(_halves_exchange(pieces), "exchange_halves_late")
            self.late_sums = add_siblings("late", pieces, siblings)
            return _scatter_exchange([s16 for _, s16 in self.late_sums])

    reducer = Reducer()

    loss_part, grad_x, small_g, raw, early, late = _local_step(
        x[0], target[0], [_small_operands(w_all, l) for l in range(DEPTH)], w_in0, gathers, matmul_weights, reducer)

    reduced = {}
    for tag, keys, sums, others in (("early", early_keys, reducer.sums, early), ("late", late_keys, reducer.late_sums, late)):
        for n, ((l, i), (s32, _), o) in enumerate(zip(keys, sums, others)):
            reduced[i] = _add_chips(place, s32, o, l, reduced.get(i), f"add_chips_{tag}{n}")
    joined = _join_halves([reduced[i] for i in range(len(BIG))], "join_halves")
    g_all = {k: q.reshape(shp) for k, q, shp in zip(big_names, joined, big_shapes)}

    names = SMALL + CONV
    packed = _pack_small([small_g[k] for k in names] + [loss_part])
    small_sum = _sum_leading(_gather_small(packed, "gather_small_grads"), "add_small")
    vals = _unpack_small(small_sum, [small_g[k].shape for k in names] + [(1, LANES)])
    loss = vals[-1][0, 0]
    for k, v in zip(names, vals[:-1]):
        if k in CONV:
            width = wts[k].shape[2]
            v = lax.dynamic_slice_in_dim(v.reshape(DEPTH, 4, -1), chip * width, width, axis=2)
        g_all[k] = v.reshape(wts[k].shape)

    shapes = [wts[k].shape for k in names]
    d_s, m_s, v_s = _adamw(_pack_small([wts[k] for k in names]), _pack_small([g_all[k] for k in names]),
                           _pack_small([ms[k] for k in names]), _pack_small([vs[k] for k in names]), "adamw_small")
    upd = dict(zip(names, zip(_unpack_small(d_s, shapes), _unpack_small(m_s, shapes), _unpack_small(v_s, shapes))))
    for k in big_names:
        upd[k] = _adamw(wts[k], g_all[k], ms[k], vs[k], f"adamw_{k}")
    return (loss, grad_x[None], *[flip(k, g_all[k]) for k in WEIGHTS], *[flip(k, upd[k][0]) for k in WEIGHTS],
            *[flip(k, upd[k][1]) for k in WEIGHTS], *[flip(k, upd[k][2]) for k in WEIGHTS])


def kernel(x, pre_mix_norm, post_mix_norm, pre_ffn_norm, post_ffn_norm, w_in, w_out, attn_sinks, ssd_conv_w, ssd_conv_b, ssd_dt_bias, ssd_A_log, ssd_D, ssd_norm_w, gdn_conv_w, gdn_dt_bias, gdn_A_log, gdn_norm_w, ffn_w_gate, ffn_w_up, ffn_w_down, loss_target, m_pre_mix_norm, m_post_mix_norm, m_pre_ffn_norm, m_post_ffn_norm, m_w_in, m_w_out, m_attn_sinks, m_ssd_conv_w, m_ssd_conv_b, m_ssd_dt_bias, m_ssd_A_log, m_ssd_D, m_ssd_norm_w, m_gdn_conv_w, m_gdn_dt_bias, m_gdn_A_log, m_gdn_norm_w, m_ffn_w_gate, m_ffn_w_up, m_ffn_w_down, v_pre_mix_norm, v_post_mix_norm, v_pre_ffn_norm, v_post_ffn_norm, v_w_in, v_w_out, v_attn_sinks, v_ssd_conv_w, v_ssd_conv_b, v_ssd_dt_bias, v_ssd_A_log, v_ssd_D, v_ssd_norm_w, v_gdn_conv_w, v_gdn_dt_bias, v_gdn_A_log, v_gdn_norm_w, v_ffn_w_gate, v_ffn_w_up, v_ffn_w_down):
    wts = dict(zip(WEIGHTS, (pre_mix_norm, post_mix_norm, pre_ffn_norm, post_ffn_norm, w_in, w_out, attn_sinks, ssd_conv_w, ssd_conv_b, ssd_dt_bias, ssd_A_log, ssd_D, ssd_norm_w, gdn_conv_w, gdn_dt_bias, gdn_A_log, gdn_norm_w, ffn_w_gate, ffn_w_up, ffn_w_down)))
    ms = dict(zip(WEIGHTS, (m_pre_mix_norm, m_post_mix_norm, m_pre_ffn_norm, m_post_ffn_norm, m_w_in, m_w_out, m_attn_sinks, m_ssd_conv_w, m_ssd_conv_b, m_ssd_dt_bias, m_ssd_A_log, m_ssd_D, m_ssd_norm_w, m_gdn_conv_w, m_gdn_dt_bias, m_gdn_A_log, m_gdn_norm_w, m_ffn_w_gate, m_ffn_w_up, m_ffn_w_down)))
    vs = dict(zip(WEIGHTS, (v_pre_mix_norm, v_post_mix_norm, v_pre_ffn_norm, v_post_ffn_norm, v_w_in, v_w_out, v_attn_sinks, v_ssd_conv_w, v_ssd_conv_b, v_ssd_dt_bias, v_ssd_A_log, v_ssd_D, v_ssd_norm_w, v_gdn_conv_w, v_gdn_dt_bias, v_gdn_A_log, v_gdn_norm_w, v_ffn_w_gate, v_ffn_w_up, v_ffn_w_down)))
    return _step(x, loss_target, wts, ms, vs)
```

```python
import functools
import math

import jax
import jax.numpy as jnp
from jax import lax
from jax.experimental import pallas as pl
from jax.experimental.pallas import tpu as pltpu

F32, BF16 = jnp.float32, jnp.bfloat16
HI = lax.Precision.HIGHEST
MESH = pl.DeviceIdType.MESH
S = jax.ShapeDtypeStruct

D_MODEL = 1024
DEPTH = 2
CHUNK = 64
SSD_CHUNK = 256
GDN_CHUNK = 128
EPS = 1e-6
FF = 2816
N_CHIPS = 4
N_DEV = 8
LANES = 128

VMEM_LIMIT_BYTES = 56 * 1024 * 1024

PC_GQKV, PC_GZ, PC_XBC, PC_ATT, PC_SZ, PC_DT, PC_BA, PC_TOT = 0, 768, 1024, 2048, 2560, 3072, 3200, 3328

ADAM_LR, ADAM_B1, ADAM_B2, ADAM_EPS, ADAM_WD, ADAM_STEP = 0.001, 0.9, 0.999, 1e-08, 0.01, 10

ALIBI_SLOPES = tuple(2.0 ** (-8.0 * (h + 1) / 4) for h in range(4))


def _params(sem=None, **kw):
    if sem is not None:
        kw["dimension_semantics"] = sem
    return pltpu.CompilerParams(vmem_limit_bytes=VMEM_LIMIT_BYTES, **kw)


def _dot(a, b, prec=None):
    return jnp.dot(a, b, precision=prec, preferred_element_type=F32)


def _dot_nt(a, b, prec=None):
    return lax.dot_general(a, b, (((1,), (1,)), ((), ())), precision=prec, preferred_element_type=F32)


def _dot_tn(a, b, prec=None):
    return lax.dot_general(a, b, (((0,), (0,)), ((), ())), precision=prec, preferred_element_type=F32)


def _iota2(n, m):
    return lax.broadcasted_iota(jnp.int32, (n, m), 0), lax.broadcasted_iota(jnp.int32, (n, m), 1)


def _pick_col(arr, idx):
    ci = lax.broadcasted_iota(jnp.int32, arr.shape, 1)
    return jnp.sum(jnp.where(ci == idx, arr, 0.0), axis=1, keepdims=True)


def _pick_row(arr, idx):
    ri = lax.broadcasted_iota(jnp.int32, arr.shape, 0)
    return jnp.sum(jnp.where(ri == idx, arr, 0.0), axis=0, keepdims=True)


def _col_to_row(col, eye):
    return jnp.sum(eye * col, axis=0, keepdims=True)


def _rms(x, w):
    return x * lax.rsqrt(jnp.mean(x * x, axis=-1, keepdims=True) + EPS) * w


def _mm_nn(a, b, tm, tn, out_dtype, name, exchange=None):
    m, k = a.shape
    n = b.shape[1]
    tm, tn = min(tm, m), min(tn, n)
    grid = (n // tn, m // tm)

    def body(a_ref, b_ref, o_ref):
        o_ref[...] = _dot(a_ref[...], b_ref[...]).astype(o_ref.dtype)

    body, x_in, x_out, x_shape, x_sems = _hosted(exchange, 2, 1, grid, body)
    outs = pl.pallas_call(
        body, grid=grid, name=name,
        in_specs=[pl.BlockSpec((tm, k), lambda j, i: (i, 0)), pl.BlockSpec((k, tn), lambda j, i: (0, j))] + x_in,
        out_specs=[pl.BlockSpec((tm, tn), lambda j, i: (i, j))] + x_out,
        out_shape=[S((m, n), out_dtype)] + x_shape, scratch_shapes=x_sems,
        compiler_params=_params(("arbitrary", "arbitrary")),
    )(a, b, *([] if exchange is None else exchange.arrays))
    return outs[0] if exchange is None else outs


def _mm_nt(a, b, tm, tn, out_dtype, name, exchange=None):
    m, k = a.shape
    n = b.shape[0]
    tm, tn = min(tm, m), min(tn, n)
    grid = (n // tn, m // tm)

    def body(a_ref, b_ref, o_ref):
        o_ref[...] = _dot_nt(a_ref[...], b_ref[...]).astype(o_ref.dtype)

    body, x_in, x_out, x_shape, x_sems = _hosted(exchange, 2, 1, grid, body)
    outs = pl.pallas_call(
        body, grid=grid, name=name,
        in_specs=[pl.BlockSpec((tm, k), lambda j, i: (i, 0)), pl.BlockSpec((tn, k), lambda j, i: (j, 0))] + x_in,
        out_specs=[pl.BlockSpec((tm, tn), lambda j, i: (i, j))] + x_out,
        out_shape=[S((m, n), out_dtype)] + x_shape, scratch_shapes=x_sems,
        compiler_params=_params(("arbitrary", "arbitrary")),
    )(a, b, *([] if exchange is None else exchange.arrays))
    return outs[0] if exchange is None else outs


def _mm_tn(a, b, tm, tn, tk, name):
    t, m = a.shape
    n = b.shape[1]
    tm, tn, tk = min(tm, m), min(tn, n), min(tk, t)

    def body(a_ref, b_ref, o_ref):
        part = _dot_tn(a_ref[...], b_ref[...])

        @pl.when(pl.program_id(2) == 0)
        def _():
            o_ref[...] = part

        @pl.when(pl.program_id(2) > 0)
        def _():
            o_ref[...] += part

    return pl.pallas_call(
        body, grid=(m // tm, n // tn, t // tk), name=name,
        in_specs=[pl.BlockSpec((tk, tm), lambda i, j, k: (k, i)), pl.BlockSpec((tk, tn), lambda i, j, k: (k, j))],
        out_specs=pl.BlockSpec((tm, tn), lambda i, j, k: (i, j)),
        out_shape=S((m, n), F32), compiler_params=_params(("arbitrary", "arbitrary", "arbitrary")),
    )(a, b)


def _rowcall(fn, rows, params, row_outs, acc_outs, name, tm=512, exchange=None):
    t = rows[0].shape[0]
    tm = min(tm, t)
    n_in = len(rows) + len(params)
    n_ro = len(row_outs)

    def body(*refs):
        ro, ao = fn(*[r[...] for r in refs[:n_in]])
        for ref, v in zip(refs[n_in:n_in + n_ro], ro):
            ref[...] = v.astype(ref.dtype)
        acc_refs = refs[n_in + n_ro:]
        if acc_refs:
            @pl.when(pl.program_id(0) == 0)
            def _():
                for ref, v in zip(acc_refs, ao):
                    ref[...] = v

            @pl.when(pl.program_id(0) > 0)
            def _():
                for ref, v in zip(acc_refs, ao):
                    ref[...] += v

    in_specs = [pl.BlockSpec((tm, r.shape[1]), lambda i: (i, 0)) for r in rows]
    in_specs += [pl.BlockSpec(p.shape, lambda i: (0, 0)) for p in params]
    out_specs = [pl.BlockSpec((tm, c), lambda i: (i, 0)) for c, _ in row_outs]
    out_specs += [pl.BlockSpec(shape, lambda i: (0, 0)) for shape in acc_outs]
    out_shape = [S((t, c), dt) for c, dt in row_outs] + [S(shape, F32) for shape in acc_outs]
    body, x_in, x_out, x_shape, x_sems = _hosted(exchange, n_in, len(out_shape), t // tm, body)
    return pl.pallas_call(
        body, grid=(t // tm,), name=name, in_specs=in_specs + x_in, out_specs=out_specs + x_out,
        out_shape=out_shape + x_shape, scratch_shapes=x_sems, compiler_params=_params(("arbitrary",)),
    )(*rows, *params, *([] if exchange is None else exchange.arrays))


def _prenorm(x, w, name, exchange=None):
    def fn(x, w):
        return (_rms(x, w),), ()
    return _rowcall(fn, [x], [w], [(D_MODEL, BF16)], [], name, exchange=exchange)


def _resid_norm(xin, m, w_post, w_next, name):
    def fn(xin, m, w_post, w_next):
        xo = xin + _rms(m, w_post)
        return (xo, _rms(xo, w_next)), ()
    return _rowcall(fn, [xin, m], [w_post, w_next], [(D_MODEL, F32), (D_MODEL, BF16)], [], name)


def _resid_loss(xin, m, w_post, target, name):
    def fn(xin, m, target, w_post):
        r, vjp = jax.vjp(_rms, m, w_post)
        err = xin + r - target
        dy = err * (1.0 / D_MODEL)
        dm, dw = vjp(dy)
        tot = jnp.sum(jnp.sum(err * err, axis=1, keepdims=True), axis=0, keepdims=True) * (0.5 / D_MODEL)
        lane = lax.broadcasted_iota(jnp.int32, (1, LANES), 1)
        return (dy, dm), (jnp.where(lane == 0, tot, 0.0), dw)
    return _rowcall(fn, [xin, m, target], [w_post], [(D_MODEL, F32), (D_MODEL, BF16)],
                    [(1, LANES), (1, D_MODEL)], name)


def _resid_norm_bwd(x_out, m, d_direct, dh, w_post, w_next, name):
    def fn(x_out, m, d_direct, dh, w_post, w_next):
        _, vjp_n = jax.vjp(_rms, x_out, w_next)
        dx, dwn = vjp_n(dh)
        d_total = d_direct + dx
        _, vjp_p = jax.vjp(_rms, m, w_post)
        dm, dwp = vjp_p(d_total)
        return (d_total, dm), (dwn, dwp)
    return _rowcall(fn, [x_out, m, d_direct, dh], [w_post, w_next], [(D_MODEL, F32), (D_MODEL, BF16)],
                    [(1, D_MODEL), (1, D_MODEL)], name)


def _prenorm_bwd(x, d_direct, dh, w, name):
    def fn(x, d_direct, dh, w):
        _, vjp = jax.vjp(_rms, x, w)
        dx, dw = vjp(dh)
        return (d_direct + dx,), (dw,)
    return _rowcall(fn, [x, d_direct, dh], [w], [(D_MODEL, F32)], [(1, D_MODEL)], name)


FF_HALF = FF // 2


def _interleave_gu(gate_t, up_t):
    return jnp.concatenate([gate_t[:FF_HALF], up_t[:FF_HALF], gate_t[FF_HALF:], up_t[FF_HALF:]], axis=0)


def _swiglu_pair(gu):
    n = gu.shape[1] // 2
    return jax.nn.silu(gu[:, :n]) * gu[:, n:]


def _ffn_up(h2, w_gu, name, tm=512, exchange=None):
    t, k = h2.shape
    tm = min(tm, t)
    grid = (2, t // tm)

    def body(a_ref, b_ref, gu_ref, act_ref):
        gu = _dot_nt(a_ref[...], b_ref[...])
        gu_ref[...] = gu.astype(gu_ref.dtype)
        act_ref[...] = _swiglu_pair(gu).astype(act_ref.dtype)

    body, x_in, x_out, x_shape, x_sems = _hosted(exchange, 2, 2, grid, body)
    return pl.pallas_call(
        body, grid=grid, name=name,
        in_specs=[pl.BlockSpec((tm, k), lambda j, i: (i, 0)), pl.BlockSpec((FF, k), lambda j, i: (j, 0))] + x_in,
        out_specs=[pl.BlockSpec((tm, FF), lambda j, i: (i, j)), pl.BlockSpec((tm, FF_HALF), lambda j, i: (i, j))] + x_out,
        out_shape=[S((t, 2 * FF), BF16), S((t, FF), BF16)] + x_shape, scratch_shapes=x_sems,
        compiler_params=_params(("arbitrary", "arbitrary")),
    )(h2, w_gu, *([] if exchange is None else exchange.arrays))


def _ffn_down_bwd(d_f, w_down, gu, name, tm=512):
    t, k = d_f.shape
    tm = min(tm, t)

    def body(a_ref, b_ref, gu_ref, o_ref):
        d_act = _dot_nt(a_ref[...], b_ref[...])
        _, vjp = jax.vjp(_swiglu_pair, gu_ref[...].astype(F32))
        o_ref[...] = vjp(d_act)[0].astype(o_ref.dtype)

    return pl.pallas_call(
        body, grid=(2, t // tm), name=name,
        in_specs=[pl.BlockSpec((tm, k), lambda j, i: (i, 0)), pl.BlockSpec((FF_HALF, k), lambda j, i: (j, 0)),
                  pl.BlockSpec((tm, FF), lambda j, i: (i, j))],
        out_specs=pl.BlockSpec((tm, FF), lambda j, i: (i, j)),
        out_shape=S((t, 2 * FF), BF16), compiler_params=_params(("arbitrary", "arbitrary")),
    )(d_f, w_down, gu)


def _conv_fwd(proj, col0, width, w, b, name, tm=512):
    t = proj.shape[0]
    tm = min(tm, t)
    cb = col0 // width

    def body(x_ref, w_ref, b_ref, o_ref, ext):
        @pl.when(pl.program_id(0) == 0)
        def _():
            ext[0:8, :] = jnp.zeros((8, width), F32)

        ext[8:8 + tm, :] = x_ref[...]
        y = b_ref[...] + w_ref[0:1, :] * ext[pl.ds(5, tm), :]
        for k in range(1, 4):
            y = y + w_ref[k:k + 1, :] * ext[pl.ds(5 + k, tm), :]
        o_ref[...] = jax.nn.silu(y)
        ext[0:8, :] = ext[tm:tm + 8, :]

    return pl.pallas_call(
        body, grid=(t // tm,), name=name,
        in_specs=[pl.BlockSpec((tm, width), lambda i: (i, cb)), pl.BlockSpec((4, width), lambda i: (0, 0)),
                  pl.BlockSpec((1, width), lambda i: (0, 0))],
        out_specs=pl.BlockSpec((tm, width), lambda i: (i, 0)),
        out_shape=S((t, width), F32), scratch_shapes=[pltpu.VMEM((tm + 8, width), F32)],
        compiler_params=_params(("arbitrary",)),
    )(proj, w, b)


def _conv_bwd(proj, col0, width, w, b, dact, name, tm=512):
    t = proj.shape[0]
    tm = min(tm, t)
    nb = t // tm
    cb = col0 // width
    hb = tm // 8

    def body(x_ref, halo_ref, d_ref, w_ref, b_ref, dx_ref, dw_ref, db_ref, extx, extd):
        i = pl.program_id(0)
        blk = nb - 1 - i

        @pl.when(i == 0)
        def _():
            extd[tm:tm + 8, :] = jnp.zeros((8, width), F32)
            dw_ref[...] = jnp.zeros((4, width), F32)
            db_ref[...] = jnp.zeros((1, width), F32)

        extx[0:8, :] = jnp.where(blk == 0, 0.0, halo_ref[...])
        extx[8:8 + tm, :] = x_ref[...]
        y = b_ref[...] + w_ref[0:1, :] * extx[pl.ds(5, tm), :]
        for k in range(1, 4):
            y = y + w_ref[k:k + 1, :] * extx[pl.ds(5 + k, tm), :]
        sig = jax.nn.sigmoid(y)
        dy = d_ref[...] * (sig * (1.0 + y * (1.0 - sig)))
        extd[0:tm, :] = dy
        dx = w_ref[0:1, :] * extd[pl.ds(3, tm), :]
        for k in range(1, 4):
            dx = dx + w_ref[k:k + 1, :] * extd[pl.ds(3 - k, tm), :]
        dx_ref[...] = dx.astype(dx_ref.dtype)
        for k in range(4):
            dw_ref[k:k + 1, :] += jnp.sum(dy * extx[pl.ds(5 + k, tm), :], axis=0, keepdims=True)
        db_ref[...] += jnp.sum(dy, axis=0, keepdims=True)
        extd[tm:tm + 8, :] = extd[0:8, :]

    return pl.pallas_call(
        body, grid=(nb,), name=name,
        in_specs=[pl.BlockSpec((tm, width), lambda i: (nb - 1 - i, cb)),
                  pl.BlockSpec((8, width), lambda i: (jnp.maximum((nb - 1 - i) * hb - 1, 0), cb)),
                  pl.BlockSpec((tm, width), lambda i: (nb - 1 - i, 0)),
                  pl.BlockSpec((4, width), lambda i: (0, 0)), pl.BlockSpec((1, width), lambda i: (0, 0))],
        out_specs=[pl.BlockSpec((tm, width), lambda i: (nb - 1 - i, 0)), pl.BlockSpec((4, width), lambda i: (0, 0)),
                   pl.BlockSpec((1, width), lambda i: (0, 0))],
        out_shape=[S((t, width), BF16), S((4, width), F32), S((1, width), F32)],
        scratch_shapes=[pltpu.VMEM((tm + 8, width), F32), pltpu.VMEM((tm + 8, width), F32)],
        compiler_params=_params(("arbitrary",)),
    )(proj, proj, dact, w, b)


SWA_BQ = 256
SWA_BACK = 128


def _swa_block(q, kw, vw, sinks, blk, bf16_operands=False):
    op = (lambda a: a.astype(BF16)) if bf16_operands else (lambda a: a)
    nq, nk = SWA_BQ, SWA_BQ + SWA_BACK
    r, j = _iota2(nq, nk)
    rel = r // CHUNK + 2 - j // CHUNK
    valid = (rel >= 0) & (rel <= 2) & (blk * (SWA_BQ // CHUNK) + j // CHUNK - 2 >= 0)
    dist = jnp.abs(r + SWA_BACK - j).astype(F32)
    outs = []
    for h in range(4):
        kv = h // 2
        qh = q[:, 64 * h:64 * h + 64]
        kh = kw[:, 64 * kv:64 * kv + 64]
        vh = vw[:, 64 * kv:64 * kv + 64]
        s = _dot_nt(op(qh), op(kh)) * 0.125 - ALIBI_SLOPES[h] * dist
        s = jnp.where(valid, s, -1e30)
        sink = _pick_col(sinks, h)
        m = jnp.maximum(jnp.max(s, axis=1, keepdims=True), sink)
        e = jnp.exp(s - m)
        den = jnp.sum(e, axis=1, keepdims=True) + jnp.exp(sink - m)
        outs.append(_dot(op(e / den), op(vh)))
    return jnp.concatenate(outs, axis=1)


def _swa_fwd(proj, sinks, name):
    t = proj.shape[0]
    qb, kb = PC_ATT // 256, PC_ATT // 128 + 2
    win = SWA_BQ + SWA_BACK

    def body(q_ref, k_ref, v_ref, s_ref, o_ref, kp, vp):
        i = pl.program_id(0)

        @pl.when(i == 0)
        def _():
            kp[0:SWA_BACK, :] = jnp.zeros((SWA_BACK, 128), F32)
            vp[0:SWA_BACK, :] = jnp.zeros((SWA_BACK, 128), F32)
            kp[SWA_BACK:, :] = k_ref[...]
            vp[SWA_BACK:, :] = v_ref[...]

        start = pl.multiple_of(i * SWA_BQ, SWA_BQ)
        o = _swa_block(q_ref[...], kp[pl.ds(start, win), :], vp[pl.ds(start, win), :], s_ref[...], i)
        o_ref[...] = o.astype(o_ref.dtype)

    return pl.pallas_call(
        body, grid=(t // SWA_BQ,), name=name,
        in_specs=[pl.BlockSpec((SWA_BQ, 256), lambda i: (i, qb)), pl.BlockSpec((t, 128), lambda i: (0, kb)),
                  pl.BlockSpec((t, 128), lambda i: (0, kb + 1)), pl.BlockSpec((1, LANES), lambda i: (0, 0))],
        out_specs=pl.BlockSpec((SWA_BQ, 256), lambda i: (i, 0)),
        out_shape=S((t, 256), BF16),
        scratch_shapes=[pltpu.VMEM((t + SWA_BACK, 128), F32), pltpu.VMEM((t + SWA_BACK, 128), F32)],
        compiler_params=_params(("arbitrary",)),
    )(proj, proj, proj, sinks)


def _swa_bwd(proj, sinks, dcat, dcol0, name):
    t = proj.shape[0]
    nb = t // SWA_BQ
    qb, kb = PC_ATT // 256, PC_ATT // 128 + 2
    db = dcol0 // 256
    win = SWA_BQ + SWA_BACK

    def body(q_ref, k_ref, v_ref, s_ref, do_ref, dq_ref, dk_ref, dv_ref, ds_ref, kp, vp, dkp, dvp):
        i = pl.program_id(0)

        @pl.when(i == 0)
        def _():
            kp[0:SWA_BACK, :] = jnp.zeros((SWA_BACK, 128), F32)
            vp[0:SWA_BACK, :] = jnp.zeros((SWA_BACK, 128), F32)
            kp[SWA_BACK:, :] = k_ref[...]
            vp[SWA_BACK:, :] = v_ref[...]
            dkp[...] = jnp.zeros_like(dkp)
            dvp[...] = jnp.zeros_like(dvp)
            ds_ref[...] = jnp.zeros_like(ds_ref)

        start = pl.multiple_of(i * SWA_BQ, SWA_BQ)
        _, vjp = jax.vjp(functools.partial(_swa_block, blk=i, bf16_operands=True), q_ref[...], kp[pl.ds(start, win), :],
                         vp[pl.ds(start, win), :], s_ref[...])
        dq, dkw, dvw, dsk = vjp(do_ref[...])
        dq_ref[...] = dq.astype(dq_ref.dtype)
        dkp[pl.ds(start, win), :] += dkw
        dvp[pl.ds(start, win), :] += dvw
        ds_ref[...] += dsk

        @pl.when(i == nb - 1)
        def _():
            dk_ref[...] = dkp[SWA_BACK:, :].astype(dk_ref.dtype)
            dv_ref[...] = dvp[SWA_BACK:, :].astype(dv_ref.dtype)

    return pl.pallas_call(
        body, grid=(nb,), name=name,
        in_specs=[pl.BlockSpec((SWA_BQ, 256), lambda i: (i, qb)), pl.BlockSpec((t, 128), lambda i: (0, kb)),
                  pl.BlockSpec((t, 128), lambda i: (0, kb + 1)), pl.BlockSpec((1, LANES), lambda i: (0, 0)),
                  pl.BlockSpec((SWA_BQ, 256), lambda i: (i, db))],
        out_specs=[pl.BlockSpec((SWA_BQ, 256), lambda i: (i, 0)), pl.BlockSpec((t, 128), lambda i: (0, 0)),
                   pl.BlockSpec((t, 128), lambda i: (0, 0)), pl.BlockSpec((1, LANES), lambda i: (0, 0))],
        out_shape=[S((t, 256), BF16), S((t, 128), BF16), S((t, 128), BF16), S((1, LANES), F32)],
        scratch_shapes=[pltpu.VMEM((t + SWA_BACK, 128), F32) for _ in range(4)],
        compiler_params=_params(("arbitrary",)),
    )(proj, proj, proj, sinks, dcat)


def _ssd_chunk(z, xbc, dt_raw, state, dtb, alog, dsk, nw, bf16_operands=False):
    n = z.shape[0]
    op = (lambda a: a.astype(BF16)) if bf16_operands else (lambda a: a)
    r, c = _iota2(n, n)
    tril = r >= c
    eye = (r == c).astype(F32)
    dt = jax.nn.softplus(dt_raw + dtb)
    acs = _dot(tril.astype(F32), dt * (-jnp.exp(alog)), HI)
    xs, bm, cm = xbc[:, :512], xbc[:, 512:768], xbc[:, 768:1024]
    heads = range(8)
    bg = [bm[:, 128 * g:128 * g + 128] for g in range(2)]
    cg = [cm[:, 128 * g:128 * g + 128] for g in range(2)]
    cb = [_dot_nt(op(cg[g]), op(bg[g])) for g in range(2)]
    dth = [_pick_col(dt, h) for h in heads]
    acol = [_pick_col(acs, h) for h in heads]
    arow = [_col_to_row(a, eye) for a in acol]
    lmat = [jnp.where(tril, jnp.exp(jnp.where(tril, a - b, 0.0)), 0.0) for a, b in zip(acol, arow)]
    xh = [xs[:, 64 * h:64 * h + 64] for h in heads]
    xc = [x * t for x, t in zip(xh, dth)]
    st = [state[64 * h:64 * h + 64, :] for h in heads]
    alast = [_pick_row(a, n - 1) for a in acol]
    y_in = [_dot(op(cb[h // 4] * lmat[h]), op(xc[h])) for h in heads]
    y_st = [_dot_nt(op(cg[h // 4]), op(st[h])) * jnp.exp(acol[h]) for h in heads]
    ys = [y_in[h] + y_st[h] + xh[h] * _pick_col(dsk, h) for h in heads]
    new_states = [st[h] * jnp.exp(alast[h]) + _dot_tn(op(xc[h] * jnp.exp(alast[h] - acol[h])), op(bg[h // 4]))
                  for h in heads]
    gg = jnp.concatenate(ys, axis=1) * jax.nn.silu(z)
    outs = []
    for gi in range(2):
        gv = gg[:, 256 * gi:256 * gi + 256]
        outs.append(gv * lax.rsqrt(jnp.mean(gv * gv, axis=-1, keepdims=True) + EPS))
    return jnp.concatenate(outs, axis=1) * nw, jnp.concatenate(new_states, axis=0)


def _ssd_fwd(proj, xbc, dtb, alog, dsk, nw, name, exchange=None):
    t = proj.shape[0]
    rows = min(SSD_CHUNK, t)
    nc = t // rows

    def body(z_ref, x_ref, dt_ref, dtb_ref, al_ref, d_ref, nw_ref, o_ref, st_ref, state):
        @pl.when(pl.program_id(0) == 0)
        def _():
            state[...] = jnp.zeros_like(state)

        st_ref[0] = state[...]
        o, ns = _ssd_chunk(z_ref[...], x_ref[...], dt_ref[...], state[...], dtb_ref[...], al_ref[...], d_ref[...],
                           nw_ref[...])
        o_ref[...] = o.astype(o_ref.dtype)
        state[...] = ns

    body, x_in, x_out, x_shape, x_sems = _hosted(exchange, 7, 2, nc, body)
    vec = pl.BlockSpec((1, LANES), lambda i: (0, 0))
    return pl.pallas_call(
        body, grid=(nc,), name=name,
        in_specs=[pl.BlockSpec((rows, 512), lambda i: (i, PC_SZ // 512)), pl.BlockSpec((rows, 1024), lambda i: (i, 0)),
                  pl.BlockSpec((rows, 128), lambda i: (i, PC_DT // 128)), vec, vec, vec,
                  pl.BlockSpec((1, 512), lambda i: (0, 0))] + x_in,
        out_specs=[pl.BlockSpec((rows, 512), lambda i: (i, 0)), pl.BlockSpec((1, 512, 128), lambda i: (i, 0, 0))] + x_out,
        out_shape=[S((t, 512), BF16), S((nc, 512, 128), F32)] + x_shape,
        scratch_shapes=[pltpu.VMEM((512, 128), F32)] + x_sems,
        compiler_params=_params(("arbitrary",)),
    )(proj, xbc, proj, dtb, alog, dsk, nw, *([] if exchange is None else exchange.arrays))


def _ssd_bwd(proj, xbc, states, dtb, alog, dsk, nw, dcat, dcol0, name, exchange=None):
    t = proj.shape[0]
    rows = min(SSD_CHUNK, t)
    nc = t // rows
    db = dcol0 // 512

    def body(z_ref, x_ref, dt_ref, st_ref, dtb_ref, al_ref, d_ref, nw_ref, do_ref,
             dz_ref, dx_ref, ddt_ref, gdtb_ref, gal_ref, gd_ref, gnw_ref, dstate):
        @pl.when(pl.program_id(0) == 0)
        def _():
            dstate[...] = jnp.zeros_like(dstate)
            gdtb_ref[...] = jnp.zeros_like(gdtb_ref)
            gal_ref[...] = jnp.zeros_like(gal_ref)
            gd_ref[...] = jnp.zeros_like(gd_ref)
            gnw_ref[...] = jnp.zeros_like(gnw_ref)

        _, vjp = jax.vjp(functools.partial(_ssd_chunk, bf16_operands=True), z_ref[...], x_ref[...], dt_ref[...],
                         st_ref[0], dtb_ref[...], al_ref[...], d_ref[...], nw_ref[...])
        dz, dx, ddt, dst, gdtb, gal, gd, gnw = vjp((do_ref[...], dstate[...]))
        dz_ref[...] = dz.astype(dz_ref.dtype)
        dx_ref[...] = dx
        ddt_ref[...] = ddt.astype(ddt_ref.dtype)
        dstate[...] = dst
        gdtb_ref[...] += gdtb
        gal_ref[...] += gal
        gd_ref[...] += gd
        gnw_ref[...] += gnw

    body, x_in, x_out, x_shape, x_sems = _hosted(exchange, 9, 7, nc, body)
    rev = lambda i: nc - 1 - i
    vec = pl.BlockSpec((1, LANES), lambda i: (0, 0))
    vec512 = pl.BlockSpec((1, 512), lambda i: (0, 0))
    return pl.pallas_call(
        body, grid=(nc,), name=name,
        in_specs=[pl.BlockSpec((rows, 512), lambda i: (rev(i), PC_SZ // 512)),
                  pl.BlockSpec((rows, 1024), lambda i: (rev(i), 0)),
                  pl.BlockSpec((rows, 128), lambda i: (rev(i), PC_DT // 128)),
                  pl.BlockSpec((1, 512, 128), lambda i: (rev(i), 0, 0)), vec, vec, vec, vec512,
                  pl.BlockSpec((rows, 512), lambda i: (rev(i), db))] + x_in,
        out_specs=[pl.BlockSpec((rows, 512), lambda i: (rev(i), 0)), pl.BlockSpec((rows, 1024), lambda i: (rev(i), 0)),
                   pl.BlockSpec((rows, 128), lambda i: (rev(i), 0)), vec, vec, vec, vec512] + x_out,
        out_shape=[S((t, 512), BF16), S((t, 1024), F32), S((t, 128), BF16), S((1, LANES), F32), S((1, LANES), F32),
                   S((1, LANES), F32), S((1, 512), F32)] + x_shape,
        scratch_shapes=[pltpu.VMEM((512, 128), F32)] + x_sems,
        compiler_params=_params(("arbitrary",)),
    )(proj, xbc, proj, states, dtb, alog, dsk, nw, dcat, *([] if exchange is None else exchange.arrays))


SOLVE_PREC = lax.Precision.HIGH


def _unit_lower_inverses(nas, known=None):
    def compute(ns):
        if known is not None:
            return tuple(known)
        n = ns[0].shape[0]
        r, c = _iota2(n, n)
        eye = (r == c).astype(F32)
        tm, pw = [eye + a for a in ns], list(ns)
        for _ in range(n.bit_length() - 2):
            pw = [_dot(p, p, SOLVE_PREC) for p in pw]
            tm = [t + _dot(t, p, SOLVE_PREC) for t, p in zip(tm, pw)]
        return tuple(tm)

    inv = jax.custom_vjp(compute)

    def fwd(ns):
        ts = compute(ns)
        return ts, ts

    def bwd(ts, gs):
        part = [_dot_nt(g, t, SOLVE_PREC) for g, t in zip(gs, ts)]
        return (tuple(_dot_tn(t, p, SOLVE_PREC) for t, p in zip(ts, part)),)

    inv.defvjp(fwd, bwd)
    return inv(nas)


def _gdn_chunk(qkv, z, ba, state, dtb, alog, nw, known_inverses=None, bf16_operands=False):
    n = qkv.shape[0]
    op = (lambda a: a.astype(BF16)) if bf16_operands else (lambda a: a)
    r, c = _iota2(n, n)
    tril = r >= c
    stril = r > c
    eye = (r == c).astype(F32)
    beta_all = jax.nn.sigmoid(ba)
    gcs = _dot(tril.astype(F32), -jnp.exp(alog) * jax.nn.softplus(ba + dtb), HI)
    heads = range(4)
    qh = [qkv[:, 64 * h:64 * h + 64] for h in heads]
    kh = [qkv[:, 256 + 64 * h:256 + 64 * h + 64] for h in heads]
    vh = [qkv[:, 512 + 64 * h:512 + 64 * h + 64] for h in heads]
    qn = [q * lax.rsqrt(jnp.sum(q * q, axis=-1, keepdims=True) + EPS) * 0.125 for q in qh]
    kn = [k * lax.rsqrt(jnp.sum(k * k, axis=-1, keepdims=True) + EPS) for k in kh]
    beta = [_pick_col(beta_all, h) for h in heads]
    gcol = [_pick_col(gcs, 4 + h) for h in heads]
    grow = [_col_to_row(g, eye) for g in gcol]
    decay = [jnp.where(tril, jnp.exp(jnp.where(tril, gc - gr, 0.0)), 0.0) for gc, gr in zip(gcol, grow)]
    kbeta = [k * b for k, b in zip(kn, beta)]
    kk = [_dot_nt(op(kb), op(k)) for kb, k in zip(kbeta, kn)]
    qk = [_dot_nt(op(q), op(k)) * dc for q, k, dc in zip(qn, kn, decay)]
    known = None if known_inverses is None else [known_inverses[n * h:n * (h + 1), :] for h in heads]
    tms = _unit_lower_inverses(tuple(-jnp.where(stril, x * dc, 0.0) for x, dc in zip(kk, decay)), known)
    rhs = [jnp.concatenate([v * b, kb * jnp.exp(g)], axis=1) for v, b, kb, g in zip(vh, beta, kbeta, gcol)]
    sol = [_dot(t, x, SOLVE_PREC) for t, x in zip(tms, rhs)]
    st = [state[64 * h:64 * h + 64, :] for h in heads]
    v_new = [s_[:, :64] - _dot(op(s_[:, 64:]), op(s)) for s_, s in zip(sol, st)]
    o = [_dot(op(q * jnp.exp(g)), op(s)) + _dot(op(x), op(vn)) for q, g, s, x, vn in zip(qn, gcol, st, qk, v_new)]
    glast = [_pick_row(g, n - 1) for g in gcol]
    new_states = [s * jnp.exp(gl) + _dot_tn(op(k * jnp.exp(gl - g)), op(vn))
                  for s, gl, k, g, vn in zip(st, glast, kn, gcol, v_new)]
    o = [x * lax.rsqrt(jnp.mean(x * x, axis=-1, keepdims=True) + EPS) * nw for x in o]
    outs = [x * jax.nn.silu(z[:, 64 * h:64 * h + 64]) for h, x in zip(heads, o)]
    return jnp.concatenate(outs, axis=1), jnp.concatenate(new_states, axis=0), jnp.concatenate(tms, axis=0)


class _Exchange:
    def __init__(self, arrays, out_shape, n_sems, start, finish):
        self.arrays, self.out_shape, self.n_sems, self.start, self.finish = arrays, out_shape, n_sems, start, finish


def _hosted(exchange, n_in, n_out, grid, body):
    if exchange is None:
        return body, [], [], [], []
    k_in, k_out = len(exchange.arrays), len(exchange.out_shape)
    grid = (grid,) if isinstance(grid, int) else tuple(grid)

    def hosted_body(*refs):
        ins, refs = refs[:n_in + k_in], refs[n_in + k_in:]
        outs, scratch = refs[:n_out + k_out], refs[n_out + k_out:]
        sems = scratch[-2:]
        first, last = True, True
        for axis, steps in enumerate(grid):
            first = first & (pl.program_id(axis) == 0)
            last = last & (pl.program_id(axis) == steps - 1)

        @pl.when(first)
        def _():
            exchange.start(ins[n_in:], outs[n_out:], sems)

        body(*ins[:n_in], *outs[:n_out], *scratch[:-2])

        @pl.when(last)
        def _():
            exchange.finish(ins[n_in:], outs[n_out:], sems)

    return hosted_body, [ANY] * k_in, [ANY] * k_out, list(exchange.out_shape), _sem_pairs(exchange.n_sems)


def _gdn_fwd(proj, qkv, dtb, alog, nw, name, exchange=None):
    t = proj.shape[0]
    rows = min(GDN_CHUNK, t)
    nc = t // rows

    def body(q_ref, z_ref, ba_ref, dtb_ref, al_ref, nw_ref, o_ref, st_ref, inv_ref, state):
        @pl.when(pl.program_id(0) == 0)
        def _():
            state[...] = jnp.zeros_like(state)

        st_ref[0] = state[...]
        o, ns, tms = _gdn_chunk(q_ref[...], z_ref[...], ba_ref[...], state[...], dtb_ref[...], al_ref[...], nw_ref[...],
                                bf16_operands=True)
        o_ref[...] = o.astype(o_ref.dtype)
        inv_ref[0] = tms
        state[...] = ns

    body, x_in, x_out, x_shape, x_sems = _hosted(exchange, 6, 3, nc, body)
    vec = pl.BlockSpec((1, LANES), lambda i: (0, 0))
    per_chunk = pl.BlockSpec((1, 256, 64), lambda i: (i, 0, 0))
    return pl.pallas_call(
        body, grid=(nc,), name=name,
        in_specs=[pl.BlockSpec((rows, 768), lambda i: (i, 0)), pl.BlockSpec((rows, 256), lambda i: (i, PC_GZ // 256)),
                  pl.BlockSpec((rows, 128), lambda i: (i, PC_BA // 128)), vec, vec, pl.BlockSpec((1, 64), lambda i: (0, 0))]
        + x_in,
        out_specs=[pl.BlockSpec((rows, 256), lambda i: (i, 0)), per_chunk,
                   pl.BlockSpec((1, 4 * rows, rows), lambda i: (i, 0, 0))] + x_out,
        out_shape=[S((t, 256), BF16), S((nc, 256, 64), F32), S((nc, 4 * rows, rows), F32)] + x_shape,
        scratch_shapes=[pltpu.VMEM((256, 64), F32)] + x_sems,
        compiler_params=_params(("arbitrary",)),
    )(qkv, proj, proj, dtb, alog, nw, *([] if exchange is None else exchange.arrays))


def _gdn_bwd(proj, qkv, states, inverses, dtb, alog, nw, dcat, dcol0, name, exchange=None):
    t = proj.shape[0]
    rows = min(GDN_CHUNK, t)
    nc = t // rows
    db = dcol0 // 256

    def body(q_ref, z_ref, ba_ref, st_ref, inv_ref, dtb_ref, al_ref, nw_ref, do_ref,
             dq_ref, dz_ref, dba_ref, gdtb_ref, gal_ref, gnw_ref, dstate):
        @pl.when(pl.program_id(0) == 0)
        def _():
            dstate[...] = jnp.zeros_like(dstate)
            gdtb_ref[...] = jnp.zeros_like(gdtb_ref)
            gal_ref[...] = jnp.zeros_like(gal_ref)
            gnw_ref[...] = jnp.zeros_like(gnw_ref)

        def chunk(*operands):
            return _gdn_chunk(*operands, known_inverses=inv_ref[0])[:2]

        _, vjp = jax.vjp(chunk, q_ref[...], z_ref[...], ba_ref[...], st_ref[0], dtb_ref[...], al_ref[...], nw_ref[...])
        dq, dz, dba, dst, gdtb, gal, gnw = vjp((do_ref[...], dstate[...]))
        dq_ref[...] = dq
        dz_ref[...] = dz.astype(dz_ref.dtype)
        dba_ref[...] = dba.astype(dba_ref.dtype)
        dstate[...] = dst
        gdtb_ref[...] += gdtb
        gal_ref[...] += gal
        gnw_ref[...] += gnw

    body, x_in, x_out, x_shape, x_sems = _hosted(exchange, 9, 6, nc, body)
    rev = lambda i: nc - 1 - i
    vec = pl.BlockSpec((1, LANES), lambda i: (0, 0))
    vec64 = pl.BlockSpec((1, 64), lambda i: (0, 0))
    per_chunk = pl.BlockSpec((1, 256, 64), lambda i: (rev(i), 0, 0))
    return pl.pallas_call(
        body, grid=(nc,), name=name,
        in_specs=[pl.BlockSpec((rows, 768), lambda i: (rev(i), 0)),
                  pl.BlockSpec((rows, 256), lambda i: (rev(i), PC_GZ // 256)),
                  pl.BlockSpec((rows, 128), lambda i: (rev(i), PC_BA // 128)),
                  per_chunk, pl.BlockSpec((1, 4 * rows, rows), lambda i: (rev(i), 0, 0)), vec, vec, vec64,
                  pl.BlockSpec((rows, 256), lambda i: (rev(i), db))] + x_in,
        out_specs=[pl.BlockSpec((rows, 768), lambda i: (rev(i), 0)), pl.BlockSpec((rows, 256), lambda i: (rev(i), 0)),
                   pl.BlockSpec((rows, 128), lambda i: (rev(i), 0)), vec, vec, vec64] + x_out,
        out_shape=[S((t, 768), F32), S((t, 256), BF16), S((t, 128), BF16), S((1, LANES), F32), S((1, LANES), F32),
                   S((1, 64), F32)] + x_shape,
        scratch_shapes=[pltpu.VMEM((256, 64), F32)] + x_sems,
        compiler_params=_params(("arbitrary",)),
    )(qkv, proj, proj, states, inverses, dtb, alog, nw, dcat, *([] if exchange is None else exchange.arrays))


def _pad_cols(w):
    z = jnp.zeros((w.shape[0], 120), w.dtype)
    return jnp.concatenate([w[:, 2056:2824], w[:, 2824:3080], w[:, 1024:2048], w[:, 0:512], w[:, 512:1024],
                            w[:, 2048:2056], z, w[:, 3080:3088], z], axis=1)


def _unpad_cols(g):
    return jnp.concatenate([g[:, PC_ATT:PC_ATT + 512], g[:, PC_SZ:PC_SZ + 512], g[:, PC_XBC:PC_XBC + 1024],
                            g[:, PC_DT:PC_DT + 8], g[:, PC_GQKV:PC_GQKV + 768], g[:, PC_GZ:PC_GZ + 256],
                            g[:, PC_BA:PC_BA + 8]], axis=1)


def _vec128(v, at=0):
    return jnp.zeros((1, LANES), F32).at[0, at:at + v.shape[0]].set(v)


def _in_weight(gathered):
    return _pad_cols(jnp.concatenate([gathered[k].reshape(D_MODEL, -1) for k in range(N_CHIPS)], axis=1))


def _matmul_weights(w_in, gathered):
    rows = lambda name: gathered[name].reshape(-1, gathered[name].shape[-1])
    w_out = rows("w_out")
    return dict(
        w_in=w_in,
        w_out=jnp.concatenate([w_out[256:768], w_out[0:256], w_out[768:1024]], axis=0),
        w_gu=_interleave_gu(rows("ffn_w_gate"), rows("ffn_w_up")),
        w_down=rows("ffn_w_down"))


def _small_operands(w, l):
    return dict(
        pre_mix=w["pre_mix_norm"][l][None], post_mix=w["post_mix_norm"][l][None],
        pre_ffn=w["pre_ffn_norm"][l][None], post_ffn=w["post_ffn_norm"][l][None],
        sinks=_vec128(w["attn_sinks"][l]),
        s_cw=w["ssd_conv_w"][l], s_cb=w["ssd_conv_b"][l][None],
        s_dtb=_vec128(w["ssd_dt_bias"][l]), s_alog=_vec128(w["ssd_A_log"][l]), s_d=_vec128(w["ssd_D"][l]),
        s_nw=w["ssd_norm_w"][l][None],
        g_cw=w["gdn_conv_w"][l], g_cb=jnp.zeros((1, 768), F32),
        g_dtb=_vec128(w["gdn_dt_bias"][l], 4), g_alog=_vec128(w["gdn_A_log"][l], 4), g_nw=w["gdn_norm_w"][l][None],
    )


RAW_GRADS = ("w_in_pad", "w_out_cat", "w_gu", "ffn_w_down")
DW_ROWS = 4096


def _local_step(x, target, lw, first_gather, gathers, matmul_weights, reducer):
    saved, landed = [], {}
    xin = x
    h, first_landed = _prenorm(x, lw[0]["pre_mix"], "prenorm0", first_gather)
    w_in0 = _in_weight(first_landed)
    for l in range(DEPTH):
        p = lw[l]
        carry = (lambda kind: gathers[kind]) if l == 0 else (lambda kind: None)
        proj = _mm_nn(h, w_in0 if l == 0 else p["w_in"], 512, PC_TOT, F32, f"inproj{l}")
        xbc = _conv_fwd(proj, PC_XBC, 1024, p["s_cw"], p["s_cb"], f"ssd_conv{l}")
        gqkv = _conv_fwd(proj, PC_GQKV, 768, p["g_cw"], p["g_cb"], f"gdn_conv{l}")
        att = _swa_fwd(proj, p["sinks"], f"swa{l}")
        ssd, s_states, *landed_s = _ssd_fwd(proj, xbc, p["s_dtb"], p["s_alog"], p["s_d"], p["s_nw"], f"ssd{l}",
                                            carry("ssd"))
        gdn, g_states, g_inv, *landed_g = _gdn_fwd(proj, gqkv, p["g_dtb"], p["g_alog"], p["g_nw"], f"gdn{l}",
                                                   carry("gdn"))
        if l == 0:
            landed.update(ssd=landed_s, gdn=landed_g)
            p.update(matmul_weights(0, landed), w_in=w_in0)
        cat = jnp.concatenate([ssd, att, gdn], axis=1)
        mix = _mm_nn(cat, p["w_out"], 512, 1024, F32, f"outproj{l}")
        x1, h2 = _resid_norm(xin, mix, p["post_mix"], p["pre_ffn"], f"postmix{l}")
        gu, act, *landed_u = _ffn_up(h2, p["w_gu"], f"ffn_gu{l}", exchange=carry("ffn_gu"))
        if l == 0:
            f, *landed_d = _mm_nn(act, p["w_down"], 512, 1024, F32, f"ffn_down{l}", carry("ffn_down"))
            landed.update(ffn_gu=landed_u, ffn_down=landed_d)
            lw[1].update(matmul_weights(1, landed))
        else:
            f = _mm_nn(act, p["w_down"], 512, 1024, F32, f"ffn_down{l}")
        saved.append(dict(xin=xin, h=h, proj=proj, xbc=xbc, gqkv=gqkv, s_states=s_states, g_states=g_states, g_inv=g_inv,
                          cat=cat, mix=mix, x1=x1, h2=h2, gu=gu, act=act, f=f))
        if l + 1 < DEPTH:
            xin, h = _resid_norm(x1, f, p["post_ffn"], lw[l + 1]["pre_mix"], f"postffn{l}")

    g = {k: [None] * DEPTH for k in SMALL + CONV + RAW_GRADS}
    last = saved[-1]
    d_x2, d_f, loss_part, g["post_ffn_norm"][DEPTH - 1] = _resid_loss(
        last["x1"], last["f"], lw[-1]["post_ffn"], target, "loss")
    early, late = [], []
    for l in reversed(range(DEPTH)):
        p, s = lw[l], saved[l]
        d_gu = _ffn_down_bwd(d_f, p["w_down"], s["gu"], f"d_act{l}")
        g["ffn_w_down"][l] = _mm_tn(s["act"], d_f, FF_HALF, 512, DW_ROWS, f"dw_down{l}")
        d_h2 = _mm_nn(d_gu, p["w_gu"], 512, 1024, F32, f"d_h2{l}")
        g["w_gu"][l] = _mm_tn(d_gu, s["h2"], FF_HALF, 512, DW_ROWS, f"dw_gu{l}")
        d_x1, d_mix, g["pre_ffn_norm"][l], g["post_mix_norm"][l] = _resid_norm_bwd(
            s["x1"], s["mix"], d_x2, d_h2, p["post_mix"], p["pre_ffn"], f"d_postmix{l}")
        d_cat = _mm_nt(d_mix, p["w_out"], 512, 1024, F32, f"d_cat{l}")
        g["w_out_cat"][l] = _mm_tn(s["cat"], d_mix, 512, 1024, DW_ROWS, f"dw_out{l}")
        d_q, d_k, d_v, g_sinks = _swa_bwd(s["proj"], p["sinks"], d_cat, 512, f"d_swa{l}")
        d_sz, d_xbc, d_dt, g_dtb, g_alog, g_d, g["ssd_norm_w"][l], *siblings = _ssd_bwd(
            s["proj"], s["xbc"], s["s_states"], p["s_dtb"], p["s_alog"], p["s_d"], p["s_nw"], d_cat, 0, f"d_ssd{l}",
            reducer.exchange(g) if l == 0 else None)
        d_gq, d_gz, d_ba, gg_dtb, gg_alog, g["gdn_norm_w"][l], *landed_b = _gdn_bwd(
            s["proj"], s["gqkv"], s["g_states"], s["g_inv"], p["g_dtb"], p["g_alog"], p["g_nw"], d_cat, 768, f"d_gdn{l}",
            reducer.scatter(siblings) if l == 0 else None)
        if l == 0:
            early = landed_b
        d_xbc_raw, g["ssd_conv_w"][l], g["ssd_conv_b"][l] = _conv_bwd(
            s["proj"], PC_XBC, 1024, p["s_cw"], p["s_cb"], d_xbc, f"d_ssd_conv{l}")
        d_gq_raw, g["gdn_conv_w"][l], _ = _conv_bwd(s["proj"], PC_GQKV, 768, p["g_cw"], p["g_cb"], d_gq, f"d_gdn_conv{l}")
        d_proj = jnp.concatenate([d_gq_raw, d_gz, d_xbc_raw, d_q, d_k, d_v, d_sz, d_dt, d_ba], axis=1)
        g["w_in_pad"][l] = _mm_tn(s["h"], d_proj, 512, PC_TOT // 2, DW_ROWS, f"dw_in{l}")
        if l == 0:
            d_h, *late = _mm_nt(d_proj, p["w_in"], 512, 1024, F32, f"d_h{l}", reducer.late(g))
        else:
            d_h = _mm_nt(d_proj, p["w_in"], 512, 1024, F32, f"d_h{l}")
        g["attn_sinks"][l] = g_sinks[0, :4]
        g["ssd_dt_bias"][l], g["ssd_A_log"][l], g["ssd_D"][l] = g_dtb[0, :8], g_alog[0, :8], g_d[0, :8]
        g["gdn_dt_bias"][l], g["gdn_A_log"][l] = gg_dtb[0, 4:8], gg_alog[0, 4:8]
        if l > 0:
            sp = saved[l - 1]
            d_x2, d_f, g["pre_mix_norm"][l], g["post_ffn_norm"][l - 1] = _resid_norm_bwd(
                s["xin"], sp["f"], d_x1, d_h, lw[l - 1]["post_ffn"], p["pre_mix"], f"d_postffn{l - 1}")
        else:
            grad_x, g["pre_mix_norm"][0] = _prenorm_bwd(s["xin"], d_x1, d_h, p["pre_mix"], "d_prenorm0")

    small = {k: jnp.stack([a.reshape(-1) for a in g[k]], axis=0) for k in SMALL + CONV}
    return loss_part, grad_x, small, {k: g[k] for k in RAW_GRADS}, early, late


BIG = ("w_in", "w_out", "ffn_w_gate", "ffn_w_up", "ffn_w_down")
CONV = ("ssd_conv_w", "gdn_conv_w")
TRANSPOSED = ("ffn_w_gate", "ffn_w_up")
SMALL = ("pre_mix_norm", "post_mix_norm", "pre_ffn_norm", "post_ffn_norm", "attn_sinks", "ssd_conv_b", "ssd_dt_bias",
         "ssd_A_log", "ssd_D", "ssd_norm_w", "gdn_dt_bias", "gdn_A_log", "gdn_norm_w")


def _row_tile(rows, cap):
    best = rows
    for t in range(8, min(cap, rows) + 1, 8):
        if rows % t == 0:
            best = t
    return best


SMALL_UNIT = 8 * LANES


def _pack_small(vals):
    rows = []
    for a in vals:
        f = a.reshape(-1)
        pad = -f.shape[0] % SMALL_UNIT
        rows.append(jnp.concatenate([f, jnp.zeros((pad,), F32)]).reshape(-1, LANES))
    return jnp.concatenate(rows, axis=0)


def _unpack_small(mat, shapes):
    out, r = [], 0
    for shp in shapes:
        n = math.prod(shp)
        nr = -(-n // SMALL_UNIT) * 8
        out.append(mat[r:r + nr].reshape(-1)[:n].reshape(shp))
        r += nr
    return out


def _place():
    x, y, c = lax.axis_index("x"), lax.axis_index("y"), lax.axis_index("c")
    chips = [(1 - x, y), (x, 1 - y), (1 - x, 1 - y)]
    return x, y, c, chips


ANY = pl.BlockSpec(memory_space=pl.ANY)


def _remote(src, dst, sems, k, to):
    send_sems, recv_sems = sems
    return pltpu.make_async_remote_copy(src_ref=src, dst_ref=dst, send_sem=send_sems.at[k], recv_sem=recv_sems.at[k],
                                        device_id=to, device_id_type=MESH)


def _sem_pairs(n):
    return [pltpu.SemaphoreType.DMA((n,)), pltpu.SemaphoreType.DMA((n,))]


def _run_exchange(exchange, name):
    k = len(exchange.arrays)

    def body(*refs):
        ins, outs, sems = refs[:k], refs[k:-2], refs[-2:]
        exchange.start(ins, outs, sems)
        exchange.finish(ins, outs, sems)

    return pl.pallas_call(
        body, name=name, in_specs=[ANY] * k, out_specs=[ANY] * len(exchange.out_shape),
        out_shape=list(exchange.out_shape), scratch_shapes=_sem_pairs(exchange.n_sems),
    )(*exchange.arrays)


def _gather_exchange(shards):
    n = len(shards)

    def sends(s_refs, g_refs, sems):
        x, y, c, chips = _place()
        me = 2 * x + y
        over_ici = [_remote(s_refs[i].at[c], g_refs[i].at[me, c], sems, 7 * i + j, (px, py, c))
                    for i in range(n) for j, (px, py) in enumerate(chips)]
        return over_ici + [_remote(s_refs[i], g_refs[i].at[me], sems, 7 * i + 6, (x, y, 1 - c)) for i in range(n)]

    def start(s_refs, g_refs, sems):
        for cp in sends(s_refs, g_refs, sems):
            cp.start()

    def finish(s_refs, g_refs, sems):
        x, y, c, chips = _place()
        sib = (x, y, 1 - c)
        passed = []
        for j, (px, py) in enumerate(chips):
            for i in range(n):
                landed = g_refs[i].at[2 * px + py, c]
                _remote(landed, landed, sems, 7 * i + j, (px, py, c)).wait_recv()
                fw = _remote(landed, landed, sems, 7 * i + 3 + j, sib)
                fw.start()
                passed.append(fw)
        for j, (px, py) in enumerate(chips):
            for i in range(n):
                landed = g_refs[i].at[2 * px + py, 1 - c]
                _remote(landed, landed, sems, 7 * i + 3 + j, sib).wait_recv()
        for i in range(n):
            mine = g_refs[i].at[2 * x + y]
            _remote(mine, mine, sems, 7 * i + 6, sib).wait_recv()
        for cp in sends(s_refs, g_refs, sems) + passed:
            cp.wait_send()

    return _Exchange(shards, [S((N_CHIPS,) + a.shape, a.dtype) for a in shards], 7 * n, start, finish)


def _halves_exchange(pieces):
    n = len(pieces)
    ds = [a for a, _ in pieces]

    def copies(d_refs, t_refs, sems):
        x, y, c, _ = _place()
        return [_remote(d_refs[i].at[:, pieces[i][1], :, 1 - c], t_refs[i], sems, i, (x, y, 1 - c)) for i in range(n)]

    def start(d_refs, t_refs, sems):
        for cp in copies(d_refs, t_refs, sems):
            cp.start()

    def finish(d_refs, t_refs, sems):
        for cp in copies(d_refs, t_refs, sems):
            cp.wait()

    return _Exchange(ds, [S((2, 2) + a.shape[4:], a.dtype) for a in ds], n, start, finish)


def _scatter_exchange(ps):
    n = len(ps)

    def copies(p_refs, u_refs, sems):
        x, y, c, chips = _place()
        return [_remote(p_refs[i].at[2 * px + py], u_refs[i].at[j], sems, 3 * i + j, (px, py, c))
                for j, (px, py) in enumerate(chips) for i in range(n)]

    def start(p_refs, u_refs, sems):
        for cp in copies(p_refs, u_refs, sems):
            cp.start()

    def finish(p_refs, u_refs, sems):
        for cp in copies(p_refs, u_refs, sems):
            cp.wait()

    return _Exchange(ps, [S((3,) + a.shape[1:], a.dtype) for a in ps], 3 * n, start, finish)


def _join_halves(qs, name):
    n = len(qs)

    def body(*refs):
        o_refs, sems = refs[n:2 * n], refs[2 * n:]
        x, y, c, _ = _place()
        cps = [_remote(o_refs[i].at[:, c], o_refs[i].at[:, c], sems, i, (x, y, 1 - c)) for i in range(n)]
        for cp in cps:
            cp.start()
        for i in range(n):
            other = o_refs[i].at[:, 1 - c]
            _remote(other, other, sems, i, (x, y, 1 - c)).wait_recv()
        for cp in cps:
            cp.wait_send()

    return pl.pallas_call(
        body, name=name, in_specs=[ANY] * n, out_specs=[ANY] * n,
        out_shape=[S(a.shape, a.dtype) for a in qs], input_output_aliases={i: i for i in range(n)},
        scratch_shapes=_sem_pairs(n),
    )(*qs)


def _gather_small(v, name):
    def body(v_ref, o_ref, send_sems, recv_sems, local_sem):
        x, y, c, _ = _place()
        me = 4 * x + 2 * y + c
        mine = pltpu.make_async_copy(v_ref, o_ref.at[me], local_sem)
        mine.start()
        cps = []
        for k in range(1, N_DEV):
            fx, fy, fc = (k >> 2) & 1, (k >> 1) & 1, k & 1
            peer = (x ^ fx, y ^ fy, c ^ fc)
            cps.append(pltpu.make_async_remote_copy(
                src_ref=v_ref, dst_ref=o_ref.at[me], send_sem=send_sems.at[k - 1], recv_sem=recv_sems.at[k - 1],
                device_id=peer, device_id_type=MESH))
        for cp in cps:
            cp.start()
        for k in range(1, N_DEV):
            fx, fy, fc = (k >> 2) & 1, (k >> 1) & 1, k & 1
            dst = o_ref.at[4 * (x ^ fx) + 2 * (y ^ fy) + (c ^ fc)]
            pltpu.make_async_remote_copy(src_ref=dst, dst_ref=dst, send_sem=send_sems.at[k - 1],
                                         recv_sem=recv_sems.at[k - 1], device_id=(x, y, c),
                                         device_id_type=MESH).wait_recv()
        for cp in cps:
            cp.wait_send()
        mine.wait()

    return pl.pallas_call(
        body, name=name, in_specs=[ANY], out_specs=ANY, out_shape=S((N_DEV,) + v.shape, F32),
        scratch_shapes=[pltpu.SemaphoreType.DMA((N_DEV - 1,)), pltpu.SemaphoreType.DMA((N_DEV - 1,)),
                        pltpu.SemaphoreType.DMA],
    )(v)


def _sum_leading(a, name):
    n, rows, cols = a.shape
    tm = _row_tile(rows, 640)

    def body(a_ref, o_ref):
        acc = a_ref[0]
        for k in range(1, n):
            acc = acc + a_ref[k]
        o_ref[...] = acc

    return pl.pallas_call(
        body, grid=(rows // tm,), name=name, in_specs=[pl.BlockSpec((n, tm, cols), lambda i: (0, i, 0))],
        out_specs=pl.BlockSpec((tm, cols), lambda i: (i, 0)), out_shape=S((rows, cols), F32),
        compiler_params=_params(("arbitrary",)),
    )(a)


def _add_sibling(place, piece, b, name):
    a, pick = piece
    n, hr, cols = b.shape
    tm = _row_tile(hr, 512)

    def body(place_ref, a_ref, b_ref, o_ref, o16_ref):
        tot = a_ref[0, 0, 0] + b_ref[...]
        o16_ref[...] = tot.astype(BF16)

        @pl.when(pl.program_id(1) == place_ref[1])
        def _():
            o_ref[...] = tot[0]

    spec = pl.BlockSpec((1, tm, cols), lambda i, k, pr: (k, i, 0))
    return pl.pallas_call(
        body, name=name, out_shape=[S((hr, cols), F32), S((n, hr, cols), BF16)],
        grid_spec=pltpu.PrefetchScalarGridSpec(
            num_scalar_prefetch=1, grid=(hr // tm, n),
            in_specs=[pl.BlockSpec((1, 1, 1, 1, tm, cols), lambda i, k, pr: (k // 2, pick, k % 2, pr[0], i, 0)), spec],
            out_specs=[pl.BlockSpec((tm, cols), lambda i, k, pr: (i, 0)), spec]),
        compiler_params=_params(("arbitrary", "arbitrary")),
    )(place, a, b)


def _add_chips(place, sums, others, layer, into, name):
    hr, cols = sums.shape
    tm = _row_tile(hr, 512)

    def body(place_ref, m_ref, o_ref, *rest):
        acc = m_ref[...]
        for k in range(others.shape[0]):
            acc = acc + o_ref[k].astype(F32)
        rest[-1][0, 0] = acc

    kept = [] if into is None else [into]
    return pl.pallas_call(
        body, name=name, out_shape=S((DEPTH, 2, hr, cols), F32),
        grid_spec=pltpu.PrefetchScalarGridSpec(
            num_scalar_prefetch=1, grid=(hr // tm,),
            in_specs=[pl.BlockSpec((tm, cols), lambda i, pr: (i, 0)),
                      pl.BlockSpec((others.shape[0], tm, cols), lambda i, pr: (0, i, 0))] + [ANY] * len(kept),
            out_specs=pl.BlockSpec((1, 1, tm, cols), lambda i, pr: (layer, pr[0], i, 0))),
        input_output_aliases={3: 0} if kept else {},
        compiler_params=_params(("arbitrary",)),
    )(place, sums, others, *kept)


def _adamw(wt, g, m, v, name):
    shape = wt.shape
    cols = shape[-1]
    rows = math.prod(shape[:-1])
    tm = rows
    for cand in (512, 256, 128, 64, 32, 16, 8):
        if rows % cand == 0:
            tm = cand
            break
    c1 = 1.0 - ADAM_B1 ** ADAM_STEP
    c2 = 1.0 - ADAM_B2 ** ADAM_STEP

    def body(w_ref, g_ref, m_ref, v_ref, d_ref, nm_ref, nv_ref):
        gv = g_ref[...]
        nm = ADAM_B1 * m_ref[...] + (1.0 - ADAM_B1) * gv
        nv = ADAM_B2 * v_ref[...] + (1.0 - ADAM_B2) * (gv * gv)
        d_ref[...] = -ADAM_LR * ((nm / c1) / (jnp.sqrt(nv / c2) + ADAM_EPS) + ADAM_WD * w_ref[...])
        nm_ref[...] = nm
        nv_ref[...] = nv

    spec = pl.BlockSpec((tm, cols), lambda i: (i, 0))
    outs = pl.pallas_call(
        body, grid=(rows // tm,), name=name, in_specs=[spec] * 4, out_specs=[spec] * 3,
        out_shape=[S((rows, cols), F32)] * 3, compiler_params=_params(("arbitrary",)),
    )(*[a.reshape(rows, cols) for a in (wt, g, m, v)])
    return [o.reshape(shape) for o in outs]


WEIGHTS = ('pre_mix_norm', 'post_mix_norm', 'pre_ffn_norm', 'post_ffn_norm', 'w_in', 'w_out', 'attn_sinks', 'ssd_conv_w',
           'ssd_conv_b', 'ssd_dt_bias', 'ssd_A_log', 'ssd_D', 'ssd_norm_w', 'gdn_conv_w', 'gdn_dt_bias', 'gdn_A_log',
           'gdn_norm_w', 'ffn_w_gate', 'ffn_w_up', 'ffn_w_down')


def _chip_piece(i, raw, shape):
    half = (2, shape[1] // 2, shape[2])
    if i in (2, 3):
        return raw["w_gu"].reshape((2, 2, 2) + half), i - 2
    if i == 0:
        g, width = _unpad_cols(raw["w_in_pad"]), shape[2]
        g = jnp.stack([g[:, k * width:(k + 1) * width] for k in range(N_CHIPS)])
    elif i == 1:
        g = raw["w_out_cat"]
        g = jnp.concatenate([g[512:768], g[0:512], g[768:1024]], axis=0)
    else:
        g = raw["ffn_w_down"]
    return g.reshape((2, 1, 2) + half), 0


def _step(x, target, wts, ms, vs):
    chip = 2 * lax.axis_index("x") + lax.axis_index("y")
    place = jnp.stack([lax.axis_index("c"), chip]).astype(jnp.int32)
    big_names = list(BIG)
    flip = lambda k, a: jnp.swapaxes(a, 1, 2) if k in TRANSPOSED else a
    wts, ms, vs = ({k: flip(k, a) for k, a in d.items()} for d in (wts, ms, vs))
    big_shapes = [wts[k].shape for k in big_names]
    halves = lambda a: a.reshape((2, a.shape[0] // 2) + a.shape[1:])

    shard = lambda name, l: halves(wts[name][l].astype(BF16))
    first_gather = _gather_exchange([shard("w_in", 0)])
    carried_by = {"ssd": [("w_out", 0), ("ffn_w_gate", 0)],
                  "gdn": [("ffn_w_up", 0), ("ffn_w_down", 0), ("w_in", 1), ("w_out", 1)],
                  "ffn_gu": [("ffn_w_gate", 1), ("ffn_w_up", 1)],
                  "ffn_down": [("ffn_w_down", 1)]}
    gathers = {kind: _gather_exchange([shard(*key) for key in keys]) for kind, keys in carried_by.items()}

    def matmul_weights(l, landed):
        gathered = {}
        for kind, keys in carried_by.items():
            for key, got in zip(keys, landed.get(kind, [])):
                if key[1] == l:
                    gathered[key[0]] = got.reshape((N_CHIPS,) + wts[key[0]].shape[1:])
        return _matmul_weights(None if l == 0 else _in_weight(gathered["w_in"]), gathered)

    conv = _gather_small(_pack_small([wts[k] for k in CONV]), "gather_conv_weights")
    conv = [_unpack_small(conv[2 * k], [wts[n].shape for n in CONV]) for k in range(N_CHIPS)]
    w_all = dict(wts)
    for i, n in enumerate(CONV):
        w_all[n] = jnp.concatenate([conv[k][i] for k in range(N_CHIPS)], axis=2)

    early_keys = [(DEPTH - 1, 0)] + [(l, i) for l in reversed(range(DEPTH)) for i in range(1, len(BIG))]
    late_keys = [(l, 0) for l in range(DEPTH - 1)]

    def pieces_of(keys, g):
        return [_chip_piece(i, {k: g[k][l] for k in RAW_GRADS}, big_shapes[i]) for l, i in keys]

    def add_siblings(tag, pieces, siblings):
        return [_add_sibling(place, p, t.reshape((N_CHIPS,) + t.shape[2:]), f"add_sibling_{tag}{n}")
                for n, (p, t) in enumerate(zip(pieces, siblings))]

    class Reducer:
        pieces, sums, late_sums = [], [], []

        def exchange(self, g):
            self.pieces = pieces_of(early_keys, g)
            return _halves_exchange(self.pieces)

        def scatter(self, siblings):
            self.sums = add_siblings("early", self.pieces, siblings)
            return _scatter_exchange([s16 for _, s16 in self.sums])

        def late(self, g):
            pieces = pieces_of(late_keys, g)
            siblings = _run_exchange(_halves_exchange(pieces), "exchange_halves_late")
            self.late_sums = add_siblings("late", pieces, siblings)
            return _scatter_exchange([s16 for _, s16 in self.late_sums])

    reducer = Reducer()

    loss_part, grad_x, small_g, raw, early, late = _local_step(
        x[0], target[0], [_small_operands(w_all, l) for l in range(DEPTH)], first_gather, gathers, matmul_weights, reducer)

    reduced = {}
    for tag, keys, sums, others in (("early", early_keys, reducer.sums, early), ("late", late_keys, reducer.late_sums, late)):
        for n, ((l, i), (s32, _), o) in enumerate(zip(keys, sums, others)):
            reduced[i] = _add_chips(place, s32, o, l, reduced.get(i), f"add_chips_{tag}{n}")
    joined = _join_halves([reduced[i] for i in range(len(BIG))], "join_halves")
    g_all = {k: q.reshape(shp) for k, q, shp in zip(big_names, joined, big_shapes)}

    names = SMALL + CONV
    packed = _pack_small([small_g[k] for k in names] + [loss_part])
    small_sum = _sum_leading(_gather_small(packed, "gather_small_grads"), "add_small")
    vals = _unpack_small(small_sum, [small_g[k].shape for k in names] + [(1, LANES)])
    loss = vals[-1][0, 0]
    for k, v in zip(names, vals[:-1]):
        if k in CONV:
            width = wts[k].shape[2]
            v = lax.dynamic_slice_in_dim(v.reshape(DEPTH, 4, -1), chip * width, width, axis=2)
        g_all[k] = v.reshape(wts[k].shape)

    shapes = [wts[k].shape for k in names]
    d_s, m_s, v_s = _adamw(_pack_small([wts[k] for k in names]), _pack_small([g_all[k] for k in names]),
                           _pack_small([ms[k] for k in names]), _pack_small([vs[k] for k in names]), "adamw_small")
    upd = dict(zip(names, zip(_unpack_small(d_s, shapes), _unpack_small(m_s, shapes), _unpack_small(v_s, shapes))))
    for k in big_names:
        upd[k] = _adamw(wts[k], g_all[k], ms[k], vs[k], f"adamw_{k}")
    return (loss, grad_x[None], *[flip(k, g_all[k]) for k in WEIGHTS], *[flip(k, upd[k][0]) for k in WEIGHTS],
            *[flip(k, upd[k][1]) for k in WEIGHTS], *[flip(k, upd[k][2]) for k in WEIGHTS])


def kernel(x, pre_mix_norm, post_mix_norm, pre_ffn_norm, post_ffn_norm, w_in, w_out, attn_sinks, ssd_conv_w, ssd_conv_b, ssd_dt_bias, ssd_A_log, ssd_D, ssd_norm_w, gdn_conv_w, gdn_dt_bias, gdn_A_log, gdn_norm_w, ffn_w_gate, ffn_w_up, ffn_w_down, loss_target, m_pre_mix_norm, m_post_mix_norm, m_pre_ffn_norm, m_post_ffn_norm, m_w_in, m_w_out, m_attn_sinks, m_ssd_conv_w, m_ssd_conv_b, m_ssd_dt_bias, m_ssd_A_log, m_ssd_D, m_ssd_norm_w, m_gdn_conv_w, m_gdn_dt_bias, m_gdn_A_log, m_gdn_norm_w, m_ffn_w_gate, m_ffn_w_up, m_ffn_w_down, v_pre_mix_norm, v_post_mix_norm, v_pre_ffn_norm, v_post_ffn_norm, v_w_in, v_w_out, v_attn_sinks, v_ssd_conv_w, v_ssd_conv_b, v_ssd_dt_bias, v_ssd_A_log, v_ssd_D, v_ssd_norm_w, v_gdn_conv_w, v_gdn_dt_bias, v_gdn_A_log, v_gdn_norm_w, v_ffn_w_gate, v_ffn_w_up, v_ffn_w_down):
    wts = dict(zip(WEIGHTS, (pre_mix_norm, post_mix_norm, pre_ffn_norm, post_ffn_norm, w_in, w_out, attn_sinks, ssd_conv_w, ssd_conv_b, ssd_dt_bias, ssd_A_log, ssd_D, ssd_norm_w, gdn_conv_w, gdn_dt_bias, gdn_A_log, gdn_norm_w, ffn_w_gate, ffn_w_up, ffn_w_down)))
    ms = dict(zip(WEIGHTS, (m_pre_mix_norm, m_post_mix_norm, m_pre_ffn_norm, m_post_ffn_norm, m_w_in, m_w_out, m_attn_sinks, m_ssd_conv_w, m_ssd_conv_b, m_ssd_dt_bias, m_ssd_A_log, m_ssd_D, m_ssd_norm_w, m_gdn_conv_w, m_gdn_dt_bias, m_gdn_A_log, m_gdn_norm_w, m_ffn_w_gate, m_ffn_w_up, m_ffn_w_down)))
    vs = dict(zip(WEIGHTS, (v_pre_mix_norm, v_post_mix_norm, v_pre_ffn_norm, v_post_ffn_norm, v_w_in, v_w_out, v_attn_sinks, v_ssd_conv_w, v_ssd_conv_b, v_ssd_dt_bias, v_ssd_A_log, v_ssd_D, v_ssd_norm_w, v_gdn_conv_w, v_gdn_dt_bias, v_gdn_A_log, v_gdn_norm_w, v_ffn_w_gate, v_ffn_w_up, v_ffn_w_down)))
    return _step(x, loss_target, wts, ms, vs)
```

```python
import functools
import math

import jax
import jax.numpy as jnp
from jax import lax
from jax.experimental import pallas as pl
from jax.experimental.pallas import tpu as pltpu

F32, BF16 = jnp.float32, jnp.bfloat16
HI = lax.Precision.HIGHEST
MESH = pl.DeviceIdType.MESH
S = jax.ShapeDtypeStruct

D_MODEL = 1024
DEPTH = 2
CHUNK = 64
SSD_CHUNK = 256
GDN_CHUNK = 128
EPS = 1e-6
FF = 2816
N_CHIPS = 4
N_DEV = 8
LANES = 128

VMEM_LIMIT_BYTES = 56 * 1024 * 1024

PC_GQKV, PC_GZ, PC_XBC, PC_ATT, PC_SZ, PC_DT, PC_BA, PC_TOT = 0, 768, 1024, 2048, 2560, 3072, 3200, 3328

ADAM_LR, ADAM_B1, ADAM_B2, ADAM_EPS, ADAM_WD, ADAM_STEP = 0.001, 0.9, 0.999, 1e-08, 0.01, 10

ALIBI_SLOPES = tuple(2.0 ** (-8.0 * (h + 1) / 4) for h in range(4))


def _params(sem=None, **kw):
    if sem is not None:
        kw["dimension_semantics"] = sem
    return pltpu.CompilerParams(vmem_limit_bytes=VMEM_LIMIT_BYTES, **kw)


def _dot(a, b, prec=None):
    return jnp.dot(a, b, precision=prec, preferred_element_type=F32)


def _dot_nt(a, b, prec=None):
    return lax.dot_general(a, b, (((1,), (1,)), ((), ())), precision=prec, preferred_element_type=F32)


def _dot_tn(a, b, prec=None):
    return lax.dot_general(a, b, (((0,), (0,)), ((), ())), precision=prec, preferred_element_type=F32)


def _iota2(n, m):
    return lax.broadcasted_iota(jnp.int32, (n, m), 0), lax.broadcasted_iota(jnp.int32, (n, m), 1)


def _pick_col(arr, idx):
    ci = lax.broadcasted_iota(jnp.int32, arr.shape, 1)
    return jnp.sum(jnp.where(ci == idx, arr, 0.0), axis=1, keepdims=True)


def _pick_row(arr, idx):
    ri = lax.broadcasted_iota(jnp.int32, arr.shape, 0)
    return jnp.sum(jnp.where(ri == idx, arr, 0.0), axis=0, keepdims=True)


def _col_to_row(col, eye):
    return jnp.sum(eye * col, axis=0, keepdims=True)


def _rms(x, w):
    return x * lax.rsqrt(jnp.mean(x * x, axis=-1, keepdims=True) + EPS) * w


def _mm_nn(a, b, tm, tn, out_dtype, name, exchange=None):
    m, k = a.shape
    n = b.shape[1]
    tm, tn = min(tm, m), min(tn, n)
    grid = (n // tn, m // tm)

    def body(a_ref, b_ref, o_ref):
        o_ref[...] = _dot(a_ref[...], b_ref[...]).astype(o_ref.dtype)

    body, x_in, x_out, x_shape, x_sems = _hosted(exchange, 2, 1, grid, body)
    outs = pl.pallas_call(
        body, grid=grid, name=name,
        in_specs=[pl.BlockSpec((tm, k), lambda j, i: (i, 0)), pl.BlockSpec((k, tn), lambda j, i: (0, j))] + x_in,
        out_specs=[pl.BlockSpec((tm, tn), lambda j, i: (i, j))] + x_out,
        out_shape=[S((m, n), out_dtype)] + x_shape, scratch_shapes=x_sems,
        compiler_params=_params(("arbitrary", "arbitrary")),
    )(a, b, *([] if exchange is None else exchange.arrays))
    return outs[0] if exchange is None else outs


def _mm_nt(a, b, tm, tn, out_dtype, name, exchange=None):
    m, k = a.shape
    n = b.shape[0]
    tm, tn = min(tm, m), min(tn, n)
    grid = (n // tn, m // tm)

    def body(a_ref, b_ref, o_ref):
        o_ref[...] = _dot_nt(a_ref[...], b_ref[...]).astype(o_ref.dtype)

    body, x_in, x_out, x_shape, x_sems = _hosted(exchange, 2, 1, grid, body)
    outs = pl.pallas_call(
        body, grid=grid, name=name,
        in_specs=[pl.BlockSpec((tm, k), lambda j, i: (i, 0)), pl.BlockSpec((tn, k), lambda j, i: (j, 0))] + x_in,
        out_specs=[pl.BlockSpec((tm, tn), lambda j, i: (i, j))] + x_out,
        out_shape=[S((m, n), out_dtype)] + x_shape, scratch_shapes=x_sems,
        compiler_params=_params(("arbitrary", "arbitrary")),
    )(a, b, *([] if exchange is None else exchange.arrays))
    return outs[0] if exchange is None else outs


def _mm_tn(a, b, tm, tn, tk, name):
    t, m = a.shape
    n = b.shape[1]
    tm, tn, tk = min(tm, m), min(tn, n), min(tk, t)

    def body(a_ref, b_ref, o_ref):
        part = _dot_tn(a_ref[...], b_ref[...])

        @pl.when(pl.program_id(2) == 0)
        def _():
            o_ref[...] = part

        @pl.when(pl.program_id(2) > 0)
        def _():
            o_ref[...] += part

    return pl.pallas_call(
        body, grid=(m // tm, n // tn, t // tk), name=name,
        in_specs=[pl.BlockSpec((tk, tm), lambda i, j, k: (k, i)), pl.BlockSpec((tk, tn), lambda i, j, k: (k, j))],
        out_specs=pl.BlockSpec((tm, tn), lambda i, j, k: (i, j)),
        out_shape=S((m, n), F32), compiler_params=_params(("arbitrary", "arbitrary", "arbitrary")),
    )(a, b)


def _rowcall(fn, rows, params, row_outs, acc_outs, name, tm=512, exchange=None):
    t = rows[0].shape[0]
    tm = min(tm, t)
    n_in = len(rows) + len(params)
    n_ro = len(row_outs)

    def body(*refs):
        ro, ao = fn(*[r[...] for r in refs[:n_in]])
        for ref, v in zip(refs[n_in:n_in + n_ro], ro):
            ref[...] = v.astype(ref.dtype)
        acc_refs = refs[n_in + n_ro:]
        if acc_refs:
            @pl.when(pl.program_id(0) == 0)
            def _():
                for ref, v in zip(acc_refs, ao):
                    ref[...] = v

            @pl.when(pl.program_id(0) > 0)
            def _():
                for ref, v in zip(acc_refs, ao):
                    ref[...] += v

    in_specs = [pl.BlockSpec((tm, r.shape[1]), lambda i: (i, 0)) for r in rows]
    in_specs += [pl.BlockSpec(p.shape, lambda i: (0, 0)) for p in params]
    out_specs = [pl.BlockSpec((tm, c), lambda i: (i, 0)) for c, _ in row_outs]
    out_specs += [pl.BlockSpec(shape, lambda i: (0, 0)) for shape in acc_outs]
    out_shape = [S((t, c), dt) for c, dt in row_outs] + [S(shape, F32) for shape in acc_outs]
    body, x_in, x_out, x_shape, x_sems = _hosted(exchange, n_in, len(out_shape), t // tm, body)
    return pl.pallas_call(
        body, grid=(t // tm,), name=name, in_specs=in_specs + x_in, out_specs=out_specs + x_out,
        out_shape=out_shape + x_shape, scratch_shapes=x_sems, compiler_params=_params(("arbitrary",)),
    )(*rows, *params, *([] if exchange is None else exchange.arrays))


def _prenorm(x, w, name, exchange=None):
    def fn(x, w):
        return (_rms(x, w),), ()
    return _rowcall(fn, [x], [w], [(D_MODEL, BF16)], [], name, exchange=exchange)


def _mm_resid_norm(a, b, xin, w_post, w_next, name, exchange=None, tm=512):
    t, k = a.shape
    n = b.shape[1]
    tm = min(tm, t)

    def body(a_ref, b_ref, x_ref, wp_ref, wn_ref, m_ref, xo_ref, h_ref):
        m = _dot(a_ref[...], b_ref[...])
        m_ref[...] = m
        xo = x_ref[...] + _rms(m, wp_ref[...])
        xo_ref[...] = xo
        h_ref[...] = _rms(xo, wn_ref[...]).astype(h_ref.dtype)

    body, x_in, x_out, x_shape, x_sems = _hosted(exchange, 5, 3, t // tm, body)
    row = pl.BlockSpec((tm, n), lambda i: (i, 0))
    vec = pl.BlockSpec((1, n), lambda i: (0, 0))
    return pl.pallas_call(
        body, grid=(t // tm,), name=name,
        in_specs=[pl.BlockSpec((tm, k), lambda i: (i, 0)), pl.BlockSpec((k, n), lambda i: (0, 0)), row, vec, vec] + x_in,
        out_specs=[row, row, row] + x_out,
        out_shape=[S((t, n), F32), S((t, n), F32), S((t, n), BF16)] + x_shape, scratch_shapes=x_sems,
        compiler_params=_params(("arbitrary",)),
    )(a, b, xin, w_post, w_next, *([] if exchange is None else exchange.arrays))


def _resid_loss(xin, m, w_post, target, name):
    def fn(xin, m, target, w_post):
        r, vjp = jax.vjp(_rms, m, w_post)
        err = xin + r - target
        dy = err * (1.0 / D_MODEL)
        dm, dw = vjp(dy)
        tot = jnp.sum(jnp.sum(err * err, axis=1, keepdims=True), axis=0, keepdims=True) * (0.5 / D_MODEL)
        lane = lax.broadcasted_iota(jnp.int32, (1, LANES), 1)
        return (dy, dm), (jnp.where(lane == 0, tot, 0.0), dw)
    return _rowcall(fn, [xin, m, target], [w_post], [(D_MODEL, F32), (D_MODEL, BF16)],
                    [(1, LANES), (1, D_MODEL)], name)


def _resid_norm_bwd(x_out, m, d_direct, dh, w_post, w_next, name):
    def fn(x_out, m, d_direct, dh, w_post, w_next):
        _, vjp_n = jax.vjp(_rms, x_out, w_next)
        dx, dwn = vjp_n(dh)
        d_total = d_direct + dx
        _, vjp_p = jax.vjp(_rms, m, w_post)
        dm, dwp = vjp_p(d_total)
        return (d_total, dm), (dwn, dwp)
    return _rowcall(fn, [x_out, m, d_direct, dh], [w_post, w_next], [(D_MODEL, F32), (D_MODEL, BF16)],
                    [(1, D_MODEL), (1, D_MODEL)], name)


def _prenorm_bwd(x, d_direct, dh, w, name):
    def fn(x, d_direct, dh, w):
        _, vjp = jax.vjp(_rms, x, w)
        dx, dw = vjp(dh)
        return (d_direct + dx,), (dw,)
    return _rowcall(fn, [x, d_direct, dh], [w], [(D_MODEL, F32)], [(1, D_MODEL)], name)


FF_HALF = FF // 2


def _interleave_gu(gate_t, up_t):
    return jnp.concatenate([gate_t[:FF_HALF], up_t[:FF_HALF], gate_t[FF_HALF:], up_t[FF_HALF:]], axis=0)


def _swiglu_pair(gu):
    n = gu.shape[1] // 2
    return jax.nn.silu(gu[:, :n]) * gu[:, n:]


def _ffn_up(h2, w_gu, name, tm=512, exchange=None):
    t, k = h2.shape
    tm = min(tm, t)
    grid = (2, t // tm)

    def body(a_ref, b_ref, gu_ref, act_ref):
        gu = _dot_nt(a_ref[...], b_ref[...])
        gu_ref[...] = gu.astype(gu_ref.dtype)
        act_ref[...] = _swiglu_pair(gu).astype(act_ref.dtype)

    body, x_in, x_out, x_shape, x_sems = _hosted(exchange, 2, 2, grid, body)
    return pl.pallas_call(
        body, grid=grid, name=name,
        in_specs=[pl.BlockSpec((tm, k), lambda j, i: (i, 0)), pl.BlockSpec((FF, k), lambda j, i: (j, 0))] + x_in,
        out_specs=[pl.BlockSpec((tm, FF), lambda j, i: (i, j)), pl.BlockSpec((tm, FF_HALF), lambda j, i: (i, j))] + x_out,
        out_shape=[S((t, 2 * FF), BF16), S((t, FF), BF16)] + x_shape, scratch_shapes=x_sems,
        compiler_params=_params(("arbitrary", "arbitrary")),
    )(h2, w_gu, *([] if exchange is None else exchange.arrays))


def _ffn_down_bwd(d_f, w_down, gu, name, tm=512):
    t, k = d_f.shape
    tm = min(tm, t)

    def body(a_ref, b_ref, gu_ref, o_ref):
        d_act = _dot_nt(a_ref[...], b_ref[...])
        _, vjp = jax.vjp(_swiglu_pair, gu_ref[...].astype(F32))
        o_ref[...] = vjp(d_act)[0].astype(o_ref.dtype)

    return pl.pallas_call(
        body, grid=(2, t // tm), name=name,
        in_specs=[pl.BlockSpec((tm, k), lambda j, i: (i, 0)), pl.BlockSpec((FF_HALF, k), lambda j, i: (j, 0)),
                  pl.BlockSpec((tm, FF), lambda j, i: (i, j))],
        out_specs=pl.BlockSpec((tm, FF), lambda j, i: (i, j)),
        out_shape=S((t, 2 * FF), BF16), compiler_params=_params(("arbitrary", "arbitrary")),
    )(d_f, w_down, gu)


def _conv_fwd(proj, col0, width, w, b, name, tm=512):
    t = proj.shape[0]
    tm = min(tm, t)
    cb = col0 // width

    def body(x_ref, w_ref, b_ref, o_ref, ext):
        @pl.when(pl.program_id(0) == 0)
        def _():
            ext[0:8, :] = jnp.zeros((8, width), F32)

        ext[8:8 + tm, :] = x_ref[...]
        y = b_ref[...] + w_ref[0:1, :] * ext[pl.ds(5, tm), :]
        for k in range(1, 4):
            y = y + w_ref[k:k + 1, :] * ext[pl.ds(5 + k, tm), :]
        o_ref[...] = jax.nn.silu(y)
        ext[0:8, :] = ext[tm:tm + 8, :]

    return pl.pallas_call(
        body, grid=(t // tm,), name=name,
        in_specs=[pl.BlockSpec((tm, width), lambda i: (i, cb)), pl.BlockSpec((4, width), lambda i: (0, 0)),
                  pl.BlockSpec((1, width), lambda i: (0, 0))],
        out_specs=pl.BlockSpec((tm, width), lambda i: (i, 0)),
        out_shape=S((t, width), F32), scratch_shapes=[pltpu.VMEM((tm + 8, width), F32)],
        compiler_params=_params(("arbitrary",)),
    )(proj, w, b)


def _conv_bwd(proj, col0, width, w, b, dact, name, tm=512):
    t = proj.shape[0]
    tm = min(tm, t)
    nb = t // tm
    cb = col0 // width
    hb = tm // 8

    def body(x_ref, halo_ref, d_ref, w_ref, b_ref, dx_ref, dw_ref, db_ref, extx, extd):
        i = pl.program_id(0)
        blk = nb - 1 - i

        @pl.when(i == 0)
        def _():
            extd[tm:tm + 8, :] = jnp.zeros((8, width), F32)
            dw_ref[...] = jnp.zeros((4, width), F32)
            db_ref[...] = jnp.zeros((1, width), F32)

        extx[0:8, :] = jnp.where(blk == 0, 0.0, halo_ref[...])
        extx[8:8 + tm, :] = x_ref[...]
        y = b_ref[...] + w_ref[0:1, :] * extx[pl.ds(5, tm), :]
        for k in range(1, 4):
            y = y + w_ref[k:k + 1, :] * extx[pl.ds(5 + k, tm), :]
        sig = jax.nn.sigmoid(y)
        dy = d_ref[...] * (sig * (1.0 + y * (1.0 - sig)))
        extd[0:tm, :] = dy
        dx = w_ref[0:1, :] * extd[pl.ds(3, tm), :]
        for k in range(1, 4):
            dx = dx + w_ref[k:k + 1, :] * extd[pl.ds(3 - k, tm), :]
        dx_ref[...] = dx.astype(dx_ref.dtype)
        for k in range(4):
            dw_ref[k:k + 1, :] += jnp.sum(dy * extx[pl.ds(5 + k, tm), :], axis=0, keepdims=True)
        db_ref[...] += jnp.sum(dy, axis=0, keepdims=True)
        extd[tm:tm + 8, :] = extd[0:8, :]

    return pl.pallas_call(
        body, grid=(nb,), name=name,
        in_specs=[pl.BlockSpec((tm, width), lambda i: (nb - 1 - i, cb)),
                  pl.BlockSpec((8, width), lambda i: (jnp.maximum((nb - 1 - i) * hb - 1, 0), cb)),
                  pl.BlockSpec((tm, width), lambda i: (nb - 1 - i, 0)),
                  pl.BlockSpec((4, width), lambda i: (0, 0)), pl.BlockSpec((1, width), lambda i: (0, 0))],
        out_specs=[pl.BlockSpec((tm, width), lambda i: (nb - 1 - i, 0)), pl.BlockSpec((4, width), lambda i: (0, 0)),
                   pl.BlockSpec((1, width), lambda i: (0, 0))],
        out_shape=[S((t, width), BF16), S((4, width), F32), S((1, width), F32)],
        scratch_shapes=[pltpu.VMEM((tm + 8, width), F32), pltpu.VMEM((tm + 8, width), F32)],
        compiler_params=_params(("arbitrary",)),
    )(proj, proj, dact, w, b)


SWA_BQ = 256
SWA_BACK = 128


def _swa_block(q, kw, vw, sinks, blk, bf16_operands=False):
    op = (lambda a: a.astype(BF16)) if bf16_operands else (lambda a: a)
    nq, nk = SWA_BQ, SWA_BQ + SWA_BACK
    r, j = _iota2(nq, nk)
    rel = r // CHUNK + 2 - j // CHUNK
    valid = (rel >= 0) & (rel <= 2) & (blk * (SWA_BQ // CHUNK) + j // CHUNK - 2 >= 0)
    dist = jnp.abs(r + SWA_BACK - j).astype(F32)
    outs = []
    for h in range(4):
        kv = h // 2
        qh = q[:, 64 * h:64 * h + 64]
        kh = kw[:, 64 * kv:64 * kv + 64]
        vh = vw[:, 64 * kv:64 * kv + 64]
        s = _dot_nt(op(qh), op(kh)) * 0.125 - ALIBI_SLOPES[h] * dist
        s = jnp.where(valid, s, -1e30)
        sink = _pick_col(sinks, h)
        m = jnp.maximum(jnp.max(s, axis=1, keepdims=True), sink)
        e = jnp.exp(s - m)
        den = jnp.sum(e, axis=1, keepdims=True) + jnp.exp(sink - m)
        outs.append(_dot(op(e / den), op(vh)))
    return jnp.concatenate(outs, axis=1)


def _swa_fwd(proj, sinks, name):
    t = proj.shape[0]
    qb, kb = PC_ATT // 256, PC_ATT // 128 + 2
    win = SWA_BQ + SWA_BACK

    def body(q_ref, k_ref, v_ref, s_ref, o_ref, kp, vp):
        i = pl.program_id(0)

        @pl.when(i == 0)
        def _():
            kp[0:SWA_BACK, :] = jnp.zeros((SWA_BACK, 128), F32)
            vp[0:SWA_BACK, :] = jnp.zeros((SWA_BACK, 128), F32)
            kp[SWA_BACK:, :] = k_ref[...]
            vp[SWA_BACK:, :] = v_ref[...]

        start = pl.multiple_of(i * SWA_BQ, SWA_BQ)
        o = _swa_block(q_ref[...], kp[pl.ds(start, win), :], vp[pl.ds(start, win), :], s_ref[...], i)
        o_ref[...] = o.astype(o_ref.dtype)

    return pl.pallas_call(
        body, grid=(t // SWA_BQ,), name=name,
        in_specs=[pl.BlockSpec((SWA_BQ, 256), lambda i: (i, qb)), pl.BlockSpec((t, 128), lambda i: (0, kb)),
                  pl.BlockSpec((t, 128), lambda i: (0, kb + 1)), pl.BlockSpec((1, LANES), lambda i: (0, 0))],
        out_specs=pl.BlockSpec((SWA_BQ, 256), lambda i: (i, 0)),
        out_shape=S((t, 256), BF16),
        scratch_shapes=[pltpu.VMEM((t + SWA_BACK, 128), F32), pltpu.VMEM((t + SWA_BACK, 128), F32)],
        compiler_params=_params(("arbitrary",)),
    )(proj, proj, proj, sinks)


def _swa_bwd(proj, sinks, dcat, dcol0, name):
    t = proj.shape[0]
    nb = t // SWA_BQ
    qb, kb = PC_ATT // 256, PC_ATT // 128 + 2
    db = dcol0 // 256
    win = SWA_BQ + SWA_BACK

    def body(q_ref, k_ref, v_ref, s_ref, do_ref, dq_ref, dk_ref, dv_ref, ds_ref, kp, vp, dkp, dvp):
        i = pl.program_id(0)

        @pl.when(i == 0)
        def _():
            kp[0:SWA_BACK, :] = jnp.zeros((SWA_BACK, 128), F32)
            vp[0:SWA_BACK, :] = jnp.zeros((SWA_BACK, 128), F32)
            kp[SWA_BACK:, :] = k_ref[...]
            vp[SWA_BACK:, :] = v_ref[...]
            dkp[...] = jnp.zeros_like(dkp)
            dvp[...] = jnp.zeros_like(dvp)
            ds_ref[...] = jnp.zeros_like(ds_ref)

        start = pl.multiple_of(i * SWA_BQ, SWA_BQ)
        _, vjp = jax.vjp(functools.partial(_swa_block, blk=i, bf16_operands=True), q_ref[...], kp[pl.ds(start, win), :],
                         vp[pl.ds(start, win), :], s_ref[...])
        dq, dkw, dvw, dsk = vjp(do_ref[...])
        dq_ref[...] = dq.astype(dq_ref.dtype)
        dkp[pl.ds(start, win), :] += dkw
        dvp[pl.ds(start, win), :] += dvw
        ds_ref[...] += dsk

        @pl.when(i == nb - 1)
        def _():
            dk_ref[...] = dkp[SWA_BACK:, :].astype(dk_ref.dtype)
            dv_ref[...] = dvp[SWA_BACK:, :].astype(dv_ref.dtype)

    return pl.pallas_call(
        body, grid=(nb,), name=name,
        in_specs=[pl.BlockSpec((SWA_BQ, 256), lambda i: (i, qb)), pl.BlockSpec((t, 128), lambda i: (0, kb)),
                  pl.BlockSpec((t, 128), lambda i: (0, kb + 1)), pl.BlockSpec((1, LANES), lambda i: (0, 0)),
                  pl.BlockSpec((SWA_BQ, 256), lambda i: (i, db))],
        out_specs=[pl.BlockSpec((SWA_BQ, 256), lambda i: (i, 0)), pl.BlockSpec((t, 128), lambda i: (0, 0)),
                   pl.BlockSpec((t, 128), lambda i: (0, 0)), pl.BlockSpec((1, LANES), lambda i: (0, 0))],
        out_shape=[S((t, 256), BF16), S((t, 128), BF16), S((t, 128), BF16), S((1, LANES), F32)],
        scratch_shapes=[pltpu.VMEM((t + SWA_BACK, 128), F32) for _ in range(4)],
        compiler_params=_params(("arbitrary",)),
    )(proj, proj, proj, sinks, dcat)


def _ssd_chunk(z, xbc, dt_raw, state, dtb, alog, dsk, nw, bf16_operands=False):
    n = z.shape[0]
    op = (lambda a: a.astype(BF16)) if bf16_operands else (lambda a: a)
    r, c = _iota2(n, n)
    tril = r >= c
    eye = (r == c).astype(F32)
    dt = jax.nn.softplus(dt_raw + dtb)
    acs = _dot(tril.astype(F32), dt * (-jnp.exp(alog)), HI)
    xs, bm, cm = xbc[:, :512], xbc[:, 512:768], xbc[:, 768:1024]
    heads = range(8)
    bg = [bm[:, 128 * g:128 * g + 128] for g in range(2)]
    cg = [cm[:, 128 * g:128 * g + 128] for g in range(2)]
    cb = [_dot_nt(op(cg[g]), op(bg[g])) for g in range(2)]
    dth = [_pick_col(dt, h) for h in heads]
    acol = [_pick_col(acs, h) for h in heads]
    arow = [_col_to_row(a, eye) for a in acol]
    lmat = [jnp.where(tril, jnp.exp(jnp.where(tril, a - b, 0.0)), 0.0) for a, b in zip(acol, arow)]
    xh = [xs[:, 64 * h:64 * h + 64] for h in heads]
    xc = [x * t for x, t in zip(xh, dth)]
    st = [state[64 * h:64 * h + 64, :] for h in heads]
    alast = [_pick_row(a, n - 1) for a in acol]
    y_in = [_dot(op(cb[h // 4] * lmat[h]), op(xc[h])) for h in heads]
    y_st = [_dot_nt(op(cg[h // 4]), op(st[h])) * jnp.exp(acol[h]) for h in heads]
    ys = [y_in[h] + y_st[h] + xh[h] * _pick_col(dsk, h) for h in heads]
    new_states = [st[h] * jnp.exp(alast[h]) + _dot_tn(op(xc[h] * jnp.exp(alast[h] - acol[h])), op(bg[h // 4]))
                  for h in heads]
    gg = jnp.concatenate(ys, axis=1) * jax.nn.silu(z)
    outs = []
    for gi in range(2):
        gv = gg[:, 256 * gi:256 * gi + 256]
        outs.append(gv * lax.rsqrt(jnp.mean(gv * gv, axis=-1, keepdims=True) + EPS))
    return jnp.concatenate(outs, axis=1) * nw, jnp.concatenate(new_states, axis=0)


def _ssd_fwd(proj, xbc, dtb, alog, dsk, nw, name, exchange=None):
    t = proj.shape[0]
    rows = min(SSD_CHUNK, t)
    nc = t // rows

    def body(z_ref, x_ref, dt_ref, dtb_ref, al_ref, d_ref, nw_ref, o_ref, st_ref, state):
        @pl.when(pl.program_id(0) == 0)
        def _():
            state[...] = jnp.zeros_like(state)

        st_ref[0] = state[...]
        o, ns = _ssd_chunk(z_ref[...], x_ref[...], dt_ref[...], state[...], dtb_ref[...], al_ref[...], d_ref[...],
                           nw_ref[...])
        o_ref[...] = o.astype(o_ref.dtype)
        state[...] = ns

    body, x_in, x_out, x_shape, x_sems = _hosted(exchange, 7, 2, nc, body)
    vec = pl.BlockSpec((1, LANES), lambda i: (0, 0))
    return pl.pallas_call(
        body, grid=(nc,), name=name,
        in_specs=[pl.BlockSpec((rows, 512), lambda i: (i, PC_SZ // 512)), pl.BlockSpec((rows, 1024), lambda i: (i, 0)),
                  pl.BlockSpec((rows, 128), lambda i: (i, PC_DT // 128)), vec, vec, vec,
                  pl.BlockSpec((1, 512), lambda i: (0, 0))] + x_in,
        out_specs=[pl.BlockSpec((rows, 512), lambda i: (i, 0)), pl.BlockSpec((1, 512, 128), lambda i: (i, 0, 0))] + x_out,
        out_shape=[S((t, 512), BF16), S((nc, 512, 128), F32)] + x_shape,
        scratch_shapes=[pltpu.VMEM((512, 128), F32)] + x_sems,
        compiler_params=_params(("arbitrary",)),
    )(proj, xbc, proj, dtb, alog, dsk, nw, *([] if exchange is None else exchange.arrays))


def _ssd_bwd(proj, xbc, states, dtb, alog, dsk, nw, dcat, dcol0, name, exchange=None):
    t = proj.shape[0]
    rows = min(SSD_CHUNK, t)
    nc = t // rows
    db = dcol0 // 512

    def body(z_ref, x_ref, dt_ref, st_ref, dtb_ref, al_ref, d_ref, nw_ref, do_ref,
             dz_ref, dx_ref, ddt_ref, gdtb_ref, gal_ref, gd_ref, gnw_ref, dstate):
        @pl.when(pl.program_id(0) == 0)
        def _():
            dstate[...] = jnp.zeros_like(dstate)
            gdtb_ref[...] = jnp.zeros_like(gdtb_ref)
            gal_ref[...] = jnp.zeros_like(gal_ref)
            gd_ref[...] = jnp.zeros_like(gd_ref)
            gnw_ref[...] = jnp.zeros_like(gnw_ref)

        _, vjp = jax.vjp(functools.partial(_ssd_chunk, bf16_operands=True), z_ref[...], x_ref[...], dt_ref[...],
                         st_ref[0], dtb_ref[...], al_ref[...], d_ref[...], nw_ref[...])
        dz, dx, ddt, dst, gdtb, gal, gd, gnw = vjp((do_ref[...], dstate[...]))
        dz_ref[...] = dz.astype(dz_ref.dtype)
        dx_ref[...] = dx
        ddt_ref[...] = ddt.astype(ddt_ref.dtype)
        dstate[...] = dst
        gdtb_ref[...] += gdtb
        gal_ref[...] += gal
        gd_ref[...] += gd
        gnw_ref[...] += gnw

    body, x_in, x_out, x_shape, x_sems = _hosted(exchange, 9, 7, nc, body)
    rev = lambda i: nc - 1 - i
    vec = pl.BlockSpec((1, LANES), lambda i: (0, 0))
    vec512 = pl.BlockSpec((1, 512), lambda i: (0, 0))
    return pl.pallas_call(
        body, grid=(nc,), name=name,
        in_specs=[pl.BlockSpec((rows, 512), lambda i: (rev(i), PC_SZ // 512)),
                  pl.BlockSpec((rows, 1024), lambda i: (rev(i), 0)),
                  pl.BlockSpec((rows, 128), lambda i: (rev(i), PC_DT // 128)),
                  pl.BlockSpec((1, 512, 128), lambda i: (rev(i), 0, 0)), vec, vec, vec, vec512,
                  pl.BlockSpec((rows, 512), lambda i: (rev(i), db))] + x_in,
        out_specs=[pl.BlockSpec((rows, 512), lambda i: (rev(i), 0)), pl.BlockSpec((rows, 1024), lambda i: (rev(i), 0)),
                   pl.BlockSpec((rows, 128), lambda i: (rev(i), 0)), vec, vec, vec, vec512] + x_out,
        out_shape=[S((t, 512), BF16), S((t, 1024), F32), S((t, 128), BF16), S((1, LANES), F32), S((1, LANES), F32),
                   S((1, LANES), F32), S((1, 512), F32)] + x_shape,
        scratch_shapes=[pltpu.VMEM((512, 128), F32)] + x_sems,
        compiler_params=_params(("arbitrary",)),
    )(proj, xbc, proj, states, dtb, alog, dsk, nw, dcat, *([] if exchange is None else exchange.arrays))


SOLVE_PREC = lax.Precision.HIGH


def _unit_lower_inverses(nas, known=None):
    def compute(ns):
        if known is not None:
            return tuple(known)
        n = ns[0].shape[0]
        r, c = _iota2(n, n)
        eye = (r == c).astype(F32)
        tm, pw = [eye + a for a in ns], list(ns)
        for _ in range(n.bit_length() - 2):
            pw = [_dot(p, p, SOLVE_PREC) for p in pw]
            tm = [t + _dot(t, p, SOLVE_PREC) for t, p in zip(tm, pw)]
        return tuple(tm)

    inv = jax.custom_vjp(compute)

    def fwd(ns):
        ts = compute(ns)
        return ts, ts

    def bwd(ts, gs):
        part = [_dot_nt(g, t, SOLVE_PREC) for g, t in zip(gs, ts)]
        return (tuple(_dot_tn(t, p, SOLVE_PREC) for t, p in zip(ts, part)),)

    inv.defvjp(fwd, bwd)
    return inv(nas)


def _gdn_chunk(qkv, z, ba, state, dtb, alog, nw, known_inverses=None, bf16_operands=False):
    n = qkv.shape[0]
    op = (lambda a: a.astype(BF16)) if bf16_operands else (lambda a: a)
    r, c = _iota2(n, n)
    tril = r >= c
    stril = r > c
    eye = (r == c).astype(F32)
    beta_all = jax.nn.sigmoid(ba)
    gcs = _dot(tril.astype(F32), -jnp.exp(alog) * jax.nn.softplus(ba + dtb), HI)
    heads = range(4)
    qh = [qkv[:, 64 * h:64 * h + 64] for h in heads]
    kh = [qkv[:, 256 + 64 * h:256 + 64 * h + 64] for h in heads]
    vh = [qkv[:, 512 + 64 * h:512 + 64 * h + 64] for h in heads]
    qn = [q * lax.rsqrt(jnp.sum(q * q, axis=-1, keepdims=True) + EPS) * 0.125 for q in qh]
    kn = [k * lax.rsqrt(jnp.sum(k * k, axis=-1, keepdims=True) + EPS) for k in kh]
    beta = [_pick_col(beta_all, h) for h in heads]
    gcol = [_pick_col(gcs, 4 + h) for h in heads]
    grow = [_col_to_row(g, eye) for g in gcol]
    decay = [jnp.where(tril, jnp.exp(jnp.where(tril, gc - gr, 0.0)), 0.0) for gc, gr in zip(gcol, grow)]
    kbeta = [k * b for k, b in zip(kn, beta)]
    kk = [_dot_nt(op(kb), op(k)) for kb, k in zip(kbeta, kn)]
    qk = [_dot_nt(op(q), op(k)) * dc for q, k, dc in zip(qn, kn, decay)]
    known = None if known_inverses is None else [known_inverses[n * h:n * (h + 1), :] for h in heads]
    tms = _unit_lower_inverses(tuple(-jnp.where(stril, x * dc, 0.0) for x, dc in zip(kk, decay)), known)
    rhs = [jnp.concatenate([v * b, kb * jnp.exp(g)], axis=1) for v, b, kb, g in zip(vh, beta, kbeta, gcol)]
    sol = [_dot(t, x, SOLVE_PREC) for t, x in zip(tms, rhs)]
    st = [state[64 * h:64 * h + 64, :] for h in heads]
    v_new = [s_[:, :64] - _dot(op(s_[:, 64:]), op(s)) for s_, s in zip(sol, st)]
    o = [_dot(op(q * jnp.exp(g)), op(s)) + _dot(op(x), op(vn)) for q, g, s, x, vn in zip(qn, gcol, st, qk, v_new)]
    glast = [_pick_row(g, n - 1) for g in gcol]
    new_states = [s * jnp.exp(gl) + _dot_tn(op(k * jnp.exp(gl - g)), op(vn))
                  for s, gl, k, g, vn in zip(st, glast, kn, gcol, v_new)]
    o = [x * lax.rsqrt(jnp.mean(x * x, axis=-1, keepdims=True) + EPS) * nw for x in o]
    outs = [x * jax.nn.silu(z[:, 64 * h:64 * h + 64]) for h, x in zip(heads, o)]
    return jnp.concatenate(outs, axis=1), jnp.concatenate(new_states, axis=0), jnp.concatenate(tms, axis=0)


class _Exchange:
    def __init__(self, arrays, out_shape, n_sems, start, finish):
        self.arrays, self.out_shape, self.n_sems, self.start, self.finish = arrays, out_shape, n_sems, start, finish


def _hosted(exchange, n_in, n_out, grid, body):
    if exchange is None:
        return body, [], [], [], []
    k_in, k_out = len(exchange.arrays), len(exchange.out_shape)
    grid = (grid,) if isinstance(grid, int) else tuple(grid)

    def hosted_body(*refs):
        ins, refs = refs[:n_in + k_in], refs[n_in + k_in:]
        outs, scratch = refs[:n_out + k_out], refs[n_out + k_out:]
        sems = scratch[-2:]
        first, last = True, True
        for axis, steps in enumerate(grid):
            first = first & (pl.program_id(axis) == 0)
            last = last & (pl.program_id(axis) == steps - 1)

        @pl.when(first)
        def _():
            exchange.start(ins[n_in:], outs[n_out:], sems)

        body(*ins[:n_in], *outs[:n_out], *scratch[:-2])

        @pl.when(last)
        def _():
            exchange.finish(ins[n_in:], outs[n_out:], sems)

    return hosted_body, [ANY] * k_in, [ANY] * k_out, list(exchange.out_shape), _sem_pairs(exchange.n_sems)


def _gdn_fwd(proj, qkv, dtb, alog, nw, name, exchange=None):
    t = proj.shape[0]
    rows = min(GDN_CHUNK, t)
    nc = t // rows

    def body(q_ref, z_ref, ba_ref, dtb_ref, al_ref, nw_ref, o_ref, st_ref, inv_ref, state):
        @pl.when(pl.program_id(0) == 0)
        def _():
            state[...] = jnp.zeros_like(state)

        st_ref[0] = state[...]
        o, ns, tms = _gdn_chunk(q_ref[...], z_ref[...], ba_ref[...], state[...], dtb_ref[...], al_ref[...], nw_ref[...],
                                bf16_operands=True)
        o_ref[...] = o.astype(o_ref.dtype)
        inv_ref[0] = tms
        state[...] = ns

    body, x_in, x_out, x_shape, x_sems = _hosted(exchange, 6, 3, nc, body)
    vec = pl.BlockSpec((1, LANES), lambda i: (0, 0))
    per_chunk = pl.BlockSpec((1, 256, 64), lambda i: (i, 0, 0))
    return pl.pallas_call(
        body, grid=(nc,), name=name,
        in_specs=[pl.BlockSpec((rows, 768), lambda i: (i, 0)), pl.BlockSpec((rows, 256), lambda i: (i, PC_GZ // 256)),
                  pl.BlockSpec((rows, 128), lambda i: (i, PC_BA // 128)), vec, vec, pl.BlockSpec((1, 64), lambda i: (0, 0))]
        + x_in,
        out_specs=[pl.BlockSpec((rows, 256), lambda i: (i, 0)), per_chunk,
                   pl.BlockSpec((1, 4 * rows, rows), lambda i: (i, 0, 0))] + x_out,
        out_shape=[S((t, 256), BF16), S((nc, 256, 64), F32), S((nc, 4 * rows, rows), F32)] + x_shape,
        scratch_shapes=[pltpu.VMEM((256, 64), F32)] + x_sems,
        compiler_params=_params(("arbitrary",)),
    )(qkv, proj, proj, dtb, alog, nw, *([] if exchange is None else exchange.arrays))


def _gdn_bwd(proj, qkv, states, inverses, dtb, alog, nw, dcat, dcol0, name, exchange=None):
    t = proj.shape[0]
    rows = min(GDN_CHUNK, t)
    nc = t // rows
    db = dcol0 // 256

    def body(q_ref, z_ref, ba_ref, st_ref, inv_ref, dtb_ref, al_ref, nw_ref, do_ref,
             dq_ref, dz_ref, dba_ref, gdtb_ref, gal_ref, gnw_ref, dstate):
        @pl.when(pl.program_id(0) == 0)
        def _():
            dstate[...] = jnp.zeros_like(dstate)
            gdtb_ref[...] = jnp.zeros_like(gdtb_ref)
            gal_ref[...] = jnp.zeros_like(gal_ref)
            gnw_ref[...] = jnp.zeros_like(gnw_ref)

        def chunk(*operands):
            return _gdn_chunk(*operands, known_inverses=inv_ref[0])[:2]

        _, vjp = jax.vjp(chunk, q_ref[...], z_ref[...], ba_ref[...], st_ref[0], dtb_ref[...], al_ref[...], nw_ref[...])
        dq, dz, dba, dst, gdtb, gal, gnw = vjp((do_ref[...], dstate[...]))
        dq_ref[...] = dq
        dz_ref[...] = dz.astype(dz_ref.dtype)
        dba_ref[...] = dba.astype(dba_ref.dtype)
        dstate[...] = dst
        gdtb_ref[...] += gdtb
        gal_ref[...] += gal
        gnw_ref[...] += gnw

    body, x_in, x_out, x_shape, x_sems = _hosted(exchange, 9, 6, nc, body)
    rev = lambda i: nc - 1 - i
    vec = pl.BlockSpec((1, LANES), lambda i: (0, 0))
    vec64 = pl.BlockSpec((1, 64), lambda i: (0, 0))
    per_chunk = pl.BlockSpec((1, 256, 64), lambda i: (rev(i), 0, 0))
    return pl.pallas_call(
        body, grid=(nc,), name=name,
        in_specs=[pl.BlockSpec((rows, 768), lambda i: (rev(i), 0)),
                  pl.BlockSpec((rows, 256), lambda i: (rev(i), PC_GZ // 256)),
                  pl.BlockSpec((rows, 128), lambda i: (rev(i), PC_BA // 128)),
                  per_chunk, pl.BlockSpec((1, 4 * rows, rows), lambda i: (rev(i), 0, 0)), vec, vec, vec64,
                  pl.BlockSpec((rows, 256), lambda i: (rev(i), db))] + x_in,
        out_specs=[pl.BlockSpec((rows, 768), lambda i: (rev(i), 0)), pl.BlockSpec((rows, 256), lambda i: (rev(i), 0)),
                   pl.BlockSpec((rows, 128), lambda i: (rev(i), 0)), vec, vec, vec64] + x_out,
        out_shape=[S((t, 768), F32), S((t, 256), BF16), S((t, 128), BF16), S((1, LANES), F32), S((1, LANES), F32),
                   S((1, 64), F32)] + x_shape,
        scratch_shapes=[pltpu.VMEM((256, 64), F32)] + x_sems,
        compiler_params=_params(("arbitrary",)),
    )(qkv, proj, proj, states, inverses, dtb, alog, nw, dcat, *([] if exchange is None else exchange.arrays))


def _pad_cols(w):
    z = jnp.zeros((w.shape[0], 120), w.dtype)
    return jnp.concatenate([w[:, 2056:2824], w[:, 2824:3080], w[:, 1024:2048], w[:, 0:512], w[:, 512:1024],
                            w[:, 2048:2056], z, w[:, 3080:3088], z], axis=1)


def _unpad_cols(g):
    return jnp.concatenate([g[:, PC_ATT:PC_ATT + 512], g[:, PC_SZ:PC_SZ + 512], g[:, PC_XBC:PC_XBC + 1024],
                            g[:, PC_DT:PC_DT + 8], g[:, PC_GQKV:PC_GQKV + 768], g[:, PC_GZ:PC_GZ + 256],
                            g[:, PC_BA:PC_BA + 8]], axis=1)


def _vec128(v, at=0):
    return jnp.zeros((1, LANES), F32).at[0, at:at + v.shape[0]].set(v)


def _in_weight(gathered):
    return _pad_cols(jnp.concatenate([gathered[k].reshape(D_MODEL, -1) for k in range(N_CHIPS)], axis=1))


def _matmul_weights(w_in, gathered):
    rows = lambda name: gathered[name].reshape(-1, gathered[name].shape[-1])
    w_out = rows("w_out")
    return dict(
        w_in=w_in,
        w_out=jnp.concatenate([w_out[256:768], w_out[0:256], w_out[768:1024]], axis=0),
        w_gu=_interleave_gu(rows("ffn_w_gate"), rows("ffn_w_up")),
        w_down=rows("ffn_w_down"))


def _small_operands(w, l):
    return dict(
        pre_mix=w["pre_mix_norm"][l][None], post_mix=w["post_mix_norm"][l][None],
        pre_ffn=w["pre_ffn_norm"][l][None], post_ffn=w["post_ffn_norm"][l][None],
        sinks=_vec128(w["attn_sinks"][l]),
        s_cw=w["ssd_conv_w"][l], s_cb=w["ssd_conv_b"][l][None],
        s_dtb=_vec128(w["ssd_dt_bias"][l]), s_alog=_vec128(w["ssd_A_log"][l]), s_d=_vec128(w["ssd_D"][l]),
        s_nw=w["ssd_norm_w"][l][None],
        g_cw=w["gdn_conv_w"][l], g_cb=jnp.zeros((1, 768), F32),
        g_dtb=_vec128(w["gdn_dt_bias"][l], 4), g_alog=_vec128(w["gdn_A_log"][l], 4), g_nw=w["gdn_norm_w"][l][None],
    )


RAW_GRADS = ("w_in_pad", "w_out_cat", "w_gu", "ffn_w_down")
DW_ROWS = 4096


def _local_step(x, target, lw, first_gather, gathers, matmul_weights, reducer):
    saved, landed = [], {}
    xin = x
    h, first_landed = _prenorm(x, lw[0]["pre_mix"], "prenorm0", first_gather)
    w_in0 = _in_weight(first_landed)
    for l in range(DEPTH):
        p = lw[l]
        carry = (lambda kind: gathers[kind]) if l == 0 else (lambda kind: None)
        proj = _mm_nn(h, w_in0 if l == 0 else p["w_in"], 512, PC_TOT, F32, f"inproj{l}")
        xbc = _conv_fwd(proj, PC_XBC, 1024, p["s_cw"], p["s_cb"], f"ssd_conv{l}")
        gqkv = _conv_fwd(proj, PC_GQKV, 768, p["g_cw"], p["g_cb"], f"gdn_conv{l}")
        att = _swa_fwd(proj, p["sinks"], f"swa{l}")
        ssd, s_states, *landed_s = _ssd_fwd(proj, xbc, p["s_dtb"], p["s_alog"], p["s_d"], p["s_nw"], f"ssd{l}",
                                            carry("ssd"))
        gdn, g_states, g_inv, *landed_g = _gdn_fwd(proj, gqkv, p["g_dtb"], p["g_alog"], p["g_nw"], f"gdn{l}",
                                                   carry("gdn"))
        if l == 0:
            landed.update(ssd=landed_s, gdn=landed_g)
            p.update(matmul_weights(0, landed), w_in=w_in0)
        cat = jnp.concatenate([ssd, att, gdn], axis=1)
        mix, x1, h2 = _mm_resid_norm(cat, p["w_out"], xin, p["post_mix"], p["pre_ffn"], f"outproj{l}")
        gu, act, *landed_u = _ffn_up(h2, p["w_gu"], f"ffn_gu{l}", exchange=carry("ffn_gu"))
        if l == 0:
            f, x_next, h_next, *landed_d = _mm_resid_norm(act, p["w_down"], x1, p["post_ffn"], lw[1]["pre_mix"],
                                                          f"ffn_down{l}", carry("ffn_down"))
            landed.update(ffn_gu=landed_u, ffn_down=landed_d)
            lw[1].update(matmul_weights(1, landed))
        else:
            f = _mm_nn(act, p["w_down"], 512, 1024, F32, f"ffn_down{l}")
        saved.append(dict(xin=xin, h=h, proj=proj, xbc=xbc, gqkv=gqkv, s_states=s_states, g_states=g_states, g_inv=g_inv,
                          cat=cat, mix=mix, x1=x1, h2=h2, gu=gu, act=act, f=f))
        if l == 0:
            xin, h = x_next, h_next

    g = {k: [None] * DEPTH for k in SMALL + CONV + RAW_GRADS}
    last = saved[-1]
    d_x2, d_f, loss_part, g["post_ffn_norm"][DEPTH - 1] = _resid_loss(
        last["x1"], last["f"], lw[-1]["post_ffn"], target, "loss")
    early, late = [], []
    for l in reversed(range(DEPTH)):
        p, s = lw[l], saved[l]
        d_gu = _ffn_down_bwd(d_f, p["w_down"], s["gu"], f"d_act{l}")
        g["ffn_w_down"][l] = _mm_tn(s["act"], d_f, FF_HALF, 512, DW_ROWS, f"dw_down{l}")
        d_h2 = _mm_nn(d_gu, p["w_gu"], 512, 1024, F32, f"d_h2{l}")
        g["w_gu"][l] = _mm_tn(d_gu, s["h2"], FF_HALF, 512, DW_ROWS, f"dw_gu{l}")
        d_x1, d_mix, g["pre_ffn_norm"][l], g["post_mix_norm"][l] = _resid_norm_bwd(
            s["x1"], s["mix"], d_x2, d_h2, p["post_mix"], p["pre_ffn"], f"d_postmix{l}")
        d_cat = _mm_nt(d_mix, p["w_out"], 512, 1024, F32, f"d_cat{l}")
        g["w_out_cat"][l] = _mm_tn(s["cat"], d_mix, 512, 1024, DW_ROWS, f"dw_out{l}")
        d_q, d_k, d_v, g_sinks = _swa_bwd(s["proj"], p["sinks"], d_cat, 512, f"d_swa{l}")
        d_sz, d_xbc, d_dt, g_dtb, g_alog, g_d, g["ssd_norm_w"][l], *siblings = _ssd_bwd(
            s["proj"], s["xbc"], s["s_states"], p["s_dtb"], p["s_alog"], p["s_d"], p["s_nw"], d_cat, 0, f"d_ssd{l}",
            reducer.exchange(g) if l == 0 else None)
        d_gq, d_gz, d_ba, gg_dtb, gg_alog, g["gdn_norm_w"][l], *landed_b = _gdn_bwd(
            s["proj"], s["gqkv"], s["g_states"], s["g_inv"], p["g_dtb"], p["g_alog"], p["g_nw"], d_cat, 768, f"d_gdn{l}",
            reducer.scatter(siblings) if l == 0 else None)
        if l == 0:
            early = landed_b
        d_xbc_raw, g["ssd_conv_w"][l], g["ssd_conv_b"][l] = _conv_bwd(
            s["proj"], PC_XBC, 1024, p["s_cw"], p["s_cb"], d_xbc, f"d_ssd_conv{l}")
        d_gq_raw, g["gdn_conv_w"][l], _ = _conv_bwd(s["proj"], PC_GQKV, 768, p["g_cw"], p["g_cb"], d_gq, f"d_gdn_conv{l}")
        d_proj = jnp.concatenate([d_gq_raw, d_gz, d_xbc_raw, d_q, d_k, d_v, d_sz, d_dt, d_ba], axis=1)
        g["w_in_pad"][l] = _mm_tn(s["h"], d_proj, 512, PC_TOT // 2, DW_ROWS, f"dw_in{l}")
        if l == 0:
            d_h, *late = _mm_nt(d_proj, p["w_in"], 512, 1024, F32, f"d_h{l}", reducer.late(g))
        else:
            d_h = _mm_nt(d_proj, p["w_in"], 512, 1024, F32, f"d_h{l}")
        g["attn_sinks"][l] = g_sinks[0, :4]
        g["ssd_dt_bias"][l], g["ssd_A_log"][l], g["ssd_D"][l] = g_dtb[0, :8], g_alog[0, :8], g_d[0, :8]
        g["gdn_dt_bias"][l], g["gdn_A_log"][l] = gg_dtb[0, 4:8], gg_alog[0, 4:8]
        if l > 0:
            sp = saved[l - 1]
            d_x2, d_f, g["pre_mix_norm"][l], g["post_ffn_norm"][l - 1] = _resid_norm_bwd(
                s["xin"], sp["f"], d_x1, d_h, lw[l - 1]["post_ffn"], p["pre_mix"], f"d_postffn{l - 1}")
        else:
            grad_x, g["pre_mix_norm"][0] = _prenorm_bwd(s["xin"], d_x1, d_h, p["pre_mix"], "d_prenorm0")

    small = {k: jnp.stack([a.reshape(-1) for a in g[k]], axis=0) for k in SMALL + CONV}
    return loss_part, grad_x, small, {k: g[k] for k in RAW_GRADS}, early, late


BIG = ("w_in", "w_out", "ffn_w_gate", "ffn_w_up", "ffn_w_down")
CONV = ("ssd_conv_w", "gdn_conv_w")
TRANSPOSED = ("ffn_w_gate", "ffn_w_up")
SMALL = ("pre_mix_norm", "post_mix_norm", "pre_ffn_norm", "post_ffn_norm", "attn_sinks", "ssd_conv_b", "ssd_dt_bias",
         "ssd_A_log", "ssd_D", "ssd_norm_w", "gdn_dt_bias", "gdn_A_log", "gdn_norm_w")


def _row_tile(rows, cap):
    best = rows
    for t in range(8, min(cap, rows) + 1, 8):
        if rows % t == 0:
            best = t
    return best


SMALL_UNIT = 8 * LANES


def _pack_small(vals):
    rows = []
    for a in vals:
        f = a.reshape(-1)
        pad = -f.shape[0] % SMALL_UNIT
        rows.append(jnp.concatenate([f, jnp.zeros((pad,), F32)]).reshape(-1, LANES))
    return jnp.concatenate(rows, axis=0)


def _unpack_small(mat, shapes):
    out, r = [], 0
    for shp in shapes:
        n = math.prod(shp)
        nr = -(-n // SMALL_UNIT) * 8
        out.append(mat[r:r + nr].reshape(-1)[:n].reshape(shp))
        r += nr
    return out


def _place():
    x, y, c = lax.axis_index("x"), lax.axis_index("y"), lax.axis_index("c")
    chips = [(1 - x, y), (x, 1 - y), (1 - x, 1 - y)]
    return x, y, c, chips


ANY = pl.BlockSpec(memory_space=pl.ANY)


def _remote(src, dst, sems, k, to):
    send_sems, recv_sems = sems
    return pltpu.make_async_remote_copy(src_ref=src, dst_ref=dst, send_sem=send_sems.at[k], recv_sem=recv_sems.at[k],
                                        device_id=to, device_id_type=MESH)


def _sem_pairs(n):
    return [pltpu.SemaphoreType.DMA((n,)), pltpu.SemaphoreType.DMA((n,))]


def _run_exchange(exchange, name):
    k = len(exchange.arrays)

    def body(*refs):
        ins, outs, sems = refs[:k], refs[k:-2], refs[-2:]
        exchange.start(ins, outs, sems)
        exchange.finish(ins, outs, sems)

    return pl.pallas_call(
        body, name=name, in_specs=[ANY] * k, out_specs=[ANY] * len(exchange.out_shape),
        out_shape=list(exchange.out_shape), scratch_shapes=_sem_pairs(exchange.n_sems),
    )(*exchange.arrays)


def _gather_exchange(shards):
    n = len(shards)

    def sends(s_refs, g_refs, sems):
        x, y, c, chips = _place()
        me = 2 * x + y
        over_ici = [_remote(s_refs[i].at[c], g_refs[i].at[me, c], sems, 7 * i + j, (px, py, c))
                    for i in range(n) for j, (px, py) in enumerate(chips)]
        return over_ici + [_remote(s_refs[i], g_refs[i].at[me], sems, 7 * i + 6, (x, y, 1 - c)) for i in range(n)]

    def start(s_refs, g_refs, sems):
        for cp in sends(s_refs, g_refs, sems):
            cp.start()

    def finish(s_refs, g_refs, sems):
        x, y, c, chips = _place()
        sib = (x, y, 1 - c)
        passed = []
        for j, (px, py) in enumerate(chips):
            for i in range(n):
                landed = g_refs[i].at[2 * px + py, c]
                _remote(landed, landed, sems, 7 * i + j, (px, py, c)).wait_recv()
                fw = _remote(landed, landed, sems, 7 * i + 3 + j, sib)
                fw.start()
                passed.append(fw)
        for j, (px, py) in enumerate(chips):
            for i in range(n):
                landed = g_refs[i].at[2 * px + py, 1 - c]
                _remote(landed, landed, sems, 7 * i + 3 + j, sib).wait_recv()
        for i in range(n):
            mine = g_refs[i].at[2 * x + y]
            _remote(mine, mine, sems, 7 * i + 6, sib).wait_recv()
        for cp in sends(s_refs, g_refs, sems) + passed:
            cp.wait_send()

    return _Exchange(shards, [S((N_CHIPS,) + a.shape, a.dtype) for a in shards], 7 * n, start, finish)


def _halves_exchange(pieces):
    n = len(pieces)
    ds = [a for a, _ in pieces]

    def copies(d_refs, t_refs, sems):
        x, y, c, _ = _place()
        return [_remote(d_refs[i].at[:, pieces[i][1], :, 1 - c], t_refs[i], sems, i, (x, y, 1 - c)) for i in range(n)]

    def start(d_refs, t_refs, sems):
        for cp in copies(d_refs, t_refs, sems):
            cp.start()

    def finish(d_refs, t_refs, sems):
        for cp in copies(d_refs, t_refs, sems):
            cp.wait()

    return _Exchange(ds, [S((2, 2) + a.shape[4:], a.dtype) for a in ds], n, start, finish)


def _scatter_exchange(ps):
    n = len(ps)

    def copies(p_refs, u_refs, sems):
        x, y, c, chips = _place()
        return [_remote(p_refs[i].at[2 * px + py], u_refs[i].at[j], sems, 3 * i + j, (px, py, c))
                for j, (px, py) in enumerate(chips) for i in range(n)]

    def start(p_refs, u_refs, sems):
        for cp in copies(p_refs, u_refs, sems):
            cp.start()

    def finish(p_refs, u_refs, sems):
        for cp in copies(p_refs, u_refs, sems):
            cp.wait()

    return _Exchange(ps, [S((3,) + a.shape[1:], a.dtype) for a in ps], 3 * n, start, finish)


def _join_halves(qs, name):
    n = len(qs)

    def body(*refs):
        o_refs, sems = refs[n:2 * n], refs[2 * n:]
        x, y, c, _ = _place()
        cps = [_remote(o_refs[i].at[:, c], o_refs[i].at[:, c], sems, i, (x, y, 1 - c)) for i in range(n)]
        for cp in cps:
            cp.start()
        for i in range(n):
            other = o_refs[i].at[:, 1 - c]
            _remote(other, other, sems, i, (x, y, 1 - c)).wait_recv()
        for cp in cps:
            cp.wait_send()

    return pl.pallas_call(
        body, name=name, in_specs=[ANY] * n, out_specs=[ANY] * n,
        out_shape=[S(a.shape, a.dtype) for a in qs], input_output_aliases={i: i for i in range(n)},
        scratch_shapes=_sem_pairs(n),
    )(*qs)


def _gather_small(v, name):
    def body(v_ref, o_ref, send_sems, recv_sems, local_sem):
        x, y, c, _ = _place()
        me = 4 * x + 2 * y + c
        mine = pltpu.make_async_copy(v_ref, o_ref.at[me], local_sem)
        mine.start()
        cps = []
        for k in range(1, N_DEV):
            fx, fy, fc = (k >> 2) & 1, (k >> 1) & 1, k & 1
            peer = (x ^ fx, y ^ fy, c ^ fc)
            cps.append(pltpu.make_async_remote_copy(
                src_ref=v_ref, dst_ref=o_ref.at[me], send_sem=send_sems.at[k - 1], recv_sem=recv_sems.at[k - 1],
                device_id=peer, device_id_type=MESH))
        for cp in cps:
            cp.start()
        for k in range(1, N_DEV):
            fx, fy, fc = (k >> 2) & 1, (k >> 1) & 1, k & 1
            dst = o_ref.at[4 * (x ^ fx) + 2 * (y ^ fy) + (c ^ fc)]
            pltpu.make_async_remote_copy(src_ref=dst, dst_ref=dst, send_sem=send_sems.at[k - 1],
                                         recv_sem=recv_sems.at[k - 1], device_id=(x, y, c),
                                         device_id_type=MESH).wait_recv()
        for cp in cps:
            cp.wait_send()
        mine.wait()

    return pl.pallas_call(
        body, name=name, in_specs=[ANY], out_specs=ANY, out_shape=S((N_DEV,) + v.shape, F32),
        scratch_shapes=[pltpu.SemaphoreType.DMA((N_DEV - 1,)), pltpu.SemaphoreType.DMA((N_DEV - 1,)),
                        pltpu.SemaphoreType.DMA],
    )(v)


def _sum_leading(a, name):
    n, rows, cols = a.shape
    tm = _row_tile(rows, 640)

    def body(a_ref, o_ref):
        acc = a_ref[0]
        for k in range(1, n):
            acc = acc + a_ref[k]
        o_ref[...] = acc

    return pl.pallas_call(
        body, grid=(rows // tm,), name=name, in_specs=[pl.BlockSpec((n, tm, cols), lambda i: (0, i, 0))],
        out_specs=pl.BlockSpec((tm, cols), lambda i: (i, 0)), out_shape=S((rows, cols), F32),
        compiler_params=_params(("arbitrary",)),
    )(a)


def _add_sibling(place, piece, b, name):
    a, pick = piece
    n, hr, cols = b.shape
    tm = _row_tile(hr, 512)

    def body(place_ref, a_ref, b_ref, o_ref, o16_ref):
        tot = a_ref[0, 0, 0] + b_ref[...]
        o16_ref[...] = tot.astype(BF16)

        @pl.when(pl.program_id(1) == place_ref[1])
        def _():
            o_ref[...] = tot[0]

    spec = pl.BlockSpec((1, tm, cols), lambda i, k, pr: (k, i, 0))
    return pl.pallas_call(
        body, name=name, out_shape=[S((hr, cols), F32), S((n, hr, cols), BF16)],
        grid_spec=pltpu.PrefetchScalarGridSpec(
            num_scalar_prefetch=1, grid=(hr // tm, n),
            in_specs=[pl.BlockSpec((1, 1, 1, 1, tm, cols), lambda i, k, pr: (k // 2, pick, k % 2, pr[0], i, 0)), spec],
            out_specs=[pl.BlockSpec((tm, cols), lambda i, k, pr: (i, 0)), spec]),
        compiler_params=_params(("arbitrary", "arbitrary")),
    )(place, a, b)


def _add_chips(place, sums, others, layer, into, name):
    hr, cols = sums.shape
    tm = _row_tile(hr, 512)

    def body(place_ref, m_ref, o_ref, *rest):
        acc = m_ref[...]
        for k in range(others.shape[0]):
            acc = acc + o_ref[k].astype(F32)
        rest[-1][0, 0] = acc

    kept = [] if into is None else [into]
    return pl.pallas_call(
        body, name=name, out_shape=S((DEPTH, 2, hr, cols), F32),
        grid_spec=pltpu.PrefetchScalarGridSpec(
            num_scalar_prefetch=1, grid=(hr // tm,),
            in_specs=[pl.BlockSpec((tm, cols), lambda i, pr: (i, 0)),
                      pl.BlockSpec((others.shape[0], tm, cols), lambda i, pr: (0, i, 0))] + [ANY] * len(kept),
            out_specs=pl.BlockSpec((1, 1, tm, cols), lambda i, pr: (layer, pr[0], i, 0))),
        input_output_aliases={3: 0} if kept else {},
        compiler_params=_params(("arbitrary",)),
    )(place, sums, others, *kept)


def _adamw(wt, g, m, v, name):
    shape = wt.shape
    cols = shape[-1]
    rows = math.prod(shape[:-1])
    tm = rows
    for cand in (512, 256, 128, 64, 32, 16, 8):
        if rows % cand == 0:
            tm = cand
            break
    c1 = 1.0 - ADAM_B1 ** ADAM_STEP
    c2 = 1.0 - ADAM_B2 ** ADAM_STEP

    def body(w_ref, g_ref, m_ref, v_ref, d_ref, nm_ref, nv_ref):
        gv = g_ref[...]
        nm = ADAM_B1 * m_ref[...] + (1.0 - ADAM_B1) * gv
        nv = ADAM_B2 * v_ref[...] + (1.0 - ADAM_B2) * (gv * gv)
        d_ref[...] = -ADAM_LR * ((nm / c1) / (jnp.sqrt(nv / c2) + ADAM_EPS) + ADAM_WD * w_ref[...])
        nm_ref[...] = nm
        nv_ref[...] = nv

    spec = pl.BlockSpec((tm, cols), lambda i: (i, 0))
    outs = pl.pallas_call(
        body, grid=(rows // tm,), name=name, in_specs=[spec] * 4, out_specs=[spec] * 3,
        out_shape=[S((rows, cols), F32)] * 3, compiler_params=_params(("arbitrary",)),
    )(*[a.reshape(rows, cols) for a in (wt, g, m, v)])
    return [o.reshape(shape) for o in outs]


WEIGHTS = ('pre_mix_norm', 'post_mix_norm', 'pre_ffn_norm', 'post_ffn_norm', 'w_in', 'w_out', 'attn_sinks', 'ssd_conv_w',
           'ssd_conv_b', 'ssd_dt_bias', 'ssd_A_log', 'ssd_D', 'ssd_norm_w', 'gdn_conv_w', 'gdn_dt_bias', 'gdn_A_log',
           'gdn_norm_w', 'ffn_w_gate', 'ffn_w_up', 'ffn_w_down')


def _chip_piece(i, raw, shape):
    half = (2, shape[1] // 2, shape[2])
    if i in (2, 3):
        return raw["w_gu"].reshape((2, 2, 2) + half), i - 2
    if i == 0:
        g, width = _unpad_cols(raw["w_in_pad"]), shape[2]
        g = jnp.stack([g[:, k * width:(k + 1) * width] for k in range(N_CHIPS)])
    elif i == 1:
        g = raw["w_out_cat"]
        g = jnp.concatenate([g[512:768], g[0:512], g[768:1024]], axis=0)
    else:
        g = raw["ffn_w_down"]
    return g.reshape((2, 1, 2) + half), 0


def _step(x, target, wts, ms, vs):
    chip = 2 * lax.axis_index("x") + lax.axis_index("y")
    place = jnp.stack([lax.axis_index("c"), chip]).astype(jnp.int32)
    big_names = list(BIG)
    flip = lambda k, a: jnp.swapaxes(a, 1, 2) if k in TRANSPOSED else a
    wts, ms, vs = ({k: flip(k, a) for k, a in d.items()} for d in (wts, ms, vs))
    big_shapes = [wts[k].shape for k in big_names]
    halves = lambda a: a.reshape((2, a.shape[0] // 2) + a.shape[1:])

    shard = lambda name, l: halves(wts[name][l].astype(BF16))
    first_gather = _gather_exchange([shard("w_in", 0)])
    carried_by = {"ssd": [("w_out", 0), ("ffn_w_gate", 0)],
                  "gdn": [("ffn_w_up", 0), ("ffn_w_down", 0), ("w_in", 1), ("w_out", 1)],
                  "ffn_gu": [("ffn_w_gate", 1), ("ffn_w_up", 1)],
                  "ffn_down": [("ffn_w_down", 1)]}
    gathers = {kind: _gather_exchange([shard(*key) for key in keys]) for kind, keys in carried_by.items()}

    def matmul_weights(l, landed):
        gathered = {}
        for kind, keys in carried_by.items():
            for key, got in zip(keys, landed.get(kind, [])):
                if key[1] == l:
                    gathered[key[0]] = got.reshape((N_CHIPS,) + wts[key[0]].shape[1:])
        return _matmul_weights(None if l == 0 else _in_weight(gathered["w_in"]), gathered)

    conv = _gather_small(_pack_small([wts[k] for k in CONV]), "gather_conv_weights")
    conv = [_unpack_small(conv[2 * k], [wts[n].shape for n in CONV]) for k in range(N_CHIPS)]
    w_all = dict(wts)
    for i, n in enumerate(CONV):
        w_all[n] = jnp.concatenate([conv[k][i] for k in range(N_CHIPS)], axis=2)

    early_keys = [(DEPTH - 1, 0)] + [(l, i) for l in reversed(range(DEPTH)) for i in range(1, len(BIG))]
    late_keys = [(l, 0) for l in range(DEPTH - 1)]

    def pieces_of(keys, g):
        return [_chip_piece(i, {k: g[k][l] for k in RAW_GRADS}, big_shapes[i]) for l, i in keys]

    def add_siblings(tag, pieces, siblings):
        return [_add_sibling(place, p, t.reshape((N_CHIPS,) + t.shape[2:]), f"add_sibling_{tag}{n}")
                for n, (p, t) in enumerate(zip(pieces, siblings))]

    class Reducer:
        pieces, sums, late_sums = [], [], []

        def exchange(self, g):
            self.pieces = pieces_of(early_keys, g)
            return _halves_exchange(self.pieces)

        def scatter(self, siblings):
            self.sums = add_siblings("early", self.pieces, siblings)
            return _scatter_exchange([s16 for _, s16 in self.sums])

        def late(self, g):
            pieces = pieces_of(late_keys, g)
            siblings = _run_exchange(_halves_exchange(pieces), "exchange_halves_late")
            self.late_sums = add_siblings("late", pieces, siblings)
            return _scatter_exchange([s16 for _, s16 in self.late_sums])

    reducer = Reducer()

    loss_part, grad_x, small_g, raw, early, late = _local_step(
        x[0], target[0], [_small_operands(w_all, l) for l in range(DEPTH)], first_gather, gathers, matmul_weights, reducer)

    reduced = {}
    for tag, keys, sums, others in (("early", early_keys, reducer.sums, early), ("late", late_keys, reducer.late_sums, late)):
        for n, ((l, i), (s32, _), o) in enumerate(zip(keys, sums, others)):
            reduced[i] = _add_chips(place, s32, o, l, reduced.get(i), f"add_chips_{tag}{n}")
    joined = _join_halves([reduced[i] for i in range(len(BIG))], "join_halves")
    g_all = {k: q.reshape(shp) for k, q, shp in zip(big_names, joined, big_shapes)}

    names = SMALL + CONV
    packed = _pack_small([small_g[k] for k in names] + [loss_part])
    small_sum = _sum_leading(_gather_small(packed, "gather_small_grads"), "add_small")
    vals = _unpack_small(small_sum, [small_g[k].shape for k in names] + [(1, LANES)])
    loss = vals[-1][0, 0]
    for k, v in zip(names, vals[:-1]):
        if k in CONV:
            width = wts[k].shape[2]
            v = lax.dynamic_slice_in_dim(v.reshape(DEPTH, 4, -1), chip * width, width, axis=2)
        g_all[k] = v.reshape(wts[k].shape)

    shapes = [wts[k].shape for k in names]
    d_s, m_s, v_s = _adamw(_pack_small([wts[k] for k in names]), _pack_small([g_all[k] for k in names]),
                           _pack_small([ms[k] for k in names]), _pack_small([vs[k] for k in names]), "adamw_small")
    upd = dict(zip(names, zip(_unpack_small(d_s, shapes), _unpack_small(m_s, shapes), _unpack_small(v_s, shapes))))
    for k in big_names:
        upd[k] = _adamw(wts[k], g_all[k], ms[k], vs[k], f"adamw_{k}")
    return (loss, grad_x[None], *[flip(k, g_all[k]) for k in WEIGHTS], *[flip(k, upd[k][0]) for k in WEIGHTS],
            *[flip(k, upd[k][1]) for k in WEIGHTS], *[flip(k, upd[k][2]) for k in WEIGHTS])


def kernel(x, pre_mix_norm, post_mix_norm, pre_ffn_norm, post_ffn_norm, w_in, w_out, attn_sinks, ssd_conv_w, ssd_conv_b, ssd_dt_bias, ssd_A_log, ssd_D, ssd_norm_w, gdn_conv_w, gdn_dt_bias, gdn_A_log, gdn_norm_w, ffn_w_gate, ffn_w_up, ffn_w_down, loss_target, m_pre_mix_norm, m_post_mix_norm, m_pre_ffn_norm, m_post_ffn_norm, m_w_in, m_w_out, m_attn_sinks, m_ssd_conv_w, m_ssd_conv_b, m_ssd_dt_bias, m_ssd_A_log, m_ssd_D, m_ssd_norm_w, m_gdn_conv_w, m_gdn_dt_bias, m_gdn_A_log, m_gdn_norm_w, m_ffn_w_gate, m_ffn_w_up, m_ffn_w_down, v_pre_mix_norm, v_post_mix_norm, v_pre_ffn_norm, v_post_ffn_norm, v_w_in, v_w_out, v_attn_sinks, v_ssd_conv_w, v_ssd_conv_b, v_ssd_dt_bias, v_ssd_A_log, v_ssd_D, v_ssd_norm_w, v_gdn_conv_w, v_gdn_dt_bias, v_gdn_A_log, v_gdn_norm_w, v_ffn_w_gate, v_ffn_w_up, v_ffn_w_down):
    wts = dict(zip(WEIGHTS, (pre_mix_norm, post_mix_norm, pre_ffn_norm, post_ffn_norm, w_in, w_out, attn_sinks, ssd_conv_w, ssd_conv_b, ssd_dt_bias, ssd_A_log, ssd_D, ssd_norm_w, gdn_conv_w, gdn_dt_bias, gdn_A_log, gdn_norm_w, ffn_w_gate, ffn_w_up, ffn_w_down)))
    ms = dict(zip(WEIGHTS, (m_pre_mix_norm, m_post_mix_norm, m_pre_ffn_norm, m_post_ffn_norm, m_w_in, m_w_out, m_attn_sinks, m_ssd_conv_w, m_ssd_conv_b, m_ssd_dt_bias, m_ssd_A_log, m_ssd_D, m_ssd_norm_w, m_gdn_conv_w, m_gdn_dt_bias, m_gdn_A_log, m_gdn_norm_w, m_ffn_w_gate, m_ffn_w_up, m_ffn_w_down)))
    vs = dict(zip(WEIGHTS, (v_pre_mix_norm, v_post_mix_norm, v_pre_ffn_norm, v_post_ffn_norm, v_w_in, v_w_out, v_attn_sinks, v_ssd_conv_w, v_ssd_conv_b, v_ssd_dt_bias, v_ssd_A_log, v_ssd_D, v_ssd_norm_w, v_gdn_conv_w, v_gdn_dt_bias, v_gdn_A_log, v_gdn_norm_w, v_ffn_w_gate, v_ffn_w_up, v_ffn_w_down)))
    return _step(x, loss_target, wts, ms, vs)
```

```python
import functools
import math

import jax
import jax.numpy as jnp
from jax import lax
from jax.experimental import pallas as pl
from jax.experimental.pallas import tpu as pltpu

F32, BF16 = jnp.float32, jnp.bfloat16
HI = lax.Precision.HIGHEST
MESH = pl.DeviceIdType.MESH
S = jax.ShapeDtypeStruct

D_MODEL = 1024
DEPTH = 2
CHUNK = 64
SSD_CHUNK = 256
GDN_CHUNK = 128
EPS = 1e-6
FF = 2816
N_CHIPS = 4
N_DEV = 8
LANES = 128

VMEM_LIMIT_BYTES = 56 * 1024 * 1024

PC_GQKV, PC_GZ, PC_XBC, PC_ATT, PC_SZ, PC_DT, PC_BA, PC_TOT = 0, 768, 1024, 2048, 2560, 3072, 3200, 3328

ADAM_LR, ADAM_B1, ADAM_B2, ADAM_EPS, ADAM_WD, ADAM_STEP = 0.001, 0.9, 0.999, 1e-08, 0.01, 10

ALIBI_SLOPES = tuple(2.0 ** (-8.0 * (h + 1) / 4) for h in range(4))


def _params(sem=None, **kw):
    if sem is not None:
        kw["dimension_semantics"] = sem
    return pltpu.CompilerParams(vmem_limit_bytes=VMEM_LIMIT_BYTES, **kw)


def _dot(a, b, prec=None):
    return jnp.dot(a, b, precision=prec, preferred_element_type=F32)


def _dot_nt(a, b, prec=None):
    return lax.dot_general(a, b, (((1,), (1,)), ((), ())), precision=prec, preferred_element_type=F32)


def _dot_tn(a, b, prec=None):
    return lax.dot_general(a, b, (((0,), (0,)), ((), ())), precision=prec, preferred_element_type=F32)


def _iota2(n, m):
    return lax.broadcasted_iota(jnp.int32, (n, m), 0), lax.broadcasted_iota(jnp.int32, (n, m), 1)


def _pick_col(arr, idx):
    ci = lax.broadcasted_iota(jnp.int32, arr.shape, 1)
    return jnp.sum(jnp.where(ci == idx, arr, 0.0), axis=1, keepdims=True)


def _pick_row(arr, idx):
    ri = lax.broadcasted_iota(jnp.int32, arr.shape, 0)
    return jnp.sum(jnp.where(ri == idx, arr, 0.0), axis=0, keepdims=True)


def _col_to_row(col, eye):
    return jnp.sum(eye * col, axis=0, keepdims=True)


def _rms(x, w):
    return x * lax.rsqrt(jnp.mean(x * x, axis=-1, keepdims=True) + EPS) * w


def _mm_nn(a, b, tm, tn, out_dtype, name, exchange=None):
    m, k = a.shape
    n = b.shape[1]
    tm, tn = min(tm, m), min(tn, n)
    grid = (n // tn, m // tm)

    def body(a_ref, b_ref, o_ref):
        o_ref[...] = _dot(a_ref[...], b_ref[...]).astype(o_ref.dtype)

    body, x_in, x_out, x_shape, x_sems = _hosted(exchange, 2, 1, grid, body)
    outs = pl.pallas_call(
        body, grid=grid, name=name,
        in_specs=[pl.BlockSpec((tm, k), lambda j, i: (i, 0)), pl.BlockSpec((k, tn), lambda j, i: (0, j))] + x_in,
        out_specs=[pl.BlockSpec((tm, tn), lambda j, i: (i, j))] + x_out,
        out_shape=[S((m, n), out_dtype)] + x_shape, scratch_shapes=x_sems,
        compiler_params=_params(("arbitrary", "arbitrary")),
    )(a, b, *([] if exchange is None else exchange.arrays))
    return outs[0] if exchange is None else outs


def _mm_nt(a, b, tm, tn, out_dtype, name, exchange=None):
    m, k = a.shape
    n = b.shape[0]
    tm, tn = min(tm, m), min(tn, n)
    grid = (n // tn, m // tm)

    def body(a_ref, b_ref, o_ref):
        o_ref[...] = _dot_nt(a_ref[...], b_ref[...]).astype(o_ref.dtype)

    body, x_in, x_out, x_shape, x_sems = _hosted(exchange, 2, 1, grid, body)
    outs = pl.pallas_call(
        body, grid=grid, name=name,
        in_specs=[pl.BlockSpec((tm, k), lambda j, i: (i, 0)),
                  pl.BlockSpec((tn, k), lambda j, i: (j, 0), **({"pipeline_mode": pl.Buffered(1)} if n == tn else {}))]
        + x_in,
        out_specs=[pl.BlockSpec((tm, tn), lambda j, i: (i, j))] + x_out,
        out_shape=[S((m, n), out_dtype)] + x_shape, scratch_shapes=x_sems,
        compiler_params=_params(("arbitrary", "arbitrary")),
    )(a, b, *([] if exchange is None else exchange.arrays))
    return outs[0] if exchange is None else outs


def _mm_tn(a, b, tm, tn, tk, name):
    t, m = a.shape
    n = b.shape[1]
    tm, tn, tk = min(tm, m), min(tn, n), min(tk, t)

    def body(a_ref, b_ref, o_ref):
        part = _dot_tn(a_ref[...], b_ref[...])

        @pl.when(pl.program_id(2) == 0)
        def _():
            o_ref[...] = part

        @pl.when(pl.program_id(2) > 0)
        def _():
            o_ref[...] += part

    return pl.pallas_call(
        body, grid=(m // tm, n // tn, t // tk), name=name,
        in_specs=[pl.BlockSpec((tk, tm), lambda i, j, k: (k, i)), pl.BlockSpec((tk, tn), lambda i, j, k: (k, j))],
        out_specs=pl.BlockSpec((tm, tn), lambda i, j, k: (i, j)),
        out_shape=S((m, n), F32), compiler_params=_params(("arbitrary", "arbitrary", "arbitrary")),
    )(a, b)


def _rowcall(fn, rows, params, row_outs, acc_outs, name, tm=512, exchange=None):
    t = rows[0].shape[0]
    tm = min(tm, t)
    n_in = len(rows) + len(params)
    n_ro = len(row_outs)

    def body(*refs):
        ro, ao = fn(*[r[...] for r in refs[:n_in]])
        for ref, v in zip(refs[n_in:n_in + n_ro], ro):
            ref[...] = v.astype(ref.dtype)
        acc_refs = refs[n_in + n_ro:]
        if acc_refs:
            @pl.when(pl.program_id(0) == 0)
            def _():
                for ref, v in zip(acc_refs, ao):
                    ref[...] = v

            @pl.when(pl.program_id(0) > 0)
            def _():
                for ref, v in zip(acc_refs, ao):
                    ref[...] += v

    in_specs = [pl.BlockSpec((tm, r.shape[1]), lambda i: (i, 0)) for r in rows]
    in_specs += [pl.BlockSpec(p.shape, lambda i: (0, 0)) for p in params]
    out_specs = [pl.BlockSpec((tm, c), lambda i: (i, 0)) for c, _ in row_outs]
    out_specs += [pl.BlockSpec(shape, lambda i: (0, 0)) for shape in acc_outs]
    out_shape = [S((t, c), dt) for c, dt in row_outs] + [S(shape, F32) for shape in acc_outs]
    body, x_in, x_out, x_shape, x_sems = _hosted(exchange, n_in, len(out_shape), t // tm, body)
    return pl.pallas_call(
        body, grid=(t // tm,), name=name, in_specs=in_specs + x_in, out_specs=out_specs + x_out,
        out_shape=out_shape + x_shape, scratch_shapes=x_sems, compiler_params=_params(("arbitrary",)),
    )(*rows, *params, *([] if exchange is None else exchange.arrays))


def _prenorm(x, w, name, exchange=None):
    def fn(x, w):
        return (_rms(x, w),), ()
    return _rowcall(fn, [x], [w], [(D_MODEL, BF16)], [], name, exchange=exchange)


def _resid_norm(xin, m, w_post, w_next, name):
    def fn(xin, m, w_post, w_next):
        xo = xin + _rms(m, w_post)
        return (xo, _rms(xo, w_next)), ()
    return _rowcall(fn, [xin, m], [w_post, w_next], [(D_MODEL, F32), (D_MODEL, BF16)], [], name)


def _resid_loss(xin, m, w_post, target, name):
    def fn(xin, m, target, w_post):
        r, vjp = jax.vjp(_rms, m, w_post)
        err = xin + r - target
        dy = err * (1.0 / D_MODEL)
        dm, dw = vjp(dy)
        tot = jnp.sum(jnp.sum(err * err, axis=1, keepdims=True), axis=0, keepdims=True) * (0.5 / D_MODEL)
        lane = lax.broadcasted_iota(jnp.int32, (1, LANES), 1)
        return (dy, dm), (jnp.where(lane == 0, tot, 0.0), dw)
    return _rowcall(fn, [xin, m, target], [w_post], [(D_MODEL, F32), (D_MODEL, BF16)],
                    [(1, LANES), (1, D_MODEL)], name)


def _resid_norm_bwd(x_out, m, d_direct, dh, w_post, w_next, name):
    def fn(x_out, m, d_direct, dh, w_post, w_next):
        _, vjp_n = jax.vjp(_rms, x_out, w_next)
        dx, dwn = vjp_n(dh)
        d_total = d_direct + dx
        _, vjp_p = jax.vjp(_rms, m, w_post)
        dm, dwp = vjp_p(d_total)
        return (d_total, dm), (dwn, dwp)
    return _rowcall(fn, [x_out, m, d_direct, dh], [w_post, w_next], [(D_MODEL, F32), (D_MODEL, BF16)],
                    [(1, D_MODEL), (1, D_MODEL)], name)


def _prenorm_bwd(x, d_direct, dh, w, name):
    def fn(x, d_direct, dh, w):
        _, vjp = jax.vjp(_rms, x, w)
        dx, dw = vjp(dh)
        return (d_direct + dx,), (dw,)
    return _rowcall(fn, [x, d_direct, dh], [w], [(D_MODEL, F32)], [(1, D_MODEL)], name)


FF_HALF = FF // 2


def _interleave_gu(gate_t, up_t):
    return jnp.concatenate([gate_t[:FF_HALF], up_t[:FF_HALF], gate_t[FF_HALF:], up_t[FF_HALF:]], axis=0)


def _swiglu_pair(gu):
    n = gu.shape[1] // 2
    return jax.nn.silu(gu[:, :n]) * gu[:, n:]


def _ffn_up(h2, w_gu, name, tm=512, exchange=None):
    t, k = h2.shape
    tm = min(tm, t)
    grid = (2, t // tm)

    def body(a_ref, b_ref, gu_ref, act_ref):
        gu = _dot_nt(a_ref[...], b_ref[...])
        gu_ref[...] = gu.astype(gu_ref.dtype)
        act_ref[...] = _swiglu_pair(gu).astype(act_ref.dtype)

    body, x_in, x_out, x_shape, x_sems = _hosted(exchange, 2, 2, grid, body)
    return pl.pallas_call(
        body, grid=grid, name=name,
        in_specs=[pl.BlockSpec((tm, k), lambda j, i: (i, 0)), pl.BlockSpec((FF, k), lambda j, i: (j, 0))] + x_in,
        out_specs=[pl.BlockSpec((tm, FF), lambda j, i: (i, j)), pl.BlockSpec((tm, FF_HALF), lambda j, i: (i, j))] + x_out,
        out_shape=[S((t, 2 * FF), BF16), S((t, FF), BF16)] + x_shape, scratch_shapes=x_sems,
        compiler_params=_params(("arbitrary", "arbitrary")),
    )(h2, w_gu, *([] if exchange is None else exchange.arrays))


def _ffn_down_bwd(d_f, w_down, gu, name, tm=512):
    t, k = d_f.shape
    tm = min(tm, t)

    def body(a_ref, b_ref, gu_ref, o_ref):
        d_act = _dot_nt(a_ref[...], b_ref[...])
        _, vjp = jax.vjp(_swiglu_pair, gu_ref[...].astype(F32))
        o_ref[...] = vjp(d_act)[0].astype(o_ref.dtype)

    return pl.pallas_call(
        body, grid=(2, t // tm), name=name,
        in_specs=[pl.BlockSpec((tm, k), lambda j, i: (i, 0)), pl.BlockSpec((FF_HALF, k), lambda j, i: (j, 0)),
                  pl.BlockSpec((tm, FF), lambda j, i: (i, j))],
        out_specs=pl.BlockSpec((tm, FF), lambda j, i: (i, j)),
        out_shape=S((t, 2 * FF), BF16), compiler_params=_params(("arbitrary", "arbitrary")),
    )(d_f, w_down, gu)


def _conv_fwd(proj, col0, width, w, b, name, tm=512):
    t = proj.shape[0]
    tm = min(tm, t)
    cb = col0 // width

    def body(x_ref, w_ref, b_ref, o_ref, ext):
        @pl.when(pl.program_id(0) == 0)
        def _():
            ext[0:8, :] = jnp.zeros((8, width), F32)

        ext[8:8 + tm, :] = x_ref[...]
        y = b_ref[...] + w_ref[0:1, :] * ext[pl.ds(5, tm), :]
        for k in range(1, 4):
            y = y + w_ref[k:k + 1, :] * ext[pl.ds(5 + k, tm), :]
        o_ref[...] = jax.nn.silu(y)
        ext[0:8, :] = ext[tm:tm + 8, :]

    return pl.pallas_call(
        body, grid=(t // tm,), name=name,
        in_specs=[pl.BlockSpec((tm, width), lambda i: (i, cb)), pl.BlockSpec((4, width), lambda i: (0, 0)),
                  pl.BlockSpec((1, width), lambda i: (0, 0))],
        out_specs=pl.BlockSpec((tm, width), lambda i: (i, 0)),
        out_shape=S((t, width), F32), scratch_shapes=[pltpu.VMEM((tm + 8, width), F32)],
        compiler_params=_params(("arbitrary",)),
    )(proj, w, b)


def _conv_bwd(proj, col0, width, w, b, dact, name, tm=512):
    t = proj.shape[0]
    tm = min(tm, t)
    nb = t // tm
    cb = col0 // width
    hb = tm // 8

    def body(x_ref, halo_ref, d_ref, w_ref, b_ref, dx_ref, dw_ref, db_ref, extx, extd):
        i = pl.program_id(0)
        blk = nb - 1 - i

        @pl.when(i == 0)
        def _():
            extd[tm:tm + 8, :] = jnp.zeros((8, width), F32)
            dw_ref[...] = jnp.zeros((4, width), F32)
            db_ref[...] = jnp.zeros((1, width), F32)

        extx[0:8, :] = jnp.where(blk == 0, 0.0, halo_ref[...])
        extx[8:8 + tm, :] = x_ref[...]
        y = b_ref[...] + w_ref[0:1, :] * extx[pl.ds(5, tm), :]
        for k in range(1, 4):
            y = y + w_ref[k:k + 1, :] * extx[pl.ds(5 + k, tm), :]
        sig = jax.nn.sigmoid(y)
        dy = d_ref[...] * (sig * (1.0 + y * (1.0 - sig)))
        extd[0:tm, :] = dy
        dx = w_ref[0:1, :] * extd[pl.ds(3, tm), :]
        for k in range(1, 4):
            dx = dx + w_ref[k:k + 1, :] * extd[pl.ds(3 - k, tm), :]
        dx_ref[...] = dx.astype(dx_ref.dtype)
        for k in range(4):
            dw_ref[k:k + 1, :] += jnp.sum(dy * extx[pl.ds(5 + k, tm), :], axis=0, keepdims=True)
        db_ref[...] += jnp.sum(dy, axis=0, keepdims=True)
        extd[tm:tm + 8, :] = extd[0:8, :]

    return pl.pallas_call(
        body, grid=(nb,), name=name,
        in_specs=[pl.BlockSpec((tm, width), lambda i: (nb - 1 - i, cb)),
                  pl.BlockSpec((8, width), lambda i: (jnp.maximum((nb - 1 - i) * hb - 1, 0), cb)),
                  pl.BlockSpec((tm, width), lambda i: (nb - 1 - i, 0)),
                  pl.BlockSpec((4, width), lambda i: (0, 0)), pl.BlockSpec((1, width), lambda i: (0, 0))],
        out_specs=[pl.BlockSpec((tm, width), lambda i: (nb - 1 - i, 0)), pl.BlockSpec((4, width), lambda i: (0, 0)),
                   pl.BlockSpec((1, width), lambda i: (0, 0))],
        out_shape=[S((t, width), BF16), S((4, width), F32), S((1, width), F32)],
        scratch_shapes=[pltpu.VMEM((tm + 8, width), F32), pltpu.VMEM((tm + 8, width), F32)],
        compiler_params=_params(("arbitrary",)),
    )(proj, proj, dact, w, b)


SWA_BQ = 256
SWA_BACK = 128


def _swa_block(q, kw, vw, sinks, blk, bf16_operands=False):
    op = (lambda a: a.astype(BF16)) if bf16_operands else (lambda a: a)
    nq, nk = SWA_BQ, SWA_BQ + SWA_BACK
    r, j = _iota2(nq, nk)
    rel = r // CHUNK + 2 - j // CHUNK
    valid = (rel >= 0) & (rel <= 2) & (blk * (SWA_BQ // CHUNK) + j // CHUNK - 2 >= 0)
    dist = jnp.abs(r + SWA_BACK - j).astype(F32)
    outs = []
    for h in range(4):
        kv = h // 2
        qh = q[:, 64 * h:64 * h + 64]
        kh = kw[:, 64 * kv:64 * kv + 64]
        vh = vw[:, 64 * kv:64 * kv + 64]
        s = _dot_nt(op(qh), op(kh)) * 0.125 - ALIBI_SLOPES[h] * dist
        s = jnp.where(valid, s, -1e30)
        sink = _pick_col(sinks, h)
        m = jnp.maximum(jnp.max(s, axis=1, keepdims=True), sink)
        e = jnp.exp(s - m)
        den = jnp.sum(e, axis=1, keepdims=True) + jnp.exp(sink - m)
        outs.append(_dot(op(e / den), op(vh)))
    return jnp.concatenate(outs, axis=1)


def _swa_fwd(proj, sinks, name):
    t = proj.shape[0]
    qb, kb = PC_ATT // 256, PC_ATT // 128 + 2
    win = SWA_BQ + SWA_BACK

    def body(q_ref, k_ref, v_ref, s_ref, o_ref, kp, vp):
        i = pl.program_id(0)

        @pl.when(i == 0)
        def _():
            kp[0:SWA_BACK, :] = jnp.zeros((SWA_BACK, 128), F32)
            vp[0:SWA_BACK, :] = jnp.zeros((SWA_BACK, 128), F32)
            kp[SWA_BACK:, :] = k_ref[...]
            vp[SWA_BACK:, :] = v_ref[...]

        start = pl.multiple_of(i * SWA_BQ, SWA_BQ)
        o = _swa_block(q_ref[...], kp[pl.ds(start, win), :], vp[pl.ds(start, win), :], s_ref[...], i)
        o_ref[...] = o.astype(o_ref.dtype)

    return pl.pallas_call(
        body, grid=(t // SWA_BQ,), name=name,
        in_specs=[pl.BlockSpec((SWA_BQ, 256), lambda i: (i, qb)), pl.BlockSpec((t, 128), lambda i: (0, kb)),
                  pl.BlockSpec((t, 128), lambda i: (0, kb + 1)), pl.BlockSpec((1, LANES), lambda i: (0, 0))],
        out_specs=pl.BlockSpec((SWA_BQ, 256), lambda i: (i, 0)),
        out_shape=S((t, 256), BF16),
        scratch_shapes=[pltpu.VMEM((t + SWA_BACK, 128), F32), pltpu.VMEM((t + SWA_BACK, 128), F32)],
        compiler_params=_params(("arbitrary",)),
    )(proj, proj, proj, sinks)


def _swa_bwd(proj, sinks, dcat, dcol0, name):
    t = proj.shape[0]
    nb = t // SWA_BQ
    qb, kb = PC_ATT // 256, PC_ATT // 128 + 2
    db = dcol0 // 256
    win = SWA_BQ + SWA_BACK

    def body(q_ref, k_ref, v_ref, s_ref, do_ref, dq_ref, dk_ref, dv_ref, ds_ref, kp, vp, dkp, dvp):
        i = pl.program_id(0)

        @pl.when(i == 0)
        def _():
            kp[0:SWA_BACK, :] = jnp.zeros((SWA_BACK, 128), F32)
            vp[0:SWA_BACK, :] = jnp.zeros((SWA_BACK, 128), F32)
            kp[SWA_BACK:, :] = k_ref[...]
            vp[SWA_BACK:, :] = v_ref[...]
            dkp[...] = jnp.zeros_like(dkp)
            dvp[...] = jnp.zeros_like(dvp)
            ds_ref[...] = jnp.zeros_like(ds_ref)

        start = pl.multiple_of(i * SWA_BQ, SWA_BQ)
        _, vjp = jax.vjp(functools.partial(_swa_block, blk=i, bf16_operands=True), q_ref[...], kp[pl.ds(start, win), :],
                         vp[pl.ds(start, win), :], s_ref[...])
        dq, dkw, dvw, dsk = vjp(do_ref[...])
        dq_ref[...] = dq.astype(dq_ref.dtype)
        dkp[pl.ds(start, win), :] += dkw
        dvp[pl.ds(start, win), :] += dvw
        ds_ref[...] += dsk

        @pl.when(i == nb - 1)
        def _():
            dk_ref[...] = dkp[SWA_BACK:, :].astype(dk_ref.dtype)
            dv_ref[...] = dvp[SWA_BACK:, :].astype(dv_ref.dtype)

    return pl.pallas_call(
        body, grid=(nb,), name=name,
        in_specs=[pl.BlockSpec((SWA_BQ, 256), lambda i: (i, qb)), pl.BlockSpec((t, 128), lambda i: (0, kb)),
                  pl.BlockSpec((t, 128), lambda i: (0, kb + 1)), pl.BlockSpec((1, LANES), lambda i: (0, 0)),
                  pl.BlockSpec((SWA_BQ, 256), lambda i: (i, db))],
        out_specs=[pl.BlockSpec((SWA_BQ, 256), lambda i: (i, 0)), pl.BlockSpec((t, 128), lambda i: (0, 0)),
                   pl.BlockSpec((t, 128), lambda i: (0, 0)), pl.BlockSpec((1, LANES), lambda i: (0, 0))],
        out_shape=[S((t, 256), BF16), S((t, 128), BF16), S((t, 128), BF16), S((1, LANES), F32)],
        scratch_shapes=[pltpu.VMEM((t + SWA_BACK, 128), F32) for _ in range(4)],
        compiler_params=_params(("arbitrary",)),
    )(proj, proj, proj, sinks, dcat)


def _ssd_chunk(z, xbc, dt_raw, state, dtb, alog, dsk, nw, bf16_operands=False):
    n = z.shape[0]
    op = (lambda a: a.astype(BF16)) if bf16_operands else (lambda a: a)
    r, c = _iota2(n, n)
    tril = r >= c
    eye = (r == c).astype(F32)
    dt = jax.nn.softplus(dt_raw + dtb)
    acs = _dot(tril.astype(F32), dt * (-jnp.exp(alog)), HI)
    xs, bm, cm = xbc[:, :512], xbc[:, 512:768], xbc[:, 768:1024]
    heads = range(8)
    bg = [bm[:, 128 * g:128 * g + 128] for g in range(2)]
    cg = [cm[:, 128 * g:128 * g + 128] for g in range(2)]
    cb = [_dot_nt(op(cg[g]), op(bg[g])) for g in range(2)]
    dth = [_pick_col(dt, h) for h in heads]
    acol = [_pick_col(acs, h) for h in heads]
    arow = [_col_to_row(a, eye) for a in acol]
    lmat = [jnp.where(tril, jnp.exp(jnp.where(tril, a - b, 0.0)), 0.0) for a, b in zip(acol, arow)]
    xh = [xs[:, 64 * h:64 * h + 64] for h in heads]
    xc = [x * t for x, t in zip(xh, dth)]
    st = [state[64 * h:64 * h + 64, :] for h in heads]
    alast = [_pick_row(a, n - 1) for a in acol]
    y_in = [_dot(op(cb[h // 4] * lmat[h]), op(xc[h])) for h in heads]
    y_st = [_dot_nt(op(cg[h // 4]), op(st[h])) * jnp.exp(acol[h]) for h in heads]
    ys = [y_in[h] + y_st[h] + xh[h] * _pick_col(dsk, h) for h in heads]
    new_states = [st[h] * jnp.exp(alast[h]) + _dot_tn(op(xc[h] * jnp.exp(alast[h] - acol[h])), op(bg[h // 4]))
                  for h in heads]
    gg = jnp.concatenate(ys, axis=1) * jax.nn.silu(z)
    outs = []
    for gi in range(2):
        gv = gg[:, 256 * gi:256 * gi + 256]
        outs.append(gv * lax.rsqrt(jnp.mean(gv * gv, axis=-1, keepdims=True) + EPS))
    return jnp.concatenate(outs, axis=1) * nw, jnp.concatenate(new_states, axis=0)


def _ssd_fwd(proj, xbc, dtb, alog, dsk, nw, name, exchange=None):
    t = proj.shape[0]
    rows = min(SSD_CHUNK, t)
    nc = t // rows

    def body(z_ref, x_ref, dt_ref, dtb_ref, al_ref, d_ref, nw_ref, o_ref, st_ref, state):
        @pl.when(pl.program_id(0) == 0)
        def _():
            state[...] = jnp.zeros_like(state)

        st_ref[0] = state[...]
        o, ns = _ssd_chunk(z_ref[...], x_ref[...], dt_ref[...], state[...], dtb_ref[...], al_ref[...], d_ref[...],
                           nw_ref[...])
        o_ref[...] = o.astype(o_ref.dtype)
        state[...] = ns

    body, x_in, x_out, x_shape, x_sems = _hosted(exchange, 7, 2, nc, body)
    vec = pl.BlockSpec((1, LANES), lambda i: (0, 0))
    return pl.pallas_call(
        body, grid=(nc,), name=name,
        in_specs=[pl.BlockSpec((rows, 512), lambda i: (i, PC_SZ // 512)), pl.BlockSpec((rows, 1024), lambda i: (i, 0)),
                  pl.BlockSpec((rows, 128), lambda i: (i, PC_DT // 128)), vec, vec, vec,
                  pl.BlockSpec((1, 512), lambda i: (0, 0))] + x_in,
        out_specs=[pl.BlockSpec((rows, 512), lambda i: (i, 0)), pl.BlockSpec((1, 512, 128), lambda i: (i, 0, 0))] + x_out,
        out_shape=[S((t, 512), BF16), S((nc, 512, 128), F32)] + x_shape,
        scratch_shapes=[pltpu.VMEM((512, 128), F32)] + x_sems,
        compiler_params=_params(("arbitrary",)),
    )(proj, xbc, proj, dtb, alog, dsk, nw, *([] if exchange is None else exchange.arrays))


def _ssd_bwd(proj, xbc, states, dtb, alog, dsk, nw, dcat, dcol0, name, exchange=None):
    t = proj.shape[0]
    rows = min(SSD_CHUNK, t)
    nc = t // rows
    db = dcol0 // 512

    def body(z_ref, x_ref, dt_ref, st_ref, dtb_ref, al_ref, d_ref, nw_ref, do_ref,
             dz_ref, dx_ref, ddt_ref, gdtb_ref, gal_ref, gd_ref, gnw_ref, dstate):
        @pl.when(pl.program_id(0) == 0)
        def _():
            dstate[...] = jnp.zeros_like(dstate)
            gdtb_ref[...] = jnp.zeros_like(gdtb_ref)
            gal_ref[...] = jnp.zeros_like(gal_ref)
            gd_ref[...] = jnp.zeros_like(gd_ref)
            gnw_ref[...] = jnp.zeros_like(gnw_ref)

        _, vjp = jax.vjp(functools.partial(_ssd_chunk, bf16_operands=True), z_ref[...], x_ref[...], dt_ref[...],
                         st_ref[0], dtb_ref[...], al_ref[...], d_ref[...], nw_ref[...])
        dz, dx, ddt, dst, gdtb, gal, gd, gnw = vjp((do_ref[...], dstate[...]))
        dz_ref[...] = dz.astype(dz_ref.dtype)
        dx_ref[...] = dx
        ddt_ref[...] = ddt.astype(ddt_ref.dtype)
        dstate[...] = dst
        gdtb_ref[...] += gdtb
        gal_ref[...] += gal
        gd_ref[...] += gd
        gnw_ref[...] += gnw

    body, x_in, x_out, x_shape, x_sems = _hosted(exchange, 9, 7, nc, body)
    rev = lambda i: nc - 1 - i
    vec = pl.BlockSpec((1, LANES), lambda i: (0, 0))
    vec512 = pl.BlockSpec((1, 512), lambda i: (0, 0))
    return pl.pallas_call(
        body, grid=(nc,), name=name,
        in_specs=[pl.BlockSpec((rows, 512), lambda i: (rev(i), PC_SZ // 512)),
                  pl.BlockSpec((rows, 1024), lambda i: (rev(i), 0)),
                  pl.BlockSpec((rows, 128), lambda i: (rev(i), PC_DT // 128)),
                  pl.BlockSpec((1, 512, 128), lambda i: (rev(i), 0, 0)), vec, vec, vec, vec512,
                  pl.BlockSpec((rows, 512), lambda i: (rev(i), db))] + x_in,
        out_specs=[pl.BlockSpec((rows, 512), lambda i: (rev(i), 0)), pl.BlockSpec((rows, 1024), lambda i: (rev(i), 0)),
                   pl.BlockSpec((rows, 128), lambda i: (rev(i), 0)), vec, vec, vec, vec512] + x_out,
        out_shape=[S((t, 512), BF16), S((t, 1024), F32), S((t, 128), BF16), S((1, LANES), F32), S((1, LANES), F32),
                   S((1, LANES), F32), S((1, 512), F32)] + x_shape,
        scratch_shapes=[pltpu.VMEM((512, 128), F32)] + x_sems,
        compiler_params=_params(("arbitrary",)),
    )(proj, xbc, proj, states, dtb, alog, dsk, nw, dcat, *([] if exchange is None else exchange.arrays))


SOLVE_PREC = lax.Precision.HIGH


def _unit_lower_inverses(nas, known=None):
    def compute(ns):
        if known is not None:
            return tuple(known)
        n = ns[0].shape[0]
        r, c = _iota2(n, n)
        eye = (r == c).astype(F32)
        tm, pw = [eye + a for a in ns], list(ns)
        for _ in range(n.bit_length() - 2):
            pw = [_dot(p, p, SOLVE_PREC) for p in pw]
            tm = [t + _dot(t, p, SOLVE_PREC) for t, p in zip(tm, pw)]
        return tuple(tm)

    inv = jax.custom_vjp(compute)

    def fwd(ns):
        ts = compute(ns)
        return ts, ts

    def bwd(ts, gs):
        part = [_dot_nt(g, t, SOLVE_PREC) for g, t in zip(gs, ts)]
        return (tuple(_dot_tn(t, p, SOLVE_PREC) for t, p in zip(ts, part)),)

    inv.defvjp(fwd, bwd)
    return inv(nas)


def _gdn_chunk(qkv, z, ba, state, dtb, alog, nw, known_inverses=None, bf16_operands=False):
    n = qkv.shape[0]
    op = (lambda a: a.astype(BF16)) if bf16_operands else (lambda a: a)
    r, c = _iota2(n, n)
    tril = r >= c
    stril = r > c
    eye = (r == c).astype(F32)
    beta_all = jax.nn.sigmoid(ba)
    gcs = _dot(tril.astype(F32), -jnp.exp(alog) * jax.nn.softplus(ba + dtb), HI)
    heads = range(4)
    qh = [qkv[:, 64 * h:64 * h + 64] for h in heads]
    kh = [qkv[:, 256 + 64 * h:256 + 64 * h + 64] for h in heads]
    vh = [qkv[:, 512 + 64 * h:512 + 64 * h + 64] for h in heads]
    qn = [q * lax.rsqrt(jnp.sum(q * q, axis=-1, keepdims=True) + EPS) * 0.125 for q in qh]
    kn = [k * lax.rsqrt(jnp.sum(k * k, axis=-1, keepdims=True) + EPS) for k in kh]
    beta = [_pick_col(beta_all, h) for h in heads]
    gcol = [_pick_col(gcs, 4 + h) for h in heads]
    grow = [_col_to_row(g, eye) for g in gcol]
    decay = [jnp.where(tril, jnp.exp(jnp.where(tril, gc - gr, 0.0)), 0.0) for gc, gr in zip(gcol, grow)]
    kbeta = [k * b for k, b in zip(kn, beta)]
    kk = [_dot_nt(op(kb), op(k)) for kb, k in zip(kbeta, kn)]
    qk = [_dot_nt(op(q), op(k)) * dc for q, k, dc in zip(qn, kn, decay)]
    known = None if known_inverses is None else [known_inverses[n * h:n * (h + 1), :] for h in heads]
    tms = _unit_lower_inverses(tuple(-jnp.where(stril, x * dc, 0.0) for x, dc in zip(kk, decay)), known)
    rhs = [jnp.concatenate([v * b, kb * jnp.exp(g)], axis=1) for v, b, kb, g in zip(vh, beta, kbeta, gcol)]
    sol = [_dot(t, x, SOLVE_PREC) for t, x in zip(tms, rhs)]
    st = [state[64 * h:64 * h + 64, :] for h in heads]
    v_new = [s_[:, :64] - _dot(op(s_[:, 64:]), op(s)) for s_, s in zip(sol, st)]
    o = [_dot(op(q * jnp.exp(g)), op(s)) + _dot(op(x), op(vn)) for q, g, s, x, vn in zip(qn, gcol, st, qk, v_new)]
    glast = [_pick_row(g, n - 1) for g in gcol]
    new_states = [s * jnp.exp(gl) + _dot_tn(op(k * jnp.exp(gl - g)), op(vn))
                  for s, gl, k, g, vn in zip(st, glast, kn, gcol, v_new)]
    o = [x * lax.rsqrt(jnp.mean(x * x, axis=-1, keepdims=True) + EPS) * nw for x in o]
    outs = [x * jax.nn.silu(z[:, 64 * h:64 * h + 64]) for h, x in zip(heads, o)]
    return jnp.concatenate(outs, axis=1), jnp.concatenate(new_states, axis=0), jnp.concatenate(tms, axis=0)


class _Exchange:
    def __init__(self, arrays, out_shape, n_sems, start, finish):
        self.arrays, self.out_shape, self.n_sems, self.start, self.finish = arrays, out_shape, n_sems, start, finish


def _hosted(exchange, n_in, n_out, grid, body):
    if exchange is None:
        return body, [], [], [], []
    k_in, k_out = len(exchange.arrays), len(exchange.out_shape)
    grid = (grid,) if isinstance(grid, int) else tuple(grid)

    def hosted_body(*refs):
        ins, refs = refs[:n_in + k_in], refs[n_in + k_in:]
        outs, scratch = refs[:n_out + k_out], refs[n_out + k_out:]
        sems = scratch[-2:]
        first, last = True, True
        for axis, steps in enumerate(grid):
            first = first & (pl.program_id(axis) == 0)
            last = last & (pl.program_id(axis) == steps - 1)

        @pl.when(first)
        def _():
            exchange.start(ins[n_in:], outs[n_out:], sems)

        body(*ins[:n_in], *outs[:n_out], *scratch[:-2])

        @pl.when(last)
        def _():
            exchange.finish(ins[n_in:], outs[n_out:], sems)

    return hosted_body, [ANY] * k_in, [ANY] * k_out, list(exchange.out_shape), _sem_pairs(exchange.n_sems)


def _gdn_fwd(proj, qkv, dtb, alog, nw, name, exchange=None):
    t = proj.shape[0]
    rows = min(GDN_CHUNK, t)
    nc = t // rows

    def body(q_ref, z_ref, ba_ref, dtb_ref, al_ref, nw_ref, o_ref, st_ref, inv_ref, state):
        @pl.when(pl.program_id(0) == 0)
        def _():
            state[...] = jnp.zeros_like(state)

        st_ref[0] = state[...]
        o, ns, tms = _gdn_chunk(q_ref[...], z_ref[...], ba_ref[...], state[...], dtb_ref[...], al_ref[...], nw_ref[...],
                                bf16_operands=True)
        o_ref[...] = o.astype(o_ref.dtype)
        inv_ref[0] = tms
        state[...] = ns

    body, x_in, x_out, x_shape, x_sems = _hosted(exchange, 6, 3, nc, body)
    vec = pl.BlockSpec((1, LANES), lambda i: (0, 0))
    per_chunk = pl.BlockSpec((1, 256, 64), lambda i: (i, 0, 0))
    return pl.pallas_call(
        body, grid=(nc,), name=name,
        in_specs=[pl.BlockSpec((rows, 768), lambda i: (i, 0)), pl.BlockSpec((rows, 256), lambda i: (i, PC_GZ // 256)),
                  pl.BlockSpec((rows, 128), lambda i: (i, PC_BA // 128)), vec, vec, pl.BlockSpec((1, 64), lambda i: (0, 0))]
        + x_in,
        out_specs=[pl.BlockSpec((rows, 256), lambda i: (i, 0)), per_chunk,
                   pl.BlockSpec((1, 4 * rows, rows), lambda i: (i, 0, 0))] + x_out,
        out_shape=[S((t, 256), BF16), S((nc, 256, 64), F32), S((nc, 4 * rows, rows), F32)] + x_shape,
        scratch_shapes=[pltpu.VMEM((256, 64), F32)] + x_sems,
        compiler_params=_params(("arbitrary",)),
    )(qkv, proj, proj, dtb, alog, nw, *([] if exchange is None else exchange.arrays))


def _gdn_bwd(proj, qkv, states, inverses, dtb, alog, nw, dcat, dcol0, name, exchange=None):
    t = proj.shape[0]
    rows = min(GDN_CHUNK, t)
    nc = t // rows
    db = dcol0 // 256

    def body(q_ref, z_ref, ba_ref, st_ref, inv_ref, dtb_ref, al_ref, nw_ref, do_ref,
             dq_ref, dz_ref, dba_ref, gdtb_ref, gal_ref, gnw_ref, dstate):
        @pl.when(pl.program_id(0) == 0)
        def _():
            dstate[...] = jnp.zeros_like(dstate)
            gdtb_ref[...] = jnp.zeros_like(gdtb_ref)
            gal_ref[...] = jnp.zeros_like(gal_ref)
            gnw_ref[...] = jnp.zeros_like(gnw_ref)

        def chunk(*operands):
            return _gdn_chunk(*operands, known_inverses=inv_ref[0])[:2]

        _, vjp = jax.vjp(chunk, q_ref[...], z_ref[...], ba_ref[...], st_ref[0], dtb_ref[...], al_ref[...], nw_ref[...])
        dq, dz, dba, dst, gdtb, gal, gnw = vjp((do_ref[...], dstate[...]))
        dq_ref[...] = dq
        dz_ref[...] = dz.astype(dz_ref.dtype)
        dba_ref[...] = dba.astype(dba_ref.dtype)
        dstate[...] = dst
        gdtb_ref[...] += gdtb
        gal_ref[...] += gal
        gnw_ref[...] += gnw

    body, x_in, x_out, x_shape, x_sems = _hosted(exchange, 9, 6, nc, body)
    rev = lambda i: nc - 1 - i
    vec = pl.BlockSpec((1, LANES), lambda i: (0, 0))
    vec64 = pl.BlockSpec((1, 64), lambda i: (0, 0))
    per_chunk = pl.BlockSpec((1, 256, 64), lambda i: (rev(i), 0, 0))
    return pl.pallas_call(
        body, grid=(nc,), name=name,
        in_specs=[pl.BlockSpec((rows, 768), lambda i: (rev(i), 0)),
                  pl.BlockSpec((rows, 256), lambda i: (rev(i), PC_GZ // 256)),
                  pl.BlockSpec((rows, 128), lambda i: (rev(i), PC_BA // 128)),
                  per_chunk, pl.BlockSpec((1, 4 * rows, rows), lambda i: (rev(i), 0, 0)), vec, vec, vec64,
                  pl.BlockSpec((rows, 256), lambda i: (rev(i), db))] + x_in,
        out_specs=[pl.BlockSpec((rows, 768), lambda i: (rev(i), 0)), pl.BlockSpec((rows, 256), lambda i: (rev(i), 0)),
                   pl.BlockSpec((rows, 128), lambda i: (rev(i), 0)), vec, vec, vec64] + x_out,
        out_shape=[S((t, 768), F32), S((t, 256), BF16), S((t, 128), BF16), S((1, LANES), F32), S((1, LANES), F32),
                   S((1, 64), F32)] + x_shape,
        scratch_shapes=[pltpu.VMEM((256, 64), F32)] + x_sems,
        compiler_params=_params(("arbitrary",)),
    )(qkv, proj, proj, states, inverses, dtb, alog, nw, dcat, *([] if exchange is None else exchange.arrays))


def _pad_cols(w):
    z = jnp.zeros((w.shape[0], 120), w.dtype)
    return jnp.concatenate([w[:, 2056:2824], w[:, 2824:3080], w[:, 1024:2048], w[:, 0:512], w[:, 512:1024],
                            w[:, 2048:2056], z, w[:, 3080:3088], z], axis=1)


def _unpad_cols(g):
    return jnp.concatenate([g[:, PC_ATT:PC_ATT + 512], g[:, PC_SZ:PC_SZ + 512], g[:, PC_XBC:PC_XBC + 1024],
                            g[:, PC_DT:PC_DT + 8], g[:, PC_GQKV:PC_GQKV + 768], g[:, PC_GZ:PC_GZ + 256],
                            g[:, PC_BA:PC_BA + 8]], axis=1)


def _vec128(v, at=0):
    return jnp.zeros((1, LANES), F32).at[0, at:at + v.shape[0]].set(v)


def _in_weight(gathered):
    return _pad_cols(jnp.concatenate([gathered[k].reshape(D_MODEL, -1) for k in range(N_CHIPS)], axis=1))


def _matmul_weights(w_in, gathered):
    rows = lambda name: gathered[name].reshape(-1, gathered[name].shape[-1])
    w_out = rows("w_out")
    return dict(
        w_in=w_in,
        w_out=jnp.concatenate([w_out[256:768], w_out[0:256], w_out[768:1024]], axis=0),
        w_gu=_interleave_gu(rows("ffn_w_gate"), rows("ffn_w_up")),
        w_down=rows("ffn_w_down"))


def _small_operands(w, l):
    return dict(
        pre_mix=w["pre_mix_norm"][l][None], post_mix=w["post_mix_norm"][l][None],
        pre_ffn=w["pre_ffn_norm"][l][None], post_ffn=w["post_ffn_norm"][l][None],
        sinks=_vec128(w["attn_sinks"][l]),
        s_cw=w["ssd_conv_w"][l], s_cb=w["ssd_conv_b"][l][None],
        s_dtb=_vec128(w["ssd_dt_bias"][l]), s_alog=_vec128(w["ssd_A_log"][l]), s_d=_vec128(w["ssd_D"][l]),
        s_nw=w["ssd_norm_w"][l][None],
        g_cw=w["gdn_conv_w"][l], g_cb=jnp.zeros((1, 768), F32),
        g_dtb=_vec128(w["gdn_dt_bias"][l], 4), g_alog=_vec128(w["gdn_A_log"][l], 4), g_nw=w["gdn_norm_w"][l][None],
    )


RAW_GRADS = ("w_in_pad", "w_out_cat", "w_gu", "ffn_w_down")
DW_ROWS = 4096


def _local_step(x, target, lw, first_gather, gathers, matmul_weights, reducer):
    saved, landed = [], {}
    xin = x
    h, first_landed = _prenorm(x, lw[0]["pre_mix"], "prenorm0", first_gather)
    w_in0 = _in_weight(first_landed)
    for l in range(DEPTH):
        p = lw[l]
        carry = (lambda kind: gathers[kind]) if l == 0 else (lambda kind: None)
        proj = _mm_nn(h, w_in0 if l == 0 else p["w_in"], 512, PC_TOT, F32, f"inproj{l}")
        xbc = _conv_fwd(proj, PC_XBC, 1024, p["s_cw"], p["s_cb"], f"ssd_conv{l}")
        gqkv = _conv_fwd(proj, PC_GQKV, 768, p["g_cw"], p["g_cb"], f"gdn_conv{l}")
        att = _swa_fwd(proj, p["sinks"], f"swa{l}")
        ssd, s_states, *landed_s = _ssd_fwd(proj, xbc, p["s_dtb"], p["s_alog"], p["s_d"], p["s_nw"], f"ssd{l}",
                                            carry("ssd"))
        gdn, g_states, g_inv, *landed_g = _gdn_fwd(proj, gqkv, p["g_dtb"], p["g_alog"], p["g_nw"], f"gdn{l}",
                                                   carry("gdn"))
        if l == 0:
            landed.update(ssd=landed_s, gdn=landed_g)
            p.update(matmul_weights(0, landed), w_in=w_in0)
        cat = jnp.concatenate([ssd, att, gdn], axis=1)
        mix = _mm_nn(cat, p["w_out"], 512, 1024, F32, f"outproj{l}")
        x1, h2 = _resid_norm(xin, mix, p["post_mix"], p["pre_ffn"], f"postmix{l}")
        gu, act, *landed_u = _ffn_up(h2, p["w_gu"], f"ffn_gu{l}", exchange=carry("ffn_gu"))
        if l == 0:
            f, *landed_d = _mm_nn(act, p["w_down"], 512, 1024, F32, f"ffn_down{l}", carry("ffn_down"))
            landed.update(ffn_gu=landed_u, ffn_down=landed_d)
            lw[1].update(matmul_weights(1, landed))
        else:
            f = _mm_nn(act, p["w_down"], 512, 1024, F32, f"ffn_down{l}")
        saved.append(dict(xin=xin, h=h, proj=proj, xbc=xbc, gqkv=gqkv, s_states=s_states, g_states=g_states, g_inv=g_inv,
                          cat=cat, mix=mix, x1=x1, h2=h2, gu=gu, act=act, f=f))
        if l + 1 < DEPTH:
            xin, h = _resid_norm(x1, f, p["post_ffn"], lw[l + 1]["pre_mix"], f"postffn{l}")

    g = {k: [None] * DEPTH for k in SMALL + CONV + RAW_GRADS}
    last = saved[-1]
    d_x2, d_f, loss_part, g["post_ffn_norm"][DEPTH - 1] = _resid_loss(
        last["x1"], last["f"], lw[-1]["post_ffn"], target, "loss")
    early, late = [], []
    for l in reversed(range(DEPTH)):
        p, s = lw[l], saved[l]
        d_gu = _ffn_down_bwd(d_f, p["w_down"], s["gu"], f"d_act{l}")
        g["ffn_w_down"][l] = _mm_tn(s["act"], d_f, FF_HALF, 512, DW_ROWS, f"dw_down{l}")
        d_h2 = _mm_nn(d_gu, p["w_gu"], 512, 1024, F32, f"d_h2{l}")
        g["w_gu"][l] = _mm_tn(d_gu, s["h2"], FF_HALF, 512, DW_ROWS, f"dw_gu{l}")
        d_x1, d_mix, g["pre_ffn_norm"][l], g["post_mix_norm"][l] = _resid_norm_bwd(
            s["x1"], s["mix"], d_x2, d_h2, p["post_mix"], p["pre_ffn"], f"d_postmix{l}")
        d_cat = _mm_nt(d_mix, p["w_out"], 512, 1024, F32, f"d_cat{l}")
        g["w_out_cat"][l] = _mm_tn(s["cat"], d_mix, 512, 1024, DW_ROWS, f"dw_out{l}")
        d_q, d_k, d_v, g_sinks = _swa_bwd(s["proj"], p["sinks"], d_cat, 512, f"d_swa{l}")
        d_sz, d_xbc, d_dt, g_dtb, g_alog, g_d, g["ssd_norm_w"][l], *siblings = _ssd_bwd(
            s["proj"], s["xbc"], s["s_states"], p["s_dtb"], p["s_alog"], p["s_d"], p["s_nw"], d_cat, 0, f"d_ssd{l}",
            reducer.exchange(g) if l == 0 else None)
        d_gq, d_gz, d_ba, gg_dtb, gg_alog, g["gdn_norm_w"][l], *landed_b = _gdn_bwd(
            s["proj"], s["gqkv"], s["g_states"], s["g_inv"], p["g_dtb"], p["g_alog"], p["g_nw"], d_cat, 768, f"d_gdn{l}",
            reducer.scatter(siblings) if l == 0 else None)
        if l == 0:
            early = landed_b
        d_xbc_raw, g["ssd_conv_w"][l], g["ssd_conv_b"][l] = _conv_bwd(
            s["proj"], PC_XBC, 1024, p["s_cw"], p["s_cb"], d_xbc, f"d_ssd_conv{l}")
        d_gq_raw, g["gdn_conv_w"][l], _ = _conv_bwd(s["proj"], PC_GQKV, 768, p["g_cw"], p["g_cb"], d_gq, f"d_gdn_conv{l}")
        d_proj = jnp.concatenate([d_gq_raw, d_gz, d_xbc_raw, d_q, d_k, d_v, d_sz, d_dt, d_ba], axis=1)
        g["w_in_pad"][l] = _mm_tn(s["h"], d_proj, 512, PC_TOT // 2, DW_ROWS, f"dw_in{l}")
        if l == 0:
            d_h, *late = _mm_nt(d_proj, p["w_in"], 1024, 1024, F32, f"d_h{l}", reducer.late(g))
        else:
            d_h = _mm_nt(d_proj, p["w_in"], 1024, 1024, F32, f"d_h{l}")
        g["attn_sinks"][l] = g_sinks[0, :4]
        g["ssd_dt_bias"][l], g["ssd_A_log"][l], g["ssd_D"][l] = g_dtb[0, :8], g_alog[0, :8], g_d[0, :8]
        g["gdn_dt_bias"][l], g["gdn_A_log"][l] = gg_dtb[0, 4:8], gg_alog[0, 4:8]
        if l > 0:
            sp = saved[l - 1]
            d_x2, d_f, g["pre_mix_norm"][l], g["post_ffn_norm"][l - 1] = _resid_norm_bwd(
                s["xin"], sp["f"], d_x1, d_h, lw[l - 1]["post_ffn"], p["pre_mix"], f"d_postffn{l - 1}")
        else:
            grad_x, g["pre_mix_norm"][0] = _prenorm_bwd(s["xin"], d_x1, d_h, p["pre_mix"], "d_prenorm0")

    small = {k: jnp.stack([a.reshape(-1) for a in g[k]], axis=0) for k in SMALL + CONV}
    return loss_part, grad_x, small, {k: g[k] for k in RAW_GRADS}, early, late


BIG = ("w_in", "w_out", "ffn_w_gate", "ffn_w_up", "ffn_w_down")
CONV = ("ssd_conv_w", "gdn_conv_w")
TRANSPOSED = ("ffn_w_gate", "ffn_w_up")
SMALL = ("pre_mix_norm", "post_mix_norm", "pre_ffn_norm", "post_ffn_norm", "attn_sinks", "ssd_conv_b", "ssd_dt_bias",
         "ssd_A_log", "ssd_D", "ssd_norm_w", "gdn_dt_bias", "gdn_A_log", "gdn_norm_w")


def _row_tile(rows, cap):
    best = rows
    for t in range(8, min(cap, rows) + 1, 8):
        if rows % t == 0:
            best = t
    return best


SMALL_UNIT = 8 * LANES


def _pack_small(vals):
    rows = []
    for a in vals:
        f = a.reshape(-1)
        pad = -f.shape[0] % SMALL_UNIT
        rows.append(jnp.concatenate([f, jnp.zeros((pad,), F32)]).reshape(-1, LANES))
    return jnp.concatenate(rows, axis=0)


def _unpack_small(mat, shapes):
    out, r = [], 0
    for shp in shapes:
        n = math.prod(shp)
        nr = -(-n // SMALL_UNIT) * 8
        out.append(mat[r:r + nr].reshape(-1)[:n].reshape(shp))
        r += nr
    return out


def _place():
    x, y, c = lax.axis_index("x"), lax.axis_index("y"), lax.axis_index("c")
    chips = [(1 - x, y), (x, 1 - y), (1 - x, 1 - y)]
    return x, y, c, chips


ANY = pl.BlockSpec(memory_space=pl.ANY)


def _remote(src, dst, sems, k, to):
    send_sems, recv_sems = sems
    return pltpu.make_async_remote_copy(src_ref=src, dst_ref=dst, send_sem=send_sems.at[k], recv_sem=recv_sems.at[k],
                                        device_id=to, device_id_type=MESH)


def _sem_pairs(n):
    return [pltpu.SemaphoreType.DMA((n,)), pltpu.SemaphoreType.DMA((n,))]


def _run_exchange(exchange, name):
    k = len(exchange.arrays)

    def body(*refs):
        ins, outs, sems = refs[:k], refs[k:-2], refs[-2:]
        exchange.start(ins, outs, sems)
        exchange.finish(ins, outs, sems)

    return pl.pallas_call(
        body, name=name, in_specs=[ANY] * k, out_specs=[ANY] * len(exchange.out_shape),
        out_shape=list(exchange.out_shape), scratch_shapes=_sem_pairs(exchange.n_sems),
    )(*exchange.arrays)


def _gather_exchange(shards):
    n = len(shards)

    def sends(s_refs, g_refs, sems):
        x, y, c, chips = _place()
        me = 2 * x + y
        over_ici = [_remote(s_refs[i].at[c], g_refs[i].at[me, c], sems, 7 * i + j, (px, py, c))
                    for i in range(n) for j, (px, py) in enumerate(chips)]
        return over_ici + [_remote(s_refs[i], g_refs[i].at[me], sems, 7 * i + 6, (x, y, 1 - c)) for i in range(n)]

    def start(s_refs, g_refs, sems):
        for cp in sends(s_refs, g_refs, sems):
            cp.start()

    def finish(s_refs, g_refs, sems):
        x, y, c, chips = _place()
        sib = (x, y, 1 - c)
        passed = []
        for j, (px, py) in enumerate(chips):
            for i in range(n):
                landed = g_refs[i].at[2 * px + py, c]
                _remote(landed, landed, sems, 7 * i + j, (px, py, c)).wait_recv()
                fw = _remote(landed, landed, sems, 7 * i + 3 + j, sib)
                fw.start()
                passed.append(fw)
        for j, (px, py) in enumerate(chips):
            for i in range(n):
                landed = g_refs[i].at[2 * px + py, 1 - c]
                _remote(landed, landed, sems, 7 * i + 3 + j, sib).wait_recv()
        for i in range(n):
            mine = g_refs[i].at[2 * x + y]
            _remote(mine, mine, sems, 7 * i + 6, sib).wait_recv()
        for cp in sends(s_refs, g_refs, sems) + passed:
            cp.wait_send()

    return _Exchange(shards, [S((N_CHIPS,) + a.shape, a.dtype) for a in shards], 7 * n, start, finish)


def _halves_exchange(pieces):
    n = len(pieces)
    ds = [a for a, _ in pieces]

    def copies(d_refs, t_refs, sems):
        x, y, c, _ = _place()
        return [_remote(d_refs[i].at[:, pieces[i][1], :, 1 - c], t_refs[i], sems, i, (x, y, 1 - c)) for i in range(n)]

    def start(d_refs, t_refs, sems):
        for cp in copies(d_refs, t_refs, sems):
            cp.start()

    def finish(d_refs, t_refs, sems):
        for cp in copies(d_refs, t_refs, sems):
            cp.wait()

    return _Exchange(ds, [S((2, 2) + a.shape[4:], a.dtype) for a in ds], n, start, finish)


def _scatter_exchange(ps):
    n = len(ps)

    def copies(p_refs, u_refs, sems):
        x, y, c, chips = _place()
        return [_remote(p_refs[i].at[2 * px + py], u_refs[i].at[j], sems, 3 * i + j, (px, py, c))
                for j, (px, py) in enumerate(chips) for i in range(n)]

    def start(p_refs, u_refs, sems):
        for cp in copies(p_refs, u_refs, sems):
            cp.start()

    def finish(p_refs, u_refs, sems):
        for cp in copies(p_refs, u_refs, sems):
            cp.wait()

    return _Exchange(ps, [S((3,) + a.shape[1:], a.dtype) for a in ps], 3 * n, start, finish)


def _join_halves(qs, name):
    n = len(qs)

    def body(*refs):
        o_refs, sems = refs[n:2 * n], refs[2 * n:]
        x, y, c, _ = _place()
        cps = [_remote(o_refs[i].at[:, c], o_refs[i].at[:, c], sems, i, (x, y, 1 - c)) for i in range(n)]
        for cp in cps:
            cp.start()
        for i in range(n):
            other = o_refs[i].at[:, 1 - c]
            _remote(other, other, sems, i, (x, y, 1 - c)).wait_recv()
        for cp in cps:
            cp.wait_send()

    return pl.pallas_call(
        body, name=name, in_specs=[ANY] * n, out_specs=[ANY] * n,
        out_shape=[S(a.shape, a.dtype) for a in qs], input_output_aliases={i: i for i in range(n)},
        scratch_shapes=_sem_pairs(n),
    )(*qs)


def _gather_small(v, name):
    def body(v_ref, o_ref, send_sems, recv_sems, local_sem):
        x, y, c, _ = _place()
        me = 4 * x + 2 * y + c
        mine = pltpu.make_async_copy(v_ref, o_ref.at[me], local_sem)
        mine.start()
        cps = []
        for k in range(1, N_DEV):
            fx, fy, fc = (k >> 2) & 1, (k >> 1) & 1, k & 1
            peer = (x ^ fx, y ^ fy, c ^ fc)
            cps.append(pltpu.make_async_remote_copy(
                src_ref=v_ref, dst_ref=o_ref.at[me], send_sem=send_sems.at[k - 1], recv_sem=recv_sems.at[k - 1],
                device_id=peer, device_id_type=MESH))
        for cp in cps:
            cp.start()
        for k in range(1, N_DEV):
            fx, fy, fc = (k >> 2) & 1, (k >> 1) & 1, k & 1
            dst = o_ref.at[4 * (x ^ fx) + 2 * (y ^ fy) + (c ^ fc)]
            pltpu.make_async_remote_copy(src_ref=dst, dst_ref=dst, send_sem=send_sems.at[k - 1],
                                         recv_sem=recv_sems.at[k - 1], device_id=(x, y, c),
                                         device_id_type=MESH).wait_recv()
        for cp in cps:
            cp.wait_send()
        mine.wait()

    return pl.pallas_call(
        body, name=name, in_specs=[ANY], out_specs=ANY, out_shape=S((N_DEV,) + v.shape, F32),
        scratch_shapes=[pltpu.SemaphoreType.DMA((N_DEV - 1,)), pltpu.SemaphoreType.DMA((N_DEV - 1,)),
                        pltpu.SemaphoreType.DMA],
    )(v)


def _sum_leading(a, name):
    n, rows, cols = a.shape
    tm = _row_tile(rows, 640)

    def body(a_ref, o_ref):
        acc = a_ref[0]
        for k in range(1, n):
            acc = acc + a_ref[k]
        o_ref[...] = acc

    return pl.pallas_call(
        body, grid=(rows // tm,), name=name, in_specs=[pl.BlockSpec((n, tm, cols), lambda i: (0, i, 0))],
        out_specs=pl.BlockSpec((tm, cols), lambda i: (i, 0)), out_shape=S((rows, cols), F32),
        compiler_params=_params(("arbitrary",)),
    )(a)


def _add_sibling(place, piece, b, name):
    a, pick = piece
    n, hr, cols = b.shape
    tm = _row_tile(hr, 512)

    def body(place_ref, a_ref, b_ref, o_ref, o16_ref):
        tot = a_ref[0, 0, 0] + b_ref[...]
        o16_ref[...] = tot.astype(BF16)

        @pl.when(pl.program_id(1) == place_ref[1])
        def _():
            o_ref[...] = tot[0]

    spec = pl.BlockSpec((1, tm, cols), lambda i, k, pr: (k, i, 0))
    return pl.pallas_call(
        body, name=name, out_shape=[S((hr, cols), F32), S((n, hr, cols), BF16)],
        grid_spec=pltpu.PrefetchScalarGridSpec(
            num_scalar_prefetch=1, grid=(hr // tm, n),
            in_specs=[pl.BlockSpec((1, 1, 1, 1, tm, cols), lambda i, k, pr: (k // 2, pick, k % 2, pr[0], i, 0)), spec],
            out_specs=[pl.BlockSpec((tm, cols), lambda i, k, pr: (i, 0)), spec]),
        compiler_params=_params(("arbitrary", "arbitrary")),
    )(place, a, b)


def _add_chips(place, sums, others, layer, into, name):
    hr, cols = sums.shape
    tm = _row_tile(hr, 512)

    def body(place_ref, m_ref, o_ref, *rest):
        acc = m_ref[...]
        for k in range(others.shape[0]):
            acc = acc + o_ref[k].astype(F32)
        rest[-1][0, 0] = acc

    kept = [] if into is None else [into]
    return pl.pallas_call(
        body, name=name, out_shape=S((DEPTH, 2, hr, cols), F32),
        grid_spec=pltpu.PrefetchScalarGridSpec(
            num_scalar_prefetch=1, grid=(hr // tm,),
            in_specs=[pl.BlockSpec((tm, cols), lambda i, pr: (i, 0)),
                      pl.BlockSpec((others.shape[0], tm, cols), lambda i, pr: (0, i, 0))] + [ANY] * len(kept),
            out_specs=pl.BlockSpec((1, 1, tm, cols), lambda i, pr: (layer, pr[0], i, 0))),
        input_output_aliases={3: 0} if kept else {},
        compiler_params=_params(("arbitrary",)),
    )(place, sums, others, *kept)


def _adamw(wt, g, m, v, name):
    shape = wt.shape
    cols = shape[-1]
    rows = math.prod(shape[:-1])
    tm = rows
    for cand in (512, 256, 128, 64, 32, 16, 8):
        if rows % cand == 0:
            tm = cand
            break
    c1 = 1.0 - ADAM_B1 ** ADAM_STEP
    c2 = 1.0 - ADAM_B2 ** ADAM_STEP

    def body(w_ref, g_ref, m_ref, v_ref, d_ref, nm_ref, nv_ref):
        gv = g_ref[...]
        nm = ADAM_B1 * m_ref[...] + (1.0 - ADAM_B1) * gv
        nv = ADAM_B2 * v_ref[...] + (1.0 - ADAM_B2) * (gv * gv)
        d_ref[...] = -ADAM_LR * ((nm / c1) / (jnp.sqrt(nv / c2) + ADAM_EPS) + ADAM_WD * w_ref[...])
        nm_ref[...] = nm
        nv_ref[...] = nv

    spec = pl.BlockSpec((tm, cols), lambda i: (i, 0))
    outs = pl.pallas_call(
        body, grid=(rows // tm,), name=name, in_specs=[spec] * 4, out_specs=[spec] * 3,
        out_shape=[S((rows, cols), F32)] * 3, compiler_params=_params(("arbitrary",)),
    )(*[a.reshape(rows, cols) for a in (wt, g, m, v)])
    return [o.reshape(shape) for o in outs]


WEIGHTS = ('pre_mix_norm', 'post_mix_norm', 'pre_ffn_norm', 'post_ffn_norm', 'w_in', 'w_out', 'attn_sinks', 'ssd_conv_w',
           'ssd_conv_b', 'ssd_dt_bias', 'ssd_A_log', 'ssd_D', 'ssd_norm_w', 'gdn_conv_w', 'gdn_dt_bias', 'gdn_A_log',
           'gdn_norm_w', 'ffn_w_gate', 'ffn_w_up', 'ffn_w_down')


def _chip_piece(i, raw, shape):
    half = (2, shape[1] // 2, shape[2])
    if i in (2, 3):
        return raw["w_gu"].reshape((2, 2, 2) + half), i - 2
    if i == 0:
        g, width = _unpad_cols(raw["w_in_pad"]), shape[2]
        g = jnp.stack([g[:, k * width:(k + 1) * width] for k in range(N_CHIPS)])
    elif i == 1:
        g = raw["w_out_cat"]
        g = jnp.concatenate([g[512:768], g[0:512], g[768:1024]], axis=0)
    else:
        g = raw["ffn_w_down"]
    return g.reshape((2, 1, 2) + half), 0


def _step(x, target, wts, ms, vs):
    chip = 2 * lax.axis_index("x") + lax.axis_index("y")
    place = jnp.stack([lax.axis_index("c"), chip]).astype(jnp.int32)
    big_names = list(BIG)
    flip = lambda k, a: jnp.swapaxes(a, 1, 2) if k in TRANSPOSED else a
    wts, ms, vs = ({k: flip(k, a) for k, a in d.items()} for d in (wts, ms, vs))
    big_shapes = [wts[k].shape for k in big_names]
    halves = lambda a: a.reshape((2, a.shape[0] // 2) + a.shape[1:])

    shard = lambda name, l: halves(wts[name][l].astype(BF16))
    first_gather = _gather_exchange([shard("w_in", 0)])
    carried_by = {"ssd": [("w_out", 0), ("ffn_w_gate", 0)],
                  "gdn": [("ffn_w_up", 0), ("ffn_w_down", 0), ("w_in", 1), ("w_out", 1)],
                  "ffn_gu": [("ffn_w_gate", 1), ("ffn_w_up", 1)],
                  "ffn_down": [("ffn_w_down", 1)]}
    gathers = {kind: _gather_exchange([shard(*key) for key in keys]) for kind, keys in carried_by.items()}

    def matmul_weights(l, landed):
        gathered = {}
        for kind, keys in carried_by.items():
            for key, got in zip(keys, landed.get(kind, [])):
                if key[1] == l:
                    gathered[key[0]] = got.reshape((N_CHIPS,) + wts[key[0]].shape[1:])
        return _matmul_weights(None if l == 0 else _in_weight(gathered["w_in"]), gathered)

    conv = _gather_small(_pack_small([wts[k] for k in CONV]), "gather_conv_weights")
    conv = [_unpack_small(conv[2 * k], [wts[n].shape for n in CONV]) for k in range(N_CHIPS)]
    w_all = dict(wts)
    for i, n in enumerate(CONV):
        w_all[n] = jnp.concatenate([conv[k][i] for k in range(N_CHIPS)], axis=2)

    early_keys = [(DEPTH - 1, 0)] + [(l, i) for l in reversed(range(DEPTH)) for i in range(1, len(BIG))]
    late_keys = [(l, 0) for l in range(DEPTH - 1)]

    def pieces_of(keys, g):
        return [_chip_piece(i, {k: g[k][l] for k in RAW_GRADS}, big_shapes[i]) for l, i in keys]

    def add_siblings(tag, pieces, siblings):
        return [_add_sibling(place, p, t.reshape((N_CHIPS,) + t.shape[2:]), f"add_sibling_{tag}{n}")
                for n, (p, t) in enumerate(zip(pieces, siblings))]

    class Reducer:
        pieces, sums, late_sums = [], [], []

        def exchange(self, g):
            self.pieces = pieces_of(early_keys, g)
            return _halves_exchange(self.pieces)

        def scatter(self, siblings):
            self.sums = add_siblings("early", self.pieces, siblings)
            return _scatter_exchange([s16 for _, s16 in self.sums])

        def late(self, g):
            pieces = pieces_of(late_keys, g)
            siblings = _run_exchange(_halves_exchange(pieces), "exchange_halves_late")
            self.late_sums = add_siblings("late", pieces, siblings)
            return _scatter_exchange([s16 for _, s16 in self.late_sums])

    reducer = Reducer()

    loss_part, grad_x, small_g, raw, early, late = _local_step(
        x[0], target[0], [_small_operands(w_all, l) for l in range(DEPTH)], first_gather, gathers, matmul_weights, reducer)

    reduced = {}
    for tag, keys, sums, others in (("early", early_keys, reducer.sums, early), ("late", late_keys, reducer.late_sums, late)):
        for n, ((l, i), (s32, _), o) in enumerate(zip(keys, sums, others)):
            reduced[i] = _add_chips(place, s32, o, l, reduced.get(i), f"add_chips_{tag}{n}")
    joined = _join_halves([reduced[i] for i in range(len(BIG))], "join_halves")
    g_all = {k: q.reshape(shp) for k, q, shp in zip(big_names, joined, big_shapes)}

    names = SMALL + CONV
    packed = _pack_small([small_g[k] for k in names] + [loss_part])
    small_sum = _sum_leading(_gather_small(packed, "gather_small_grads"), "add_small")
    vals = _unpack_small(small_sum, [small_g[k].shape for k in names] + [(1, LANES)])
    loss = vals[-1][0, 0]
    for k, v in zip(names, vals[:-1]):
        if k in CONV:
            width = wts[k].shape[2]
            v = lax.dynamic_slice_in_dim(v.reshape(DEPTH, 4, -1), chip * width, width, axis=2)
        g_all[k] = v.reshape(wts[k].shape)

    shapes = [wts[k].shape for k in names]
    d_s, m_s, v_s = _adamw(_pack_small([wts[k] for k in names]), _pack_small([g_all[k] for k in names]),
                           _pack_small([ms[k] for k in names]), _pack_small([vs[k] for k in names]), "adamw_small")
    upd = dict(zip(names, zip(_unpack_small(d_s, shapes), _unpack_small(m_s, shapes), _unpack_small(v_s, shapes))))
    for k in big_names:
        upd[k] = _adamw(wts[k], g_all[k], ms[k], vs[k], f"adamw_{k}")
    return (loss, grad_x[None], *[flip(k, g_all[k]) for k in WEIGHTS], *[flip(k, upd[k][0]) for k in WEIGHTS],
            *[flip(k, upd[k][1]) for k in WEIGHTS], *[flip(k, upd[k][2]) for k in WEIGHTS])


def kernel(x, pre_mix_norm, post_mix_norm, pre_ffn_norm, post_ffn_norm, w_in, w_out, attn_sinks, ssd_conv_w, ssd_conv_b, ssd_dt_bias, ssd_A_log, ssd_D, ssd_norm_w, gdn_conv_w, gdn_dt_bias, gdn_A_log, gdn_norm_w, ffn_w_gate, ffn_w_up, ffn_w_down, loss_target, m_pre_mix_norm, m_post_mix_norm, m_pre_ffn_norm, m_post_ffn_norm, m_w_in, m_w_out, m_attn_sinks, m_ssd_conv_w, m_ssd_conv_b, m_ssd_dt_bias, m_ssd_A_log, m_ssd_D, m_ssd_norm_w, m_gdn_conv_w, m_gdn_dt_bias, m_gdn_A_log, m_gdn_norm_w, m_ffn_w_gate, m_ffn_w_up, m_ffn_w_down, v_pre_mix_norm, v_post_mix_norm, v_pre_ffn_norm, v_post_ffn_norm, v_w_in, v_w_out, v_attn_sinks, v_ssd_conv_w, v_ssd_conv_b, v_ssd_dt_bias, v_ssd_A_log, v_ssd_D, v_ssd_norm_w, v_gdn_conv_w, v_gdn_dt_bias, v_gdn_A_log, v_gdn_norm_w, v_ffn_w_gate, v_ffn_w_up, v_ffn_w_down):
    wts = dict(zip(WEIGHTS, (pre_mix_norm, post_mix_norm, pre_ffn_norm, post_ffn_norm, w_in, w_out, attn_sinks, ssd_conv_w, ssd_conv_b, ssd_dt_bias, ssd_A_log, ssd_D, ssd_norm_w, gdn_conv_w, gdn_dt_bias, gdn_A_log, gdn_norm_w, ffn_w_gate, ffn_w_up, ffn_w_down)))
    ms = dict(zip(WEIGHTS, (m_pre_mix_norm, m_post_mix_norm, m_pre_ffn_norm, m_post_ffn_norm, m_w_in, m_w_out, m_attn_sinks, m_ssd_conv_w, m_ssd_conv_b, m_ssd_dt_bias, m_ssd_A_log, m_ssd_D, m_ssd_norm_w, m_gdn_conv_w, m_gdn_dt_bias, m_gdn_A_log, m_gdn_norm_w, m_ffn_w_gate, m_ffn_w_up, m_ffn_w_down)))
    vs = dict(zip(WEIGHTS, (v_pre_mix_norm, v_post_mix_norm, v_pre_ffn_norm, v_post_ffn_norm, v_w_in, v_w_out, v_attn_sinks, v_ssd_conv_w, v_ssd_conv_b, v_ssd_dt_bias, v_ssd_A_log, v_ssd_D, v_ssd_norm_w, v_gdn_conv_w, v_gdn_dt_bias, v_gdn_A_log, v_gdn_norm_w, v_ffn_w_gate, v_ffn_w_up, v_ffn_w_down)))
    return _step(x, loss_target, wts, ms, vs)
```
